```python
import jax, jax.numpy as jnp
from jax import lax
import numpy as np

D_MODEL = 1024
BATCH = 8
SEQ = 2048
DEPTH = 4

EPS = 1e-6
F_FLOOR = 1e-30
GDN_HEADS = 4
GDN_HEAD_DIM = 128
GDN_WIDTH = GDN_HEADS * GDN_HEAD_DIM
GDN_CONV = 4
GDN_CHUNK = 64
HGRN_HEADS = 4
HGRN_STATE = 128
HGRN_HEAD_DIM = 128
HGRN_KEY_WIDTH = HGRN_HEADS * HGRN_STATE
HGRN_VAL_WIDTH = HGRN_HEADS * HGRN_HEAD_DIM
HGRN_CHUNK = 16
MIX_WIDTH = GDN_WIDTH + HGRN_VAL_WIDTH
AB_SIZES = (3 * GDN_WIDTH, GDN_WIDTH, GDN_HEADS, GDN_HEADS,
            HGRN_KEY_WIDTH, HGRN_KEY_WIDTH, HGRN_VAL_WIDTH, HGRN_VAL_WIDTH)
AB_COLS = sum(AB_SIZES)
LRU_WIDTH = D_MODEL
LRU_HEADS = 4
LRU_BLOCK = LRU_WIDTH // LRU_HEADS
LRU_CONV = 4
RG_C = 8.0
D_FF = 2816
FFN_CONV = 3
N_EVEN = (DEPTH + 1) // 2
N_ODD = DEPTH // 2

kernel_name = 'hybrid_gdn_hgrn2_rglru_convffn'


def _rmsnorm(x, gain):
    x32 = x.astype(jnp.float32)
    y = x32 * lax.rsqrt(jnp.mean(x32 * x32, axis=-1, keepdims=True) + EPS)
    return (y * gain.astype(jnp.float32)).astype(x.dtype)


def _l2norm(x):
    return x * lax.rsqrt(jnp.sum(x * x, axis=-1, keepdims=True) + EPS)


def _causal_dwconv(x, w):
    width, ch = w.shape
    return lax.conv_general_dilated(
        x, w[:, None, :].astype(x.dtype), window_strides=(1,), padding=[(width - 1, 0)],
        dimension_numbers=('NWC', 'WIO', 'NWC'), feature_group_count=ch)


def _heads(t, nh):
    b, s, _ = t.shape
    return t.reshape(b, s, nh, -1).transpose(0, 2, 1, 3)


def _masked_exp(logits, mask):
    return jnp.where(mask, jnp.exp(jnp.where(mask, logits, 0.0)), 0.0)


def _gated_delta_rule(q, k, v, g, beta):
    bsz, nh, seq, dk = q.shape
    dv = v.shape[-1]
    c = GDN_CHUNK
    n = seq // c
    q, k = (t.reshape(bsz, nh, n, c, dk) for t in (q, k))
    v = v.reshape(bsz, nh, n, c, dv)
    g, beta = (t.reshape(bsz, nh, n, c) for t in (g, beta))
    gc = jnp.cumsum(g, axis=-1)
    causal = jnp.tril(jnp.ones((c, c), dtype=bool))
    decay = _masked_exp(gc[..., :, None] - gc[..., None, :], causal)
    k_beta = k * beta[..., None]
    lower = jnp.tril(jnp.einsum('bhnik,bhnjk->bhnij', k_beta, k) * decay, -1)
    rhs = jnp.concatenate([v * beta[..., None], k_beta * jnp.exp(gc)[..., None]], axis=-1)
    sol = lax.linalg.triangular_solve(lower + jnp.eye(c, dtype=q.dtype), rhs, left_side=True,
                                      lower=True, unit_diagonal=True)
    u, w = sol[..., :dv], sol[..., dv:]
    attn = jnp.einsum('bhnik,bhnjk->bhnij', q, k) * decay
    q_dec = q * jnp.exp(gc)[..., None]
    k_dec = k * jnp.exp(gc[..., -1:] - gc)[..., None]
    chunk_decay = jnp.exp(gc[..., -1])

    def step(state, xs):
        u_c, w_c, attn_c, q_c, k_c, d_c = xs
        v_new = u_c - jnp.einsum('bhck,bhkv->bhcv', w_c, state)
        o_c = (jnp.einsum('bhck,bhkv->bhcv', q_c, state)
               + jnp.einsum('bhij,bhjv->bhiv', attn_c, v_new))
        state = state * d_c[..., None, None] + jnp.einsum('bhck,bhcv->bhkv', k_c, v_new)
        return state, o_c

    xs = tuple(jnp.moveaxis(t, 2, 0) for t in (u, w, attn, q_dec, k_dec, chunk_decay))
    _, o = lax.scan(step, jnp.zeros((bsz, nh, dk, dv), q.dtype), xs)
    return jnp.moveaxis(o, 0, 2).reshape(bsz, nh, seq, dv)


def _chunk_gla(q, k, v, log_f):
    bsz, nh, seq, dk = q.shape
    dv = v.shape[-1]
    c = HGRN_CHUNK
    n = seq // c
    q, k, log_f = (t.reshape(bsz, nh, n, c, dk) for t in (q, k, log_f))
    v = v.reshape(bsz, nh, n, c, dv)
    b = jnp.cumsum(log_f, axis=3)
    causal = jnp.tril(jnp.ones((c, c), dtype=bool))[:, :, None]
    rel = _masked_exp(b[..., :, None, :] - b[..., None, :, :], causal)
    scores = jnp.sum(q[..., :, None, :] * k[..., None, :, :] * rel, axis=-1)
    o_intra = jnp.einsum('bhnij,bhnjv->bhniv', scores, v)
    b_last = b[..., -1:, :]
    q_dec = q * jnp.exp(b)
    k_dec = k * jnp.exp(b_last - b)
    chunk_decay = jnp.exp(b_last[..., 0, :])

    def step(state, xs):
        q_c, k_c, v_c, d_c = xs
        o_c = jnp.einsum('bhck,bhkv->bhcv', q_c, state)
        state = state * d_c[..., None] + jnp.einsum('bhck,bhcv->bhkv', k_c, v_c)
        return state, o_c

    xs = tuple(jnp.moveaxis(t, 2, 0) for t in (q_dec, k_dec, v, chunk_decay))
    _, o_inter = lax.scan(step, jnp.zeros((bsz, nh, dk, dv), q.dtype), xs)
    return (o_intra + jnp.moveaxis(o_inter, 0, 2)).reshape(bsz, nh, seq, dv)


def _even_mixer(h, w_in, conv_w, a_log, dt_bias, gdn_gain, lower_bound, hgrn_gain, w_out):
    f32 = jnp.float32
    bsz, seq, _ = h.shape
    p = h @ w_in
    offsets = [int(o) for o in np.cumsum(AB_SIZES)[:-1]]
    qkv_a, z_a, beta_a, alpha_a, q_b, f_b, i_b, g_b = jnp.split(p, offsets, axis=-1)
    qkv = jax.nn.silu(_causal_dwconv(qkv_a, conv_w))
    q, k, v = (_heads(t, GDN_HEADS).astype(f32) for t in jnp.split(qkv, 3, axis=-1))
    q = _l2norm(q) * (GDN_HEAD_DIM ** -0.5)
    k = _l2norm(k)
    beta = jax.nn.sigmoid(beta_a.astype(f32)).transpose(0, 2, 1)
    g = (-jnp.exp(a_log.astype(f32))
         * jax.nn.softplus(alpha_a.astype(f32) + dt_bias.astype(f32))).transpose(0, 2, 1)
    o_a = _gated_delta_rule(q, k, v, g, beta).transpose(0, 2, 1, 3)
    z = jax.nn.silu(z_a.astype(f32)).reshape(bsz, seq, GDN_HEADS, GDN_HEAD_DIM)
    o_a = _rmsnorm(o_a, gdn_gain) * z
    f = lower_bound + (1.0 - lower_bound) * jax.nn.sigmoid(f_b.astype(f32))
    log_f = jnp.log(jnp.maximum(f, F_FLOOR))
    k_b = 1.0 - f
    q_b = jax.nn.silu(q_b.astype(f32))
    o_b = _chunk_gla(_heads(q_b, HGRN_HEADS), _heads(k_b, HGRN_HEADS),
                     _heads(i_b.astype(f32), HGRN_HEADS), _heads(log_f, HGRN_HEADS))
    o_b = o_b.transpose(0, 2, 1, 3)
    gate_b = jax.nn.silu(g_b.astype(f32)).reshape(bsz, seq, HGRN_HEADS, HGRN_HEAD_DIM)
    o_b = _rmsnorm(o_b, hgrn_gain) * gate_b
    o = jnp.concatenate([o_a.reshape(bsz, seq, GDN_WIDTH),
                         o_b.reshape(bsz, seq, HGRN_VAL_WIDTH)], axis=-1).astype(h.dtype)
    return o @ w_out


def _rglru_block(h, w_in, conv_w, conv_b, gate_a_w, gate_a_b, gate_x_w, gate_x_b, lam, w_out):
    f32 = jnp.float32
    bsz, seq, _ = h.shape
    y_branch, x_branch = jnp.split(h @ w_in, 2, axis=-1)
    gate = jax.nn.gelu(y_branch.astype(f32), approximate=True)
    xc = (_causal_dwconv(x_branch, conv_w) + conv_b).astype(f32)
    xb = xc.reshape(bsz, seq, LRU_HEADS, LRU_BLOCK)
    r = jax.nn.sigmoid(jnp.einsum('bthi,hij->bthj', xb, gate_a_w.astype(f32)).reshape(bsz, seq, LRU_WIDTH)
                       + gate_a_b.astype(f32))
    i = jax.nn.sigmoid(jnp.einsum('bthi,hij->bthj', xb, gate_x_w.astype(f32)).reshape(bsz, seq, LRU_WIDTH)
                       + gate_x_b.astype(f32))
    log_a = -RG_C * r * jax.nn.softplus(-lam.astype(f32))
    a = jnp.exp(log_a)
    u = jnp.sqrt(jnp.maximum(-jnp.expm1(2.0 * log_a), 0.0)) * (i * xc)

    def combine(left, right):
        a_l, b_l = left
        a_r, b_r = right
        return a_l * a_r, a_r * b_l + b_r

    _, hs = lax.associative_scan(combine, (a, u), axis=1)
    return (hs * gate).astype(h.dtype) @ w_out


def _conv_ffn(h, w_up, conv_w, conv_b, w_down):
    gate, val = jnp.split(h @ w_up, 2, axis=-1)
    gate = _causal_dwconv(gate, conv_w) + conv_b
    return (jax.nn.silu(gate) * val) @ w_down


def _fwd_setup_inputs(seed: int = 0) -> dict:
    key = jax.random.key(seed)
    ks = iter(jax.random.split(key, 40))
    f32 = jnp.float32

    def nrm(shape, scale):
        return jax.random.normal(next(ks), shape, f32) * scale

    def uni(shape, lo, hi):
        return jax.random.uniform(next(ks), shape, f32, lo, hi)

    x = nrm((BATCH, SEQ, D_MODEL), 1.0)
    norm_mix = 1.0 + nrm((DEPTH, D_MODEL), 0.02)
    norm_ffn = 1.0 + nrm((DEPTH, D_MODEL), 0.02)
    norm_final = 1.0 + nrm((D_MODEL,), 0.02)
    ab_w_in = nrm((N_EVEN, D_MODEL, AB_COLS), D_MODEL ** -0.5)
    gdn_conv_w = nrm((N_EVEN, GDN_CONV, 3 * GDN_WIDTH), GDN_CONV ** -0.5)
    gdn_a_log = jnp.log(uni((N_EVEN, GDN_HEADS), 1.0, 16.0))
    dt = jnp.exp(uni((N_EVEN, GDN_HEADS), float(np.log(1e-3)), float(np.log(1e-1))))
    gdn_dt_bias = dt + jnp.log(-jnp.expm1(-dt))
    gdn_norm = 1.0 + nrm((N_EVEN, GDN_HEAD_DIM), 0.02)
    hgrn_lower_bounds = 1.0 + nrm((N_EVEN, HGRN_KEY_WIDTH), 0.1)
    hgrn_norm = 1.0 + nrm((N_EVEN, HGRN_HEAD_DIM), 0.02)
    ab_w_out = nrm((N_EVEN, MIX_WIDTH, D_MODEL), MIX_WIDTH ** -0.5)
    c_w_in = nrm((N_ODD, D_MODEL, 2 * LRU_WIDTH), D_MODEL ** -0.5)
    c_conv_w = nrm((N_ODD, LRU_CONV, LRU_WIDTH), LRU_CONV ** -0.5)
    c_conv_b = nrm((N_ODD, LRU_WIDTH), 0.01)
    c_gate_a_w = nrm((N_ODD, LRU_HEADS, LRU_BLOCK, LRU_BLOCK), LRU_BLOCK ** -0.5)
    c_gate_a_b = nrm((N_ODD, LRU_WIDTH), 0.01)
    c_gate_x_w = nrm((N_ODD, LRU_HEADS, LRU_BLOCK, LRU_BLOCK), LRU_BLOCK ** -0.5)
    c_gate_x_b = nrm((N_ODD, LRU_WIDTH), 0.01)
    a_c = uni((N_ODD, LRU_WIDTH), 0.9, 0.999) ** (1.0 / RG_C)
    c_lambda = jnp.log(a_c) - jnp.log1p(-a_c)
    c_w_out = nrm((N_ODD, LRU_WIDTH, D_MODEL), LRU_WIDTH ** -0.5)
    ffn_w_up = nrm((DEPTH, D_MODEL, 2 * D_FF), D_MODEL ** -0.5)
    ffn_conv_w = nrm((DEPTH, FFN_CONV, D_FF), FFN_CONV ** -0.5)
    ffn_conv_b = nrm((DEPTH, D_FF), 0.01)
    ffn_w_down = nrm((DEPTH, D_FF, D_MODEL), D_FF ** -0.5)
    return {'x': x, 'norm_mix': norm_mix, 'norm_ffn': norm_ffn, 'norm_final': norm_final,
            'ab_w_in': ab_w_in, 'gdn_conv_w': gdn_conv_w, 'gdn_a_log': gdn_a_log,
            'gdn_dt_bias': gdn_dt_bias, 'gdn_norm': gdn_norm, 'hgrn_lower_bounds': hgrn_lower_bounds,
            'hgrn_norm': hgrn_norm, 'ab_w_out': ab_w_out, 'c_w_in': c_w_in, 'c_conv_w': c_conv_w,
            'c_conv_b': c_conv_b, 'c_gate_a_w': c_gate_a_w, 'c_gate_a_b': c_gate_a_b,
            'c_gate_x_w': c_gate_x_w, 'c_gate_x_b': c_gate_x_b, 'c_lambda': c_lambda,
            'c_w_out': c_w_out, 'ffn_w_up': ffn_w_up, 'ffn_conv_w': ffn_conv_w,
            'ffn_conv_b': ffn_conv_b, 'ffn_w_down': ffn_w_down}


def _fwd_reference(x, norm_mix, norm_ffn, norm_final, ab_w_in, gdn_conv_w, gdn_a_log, gdn_dt_bias,
              gdn_norm, hgrn_lower_bounds, hgrn_norm, ab_w_out, c_w_in, c_conv_w, c_conv_b,
              c_gate_a_w, c_gate_a_b, c_gate_x_w, c_gate_x_b, c_lambda, c_w_out,
              ffn_w_up, ffn_conv_w, ffn_conv_b, ffn_w_down):
    lb_p = jax.nn.softmax(hgrn_lower_bounds.astype(jnp.float32), axis=0)
    lower_bounds = jnp.cumsum(lb_p, axis=0) - lb_p[0]
    for layer in range(DEPTH):
        j = layer // 2
        h = _rmsnorm(x, norm_mix[layer])
        if layer % 2 == 0:
            mix = _even_mixer(h, ab_w_in[j], gdn_conv_w[j], gdn_a_log[j], gdn_dt_bias[j],
                              gdn_norm[j], lower_bounds[j], hgrn_norm[j], ab_w_out[j])
        else:
            mix = _rglru_block(h, c_w_in[j], c_conv_w[j], c_conv_b[j], c_gate_a_w[j],
                               c_gate_a_b[j], c_gate_x_w[j], c_gate_x_b[j], c_lambda[j], c_w_out[j])
        x = x + mix
        h = _rmsnorm(x, norm_ffn[layer])
        x = x + _conv_ffn(h, ffn_w_up[layer], ffn_conv_w[layer], ffn_conv_b[layer], ffn_w_down[layer])
    return _rmsnorm(x, norm_final)


import jax as _jax
import jax.numpy as _jnp

TWIN_FORMAT = 'train_step'
FWD_PARAMS = ['x', 'norm_mix', 'norm_ffn', 'norm_final', 'ab_w_in', 'gdn_conv_w', 'gdn_a_log', 'gdn_dt_bias', 'gdn_norm', 'hgrn_lower_bounds', 'hgrn_norm', 'ab_w_out', 'c_w_in', 'c_conv_w', 'c_conv_b', 'c_gate_a_w', 'c_gate_a_b', 'c_gate_x_w', 'c_gate_x_b', 'c_lambda', 'c_w_out', 'ffn_w_up', 'ffn_conv_w', 'ffn_conv_b', 'ffn_w_down']
TWIN_WEIGHTS = ['norm_mix', 'norm_ffn', 'norm_final', 'ab_w_in', 'gdn_conv_w', 'gdn_a_log', 'gdn_dt_bias', 'gdn_norm', 'hgrn_lower_bounds', 'hgrn_norm', 'ab_w_out', 'c_w_in', 'c_conv_w', 'c_conv_b', 'c_gate_a_w', 'c_gate_a_b', 'c_gate_x_w', 'c_gate_x_b', 'c_lambda', 'c_w_out', 'ffn_w_up', 'ffn_conv_w', 'ffn_conv_b', 'ffn_w_down']
TWIN_DIFF_INPUT = 'x'
TWIN_INPUTS = ['x', 'norm_mix', 'norm_ffn', 'norm_final', 'ab_w_in', 'gdn_conv_w', 'gdn_a_log', 'gdn_dt_bias', 'gdn_norm', 'hgrn_lower_bounds', 'hgrn_norm', 'ab_w_out', 'c_w_in', 'c_conv_w', 'c_conv_b', 'c_gate_a_w', 'c_gate_a_b', 'c_gate_x_w', 'c_gate_x_b', 'c_lambda', 'c_w_out', 'ffn_w_up', 'ffn_conv_w', 'ffn_conv_b', 'ffn_w_down', 'loss_target', 'm_norm_mix', 'm_norm_ffn', 'm_norm_final', 'm_ab_w_in', 'm_gdn_conv_w', 'm_gdn_a_log', 'm_gdn_dt_bias', 'm_gdn_norm', 'm_hgrn_lower_bounds', 'm_hgrn_norm', 'm_ab_w_out', 'm_c_w_in', 'm_c_conv_w', 'm_c_conv_b', 'm_c_gate_a_w', 'm_c_gate_a_b', 'm_c_gate_x_w', 'm_c_gate_x_b', 'm_c_lambda', 'm_c_w_out', 'm_ffn_w_up', 'm_ffn_conv_w', 'm_ffn_conv_b', 'm_ffn_w_down', 'v_norm_mix', 'v_norm_ffn', 'v_norm_final', 'v_ab_w_in', 'v_gdn_conv_w', 'v_gdn_a_log', 'v_gdn_dt_bias', 'v_gdn_norm', 'v_hgrn_lower_bounds', 'v_hgrn_norm', 'v_ab_w_out', 'v_c_w_in', 'v_c_conv_w', 'v_c_conv_b', 'v_c_gate_a_w', 'v_c_gate_a_b', 'v_c_gate_x_w', 'v_c_gate_x_b', 'v_c_lambda', 'v_c_w_out', 'v_ffn_w_up', 'v_ffn_conv_w', 'v_ffn_conv_b', 'v_ffn_w_down']
TWIN_OUTPUTS = ['loss', 'grad_x', 'grad_norm_mix', 'grad_norm_ffn', 'grad_norm_final', 'grad_ab_w_in', 'grad_gdn_conv_w', 'grad_gdn_a_log', 'grad_gdn_dt_bias', 'grad_gdn_norm', 'grad_hgrn_lower_bounds', 'grad_hgrn_norm', 'grad_ab_w_out', 'grad_c_w_in', 'grad_c_conv_w', 'grad_c_conv_b', 'grad_c_gate_a_w', 'grad_c_gate_a_b', 'grad_c_gate_x_w', 'grad_c_gate_x_b', 'grad_c_lambda', 'grad_c_w_out', 'grad_ffn_w_up', 'grad_ffn_conv_w', 'grad_ffn_conv_b', 'grad_ffn_w_down', 'delta_norm_mix', 'delta_norm_ffn', 'delta_norm_final', 'delta_ab_w_in', 'delta_gdn_conv_w', 'delta_gdn_a_log', 'delta_gdn_dt_bias', 'delta_gdn_norm', 'delta_hgrn_lower_bounds', 'delta_hgrn_norm', 'delta_ab_w_out', 'delta_c_w_in', 'delta_c_conv_w', 'delta_c_conv_b', 'delta_c_gate_a_w', 'delta_c_gate_a_b', 'delta_c_gate_x_w', 'delta_c_gate_x_b', 'delta_c_lambda', 'delta_c_w_out', 'delta_ffn_w_up', 'delta_ffn_conv_w', 'delta_ffn_conv_b', 'delta_ffn_w_down', 'new_m_norm_mix', 'new_m_norm_ffn', 'new_m_norm_final', 'new_m_ab_w_in', 'new_m_gdn_conv_w', 'new_m_gdn_a_log', 'new_m_gdn_dt_bias', 'new_m_gdn_norm', 'new_m_hgrn_lower_bounds', 'new_m_hgrn_norm', 'new_m_ab_w_out', 'new_m_c_w_in', 'new_m_c_conv_w', 'new_m_c_conv_b', 'new_m_c_gate_a_w', 'new_m_c_gate_a_b', 'new_m_c_gate_x_w', 'new_m_c_gate_x_b', 'new_m_c_lambda', 'new_m_c_w_out', 'new_m_ffn_w_up', 'new_m_ffn_conv_w', 'new_m_ffn_conv_b', 'new_m_ffn_w_down', 'new_v_norm_mix', 'new_v_norm_ffn', 'new_v_norm_final', 'new_v_ab_w_in', 'new_v_gdn_conv_w', 'new_v_gdn_a_log', 'new_v_gdn_dt_bias', 'new_v_gdn_norm', 'new_v_hgrn_lower_bounds', 'new_v_hgrn_norm', 'new_v_ab_w_out', 'new_v_c_w_in', 'new_v_c_conv_w', 'new_v_c_conv_b', 'new_v_c_gate_a_w', 'new_v_c_gate_a_b', 'new_v_c_gate_x_w', 'new_v_c_gate_x_b', 'new_v_c_lambda', 'new_v_c_w_out', 'new_v_ffn_w_up', 'new_v_ffn_conv_w', 'new_v_ffn_conv_b', 'new_v_ffn_w_down']
TWIN_LEAF_KINDS = {'loss': 'loss', 'grad_x': 'grad_x', 'grad_norm_mix': 'grad_w', 'grad_norm_ffn': 'grad_w', 'grad_norm_final': 'grad_w', 'grad_ab_w_in': 'grad_w', 'grad_gdn_conv_w': 'grad_w', 'grad_gdn_a_log': 'grad_w', 'grad_gdn_dt_bias': 'grad_w', 'grad_gdn_norm': 'grad_w', 'grad_hgrn_lower_bounds': 'grad_w', 'grad_hgrn_norm': 'grad_w', 'grad_ab_w_out': 'grad_w', 'grad_c_w_in': 'grad_w', 'grad_c_conv_w': 'grad_w', 'grad_c_conv_b': 'grad_w', 'grad_c_gate_a_w': 'grad_w', 'grad_c_gate_a_b': 'grad_w', 'grad_c_gate_x_w': 'grad_w', 'grad_c_gate_x_b': 'grad_w', 'grad_c_lambda': 'grad_w', 'grad_c_w_out': 'grad_w', 'grad_ffn_w_up': 'grad_w', 'grad_ffn_conv_w': 'grad_w', 'grad_ffn_conv_b': 'grad_w', 'grad_ffn_w_down': 'grad_w', 'delta_norm_mix': 'delta_w', 'delta_norm_ffn': 'delta_w', 'delta_norm_final': 'delta_w', 'delta_ab_w_in': 'delta_w', 'delta_gdn_conv_w': 'delta_w', 'delta_gdn_a_log': 'delta_w', 'delta_gdn_dt_bias': 'delta_w', 'delta_gdn_norm': 'delta_w', 'delta_hgrn_lower_bounds': 'delta_w', 'delta_hgrn_norm': 'delta_w', 'delta_ab_w_out': 'delta_w', 'delta_c_w_in': 'delta_w', 'delta_c_conv_w': 'delta_w', 'delta_c_conv_b': 'delta_w', 'delta_c_gate_a_w': 'delta_w', 'delta_c_gate_a_b': 'delta_w', 'delta_c_gate_x_w': 'delta_w', 'delta_c_gate_x_b': 'delta_w', 'delta_c_lambda': 'delta_w', 'delta_c_w_out': 'delta_w', 'delta_ffn_w_up': 'delta_w', 'delta_ffn_conv_w': 'delta_w', 'delta_ffn_conv_b': 'delta_w', 'delta_ffn_w_down': 'delta_w', 'new_m_norm_mix': 'new_m', 'new_m_norm_ffn': 'new_m', 'new_m_norm_final': 'new_m', 'new_m_ab_w_in': 'new_m', 'new_m_gdn_conv_w': 'new_m', 'new_m_gdn_a_log': 'new_m', 'new_m_gdn_dt_bias': 'new_m', 'new_m_gdn_norm': 'new_m', 'new_m_hgrn_lower_bounds': 'new_m', 'new_m_hgrn_norm': 'new_m', 'new_m_ab_w_out': 'new_m', 'new_m_c_w_in': 'new_m', 'new_m_c_conv_w': 'new_m', 'new_m_c_conv_b': 'new_m', 'new_m_c_gate_a_w': 'new_m', 'new_m_c_gate_a_b': 'new_m', 'new_m_c_gate_x_w': 'new_m', 'new_m_c_gate_x_b': 'new_m', 'new_m_c_lambda': 'new_m', 'new_m_c_w_out': 'new_m', 'new_m_ffn_w_up': 'new_m', 'new_m_ffn_conv_w': 'new_m', 'new_m_ffn_conv_b': 'new_m', 'new_m_ffn_w_down': 'new_m', 'new_v_norm_mix': 'new_v', 'new_v_norm_ffn': 'new_v', 'new_v_norm_final': 'new_v', 'new_v_ab_w_in': 'new_v', 'new_v_gdn_conv_w': 'new_v', 'new_v_gdn_a_log': 'new_v', 'new_v_gdn_dt_bias': 'new_v', 'new_v_gdn_norm': 'new_v', 'new_v_hgrn_lower_bounds': 'new_v', 'new_v_hgrn_norm': 'new_v', 'new_v_ab_w_out': 'new_v', 'new_v_c_w_in': 'new_v', 'new_v_c_conv_w': 'new_v', 'new_v_c_conv_b': 'new_v', 'new_v_c_gate_a_w': 'new_v', 'new_v_c_gate_a_b': 'new_v', 'new_v_c_gate_x_w': 'new_v', 'new_v_c_gate_x_b': 'new_v', 'new_v_c_lambda': 'new_v', 'new_v_c_w_out': 'new_v', 'new_v_ffn_w_up': 'new_v', 'new_v_ffn_conv_w': 'new_v', 'new_v_ffn_conv_b': 'new_v', 'new_v_ffn_w_down': 'new_v'}


def _forward(args):
    return _fwd_reference(*[args[k] for k in FWD_PARAMS])


def _output_shape():
    out = _jax.eval_shape(lambda: _forward(_fwd_setup_inputs(0)))
    return out.shape, out.dtype

N_MICROBATCH = 1
ADAM_LR = 0.001
ADAM_B1 = 0.9
ADAM_B2 = 0.999
ADAM_EPS = 1e-08
ADAM_WD = 0.01
ADAM_STEP = 10
PER_EXAMPLE_BATCH_AXIS = {'x': 0, 'loss_target': 0}
SHARED_INPUTS = []
_WEIGHT_DTYPES = {'norm_mix': _jnp.float32, 'norm_ffn': _jnp.float32, 'norm_final': _jnp.float32, 'ab_w_in': _jnp.float32, 'gdn_conv_w': _jnp.float32, 'gdn_a_log': _jnp.float32, 'gdn_dt_bias': _jnp.float32, 'gdn_norm': _jnp.float32, 'hgrn_lower_bounds': _jnp.float32, 'hgrn_norm': _jnp.float32, 'ab_w_out': _jnp.float32, 'c_w_in': _jnp.float32, 'c_conv_w': _jnp.float32, 'c_conv_b': _jnp.float32, 'c_gate_a_w': _jnp.float32, 'c_gate_a_b': _jnp.float32, 'c_gate_x_w': _jnp.float32, 'c_gate_x_b': _jnp.float32, 'c_lambda': _jnp.float32, 'c_w_out': _jnp.float32, 'ffn_w_up': _jnp.float32, 'ffn_conv_w': _jnp.float32, 'ffn_conv_b': _jnp.float32, 'ffn_w_down': _jnp.float32}
MOMENT_SCALE = {'norm_mix': 1.072662e-01, 'norm_ffn': 8.782801e-02, 'norm_final': 1.599136e+01, 'ab_w_in': 6.312873e-02, 'gdn_conv_w': 6.340060e-02, 'gdn_a_log': 5.029636e-01, 'gdn_dt_bias': 4.945357e-01, 'gdn_norm': 1.771318e-01, 'hgrn_lower_bounds': 5.504404e-03, 'hgrn_norm': 1.738203e-01, 'ab_w_out': 8.619006e-02, 'c_w_in': 5.872390e-02, 'c_conv_w': 6.120109e-02, 'c_conv_b': 5.121939e-01, 'c_gate_a_w': 1.188048e-02, 'c_gate_a_b': 1.322122e-02, 'c_gate_x_w': 2.117475e-02, 'c_gate_x_b': 2.471942e-02, 'c_lambda': 3.029860e-02, 'c_w_out': 6.075590e-02, 'ffn_w_up': 3.747011e-02, 'ffn_conv_w': 3.822100e-02, 'ffn_conv_b': 3.683148e-02, 'ffn_w_down': 6.117079e-02}


def _to_microbatches(a, axis):
    t = _jnp.moveaxis(a, axis, 0)
    t = t.reshape((N_MICROBATCH, t.shape[0] // N_MICROBATCH) + t.shape[1:])
    return _jnp.moveaxis(t, 1, axis + 1)


def setup_inputs(seed: int = 0) -> dict:
    inp = _fwd_setup_inputs(seed)
    key = _jax.random.fold_in(_jax.random.key(seed), 7919)
    shape, _ = _output_shape()
    out = dict(inp)
    out["loss_target"] = _jax.random.normal(_jax.random.fold_in(key, 0), shape, _jnp.float32)
    for i, name in enumerate(TWIN_WEIGHTS):
        w = inp[name].astype(_jnp.float32)
        if MOMENT_SCALE is None:
            s = _jnp.sqrt(_jnp.mean(_jnp.square(w)) + 1e-30)
        else:
            s = MOMENT_SCALE[name]
        km, kv = _jax.random.split(_jax.random.fold_in(key, i + 1))
        out[name] = w
        out["m_" + name] = s * _jax.random.normal(km, w.shape, _jnp.float32)
        out["v_" + name] = (s * s) * _jax.random.uniform(kv, w.shape, _jnp.float32, 0.5, 1.5)
    if N_MICROBATCH > 1:
        for name, axis in PER_EXAMPLE_BATCH_AXIS.items():
            out[name] = _to_microbatches(out[name], axis)
    return {'x': out['x'], 'norm_mix': out['norm_mix'], 'norm_ffn': out['norm_ffn'], 'norm_final': out['norm_final'], 'ab_w_in': out['ab_w_in'], 'gdn_conv_w': out['gdn_conv_w'], 'gdn_a_log': out['gdn_a_log'], 'gdn_dt_bias': out['gdn_dt_bias'], 'gdn_norm': out['gdn_norm'], 'hgrn_lower_bounds': out['hgrn_lower_bounds'], 'hgrn_norm': out['hgrn_norm'], 'ab_w_out': out['ab_w_out'], 'c_w_in': out['c_w_in'], 'c_conv_w': out['c_conv_w'], 'c_conv_b': out['c_conv_b'], 'c_gate_a_w': out['c_gate_a_w'], 'c_gate_a_b': out['c_gate_a_b'], 'c_gate_x_w': out['c_gate_x_w'], 'c_gate_x_b': out['c_gate_x_b'], 'c_lambda': out['c_lambda'], 'c_w_out': out['c_w_out'], 'ffn_w_up': out['ffn_w_up'], 'ffn_conv_w': out['ffn_conv_w'], 'ffn_conv_b': out['ffn_conv_b'], 'ffn_w_down': out['ffn_w_down'], 'loss_target': out['loss_target'], 'm_norm_mix': out['m_norm_mix'], 'm_norm_ffn': out['m_norm_ffn'], 'm_norm_final': out['m_norm_final'], 'm_ab_w_in': out['m_ab_w_in'], 'm_gdn_conv_w': out['m_gdn_conv_w'], 'm_gdn_a_log': out['m_gdn_a_log'], 'm_gdn_dt_bias': out['m_gdn_dt_bias'], 'm_gdn_norm': out['m_gdn_norm'], 'm_hgrn_lower_bounds': out['m_hgrn_lower_bounds'], 'm_hgrn_norm': out['m_hgrn_norm'], 'm_ab_w_out': out['m_ab_w_out'], 'm_c_w_in': out['m_c_w_in'], 'm_c_conv_w': out['m_c_conv_w'], 'm_c_conv_b': out['m_c_conv_b'], 'm_c_gate_a_w': out['m_c_gate_a_w'], 'm_c_gate_a_b': out['m_c_gate_a_b'], 'm_c_gate_x_w': out['m_c_gate_x_w'], 'm_c_gate_x_b': out['m_c_gate_x_b'], 'm_c_lambda': out['m_c_lambda'], 'm_c_w_out': out['m_c_w_out'], 'm_ffn_w_up': out['m_ffn_w_up'], 'm_ffn_conv_w': out['m_ffn_conv_w'], 'm_ffn_conv_b': out['m_ffn_conv_b'], 'm_ffn_w_down': out['m_ffn_w_down'], 'v_norm_mix': out['v_norm_mix'], 'v_norm_ffn': out['v_norm_ffn'], 'v_norm_final': out['v_norm_final'], 'v_ab_w_in': out['v_ab_w_in'], 'v_gdn_conv_w': out['v_gdn_conv_w'], 'v_gdn_a_log': out['v_gdn_a_log'], 'v_gdn_dt_bias': out['v_gdn_dt_bias'], 'v_gdn_norm': out['v_gdn_norm'], 'v_hgrn_lower_bounds': out['v_hgrn_lower_bounds'], 'v_hgrn_norm': out['v_hgrn_norm'], 'v_ab_w_out': out['v_ab_w_out'], 'v_c_w_in': out['v_c_w_in'], 'v_c_conv_w': out['v_c_conv_w'], 'v_c_conv_b': out['v_c_conv_b'], 'v_c_gate_a_w': out['v_c_gate_a_w'], 'v_c_gate_a_b': out['v_c_gate_a_b'], 'v_c_gate_x_w': out['v_c_gate_x_w'], 'v_c_gate_x_b': out['v_c_gate_x_b'], 'v_c_lambda': out['v_c_lambda'], 'v_c_w_out': out['v_c_w_out'], 'v_ffn_w_up': out['v_ffn_w_up'], 'v_ffn_conv_w': out['v_ffn_conv_w'], 'v_ffn_conv_b': out['v_ffn_conv_b'], 'v_ffn_w_down': out['v_ffn_w_down']}


def _loss(weights, diff, rest, loss_target):
    with _jax.named_scope("forward"):
        args = {**rest, TWIN_DIFF_INPUT: diff, **{k: w.astype(_WEIGHT_DTYPES[k]) for k, w in weights.items()}}
        y = _forward(args)
    with _jax.named_scope("loss_head"):
        err = _jnp.square(y.astype(_jnp.float32) - loss_target)
        return 0.5 * _jnp.sum(_jnp.mean(err, axis=-1)) if err.ndim else 0.5 * err


def _adamw(w, g, m, v):
    m = ADAM_B1 * m + (1.0 - ADAM_B1) * g
    v = ADAM_B2 * v + (1.0 - ADAM_B2) * _jnp.square(g)
    m_hat = m / (1.0 - ADAM_B1 ** ADAM_STEP)
    v_hat = v / (1.0 - ADAM_B2 ** ADAM_STEP)
    delta = -ADAM_LR * (m_hat / (_jnp.sqrt(v_hat) + ADAM_EPS) + ADAM_WD * w)
    return delta, m, v


def reference(x, norm_mix, norm_ffn, norm_final, ab_w_in, gdn_conv_w, gdn_a_log, gdn_dt_bias, gdn_norm, hgrn_lower_bounds, hgrn_norm, ab_w_out, c_w_in, c_conv_w, c_conv_b, c_gate_a_w, c_gate_a_b, c_gate_x_w, c_gate_x_b, c_lambda, c_w_out, ffn_w_up, ffn_conv_w, ffn_conv_b, ffn_w_down, loss_target, m_norm_mix, m_norm_ffn, m_norm_final, m_ab_w_in, m_gdn_conv_w, m_gdn_a_log, m_gdn_dt_bias, m_gdn_norm, m_hgrn_lower_bounds, m_hgrn_norm, m_ab_w_out, m_c_w_in, m_c_conv_w, m_c_conv_b, m_c_gate_a_w, m_c_gate_a_b, m_c_gate_x_w, m_c_gate_x_b, m_c_lambda, m_c_w_out, m_ffn_w_up, m_ffn_conv_w, m_ffn_conv_b, m_ffn_w_down, v_norm_mix, v_norm_ffn, v_norm_final, v_ab_w_in, v_gdn_conv_w, v_gdn_a_log, v_gdn_dt_bias, v_gdn_norm, v_hgrn_lower_bounds, v_hgrn_norm, v_ab_w_out, v_c_w_in, v_c_conv_w, v_c_conv_b, v_c_gate_a_w, v_c_gate_a_b, v_c_gate_x_w, v_c_gate_x_b, v_c_lambda, v_c_w_out, v_ffn_w_up, v_ffn_conv_w, v_ffn_conv_b, v_ffn_w_down):
    given = dict(x=x, norm_mix=norm_mix, norm_ffn=norm_ffn, norm_final=norm_final, ab_w_in=ab_w_in, gdn_conv_w=gdn_conv_w, gdn_a_log=gdn_a_log, gdn_dt_bias=gdn_dt_bias, gdn_norm=gdn_norm, hgrn_lower_bounds=hgrn_lower_bounds, hgrn_norm=hgrn_norm, ab_w_out=ab_w_out, c_w_in=c_w_in, c_conv_w=c_conv_w, c_conv_b=c_conv_b, c_gate_a_w=c_gate_a_w, c_gate_a_b=c_gate_a_b, c_gate_x_w=c_gate_x_w, c_gate_x_b=c_gate_x_b, c_lambda=c_lambda, c_w_out=c_w_out, ffn_w_up=ffn_w_up, ffn_conv_w=ffn_conv_w, ffn_conv_b=ffn_conv_b, ffn_w_down=ffn_w_down, loss_target=loss_target, m_norm_mix=m_norm_mix, m_norm_ffn=m_norm_ffn, m_norm_final=m_norm_final, m_ab_w_in=m_ab_w_in, m_gdn_conv_w=m_gdn_conv_w, m_gdn_a_log=m_gdn_a_log, m_gdn_dt_bias=m_gdn_dt_bias, m_gdn_norm=m_gdn_norm, m_hgrn_lower_bounds=m_hgrn_lower_bounds, m_hgrn_norm=m_hgrn_norm, m_ab_w_out=m_ab_w_out, m_c_w_in=m_c_w_in, m_c_conv_w=m_c_conv_w, m_c_conv_b=m_c_conv_b, m_c_gate_a_w=m_c_gate_a_w, m_c_gate_a_b=m_c_gate_a_b, m_c_gate_x_w=m_c_gate_x_w, m_c_gate_x_b=m_c_gate_x_b, m_c_lambda=m_c_lambda, m_c_w_out=m_c_w_out, m_ffn_w_up=m_ffn_w_up, m_ffn_conv_w=m_ffn_conv_w, m_ffn_conv_b=m_ffn_conv_b, m_ffn_w_down=m_ffn_w_down, v_norm_mix=v_norm_mix, v_norm_ffn=v_norm_ffn, v_norm_final=v_norm_final, v_ab_w_in=v_ab_w_in, v_gdn_conv_w=v_gdn_conv_w, v_gdn_a_log=v_gdn_a_log, v_gdn_dt_bias=v_gdn_dt_bias, v_gdn_norm=v_gdn_norm, v_hgrn_lower_bounds=v_hgrn_lower_bounds, v_hgrn_norm=v_hgrn_norm, v_ab_w_out=v_ab_w_out, v_c_w_in=v_c_w_in, v_c_conv_w=v_c_conv_w, v_c_conv_b=v_c_conv_b, v_c_gate_a_w=v_c_gate_a_w, v_c_gate_a_b=v_c_gate_a_b, v_c_gate_x_w=v_c_gate_x_w, v_c_gate_x_b=v_c_gate_x_b, v_c_lambda=v_c_lambda, v_c_w_out=v_c_w_out, v_ffn_w_up=v_ffn_w_up, v_ffn_conv_w=v_ffn_conv_w, v_ffn_conv_b=v_ffn_conv_b, v_ffn_w_down=v_ffn_w_down)
    weights = {n: given[n] for n in TWIN_WEIGHTS}
    shared = {n: given[n] for n in SHARED_INPUTS}
    per_example = {n: given[n] for n in ['x']}
    grad_fn = _jax.value_and_grad(_loss, argnums=(0, 1))

    def one_microbatch(ex, loss_target):
        ex = dict(ex)
        diff = ex.pop(TWIN_DIFF_INPUT)
        return grad_fn(weights, diff, {**shared, **ex}, loss_target)

    if N_MICROBATCH == 1:
        loss, (grad_w, grad_x) = one_microbatch(per_example, given["loss_target"])
    else:
        def body(carry, xs):
            loss_sum, grad_sum = carry
            l_k, (gw_k, gx_k) = one_microbatch(xs[0], xs[1])
            with _jax.named_scope("update"):
                return (loss_sum + l_k, _jax.tree.map(_jnp.add, grad_sum, gw_k)), gx_k

        init = (_jnp.zeros((), _jnp.float32), _jax.tree.map(_jnp.zeros_like, weights))
        (loss, grad_w), grad_x = _jax.lax.scan(body, init, (per_example, given["loss_target"]))
    with _jax.named_scope("update"):
        delta_w, new_m, new_v = {}, {}, {}
        for n in TWIN_WEIGHTS:
            delta_w[n], new_m[n], new_v[n] = _adamw(weights[n], grad_w[n], given["m_" + n], given["v_" + n])
    return (loss, grad_x, *[grad_w[n] for n in TWIN_WEIGHTS], *[delta_w[n] for n in TWIN_WEIGHTS],
            *[new_m[n] for n in TWIN_WEIGHTS], *[new_v[n] for n in TWIN_WEIGHTS])
```

```python
import functools

import jax
import jax.numpy as jnp
from jax import lax
from jax.experimental import pallas as pl
from jax.experimental.pallas import tpu as pltpu

f32 = jnp.float32
bf16 = jnp.bfloat16
HI = lax.Precision.HIGHEST

D = 1024
EPS = 1e-6
F_FLOOR = 1e-30
GDN_CHUNK = 64
HGRN_CHUNK = 16
HGRN_STEP = 128
HEAD = 128
NH = 4
LRU_BLOCK = 256
D_FF = 2816
RG_C = 8.0
C_QKV, C_Z, C_QB, C_FB, C_IB, C_GB, C_BA = 0, 1536, 2048, 2560, 3072, 3584, 4096
AB_COLS_PAD = 4224
TT = 256
VMEM_LIMIT = 56 * 1024 * 1024

ADAM_LR, ADAM_B1, ADAM_B2, ADAM_EPS, ADAM_WD, ADAM_STEP = 0.001, 0.9, 0.999, 1e-08, 0.01, 10


def _cp(**kw):
    return pltpu.CompilerParams(vmem_limit_bytes=VMEM_LIMIT, **kw)


def _sds(shape, dtype):
    return jax.ShapeDtypeStruct(shape, dtype)


def _dot(a, b, dims, precision=None):
    return lax.dot_general(a, b, (dims, ((), ())), precision=precision, preferred_element_type=f32)


NN = ((1,), (0,))
NT = ((1,), (1,))
TN = ((0,), (0,))


def _rms(x, g):
    return x * lax.rsqrt(jnp.mean(x * x, axis=-1, keepdims=True) + EPS) * g


def _silu(x):
    return x * jax.nn.sigmoid(x)


def _mm(a, b, mode, tm, tn, out_dtype, name):
    if mode == "nn":
        (m, k), n = a.shape, b.shape[1]
        a_spec = pl.BlockSpec((tm, k), lambda i, j: (i, 0))
        b_spec = pl.BlockSpec((k, tn), lambda i, j: (0, j))
        dims = NN
    elif mode == "nt":
        (m, k), n = a.shape, b.shape[0]
        a_spec = pl.BlockSpec((tm, k), lambda i, j: (i, 0))
        b_spec = pl.BlockSpec((tn, k), lambda i, j: (j, 0))
        dims = NT
    else:
        (k, m), n = a.shape, b.shape[1]
        a_spec = pl.BlockSpec((k, tm), lambda i, j: (0, i))
        b_spec = pl.BlockSpec((k, tn), lambda i, j: (0, j))
        dims = TN
    assert m % tm == 0 and n % tn == 0, (name, m, n, tm, tn)

    def body(a_ref, b_ref, o_ref):
        o_ref[...] = _dot(a_ref[...], b_ref[...], dims).astype(out_dtype)

    return pl.pallas_call(
        body, grid=(m // tm, n // tn), in_specs=[a_spec, b_spec],
        out_specs=pl.BlockSpec((tm, tn), lambda i, j: (i, j)),
        out_shape=_sds((m, n), out_dtype), compiler_params=_cp(), name=name)(a, b)


def _tile(n, cands):
    for c in cands:
        if n % c == 0:
            return c
    raise ValueError(n)


def _mm_auto(a, b, mode, out_dtype, name):
    m = a.shape[1] if mode == "tn" else a.shape[0]
    n = b.shape[0] if mode == "nt" else b.shape[1]
    return _mm(a, b, mode, _tile(m, (512, 256, 128)), _tile(n, (1024, 1408, 512, 384, 256, 128)), out_dtype, name)


def _rmsnorm_fwd(x, gain, name):
    t = x.shape[0]

    def body(x_ref, g_ref, h_ref):
        h_ref[...] = _rms(x_ref[...], g_ref[...]).astype(bf16)

    return pl.pallas_call(
        body, grid=(t // TT,),
        in_specs=[pl.BlockSpec((TT, D), lambda i: (i, 0)), pl.BlockSpec((1, D), lambda i: (0, 0))],
        out_specs=pl.BlockSpec((TT, D), lambda i: (i, 0)),
        out_shape=_sds((t, D), bf16), compiler_params=_cp(), name=name)(x, gain)


def _rmsnorm_bwd(x, gain, dh, dres, name):
    t = x.shape[0]

    def body(x_ref, g_ref, dh_ref, dres_ref, dx_ref, dxb_ref, dg_ref):
        _, vjp = jax.vjp(_rms, x_ref[...], g_ref[...])
        dx, dg = vjp(dh_ref[...])
        dx = dx + dres_ref[...]
        dx_ref[...] = dx
        dxb_ref[...] = dx.astype(bf16)

        @pl.when(pl.program_id(0) == 0)
        def _():
            dg_ref[...] = jnp.zeros_like(dg_ref)

        dg_ref[...] += dg

    row = pl.BlockSpec((TT, D), lambda i: (i, 0))
    vec = pl.BlockSpec((1, D), lambda i: (0, 0))
    return pl.pallas_call(
        body, grid=(t // TT,), in_specs=[row, vec, row, row], out_specs=[row, row, vec],
        out_shape=[_sds((t, D), f32), _sds((t, D), bf16), _sds((1, D), f32)],
        compiler_params=_cp(), name=name)(x, gain, dh, dres)


def _residual_add(x, y, name):
    t = x.shape[0]

    def body(x_ref, y_ref, o_ref):
        o_ref[...] = x_ref[...] + y_ref[...]

    row = pl.BlockSpec((TT, D), lambda i: (i, 0))
    return pl.pallas_call(body, grid=(t // TT,), in_specs=[row, row], out_specs=row,
                          out_shape=_sds((t, D), f32), compiler_params=_cp(), name=name)(x, y)


def _final_loss(x, gain, target, name):
    t = x.shape[0]

    def loss_fn(xv, gv, tv):
        e = _rms(xv, gv) - tv
        return 0.5 * jnp.sum(jnp.mean(e * e, axis=-1))

    def body(x_ref, g_ref, t_ref, loss_ref, dx_ref, dxb_ref, dg_ref):
        val, (dx, dg) = jax.value_and_grad(loss_fn, argnums=(0, 1))(x_ref[...], g_ref[...], t_ref[...])

        @pl.when(pl.program_id(0) == 0)
        def _():
            dg_ref[...] = jnp.zeros_like(dg_ref)
            loss_ref[...] = jnp.zeros_like(loss_ref)

        dx_ref[...] = dx
        dxb_ref[...] = dx.astype(bf16)
        dg_ref[...] += dg
        loss_ref[...] += jnp.full((1, 128), val, f32)

    row = pl.BlockSpec((TT, D), lambda i: (i, 0))
    vec = pl.BlockSpec((1, D), lambda i: (0, 0))
    return pl.pallas_call(
        body, grid=(t // TT,), in_specs=[row, vec, row],
        out_specs=[pl.BlockSpec((1, 128), lambda i: (0, 0)), row, row, vec],
        out_shape=[_sds((1, 128), f32), _sds((t, D), f32), _sds((t, D), bf16), _sds((1, D), f32)],
        compiler_params=_cp(), name=name)(x, gain, target)


def _conv_fwd(x, col0, c, w, b, name, tc=256, val=None, val_col0=0):
    t = x.shape[0]
    width = w.shape[0]
    nt = t // TT
    hb = TT // 8

    def body(*refs):
        if val is None:
            x_ref, xh_ref, w_ref, b_ref, o_ref, xp = refs
        else:
            x_ref, xh_ref, w_ref, b_ref, v_ref, o_ref, act_ref, xp = refs
        i = pl.program_id(1)
        xp[0:8, :] = jnp.where(i == 0, 0.0, xh_ref[...])
        xp[8:, :] = x_ref[...]
        acc = jnp.zeros((TT, tc), f32) + b_ref[...]
        for k in range(width):
            acc = acc + w_ref[k:k + 1, :] * xp[pl.ds(8 - (width - 1) + k, TT), :]
        o_ref[...] = acc
        if val is not None:
            act_ref[...] = (_silu(acc) * v_ref[...]).astype(bf16)

    in_specs = [
        pl.BlockSpec((TT, tc), lambda j, i: (i, j + col0)),
        pl.BlockSpec((8, tc), lambda j, i: (jnp.maximum(i * hb - 1, 0), j + col0)),
        pl.BlockSpec((width, tc), lambda j, i: (0, j)),
        pl.BlockSpec((1, tc), lambda j, i: (0, j)),
    ]
    args = [x, x, w, b]
    out_specs = [pl.BlockSpec((TT, tc), lambda j, i: (i, j))]
    out_shape = [_sds((t, c), f32)]
    if val is not None:
        in_specs.append(pl.BlockSpec((TT, tc), lambda j, i: (i, j + val_col0)))
        args.append(val)
        out_specs.append(pl.BlockSpec((TT, tc), lambda j, i: (i, j)))
        out_shape.append(_sds((t, c), bf16))
    res = pl.pallas_call(
        body, grid=(c // tc, nt), in_specs=in_specs, out_specs=out_specs, out_shape=out_shape,
        scratch_shapes=[pltpu.VMEM((TT + 8, tc), f32)], compiler_params=_cp(), name=name)(*args)
    return res[0] if val is None else res


def _conv_bwd(dc, x, col0, w, name, tc=256, dx_dtype=bf16):
    t, c = dc.shape
    width = w.shape[0]
    nt = t // TT
    hb = TT // 8

    def body(dc_ref, dcn_ref, x_ref, xh_ref, w_ref, dx_ref, dw_ref, db_ref, dcp, xp):
        i = pl.program_id(1)
        dcv = dc_ref[...]
        dcp[0:TT, :] = dcv
        dcp[TT:, :] = jnp.where(i == nt - 1, 0.0, dcn_ref[...])
        xp[0:8, :] = jnp.where(i == 0, 0.0, xh_ref[...])
        xp[8:, :] = x_ref[...]

        @pl.when(i == 0)
        def _():
            dw_ref[...] = jnp.zeros_like(dw_ref)
            db_ref[...] = jnp.zeros_like(db_ref)

        acc = jnp.zeros((TT, tc), f32)
        for k in range(width):
            acc = acc + w_ref[k:k + 1, :] * dcp[pl.ds((width - 1) - k, TT), :]
            dw_ref[k:k + 1, :] += jnp.sum(dcv * xp[pl.ds(8 - (width - 1) + k, TT), :], axis=0, keepdims=True)
        dx_ref[...] = acc.astype(dx_dtype)
        db_ref[...] += jnp.sum(dcv, axis=0, keepdims=True)

    in_specs = [
        pl.BlockSpec((TT, tc), lambda j, i: (i, j)),
        pl.BlockSpec((8, tc), lambda j, i: (jnp.minimum((i + 1) * hb, t // 8 - 1), j)),
        pl.BlockSpec((TT, tc), lambda j, i: (i, j + col0)),
        pl.BlockSpec((8, tc), lambda j, i: (jnp.maximum(i * hb - 1, 0), j + col0)),
        pl.BlockSpec((width, tc), lambda j, i: (0, j)),
    ]
    out_specs = [
        pl.BlockSpec((TT, tc), lambda j, i: (i, j)),
        pl.BlockSpec((8, tc), lambda j, i: (0, j)),
        pl.BlockSpec((1, tc), lambda j, i: (0, j)),
    ]
    return pl.pallas_call(
        body, grid=(c // tc, nt), in_specs=in_specs, out_specs=out_specs,
        out_shape=[_sds((t, c), dx_dtype), _sds((8, c), f32), _sds((1, c), f32)],
        scratch_shapes=[pltpu.VMEM((TT + 8, tc), f32), pltpu.VMEM((TT + 8, tc), f32)],
        compiler_params=_cp(), name=name)(dc, dc, x, x, w)


def _gdn_chunk(cq, ck, cv, ba, z, s, alog_v, dtb_v, gain, hd):
    c = GDN_CHUNK
    lane = lax.broadcasted_iota(jnp.int32, (1, HEAD), 1)
    mb = (lane == hd).astype(f32)
    ma = (lane == hd + NH).astype(f32)
    beta = jax.nn.sigmoid(jnp.sum(ba * mb, axis=1, keepdims=True))
    alpha = jnp.sum(ba * ma, axis=1, keepdims=True)
    alog = jnp.sum(alog_v * ma, axis=1, keepdims=True)
    dtb = jnp.sum(dtb_v * ma, axis=1, keepdims=True)
    g = -jnp.exp(alog) * jax.nn.softplus(alpha + dtb)
    q = _silu(cq)
    q = q * lax.rsqrt(jnp.sum(q * q, axis=-1, keepdims=True) + EPS) * (HEAD ** -0.5)
    k = _silu(ck)
    k = k * lax.rsqrt(jnp.sum(k * k, axis=-1, keepdims=True) + EPS)
    v = _silu(cv)
    row = lax.broadcasted_iota(jnp.int32, (c, c), 0)
    col = lax.broadcasted_iota(jnp.int32, (c, c), 1)
    causal = row >= col
    tril = causal.astype(f32)
    gc = _dot(tril, jnp.broadcast_to(g, (c, HEAD)), NN, HI)
    gcc = _dot(tril, jnp.broadcast_to(g, (c, c)), NN, HI)
    diff = jnp.where(causal, gcc - gcc.T, 0.0)
    decay = jnp.where(causal, jnp.exp(diff), 0.0)
    kb = k * beta
    nmat = -jnp.where(row > col, _dot(kb, k, NT) * decay, 0.0)
    egc = jnp.exp(gc)
    u = v * beta
    w = kb * egc
    p = nmat
    for it in range(6):
        u = u + _dot(p, u, NN, HI)
        w = w + _dot(p, w, NN, HI)
        if it < 5:
            p = _dot(p, p, NN, HI)
    attn = _dot(q, k, NT) * decay
    rowv = lax.broadcasted_iota(jnp.int32, (c, 1), 0)
    gc_last = jnp.sum(jnp.where(rowv == c - 1, gc, 0.0), axis=0, keepdims=True)
    q_dec = q * egc
    k_dec = k * jnp.exp(gc_last - gc)
    v_new = u - _dot(w, s, NN)
    o = _dot(q_dec, s, NN) + _dot(attn, v_new, NN)
    s_new = s * jnp.exp(gc_last) + _dot(k_dec, v_new, TN)
    o = _rms(o, gain) * _silu(z)
    return o, s_new


def _gdn_specs(nch):
    c = GDN_CHUNK
    blk = lambda off: pl.BlockSpec((c, HEAD), lambda n, h: (n, off + h))
    vec = pl.BlockSpec((1, HEAD), lambda n, h: (0, 0))
    return blk, vec


def _gdn_fwd(conv, p, alog_v, dtb_v, gain, name):
    t = conv.shape[0]
    c = GDN_CHUNK
    nch = t // c
    blk, vec = _gdn_specs(nch)

    def body(cq, ck, cv, ba, z, al, dt, gn, o_ref, ss_ref, s_scr):
        n, h = pl.program_id(0), pl.program_id(1)

        @pl.when(n == 0)
        def _():
            s_scr[h] = jnp.zeros((HEAD, HEAD), f32)

        s = s_scr[h]
        ss_ref[0, 0] = s
        o, s_new = _gdn_chunk(cq[...], ck[...], cv[...], ba[...], z[...], s, al[...], dt[...], gn[...], h)
        o_ref[...] = o.astype(bf16)
        s_scr[h] = s_new

    return pl.pallas_call(
        body, grid=(nch, NH),
        in_specs=[blk(0), blk(4), blk(8),
                  pl.BlockSpec((c, HEAD), lambda n, h: (n, C_BA // HEAD)),
                  blk(C_Z // HEAD), vec, vec, vec],
        out_specs=[pl.BlockSpec((c, HEAD), lambda n, h: (n, h)),
                   pl.BlockSpec((1, 1, HEAD, HEAD), lambda n, h: (n, h, 0, 0))],
        out_shape=[_sds((t, NH * HEAD), bf16), _sds((nch, NH, HEAD, HEAD), f32)],
        scratch_shapes=[pltpu.VMEM((NH, HEAD, HEAD), f32)],
        compiler_params=_cp(), name=name)(conv, conv, conv, p, p, alog_v, dtb_v, gain)


def _gdn_bwd(conv, p, alog_v, dtb_v, gain, states, do, name):
    t = conv.shape[0]
    c = GDN_CHUNK
    nch = t // c
    rblk = lambda off: pl.BlockSpec((c, HEAD), lambda n, h: (nch - 1 - n, off + h))
    vec = pl.BlockSpec((1, HEAD), lambda n, h: (0, 0))

    def body(cq, ck, cv, ba, z, al, dt, gn, ss_ref, do_ref,
             dq_ref, dk_ref, dv_ref, dz_ref, dba_ref, dal_ref, ddt_ref, dgn_ref, ds_scr):
        n, h = pl.program_id(0), pl.program_id(1)

        @pl.when(n == 0)
        def _():
            ds_scr[h] = jnp.zeros((HEAD, HEAD), f32)

        @pl.when((n == 0) & (h == 0))
        def _():
            dal_ref[...] = jnp.zeros_like(dal_ref)
            ddt_ref[...] = jnp.zeros_like(ddt_ref)
            dgn_ref[...] = jnp.zeros_like(dgn_ref)

        @pl.when(h == 0)
        def _():
            dba_ref[...] = jnp.zeros_like(dba_ref)

        fn = functools.partial(_gdn_chunk, hd=h)
        _, vjp = jax.vjp(fn, cq[...], ck[...], cv[...], ba[...], z[...], ss_ref[0, 0], al[...], dt[...], gn[...])
        dq, dk, dv, dba, dz, ds, dal, ddt, dgn = vjp((do_ref[...], ds_scr[h]))
        dq_ref[...] = dq
        dk_ref[...] = dk
        dv_ref[...] = dv
        dz_ref[...] = dz
        dba_ref[...] += dba
        dal_ref[...] += dal
        ddt_ref[...] += ddt
        dgn_ref[...] += dgn
        ds_scr[h] = ds

    hblk = pl.BlockSpec((c, HEAD), lambda n, h: (nch - 1 - n, h))
    return pl.pallas_call(
        body, grid=(nch, NH),
        in_specs=[rblk(0), rblk(4), rblk(8),
                  pl.BlockSpec((c, HEAD), lambda n, h: (nch - 1 - n, C_BA // HEAD)),
                  rblk(C_Z // HEAD), vec, vec, vec,
                  pl.BlockSpec((1, 1, HEAD, HEAD), lambda n, h: (nch - 1 - n, h, 0, 0)),
                  hblk],
        out_specs=[hblk, hblk, hblk, hblk,
                   pl.BlockSpec((c, HEAD), lambda n, h: (nch - 1 - n, 0)), vec, vec, vec],
        out_shape=[_sds((t, 512), f32)] * 4 + [_sds((t, HEAD), f32)] + [_sds((1, HEAD), f32)] * 3,
        scratch_shapes=[pltpu.VMEM((NH, HEAD, HEAD), f32)],
        compiler_params=_cp(), name=name)(conv, conv, conv, p, p, alog_v, dtb_v, gain, states, do)


def _hgrn_chunk(qb, fb, ib, gb, st, lb, gain):
    c = HGRN_CHUNK
    f = lb + (1.0 - lb) * jax.nn.sigmoid(fb)
    logf = jnp.log(jnp.maximum(f, F_FLOOR))
    k = 1.0 - f
    q = _silu(qb)
    v = ib
    row = lax.broadcasted_iota(jnp.int32, (c, c), 0)
    col = lax.broadcasted_iota(jnp.int32, (c, c), 1)
    b = _dot((row >= col).astype(f32), logf, NN, HI)
    ri = lax.broadcasted_iota(jnp.int32, (c, 1), 0)
    o = jnp.zeros((c, HEAD), f32)
    for j in range(c):
        mj = ri == j
        bj = jnp.sum(jnp.where(mj, b, 0.0), axis=0, keepdims=True)
        kj = jnp.sum(jnp.where(mj, k, 0.0), axis=0, keepdims=True)
        vj = jnp.sum(jnp.where(mj, v, 0.0), axis=0, keepdims=True)
        ok = ri >= j
        e = jnp.where(ok, jnp.exp(jnp.where(ok, b - bj, 0.0)), 0.0)
        o = o + jnp.sum(q * kj * e, axis=1, keepdims=True) * vj
    b_last = jnp.sum(jnp.where(ri == c - 1, b, 0.0), axis=0, keepdims=True)
    q_dec = q * jnp.exp(b)
    k_dec = k * jnp.exp(b_last - b)
    o = o + _dot(q_dec, st, NT)
    st_new = st * jnp.exp(b_last) + _dot(v, k_dec, TN)
    o = _rms(o, gain) * _silu(gb)
    return o, st_new


def _hgrn_fwd(p, lb, gain, name):
    t = p.shape[0]
    r = HGRN_STEP
    ns = t // r
    nsub = r // HGRN_CHUNK
    blk = lambda off: pl.BlockSpec((r, HEAD), lambda n, h: (n, off // HEAD + h))

    def body(qb, fb, ib, gb, lb_ref, gn, o_ref, ss_ref, s_scr):
        n, h = pl.program_id(0), pl.program_id(1)

        @pl.when(n == 0)
        def _():
            s_scr[h] = jnp.zeros((HEAD, HEAD), f32)

        st = s_scr[h]
        ss_ref[0, 0] = st
        for ch in range(nsub):
            rows = pl.ds(ch * HGRN_CHUNK, HGRN_CHUNK)
            o, st = _hgrn_chunk(qb[rows, :], fb[rows, :], ib[rows, :], gb[rows, :], st, lb_ref[...], gn[...])
            o_ref[rows, :] = o.astype(bf16)
        s_scr[h] = st

    return pl.pallas_call(
        body, grid=(ns, NH),
        in_specs=[blk(C_QB), blk(C_FB), blk(C_IB), blk(C_GB),
                  pl.BlockSpec((1, HEAD), lambda n, h: (0, h)), pl.BlockSpec((1, HEAD), lambda n, h: (0, 0))],
        out_specs=[pl.BlockSpec((r, HEAD), lambda n, h: (n, h)),
                   pl.BlockSpec((1, 1, HEAD, HEAD), lambda n, h: (n, h, 0, 0))],
        out_shape=[_sds((t, NH * HEAD), bf16), _sds((ns, NH, HEAD, HEAD), f32)],
        scratch_shapes=[pltpu.VMEM((NH, HEAD, HEAD), f32)],
        compiler_params=_cp(), name=name)(p, p, p, p, lb, gain)


def _hgrn_bwd(p, lb, gain, states, do, name):
    t = p.shape[0]
    r = HGRN_STEP
    ns = t // r
    nsub = r // HGRN_CHUNK
    blk = lambda off: pl.BlockSpec((r, HEAD), lambda n, h: (ns - 1 - n, off // HEAD + h))
    hblk = pl.BlockSpec((r, HEAD), lambda n, h: (ns - 1 - n, h))

    def body(qb, fb, ib, gb, lb_ref, gn, ss_ref, do_ref,
             dqb, dfb, dib, dgb, dlb_ref, dgn_ref, ds_scr, st_scr):
        n, h = pl.program_id(0), pl.program_id(1)

        @pl.when(n == 0)
        def _():
            ds_scr[h] = jnp.zeros((HEAD, HEAD), f32)

        @pl.when((n == 0) & (h == 0))
        def _():
            dgn_ref[...] = jnp.zeros_like(dgn_ref)

        lbv, gnv = lb_ref[...], gn[...]
        st = ss_ref[0, 0]
        for ch in range(nsub):
            rows = pl.ds(ch * HGRN_CHUNK, HGRN_CHUNK)
            st_scr[ch] = st
            if ch < nsub - 1:
                _, st = _hgrn_chunk(qb[rows, :], fb[rows, :], ib[rows, :], gb[rows, :], st, lbv, gnv)
        ds = ds_scr[h]
        dlb = jnp.zeros((1, HEAD), f32)
        dgn = jnp.zeros((1, HEAD), f32)
        for ch in reversed(range(nsub)):
            rows = pl.ds(ch * HGRN_CHUNK, HGRN_CHUNK)
            _, vjp = jax.vjp(_hgrn_chunk, qb[rows, :], fb[rows, :], ib[rows, :], gb[rows, :], st_scr[ch], lbv, gnv)
            g_q, g_f, g_i, g_g, ds, g_lb, g_gn = vjp((do_ref[rows, :], ds))
            dqb[rows, :] = g_q
            dfb[rows, :] = g_f
            dib[rows, :] = g_i
            dgb[rows, :] = g_g
            dlb = dlb + g_lb
            dgn = dgn + g_gn
        ds_scr[h] = ds
        dlb_ref[0] = dlb
        dgn_ref[...] += dgn

    return pl.pallas_call(
        body, grid=(ns, NH),
        in_specs=[blk(C_QB), blk(C_FB), blk(C_IB), blk(C_GB),
                  pl.BlockSpec((1, HEAD), lambda n, h: (0, h)), pl.BlockSpec((1, HEAD), lambda n, h: (0, 0)),
                  pl.BlockSpec((1, 1, HEAD, HEAD), lambda n, h: (ns - 1 - n, h, 0, 0)), hblk],
        out_specs=[hblk, hblk, hblk, hblk,
                   pl.BlockSpec((1, 1, HEAD), lambda n, h: (n, 0, h)),
                   pl.BlockSpec((1, HEAD), lambda n, h: (0, 0))],
        out_shape=[_sds((t, 512), f32)] * 4 + [_sds((ns, 1, 512), f32), _sds((1, HEAD), f32)],
        scratch_shapes=[pltpu.VMEM((NH, HEAD, HEAD), f32), pltpu.VMEM((nsub, HEAD, HEAD), f32)],
        compiler_params=_cp(), name=name)(p, p, p, p, lb, gain, states, do)


def _lru_gates(xb, wa, wx, ba, bx, lam):
    xh = xb.astype(bf16)
    r = jax.nn.sigmoid(_dot(xh, wa.astype(bf16), NN) + ba)
    i = jax.nn.sigmoid(_dot(xh, wx.astype(bf16), NN) + bx)
    log_a = -RG_C * r * jax.nn.softplus(-lam)
    a = jnp.exp(log_a)
    t2 = 2.0 * log_a
    series = -t2 * (1.0 + t2 * (0.5 + t2 * (1.0 / 6.0 + t2 * (1.0 / 24.0))))
    om = jnp.where(t2 > -1e-2, series, 1.0 - jnp.exp(t2))
    u = jnp.sqrt(jnp.maximum(om, 0.0)) * (i * xb)
    return a, u


def _lru_gates_fwd(xc, wa, wx, ba, bx, lam, name):
    t = xc.shape[0]
    blk = pl.BlockSpec((TT, LRU_BLOCK), lambda h, i: (i, h))
    wsp = pl.BlockSpec((1, LRU_BLOCK, LRU_BLOCK), lambda h, i: (h, 0, 0))
    vsp = pl.BlockSpec((1, LRU_BLOCK), lambda h, i: (0, h))

    def body(x_ref, wa_ref, wx_ref, ba_ref, bx_ref, lam_ref, a_ref, u_ref):
        a, u = _lru_gates(x_ref[...], wa_ref[0], wx_ref[0], ba_ref[...], bx_ref[...], lam_ref[...])
        a_ref[...] = a
        u_ref[...] = u

    return pl.pallas_call(
        body, grid=(NH, t // TT), in_specs=[blk, wsp, wsp, vsp, vsp, vsp], out_specs=[blk, blk],
        out_shape=[_sds((t, D), f32)] * 2, compiler_params=_cp(), name=name)(xc, wa, wx, ba, bx, lam)


def _lru_gates_bwd(xc, wa, wx, ba, bx, lam, da, du, name):
    t = xc.shape[0]
    blk = pl.BlockSpec((TT, LRU_BLOCK), lambda h, i: (i, h))
    wsp = pl.BlockSpec((1, LRU_BLOCK, LRU_BLOCK), lambda h, i: (h, 0, 0))
    vsp = pl.BlockSpec((1, LRU_BLOCK), lambda h, i: (0, h))

    def body(x_ref, wa_ref, wx_ref, ba_ref, bx_ref, lam_ref, da_ref, du_ref,
             dx_ref, dwa_ref, dwx_ref, dba_ref, dbx_ref, dlam_ref):
        @pl.when(pl.program_id(1) == 0)
        def _():
            for r in (dwa_ref, dwx_ref, dba_ref, dbx_ref, dlam_ref):
                r[...] = jnp.zeros_like(r)

        _, vjp = jax.vjp(_lru_gates, x_ref[...], wa_ref[0], wx_ref[0], ba_ref[...], bx_ref[...], lam_ref[...])
        dx, dwa, dwx, dba, dbx, dlam = vjp((da_ref[...], du_ref[...]))
        dx_ref[...] = dx
        dwa_ref[0] += dwa
        dwx_ref[0] += dwx
        dba_ref[...] += dba
        dbx_ref[...] += dbx
        dlam_ref[...] += dlam

    return pl.pallas_call(
        body, grid=(NH, t // TT), in_specs=[blk, wsp, wsp, vsp, vsp, vsp, blk, blk],
        out_specs=[blk, wsp, wsp, vsp, vsp, vsp],
        out_shape=[_sds((t, D), f32), _sds((NH, LRU_BLOCK, LRU_BLOCK), f32), _sds((NH, LRU_BLOCK, LRU_BLOCK), f32),
                   _sds((1, D), f32), _sds((1, D), f32), _sds((1, D), f32)],
        compiler_params=_cp(), name=name)(xc, wa, wx, ba, bx, lam, da, du)


_SCAN_SHIFTS = (1, 2, 4, 8, 16, 32, 64, 128)
_SCAN_PAD = 128


def _gelu(y):
    return jax.nn.gelu(y, approximate=True)


def _lru_scan_fwd(a, u, p2, name):
    t = a.shape[0]
    tc = 128
    blk = pl.BlockSpec((TT, tc), lambda j, i: (i, j))

    def body(a_ref, u_ref, y_ref, h_ref, hg_ref, a_s, b_s, carry):
        i = pl.program_id(1)

        @pl.when(i == 0)
        def _():
            carry[...] = jnp.zeros_like(carry)
            a_s[0:_SCAN_PAD, :] = jnp.ones((_SCAN_PAD, tc), f32)
            b_s[0:_SCAN_PAD, :] = jnp.zeros((_SCAN_PAD, tc), f32)

        av, bv = a_ref[...], u_ref[...]
        for s in _SCAN_SHIFTS:
            a_s[_SCAN_PAD:, :] = av
            b_s[_SCAN_PAD:, :] = bv
            ash = a_s[pl.ds(_SCAN_PAD - s, TT), :]
            bsh = b_s[pl.ds(_SCAN_PAD - s, TT), :]
            bv = bv + av * bsh
            av = av * ash
        h = bv + av * carry[7:8, :]
        h_ref[...] = h
        hg_ref[...] = (h * _gelu(y_ref[...])).astype(bf16)
        carry[...] = h[TT - 8:, :]

    return pl.pallas_call(
        body, grid=(D // tc, t // TT), in_specs=[blk, blk, blk], out_specs=[blk, blk],
        out_shape=[_sds((t, D), f32), _sds((t, D), bf16)],
        scratch_shapes=[pltpu.VMEM((_SCAN_PAD + TT, tc), f32), pltpu.VMEM((_SCAN_PAD + TT, tc), f32),
                        pltpu.VMEM((8, tc), f32)],
        compiler_params=_cp(), name=name)(a, u, p2)


def _lru_scan_bwd(a, h, p2, dhg, name):
    t = a.shape[0]
    tc = 128
    nt = t // TT
    hb = TT // 8
    rblk = pl.BlockSpec((TT, tc), lambda j, i: (nt - 1 - i, j))

    def body(a_ref, an_ref, h_ref, hp_ref, y_ref, dhg_ref, du_ref, da_ref, dy_ref, a_s, b_s, ap, hp, carry):
        i = pl.program_id(1)

        @pl.when(i == 0)
        def _():
            carry[...] = jnp.zeros_like(carry)
            a_s[TT:, :] = jnp.ones((_SCAN_PAD, tc), f32)
            b_s[TT:, :] = jnp.zeros((_SCAN_PAD, tc), f32)

        ap[0:TT, :] = a_ref[...]
        ap[TT:, :] = jnp.where(i == 0, 0.0, an_ref[...])
        hp[0:8, :] = jnp.where(i == nt - 1, 0.0, hp_ref[...])
        hp[8:, :] = h_ref[...]
        y = y_ref[...]
        gate, gvjp = jax.vjp(_gelu, y)
        dhg_v = dhg_ref[...]
        dy_ref[...] = gvjp(dhg_v * h_ref[...])[0]
        av = ap[pl.ds(1, TT), :]
        bv = dhg_v * gate
        for s in _SCAN_SHIFTS:
            a_s[0:TT, :] = av
            b_s[0:TT, :] = bv
            ash = a_s[pl.ds(s, TT), :]
            bsh = b_s[pl.ds(s, TT), :]
            bv = bv + av * bsh
            av = av * ash
        g = bv + av * carry[0:1, :]
        du_ref[...] = g
        da_ref[...] = g * hp[pl.ds(7, TT), :]
        carry[...] = g[0:8, :]

    in_specs = [
        rblk,
        pl.BlockSpec((8, tc), lambda j, i: (jnp.minimum((nt - i) * hb, t // 8 - 1), j)),
        rblk,
        pl.BlockSpec((8, tc), lambda j, i: (jnp.maximum((nt - 1 - i) * hb - 1, 0), j)),
        rblk, rblk,
    ]
    return pl.pallas_call(
        body, grid=(D // tc, nt), in_specs=in_specs, out_specs=[rblk, rblk, rblk],
        out_shape=[_sds((t, D), f32)] * 3,
        scratch_shapes=[pltpu.VMEM((TT + _SCAN_PAD, tc), f32), pltpu.VMEM((TT + _SCAN_PAD, tc), f32),
                        pltpu.VMEM((TT + 8, tc), f32), pltpu.VMEM((TT + 8, tc), f32), pltpu.VMEM((8, tc), f32)],
        compiler_params=_cp(), name=name)(a, a, h, h, p2, dhg)


def _ffn_act_bwd(gc, up, dact, name, tc=256):
    t = gc.shape[0]
    blk = pl.BlockSpec((TT, tc), lambda i, j: (i, j))

    def body(gc_ref, v_ref, da_ref, dgc_ref, dv_ref):
        _, vjp = jax.vjp(lambda g, v: _silu(g) * v, gc_ref[...], v_ref[...])
        dg, dv = vjp(da_ref[...])
        dgc_ref[...] = dg
        dv_ref[...] = dv.astype(bf16)

    return pl.pallas_call(
        body, grid=(t // TT, D_FF // tc),
        in_specs=[blk, pl.BlockSpec((TT, tc), lambda i, j: (i, j + D_FF // tc)), blk], out_specs=[blk, blk],
        out_shape=[_sds((t, D_FF), f32), _sds((t, D_FF), bf16)], compiler_params=_cp(), name=name)(gc, up, dact)


def _lower_bounds_fwd(w):
    def body(w_ref, o0_ref, o1_ref):
        wv = w_ref[...]
        o0, o1 = _lb_rows(wv[0:1, :], wv[1:2, :])
        o0_ref[...] = o0
        o1_ref[...] = o1

    return pl.pallas_call(body, out_shape=[_sds((1, 512), f32)] * 2, name="lower_bounds_fwd")(w)


def _lb_rows(w0, w1):
    m = jnp.maximum(w0, w1)
    e0, e1 = jnp.exp(w0 - m), jnp.exp(w1 - m)
    s = e0 + e1
    p0, p1 = e0 / s, e1 / s
    return p0 - p0, (p0 + p1) - p0


def _lower_bounds_bwd(w, d0, d1):
    def body(w_ref, d0_ref, d1_ref, g0_ref, g1_ref):
        wv = w_ref[...]
        _, vjp = jax.vjp(_lb_rows, wv[0:1, :], wv[1:2, :])
        g0, g1 = vjp((d0_ref[...], d1_ref[...]))
        g0_ref[...] = g0
        g1_ref[...] = g1

    return pl.pallas_call(body, out_shape=[_sds((1, 512), f32)] * 2, name="lower_bounds_bwd")(w, d0, d1)


def _local_step(x, target, wt):
    depth = 4
    res = []
    lb0, lb1 = _lower_bounds_fwd(wt["hgrn_lower_bounds"])
    lbs = [lb0, lb1]
    for layer in range(depth):
        j = layer // 2
        sv = {"x_in": x}
        h1 = _rmsnorm_fwd(x, wt["norm_mix"][layer], "rms_fwd")
        sv["h1"] = h1
        if layer % 2 == 0:
            p = _mm_auto(h1, wt["ab_w_in"][j], "nn", f32, "mm_ab_in")
            conv = _conv_fwd(p, 0, 1536, wt["gdn_conv_w"][j], jnp.zeros((1, 1536), f32), "gdn_conv_fwd")
            o_a, s_a = _gdn_fwd(conv, p, wt["alog_v"][j], wt["dtb_v"][j], wt["gdn_norm"][j], "gdn_fwd")
            o_b, s_b = _hgrn_fwd(p, lbs[j], wt["hgrn_norm"][j], "hgrn_fwd")
            o = jnp.concatenate([o_a, o_b], axis=1)
            mix = _mm_auto(o, wt["ab_w_out"][j], "nn", f32, "mm_ab_out")
            sv.update(p=p, conv=conv, s_a=s_a, s_b=s_b, o=o)
        else:
            p2 = _mm_auto(h1, wt["c_w_in"][j], "nn", f32, "mm_c_in")
            xc = _conv_fwd(p2, D // 256, D, wt["c_conv_w"][j], wt["c_conv_b"][j], "lru_conv_fwd")
            a, u = _lru_gates_fwd(xc, wt["c_gate_a_w"][j], wt["c_gate_x_w"][j], wt["c_gate_a_b"][j],
                                  wt["c_gate_x_b"][j], wt["c_lambda"][j], "lru_gates_fwd")
            h, hg = _lru_scan_fwd(a, u, p2, "lru_scan_fwd")
            mix = _mm_auto(hg, wt["c_w_out"][j], "nn", f32, "mm_c_out")
            sv.update(p2=p2, xc=xc, a=a, h=h, hg=hg)
        x = _residual_add(x, mix, "res_add")
        sv["x_mid"] = x
        h2 = _rmsnorm_fwd(x, wt["norm_ffn"][layer], "rms_fwd")
        up = _mm_auto(h2, wt["ffn_w_up"][layer], "nn", f32, "mm_up")
        gc, act = _conv_fwd(up, 0, D_FF, wt["ffn_conv_w"][layer], wt["ffn_conv_b"][layer], "ffn_conv_fwd",
                            val=up, val_col0=D_FF // 256)
        down = _mm_auto(act, wt["ffn_w_down"][layer], "nn", f32, "mm_down")
        x = _residual_add(x, down, "res_add")
        sv.update(h2=h2, up=up, gc=gc, act=act)
        res.append(sv)

    loss, dx, dxb, d_norm_final = _final_loss(x, wt["norm_final"], target, "final_loss")

    g = {k: [None] * len(v) for k, v in wt.items() if isinstance(v, list)}
    g["norm_final"] = d_norm_final
    d_lbs = [None, None]
    for layer in reversed(range(depth)):
        j = layer // 2
        sv = res[layer]
        dact = _mm_auto(dxb, wt["ffn_w_down"][layer], "nt", f32, "mm_down_dx")
        g["ffn_w_down"][layer] = _mm_auto(sv["act"], dxb, "tn", f32, "mm_down_dw")
        dgc, dval = _ffn_act_bwd(sv["gc"], sv["up"], dact, "ffn_act_bwd")
        dgate, dcw, dcb = _conv_bwd(dgc, sv["up"], 0, wt["ffn_conv_w"][layer], "ffn_conv_bwd")
        g["ffn_conv_w"][layer] = dcw[:3]
        g["ffn_conv_b"][layer] = dcb
        dup = jnp.concatenate([dgate, dval], axis=1)
        g["ffn_w_up"][layer] = _mm_auto(sv["h2"], dup, "tn", f32, "mm_up_dw")
        dh2 = _mm_auto(dup, wt["ffn_w_up"][layer], "nt", f32, "mm_up_dx")
        dx, dxb, g["norm_ffn"][layer] = _rmsnorm_bwd(sv["x_mid"], wt["norm_ffn"][layer], dh2, dx, "rms_bwd")
        if layer % 2 == 0:
            do = _mm_auto(dxb, wt["ab_w_out"][j], "nt", f32, "mm_ab_out_dx")
            g["ab_w_out"][j] = _mm_auto(sv["o"], dxb, "tn", f32, "mm_ab_out_dw")
            dq, dk, dv, dz, dba, dal, ddt, dgn = _gdn_bwd(
                sv["conv"], sv["p"], wt["alog_v"][j], wt["dtb_v"][j], wt["gdn_norm"][j], sv["s_a"], do[:, :512], "gdn_bwd")
            g["alog_v"][j], g["dtb_v"][j], g["gdn_norm"][j] = dal, ddt, dgn
            dconv = jnp.concatenate([dq, dk, dv], axis=1)
            dqkv, dcw, _ = _conv_bwd(dconv, sv["p"], 0, wt["gdn_conv_w"][j], "gdn_conv_bwd")
            g["gdn_conv_w"][j] = dcw[:4]
            dqb, dfb, dib, dgb, dlb, dhn = _hgrn_bwd(sv["p"], lbs[j], wt["hgrn_norm"][j], sv["s_b"], do[:, 512:], "hgrn_bwd")
            g["hgrn_norm"][j] = dhn
            d_lbs[j] = jnp.sum(dlb, axis=0)
            dp = jnp.concatenate([dqkv] + [t_.astype(bf16) for t_ in (dz, dqb, dfb, dib, dgb, dba)], axis=1)
            g["ab_w_in"][j] = _mm_auto(sv["h1"], dp, "tn", f32, "mm_ab_in_dw")
            dh1 = _mm_auto(dp, wt["ab_w_in"][j], "nt", f32, "mm_ab_in_dx")
        else:
            dhg = _mm_auto(dxb, wt["c_w_out"][j], "nt", f32, "mm_c_out_dx")
            g["c_w_out"][j] = _mm_auto(sv["hg"], dxb, "tn", f32, "mm_c_out_dw")
            du, da, dy = _lru_scan_bwd(sv["a"], sv["h"], sv["p2"], dhg, "lru_scan_bwd")
            dxc, dwa, dwx, dba_, dbx_, dlam = _lru_gates_bwd(
                sv["xc"], wt["c_gate_a_w"][j], wt["c_gate_x_w"][j], wt["c_gate_a_b"][j], wt["c_gate_x_b"][j],
                wt["c_lambda"][j], da, du, "lru_gates_bwd")
            g["c_gate_a_w"][j], g["c_gate_x_w"][j] = dwa, dwx
            g["c_gate_a_b"][j], g["c_gate_x_b"][j], g["c_lambda"][j] = dba_, dbx_, dlam
            dxbr, dcw, dcb = _conv_bwd(dxc, sv["p2"], D // 256, wt["c_conv_w"][j], "lru_conv_bwd")
            g["c_conv_w"][j] = dcw[:4]
            g["c_conv_b"][j] = dcb
            dp2 = jnp.concatenate([dy.astype(bf16), dxbr], axis=1)
            g["c_w_in"][j] = _mm_auto(sv["h1"], dp2, "tn", f32, "mm_c_in_dw")
            dh1 = _mm_auto(dp2, wt["c_w_in"][j], "nt", f32, "mm_c_in_dx")
        dx, dxb, g["norm_mix"][layer] = _rmsnorm_bwd(sv["x_in"], wt["norm_mix"][layer], dh1, dx, "rms_bwd")
    g0, g1 = _lower_bounds_bwd(wt["hgrn_lower_bounds"], d_lbs[0], d_lbs[1])
    g["hgrn_lower_bounds"] = jnp.concatenate([g0, g1], axis=0)
    return loss, dx, g


def _ab_in_to_compute(w):
    return jnp.concatenate([w[:, :2048], w[:, 2056:4104], w[:, 2048:2056], jnp.zeros((D, 120), w.dtype)], axis=1)


def _ab_in_from_compute(g):
    return jnp.concatenate([g[:, :2048], g[:, 4096:4104], g[:, 2048:4096]], axis=1)


def _lane_vec(v4):
    return jnp.zeros((1, HEAD), f32).at[0, NH:2 * NH].set(v4)


def _layout_weights(fw):
    wt = {}
    wt["norm_mix"] = [fw["norm_mix"][l][None] for l in range(4)]
    wt["norm_ffn"] = [fw["norm_ffn"][l][None] for l in range(4)]
    wt["norm_final"] = fw["norm_final"][None]
    wt["ab_w_in"] = [_ab_in_to_compute(fw["ab_w_in"][j].astype(bf16)) for j in range(2)]
    wt["gdn_conv_w"] = [fw["gdn_conv_w"][j] for j in range(2)]
    wt["alog_v"] = [_lane_vec(fw["gdn_a_log"][j]) for j in range(2)]
    wt["dtb_v"] = [_lane_vec(fw["gdn_dt_bias"][j]) for j in range(2)]
    wt["gdn_norm"] = [fw["gdn_norm"][j][None] for j in range(2)]
    wt["hgrn_lower_bounds"] = fw["hgrn_lower_bounds"]
    wt["hgrn_norm"] = [fw["hgrn_norm"][j][None] for j in range(2)]
    wt["ab_w_out"] = [fw["ab_w_out"][j].astype(bf16) for j in range(2)]
    wt["c_w_in"] = [fw["c_w_in"][j].astype(bf16) for j in range(2)]
    wt["c_conv_w"] = [fw["c_conv_w"][j] for j in range(2)]
    for k in ("c_conv_b", "c_gate_a_b", "c_gate_x_b", "c_lambda"):
        wt[k] = [fw[k][j][None] for j in range(2)]
    for k in ("c_gate_a_w", "c_gate_x_w"):
        wt[k] = [fw[k][j].astype(f32) for j in range(2)]
    wt["c_w_out"] = [fw["c_w_out"][j].astype(bf16) for j in range(2)]
    wt["ffn_w_up"] = [fw["ffn_w_up"][l].astype(bf16) for l in range(4)]
    wt["ffn_conv_w"] = [fw["ffn_conv_w"][l] for l in range(4)]
    wt["ffn_conv_b"] = [fw["ffn_conv_b"][l][None] for l in range(4)]
    wt["ffn_w_down"] = [fw["ffn_w_down"][l].astype(bf16) for l in range(4)]
    return wt


def _unlayout_grads(g):
    out = {}
    for k in ("norm_mix", "norm_ffn", "gdn_norm", "hgrn_norm", "c_conv_b", "c_gate_a_b", "c_gate_x_b", "c_lambda",
              "ffn_conv_b"):
        out[k] = jnp.concatenate(g[k], axis=0)
    out["norm_final"] = g["norm_final"][0]
    out["ab_w_in"] = jnp.stack([_ab_in_from_compute(t) for t in g["ab_w_in"]])
    out["gdn_a_log"] = jnp.stack([t[0, NH:2 * NH] for t in g["alog_v"]])
    out["gdn_dt_bias"] = jnp.stack([t[0, NH:2 * NH] for t in g["dtb_v"]])
    out["hgrn_lower_bounds"] = g["hgrn_lower_bounds"]
    for k in ("gdn_conv_w", "ab_w_out", "c_w_in", "c_conv_w", "c_gate_a_w", "c_gate_x_w", "c_w_out", "ffn_w_up",
              "ffn_conv_w", "ffn_w_down"):
        out[k] = jnp.stack(g[k])
    return out


MESH = pl.DeviceIdType.MESH
ANY = pl.BlockSpec(memory_space=pl.ANY)
CHIP_RELATIONS = ((1, 0), (0, 1), (1, 1))
N_CHIPS = 4


def _coords():
    return lax.axis_index("x"), lax.axis_index("y"), lax.axis_index("c")


def _flip(v, f):
    return 1 - v if f else v


def _half_rows(c, a, align):
    return pl.ds(pl.multiple_of(c * (a // 2), align), a // 2)


def _all_gather_chips(shards, name):
    n = len(shards)
    shapes = [s.shape for s in shards]

    def body(*refs):
        ins, outs = refs[:n], refs[n:2 * n]
        send_sems, recv_sems, loc_sems = refs[2 * n:]
        x, y, c = _coords()
        me = 2 * x + y
        sibling = (x, y, 1 - c)
        started, locs = [], []
        for p in range(n):
            loc = pltpu.make_async_copy(ins[p], outs[p].at[me], loc_sems.at[p])
            loc.start()
            locs.append(loc)
        for p in range(n):
            mine = _half_rows(c, shapes[p][0], 16)
            for r, (fx, fy) in enumerate(CHIP_RELATIONS):
                cp = pltpu.make_async_remote_copy(
                    src_ref=ins[p].at[mine], dst_ref=outs[p].at[me, mine],
                    send_sem=send_sems.at[p, r], recv_sem=recv_sems.at[p, r],
                    device_id=(_flip(x, fx), _flip(y, fy), c), device_id_type=MESH)
                cp.start()
                started.append(cp)
        for r, (fx, fy) in enumerate(CHIP_RELATIONS):
            k = 2 * _flip(x, fx) + _flip(y, fy)
            for p in range(n):
                mine = _half_rows(c, shapes[p][0], 16)
                pltpu.make_async_remote_copy(
                    src_ref=ins[p].at[mine], dst_ref=outs[p].at[k, mine],
                    send_sem=send_sems.at[p, r], recv_sem=recv_sems.at[p, r],
                    device_id=(_flip(x, fx), _flip(y, fy), c), device_id_type=MESH).wait_recv()
                fwd = pltpu.make_async_remote_copy(
                    src_ref=outs[p].at[k, mine], dst_ref=outs[p].at[k, mine],
                    send_sem=send_sems.at[p, 3 + r], recv_sem=recv_sems.at[p, 3 + r],
                    device_id=sibling, device_id_type=MESH)
                fwd.start()
                started.append(fwd)
        for r, (fx, fy) in enumerate(CHIP_RELATIONS):
            k = 2 * _flip(x, fx) + _flip(y, fy)
            for p in range(n):
                theirs = _half_rows(1 - c, shapes[p][0], 16)
                pltpu.make_async_remote_copy(
                    src_ref=outs[p].at[k, theirs], dst_ref=outs[p].at[k, theirs],
                    send_sem=send_sems.at[p, 3 + r], recv_sem=recv_sems.at[p, 3 + r],
                    device_id=sibling, device_id_type=MESH).wait_recv()
        for cp in started:
            cp.wait_send()
        for loc in locs:
            loc.wait()

    return pl.pallas_call(
        body, in_specs=[ANY] * n, out_specs=[ANY] * n,
        out_shape=[_sds((N_CHIPS,) + s.shape, s.dtype) for s in shards],
        scratch_shapes=[pltpu.SemaphoreType.DMA((n, 6)), pltpu.SemaphoreType.DMA((n, 6)), pltpu.SemaphoreType.DMA((n,))],
        name=name)(*shards)


def _sibling_send_other_half(gs, name):
    n = len(gs)
    shapes = [g.shape for g in gs]

    def body(*refs):
        ins, outs = refs[:n], refs[n:2 * n]
        send_sems, recv_sems = refs[2 * n:]
        x, y, c = _coords()
        cps = []
        for p in range(n):
            theirs = _half_rows(1 - c, shapes[p][1], 8)
            cp = pltpu.make_async_remote_copy(
                src_ref=ins[p].at[:, theirs], dst_ref=outs[p],
                send_sem=send_sems.at[p], recv_sem=recv_sems.at[p],
                device_id=(x, y, 1 - c), device_id_type=MESH)
            cp.start()
            cps.append(cp)
        for cp in cps:
            cp.wait()

    return pl.pallas_call(
        body, in_specs=[ANY] * n, out_specs=[ANY] * n,
        out_shape=[_sds((s[0], s[1] // 2, s[2]), f32) for s in shapes],
        scratch_shapes=[pltpu.SemaphoreType.DMA((n,)), pltpu.SemaphoreType.DMA((n,))],
        name=name)(*gs)


def _chip_exchange(ps, name):
    n = len(ps)

    def body(*refs):
        ins, outs = refs[:n], refs[n:2 * n]
        send_sems, recv_sems, loc_sems = refs[2 * n:]
        x, y, c = _coords()
        me = 2 * x + y
        locs, cps = [], []
        for p in range(n):
            loc = pltpu.make_async_copy(ins[p].at[me], outs[p].at[me], loc_sems.at[p])
            loc.start()
            locs.append(loc)
            for r, (fx, fy) in enumerate(CHIP_RELATIONS):
                k = 2 * _flip(x, fx) + _flip(y, fy)
                cp = pltpu.make_async_remote_copy(
                    src_ref=ins[p].at[k], dst_ref=outs[p].at[me],
                    send_sem=send_sems.at[p, r], recv_sem=recv_sems.at[p, r],
                    device_id=(_flip(x, fx), _flip(y, fy), c), device_id_type=MESH)
                cp.start()
                cps.append((cp, p, r, k))
        for cp, p, r, k in cps:
            pltpu.make_async_remote_copy(
                src_ref=ins[p].at[k], dst_ref=outs[p].at[k],
                send_sem=send_sems.at[p, r], recv_sem=recv_sems.at[p, r],
                device_id=(x, y, c), device_id_type=MESH).wait_recv()
        for cp, p, r, k in cps:
            cp.wait_send()
        for loc in locs:
            loc.wait()

    return pl.pallas_call(
        body, in_specs=[ANY] * n, out_specs=[ANY] * n,
        out_shape=[_sds(p.shape, p.dtype) for p in ps],
        scratch_shapes=[pltpu.SemaphoreType.DMA((n, 3)), pltpu.SemaphoreType.DMA((n, 3)), pltpu.SemaphoreType.DMA((n,))],
        name=name)(*ps)


def _sibling_all_gather(ss, name):
    n = len(ss)
    shapes = [s.shape for s in ss]

    def body(*refs):
        ins, outs = refs[:n], refs[n:2 * n]
        send_sems, recv_sems, loc_sems = refs[2 * n:]
        x, y, c = _coords()
        locs, cps = [], []
        for p in range(n):
            mine = _half_rows(c, 2 * shapes[p][0], 8)
            loc = pltpu.make_async_copy(ins[p], outs[p].at[mine], loc_sems.at[p])
            loc.start()
            locs.append(loc)
            cp = pltpu.make_async_remote_copy(
                src_ref=ins[p], dst_ref=outs[p].at[mine],
                send_sem=send_sems.at[p], recv_sem=recv_sems.at[p],
                device_id=(x, y, 1 - c), device_id_type=MESH)
            cp.start()
            cps.append(cp)
        for p, cp in enumerate(cps):
            theirs = _half_rows(1 - c, 2 * shapes[p][0], 8)
            pltpu.make_async_remote_copy(
                src_ref=ins[p], dst_ref=outs[p].at[theirs],
                send_sem=send_sems.at[p], recv_sem=recv_sems.at[p],
                device_id=(x, y, 1 - c), device_id_type=MESH).wait_recv()
        for cp in cps:
            cp.wait_send()
        for loc in locs:
            loc.wait()

    return pl.pallas_call(
        body, in_specs=[ANY] * n, out_specs=[ANY] * n,
        out_shape=[_sds((2 * s[0], s[1]), f32) for s in shapes],
        scratch_shapes=[pltpu.SemaphoreType.DMA((n,)), pltpu.SemaphoreType.DMA((n,)), pltpu.SemaphoreType.DMA((n,))],
        name=name)(*ss)


N_DEV = 8


def _all_reduce_small(pack, name):
    rows = pack.shape[0]

    def body(in_ref, sum_ref, all_ref, send_sems, recv_sems, loc_sem):
        x, y, c = _coords()
        me = 4 * x + 2 * y + c
        loc = pltpu.make_async_copy(in_ref, all_ref.at[me], loc_sem)
        loc.start()
        cps = []
        for r in range(1, N_DEV):
            fx, fy, fc = (r >> 2) & 1, (r >> 1) & 1, r & 1
            cp = pltpu.make_async_remote_copy(
                src_ref=in_ref, dst_ref=all_ref.at[me],
                send_sem=send_sems.at[r], recv_sem=recv_sems.at[r],
                device_id=(_flip(x, fx), _flip(y, fy), _flip(c, fc)), device_id_type=MESH)
            cp.start()
            cps.append(cp)
        for r in range(1, N_DEV):
            fx, fy, fc = (r >> 2) & 1, (r >> 1) & 1, r & 1
            peer = 4 * _flip(x, fx) + 2 * _flip(y, fy) + _flip(c, fc)
            pltpu.make_async_remote_copy(
                src_ref=in_ref, dst_ref=all_ref.at[peer],
                send_sem=send_sems.at[r], recv_sem=recv_sems.at[r],
                device_id=(x, y, c), device_id_type=MESH).wait_recv()
        for cp in cps:
            cp.wait_send()
        loc.wait()
        acc = all_ref[0]
        for d in range(1, N_DEV):
            acc = acc + all_ref[d]
        sum_ref[...] = acc

    vm = pl.BlockSpec(memory_space=pltpu.VMEM)
    return pl.pallas_call(
        body, in_specs=[vm], out_specs=[vm, vm],
        out_shape=[_sds((rows, 128), f32), _sds((N_DEV, rows, 128), f32)],
        scratch_shapes=[pltpu.SemaphoreType.DMA((N_DEV,)), pltpu.SemaphoreType.DMA((N_DEV,)), pltpu.SemaphoreType.DMA],
        name=name)(pack)[0]


ROWS_EW = 128


def _add_own_half(g, rs, c_arr, name):
    s, a, b = g.shape
    nrt = (a // 2) // ROWS_EW

    def body(c_ref, g_ref, r_ref, o_ref):
        o_ref[...] = g_ref[...] + r_ref[...]

    grid_spec = pltpu.PrefetchScalarGridSpec(
        num_scalar_prefetch=1, grid=(s, nrt),
        in_specs=[pl.BlockSpec((1, ROWS_EW, b), lambda k, i, c_ref: (k, c_ref[0] * nrt + i, 0)),
                  pl.BlockSpec((1, ROWS_EW, b), lambda k, i, c_ref: (k, i, 0))],
        out_specs=pl.BlockSpec((1, ROWS_EW, b), lambda k, i, c_ref: (k, i, 0)))
    return pl.pallas_call(body, grid_spec=grid_spec, out_shape=_sds((s, a // 2, b), f32),
                          compiler_params=_cp(), name=name)(c_arr, g, rs)


def _sum_slots(rc, name):
    s, r, b = rc.shape

    def body(r_ref, o_ref):
        acc = r_ref[0]
        for k in range(1, s):
            acc = acc + r_ref[k]
        o_ref[...] = acc

    return pl.pallas_call(
        body, grid=(r // ROWS_EW,), in_specs=[pl.BlockSpec((s, ROWS_EW, b), lambda i: (0, i, 0))],
        out_specs=pl.BlockSpec((ROWS_EW, b), lambda i: (i, 0)), out_shape=_sds((r, b), f32),
        compiler_params=_cp(), name=name)(rc)


def _adamw_math(w, g, m, v):
    m = ADAM_B1 * m + (1.0 - ADAM_B1) * g
    v = ADAM_B2 * v + (1.0 - ADAM_B2) * (g * g)
    m_hat = m / (1.0 - ADAM_B1 ** ADAM_STEP)
    v_hat = v / (1.0 - ADAM_B2 ** ADAM_STEP)
    delta = -ADAM_LR * (m_hat / (jnp.sqrt(v_hat) + ADAM_EPS) + ADAM_WD * w)
    return delta, m, v


def _adamw_big(w, g, m, v, name):
    a, b = w.shape

    def body(w_ref, g_ref, m_ref, v_ref, go_ref, d_ref, mo_ref, vo_ref):
        gv = g_ref[...]
        d, mn, vn = _adamw_math(w_ref[...], gv, m_ref[...], v_ref[...])
        go_ref[...] = gv
        d_ref[...] = d
        mo_ref[...] = mn
        vo_ref[...] = vn

    blk = pl.BlockSpec((ROWS_EW, b), lambda i: (i, 0))
    return pl.pallas_call(body, grid=(a // ROWS_EW,), in_specs=[blk] * 4, out_specs=[blk] * 4,
                          out_shape=[_sds((a, b), f32)] * 4, compiler_params=_cp(), name=name)(w, g, m, v)


def _adamw_small(ws, gs, ms, vs, name):
    n = len(ws)

    def body(*refs):
        w_r, g_r, m_r, v_r = refs[:n], refs[n:2 * n], refs[2 * n:3 * n], refs[3 * n:4 * n]
        go_r, d_r, mo_r, vo_r = refs[4 * n:5 * n], refs[5 * n:6 * n], refs[6 * n:7 * n], refs[7 * n:8 * n]
        for p in range(n):
            gv = g_r[p][...]
            d, mn, vn = _adamw_math(w_r[p][...], gv, m_r[p][...], v_r[p][...])
            go_r[p][...] = gv
            d_r[p][...] = d
            mo_r[p][...] = mn
            vo_r[p][...] = vn

    vm = pl.BlockSpec(memory_space=pltpu.VMEM)
    shp = [_sds(w.shape, f32) for w in ws]
    res = pl.pallas_call(body, in_specs=[vm] * (4 * n), out_specs=[vm] * (4 * n), out_shape=shp * 4,
                         name=name)(*ws, *gs, *ms, *vs)
    return res[:n], res[n:2 * n], res[2 * n:3 * n], res[3 * n:]


WEIGHTS = ["norm_mix", "norm_ffn", "norm_final", "ab_w_in", "gdn_conv_w", "gdn_a_log", "gdn_dt_bias", "gdn_norm",
           "hgrn_lower_bounds", "hgrn_norm", "ab_w_out", "c_w_in", "c_conv_w", "c_conv_b", "c_gate_a_w", "c_gate_a_b",
           "c_gate_x_w", "c_gate_x_b", "c_lambda", "c_w_out", "ffn_w_up", "ffn_conv_w", "ffn_conv_b", "ffn_w_down"]
BIG = {"ab_w_in": "col", "ab_w_out": "row", "c_w_in": "col", "c_gate_a_w": "gate", "c_gate_x_w": "gate",
       "c_w_out": "row", "ffn_w_up": "col", "ffn_w_down": "row"}
SMALL_SHARDED = ["gdn_conv_w", "c_conv_w", "c_conv_b", "c_gate_a_b", "c_gate_x_b", "c_lambda", "ffn_conv_w"]
SMALL = [n for n in WEIGHTS if n not in BIG]
FULL_SHAPES = {
    "norm_mix": (4, 1024), "norm_ffn": (4, 1024), "norm_final": (1024,), "ab_w_in": (2, 1024, 4104),
    "gdn_conv_w": (2, 4, 1536), "gdn_a_log": (2, 4), "gdn_dt_bias": (2, 4), "gdn_norm": (2, 128),
    "hgrn_lower_bounds": (2, 512), "hgrn_norm": (2, 128), "ab_w_out": (2, 1024, 1024), "c_w_in": (2, 1024, 2048),
    "c_conv_w": (2, 4, 1024), "c_conv_b": (2, 1024), "c_gate_a_w": (2, 4, 256, 256), "c_gate_a_b": (2, 1024),
    "c_gate_x_w": (2, 4, 256, 256), "c_gate_x_b": (2, 1024), "c_lambda": (2, 1024), "c_w_out": (2, 1024, 1024),
    "ffn_w_up": (4, 1024, 5632), "ffn_conv_w": (4, 3, 2816), "ffn_conv_b": (4, 2816), "ffn_w_down": (4, 2816, 1024)}


def _shard2d(name, shard):
    return shard.reshape(-1, shard.shape[-1])


def _full_from_slots(name, slots):
    full = FULL_SHAPES[name]
    kind = BIG[name]
    if kind == "col":
        l, r, cdim = full
        return slots.reshape(4, l, r, cdim // 4).transpose(1, 2, 0, 3).reshape(full)
    if kind == "row":
        l, r, cdim = full
        return slots.reshape(4, l, r // 4, cdim).transpose(1, 0, 2, 3).reshape(full)
    l, h, r, cdim = full
    return slots.reshape(4, l, h, r // 4, cdim).transpose(1, 2, 0, 3, 4).reshape(full)


def _slots_from_full(name, g):
    full = FULL_SHAPES[name]
    kind = BIG[name]
    if kind == "col":
        l, r, cdim = full
        return g.reshape(l, r, 4, cdim // 4).transpose(2, 0, 1, 3).reshape(4, l * r, cdim // 4)
    if kind == "row":
        l, r, cdim = full
        return g.reshape(l, 4, r // 4, cdim).transpose(1, 0, 2, 3).reshape(4, l * (r // 4), cdim)
    l, h, r, cdim = full
    return g.reshape(l, h, 4, r // 4, cdim).transpose(2, 0, 1, 3, 4).reshape(4, l * h * (r // 4), cdim)


def _pack_rows(arrs, rows):
    flat = jnp.concatenate([a.reshape(-1) for a in arrs])
    return jnp.pad(flat, (0, rows * 128 - flat.shape[0])).reshape(rows, 128)


def _unpack_rows(pack, shapes):
    flat = pack.reshape(-1)
    out, off = [], 0
    for s in shapes:
        size = 1
        for d in s:
            size *= d
        out.append(flat[off:off + size].reshape(s))
        off += size
    return out


def kernel(x, norm_mix, norm_ffn, norm_final, ab_w_in, gdn_conv_w, gdn_a_log, gdn_dt_bias, gdn_norm, hgrn_lower_bounds, hgrn_norm, ab_w_out, c_w_in, c_conv_w, c_conv_b, c_gate_a_w, c_gate_a_b, c_gate_x_w, c_gate_x_b, c_lambda, c_w_out, ffn_w_up, ffn_conv_w, ffn_conv_b, ffn_w_down, loss_target, m_norm_mix, m_norm_ffn, m_norm_final, m_ab_w_in, m_gdn_conv_w, m_gdn_a_log, m_gdn_dt_bias, m_gdn_norm, m_hgrn_lower_bounds, m_hgrn_norm, m_ab_w_out, m_c_w_in, m_c_conv_w, m_c_conv_b, m_c_gate_a_w, m_c_gate_a_b, m_c_gate_x_w, m_c_gate_x_b, m_c_lambda, m_c_w_out, m_ffn_w_up, m_ffn_conv_w, m_ffn_conv_b, m_ffn_w_down, v_norm_mix, v_norm_ffn, v_norm_final, v_ab_w_in, v_gdn_conv_w, v_gdn_a_log, v_gdn_dt_bias, v_gdn_norm, v_hgrn_lower_bounds, v_hgrn_norm, v_ab_w_out, v_c_w_in, v_c_conv_w, v_c_conv_b, v_c_gate_a_w, v_c_gate_a_b, v_c_gate_x_w, v_c_gate_x_b, v_c_lambda, v_c_w_out, v_ffn_w_up, v_ffn_conv_w, v_ffn_conv_b, v_ffn_w_down):
    w = dict(zip(WEIGHTS, (norm_mix, norm_ffn, norm_final, ab_w_in, gdn_conv_w, gdn_a_log, gdn_dt_bias, gdn_norm, hgrn_lower_bounds, hgrn_norm, ab_w_out, c_w_in, c_conv_w, c_conv_b, c_gate_a_w, c_gate_a_b, c_gate_x_w, c_gate_x_b, c_lambda, c_w_out, ffn_w_up, ffn_conv_w, ffn_conv_b, ffn_w_down)))
    m = dict(zip(WEIGHTS, (m_norm_mix, m_norm_ffn, m_norm_final, m_ab_w_in, m_gdn_conv_w, m_gdn_a_log, m_gdn_dt_bias, m_gdn_norm, m_hgrn_lower_bounds, m_hgrn_norm, m_ab_w_out, m_c_w_in, m_c_conv_w, m_c_conv_b, m_c_gate_a_w, m_c_gate_a_b, m_c_gate_x_w, m_c_gate_x_b, m_c_lambda, m_c_w_out, m_ffn_w_up, m_ffn_conv_w, m_ffn_conv_b, m_ffn_w_down)))
    v = dict(zip(WEIGHTS, (v_norm_mix, v_norm_ffn, v_norm_final, v_ab_w_in, v_gdn_conv_w, v_gdn_a_log, v_gdn_dt_bias, v_gdn_norm, v_hgrn_lower_bounds, v_hgrn_norm, v_ab_w_out, v_c_w_in, v_c_conv_w, v_c_conv_b, v_c_gate_a_w, v_c_gate_a_b, v_c_gate_x_w, v_c_gate_x_b, v_c_lambda, v_c_w_out, v_ffn_w_up, v_ffn_conv_w, v_ffn_conv_b, v_ffn_w_down)))
    big = list(BIG)
    chip = 2 * lax.axis_index("x") + lax.axis_index("y")
    c_arr = lax.axis_index("c").astype(jnp.int32).reshape(1)

    small_shard_shapes = [w[n].shape for n in SMALL_SHARDED]
    small_pack = _pack_rows([w[n] for n in SMALL_SHARDED], 128)
    gathered = _all_gather_chips([_shard2d(n, w[n]).astype(bf16) for n in big] + [small_pack], "all_gather_weights")
    fw = {n: _full_from_slots(n, gathered[i]) for i, n in enumerate(big)}
    per_chip = [_unpack_rows(gathered[-1][k], small_shard_shapes) for k in range(N_CHIPS)]
    for i, n in enumerate(SMALL_SHARDED):
        fw[n] = jnp.concatenate([per_chip[k][i] for k in range(N_CHIPS)], axis=-1)
    for n in WEIGHTS:
        if n not in fw:
            fw[n] = w[n]

    loss, dx, g = _local_step(x[0], loss_target[0], _layout_weights(fw))
    gf = _unlayout_grads(g)
    loss = lax.psum(loss[0, 0], ("x", "y", "c"))

    slots = [_slots_from_full(n, gf[n]) for n in big]
    from_sibling = _sibling_send_other_half(slots, "rs_sibling_halves")
    partial = [_add_own_half(slots[i], from_sibling[i], c_arr, "rs_add_sibling") for i in range(len(big))]
    from_chips = _chip_exchange(partial, "rs_chip_exchange")
    summed = [_sum_slots(t, "rs_sum_chips") for t in from_chips]
    g_shard = dict(zip(big, _sibling_all_gather(summed, "rs_sibling_gather")))

    small_full_shapes = [FULL_SHAPES[n] for n in SMALL]
    small_sum = _all_reduce_small(_pack_rows([gf[n] for n in SMALL], 664), "all_reduce_small")
    g_small = dict(zip(SMALL, _unpack_rows(small_sum, small_full_shapes)))
    for n in SMALL_SHARDED:
        width = w[n].shape[-1]
        g_small[n] = lax.dynamic_slice_in_dim(g_small[n], chip * width, width, axis=-1)

    out_g, out_d, out_m, out_v = {}, {}, {}, {}
    for n in big:
        shp = w[n].shape
        res = _adamw_big(_shard2d(n, w[n]), g_shard[n], _shard2d(n, m[n]), _shard2d(n, v[n]), "adamw_" + n)
        out_g[n], out_d[n], out_m[n], out_v[n] = (t.reshape(shp) for t in res)
    as2d = lambda t: t.reshape(-1, t.shape[-1])
    sg, sd, sm, sv = _adamw_small([as2d(w[n]) for n in SMALL], [as2d(g_small[n]) for n in SMALL],
                                  [as2d(m[n]) for n in SMALL], [as2d(v[n]) for n in SMALL], "adamw_small")
    for i, n in enumerate(SMALL):
        out_g[n], out_d[n], out_m[n], out_v[n] = (t[i].reshape(w[n].shape) for t in (sg, sd, sm, sv))
    return (loss, dx[None], *[out_g[n] for n in WEIGHTS], *[out_d[n] for n in WEIGHTS],
            *[out_m[n] for n in WEIGHTS], *[out_v[n] for n in WEIGHTS])
```

```python
import functools

import jax
import jax.numpy as jnp
from jax import lax
from jax.experimental import pallas as pl
from jax.experimental.pallas import tpu as pltpu

f32 = jnp.float32
bf16 = jnp.bfloat16
HI = lax.Precision.HIGHEST

D = 1024
EPS = 1e-6
F_FLOOR = 1e-30
GDN_CHUNK = 64
GDN_INTRA_CHUNKS = 1
HGRN_CHUNK = 16
HGRN_STEP = 128
HEAD = 128
NH = 4
LRU_BLOCK = 256
D_FF = 2816
RG_C = 8.0
C_QKV, C_Z, C_QB, C_FB, C_IB, C_GB, C_BA = 0, 1536, 2048, 2560, 3072, 3584, 4096
AB_COLS_PAD = 4224
TT = 256
VMEM_LIMIT = 56 * 1024 * 1024

ADAM_LR, ADAM_B1, ADAM_B2, ADAM_EPS, ADAM_WD, ADAM_STEP = 0.001, 0.9, 0.999, 1e-08, 0.01, 10


def _cp(**kw):
    return pltpu.CompilerParams(vmem_limit_bytes=VMEM_LIMIT, **kw)


def _sds(shape, dtype):
    return jax.ShapeDtypeStruct(shape, dtype)


def _dot(a, b, dims, precision=None):
    return lax.dot_general(a, b, (dims, ((), ())), precision=precision, preferred_element_type=f32)


NN = ((1,), (0,))
NT = ((1,), (1,))
TN = ((0,), (0,))


def _rms(x, g):
    return x * lax.rsqrt(jnp.mean(x * x, axis=-1, keepdims=True) + EPS) * g


def _silu(x):
    return x * jax.nn.sigmoid(x)


def _mm(a, b, mode, tm, tn, out_dtype, name):
    if mode == "nn":
        (m, k), n = a.shape, b.shape[1]
        a_spec = pl.BlockSpec((tm, k), lambda i, j: (i, 0))
        b_spec = pl.BlockSpec((k, tn), lambda i, j: (0, j))
        dims = NN
    elif mode == "nt":
        (m, k), n = a.shape, b.shape[0]
        a_spec = pl.BlockSpec((tm, k), lambda i, j: (i, 0))
        b_spec = pl.BlockSpec((tn, k), lambda i, j: (j, 0))
        dims = NT
    else:
        (k, m), n = a.shape, b.shape[1]
        a_spec = pl.BlockSpec((k, tm), lambda i, j: (0, i))
        b_spec = pl.BlockSpec((k, tn), lambda i, j: (0, j))
        dims = TN
    assert m % tm == 0 and n % tn == 0, (name, m, n, tm, tn)

    def body(a_ref, b_ref, o_ref):
        o_ref[...] = _dot(a_ref[...], b_ref[...], dims).astype(out_dtype)

    return pl.pallas_call(
        body, grid=(m // tm, n // tn), in_specs=[a_spec, b_spec],
        out_specs=pl.BlockSpec((tm, tn), lambda i, j: (i, j)),
        out_shape=_sds((m, n), out_dtype), compiler_params=_cp(), name=name)(a, b)


def _tile(n, cands):
    for c in cands:
        if n % c == 0:
            return c
    raise ValueError(n)


def _mm_auto(a, b, mode, out_dtype, name):
    m = a.shape[1] if mode == "tn" else a.shape[0]
    n = b.shape[0] if mode == "nt" else b.shape[1]
    return _mm(a, b, mode, _tile(m, (512, 256, 128)), _tile(n, (1024, 1408, 512, 384, 256, 128)), out_dtype, name)


def _rmsnorm_fwd(x, gain, name):
    t = x.shape[0]

    def body(x_ref, g_ref, h_ref):
        h_ref[...] = _rms(x_ref[...], g_ref[...]).astype(bf16)

    return pl.pallas_call(
        body, grid=(t // TT,),
        in_specs=[pl.BlockSpec((TT, D), lambda i: (i, 0)), pl.BlockSpec((1, D), lambda i: (0, 0))],
        out_specs=pl.BlockSpec((TT, D), lambda i: (i, 0)),
        out_shape=_sds((t, D), bf16), compiler_params=_cp(), name=name)(x, gain)


def _rmsnorm_bwd(x, gain, dh, dres, name):
    t = x.shape[0]

    def body(x_ref, g_ref, dh_ref, dres_ref, dx_ref, dxb_ref, dg_ref):
        _, vjp = jax.vjp(_rms, x_ref[...], g_ref[...])
        dx, dg = vjp(dh_ref[...])
        dx = dx + dres_ref[...]
        dx_ref[...] = dx
        dxb_ref[...] = dx.astype(bf16)

        @pl.when(pl.program_id(0) == 0)
        def _():
            dg_ref[...] = jnp.zeros_like(dg_ref)

        dg_ref[...] += dg

    row = pl.BlockSpec((TT, D), lambda i: (i, 0))
    vec = pl.BlockSpec((1, D), lambda i: (0, 0))
    return pl.pallas_call(
        body, grid=(t // TT,), in_specs=[row, vec, row, row], out_specs=[row, row, vec],
        out_shape=[_sds((t, D), f32), _sds((t, D), bf16), _sds((1, D), f32)],
        compiler_params=_cp(), name=name)(x, gain, dh, dres)


def _residual_add(x, y, name):
    t = x.shape[0]

    def body(x_ref, y_ref, o_ref):
        o_ref[...] = x_ref[...] + y_ref[...]

    row = pl.BlockSpec((TT, D), lambda i: (i, 0))
    return pl.pallas_call(body, grid=(t // TT,), in_specs=[row, row], out_specs=row,
                          out_shape=_sds((t, D), f32), compiler_params=_cp(), name=name)(x, y)


def _final_loss(x, gain, target, name):
    t = x.shape[0]

    def loss_fn(xv, gv, tv):
        e = _rms(xv, gv) - tv
        return 0.5 * jnp.sum(jnp.mean(e * e, axis=-1))

    def body(x_ref, g_ref, t_ref, loss_ref, dx_ref, dxb_ref, dg_ref):
        val, (dx, dg) = jax.value_and_grad(loss_fn, argnums=(0, 1))(x_ref[...], g_ref[...], t_ref[...])

        @pl.when(pl.program_id(0) == 0)
        def _():
            dg_ref[...] = jnp.zeros_like(dg_ref)
            loss_ref[...] = jnp.zeros_like(loss_ref)

        dx_ref[...] = dx
        dxb_ref[...] = dx.astype(bf16)
        dg_ref[...] += dg
        loss_ref[...] += jnp.full((1, 128), val, f32)

    row = pl.BlockSpec((TT, D), lambda i: (i, 0))
    vec = pl.BlockSpec((1, D), lambda i: (0, 0))
    return pl.pallas_call(
        body, grid=(t // TT,), in_specs=[row, vec, row],
        out_specs=[pl.BlockSpec((1, 128), lambda i: (0, 0)), row, row, vec],
        out_shape=[_sds((1, 128), f32), _sds((t, D), f32), _sds((t, D), bf16), _sds((1, D), f32)],
        compiler_params=_cp(), name=name)(x, gain, target)


def _conv_fwd(x, col0, c, w, b, name, tc=256, val=None, val_col0=0):
    t = x.shape[0]
    width = w.shape[0]
    nt = t // TT
    hb = TT // 8

    def body(*refs):
        if val is None:
            x_ref, xh_ref, w_ref, b_ref, o_ref, xp = refs
        else:
            x_ref, xh_ref, w_ref, b_ref, v_ref, o_ref, act_ref, xp = refs
        i = pl.program_id(1)
        xp[0:8, :] = jnp.where(i == 0, 0.0, xh_ref[...])
        xp[8:, :] = x_ref[...]
        acc = jnp.zeros((TT, tc), f32) + b_ref[...]
        for k in range(width):
            acc = acc + w_ref[k:k + 1, :] * xp[pl.ds(8 - (width - 1) + k, TT), :]
        o_ref[...] = acc
        if val is not None:
            act_ref[...] = (_silu(acc) * v_ref[...]).astype(bf16)

    in_specs = [
        pl.BlockSpec((TT, tc), lambda j, i: (i, j + col0)),
        pl.BlockSpec((8, tc), lambda j, i: (jnp.maximum(i * hb - 1, 0), j + col0)),
        pl.BlockSpec((width, tc), lambda j, i: (0, j)),
        pl.BlockSpec((1, tc), lambda j, i: (0, j)),
    ]
    args = [x, x, w, b]
    out_specs = [pl.BlockSpec((TT, tc), lambda j, i: (i, j))]
    out_shape = [_sds((t, c), f32)]
    if val is not None:
        in_specs.append(pl.BlockSpec((TT, tc), lambda j, i: (i, j + val_col0)))
        args.append(val)
        out_specs.append(pl.BlockSpec((TT, tc), lambda j, i: (i, j)))
        out_shape.append(_sds((t, c), bf16))
    res = pl.pallas_call(
        body, grid=(c // tc, nt), in_specs=in_specs, out_specs=out_specs, out_shape=out_shape,
        scratch_shapes=[pltpu.VMEM((TT + 8, tc), f32)], compiler_params=_cp(), name=name)(*args)
    return res[0] if val is None else res


def _conv_bwd(dc, x, col0, w, name, tc=256, dx_dtype=bf16):
    t, c = dc.shape
    width = w.shape[0]
    nt = t // TT
    hb = TT // 8

    def body(dc_ref, dcn_ref, x_ref, xh_ref, w_ref, dx_ref, dw_ref, db_ref, dcp, xp):
        i = pl.program_id(1)
        dcv = dc_ref[...]
        dcp[0:TT, :] = dcv
        dcp[TT:, :] = jnp.where(i == nt - 1, 0.0, dcn_ref[...])
        xp[0:8, :] = jnp.where(i == 0, 0.0, xh_ref[...])
        xp[8:, :] = x_ref[...]

        @pl.when(i == 0)
        def _():
            dw_ref[...] = jnp.zeros_like(dw_ref)
            db_ref[...] = jnp.zeros_like(db_ref)

        acc = jnp.zeros((TT, tc), f32)
        for k in range(width):
            acc = acc + w_ref[k:k + 1, :] * dcp[pl.ds((width - 1) - k, TT), :]
            dw_ref[k:k + 1, :] += jnp.sum(dcv * xp[pl.ds(8 - (width - 1) + k, TT), :], axis=0, keepdims=True)
        dx_ref[...] = acc.astype(dx_dtype)
        db_ref[...] += jnp.sum(dcv, axis=0, keepdims=True)

    in_specs = [
        pl.BlockSpec((TT, tc), lambda j, i: (i, j)),
        pl.BlockSpec((8, tc), lambda j, i: (jnp.minimum((i + 1) * hb, t // 8 - 1), j)),
        pl.BlockSpec((TT, tc), lambda j, i: (i, j + col0)),
        pl.BlockSpec((8, tc), lambda j, i: (jnp.maximum(i * hb - 1, 0), j + col0)),
        pl.BlockSpec((width, tc), lambda j, i: (0, j)),
    ]
    out_specs = [
        pl.BlockSpec((TT, tc), lambda j, i: (i, j)),
        pl.BlockSpec((8, tc), lambda j, i: (0, j)),
        pl.BlockSpec((1, tc), lambda j, i: (0, j)),
    ]
    return pl.pallas_call(
        body, grid=(c // tc, nt), in_specs=in_specs, out_specs=out_specs,
        out_shape=[_sds((t, c), dx_dtype), _sds((8, c), f32), _sds((1, c), f32)],
        scratch_shapes=[pltpu.VMEM((TT + 8, tc), f32), pltpu.VMEM((TT + 8, tc), f32)],
        compiler_params=_cp(), name=name)(dc, dc, x, x, w)


def _gdn_chunk(cq, ck, cv, ba, z, s, alog_v, dtb_v, gain, hd):
    c = GDN_CHUNK
    lane = lax.broadcasted_iota(jnp.int32, (1, HEAD), 1)
    mb = (lane == hd).astype(f32)
    ma = (lane == hd + NH).astype(f32)
    beta = jax.nn.sigmoid(jnp.sum(ba * mb, axis=1, keepdims=True))
    alpha = jnp.sum(ba * ma, axis=1, keepdims=True)
    alog = jnp.sum(alog_v * ma, axis=1, keepdims=True)
    dtb = jnp.sum(dtb_v * ma, axis=1, keepdims=True)
    g = -jnp.exp(alog) * jax.nn.softplus(alpha + dtb)
    q = _silu(cq)
    q = q * lax.rsqrt(jnp.sum(q * q, axis=-1, keepdims=True) + EPS) * (HEAD ** -0.5)
    k = _silu(ck)
    k = k * lax.rsqrt(jnp.sum(k * k, axis=-1, keepdims=True) + EPS)
    v = _silu(cv)
    row = lax.broadcasted_iota(jnp.int32, (c, c), 0)
    col = lax.broadcasted_iota(jnp.int32, (c, c), 1)
    causal = row >= col
    tril = causal.astype(f32)
    gc = _dot(tril, jnp.broadcast_to(g, (c, HEAD)), NN, HI)
    gcc = _dot(tril, jnp.broadcast_to(g, (c, c)), NN, HI)
    diff = jnp.where(causal, gcc - gcc.T, 0.0)
    decay = jnp.where(causal, jnp.exp(diff), 0.0)
    kb = k * beta
    nmat = -jnp.where(row > col, _dot(kb, k, NT) * decay, 0.0)
    egc = jnp.exp(gc)
    u = v * beta
    w = kb * egc
    p = nmat
    for it in range(6):
        u = u + _dot(p, u, NN, HI)
        w = w + _dot(p, w, NN, HI)
        if it < 5:
            p = _dot(p, p, NN, HI)
    attn = _dot(q, k, NT) * decay
    rowv = lax.broadcasted_iota(jnp.int32, (c, 1), 0)
    gc_last = jnp.sum(jnp.where(rowv == c - 1, gc, 0.0), axis=0, keepdims=True)
    q_dec = q * egc
    k_dec = k * jnp.exp(gc_last - gc)
    v_new = u - _dot(w, s, NN)
    o = _dot(q_dec, s, NN) + _dot(attn, v_new, NN)
    s_new = s * jnp.exp(gc_last) + _dot(k_dec, v_new, TN)
    o = _rms(o, gain) * _silu(z)
    return o, s_new


def _gdn_specs(nch):
    c = GDN_CHUNK
    blk = lambda off: pl.BlockSpec((c, HEAD), lambda n, h: (n, off + h))
    vec = pl.BlockSpec((1, HEAD), lambda n, h: (0, 0))
    return blk, vec


def _gdn_fwd(conv, p, alog_v, dtb_v, gain, name):
    t = conv.shape[0]
    c = GDN_CHUNK
    nch = t // c
    blk, vec = _gdn_specs(nch)

    def body(cq, ck, cv, ba, z, al, dt, gn, o_ref, ss_ref, s_scr):
        n, h = pl.program_id(0), pl.program_id(1)

        @pl.when(n == 0)
        def _():
            s_scr[h] = jnp.zeros((HEAD, HEAD), f32)

        s = s_scr[h]
        ss_ref[0, 0] = s
        o, s_new = _gdn_chunk(cq[...], ck[...], cv[...], ba[...], z[...], s, al[...], dt[...], gn[...], h)
        o_ref[...] = o.astype(bf16)
        s_scr[h] = s_new

    return pl.pallas_call(
        body, grid=(nch, NH),
        in_specs=[blk(0), blk(4), blk(8),
                  pl.BlockSpec((c, HEAD), lambda n, h: (n, C_BA // HEAD)),
                  blk(C_Z // HEAD), vec, vec, vec],
        out_specs=[pl.BlockSpec((c, HEAD), lambda n, h: (n, h)),
                   pl.BlockSpec((1, 1, HEAD, HEAD), lambda n, h: (n, h, 0, 0))],
        out_shape=[_sds((t, NH * HEAD), bf16), _sds((nch, NH, HEAD, HEAD), f32)],
        scratch_shapes=[pltpu.VMEM((NH, HEAD, HEAD), f32)],
        compiler_params=_cp(), name=name)(conv, conv, conv, p, p, alog_v, dtb_v, gain)


def _gdn_bwd(conv, p, alog_v, dtb_v, gain, states, do, name):
    t = conv.shape[0]
    c = GDN_CHUNK
    nch = t // c
    rblk = lambda off: pl.BlockSpec((c, HEAD), lambda n, h: (nch - 1 - n, off + h))
    vec = pl.BlockSpec((1, HEAD), lambda n, h: (0, 0))

    def body(cq, ck, cv, ba, z, al, dt, gn, ss_ref, do_ref,
             dq_ref, dk_ref, dv_ref, dz_ref, dba_ref, dal_ref, ddt_ref, dgn_ref, ds_scr):
        n, h = pl.program_id(0), pl.program_id(1)

        @pl.when(n == 0)
        def _():
            ds_scr[h] = jnp.zeros((HEAD, HEAD), f32)

        @pl.when((n == 0) & (h == 0))
        def _():
            dal_ref[...] = jnp.zeros_like(dal_ref)
            ddt_ref[...] = jnp.zeros_like(ddt_ref)
            dgn_ref[...] = jnp.zeros_like(dgn_ref)

        @pl.when(h == 0)
        def _():
            dba_ref[...] = jnp.zeros_like(dba_ref)

        fn = functools.partial(_gdn_chunk, hd=h)
        _, vjp = jax.vjp(fn, cq[...], ck[...], cv[...], ba[...], z[...], ss_ref[0, 0], al[...], dt[...], gn[...])
        dq, dk, dv, dba, dz, ds, dal, ddt, dgn = vjp((do_ref[...], ds_scr[h]))
        dq_ref[...] = dq
        dk_ref[...] = dk
        dv_ref[...] = dv
        dz_ref[...] = dz
        dba_ref[...] += dba
        dal_ref[...] += dal
        ddt_ref[...] += ddt
        dgn_ref[...] += dgn
        ds_scr[h] = ds

    hblk = pl.BlockSpec((c, HEAD), lambda n, h: (nch - 1 - n, h))
    return pl.pallas_call(
        body, grid=(nch, NH),
        in_specs=[rblk(0), rblk(4), rblk(8),
                  pl.BlockSpec((c, HEAD), lambda n, h: (nch - 1 - n, C_BA // HEAD)),
                  rblk(C_Z // HEAD), vec, vec, vec,
                  pl.BlockSpec((1, 1, HEAD, HEAD), lambda n, h: (nch - 1 - n, h, 0, 0)),
                  hblk],
        out_specs=[hblk, hblk, hblk, hblk,
                   pl.BlockSpec((c, HEAD), lambda n, h: (nch - 1 - n, 0)), vec, vec, vec],
        out_shape=[_sds((t, 512), f32)] * 4 + [_sds((t, HEAD), f32)] + [_sds((1, HEAD), f32)] * 3,
        scratch_shapes=[pltpu.VMEM((NH, HEAD, HEAD), f32)],
        compiler_params=_cp(), name=name)(conv, conv, conv, p, p, alog_v, dtb_v, gain, states, do)


def _bdot_impl(a, b, dims):
    return _dot(a.astype(bf16), b.astype(bf16), dims)


@functools.partial(jax.custom_vjp, nondiff_argnums=(2,))
def _bdot(a, b, dims):
    return _bdot_impl(a, b, dims)


def _bdot_fwd(a, b, dims):
    return _bdot_impl(a, b, dims), (a, b)


def _bdot_bwd(dims, res, ct):
    a, b = res
    if dims == NN:
        return _bdot_impl(ct, b, NT), _bdot_impl(a, ct, TN)
    if dims == NT:
        return _bdot_impl(ct, b, NN), _bdot_impl(ct, a, TN)
    return _bdot_impl(b, ct, NT), _bdot_impl(a, ct, NN)


_bdot.defvjp(_bdot_fwd, _bdot_bwd)


def _split2(a):
    hi = a.astype(bf16)
    return hi, (a - hi.astype(f32)).astype(bf16)


def _dot3_impl(a, b, dims):
    a_hi, a_lo = _split2(a)
    b_hi, b_lo = _split2(b)
    return (_dot(a_hi, b_hi, dims) + _dot(a_hi, b_lo, dims)) + _dot(a_lo, b_hi, dims)


@functools.partial(jax.custom_vjp, nondiff_argnums=(2,))
def _dot3(a, b, dims):
    return _dot3_impl(a, b, dims)


def _dot3_fwd(a, b, dims):
    return _dot3_impl(a, b, dims), (a, b)


def _dot3_bwd(dims, res, ct):
    a, b = res
    if dims == NN:
        return _dot3_impl(ct, b, NT), _dot3_impl(a, ct, TN)
    if dims == NT:
        return _dot3_impl(ct, b, NN), _dot3_impl(ct, a, TN)
    return _dot3_impl(b, ct, NT), _dot3_impl(a, ct, NN)


_dot3.defvjp(_dot3_fwd, _dot3_bwd)


def _tril_dot_impl(tril, x, dims):
    t = tril.astype(bf16)
    x1 = x.astype(bf16)
    r1 = x - x1.astype(f32)
    x2 = r1.astype(bf16)
    x3 = (r1 - x2.astype(f32)).astype(bf16)
    return (_dot(t, x3, dims) + _dot(t, x2, dims)) + _dot(t, x1, dims)


@jax.custom_vjp
def _cumsum_rows(tril, x):
    return _tril_dot_impl(tril, x, NN)


def _cumsum_rows_fwd(tril, x):
    return _tril_dot_impl(tril, x, NN), tril


def _cumsum_rows_bwd(tril, ct):
    return jnp.zeros_like(tril), _tril_dot_impl(tril, ct, TN)


_cumsum_rows.defvjp(_cumsum_rows_fwd, _cumsum_rows_bwd)


def _gdn_intra(cq, ck, cv, ba, alog_v, dtb_v, hd):
    c = GDN_CHUNK
    lane = lax.broadcasted_iota(jnp.int32, (1, HEAD), 1)
    mb = (lane == hd).astype(f32)
    ma = (lane == hd + NH).astype(f32)
    beta = jax.nn.sigmoid(jnp.sum(ba * mb, axis=1, keepdims=True))
    alpha = jnp.sum(ba * ma, axis=1, keepdims=True)
    alog = jnp.sum(alog_v * ma, axis=1, keepdims=True)
    dtb = jnp.sum(dtb_v * ma, axis=1, keepdims=True)
    g = -jnp.exp(alog) * jax.nn.softplus(alpha + dtb)
    q = _silu(cq)
    q = q * lax.rsqrt(jnp.sum(q * q, axis=-1, keepdims=True) + EPS) * (HEAD ** -0.5)
    k = _silu(ck)
    k = k * lax.rsqrt(jnp.sum(k * k, axis=-1, keepdims=True) + EPS)
    v = _silu(cv)
    row = lax.broadcasted_iota(jnp.int32, (c, c), 0)
    col = lax.broadcasted_iota(jnp.int32, (c, c), 1)
    causal = row >= col
    tril = causal.astype(f32)
    gc = _cumsum_rows(tril, jnp.broadcast_to(g, (c, HEAD)))
    gcc = _cumsum_rows(tril, jnp.broadcast_to(g, (c, c)))
    diff = jnp.where(causal, gcc - gcc.T, 0.0)
    decay = jnp.where(causal, jnp.exp(diff), 0.0)
    kb = k * beta
    nmat = -jnp.where(row > col, _bdot(kb, k, NT) * decay, 0.0)
    egc = jnp.exp(gc)
    sol = jnp.concatenate([v * beta, kb * egc], axis=1)
    p = nmat
    for it in range(6):
        sol = sol + _dot3(p, sol, NN)
        if it < 5:
            p = _dot3(p, p, NN)
    u, w = sol[:, :HEAD], sol[:, HEAD:]
    attn = _bdot(q, k, NT) * decay
    rowv = lax.broadcasted_iota(jnp.int32, (c, 1), 0)
    gc_last = jnp.sum(jnp.where(rowv == c - 1, gc, 0.0), axis=0, keepdims=True)
    return u, w, q * egc, k * jnp.exp(gc_last - gc), attn, jnp.exp(gc_last)


def _gdn_seq(u, w, q_dec, k_dec, attn, dl, z, s, gain):
    v_new = u - _bdot(w, s, NN)
    o = _bdot(q_dec, s, NN) + _bdot(attn, v_new, NN)
    s_new = s * dl + _bdot(k_dec, v_new, TN)
    return _rms(o, gain) * _silu(z), s_new


def _hsl(h):
    return slice(h * HEAD, (h + 1) * HEAD)


def _gdn2_fwd(conv, p, alog_v, dtb_v, gain, name):
    t = conv.shape[0]
    c = GDN_CHUNK
    nch = t // c
    w512 = NH * HEAD
    wide = lambda off: pl.BlockSpec((c, w512), lambda n: (n, off))
    vec = pl.BlockSpec((1, HEAD), lambda n: (0, 0))
    attn_spec = pl.BlockSpec((1, NH, c, c), lambda n: (n, 0, 0, 0))
    dl_spec = pl.BlockSpec((1, NH, HEAD), lambda n: (n, 0, 0))

    cps = GDN_INTRA_CHUNKS
    iwide = lambda off: pl.BlockSpec((cps * c, w512), lambda n: (n, off))

    def intra(cq, ck, cv, ba, al, dt, u_ref, w_ref, qd_ref, kd_ref, at_ref, dl_ref):
        for ci in range(cps):
            rows = slice(ci * c, (ci + 1) * c)
            for h in range(NH):
                u, w, qd, kd, at, dl = _gdn_intra(cq[rows, _hsl(h)], ck[rows, _hsl(h)], cv[rows, _hsl(h)],
                                                  ba[rows, :], al[...], dt[...], h)
                u_ref[rows, _hsl(h)] = u
                w_ref[rows, _hsl(h)] = w
                qd_ref[rows, _hsl(h)] = qd
                kd_ref[rows, _hsl(h)] = kd
                at_ref[ci, h] = at
                dl_ref[ci, h:h + 1, :] = dl

    u, w, qd, kd, at, dl = pl.pallas_call(
        intra, grid=(nch // cps,),
        in_specs=[iwide(0), iwide(1), iwide(2), pl.BlockSpec((cps * c, HEAD), lambda n: (n, C_BA // HEAD)), vec, vec],
        out_specs=[iwide(0)] * 4 + [pl.BlockSpec((cps, NH, c, c), lambda n: (n, 0, 0, 0)),
                                    pl.BlockSpec((cps, NH, HEAD), lambda n: (n, 0, 0))],
        out_shape=[_sds((t, w512), f32)] * 4 + [_sds((nch, NH, c, c), f32), _sds((nch, NH, HEAD), f32)],
        compiler_params=_cp(), name=name + "_intra")(conv, conv, conv, p, alog_v, dtb_v)

    def seq(u_ref, w_ref, qd_ref, kd_ref, at_ref, dl_ref, z_ref, gn, o_ref, ss_ref, s_scr):
        @pl.when(pl.program_id(0) == 0)
        def _():
            s_scr[...] = jnp.zeros_like(s_scr)

        for h in range(NH):
            s = s_scr[h]
            ss_ref[0, h] = s
            o, s_new = _gdn_seq(u_ref[:, _hsl(h)], w_ref[:, _hsl(h)], qd_ref[:, _hsl(h)], kd_ref[:, _hsl(h)],
                                at_ref[0, h], dl_ref[0, h:h + 1, :], z_ref[:, _hsl(h)], s, gn[...])
            o_ref[:, _hsl(h)] = o.astype(bf16)
            s_scr[h] = s_new

    o, states = pl.pallas_call(
        seq, grid=(nch,),
        in_specs=[wide(0)] * 4 + [attn_spec, dl_spec, wide(C_Z // w512), vec],
        out_specs=[wide(0), pl.BlockSpec((1, NH, HEAD, HEAD), lambda n: (n, 0, 0, 0))],
        out_shape=[_sds((t, w512), bf16), _sds((nch, NH, HEAD, HEAD), f32)],
        scratch_shapes=[pltpu.VMEM((NH, HEAD, HEAD), f32)],
        compiler_params=_cp(), name=name + "_seq")(u, w, qd, kd, at, dl, p, gain)
    return o, dict(u=u, w=w, qd=qd, kd=kd, at=at, dl=dl, states=states)


def _gdn2_bwd(conv, p, alog_v, dtb_v, gain, saved, do, name):
    t = conv.shape[0]
    c = GDN_CHUNK
    nch = t // c
    w512 = NH * HEAD
    rwide = lambda off: pl.BlockSpec((c, w512), lambda n: (nch - 1 - n, off))
    rvec = pl.BlockSpec((1, HEAD), lambda n: (0, 0))
    rattn = pl.BlockSpec((1, NH, c, c), lambda n: (nch - 1 - n, 0, 0, 0))
    rdl = pl.BlockSpec((1, NH, HEAD), lambda n: (nch - 1 - n, 0, 0))

    def seq_bwd(u_ref, w_ref, qd_ref, kd_ref, at_ref, dl_ref, z_ref, gn, ss_ref, do_ref,
                du_ref, dw_ref, dqd_ref, dkd_ref, dat_ref, ddl_ref, dz_ref, dgn_ref, ds_scr):
        @pl.when(pl.program_id(0) == 0)
        def _():
            ds_scr[...] = jnp.zeros_like(ds_scr)
            dgn_ref[...] = jnp.zeros_like(dgn_ref)

        dgn = jnp.zeros((1, HEAD), f32)
        for h in range(NH):
            _, vjp = jax.vjp(_gdn_seq, u_ref[:, _hsl(h)], w_ref[:, _hsl(h)], qd_ref[:, _hsl(h)], kd_ref[:, _hsl(h)],
                             at_ref[0, h], dl_ref[0, h:h + 1, :], z_ref[:, _hsl(h)], ss_ref[0, h], gn[...])
            du, dw, dqd, dkd, dat, ddl, dz, ds, dg = vjp((do_ref[:, _hsl(h)], ds_scr[h]))
            du_ref[:, _hsl(h)] = du
            dw_ref[:, _hsl(h)] = dw
            dqd_ref[:, _hsl(h)] = dqd
            dkd_ref[:, _hsl(h)] = dkd
            dat_ref[0, h] = dat
            ddl_ref[0, h:h + 1, :] = ddl
            dz_ref[:, _hsl(h)] = dz
            ds_scr[h] = ds
            dgn = dgn + dg
        dgn_ref[...] += dgn

    du, dw, dqd, dkd, dat, ddl, dz, dgn = pl.pallas_call(
        seq_bwd, grid=(nch,),
        in_specs=[rwide(0)] * 4 + [rattn, rdl, rwide(C_Z // w512), rvec,
                                   pl.BlockSpec((1, NH, HEAD, HEAD), lambda n: (nch - 1 - n, 0, 0, 0)), rwide(0)],
        out_specs=[rwide(0)] * 4 + [rattn, rdl, rwide(0), rvec],
        out_shape=[_sds((t, w512), f32)] * 4 + [_sds((nch, NH, c, c), f32), _sds((nch, NH, HEAD), f32),
                                                _sds((t, w512), f32), _sds((1, HEAD), f32)],
        scratch_shapes=[pltpu.VMEM((NH, HEAD, HEAD), f32)],
        compiler_params=_cp(), name=name + "_seq")(
            saved["u"], saved["w"], saved["qd"], saved["kd"], saved["at"], saved["dl"], p, gain, saved["states"], do)

    cps = GDN_INTRA_CHUNKS
    wide = lambda off: pl.BlockSpec((cps * c, w512), lambda n: (n, off))
    vec = pl.BlockSpec((1, HEAD), lambda n: (0, 0))
    attn_spec = pl.BlockSpec((cps, NH, c, c), lambda n: (n, 0, 0, 0))
    dl_spec = pl.BlockSpec((cps, NH, HEAD), lambda n: (n, 0, 0))

    def intra_bwd(cq, ck, cv, ba, al, dt, du_ref, dw_ref, dqd_ref, dkd_ref, dat_ref, ddl_ref,
                  dc_ref, dba_ref, dal_ref, ddt_ref):
        @pl.when(pl.program_id(0) == 0)
        def _():
            dal_ref[...] = jnp.zeros_like(dal_ref)
            ddt_ref[...] = jnp.zeros_like(ddt_ref)

        dal = jnp.zeros((1, HEAD), f32)
        ddt = jnp.zeros((1, HEAD), f32)
        for ci in range(cps):
            rows = slice(ci * c, (ci + 1) * c)
            dba = jnp.zeros((c, HEAD), f32)
            for h in range(NH):
                fn = functools.partial(_gdn_intra, hd=h)
                _, vjp = jax.vjp(fn, cq[rows, _hsl(h)], ck[rows, _hsl(h)], cv[rows, _hsl(h)], ba[rows, :], al[...], dt[...])
                g_q, g_k, g_v, g_ba, g_al, g_dt = vjp((du_ref[rows, _hsl(h)], dw_ref[rows, _hsl(h)], dqd_ref[rows, _hsl(h)],
                                                       dkd_ref[rows, _hsl(h)], dat_ref[ci, h], ddl_ref[ci, h:h + 1, :]))
                dc_ref[rows, _hsl(h)] = g_q
                dc_ref[rows, _hsl(NH + h)] = g_k
                dc_ref[rows, _hsl(2 * NH + h)] = g_v
                dba = dba + g_ba
                dal = dal + g_al
                ddt = ddt + g_dt
            dba_ref[rows, :] = dba
        dal_ref[...] += dal
        ddt_ref[...] += ddt

    dconv, dba, dal, ddt = pl.pallas_call(
        intra_bwd, grid=(nch // cps,),
        in_specs=[wide(0), wide(1), wide(2), pl.BlockSpec((cps * c, HEAD), lambda n: (n, C_BA // HEAD)), vec, vec]
        + [wide(0)] * 4 + [attn_spec, dl_spec],
        out_specs=[pl.BlockSpec((cps * c, 3 * w512), lambda n: (n, 0)), pl.BlockSpec((cps * c, HEAD), lambda n: (n, 0)),
                   vec, vec],
        out_shape=[_sds((t, 3 * w512), f32), _sds((t, HEAD), f32), _sds((1, HEAD), f32), _sds((1, HEAD), f32)],
        compiler_params=_cp(), name=name + "_intra")(conv, conv, conv, p, alog_v, dtb_v, du, dw, dqd, dkd, dat, ddl)
    return dconv, dz, dba, dal, ddt, dgn


def _hgrn_chunk(qb, fb, ib, gb, st, lb, gain):
    c = HGRN_CHUNK
    f = lb + (1.0 - lb) * jax.nn.sigmoid(fb)
    logf = jnp.log(jnp.maximum(f, F_FLOOR))
    k = 1.0 - f
    q = _silu(qb)
    v = ib
    row = lax.broadcasted_iota(jnp.int32, (c, c), 0)
    col = lax.broadcasted_iota(jnp.int32, (c, c), 1)
    b = _cumsum_rows((row >= col).astype(f32), logf)
    ri = lax.broadcasted_iota(jnp.int32, (c, 1), 0)
    o = jnp.zeros((c, HEAD), f32)
    for j in range(c):
        mj = ri == j
        bj = jnp.sum(jnp.where(mj, b, 0.0), axis=0, keepdims=True)
        kj = jnp.sum(jnp.where(mj, k, 0.0), axis=0, keepdims=True)
        vj = jnp.sum(jnp.where(mj, v, 0.0), axis=0, keepdims=True)
        ok = ri >= j
        e = jnp.where(ok, jnp.exp(jnp.where(ok, b - bj, 0.0)), 0.0)
        o = o + jnp.sum(q * kj * e, axis=1, keepdims=True) * vj
    b_last = jnp.sum(jnp.where(ri == c - 1, b, 0.0), axis=0, keepdims=True)
    q_dec = q * jnp.exp(b)
    k_dec = k * jnp.exp(b_last - b)
    o = o + _bdot(q_dec, st, NT)
    st_new = st * jnp.exp(b_last) + _bdot(v, k_dec, TN)
    o = _rms(o, gain) * _silu(gb)
    return o, st_new


def _hgrn_fwd(p, lb, gain, name):
    t = p.shape[0]
    r = HGRN_STEP
    ns = t // r
    nsub = r // HGRN_CHUNK
    blk = lambda off: pl.BlockSpec((r, HEAD), lambda n, h: (n, off // HEAD + h))

    def body(qb, fb, ib, gb, lb_ref, gn, o_ref, ss_ref, s_scr):
        n, h = pl.program_id(0), pl.program_id(1)

        @pl.when(n == 0)
        def _():
            s_scr[h] = jnp.zeros((HEAD, HEAD), f32)

        st = s_scr[h]
        ss_ref[0, 0] = st
        for ch in range(nsub):
            rows = pl.ds(ch * HGRN_CHUNK, HGRN_CHUNK)
            o, st = _hgrn_chunk(qb[rows, :], fb[rows, :], ib[rows, :], gb[rows, :], st, lb_ref[...], gn[...])
            o_ref[rows, :] = o.astype(bf16)
        s_scr[h] = st

    return pl.pallas_call(
        body, grid=(ns, NH),
        in_specs=[blk(C_QB), blk(C_FB), blk(C_IB), blk(C_GB),
                  pl.BlockSpec((1, HEAD), lambda n, h: (0, h)), pl.BlockSpec((1, HEAD), lambda n, h: (0, 0))],
        out_specs=[pl.BlockSpec((r, HEAD), lambda n, h: (n, h)),
                   pl.BlockSpec((1, 1, HEAD, HEAD), lambda n, h: (n, h, 0, 0))],
        out_shape=[_sds((t, NH * HEAD), bf16), _sds((ns, NH, HEAD, HEAD), f32)],
        scratch_shapes=[pltpu.VMEM((NH, HEAD, HEAD), f32)],
        compiler_params=_cp(), name=name)(p, p, p, p, lb, gain)


def _hgrn_bwd(p, lb, gain, states, do, name):
    t = p.shape[0]
    r = HGRN_STEP
    ns = t // r
    nsub = r // HGRN_CHUNK
    blk = lambda off: pl.BlockSpec((r, HEAD), lambda n, h: (ns - 1 - n, off // HEAD + h))
    hblk = pl.BlockSpec((r, HEAD), lambda n, h: (ns - 1 - n, h))

    def body(qb, fb, ib, gb, lb_ref, gn, ss_ref, do_ref,
             dqb, dfb, dib, dgb, dlb_ref, dgn_ref, ds_scr, st_scr):
        n, h = pl.program_id(0), pl.program_id(1)

        @pl.when(n == 0)
        def _():
            ds_scr[h] = jnp.zeros((HEAD, HEAD), f32)

        @pl.when((n == 0) & (h == 0))
        def _():
            dgn_ref[...] = jnp.zeros_like(dgn_ref)

        lbv, gnv = lb_ref[...], gn[...]
        st = ss_ref[0, 0]
        for ch in range(nsub):
            rows = pl.ds(ch * HGRN_CHUNK, HGRN_CHUNK)
            st_scr[ch] = st
            if ch < nsub - 1:
                _, st = _hgrn_chunk(qb[rows, :], fb[rows, :], ib[rows, :], gb[rows, :], st, lbv, gnv)
        ds = ds_scr[h]
        dlb = jnp.zeros((1, HEAD), f32)
        dgn = jnp.zeros((1, HEAD), f32)
        for ch in reversed(range(nsub)):
            rows = pl.ds(ch * HGRN_CHUNK, HGRN_CHUNK)
            _, vjp = jax.vjp(_hgrn_chunk, qb[rows, :], fb[rows, :], ib[rows, :], gb[rows, :], st_scr[ch], lbv, gnv)
            g_q, g_f, g_i, g_g, ds, g_lb, g_gn = vjp((do_ref[rows, :], ds))
            dqb[rows, :] = g_q
            dfb[rows, :] = g_f
            dib[rows, :] = g_i
            dgb[rows, :] = g_g
            dlb = dlb + g_lb
            dgn = dgn + g_gn
        ds_scr[h] = ds
        dlb_ref[0] = dlb
        dgn_ref[...] += dgn

    return pl.pallas_call(
        body, grid=(ns, NH),
        in_specs=[blk(C_QB), blk(C_FB), blk(C_IB), blk(C_GB),
                  pl.BlockSpec((1, HEAD), lambda n, h: (0, h)), pl.BlockSpec((1, HEAD), lambda n, h: (0, 0)),
                  pl.BlockSpec((1, 1, HEAD, HEAD), lambda n, h: (ns - 1 - n, h, 0, 0)),
                  pl.BlockSpec((r, HEAD), lambda n, h: (ns - 1 - n, NH + h))],
        out_specs=[hblk, hblk, hblk, hblk,
                   pl.BlockSpec((1, 1, HEAD), lambda n, h: (n, 0, h)),
                   pl.BlockSpec((1, HEAD), lambda n, h: (0, 0))],
        out_shape=[_sds((t, 512), f32)] * 4 + [_sds((ns, 1, 512), f32), _sds((1, HEAD), f32)],
        scratch_shapes=[pltpu.VMEM((NH, HEAD, HEAD), f32), pltpu.VMEM((nsub, HEAD, HEAD), f32)],
        compiler_params=_cp(), name=name)(p, p, p, p, lb, gain, states, do)


def _lru_gates(xb, wa, wx, ba, bx, lam):
    xh = xb.astype(bf16)
    r = jax.nn.sigmoid(_dot(xh, wa.astype(bf16), NN) + ba)
    i = jax.nn.sigmoid(_dot(xh, wx.astype(bf16), NN) + bx)
    log_a = -RG_C * r * jax.nn.softplus(-lam)
    a = jnp.exp(log_a)
    t2 = 2.0 * log_a
    series = -t2 * (1.0 + t2 * (0.5 + t2 * (1.0 / 6.0 + t2 * (1.0 / 24.0))))
    om = jnp.where(t2 > -1e-2, series, 1.0 - jnp.exp(t2))
    u = jnp.sqrt(jnp.maximum(om, 0.0)) * (i * xb)
    return a, u


def _lru_gates_fwd(xc, wa, wx, ba, bx, lam, name):
    t = xc.shape[0]
    blk = pl.BlockSpec((TT, LRU_BLOCK), lambda h, i: (i, h))
    wsp = pl.BlockSpec((1, LRU_BLOCK, LRU_BLOCK), lambda h, i: (h, 0, 0))
    vsp = pl.BlockSpec((1, LRU_BLOCK), lambda h, i: (0, h))

    def body(x_ref, wa_ref, wx_ref, ba_ref, bx_ref, lam_ref, a_ref, u_ref):
        a, u = _lru_gates(x_ref[...], wa_ref[0], wx_ref[0], ba_ref[...], bx_ref[...], lam_ref[...])
        a_ref[...] = a
        u_ref[...] = u

    return pl.pallas_call(
        body, grid=(NH, t // TT), in_specs=[blk, wsp, wsp, vsp, vsp, vsp], out_specs=[blk, blk],
        out_shape=[_sds((t, D), f32)] * 2, compiler_params=_cp(), name=name)(xc, wa, wx, ba, bx, lam)


def _lru_gates_bwd(xc, wa, wx, ba, bx, lam, da, du, name):
    t = xc.shape[0]
    blk = pl.BlockSpec((TT, LRU_BLOCK), lambda h, i: (i, h))
    wsp = pl.BlockSpec((1, LRU_BLOCK, LRU_BLOCK), lambda h, i: (h, 0, 0))
    vsp = pl.BlockSpec((1, LRU_BLOCK), lambda h, i: (0, h))

    def body(x_ref, wa_ref, wx_ref, ba_ref, bx_ref, lam_ref, da_ref, du_ref,
             dx_ref, dwa_ref, dwx_ref, dba_ref, dbx_ref, dlam_ref):
        @pl.when(pl.program_id(1) == 0)
        def _():
            for r in (dwa_ref, dwx_ref, dba_ref, dbx_ref, dlam_ref):
                r[...] = jnp.zeros_like(r)

        _, vjp = jax.vjp(_lru_gates, x_ref[...], wa_ref[0], wx_ref[0], ba_ref[...], bx_ref[...], lam_ref[...])
        dx, dwa, dwx, dba, dbx, dlam = vjp((da_ref[...], du_ref[...]))
        dx_ref[...] = dx
        dwa_ref[0] += dwa
        dwx_ref[0] += dwx
        dba_ref[...] += dba
        dbx_ref[...] += dbx
        dlam_ref[...] += dlam

    return pl.pallas_call(
        body, grid=(NH, t // TT), in_specs=[blk, wsp, wsp, vsp, vsp, vsp, blk, blk],
        out_specs=[blk, wsp, wsp, vsp, vsp, vsp],
        out_shape=[_sds((t, D), f32), _sds((NH, LRU_BLOCK, LRU_BLOCK), f32), _sds((NH, LRU_BLOCK, LRU_BLOCK), f32),
                   _sds((1, D), f32), _sds((1, D), f32), _sds((1, D), f32)],
        compiler_params=_cp(), name=name)(xc, wa, wx, ba, bx, lam, da, du)


_SCAN_SHIFTS = (1, 2, 4, 8, 16, 32, 64, 128)
_SCAN_PAD = 128


def _gelu(y):
    return jax.nn.gelu(y, approximate=True)


def _lru_scan_fwd(a, u, p2, name):
    t = a.shape[0]
    tc = 128
    blk = pl.BlockSpec((TT, tc), lambda j, i: (i, j))

    def body(a_ref, u_ref, y_ref, h_ref, hg_ref, a_s, b_s, carry):
        i = pl.program_id(1)

        @pl.when(i == 0)
        def _():
            carry[...] = jnp.zeros_like(carry)
            a_s[0:_SCAN_PAD, :] = jnp.ones((_SCAN_PAD, tc), f32)
            b_s[0:_SCAN_PAD, :] = jnp.zeros((_SCAN_PAD, tc), f32)

        av, bv = a_ref[...], u_ref[...]
        for s in _SCAN_SHIFTS:
            a_s[_SCAN_PAD:, :] = av
            b_s[_SCAN_PAD:, :] = bv
            ash = a_s[pl.ds(_SCAN_PAD - s, TT), :]
            bsh = b_s[pl.ds(_SCAN_PAD - s, TT), :]
            bv = bv + av * bsh
            av = av * ash
        h = bv + av * carry[7:8, :]
        h_ref[...] = h
        hg_ref[...] = (h * _gelu(y_ref[...])).astype(bf16)
        carry[...] = h[TT - 8:, :]

    return pl.pallas_call(
        body, grid=(D // tc, t // TT), in_specs=[blk, blk, blk], out_specs=[blk, blk],
        out_shape=[_sds((t, D), f32), _sds((t, D), bf16)],
        scratch_shapes=[pltpu.VMEM((_SCAN_PAD + TT, tc), f32), pltpu.VMEM((_SCAN_PAD + TT, tc), f32),
                        pltpu.VMEM((8, tc), f32)],
        compiler_params=_cp(), name=name)(a, u, p2)


def _lru_scan_bwd(a, h, p2, dhg, name):
    t = a.shape[0]
    tc = 128
    nt = t // TT
    hb = TT // 8
    rblk = pl.BlockSpec((TT, tc), lambda j, i: (nt - 1 - i, j))

    def body(a_ref, an_ref, h_ref, hp_ref, y_ref, dhg_ref, du_ref, da_ref, dy_ref, a_s, b_s, ap, hp, carry):
        i = pl.program_id(1)

        @pl.when(i == 0)
        def _():
            carry[...] = jnp.zeros_like(carry)
            a_s[TT:, :] = jnp.ones((_SCAN_PAD, tc), f32)
            b_s[TT:, :] = jnp.zeros((_SCAN_PAD, tc), f32)

        ap[0:TT, :] = a_ref[...]
        ap[TT:, :] = jnp.where(i == 0, 0.0, an_ref[...])
        hp[0:8, :] = jnp.where(i == nt - 1, 0.0, hp_ref[...])
        hp[8:, :] = h_ref[...]
        y = y_ref[...]
        gate, gvjp = jax.vjp(_gelu, y)
        dhg_v = dhg_ref[...]
        dy_ref[...] = gvjp(dhg_v * h_ref[...])[0]
        av = ap[pl.ds(1, TT), :]
        bv = dhg_v * gate
        for s in _SCAN_SHIFTS:
            a_s[0:TT, :] = av
            b_s[0:TT, :] = bv
            ash = a_s[pl.ds(s, TT), :]
            bsh = b_s[pl.ds(s, TT), :]
            bv = bv + av * bsh
            av = av * ash
        g = bv + av * carry[0:1, :]
        du_ref[...] = g
        da_ref[...] = g * hp[pl.ds(7, TT), :]
        carry[...] = g[0:8, :]

    in_specs = [
        rblk,
        pl.BlockSpec((8, tc), lambda j, i: (jnp.minimum((nt - i) * hb, t // 8 - 1), j)),
        rblk,
        pl.BlockSpec((8, tc), lambda j, i: (jnp.maximum((nt - 1 - i) * hb - 1, 0), j)),
        rblk, rblk,
    ]
    return pl.pallas_call(
        body, grid=(D // tc, nt), in_specs=in_specs, out_specs=[rblk, rblk, rblk],
        out_shape=[_sds((t, D), f32)] * 3,
        scratch_shapes=[pltpu.VMEM((TT + _SCAN_PAD, tc), f32), pltpu.VMEM((TT + _SCAN_PAD, tc), f32),
                        pltpu.VMEM((TT + 8, tc), f32), pltpu.VMEM((TT + 8, tc), f32), pltpu.VMEM((8, tc), f32)],
        compiler_params=_cp(), name=name)(a, a, h, h, p2, dhg)


def _ffn_act_bwd(gc, up, dact, name, tc=256):
    t = gc.shape[0]
    blk = pl.BlockSpec((TT, tc), lambda i, j: (i, j))

    def body(gc_ref, v_ref, da_ref, dgc_ref, dv_ref):
        _, vjp = jax.vjp(lambda g, v: _silu(g) * v, gc_ref[...], v_ref[...])
        dg, dv = vjp(da_ref[...])
        dgc_ref[...] = dg
        dv_ref[...] = dv.astype(bf16)

    return pl.pallas_call(
        body, grid=(t // TT, D_FF // tc),
        in_specs=[blk, pl.BlockSpec((TT, tc), lambda i, j: (i, j + D_FF // tc)), blk], out_specs=[blk, blk],
        out_shape=[_sds((t, D_FF), f32), _sds((t, D_FF), bf16)], compiler_params=_cp(), name=name)(gc, up, dact)


def _lower_bounds_fwd(w):
    def body(w_ref, o0_ref, o1_ref):
        wv = w_ref[...]
        o0, o1 = _lb_rows(wv[0:1, :], wv[1:2, :])
        o0_ref[...] = o0
        o1_ref[...] = o1

    return pl.pallas_call(body, out_shape=[_sds((1, 512), f32)] * 2, name="lower_bounds_fwd")(w)


def _lb_rows(w0, w1):
    m = jnp.maximum(w0, w1)
    e0, e1 = jnp.exp(w0 - m), jnp.exp(w1 - m)
    s = e0 + e1
    p0, p1 = e0 / s, e1 / s
    return p0 - p0, (p0 + p1) - p0


def _lower_bounds_bwd(w, d0, d1):
    def body(w_ref, d0_ref, d1_ref, g0_ref, g1_ref):
        wv = w_ref[...]
        _, vjp = jax.vjp(_lb_rows, wv[0:1, :], wv[1:2, :])
        g0, g1 = vjp((d0_ref[...], d1_ref[...]))
        g0_ref[...] = g0
        g1_ref[...] = g1

    return pl.pallas_call(body, out_shape=[_sds((1, 512), f32)] * 2, name="lower_bounds_bwd")(w, d0, d1)


def _local_step(x, target, wt):
    depth = 4
    res = []
    lb0, lb1 = _lower_bounds_fwd(wt["hgrn_lower_bounds"])
    lbs = [lb0, lb1]
    for layer in range(depth):
        j = layer // 2
        sv = {"x_in": x}
        h1 = _rmsnorm_fwd(x, wt["norm_mix"][layer], "rms_fwd")
        sv["h1"] = h1
        if layer % 2 == 0:
            p = _mm_auto(h1, wt["ab_w_in"][j], "nn", f32, "mm_ab_in")
            conv = _conv_fwd(p, 0, 1536, wt["gdn_conv_w"][j], jnp.zeros((1, 1536), f32), "gdn_conv_fwd", tc=768)
            o_a, s_a = _gdn2_fwd(conv, p, wt["alog_v"][j], wt["dtb_v"][j], wt["gdn_norm"][j], "gdn_fwd")
            o_b, s_b = _hgrn_fwd(p, lbs[j], wt["hgrn_norm"][j], "hgrn_fwd")
            o = jnp.concatenate([o_a, o_b], axis=1)
            mix = _mm_auto(o, wt["ab_w_out"][j], "nn", f32, "mm_ab_out")
            sv.update(p=p, conv=conv, s_a=s_a, s_b=s_b, o=o)
        else:
            p2 = _mm_auto(h1, wt["c_w_in"][j], "nn", f32, "mm_c_in")
            xc = _conv_fwd(p2, 1, D, wt["c_conv_w"][j], wt["c_conv_b"][j], "lru_conv_fwd", tc=D)
            a, u = _lru_gates_fwd(xc, wt["c_gate_a_w"][j], wt["c_gate_x_w"][j], wt["c_gate_a_b"][j],
                                  wt["c_gate_x_b"][j], wt["c_lambda"][j], "lru_gates_fwd")
            h, hg = _lru_scan_fwd(a, u, p2, "lru_scan_fwd")
            mix = _mm_auto(hg, wt["c_w_out"][j], "nn", f32, "mm_c_out")
            sv.update(p2=p2, xc=xc, a=a, h=h, hg=hg)
        x = _residual_add(x, mix, "res_add")
        sv["x_mid"] = x
        h2 = _rmsnorm_fwd(x, wt["norm_ffn"][layer], "rms_fwd")
        up = _mm_auto(h2, wt["ffn_w_up"][layer], "nn", f32, "mm_up")
        gc, act = _conv_fwd(up, 0, D_FF, wt["ffn_conv_w"][layer], wt["ffn_conv_b"][layer], "ffn_conv_fwd",
                            tc=D_FF // 2, val=up, val_col0=2)
        down = _mm_auto(act, wt["ffn_w_down"][layer], "nn", f32, "mm_down")
        x = _residual_add(x, down, "res_add")
        sv.update(h2=h2, up=up, gc=gc, act=act)
        res.append(sv)

    loss, dx, dxb, d_norm_final = _final_loss(x, wt["norm_final"], target, "final_loss")

    g = {k: [None] * len(v) for k, v in wt.items() if isinstance(v, list)}
    g["norm_final"] = d_norm_final
    d_lbs = [None, None]
    for layer in reversed(range(depth)):
        j = layer // 2
        sv = res[layer]
        dact = _mm_auto(dxb, wt["ffn_w_down"][layer], "nt", f32, "mm_down_dx")
        g["ffn_w_down"][layer] = _mm_auto(sv["act"], dxb, "tn", f32, "mm_down_dw")
        dgc, dval = _ffn_act_bwd(sv["gc"], sv["up"], dact, "ffn_act_bwd", tc=D_FF // 2)
        dgate, dcw, dcb = _conv_bwd(dgc, sv["up"], 0, wt["ffn_conv_w"][layer], "ffn_conv_bwd", tc=D_FF // 2)
        g["ffn_conv_w"][layer] = dcw[:3]
        g["ffn_conv_b"][layer] = dcb
        dup = jnp.concatenate([dgate, dval], axis=1)
        g["ffn_w_up"][layer] = _mm_auto(sv["h2"], dup, "tn", f32, "mm_up_dw")
        dh2 = _mm_auto(dup, wt["ffn_w_up"][layer], "nt", f32, "mm_up_dx")
        dx, dxb, g["norm_ffn"][layer] = _rmsnorm_bwd(sv["x_mid"], wt["norm_ffn"][layer], dh2, dx, "rms_bwd")
        if layer % 2 == 0:
            do = _mm_auto(dxb, wt["ab_w_out"][j], "nt", f32, "mm_ab_out_dx")
            g["ab_w_out"][j] = _mm_auto(sv["o"], dxb, "tn", f32, "mm_ab_out_dw")
            dconv, dz, dba, dal, ddt, dgn = _gdn2_bwd(
                sv["conv"], sv["p"], wt["alog_v"][j], wt["dtb_v"][j], wt["gdn_norm"][j], sv["s_a"], do, "gdn_bwd")
            g["alog_v"][j], g["dtb_v"][j], g["gdn_norm"][j] = dal, ddt, dgn
            dqkv, dcw, _ = _conv_bwd(dconv, sv["p"], 0, wt["gdn_conv_w"][j], "gdn_conv_bwd", tc=768)
            g["gdn_conv_w"][j] = dcw[:4]
            dqb, dfb, dib, dgb, dlb, dhn = _hgrn_bwd(sv["p"], lbs[j], wt["hgrn_norm"][j], sv["s_b"], do, "hgrn_bwd")
            g["hgrn_norm"][j] = dhn
            d_lbs[j] = jnp.sum(dlb, axis=0)
            dp = jnp.concatenate([dqkv] + [t_.astype(bf16) for t_ in (dz, dqb, dfb, dib, dgb, dba)], axis=1)
            g["ab_w_in"][j] = _mm_auto(sv["h1"], dp, "tn", f32, "mm_ab_in_dw")
            dh1 = _mm_auto(dp, wt["ab_w_in"][j], "nt", f32, "mm_ab_in_dx")
        else:
            dhg = _mm_auto(dxb, wt["c_w_out"][j], "nt", f32, "mm_c_out_dx")
            g["c_w_out"][j] = _mm_auto(sv["hg"], dxb, "tn", f32, "mm_c_out_dw")
            du, da, dy = _lru_scan_bwd(sv["a"], sv["h"], sv["p2"], dhg, "lru_scan_bwd")
            dxc, dwa, dwx, dba_, dbx_, dlam = _lru_gates_bwd(
                sv["xc"], wt["c_gate_a_w"][j], wt["c_gate_x_w"][j], wt["c_gate_a_b"][j], wt["c_gate_x_b"][j],
                wt["c_lambda"][j], da, du, "lru_gates_bwd")
            g["c_gate_a_w"][j], g["c_gate_x_w"][j] = dwa, dwx
            g["c_gate_a_b"][j], g["c_gate_x_b"][j], g["c_lambda"][j] = dba_, dbx_, dlam
            dxbr, dcw, dcb = _conv_bwd(dxc, sv["p2"], 1, wt["c_conv_w"][j], "lru_conv_bwd", tc=D)
            g["c_conv_w"][j] = dcw[:4]
            g["c_conv_b"][j] = dcb
            dp2 = jnp.concatenate([dy.astype(bf16), dxbr], axis=1)
            g["c_w_in"][j] = _mm_auto(sv["h1"], dp2, "tn", f32, "mm_c_in_dw")
            dh1 = _mm_auto(dp2, wt["c_w_in"][j], "nt", f32, "mm_c_in_dx")
        dx, dxb, g["norm_mix"][layer] = _rmsnorm_bwd(sv["x_in"], wt["norm_mix"][layer], dh1, dx, "rms_bwd")
    g0, g1 = _lower_bounds_bwd(wt["hgrn_lower_bounds"], d_lbs[0], d_lbs[1])
    g["hgrn_lower_bounds"] = jnp.concatenate([g0, g1], axis=0)
    return loss, dx, g


def _ab_in_to_compute(w):
    return jnp.concatenate([w[:, :2048], w[:, 2056:4104], w[:, 2048:2056], jnp.zeros((D, 120), w.dtype)], axis=1)


def _ab_in_from_compute(g):
    return jnp.concatenate([g[:, :2048], g[:, 4096:4104], g[:, 2048:4096]], axis=1)


def _lane_vec(v4):
    return jnp.zeros((1, HEAD), f32).at[0, NH:2 * NH].set(v4)


def _layout_weights(fw):
    wt = {}
    wt["norm_mix"] = [fw["norm_mix"][l][None] for l in range(4)]
    wt["norm_ffn"] = [fw["norm_ffn"][l][None] for l in range(4)]
    wt["norm_final"] = fw["norm_final"][None]
    wt["ab_w_in"] = [_ab_in_to_compute(fw["ab_w_in"][j].astype(bf16)) for j in range(2)]
    wt["gdn_conv_w"] = [fw["gdn_conv_w"][j] for j in range(2)]
    wt["alog_v"] = [_lane_vec(fw["gdn_a_log"][j]) for j in range(2)]
    wt["dtb_v"] = [_lane_vec(fw["gdn_dt_bias"][j]) for j in range(2)]
    wt["gdn_norm"] = [fw["gdn_norm"][j][None] for j in range(2)]
    wt["hgrn_lower_bounds"] = fw["hgrn_lower_bounds"]
    wt["hgrn_norm"] = [fw["hgrn_norm"][j][None] for j in range(2)]
    wt["ab_w_out"] = [fw["ab_w_out"][j].astype(bf16) for j in range(2)]
    wt["c_w_in"] = [fw["c_w_in"][j].astype(bf16) for j in range(2)]
    wt["c_conv_w"] = [fw["c_conv_w"][j] for j in range(2)]
    for k in ("c_conv_b", "c_gate_a_b", "c_gate_x_b", "c_lambda"):
        wt[k] = [fw[k][j][None] for j in range(2)]
    for k in ("c_gate_a_w", "c_gate_x_w"):
        wt[k] = [fw[k][j].astype(f32) for j in range(2)]
    wt["c_w_out"] = [fw["c_w_out"][j].astype(bf16) for j in range(2)]
    wt["ffn_w_up"] = [fw["ffn_w_up"][l].astype(bf16) for l in range(4)]
    wt["ffn_conv_w"] = [fw["ffn_conv_w"][l] for l in range(4)]
    wt["ffn_conv_b"] = [fw["ffn_conv_b"][l][None] for l in range(4)]
    wt["ffn_w_down"] = [fw["ffn_w_down"][l].astype(bf16) for l in range(4)]
    return wt


def _unlayout_grads(g):
    out = {}
    for k in ("norm_mix", "norm_ffn", "gdn_norm", "hgrn_norm", "c_conv_b", "c_gate_a_b", "c_gate_x_b", "c_lambda",
              "ffn_conv_b"):
        out[k] = jnp.concatenate(g[k], axis=0)
    out["norm_final"] = g["norm_final"][0]
    out["ab_w_in"] = jnp.stack([_ab_in_from_compute(t) for t in g["ab_w_in"]])
    out["gdn_a_log"] = jnp.stack([t[0, NH:2 * NH] for t in g["alog_v"]])
    out["gdn_dt_bias"] = jnp.stack([t[0, NH:2 * NH] for t in g["dtb_v"]])
    out["hgrn_lower_bounds"] = g["hgrn_lower_bounds"]
    for k in ("gdn_conv_w", "ab_w_out", "c_w_in", "c_conv_w", "c_gate_a_w", "c_gate_x_w", "c_w_out", "ffn_w_up",
              "ffn_conv_w", "ffn_w_down"):
        out[k] = jnp.stack(g[k])
    return out


MESH = pl.DeviceIdType.MESH
ANY = pl.BlockSpec(memory_space=pl.ANY)
CHIP_RELATIONS = ((1, 0), (0, 1), (1, 1))
N_CHIPS = 4


def _coords():
    return lax.axis_index("x"), lax.axis_index("y"), lax.axis_index("c")


def _flip(v, f):
    return 1 - v if f else v


def _half_rows(c, a, align):
    return pl.ds(pl.multiple_of(c * (a // 2), align), a // 2)


def _all_gather_chips(shards, name):
    n = len(shards)
    shapes = [s.shape for s in shards]

    def body(*refs):
        ins, outs = refs[:n], refs[n:2 * n]
        send_sems, recv_sems = refs[2 * n:]
        x, y, c = _coords()
        me = 2 * x + y
        sibling = (x, y, 1 - c)
        started = []
        for p in range(n):
            cp = pltpu.make_async_remote_copy(
                src_ref=ins[p], dst_ref=outs[p].at[me],
                send_sem=send_sems.at[p, 6], recv_sem=recv_sems.at[p, 6],
                device_id=sibling, device_id_type=MESH)
            cp.start()
            started.append(cp)
        for p in range(n):
            mine = _half_rows(c, shapes[p][0], 16)
            for r, (fx, fy) in enumerate(CHIP_RELATIONS):
                cp = pltpu.make_async_remote_copy(
                    src_ref=ins[p].at[mine], dst_ref=outs[p].at[me, mine],
                    send_sem=send_sems.at[p, r], recv_sem=recv_sems.at[p, r],
                    device_id=(_flip(x, fx), _flip(y, fy), c), device_id_type=MESH)
                cp.start()
                started.append(cp)
        for r, (fx, fy) in enumerate(CHIP_RELATIONS):
            k = 2 * _flip(x, fx) + _flip(y, fy)
            for p in range(n):
                mine = _half_rows(c, shapes[p][0], 16)
                pltpu.make_async_remote_copy(
                    src_ref=ins[p].at[mine], dst_ref=outs[p].at[k, mine],
                    send_sem=send_sems.at[p, r], recv_sem=recv_sems.at[p, r],
                    device_id=(_flip(x, fx), _flip(y, fy), c), device_id_type=MESH).wait_recv()
                fwd = pltpu.make_async_remote_copy(
                    src_ref=outs[p].at[k, mine], dst_ref=outs[p].at[k, mine],
                    send_sem=send_sems.at[p, 3 + r], recv_sem=recv_sems.at[p, 3 + r],
                    device_id=sibling, device_id_type=MESH)
                fwd.start()
                started.append(fwd)
        for r, (fx, fy) in enumerate(CHIP_RELATIONS):
            k = 2 * _flip(x, fx) + _flip(y, fy)
            for p in range(n):
                theirs = _half_rows(1 - c, shapes[p][0], 16)
                pltpu.make_async_remote_copy(
                    src_ref=outs[p].at[k, theirs], dst_ref=outs[p].at[k, theirs],
                    send_sem=send_sems.at[p, 3 + r], recv_sem=recv_sems.at[p, 3 + r],
                    device_id=sibling, device_id_type=MESH).wait_recv()
        for p in range(n):
            pltpu.make_async_remote_copy(
                src_ref=ins[p], dst_ref=outs[p].at[me],
                send_sem=send_sems.at[p, 6], recv_sem=recv_sems.at[p, 6],
                device_id=sibling, device_id_type=MESH).wait_recv()
        for cp in started:
            cp.wait_send()

    return pl.pallas_call(
        body, in_specs=[ANY] * n, out_specs=[ANY] * n,
        out_shape=[_sds((N_CHIPS,) + s.shape, s.dtype) for s in shards],
        scratch_shapes=[pltpu.SemaphoreType.DMA((n, 7)), pltpu.SemaphoreType.DMA((n, 7))],
        name=name)(*shards)


def _sibling_send_other_half(gs, name):
    n = len(gs)
    shapes = [g.shape for g in gs]

    def body(*refs):
        ins, outs = refs[:n], refs[n:2 * n]
        send_sems, recv_sems = refs[2 * n:]
        x, y, c = _coords()
        cps = []
        for p in range(n):
            theirs = _half_rows(1 - c, shapes[p][1], 8)
            cp = pltpu.make_async_remote_copy(
                src_ref=ins[p].at[:, theirs], dst_ref=outs[p],
                send_sem=send_sems.at[p], recv_sem=recv_sems.at[p],
                device_id=(x, y, 1 - c), device_id_type=MESH)
            cp.start()
            cps.append(cp)
        for cp in cps:
            cp.wait()

    return pl.pallas_call(
        body, in_specs=[ANY] * n, out_specs=[ANY] * n,
        out_shape=[_sds((s[0], s[1] // 2, s[2]), f32) for s in shapes],
        scratch_shapes=[pltpu.SemaphoreType.DMA((n,)), pltpu.SemaphoreType.DMA((n,))],
        name=name)(*gs)


def _chip_exchange(ps, name):
    n = len(ps)

    def body(*refs):
        ins, outs = refs[:n], refs[n:2 * n]
        send_sems, recv_sems = refs[2 * n:]
        x, y, c = _coords()
        cps = []
        for p in range(n):
            for r, (fx, fy) in enumerate(CHIP_RELATIONS):
                k = 2 * _flip(x, fx) + _flip(y, fy)
                cp = pltpu.make_async_remote_copy(
                    src_ref=ins[p].at[k], dst_ref=outs[p].at[r],
                    send_sem=send_sems.at[p, r], recv_sem=recv_sems.at[p, r],
                    device_id=(_flip(x, fx), _flip(y, fy), c), device_id_type=MESH)
                cp.start()
                cps.append(cp)
        for cp in cps:
            cp.wait_recv()
        for cp in cps:
            cp.wait_send()

    return pl.pallas_call(
        body, in_specs=[ANY] * n, out_specs=[ANY] * n,
        out_shape=[_sds((3,) + p.shape[1:], p.dtype) for p in ps],
        scratch_shapes=[pltpu.SemaphoreType.DMA((n, 3)), pltpu.SemaphoreType.DMA((n, 3))],
        name=name)(*ps)


def _sibling_fill_other_half(fs, name):
    n = len(fs)
    shapes = [f.shape for f in fs]

    def body(*refs):
        ins, outs = refs[:n], refs[n:2 * n]
        send_sems, recv_sems = refs[2 * n:]
        x, y, c = _coords()
        cps = []
        for p in range(n):
            mine = _half_rows(c, shapes[p][0], 8)
            cp = pltpu.make_async_remote_copy(
                src_ref=ins[p].at[mine], dst_ref=outs[p].at[mine],
                send_sem=send_sems.at[p], recv_sem=recv_sems.at[p],
                device_id=(x, y, 1 - c), device_id_type=MESH)
            cp.start()
            cps.append(cp)
        for p in range(n):
            theirs = _half_rows(1 - c, shapes[p][0], 8)
            pltpu.make_async_remote_copy(
                src_ref=ins[p].at[theirs], dst_ref=outs[p].at[theirs],
                send_sem=send_sems.at[p], recv_sem=recv_sems.at[p],
                device_id=(x, y, 1 - c), device_id_type=MESH).wait_recv()
        for cp in cps:
            cp.wait_send()

    return pl.pallas_call(
        body, in_specs=[ANY] * n, out_specs=[ANY] * n,
        out_shape=[_sds(f.shape, f.dtype) for f in fs],
        input_output_aliases={p: p for p in range(n)},
        scratch_shapes=[pltpu.SemaphoreType.DMA((n,)), pltpu.SemaphoreType.DMA((n,))],
        name=name)(*fs)


def _sibling_all_gather(ss, name):
    n = len(ss)
    shapes = [s.shape for s in ss]

    def body(*refs):
        ins, outs = refs[:n], refs[n:2 * n]
        send_sems, recv_sems, loc_sems = refs[2 * n:]
        x, y, c = _coords()
        locs, cps = [], []
        for p in range(n):
            mine = _half_rows(c, 2 * shapes[p][0], 8)
            loc = pltpu.make_async_copy(ins[p], outs[p].at[mine], loc_sems.at[p])
            loc.start()
            locs.append(loc)
            cp = pltpu.make_async_remote_copy(
                src_ref=ins[p], dst_ref=outs[p].at[mine],
                send_sem=send_sems.at[p], recv_sem=recv_sems.at[p],
                device_id=(x, y, 1 - c), device_id_type=MESH)
            cp.start()
            cps.append(cp)
        for p, cp in enumerate(cps):
            theirs = _half_rows(1 - c, 2 * shapes[p][0], 8)
            pltpu.make_async_remote_copy(
                src_ref=ins[p], dst_ref=outs[p].at[theirs],
                send_sem=send_sems.at[p], recv_sem=recv_sems.at[p],
                device_id=(x, y, 1 - c), device_id_type=MESH).wait_recv()
        for cp in cps:
            cp.wait_send()
        for loc in locs:
            loc.wait()

    return pl.pallas_call(
        body, in_specs=[ANY] * n, out_specs=[ANY] * n,
        out_shape=[_sds((2 * s[0], s[1]), f32) for s in shapes],
        scratch_shapes=[pltpu.SemaphoreType.DMA((n,)), pltpu.SemaphoreType.DMA((n,)), pltpu.SemaphoreType.DMA((n,))],
        name=name)(*ss)


N_DEV = 8


def _all_reduce_small(pack, name):
    rows = pack.shape[0]

    def body(in_ref, sum_ref, all_ref, send_sems, recv_sems):
        x, y, c = _coords()
        me = 4 * x + 2 * y + c
        all_ref[me] = in_ref[...]
        cps = []
        for r in range(1, N_DEV):
            fx, fy, fc = (r >> 2) & 1, (r >> 1) & 1, r & 1
            cp = pltpu.make_async_remote_copy(
                src_ref=in_ref, dst_ref=all_ref.at[me],
                send_sem=send_sems.at[r], recv_sem=recv_sems.at[r],
                device_id=(_flip(x, fx), _flip(y, fy), _flip(c, fc)), device_id_type=MESH)
            cp.start()
            cps.append(cp)
        for r in range(1, N_DEV):
            fx, fy, fc = (r >> 2) & 1, (r >> 1) & 1, r & 1
            peer = 4 * _flip(x, fx) + 2 * _flip(y, fy) + _flip(c, fc)
            pltpu.make_async_remote_copy(
                src_ref=in_ref, dst_ref=all_ref.at[peer],
                send_sem=send_sems.at[r], recv_sem=recv_sems.at[r],
                device_id=(x, y, c), device_id_type=MESH).wait_recv()
        for cp in cps:
            cp.wait_send()
        acc = all_ref[0]
        for d in range(1, N_DEV):
            acc = acc + all_ref[d]
        sum_ref[...] = acc

    vm = pl.BlockSpec(memory_space=pltpu.VMEM)
    return pl.pallas_call(
        body, in_specs=[vm], out_specs=[vm, vm],
        out_shape=[_sds((rows, 128), f32), _sds((N_DEV, rows, 128), f32)],
        scratch_shapes=[pltpu.SemaphoreType.DMA((N_DEV,)), pltpu.SemaphoreType.DMA((N_DEV,))],
        name=name)(pack)[0]


ROWS_EW = 128


def _add_own_half(g, rs, c_arr, name):
    s, a, b = g.shape
    nrt = (a // 2) // ROWS_EW

    def body(c_ref, g_ref, r_ref, o_ref):
        o_ref[...] = (g_ref[...] + r_ref[...]).astype(bf16)

    grid_spec = pltpu.PrefetchScalarGridSpec(
        num_scalar_prefetch=1, grid=(s, nrt),
        in_specs=[pl.BlockSpec((1, ROWS_EW, b), lambda k, i, c_ref: (k, c_ref[0] * nrt + i, 0)),
                  pl.BlockSpec((1, ROWS_EW, b), lambda k, i, c_ref: (k, i, 0))],
        out_specs=pl.BlockSpec((1, ROWS_EW, b), lambda k, i, c_ref: (k, i, 0)))
    return pl.pallas_call(body, grid_spec=grid_spec, out_shape=_sds((s, a // 2, b), bf16),
                          compiler_params=_cp(), name=name)(c_arr, g, rs)


def _sum_chips(g, rs, rc, ids, name):
    _, r, b = rc.shape
    nrt = r // ROWS_EW

    def body(ids_ref, g_ref, s_ref, r_ref, o_ref):
        own = g_ref[0] + s_ref[0]
        o_ref[...] = ((own + r_ref[0].astype(f32)) + r_ref[1].astype(f32)) + r_ref[2].astype(f32)

    grid_spec = pltpu.PrefetchScalarGridSpec(
        num_scalar_prefetch=1, grid=(nrt,),
        in_specs=[pl.BlockSpec((1, ROWS_EW, b), lambda i, ids_ref: (ids_ref[0], ids_ref[1] * nrt + i, 0)),
                  pl.BlockSpec((1, ROWS_EW, b), lambda i, ids_ref: (ids_ref[0], i, 0)),
                  pl.BlockSpec((3, ROWS_EW, b), lambda i, ids_ref: (0, i, 0))],
        out_specs=pl.BlockSpec((ROWS_EW, b), lambda i, ids_ref: (ids_ref[1] * nrt + i, 0)))
    return pl.pallas_call(body, grid_spec=grid_spec, out_shape=_sds((2 * r, b), f32),
                          compiler_params=_cp(), name=name)(ids, g, rs, rc)


def _adamw_math(w, g, m, v):
    m = ADAM_B1 * m + (1.0 - ADAM_B1) * g
    v = ADAM_B2 * v + (1.0 - ADAM_B2) * (g * g)
    m_hat = m / (1.0 - ADAM_B1 ** ADAM_STEP)
    v_hat = v / (1.0 - ADAM_B2 ** ADAM_STEP)
    delta = -ADAM_LR * (m_hat / (jnp.sqrt(v_hat) + ADAM_EPS) + ADAM_WD * w)
    return delta, m, v


def _adamw_big(w, g, m, v, name):
    a, b = w.shape

    def body(w_ref, g_ref, m_ref, v_ref, go_ref, d_ref, mo_ref, vo_ref):
        gv = g_ref[...]
        d, mn, vn = _adamw_math(w_ref[...], gv, m_ref[...], v_ref[...])
        go_ref[...] = gv
        d_ref[...] = d
        mo_ref[...] = mn
        vo_ref[...] = vn

    blk = pl.BlockSpec((ROWS_EW, b), lambda i: (i, 0))
    return pl.pallas_call(body, grid=(a // ROWS_EW,), in_specs=[blk] * 4, out_specs=[blk] * 4,
                          out_shape=[_sds((a, b), f32)] * 4, compiler_params=_cp(), name=name)(w, g, m, v)


def _adamw_small(ws, gs, ms, vs, name):
    n = len(ws)

    def body(*refs):
        w_r, g_r, m_r, v_r = refs[:n], refs[n:2 * n], refs[2 * n:3 * n], refs[3 * n:4 * n]
        go_r, d_r, mo_r, vo_r = refs[4 * n:5 * n], refs[5 * n:6 * n], refs[6 * n:7 * n], refs[7 * n:8 * n]
        for p in range(n):
            gv = g_r[p][...]
            d, mn, vn = _adamw_math(w_r[p][...], gv, m_r[p][...], v_r[p][...])
            go_r[p][...] = gv
            d_r[p][...] = d
            mo_r[p][...] = mn
            vo_r[p][...] = vn

    vm = pl.BlockSpec(memory_space=pltpu.VMEM)
    shp = [_sds(w.shape, f32) for w in ws]
    res = pl.pallas_call(body, in_specs=[vm] * (4 * n), out_specs=[vm] * (4 * n), out_shape=shp * 4,
                         name=name)(*ws, *gs, *ms, *vs)
    return res[:n], res[n:2 * n], res[2 * n:3 * n], res[3 * n:]


WEIGHTS = ["norm_mix", "norm_ffn", "norm_final", "ab_w_in", "gdn_conv_w", "gdn_a_log", "gdn_dt_bias", "gdn_norm",
           "hgrn_lower_bounds", "hgrn_norm", "ab_w_out", "c_w_in", "c_conv_w", "c_conv_b", "c_gate_a_w", "c_gate_a_b",
           "c_gate_x_w", "c_gate_x_b", "c_lambda", "c_w_out", "ffn_w_up", "ffn_conv_w", "ffn_conv_b", "ffn_w_down"]
BIG = {"ab_w_in": "col", "ab_w_out": "row", "c_w_in": "col", "c_gate_a_w": "gate", "c_gate_x_w": "gate",
       "c_w_out": "row", "ffn_w_up": "col", "ffn_w_down": "row"}
SMALL_SHARDED = ["gdn_conv_w", "c_conv_w", "c_conv_b", "c_gate_a_b", "c_gate_x_b", "c_lambda", "ffn_conv_w"]
SMALL = [n for n in WEIGHTS if n not in BIG]
FULL_SHAPES = {
    "norm_mix": (4, 1024), "norm_ffn": (4, 1024), "norm_final": (1024,), "ab_w_in": (2, 1024, 4104),
    "gdn_conv_w": (2, 4, 1536), "gdn_a_log": (2, 4), "gdn_dt_bias": (2, 4), "gdn_norm": (2, 128),
    "hgrn_lower_bounds": (2, 512), "hgrn_norm": (2, 128), "ab_w_out": (2, 1024, 1024), "c_w_in": (2, 1024, 2048),
    "c_conv_w": (2, 4, 1024), "c_conv_b": (2, 1024), "c_gate_a_w": (2, 4, 256, 256), "c_gate_a_b": (2, 1024),
    "c_gate_x_w": (2, 4, 256, 256), "c_gate_x_b": (2, 1024), "c_lambda": (2, 1024), "c_w_out": (2, 1024, 1024),
    "ffn_w_up": (4, 1024, 5632), "ffn_conv_w": (4, 3, 2816), "ffn_conv_b": (4, 2816), "ffn_w_down": (4, 2816, 1024)}


def _shard2d(name, shard):
    return shard.reshape(-1, shard.shape[-1])


def _full_from_slots(name, slots):
    full = FULL_SHAPES[name]
    kind = BIG[name]
    if kind == "col":
        l, r, cdim = full
        return slots.reshape(4, l, r, cdim // 4).transpose(1, 2, 0, 3).reshape(full)
    if kind == "row":
        l, r, cdim = full
        return slots.reshape(4, l, r // 4, cdim).transpose(1, 0, 2, 3).reshape(full)
    l, h, r, cdim = full
    return slots.reshape(4, l, h, r // 4, cdim).transpose(1, 2, 0, 3, 4).reshape(full)


def _slots_from_full(name, g):
    full = FULL_SHAPES[name]
    kind = BIG[name]
    if kind == "col":
        l, r, cdim = full
        return g.reshape(l, r, 4, cdim // 4).transpose(2, 0, 1, 3).reshape(4, l * r, cdim // 4)
    if kind == "row":
        l, r, cdim = full
        return g.reshape(l, 4, r // 4, cdim).transpose(1, 0, 2, 3).reshape(4, l * (r // 4), cdim)
    l, h, r, cdim = full
    return g.reshape(l, h, 4, r // 4, cdim).transpose(2, 0, 1, 3, 4).reshape(4, l * h * (r // 4), cdim)


def _pack_rows(arrs, rows):
    flat = jnp.concatenate([a.reshape(-1) for a in arrs])
    return jnp.pad(flat, (0, rows * 128 - flat.shape[0])).reshape(rows, 128)


def _unpack_rows(pack, shapes):
    flat = pack.reshape(-1)
    out, off = [], 0
    for s in shapes:
        size = 1
        for d in s:
            size *= d
        out.append(flat[off:off + size].reshape(s))
        off += size
    return out


def kernel(x, norm_mix, norm_ffn, norm_final, ab_w_in, gdn_conv_w, gdn_a_log, gdn_dt_bias, gdn_norm, hgrn_lower_bounds, hgrn_norm, ab_w_out, c_w_in, c_conv_w, c_conv_b, c_gate_a_w, c_gate_a_b, c_gate_x_w, c_gate_x_b, c_lambda, c_w_out, ffn_w_up, ffn_conv_w, ffn_conv_b, ffn_w_down, loss_target, m_norm_mix, m_norm_ffn, m_norm_final, m_ab_w_in, m_gdn_conv_w, m_gdn_a_log, m_gdn_dt_bias, m_gdn_norm, m_hgrn_lower_bounds, m_hgrn_norm, m_ab_w_out, m_c_w_in, m_c_conv_w, m_c_conv_b, m_c_gate_a_w, m_c_gate_a_b, m_c_gate_x_w, m_c_gate_x_b, m_c_lambda, m_c_w_out, m_ffn_w_up, m_ffn_conv_w, m_ffn_conv_b, m_ffn_w_down, v_norm_mix, v_norm_ffn, v_norm_final, v_ab_w_in, v_gdn_conv_w, v_gdn_a_log, v_gdn_dt_bias, v_gdn_norm, v_hgrn_lower_bounds, v_hgrn_norm, v_ab_w_out, v_c_w_in, v_c_conv_w, v_c_conv_b, v_c_gate_a_w, v_c_gate_a_b, v_c_gate_x_w, v_c_gate_x_b, v_c_lambda, v_c_w_out, v_ffn_w_up, v_ffn_conv_w, v_ffn_conv_b, v_ffn_w_down):
    w = dict(zip(WEIGHTS, (norm_mix, norm_ffn, norm_final, ab_w_in, gdn_conv_w, gdn_a_log, gdn_dt_bias, gdn_norm, hgrn_lower_bounds, hgrn_norm, ab_w_out, c_w_in, c_conv_w, c_conv_b, c_gate_a_w, c_gate_a_b, c_gate_x_w, c_gate_x_b, c_lambda, c_w_out, ffn_w_up, ffn_conv_w, ffn_conv_b, ffn_w_down)))
    m = dict(zip(WEIGHTS, (m_norm_mix, m_norm_ffn, m_norm_final, m_ab_w_in, m_gdn_conv_w, m_gdn_a_log, m_gdn_dt_bias, m_gdn_norm, m_hgrn_lower_bounds, m_hgrn_norm, m_ab_w_out, m_c_w_in, m_c_conv_w, m_c_conv_b, m_c_gate_a_w, m_c_gate_a_b, m_c_gate_x_w, m_c_gate_x_b, m_c_lambda, m_c_w_out, m_ffn_w_up, m_ffn_conv_w, m_ffn_conv_b, m_ffn_w_down)))
    v = dict(zip(WEIGHTS, (v_norm_mix, v_norm_ffn, v_norm_final, v_ab_w_in, v_gdn_conv_w, v_gdn_a_log, v_gdn_dt_bias, v_gdn_norm, v_hgrn_lower_bounds, v_hgrn_norm, v_ab_w_out, v_c_w_in, v_c_conv_w, v_c_conv_b, v_c_gate_a_w, v_c_gate_a_b, v_c_gate_x_w, v_c_gate_x_b, v_c_lambda, v_c_w_out, v_ffn_w_up, v_ffn_conv_w, v_ffn_conv_b, v_ffn_w_down)))
    big = list(BIG)
    chip = 2 * lax.axis_index("x") + lax.axis_index("y")
    c_arr = lax.axis_index("c").astype(jnp.int32).reshape(1)

    small_shard_shapes = [w[n].shape for n in SMALL_SHARDED]
    small_pack = _pack_rows([w[n] for n in SMALL_SHARDED], 128)
    gathered = _all_gather_chips([_shard2d(n, w[n]).astype(bf16) for n in big] + [small_pack], "all_gather_weights")
    fw = {n: _full_from_slots(n, gathered[i]) for i, n in enumerate(big)}
    per_chip = [_unpack_rows(gathered[-1][k], small_shard_shapes) for k in range(N_CHIPS)]
    for i, n in enumerate(SMALL_SHARDED):
        fw[n] = jnp.concatenate([per_chip[k][i] for k in range(N_CHIPS)], axis=-1)
    for n in WEIGHTS:
        if n not in fw:
            fw[n] = w[n]

    loss, dx, g = _local_step(x[0], loss_target[0], _layout_weights(fw))
    gf = _unlayout_grads(g)
    loss = lax.psum(loss[0, 0], ("x", "y", "c"))

    slots = [_slots_from_full(n, gf[n]) for n in big]
    from_sibling = _sibling_send_other_half(slots, "rs_sibling_halves")
    partial = [_add_own_half(slots[i], from_sibling[i], c_arr, "rs_add_sibling") for i in range(len(big))]
    from_chips = _chip_exchange(partial, "rs_chip_exchange")
    ids = jnp.stack([chip, lax.axis_index("c")]).astype(jnp.int32)
    summed = [_sum_chips(slots[i], from_sibling[i], from_chips[i], ids, "rs_sum_chips") for i in range(len(big))]
    g_shard = dict(zip(big, _sibling_fill_other_half(summed, "rs_sibling_fill")))

    small_full_shapes = [FULL_SHAPES[n] for n in SMALL]
    small_sum = _all_reduce_small(_pack_rows([gf[n] for n in SMALL], 664), "all_reduce_small")
    g_small = dict(zip(SMALL, _unpack_rows(small_sum, small_full_shapes)))
    for n in SMALL_SHARDED:
        width = w[n].shape[-1]
        g_small[n] = lax.dynamic_slice_in_dim(g_small[n], chip * width, width, axis=-1)

    out_g, out_d, out_m, out_v = {}, {}, {}, {}
    for n in big:
        shp = w[n].shape
        res = _adamw_big(_shard2d(n, w[n]), g_shard[n], _shard2d(n, m[n]), _shard2d(n, v[n]), "adamw_" + n)
        out_g[n], out_d[n], out_m[n], out_v[n] = (t.reshape(shp) for t in res)
    as2d = lambda t: t.reshape(-1, t.shape[-1])
    sg, sd, sm, sv = _adamw_small([as2d(w[n]) for n in SMALL], [as2d(g_small[n]) for n in SMALL],
                                  [as2d(m[n]) for n in SMALL], [as2d(v[n]) for n in SMALL], "adamw_small")
    for i, n in enumerate(SMALL):
        out_g[n], out_d[n], out_m[n], out_v[n] = (t[i].reshape(w[n].shape) for t in (sg, sd, sm, sv))
    return (loss, dx[None], *[out_g[n] for n in WEIGHTS], *[out_d[n] for n in WEIGHTS],
            *[out_m[n] for n in WEIGHTS], *[out_v[n] for n in WEIGHTS])
```

```python
import functools

import jax
import jax.numpy as jnp
from jax import lax
from jax.experimental import pallas as pl
from jax.experimental.pallas import tpu as pltpu

f32 = jnp.float32
bf16 = jnp.bfloat16
HI = lax.Precision.HIGHEST

D = 1024
EPS = 1e-6
F_FLOOR = 1e-30
GDN_CHUNK = 64
GDN_INTRA_CHUNKS = 1
HGRN_CHUNK = 16
HGRN_STEP = 128
HEAD = 128
NH = 4
LRU_BLOCK = 256
D_FF = 2816
RG_C = 8.0
C_QKV, C_Z, C_QB, C_FB, C_IB, C_GB, C_BA = 0, 1536, 2048, 2560, 3072, 3584, 4096
AB_COLS_PAD = 4224
TT = 256
VMEM_LIMIT = 56 * 1024 * 1024

ADAM_LR, ADAM_B1, ADAM_B2, ADAM_EPS, ADAM_WD, ADAM_STEP = 0.001, 0.9, 0.999, 1e-08, 0.01, 10


def _cp(**kw):
    return pltpu.CompilerParams(vmem_limit_bytes=VMEM_LIMIT, **kw)


def _sds(shape, dtype):
    return jax.ShapeDtypeStruct(shape, dtype)


def _dot(a, b, dims, precision=None):
    return lax.dot_general(a, b, (dims, ((), ())), precision=precision, preferred_element_type=f32)


NN = ((1,), (0,))
NT = ((1,), (1,))
TN = ((0,), (0,))


def _rms(x, g):
    return x * lax.rsqrt(jnp.mean(x * x, axis=-1, keepdims=True) + EPS) * g


def _silu(x):
    return x * jax.nn.sigmoid(x)


def _mm(a, b, mode, tm, tn, out_dtype, name):
    if mode == "nn":
        (m, k), n = a.shape, b.shape[1]
        a_spec = pl.BlockSpec((tm, k), lambda i, j: (i, 0))
        b_spec = pl.BlockSpec((k, tn), lambda i, j: (0, j))
        dims = NN
    elif mode == "nt":
        (m, k), n = a.shape, b.shape[0]
        a_spec = pl.BlockSpec((tm, k), lambda i, j: (i, 0))
        b_spec = pl.BlockSpec((tn, k), lambda i, j: (j, 0))
        dims = NT
    else:
        (k, m), n = a.shape, b.shape[1]
        a_spec = pl.BlockSpec((k, tm), lambda i, j: (0, i))
        b_spec = pl.BlockSpec((k, tn), lambda i, j: (0, j))
        dims = TN
    assert m % tm == 0 and n % tn == 0, (name, m, n, tm, tn)

    def body(a_ref, b_ref, o_ref):
        o_ref[...] = _dot(a_ref[...], b_ref[...], dims).astype(out_dtype)

    return pl.pallas_call(
        body, grid=(m // tm, n // tn), in_specs=[a_spec, b_spec],
        out_specs=pl.BlockSpec((tm, tn), lambda i, j: (i, j)),
        out_shape=_sds((m, n), out_dtype), compiler_params=_cp(), name=name)(a, b)


def _tile(n, cands):
    for c in cands:
        if n % c == 0:
            return c
    raise ValueError(n)


def _mm_auto(a, b, mode, out_dtype, name):
    m = a.shape[1] if mode == "tn" else a.shape[0]
    n = b.shape[0] if mode == "nt" else b.shape[1]
    return _mm(a, b, mode, _tile(m, (512, 256, 128)), _tile(n, (1024, 1408, 512, 384, 256, 128)), out_dtype, name)


def _rmsnorm_fwd(x, gain, name, after=()):
    t = x.shape[0]

    def body(x_ref, g_ref, *rest):
        h_ref = rest[len(after)]
        h_ref[...] = _rms(x_ref[...], g_ref[...]).astype(bf16)

    return pl.pallas_call(
        body, grid=(t // TT,),
        in_specs=[pl.BlockSpec((TT, D), lambda i: (i, 0)), pl.BlockSpec((1, D), lambda i: (0, 0))]
        + [pl.BlockSpec(memory_space=pl.ANY)] * len(after),
        out_specs=pl.BlockSpec((TT, D), lambda i: (i, 0)),
        out_shape=_sds((t, D), bf16), compiler_params=_cp(), name=name)(x, gain, *after)


def _rmsnorm_bwd(x, gain, dh, dres, name, after=()):
    t = x.shape[0]

    def body(x_ref, g_ref, dh_ref, dres_ref, *rest):
        dx_ref, dxb_ref, dg_ref = rest[len(after):]
        _, vjp = jax.vjp(_rms, x_ref[...], g_ref[...])
        dx, dg = vjp(dh_ref[...])
        dx = dx + dres_ref[...]
        dx_ref[...] = dx
        dxb_ref[...] = dx.astype(bf16)

        @pl.when(pl.program_id(0) == 0)
        def _():
            dg_ref[...] = jnp.zeros_like(dg_ref)

        dg_ref[...] += dg

    row = pl.BlockSpec((TT, D), lambda i: (i, 0))
    vec = pl.BlockSpec((1, D), lambda i: (0, 0))
    return pl.pallas_call(
        body, grid=(t // TT,), in_specs=[row, vec, row, row] + [pl.BlockSpec(memory_space=pl.ANY)] * len(after),
        out_specs=[row, row, vec],
        out_shape=[_sds((t, D), f32), _sds((t, D), bf16), _sds((1, D), f32)],
        compiler_params=_cp(), name=name)(x, gain, dh, dres, *after)


def _residual_add(x, y, name):
    t = x.shape[0]

    def body(x_ref, y_ref, o_ref):
        o_ref[...] = x_ref[...] + y_ref[...]

    row = pl.BlockSpec((TT, D), lambda i: (i, 0))
    return pl.pallas_call(body, grid=(t // TT,), in_specs=[row, row], out_specs=row,
                          out_shape=_sds((t, D), f32), compiler_params=_cp(), name=name)(x, y)


def _final_loss(x, gain, target, name):
    t = x.shape[0]

    def loss_fn(xv, gv, tv):
        e = _rms(xv, gv) - tv
        return 0.5 * jnp.sum(jnp.mean(e * e, axis=-1))

    def body(x_ref, g_ref, t_ref, loss_ref, dx_ref, dxb_ref, dg_ref):
        val, (dx, dg) = jax.value_and_grad(loss_fn, argnums=(0, 1))(x_ref[...], g_ref[...], t_ref[...])

        @pl.when(pl.program_id(0) == 0)
        def _():
            dg_ref[...] = jnp.zeros_like(dg_ref)
            loss_ref[...] = jnp.zeros_like(loss_ref)

        dx_ref[...] = dx
        dxb_ref[...] = dx.astype(bf16)
        dg_ref[...] += dg
        loss_ref[...] += jnp.full((1, 128), val, f32)

    row = pl.BlockSpec((TT, D), lambda i: (i, 0))
    vec = pl.BlockSpec((1, D), lambda i: (0, 0))
    return pl.pallas_call(
        body, grid=(t // TT,), in_specs=[row, vec, row],
        out_specs=[pl.BlockSpec((1, 128), lambda i: (0, 0)), row, row, vec],
        out_shape=[_sds((1, 128), f32), _sds((t, D), f32), _sds((t, D), bf16), _sds((1, D), f32)],
        compiler_params=_cp(), name=name)(x, gain, target)


def _conv_fwd(x, col0, c, w, b, name, tc=256, val=None, val_col0=0):
    t = x.shape[0]
    width = w.shape[0]
    nt = t // TT
    hb = TT // 8

    def body(*refs):
        if val is None:
            x_ref, xh_ref, w_ref, b_ref, o_ref, xp = refs
        else:
            x_ref, xh_ref, w_ref, b_ref, v_ref, o_ref, act_ref, xp = refs
        i = pl.program_id(1)
        xp[0:8, :] = jnp.where(i == 0, 0.0, xh_ref[...])
        xp[8:, :] = x_ref[...]
        acc = jnp.zeros((TT, tc), f32) + b_ref[...]
        for k in range(width):
            acc = acc + w_ref[k:k + 1, :] * xp[pl.ds(8 - (width - 1) + k, TT), :]
        o_ref[...] = acc
        if val is not None:
            act_ref[...] = (_silu(acc) * v_ref[...]).astype(bf16)

    in_specs = [
        pl.BlockSpec((TT, tc), lambda j, i: (i, j + col0)),
        pl.BlockSpec((8, tc), lambda j, i: (jnp.maximum(i * hb - 1, 0), j + col0)),
        pl.BlockSpec((width, tc), lambda j, i: (0, j)),
        pl.BlockSpec((1, tc), lambda j, i: (0, j)),
    ]
    args = [x, x, w, b]
    out_specs = [pl.BlockSpec((TT, tc), lambda j, i: (i, j))]
    out_shape = [_sds((t, c), f32)]
    if val is not None:
        in_specs.append(pl.BlockSpec((TT, tc), lambda j, i: (i, j + val_col0)))
        args.append(val)
        out_specs.append(pl.BlockSpec((TT, tc), lambda j, i: (i, j)))
        out_shape.append(_sds((t, c), bf16))
    res = pl.pallas_call(
        body, grid=(c // tc, nt), in_specs=in_specs, out_specs=out_specs, out_shape=out_shape,
        scratch_shapes=[pltpu.VMEM((TT + 8, tc), f32)], compiler_params=_cp(), name=name)(*args)
    return res[0] if val is None else res


def _conv_bwd(dc, x, col0, w, name, tc=256, dx_dtype=bf16):
    t, c = dc.shape
    width = w.shape[0]
    nt = t // TT
    hb = TT // 8

    def body(dc_ref, dcn_ref, x_ref, xh_ref, w_ref, dx_ref, dw_ref, db_ref, dcp, xp):
        i = pl.program_id(1)
        dcv = dc_ref[...]
        dcp[0:TT, :] = dcv
        dcp[TT:, :] = jnp.where(i == nt - 1, 0.0, dcn_ref[...])
        xp[0:8, :] = jnp.where(i == 0, 0.0, xh_ref[...])
        xp[8:, :] = x_ref[...]

        @pl.when(i == 0)
        def _():
            dw_ref[...] = jnp.zeros_like(dw_ref)
            db_ref[...] = jnp.zeros_like(db_ref)

        acc = jnp.zeros((TT, tc), f32)
        for k in range(width):
            acc = acc + w_ref[k:k + 1, :] * dcp[pl.ds((width - 1) - k, TT), :]
            dw_ref[k:k + 1, :] += jnp.sum(dcv * xp[pl.ds(8 - (width - 1) + k, TT), :], axis=0, keepdims=True)
        dx_ref[...] = acc.astype(dx_dtype)
        db_ref[...] += jnp.sum(dcv, axis=0, keepdims=True)

    in_specs = [
        pl.BlockSpec((TT, tc), lambda j, i: (i, j)),
        pl.BlockSpec((8, tc), lambda j, i: (jnp.minimum((i + 1) * hb, t // 8 - 1), j)),
        pl.BlockSpec((TT, tc), lambda j, i: (i, j + col0)),
        pl.BlockSpec((8, tc), lambda j, i: (jnp.maximum(i * hb - 1, 0), j + col0)),
        pl.BlockSpec((width, tc), lambda j, i: (0, j)),
    ]
    out_specs = [
        pl.BlockSpec((TT, tc), lambda j, i: (i, j)),
        pl.BlockSpec((8, tc), lambda j, i: (0, j)),
        pl.BlockSpec((1, tc), lambda j, i: (0, j)),
    ]
    return pl.pallas_call(
        body, grid=(c // tc, nt), in_specs=in_specs, out_specs=out_specs,
        out_shape=[_sds((t, c), dx_dtype), _sds((8, c), f32), _sds((1, c), f32)],
        scratch_shapes=[pltpu.VMEM((TT + 8, tc), f32), pltpu.VMEM((TT + 8, tc), f32)],
        compiler_params=_cp(), name=name)(dc, dc, x, x, w)


def _bdot_impl(a, b, dims):
    return _dot(a.astype(bf16), b.astype(bf16), dims)


@functools.partial(jax.custom_vjp, nondiff_argnums=(2,))
def _bdot(a, b, dims):
    return _bdot_impl(a, b, dims)


def _bdot_fwd(a, b, dims):
    return _bdot_impl(a, b, dims), (a, b)


def _bdot_bwd(dims, res, ct):
    a, b = res
    if dims == NN:
        return _bdot_impl(ct, b, NT), _bdot_impl(a, ct, TN)
    if dims == NT:
        return _bdot_impl(ct, b, NN), _bdot_impl(ct, a, TN)
    return _bdot_impl(b, ct, NT), _bdot_impl(a, ct, NN)


_bdot.defvjp(_bdot_fwd, _bdot_bwd)


def _split2(a):
    hi = a.astype(bf16)
    return hi, (a - hi.astype(f32)).astype(bf16)


def _dot3_impl(a, b, dims):
    a_hi, a_lo = _split2(a)
    b_hi, b_lo = _split2(b)
    return (_dot(a_hi, b_hi, dims) + _dot(a_hi, b_lo, dims)) + _dot(a_lo, b_hi, dims)


@functools.partial(jax.custom_vjp, nondiff_argnums=(2,))
def _dot3(a, b, dims):
    return _dot3_impl(a, b, dims)


def _dot3_fwd(a, b, dims):
    return _dot3_impl(a, b, dims), (a, b)


def _dot3_bwd(dims, res, ct):
    a, b = res
    if dims == NN:
        return _dot3_impl(ct, b, NT), _dot3_impl(a, ct, TN)
    if dims == NT:
        return _dot3_impl(ct, b, NN), _dot3_impl(ct, a, TN)
    return _dot3_impl(b, ct, NT), _dot3_impl(a, ct, NN)


_dot3.defvjp(_dot3_fwd, _dot3_bwd)


def _tril_dot_impl(tril, x, dims):
    t = tril.astype(bf16)
    x1 = x.astype(bf16)
    r1 = x - x1.astype(f32)
    x2 = r1.astype(bf16)
    x3 = (r1 - x2.astype(f32)).astype(bf16)
    return (_dot(t, x3, dims) + _dot(t, x2, dims)) + _dot(t, x1, dims)


@jax.custom_vjp
def _cumsum_rows(tril, x):
    return _tril_dot_impl(tril, x, NN)


def _cumsum_rows_fwd(tril, x):
    return _tril_dot_impl(tril, x, NN), tril


def _cumsum_rows_bwd(tril, ct):
    return jnp.zeros_like(tril), _tril_dot_impl(tril, ct, TN)


_cumsum_rows.defvjp(_cumsum_rows_fwd, _cumsum_rows_bwd)


def _gdn_intra(cq, ck, cv, ba, alog_v, dtb_v, hd):
    c = GDN_CHUNK
    lane = lax.broadcasted_iota(jnp.int32, (1, HEAD), 1)
    mb = (lane == hd).astype(f32)
    ma = (lane == hd + NH).astype(f32)
    beta = jax.nn.sigmoid(jnp.sum(ba * mb, axis=1, keepdims=True))
    alpha = jnp.sum(ba * ma, axis=1, keepdims=True)
    alog = jnp.sum(alog_v * ma, axis=1, keepdims=True)
    dtb = jnp.sum(dtb_v * ma, axis=1, keepdims=True)
    g = -jnp.exp(alog) * jax.nn.softplus(alpha + dtb)
    q = _silu(cq)
    q = q * lax.rsqrt(jnp.sum(q * q, axis=-1, keepdims=True) + EPS) * (HEAD ** -0.5)
    k = _silu(ck)
    k = k * lax.rsqrt(jnp.sum(k * k, axis=-1, keepdims=True) + EPS)
    v = _silu(cv)
    row = lax.broadcasted_iota(jnp.int32, (c, c), 0)
    col = lax.broadcasted_iota(jnp.int32, (c, c), 1)
    causal = row >= col
    tril = causal.astype(f32)
    gc = _cumsum_rows(tril, jnp.broadcast_to(g, (c, HEAD)))
    gcc = _cumsum_rows(tril, jnp.broadcast_to(g, (c, c)))
    diff = jnp.where(causal, gcc - gcc.T, 0.0)
    decay = jnp.where(causal, jnp.exp(diff), 0.0)
    kb = k * beta
    nmat = -jnp.where(row > col, _bdot(kb, k, NT) * decay, 0.0)
    egc = jnp.exp(gc)
    sol = jnp.concatenate([v * beta, kb * egc], axis=1)
    p = nmat
    for it in range(6):
        sol = sol + _dot3(p, sol, NN)
        if it < 5:
            p = _dot3(p, p, NN)
    u, w = sol[:, :HEAD], sol[:, HEAD:]
    attn = _bdot(q, k, NT) * decay
    rowv = lax.broadcasted_iota(jnp.int32, (c, 1), 0)
    gc_last = jnp.sum(jnp.where(rowv == c - 1, gc, 0.0), axis=0, keepdims=True)
    return u, w, q * egc, k * jnp.exp(gc_last - gc), attn, jnp.exp(gc_last)


def _gdn_seq(u, w, q_dec, k_dec, attn, dl, z, s, gain):
    v_new = u - _bdot(w, s, NN)
    o = _bdot(q_dec, s, NN) + _bdot(attn, v_new, NN)
    s_new = s * dl + _bdot(k_dec, v_new, TN)
    return _rms(o, gain) * _silu(z), s_new


def _hsl(h):
    return slice(h * HEAD, (h + 1) * HEAD)


def _gdn2_fwd(conv, p, alog_v, dtb_v, gain, name):
    t = conv.shape[0]
    c = GDN_CHUNK
    nch = t // c
    w512 = NH * HEAD
    wide = lambda off: pl.BlockSpec((c, w512), lambda n: (n, off))
    vec = pl.BlockSpec((1, HEAD), lambda n: (0, 0))
    attn_spec = pl.BlockSpec((1, NH, c, c), lambda n: (n, 0, 0, 0))
    dl_spec = pl.BlockSpec((1, NH, HEAD), lambda n: (n, 0, 0))

    cps = GDN_INTRA_CHUNKS
    iwide = lambda off: pl.BlockSpec((cps * c, w512), lambda n: (n, off))

    def intra(cq, ck, cv, ba, al, dt, u_ref, w_ref, qd_ref, kd_ref, at_ref, dl_ref):
        for ci in range(cps):
            rows = slice(ci * c, (ci + 1) * c)
            for h in range(NH):
                u, w, qd, kd, at, dl = _gdn_intra(cq[rows, _hsl(h)], ck[rows, _hsl(h)], cv[rows, _hsl(h)],
                                                  ba[rows, :], al[...], dt[...], h)
                u_ref[rows, _hsl(h)] = u
                w_ref[rows, _hsl(h)] = w
                qd_ref[rows, _hsl(h)] = qd
                kd_ref[rows, _hsl(h)] = kd
                at_ref[ci, h] = at
                dl_ref[ci, h:h + 1, :] = dl

    u, w, qd, kd, at, dl = pl.pallas_call(
        intra, grid=(nch // cps,),
        in_specs=[iwide(0), iwide(1), iwide(2), pl.BlockSpec((cps * c, HEAD), lambda n: (n, C_BA // HEAD)), vec, vec],
        out_specs=[iwide(0)] * 4 + [pl.BlockSpec((cps, NH, c, c), lambda n: (n, 0, 0, 0)),
                                    pl.BlockSpec((cps, NH, HEAD), lambda n: (n, 0, 0))],
        out_shape=[_sds((t, w512), f32)] * 4 + [_sds((nch, NH, c, c), f32), _sds((nch, NH, HEAD), f32)],
        compiler_params=_cp(), name=name + "_intra")(conv, conv, conv, p, alog_v, dtb_v)

    def seq(u_ref, w_ref, qd_ref, kd_ref, at_ref, dl_ref, z_ref, gn, o_ref, ss_ref, s_scr):
        @pl.when(pl.program_id(0) == 0)
        def _():
            s_scr[...] = jnp.zeros_like(s_scr)

        for h in range(NH):
            s = s_scr[h]
            ss_ref[0, h] = s
            o, s_new = _gdn_seq(u_ref[:, _hsl(h)], w_ref[:, _hsl(h)], qd_ref[:, _hsl(h)], kd_ref[:, _hsl(h)],
                                at_ref[0, h], dl_ref[0, h:h + 1, :], z_ref[:, _hsl(h)], s, gn[...])
            o_ref[:, _hsl(h)] = o.astype(bf16)
            s_scr[h] = s_new

    o, states = pl.pallas_call(
        seq, grid=(nch,),
        in_specs=[wide(0)] * 4 + [attn_spec, dl_spec, wide(C_Z // w512), vec],
        out_specs=[wide(0), pl.BlockSpec((1, NH, HEAD, HEAD), lambda n: (n, 0, 0, 0))],
        out_shape=[_sds((t, w512), bf16), _sds((nch, NH, HEAD, HEAD), f32)],
        scratch_shapes=[pltpu.VMEM((NH, HEAD, HEAD), f32)],
        compiler_params=_cp(), name=name + "_seq")(u, w, qd, kd, at, dl, p, gain)
    return o, dict(u=u, w=w, qd=qd, kd=kd, at=at, dl=dl, states=states)


def _gdn2_bwd(conv, p, alog_v, dtb_v, gain, saved, do, name):
    t = conv.shape[0]
    c = GDN_CHUNK
    nch = t // c
    w512 = NH * HEAD
    rwide = lambda off: pl.BlockSpec((c, w512), lambda n: (nch - 1 - n, off))
    rvec = pl.BlockSpec((1, HEAD), lambda n: (0, 0))
    rattn = pl.BlockSpec((1, NH, c, c), lambda n: (nch - 1 - n, 0, 0, 0))
    rdl = pl.BlockSpec((1, NH, HEAD), lambda n: (nch - 1 - n, 0, 0))

    def seq_bwd(u_ref, w_ref, qd_ref, kd_ref, at_ref, dl_ref, z_ref, gn, ss_ref, do_ref,
                du_ref, dw_ref, dqd_ref, dkd_ref, dat_ref, ddl_ref, dz_ref, dgn_ref, ds_scr):
        @pl.when(pl.program_id(0) == 0)
        def _():
            ds_scr[...] = jnp.zeros_like(ds_scr)
            dgn_ref[...] = jnp.zeros_like(dgn_ref)

        dgn = jnp.zeros((1, HEAD), f32)
        for h in range(NH):
            _, vjp = jax.vjp(_gdn_seq, u_ref[:, _hsl(h)], w_ref[:, _hsl(h)], qd_ref[:, _hsl(h)], kd_ref[:, _hsl(h)],
                             at_ref[0, h], dl_ref[0, h:h + 1, :], z_ref[:, _hsl(h)], ss_ref[0, h], gn[...])
            du, dw, dqd, dkd, dat, ddl, dz, ds, dg = vjp((do_ref[:, _hsl(h)], ds_scr[h]))
            du_ref[:, _hsl(h)] = du
            dw_ref[:, _hsl(h)] = dw
            dqd_ref[:, _hsl(h)] = dqd
            dkd_ref[:, _hsl(h)] = dkd
            dat_ref[0, h] = dat
            ddl_ref[0, h:h + 1, :] = ddl
            dz_ref[:, _hsl(h)] = dz
            ds_scr[h] = ds
            dgn = dgn + dg
        dgn_ref[...] += dgn

    du, dw, dqd, dkd, dat, ddl, dz, dgn = pl.pallas_call(
        seq_bwd, grid=(nch,),
        in_specs=[rwide(0)] * 4 + [rattn, rdl, rwide(C_Z // w512), rvec,
                                   pl.BlockSpec((1, NH, HEAD, HEAD), lambda n: (nch - 1 - n, 0, 0, 0)), rwide(0)],
        out_specs=[rwide(0)] * 4 + [rattn, rdl, rwide(0), rvec],
        out_shape=[_sds((t, w512), f32)] * 4 + [_sds((nch, NH, c, c), f32), _sds((nch, NH, HEAD), f32),
                                                _sds((t, w512), f32), _sds((1, HEAD), f32)],
        scratch_shapes=[pltpu.VMEM((NH, HEAD, HEAD), f32)],
        compiler_params=_cp(), name=name + "_seq")(
            saved["u"], saved["w"], saved["qd"], saved["kd"], saved["at"], saved["dl"], p, gain, saved["states"], do)

    cps = GDN_INTRA_CHUNKS
    wide = lambda off: pl.BlockSpec((cps * c, w512), lambda n: (n, off))
    vec = pl.BlockSpec((1, HEAD), lambda n: (0, 0))
    attn_spec = pl.BlockSpec((cps, NH, c, c), lambda n: (n, 0, 0, 0))
    dl_spec = pl.BlockSpec((cps, NH, HEAD), lambda n: (n, 0, 0))

    def intra_bwd(cq, ck, cv, ba, al, dt, du_ref, dw_ref, dqd_ref, dkd_ref, dat_ref, ddl_ref,
                  dc_ref, dba_ref, dal_ref, ddt_ref):
        @pl.when(pl.program_id(0) == 0)
        def _():
            dal_ref[...] = jnp.zeros_like(dal_ref)
            ddt_ref[...] = jnp.zeros_like(ddt_ref)

        dal = jnp.zeros((1, HEAD), f32)
        ddt = jnp.zeros((1, HEAD), f32)
        for ci in range(cps):
            rows = slice(ci * c, (ci + 1) * c)
            dba = jnp.zeros((c, HEAD), f32)
            for h in range(NH):
                fn = functools.partial(_gdn_intra, hd=h)
                _, vjp = jax.vjp(fn, cq[rows, _hsl(h)], ck[rows, _hsl(h)], cv[rows, _hsl(h)], ba[rows, :], al[...], dt[...])
                g_q, g_k, g_v, g_ba, g_al, g_dt = vjp((du_ref[rows, _hsl(h)], dw_ref[rows, _hsl(h)], dqd_ref[rows, _hsl(h)],
                                                       dkd_ref[rows, _hsl(h)], dat_ref[ci, h], ddl_ref[ci, h:h + 1, :]))
                dc_ref[rows, _hsl(h)] = g_q
                dc_ref[rows, _hsl(NH + h)] = g_k
                dc_ref[rows, _hsl(2 * NH + h)] = g_v
                dba = dba + g_ba
                dal = dal + g_al
                ddt = ddt + g_dt
            dba_ref[rows, :] = dba
        dal_ref[...] += dal
        ddt_ref[...] += ddt

    dconv, dba, dal, ddt = pl.pallas_call(
        intra_bwd, grid=(nch // cps,),
        in_specs=[wide(0), wide(1), wide(2), pl.BlockSpec((cps * c, HEAD), lambda n: (n, C_BA // HEAD)), vec, vec]
        + [wide(0)] * 4 + [attn_spec, dl_spec],
        out_specs=[pl.BlockSpec((cps * c, 3 * w512), lambda n: (n, 0)), pl.BlockSpec((cps * c, HEAD), lambda n: (n, 0)),
                   vec, vec],
        out_shape=[_sds((t, 3 * w512), f32), _sds((t, HEAD), f32), _sds((1, HEAD), f32), _sds((1, HEAD), f32)],
        compiler_params=_cp(), name=name + "_intra")(conv, conv, conv, p, alog_v, dtb_v, du, dw, dqd, dkd, dat, ddl)
    return dconv, dz, dba, dal, ddt, dgn


def _hgrn_chunk(qb, fb, ib, gb, st, lb, gain):
    c = HGRN_CHUNK
    f = lb + (1.0 - lb) * jax.nn.sigmoid(fb)
    logf = jnp.log(jnp.maximum(f, F_FLOOR))
    k = 1.0 - f
    q = _silu(qb)
    v = ib
    row = lax.broadcasted_iota(jnp.int32, (c, c), 0)
    col = lax.broadcasted_iota(jnp.int32, (c, c), 1)
    b = _cumsum_rows((row >= col).astype(f32), logf)
    ri = lax.broadcasted_iota(jnp.int32, (c, 1), 0)
    o = jnp.zeros((c, HEAD), f32)
    for j in range(c):
        mj = ri == j
        bj = jnp.sum(jnp.where(mj, b, 0.0), axis=0, keepdims=True)
        kj = jnp.sum(jnp.where(mj, k, 0.0), axis=0, keepdims=True)
        vj = jnp.sum(jnp.where(mj, v, 0.0), axis=0, keepdims=True)
        ok = ri >= j
        e = jnp.where(ok, jnp.exp(jnp.where(ok, b - bj, 0.0)), 0.0)
        o = o + jnp.sum(q * kj * e, axis=1, keepdims=True) * vj
    b_last = jnp.sum(jnp.where(ri == c - 1, b, 0.0), axis=0, keepdims=True)
    q_dec = q * jnp.exp(b)
    k_dec = k * jnp.exp(b_last - b)
    o = o + _bdot(q_dec, st, NT)
    st_new = st * jnp.exp(b_last) + _bdot(v, k_dec, TN)
    o = _rms(o, gain) * _silu(gb)
    return o, st_new


def _hgrn_fwd(p, lb, gain, name):
    t = p.shape[0]
    r = HGRN_STEP
    ns = t // r
    nsub = r // HGRN_CHUNK
    blk = lambda off: pl.BlockSpec((r, HEAD), lambda n, h: (n, off // HEAD + h))

    def body(qb, fb, ib, gb, lb_ref, gn, o_ref, ss_ref, s_scr):
        n, h = pl.program_id(0), pl.program_id(1)

        @pl.when(n == 0)
        def _():
            s_scr[h] = jnp.zeros((HEAD, HEAD), f32)

        st = s_scr[h]
        ss_ref[0, 0] = st
        for ch in range(nsub):
            rows = pl.ds(ch * HGRN_CHUNK, HGRN_CHUNK)
            o, st = _hgrn_chunk(qb[rows, :], fb[rows, :], ib[rows, :], gb[rows, :], st, lb_ref[...], gn[...])
            o_ref[rows, :] = o.astype(bf16)
        s_scr[h] = st

    return pl.pallas_call(
        body, grid=(ns, NH),
        in_specs=[blk(C_QB), blk(C_FB), blk(C_IB), blk(C_GB),
                  pl.BlockSpec((1, HEAD), lambda n, h: (0, h)), pl.BlockSpec((1, HEAD), lambda n, h: (0, 0))],
        out_specs=[pl.BlockSpec((r, HEAD), lambda n, h: (n, h)),
                   pl.BlockSpec((1, 1, HEAD, HEAD), lambda n, h: (n, h, 0, 0))],
        out_shape=[_sds((t, NH * HEAD), bf16), _sds((ns, NH, HEAD, HEAD), f32)],
        scratch_shapes=[pltpu.VMEM((NH, HEAD, HEAD), f32)],
        compiler_params=_cp(), name=name)(p, p, p, p, lb, gain)


def _hgrn_bwd(p, lb, gain, states, do, name):
    t = p.shape[0]
    r = HGRN_STEP
    ns = t // r
    nsub = r // HGRN_CHUNK
    blk = lambda off: pl.BlockSpec((r, HEAD), lambda n, h: (ns - 1 - n, off // HEAD + h))
    hblk = pl.BlockSpec((r, HEAD), lambda n, h: (ns - 1 - n, h))

    def body(qb, fb, ib, gb, lb_ref, gn, ss_ref, do_ref,
             dqb, dfb, dib, dgb, dlb_ref, dgn_ref, ds_scr, st_scr):
        n, h = pl.program_id(0), pl.program_id(1)

        @pl.when(n == 0)
        def _():
            ds_scr[h] = jnp.zeros((HEAD, HEAD), f32)

        @pl.when((n == 0) & (h == 0))
        def _():
            dgn_ref[...] = jnp.zeros_like(dgn_ref)

        lbv, gnv = lb_ref[...], gn[...]
        st = ss_ref[0, 0]
        for ch in range(nsub):
            rows = pl.ds(ch * HGRN_CHUNK, HGRN_CHUNK)
            st_scr[ch] = st
            if ch < nsub - 1:
                _, st = _hgrn_chunk(qb[rows, :], fb[rows, :], ib[rows, :], gb[rows, :], st, lbv, gnv)
        ds = ds_scr[h]
        dlb = jnp.zeros((1, HEAD), f32)
        dgn = jnp.zeros((1, HEAD), f32)
        for ch in reversed(range(nsub)):
            rows = pl.ds(ch * HGRN_CHUNK, HGRN_CHUNK)
            _, vjp = jax.vjp(_hgrn_chunk, qb[rows, :], fb[rows, :], ib[rows, :], gb[rows, :], st_scr[ch], lbv, gnv)
            g_q, g_f, g_i, g_g, ds, g_lb, g_gn = vjp((do_ref[rows, :], ds))
            dqb[rows, :] = g_q
            dfb[rows, :] = g_f
            dib[rows, :] = g_i
            dgb[rows, :] = g_g
            dlb = dlb + g_lb
            dgn = dgn + g_gn
        ds_scr[h] = ds
        dlb_ref[0] = dlb
        dgn_ref[...] += dgn

    return pl.pallas_call(
        body, grid=(ns, NH),
        in_specs=[blk(C_QB), blk(C_FB), blk(C_IB), blk(C_GB),
                  pl.BlockSpec((1, HEAD), lambda n, h: (0, h)), pl.BlockSpec((1, HEAD), lambda n, h: (0, 0)),
                  pl.BlockSpec((1, 1, HEAD, HEAD), lambda n, h: (ns - 1 - n, h, 0, 0)),
                  pl.BlockSpec((r, HEAD), lambda n, h: (ns - 1 - n, NH + h))],
        out_specs=[hblk, hblk, hblk, hblk,
                   pl.BlockSpec((1, 1, HEAD), lambda n, h: (n, 0, h)),
                   pl.BlockSpec((1, HEAD), lambda n, h: (0, 0))],
        out_shape=[_sds((t, 512), f32)] * 4 + [_sds((ns, 1, 512), f32), _sds((1, HEAD), f32)],
        scratch_shapes=[pltpu.VMEM((NH, HEAD, HEAD), f32), pltpu.VMEM((nsub, HEAD, HEAD), f32)],
        compiler_params=_cp(), name=name)(p, p, p, p, lb, gain, states, do)


def _lru_gates(xb, wa, wx, ba, bx, lam):
    xh = xb.astype(bf16)
    r = jax.nn.sigmoid(_dot(xh, wa.astype(bf16), NN) + ba)
    i = jax.nn.sigmoid(_dot(xh, wx.astype(bf16), NN) + bx)
    log_a = -RG_C * r * jax.nn.softplus(-lam)
    a = jnp.exp(log_a)
    t2 = 2.0 * log_a
    series = -t2 * (1.0 + t2 * (0.5 + t2 * (1.0 / 6.0 + t2 * (1.0 / 24.0))))
    om = jnp.where(t2 > -1e-2, series, 1.0 - jnp.exp(t2))
    u = jnp.sqrt(jnp.maximum(om, 0.0)) * (i * xb)
    return a, u


def _lru_gates_fwd(xc, wa, wx, ba, bx, lam, name):
    t = xc.shape[0]
    blk = pl.BlockSpec((TT, LRU_BLOCK), lambda h, i: (i, h))
    wsp = pl.BlockSpec((1, LRU_BLOCK, LRU_BLOCK), lambda h, i: (h, 0, 0))
    vsp = pl.BlockSpec((1, LRU_BLOCK), lambda h, i: (0, h))

    def body(x_ref, wa_ref, wx_ref, ba_ref, bx_ref, lam_ref, a_ref, u_ref):
        a, u = _lru_gates(x_ref[...], wa_ref[0], wx_ref[0], ba_ref[...], bx_ref[...], lam_ref[...])
        a_ref[...] = a
        u_ref[...] = u

    return pl.pallas_call(
        body, grid=(NH, t // TT), in_specs=[blk, wsp, wsp, vsp, vsp, vsp], out_specs=[blk, blk],
        out_shape=[_sds((t, D), f32)] * 2, compiler_params=_cp(), name=name)(xc, wa, wx, ba, bx, lam)


def _lru_gates_bwd(xc, wa, wx, ba, bx, lam, da, du, name):
    t = xc.shape[0]
    blk = pl.BlockSpec((TT, LRU_BLOCK), lambda h, i: (i, h))
    wsp = pl.BlockSpec((1, LRU_BLOCK, LRU_BLOCK), lambda h, i: (h, 0, 0))
    vsp = pl.BlockSpec((1, LRU_BLOCK), lambda h, i: (0, h))

    def body(x_ref, wa_ref, wx_ref, ba_ref, bx_ref, lam_ref, da_ref, du_ref,
             dx_ref, dwa_ref, dwx_ref, dba_ref, dbx_ref, dlam_ref):
        @pl.when(pl.program_id(1) == 0)
        def _():
            for r in (dwa_ref, dwx_ref, dba_ref, dbx_ref, dlam_ref):
                r[...] = jnp.zeros_like(r)

        _, vjp = jax.vjp(_lru_gates, x_ref[...], wa_ref[0], wx_ref[0], ba_ref[...], bx_ref[...], lam_ref[...])
        dx, dwa, dwx, dba, dbx, dlam = vjp((da_ref[...], du_ref[...]))
        dx_ref[...] = dx
        dwa_ref[0] += dwa
        dwx_ref[0] += dwx
        dba_ref[...] += dba
        dbx_ref[...] += dbx
        dlam_ref[...] += dlam

    return pl.pallas_call(
        body, grid=(NH, t // TT), in_specs=[blk, wsp, wsp, vsp, vsp, vsp, blk, blk],
        out_specs=[blk, wsp, wsp, vsp, vsp, vsp],
        out_shape=[_sds((t, D), f32), _sds((NH, LRU_BLOCK, LRU_BLOCK), f32), _sds((NH, LRU_BLOCK, LRU_BLOCK), f32),
                   _sds((1, D), f32), _sds((1, D), f32), _sds((1, D), f32)],
        compiler_params=_cp(), name=name)(xc, wa, wx, ba, bx, lam, da, du)


_SCAN_SHIFTS = (1, 2, 4, 8, 16, 32, 64, 128)
_SCAN_PAD = 128


def _gelu(y):
    return jax.nn.gelu(y, approximate=True)


def _lru_scan_fwd(a, u, p2, name):
    t = a.shape[0]
    tc = 128
    blk = pl.BlockSpec((TT, tc), lambda j, i: (i, j))

    def body(a_ref, u_ref, y_ref, h_ref, hg_ref, a_s, b_s, carry):
        i = pl.program_id(1)

        @pl.when(i == 0)
        def _():
            carry[...] = jnp.zeros_like(carry)
            a_s[0:_SCAN_PAD, :] = jnp.ones((_SCAN_PAD, tc), f32)
            b_s[0:_SCAN_PAD, :] = jnp.zeros((_SCAN_PAD, tc), f32)

        av, bv = a_ref[...], u_ref[...]
        for s in _SCAN_SHIFTS:
            a_s[_SCAN_PAD:, :] = av
            b_s[_SCAN_PAD:, :] = bv
            ash = a_s[pl.ds(_SCAN_PAD - s, TT), :]
            bsh = b_s[pl.ds(_SCAN_PAD - s, TT), :]
            bv = bv + av * bsh
            av = av * ash
        h = bv + av * carry[7:8, :]
        h_ref[...] = h
        hg_ref[...] = (h * _gelu(y_ref[...])).astype(bf16)
        carry[...] = h[TT - 8:, :]

    return pl.pallas_call(
        body, grid=(D // tc, t // TT), in_specs=[blk, blk, blk], out_specs=[blk, blk],
        out_shape=[_sds((t, D), f32), _sds((t, D), bf16)],
        scratch_shapes=[pltpu.VMEM((_SCAN_PAD + TT, tc), f32), pltpu.VMEM((_SCAN_PAD + TT, tc), f32),
                        pltpu.VMEM((8, tc), f32)],
        compiler_params=_cp(), name=name)(a, u, p2)


def _lru_scan_bwd(a, h, p2, dhg, name):
    t = a.shape[0]
    tc = 128
    nt = t // TT
    hb = TT // 8
    rblk = pl.BlockSpec((TT, tc), lambda j, i: (nt - 1 - i, j))

    def body(a_ref, an_ref, h_ref, hp_ref, y_ref, dhg_ref, du_ref, da_ref, dy_ref, a_s, b_s, ap, hp, carry):
        i = pl.program_id(1)

        @pl.when(i == 0)
        def _():
            carry[...] = jnp.zeros_like(carry)
            a_s[TT:, :] = jnp.ones((_SCAN_PAD, tc), f32)
            b_s[TT:, :] = jnp.zeros((_SCAN_PAD, tc), f32)

        ap[0:TT, :] = a_ref[...]
        ap[TT:, :] = jnp.where(i == 0, 0.0, an_ref[...])
        hp[0:8, :] = jnp.where(i == nt - 1, 0.0, hp_ref[...])
        hp[8:, :] = h_ref[...]
        y = y_ref[...]
        gate, gvjp = jax.vjp(_gelu, y)
        dhg_v = dhg_ref[...]
        dy_ref[...] = gvjp(dhg_v * h_ref[...])[0]
        av = ap[pl.ds(1, TT), :]
        bv = dhg_v * gate
        for s in _SCAN_SHIFTS:
            a_s[0:TT, :] = av
            b_s[0:TT, :] = bv
            ash = a_s[pl.ds(s, TT), :]
            bsh = b_s[pl.ds(s, TT), :]
            bv = bv + av * bsh
            av = av * ash
        g = bv + av * carry[0:1, :]
        du_ref[...] = g
        da_ref[...] = g * hp[pl.ds(7, TT), :]
        carry[...] = g[0:8, :]

    in_specs = [
        rblk,
        pl.BlockSpec((8, tc), lambda j, i: (jnp.minimum((nt - i) * hb, t // 8 - 1), j)),
        rblk,
        pl.BlockSpec((8, tc), lambda j, i: (jnp.maximum((nt - 1 - i) * hb - 1, 0), j)),
        rblk, rblk,
    ]
    return pl.pallas_call(
        body, grid=(D // tc, nt), in_specs=in_specs, out_specs=[rblk, rblk, rblk],
        out_shape=[_sds((t, D), f32)] * 3,
        scratch_shapes=[pltpu.VMEM((TT + _SCAN_PAD, tc), f32), pltpu.VMEM((TT + _SCAN_PAD, tc), f32),
                        pltpu.VMEM((TT + 8, tc), f32), pltpu.VMEM((TT + 8, tc), f32), pltpu.VMEM((8, tc), f32)],
        compiler_params=_cp(), name=name)(a, a, h, h, p2, dhg)


def _ffn_act_bwd(gc, up, dact, name, tc=256):
    t = gc.shape[0]
    blk = pl.BlockSpec((TT, tc), lambda i, j: (i, j))

    def body(gc_ref, v_ref, da_ref, dgc_ref, dv_ref):
        _, vjp = jax.vjp(lambda g, v: _silu(g) * v, gc_ref[...], v_ref[...])
        dg, dv = vjp(da_ref[...])
        dgc_ref[...] = dg
        dv_ref[...] = dv.astype(bf16)

    return pl.pallas_call(
        body, grid=(t // TT, D_FF // tc),
        in_specs=[blk, pl.BlockSpec((TT, tc), lambda i, j: (i, j + D_FF // tc)), blk], out_specs=[blk, blk],
        out_shape=[_sds((t, D_FF), f32), _sds((t, D_FF), bf16)], compiler_params=_cp(), name=name)(gc, up, dact)


def _lower_bounds_fwd(w):
    def body(w_ref, o0_ref, o1_ref):
        wv = w_ref[...]
        o0, o1 = _lb_rows(wv[0:1, :], wv[1:2, :])
        o0_ref[...] = o0
        o1_ref[...] = o1

    return pl.pallas_call(body, out_shape=[_sds((1, 512), f32)] * 2, name="lower_bounds_fwd")(w)


def _lb_rows(w0, w1):
    m = jnp.maximum(w0, w1)
    e0, e1 = jnp.exp(w0 - m), jnp.exp(w1 - m)
    s = e0 + e1
    p0, p1 = e0 / s, e1 / s
    return p0 - p0, (p0 + p1) - p0


def _lower_bounds_bwd(w, d0, d1):
    def body(w_ref, d0_ref, d1_ref, g0_ref, g1_ref):
        wv = w_ref[...]
        _, vjp = jax.vjp(_lb_rows, wv[0:1, :], wv[1:2, :])
        g0, g1 = vjp((d0_ref[...], d1_ref[...]))
        g0_ref[...] = g0
        g1_ref[...] = g1

    return pl.pallas_call(body, out_shape=[_sds((1, 512), f32)] * 2, name="lower_bounds_bwd")(w, d0, d1)


def _local_step(x, target, wt, pre_layer=None, post_grads=None):
    depth = 4
    res = []
    lb0, lb1 = _lower_bounds_fwd(wt["hgrn_lower_bounds"])
    lbs = [lb0, lb1]
    for layer in range(depth):
        j = layer // 2
        sv = {"x_in": x}
        deps = pre_layer(layer, x) if pre_layer else ()
        h1 = _rmsnorm_fwd(x, wt["norm_mix"][layer], "rms_fwd", deps)
        sv["h1"] = h1
        if layer % 2 == 0:
            p = _mm_auto(h1, wt["ab_w_in"][j], "nn", f32, "mm_ab_in")
            conv = _conv_fwd(p, 0, 1536, wt["gdn_conv_w"][j], jnp.zeros((1, 1536), f32), "gdn_conv_fwd", tc=768)
            o_a, s_a = _gdn2_fwd(conv, p, wt["alog_v"][j], wt["dtb_v"][j], wt["gdn_norm"][j], "gdn_fwd")
            o_b, s_b = _hgrn_fwd(p, lbs[j], wt["hgrn_norm"][j], "hgrn_fwd")
            o = jnp.concatenate([o_a, o_b], axis=1)
            mix = _mm_auto(o, wt["ab_w_out"][j], "nn", f32, "mm_ab_out")
            sv.update(p=p, conv=conv, s_a=s_a, s_b=s_b, o=o)
        else:
            p2 = _mm_auto(h1, wt["c_w_in"][j], "nn", f32, "mm_c_in")
            xc = _conv_fwd(p2, 1, D, wt["c_conv_w"][j], wt["c_conv_b"][j], "lru_conv_fwd", tc=D)
            a, u = _lru_gates_fwd(xc, wt["c_gate_a_w"][j], wt["c_gate_x_w"][j], wt["c_gate_a_b"][j],
                                  wt["c_gate_x_b"][j], wt["c_lambda"][j], "lru_gates_fwd")
            h, hg = _lru_scan_fwd(a, u, p2, "lru_scan_fwd")
            mix = _mm_auto(hg, wt["c_w_out"][j], "nn", f32, "mm_c_out")
            sv.update(p2=p2, xc=xc, a=a, h=h, hg=hg)
        x = _residual_add(x, mix, "res_add")
        sv["x_mid"] = x
        h2 = _rmsnorm_fwd(x, wt["norm_ffn"][layer], "rms_fwd")
        up = _mm_auto(h2, wt["ffn_w_up"][layer], "nn", f32, "mm_up")
        gc, act = _conv_fwd(up, 0, D_FF, wt["ffn_conv_w"][layer], wt["ffn_conv_b"][layer], "ffn_conv_fwd",
                            tc=D_FF // 2, val=up, val_col0=2)
        down = _mm_auto(act, wt["ffn_w_down"][layer], "nn", f32, "mm_down")
        x = _residual_add(x, down, "res_add")
        sv.update(h2=h2, up=up, gc=gc, act=act)
        res.append(sv)

    loss, dx, dxb, d_norm_final = _final_loss(x, wt["norm_final"], target, "final_loss")

    g = {k: [None] * len(v) for k, v in wt.items() if isinstance(v, list)}
    g["norm_final"] = d_norm_final
    d_lbs = [None, None]
    for layer in reversed(range(depth)):
        j = layer // 2
        sv = res[layer]
        dact = _mm_auto(dxb, wt["ffn_w_down"][layer], "nt", f32, "mm_down_dx")
        g["ffn_w_down"][layer] = _mm_auto(sv["act"], dxb, "tn", bf16, "mm_down_dw")
        dgc, dval = _ffn_act_bwd(sv["gc"], sv["up"], dact, "ffn_act_bwd", tc=D_FF // 2)
        dgate, dcw, dcb = _conv_bwd(dgc, sv["up"], 0, wt["ffn_conv_w"][layer], "ffn_conv_bwd", tc=D_FF // 2)
        g["ffn_conv_w"][layer] = dcw[:3]
        g["ffn_conv_b"][layer] = dcb
        dup = jnp.concatenate([dgate, dval], axis=1)
        g["ffn_w_up"][layer] = _mm_auto(sv["h2"], dup, "tn", bf16, "mm_up_dw")
        deps = post_grads(layer, "ffn", g) if post_grads else ()
        dh2 = _mm_auto(dup, wt["ffn_w_up"][layer], "nt", f32, "mm_up_dx")
        dx, dxb, g["norm_ffn"][layer] = _rmsnorm_bwd(sv["x_mid"], wt["norm_ffn"][layer], dh2, dx, "rms_bwd", deps)
        if layer % 2 == 0:
            do = _mm_auto(dxb, wt["ab_w_out"][j], "nt", f32, "mm_ab_out_dx")
            g["ab_w_out"][j] = _mm_auto(sv["o"], dxb, "tn", bf16, "mm_ab_out_dw")
            dconv, dz, dba, dal, ddt, dgn = _gdn2_bwd(
                sv["conv"], sv["p"], wt["alog_v"][j], wt["dtb_v"][j], wt["gdn_norm"][j], sv["s_a"], do, "gdn_bwd")
            g["alog_v"][j], g["dtb_v"][j], g["gdn_norm"][j] = dal, ddt, dgn
            dqkv, dcw, _ = _conv_bwd(dconv, sv["p"], 0, wt["gdn_conv_w"][j], "gdn_conv_bwd", tc=768)
            g["gdn_conv_w"][j] = dcw[:4]
            dqb, dfb, dib, dgb, dlb, dhn = _hgrn_bwd(sv["p"], lbs[j], wt["hgrn_norm"][j], sv["s_b"], do, "hgrn_bwd")
            g["hgrn_norm"][j] = dhn
            d_lbs[j] = jnp.sum(dlb, axis=0)
            dp = jnp.concatenate([dqkv] + [t_.astype(bf16) for t_ in (dz, dqb, dfb, dib, dgb, dba)], axis=1)
            g["ab_w_in"][j] = _mm_auto(sv["h1"], dp, "tn", bf16, "mm_ab_in_dw")
            dh1 = _mm_auto(dp, wt["ab_w_in"][j], "nt", f32, "mm_ab_in_dx")
        else:
            dhg = _mm_auto(dxb, wt["c_w_out"][j], "nt", f32, "mm_c_out_dx")
            g["c_w_out"][j] = _mm_auto(sv["hg"], dxb, "tn", bf16, "mm_c_out_dw")
            du, da, dy = _lru_scan_bwd(sv["a"], sv["h"], sv["p2"], dhg, "lru_scan_bwd")
            dxc, dwa, dwx, dba_, dbx_, dlam = _lru_gates_bwd(
                sv["xc"], wt["c_gate_a_w"][j], wt["c_gate_x_w"][j], wt["c_gate_a_b"][j], wt["c_gate_x_b"][j],
                wt["c_lambda"][j], da, du, "lru_gates_bwd")
            g["c_gate_a_w"][j], g["c_gate_x_w"][j] = dwa, dwx
            g["c_gate_a_b"][j], g["c_gate_x_b"][j], g["c_lambda"][j] = dba_, dbx_, dlam
            dxbr, dcw, dcb = _conv_bwd(dxc, sv["p2"], 1, wt["c_conv_w"][j], "lru_conv_bwd", tc=D)
            g["c_conv_w"][j] = dcw[:4]
            g["c_conv_b"][j] = dcb
            dp2 = jnp.concatenate([dy.astype(bf16), dxbr], axis=1)
            g["c_w_in"][j] = _mm_auto(sv["h1"], dp2, "tn", bf16, "mm_c_in_dw")
            dh1 = _mm_auto(dp2, wt["c_w_in"][j], "nt", f32, "mm_c_in_dx")
        deps = post_grads(layer, "mix", g) if post_grads else ()
        dx, dxb, g["norm_mix"][layer] = _rmsnorm_bwd(sv["x_in"], wt["norm_mix"][layer], dh1, dx, "rms_bwd", deps)
    g0, g1 = _lower_bounds_bwd(wt["hgrn_lower_bounds"], d_lbs[0], d_lbs[1])
    g["hgrn_lower_bounds"] = jnp.concatenate([g0, g1], axis=0)
    return loss, dx, g


def _ab_in_to_compute(w):
    return jnp.concatenate([w[:, :2048], w[:, 2056:4104], w[:, 2048:2056], jnp.zeros((D, 120), w.dtype)], axis=1)


def _ab_in_from_compute(g):
    return jnp.concatenate([g[:, :2048], g[:, 4096:4104], g[:, 2048:4096]], axis=1)


def _lane_vec(v4):
    return jnp.zeros((1, HEAD), f32).at[0, NH:2 * NH].set(v4)


def _layout_weights(fw):
    wt = {}
    wt["norm_mix"] = [fw["norm_mix"][l][None] for l in range(4)]
    wt["norm_ffn"] = [fw["norm_ffn"][l][None] for l in range(4)]
    wt["norm_final"] = fw["norm_final"][None]
    wt["gdn_conv_w"] = [fw["gdn_conv_w"][j] for j in range(2)]
    wt["alog_v"] = [_lane_vec(fw["gdn_a_log"][j]) for j in range(2)]
    wt["dtb_v"] = [_lane_vec(fw["gdn_dt_bias"][j]) for j in range(2)]
    wt["gdn_norm"] = [fw["gdn_norm"][j][None] for j in range(2)]
    wt["hgrn_lower_bounds"] = fw["hgrn_lower_bounds"]
    wt["hgrn_norm"] = [fw["hgrn_norm"][j][None] for j in range(2)]
    wt["c_conv_w"] = [fw["c_conv_w"][j] for j in range(2)]
    for k in ("c_conv_b", "c_gate_a_b", "c_gate_x_b", "c_lambda"):
        wt[k] = [fw[k][j][None] for j in range(2)]
    wt["ffn_conv_w"] = [fw["ffn_conv_w"][l] for l in range(4)]
    wt["ffn_conv_b"] = [fw["ffn_conv_b"][l][None] for l in range(4)]
    if "ab_w_in" in fw:
        wt["ab_w_in"] = [_ab_in_to_compute(fw["ab_w_in"][j].astype(bf16)) for j in range(2)]
        for k in ("ab_w_out", "c_w_in", "c_w_out"):
            wt[k] = [fw[k][j].astype(bf16) for j in range(2)]
        for k in ("c_gate_a_w", "c_gate_x_w"):
            wt[k] = [fw[k][j].astype(f32) for j in range(2)]
        for k in ("ffn_w_up", "ffn_w_down"):
            wt[k] = [fw[k][l].astype(bf16) for l in range(4)]
    return wt


def _layer_full(name, slots):
    kind = BIG[name]
    if kind == "col":
        full = slots.transpose(1, 0, 2).reshape(slots.shape[1], -1)
        return _ab_in_to_compute(full) if name == "ab_w_in" else full
    if kind == "row":
        return slots.reshape(-1, slots.shape[2])
    return slots.reshape(4, NH, LRU_BLOCK // 4, LRU_BLOCK).transpose(1, 0, 2, 3).reshape(NH, LRU_BLOCK, LRU_BLOCK).astype(f32)


def _layer_slots(name, g):
    kind = BIG[name]
    if kind == "col":
        if name == "ab_w_in":
            g = _ab_in_from_compute(g)
        r, cdim = g.shape
        return g.reshape(r, 4, cdim // 4).transpose(1, 0, 2).astype(bf16)
    if kind == "row":
        r, cdim = g.shape
        return g.reshape(4, r // 4, cdim).astype(bf16)
    return g.reshape(NH, 4, LRU_BLOCK // 4, LRU_BLOCK).transpose(1, 0, 2, 3).reshape(4, LRU_BLOCK, LRU_BLOCK).astype(bf16)


def _unlayout_grads(g):
    out = {}
    for k in ("norm_mix", "norm_ffn", "gdn_norm", "hgrn_norm", "c_conv_b", "c_gate_a_b", "c_gate_x_b", "c_lambda",
              "ffn_conv_b"):
        out[k] = jnp.concatenate(g[k], axis=0)
    out["norm_final"] = g["norm_final"][0]
    out["ab_w_in"] = jnp.stack([_ab_in_from_compute(t) for t in g["ab_w_in"]])
    out["gdn_a_log"] = jnp.stack([t[0, NH:2 * NH] for t in g["alog_v"]])
    out["gdn_dt_bias"] = jnp.stack([t[0, NH:2 * NH] for t in g["dtb_v"]])
    out["hgrn_lower_bounds"] = g["hgrn_lower_bounds"]
    for k in ("gdn_conv_w", "ab_w_out", "c_w_in", "c_conv_w", "c_gate_a_w", "c_gate_x_w", "c_w_out", "ffn_w_up",
              "ffn_conv_w", "ffn_w_down"):
        out[k] = jnp.stack(g[k])
    return out


MESH = pl.DeviceIdType.MESH
ANY = pl.BlockSpec(memory_space=pl.ANY)
CHIP_RELATIONS = ((1, 0), (0, 1), (1, 1))
N_CHIPS = 4


def _coords():
    return lax.axis_index("x"), lax.axis_index("y"), lax.axis_index("c")


def _flip(v, f):
    return 1 - v if f else v


def _half_rows(c, a, align):
    return pl.ds(pl.multiple_of(c * (a // 2), align), a // 2)


def _all_gather_chips(shards, name):
    n = len(shards)
    shapes = [s.shape for s in shards]

    def body(*refs):
        ins, outs = refs[:n], refs[n:2 * n]
        send_sems, recv_sems = refs[2 * n:]
        x, y, c = _coords()
        me = 2 * x + y
        sibling = (x, y, 1 - c)
        started = []
        for p in range(n):
            cp = pltpu.make_async_remote_copy(
                src_ref=ins[p], dst_ref=outs[p].at[me],
                send_sem=send_sems.at[p, 6], recv_sem=recv_sems.at[p, 6],
                device_id=sibling, device_id_type=MESH)
            cp.start()
            started.append(cp)
        for p in range(n):
            mine = _half_rows(c, shapes[p][0], 16)
            for r, (fx, fy) in enumerate(CHIP_RELATIONS):
                cp = pltpu.make_async_remote_copy(
                    src_ref=ins[p].at[mine], dst_ref=outs[p].at[me, mine],
                    send_sem=send_sems.at[p, r], recv_sem=recv_sems.at[p, r],
                    device_id=(_flip(x, fx), _flip(y, fy), c), device_id_type=MESH)
                cp.start()
                started.append(cp)
        for r, (fx, fy) in enumerate(CHIP_RELATIONS):
            k = 2 * _flip(x, fx) + _flip(y, fy)
            for p in range(n):
                mine = _half_rows(c, shapes[p][0], 16)
                pltpu.make_async_remote_copy(
                    src_ref=ins[p].at[mine], dst_ref=outs[p].at[k, mine],
                    send_sem=send_sems.at[p, r], recv_sem=recv_sems.at[p, r],
                    device_id=(_flip(x, fx), _flip(y, fy), c), device_id_type=MESH).wait_recv()
                fwd = pltpu.make_async_remote_copy(
                    src_ref=outs[p].at[k, mine], dst_ref=outs[p].at[k, mine],
                    send_sem=send_sems.at[p, 3 + r], recv_sem=recv_sems.at[p, 3 + r],
                    device_id=sibling, device_id_type=MESH)
                fwd.start()
                started.append(fwd)
        for r, (fx, fy) in enumerate(CHIP_RELATIONS):
            k = 2 * _flip(x, fx) + _flip(y, fy)
            for p in range(n):
                theirs = _half_rows(1 - c, shapes[p][0], 16)
                pltpu.make_async_remote_copy(
                    src_ref=outs[p].at[k, theirs], dst_ref=outs[p].at[k, theirs],
                    send_sem=send_sems.at[p, 3 + r], recv_sem=recv_sems.at[p, 3 + r],
                    device_id=sibling, device_id_type=MESH).wait_recv()
        for p in range(n):
            pltpu.make_async_remote_copy(
                src_ref=ins[p], dst_ref=outs[p].at[me],
                send_sem=send_sems.at[p, 6], recv_sem=recv_sems.at[p, 6],
                device_id=sibling, device_id_type=MESH).wait_recv()
        for cp in started:
            cp.wait_send()

    return pl.pallas_call(
        body, in_specs=[ANY] * n, out_specs=[ANY] * n,
        out_shape=[_sds((N_CHIPS,) + s.shape, s.dtype) for s in shards],
        scratch_shapes=[pltpu.SemaphoreType.DMA((n, 7)), pltpu.SemaphoreType.DMA((n, 7))],
        name=name)(*shards)


def _sibling_send_other_half(gs, name):
    n = len(gs)
    shapes = [g.shape for g in gs]

    def body(*refs):
        ins, outs = refs[:n], refs[n:2 * n]
        send_sems, recv_sems = refs[2 * n:]
        x, y, c = _coords()
        cps = []
        for p in range(n):
            theirs = _half_rows(1 - c, shapes[p][1], 8)
            cp = pltpu.make_async_remote_copy(
                src_ref=ins[p].at[:, theirs], dst_ref=outs[p],
                send_sem=send_sems.at[p], recv_sem=recv_sems.at[p],
                device_id=(x, y, 1 - c), device_id_type=MESH)
            cp.start()
            cps.append(cp)
        for cp in cps:
            cp.wait()

    return pl.pallas_call(
        body, in_specs=[ANY] * n, out_specs=[ANY] * n,
        out_shape=[_sds((s[0], s[1] // 2, s[2]), f32) for s in shapes],
        scratch_shapes=[pltpu.SemaphoreType.DMA((n,)), pltpu.SemaphoreType.DMA((n,))],
        name=name)(*gs)


def _chip_exchange(ps, name):
    n = len(ps)

    def body(*refs):
        ins, outs = refs[:n], refs[n:2 * n]
        send_sems, recv_sems = refs[2 * n:]
        x, y, c = _coords()
        cps = []
        for p in range(n):
            for r, (fx, fy) in enumerate(CHIP_RELATIONS):
                k = 2 * _flip(x, fx) + _flip(y, fy)
                cp = pltpu.make_async_remote_copy(
                    src_ref=ins[p].at[k], dst_ref=outs[p].at[r],
                    send_sem=send_sems.at[p, r], recv_sem=recv_sems.at[p, r],
                    device_id=(_flip(x, fx), _flip(y, fy), c), device_id_type=MESH)
                cp.start()
                cps.append(cp)
        for cp in cps:
            cp.wait_recv()
        for cp in cps:
            cp.wait_send()

    return pl.pallas_call(
        body, in_specs=[ANY] * n, out_specs=[ANY] * n,
        out_shape=[_sds((3,) + p.shape[1:], p.dtype) for p in ps],
        scratch_shapes=[pltpu.SemaphoreType.DMA((n, 3)), pltpu.SemaphoreType.DMA((n, 3))],
        name=name)(*ps)


def _sibling_fill_other_half(fs, name):
    n = len(fs)
    shapes = [f.shape for f in fs]

    def body(*refs):
        ins, outs = refs[:n], refs[n:2 * n]
        send_sems, recv_sems = refs[2 * n:]
        x, y, c = _coords()
        cps = []
        for p in range(n):
            mine = _half_rows(c, shapes[p][0], 8)
            cp = pltpu.make_async_remote_copy(
                src_ref=ins[p].at[mine], dst_ref=outs[p].at[mine],
                send_sem=send_sems.at[p], recv_sem=recv_sems.at[p],
                device_id=(x, y, 1 - c), device_id_type=MESH)
            cp.start()
            cps.append(cp)
        for p in range(n):
            theirs = _half_rows(1 - c, shapes[p][0], 8)
            pltpu.make_async_remote_copy(
                src_ref=ins[p].at[theirs], dst_ref=outs[p].at[theirs],
                send_sem=send_sems.at[p], recv_sem=recv_sems.at[p],
                device_id=(x, y, 1 - c), device_id_type=MESH).wait_recv()
        for cp in cps:
            cp.wait_send()

    return pl.pallas_call(
        body, in_specs=[ANY] * n, out_specs=[ANY] * n,
        out_shape=[_sds(f.shape, f.dtype) for f in fs],
        input_output_aliases={p: p for p in range(n)},
        scratch_shapes=[pltpu.SemaphoreType.DMA((n,)), pltpu.SemaphoreType.DMA((n,))],
        name=name)(*fs)


def _sibling_all_gather(ss, name):
    n = len(ss)
    shapes = [s.shape for s in ss]

    def body(*refs):
        ins, outs = refs[:n], refs[n:2 * n]
        send_sems, recv_sems, loc_sems = refs[2 * n:]
        x, y, c = _coords()
        locs, cps = [], []
        for p in range(n):
            mine = _half_rows(c, 2 * shapes[p][0], 8)
            loc = pltpu.make_async_copy(ins[p], outs[p].at[mine], loc_sems.at[p])
            loc.start()
            locs.append(loc)
            cp = pltpu.make_async_remote_copy(
                src_ref=ins[p], dst_ref=outs[p].at[mine],
                send_sem=send_sems.at[p], recv_sem=recv_sems.at[p],
                device_id=(x, y, 1 - c), device_id_type=MESH)
            cp.start()
            cps.append(cp)
        for p, cp in enumerate(cps):
            theirs = _half_rows(1 - c, 2 * shapes[p][0], 8)
            pltpu.make_async_remote_copy(
                src_ref=ins[p], dst_ref=outs[p].at[theirs],
                send_sem=send_sems.at[p], recv_sem=recv_sems.at[p],
                device_id=(x, y, 1 - c), device_id_type=MESH).wait_recv()
        for cp in cps:
            cp.wait_send()
        for loc in locs:
            loc.wait()

    return pl.pallas_call(
        body, in_specs=[ANY] * n, out_specs=[ANY] * n,
        out_shape=[_sds((2 * s[0], s[1]), f32) for s in shapes],
        scratch_shapes=[pltpu.SemaphoreType.DMA((n,)), pltpu.SemaphoreType.DMA((n,)), pltpu.SemaphoreType.DMA((n,))],
        name=name)(*ss)


N_DEV = 8


def _all_reduce_small(pack, name):
    rows = pack.shape[0]

    def body(in_ref, sum_ref, all_ref, send_sems, recv_sems):
        x, y, c = _coords()
        me = 4 * x + 2 * y + c
        all_ref[me] = in_ref[...]
        cps = []
        for r in range(1, N_DEV):
            fx, fy, fc = (r >> 2) & 1, (r >> 1) & 1, r & 1
            cp = pltpu.make_async_remote_copy(
                src_ref=in_ref, dst_ref=all_ref.at[me],
                send_sem=send_sems.at[r], recv_sem=recv_sems.at[r],
                device_id=(_flip(x, fx), _flip(y, fy), _flip(c, fc)), device_id_type=MESH)
            cp.start()
            cps.append(cp)
        for r in range(1, N_DEV):
            fx, fy, fc = (r >> 2) & 1, (r >> 1) & 1, r & 1
            peer = 4 * _flip(x, fx) + 2 * _flip(y, fy) + _flip(c, fc)
            pltpu.make_async_remote_copy(
                src_ref=in_ref, dst_ref=all_ref.at[peer],
                send_sem=send_sems.at[r], recv_sem=recv_sems.at[r],
                device_id=(x, y, c), device_id_type=MESH).wait_recv()
        for cp in cps:
            cp.wait_send()
        acc = all_ref[0]
        for d in range(1, N_DEV):
            acc = acc + all_ref[d]
        sum_ref[...] = acc

    vm = pl.BlockSpec(memory_space=pltpu.VMEM)
    return pl.pallas_call(
        body, in_specs=[vm], out_specs=[vm, vm],
        out_shape=[_sds((rows, 128), f32), _sds((N_DEV, rows, 128), f32)],
        scratch_shapes=[pltpu.SemaphoreType.DMA((N_DEV,)), pltpu.SemaphoreType.DMA((N_DEV,))],
        name=name)(pack)[0]


ROWS_EW = 128


def _add_own_half(g, rs, c_arr, name):
    s, a, b = g.shape
    nrt = (a // 2) // ROWS_EW

    def body(c_ref, g_ref, r_ref, o_ref):
        o_ref[...] = (g_ref[...] + r_ref[...]).astype(bf16)

    grid_spec = pltpu.PrefetchScalarGridSpec(
        num_scalar_prefetch=1, grid=(s, nrt),
        in_specs=[pl.BlockSpec((1, ROWS_EW, b), lambda k, i, c_ref: (k, c_ref[0] * nrt + i, 0)),
                  pl.BlockSpec((1, ROWS_EW, b), lambda k, i, c_ref: (k, i, 0))],
        out_specs=pl.BlockSpec((1, ROWS_EW, b), lambda k, i, c_ref: (k, i, 0)))
    return pl.pallas_call(body, grid_spec=grid_spec, out_shape=_sds((s, a // 2, b), bf16),
                          compiler_params=_cp(), name=name)(c_arr, g, rs)


def _sum_chips(g, rs, rc, ids, name):
    _, r, b = rc.shape
    nrt = r // ROWS_EW

    def body(ids_ref, g_ref, s_ref, r_ref, o_ref):
        own = g_ref[0] + s_ref[0]
        o_ref[...] = ((own + r_ref[0].astype(f32)) + r_ref[1].astype(f32)) + r_ref[2].astype(f32)

    grid_spec = pltpu.PrefetchScalarGridSpec(
        num_scalar_prefetch=1, grid=(nrt,),
        in_specs=[pl.BlockSpec((1, ROWS_EW, b), lambda i, ids_ref: (ids_ref[0], ids_ref[1] * nrt + i, 0)),
                  pl.BlockSpec((1, ROWS_EW, b), lambda i, ids_ref: (ids_ref[0], i, 0)),
                  pl.BlockSpec((3, ROWS_EW, b), lambda i, ids_ref: (0, i, 0))],
        out_specs=pl.BlockSpec((ROWS_EW, b), lambda i, ids_ref: (ids_ref[1] * nrt + i, 0)))
    return pl.pallas_call(body, grid_spec=grid_spec, out_shape=_sds((2 * r, b), f32),
                          compiler_params=_cp(), name=name)(ids, g, rs, rc)


def _adamw_math(w, g, m, v):
    m = ADAM_B1 * m + (1.0 - ADAM_B1) * g
    v = ADAM_B2 * v + (1.0 - ADAM_B2) * (g * g)
    m_hat = m / (1.0 - ADAM_B1 ** ADAM_STEP)
    v_hat = v / (1.0 - ADAM_B2 ** ADAM_STEP)
    delta = -ADAM_LR * (m_hat / (jnp.sqrt(v_hat) + ADAM_EPS) + ADAM_WD * w)
    return delta, m, v


def _adamw_big(w, g, m, v, name):
    a, b = w.shape

    def body(w_ref, g_ref, m_ref, v_ref, go_ref, d_ref, mo_ref, vo_ref):
        gv = g_ref[...]
        d, mn, vn = _adamw_math(w_ref[...], gv, m_ref[...], v_ref[...])
        go_ref[...] = gv
        d_ref[...] = d
        mo_ref[...] = mn
        vo_ref[...] = vn

    blk = pl.BlockSpec((ROWS_EW, b), lambda i: (i, 0))
    return pl.pallas_call(body, grid=(a // ROWS_EW,), in_specs=[blk] * 4, out_specs=[blk] * 4,
                          out_shape=[_sds((a, b), f32)] * 4, compiler_params=_cp(), name=name)(w, g, m, v)


def _adamw_small(ws, gs, ms, vs, name):
    n = len(ws)

    def body(*refs):
        w_r, g_r, m_r, v_r = refs[:n], refs[n:2 * n], refs[2 * n:3 * n], refs[3 * n:4 * n]
        go_r, d_r, mo_r, vo_r = refs[4 * n:5 * n], refs[5 * n:6 * n], refs[6 * n:7 * n], refs[7 * n:8 * n]
        for p in range(n):
            gv = g_r[p][...]
            d, mn, vn = _adamw_math(w_r[p][...], gv, m_r[p][...], v_r[p][...])
            go_r[p][...] = gv
            d_r[p][...] = d
            mo_r[p][...] = mn
            vo_r[p][...] = vn

    vm = pl.BlockSpec(memory_space=pltpu.VMEM)
    shp = [_sds(w.shape, f32) for w in ws]
    res = pl.pallas_call(body, in_specs=[vm] * (4 * n), out_specs=[vm] * (4 * n), out_shape=shp * 4,
                         name=name)(*ws, *gs, *ms, *vs)
    return res[:n], res[n:2 * n], res[2 * n:3 * n], res[3 * n:]


HBM = pl.BlockSpec(memory_space=pltpu.HBM)
SEM = pl.BlockSpec(memory_space=pltpu.SEMAPHORE)
EFFECT = pltpu.SideEffectType.DATAFLOW_SIDE_EFFECTING
N_REL = 8


def _rel(r):
    return (r >> 2) & 1, (r >> 1) & 1, r & 1


def _gather_copies(ins, lands, send_sems, recv_sems, shapes):
    x, y, c = _coords()
    me = 2 * x + y
    sends, recvs = [], []
    for p in range(len(ins)):
        for r in range(1, N_REL):
            fx, fy, fc = _rel(r)
            peer = (_flip(x, fx), _flip(y, fy), _flip(c, fc))
            if fx == 0 and fy == 0:
                src, dst, got = ins[p], lands[p].at[me], lands[p].at[me]
            else:
                mine = _half_rows(c, shapes[p][0], 16)
                theirs = _half_rows(_flip(c, fc), shapes[p][0], 16)
                src, dst = ins[p].at[mine], lands[p].at[me, mine]
                got = lands[p].at[2 * peer[0] + peer[1], theirs]
            sems = dict(send_sem=send_sems.at[p * N_REL + r], recv_sem=recv_sems.at[p * N_REL + r], device_id=peer,
                        device_id_type=MESH)
            sends.append(pltpu.make_async_remote_copy(src_ref=src, dst_ref=dst, **sems))
            recvs.append(pltpu.make_async_remote_copy(src_ref=src, dst_ref=got, **sems))
    return sends, recvs


def _scatter_copies(ins, lands, send_sems, recv_sems, shapes):
    x, y, c = _coords()
    sends, recvs = [], []
    for p in range(len(ins)):
        for r in range(1, N_REL):
            fx, fy, fc = _rel(r)
            peer = (_flip(x, fx), _flip(y, fy), _flip(c, fc))
            theirs = _half_rows(peer[2], shapes[p][1], 16)
            sems = dict(send_sem=send_sems.at[p * N_REL + r], recv_sem=recv_sems.at[p * N_REL + r], device_id=peer,
                        device_id_type=MESH)
            cp = pltpu.make_async_remote_copy(src_ref=ins[p].at[2 * peer[0] + peer[1], theirs], dst_ref=lands[p].at[r], **sems)
            sends.append(cp)
            recvs.append(cp)
    return sends, recvs


def _split_start(copies_fn, ins, land_shapes, name):
    n = len(ins)
    shapes = [a.shape for a in ins]

    def body(*refs):
        in_refs, land_refs = refs[:n], refs[n:2 * n]
        send_sems, recv_sems = refs[2 * n], refs[2 * n + 1]
        token = refs[-1]
        sends, _ = copies_fn(in_refs, land_refs, send_sems, recv_sems, shapes)
        for cp in sends:
            cp.start()
        token[...] = jnp.zeros_like(token)

    lands = [lax.empty(s.shape, s.dtype) for s in land_shapes]
    res = pl.pallas_call(
        body, name=name,
        out_shape=(pltpu.SemaphoreType.DMA((n * N_REL,)), pltpu.SemaphoreType.DMA((n * N_REL,)))
        + tuple(pltpu.HBM(a.shape, a.dtype) for a in ins) + tuple(pltpu.HBM(s.shape, s.dtype) for s in land_shapes)
        + (_sds((8, 128), f32),),
        in_specs=[HBM] * (2 * n), out_specs=(SEM, SEM) + (HBM,) * (2 * n) + (pl.BlockSpec(memory_space=pltpu.VMEM),),
        input_output_aliases={i: 2 + i for i in range(2 * n)},
        compiler_params=pltpu.CompilerParams(has_side_effects=EFFECT),
    )(*[pltpu.with_memory_space_constraint(a, pltpu.HBM) for a in ins],
      *[pltpu.with_memory_space_constraint(a, pltpu.HBM) for a in lands])
    return dict(sems=res[:2], ins=res[2:2 + n], lands=res[2 + n:2 + 2 * n], token=res[-1], shapes=shapes)


def _split_wait(copies_fn, started, after, name):
    n = len(started["ins"])
    shapes = started["shapes"]
    na = len(after)

    def body(*refs):
        in_refs, land_refs = refs[:n], refs[n:2 * n]
        send_sems, recv_sems = refs[2 * n], refs[2 * n + 1]
        sends, recvs = copies_fn(in_refs, land_refs, send_sems, recv_sems, shapes)
        for cp in sends:
            cp.wait_send()
        for cp in recvs:
            cp.wait_recv()

    arrs = list(started["ins"]) + list(started["lands"])
    res = pl.pallas_call(
        body, name=name,
        out_shape=tuple(pltpu.HBM(a.shape, a.dtype) for a in arrs),
        in_specs=[HBM] * (2 * n) + [SEM, SEM] + [pl.BlockSpec(memory_space=pl.ANY)] * na,
        out_specs=(HBM,) * (2 * n), input_output_aliases={i: i for i in range(2 * n)},
        compiler_params=pltpu.CompilerParams(has_side_effects=EFFECT),
    )(*arrs, *started["sems"], *after)
    return res[:n], res[n:]


def _sum_pieces(gb, land, ids, f_prev, blk, nblk, name):
    _, a, b = gb.shape
    rows = _tile(a // 2, (ROWS_EW, 176, 64, 32, 16))
    nrt = (a // 2) // rows

    def body(ids_ref, g_ref, l_ref, *rest):
        o_ref = rest[-1]
        acc = g_ref[0].astype(f32)
        for r in range(1, N_REL):
            acc = acc + l_ref[r].astype(f32)
        o_ref[...] = acc

    in_specs = [pl.BlockSpec((1, rows, b), lambda i, ids_ref: (ids_ref[0], ids_ref[1] * nrt + i, 0)),
                pl.BlockSpec((N_REL, rows, b), lambda i, ids_ref: (0, i, 0))]
    args = [ids, gb, land]
    aliases = {}
    if f_prev is not None:
        in_specs.append(pl.BlockSpec(memory_space=pl.ANY))
        args.append(f_prev)
        aliases = {3: 0}
    grid_spec = pltpu.PrefetchScalarGridSpec(
        num_scalar_prefetch=1, grid=(nrt,), in_specs=in_specs,
        out_specs=pl.BlockSpec((rows, b), lambda i, ids_ref: ((2 * blk + ids_ref[1]) * nrt + i, 0)))
    return pl.pallas_call(body, grid_spec=grid_spec, out_shape=_sds((nblk * a, b), f32),
                          input_output_aliases=aliases, compiler_params=_cp(), name=name)(*args)


def _sibling_fill_blocks(fs, nblks, name):
    n = len(fs)
    shapes = [f.shape for f in fs]

    def body(*refs):
        ins, outs = refs[:n], refs[n:2 * n]
        send_sems, recv_sems = refs[2 * n:]
        x, y, c = _coords()
        cps, waits = [], []
        k = 0
        for p in range(n):
            a = shapes[p][0] // nblks[p]
            for bi in range(nblks[p]):
                mine = pl.ds(pl.multiple_of(bi * a + c * (a // 2), 8), a // 2)
                theirs = pl.ds(pl.multiple_of(bi * a + (1 - c) * (a // 2), 8), a // 2)
                sems = dict(send_sem=send_sems.at[k], recv_sem=recv_sems.at[k], device_id=(x, y, 1 - c), device_id_type=MESH)
                cp = pltpu.make_async_remote_copy(src_ref=ins[p].at[mine], dst_ref=outs[p].at[mine], **sems)
                cp.start()
                cps.append(cp)
                waits.append(pltpu.make_async_remote_copy(src_ref=ins[p].at[theirs], dst_ref=outs[p].at[theirs], **sems))
                k += 1
        for wt_ in waits:
            wt_.wait_recv()
        for cp in cps:
            cp.wait_send()

    total = sum(nblks)
    return pl.pallas_call(
        body, in_specs=[ANY] * n, out_specs=[ANY] * n,
        out_shape=[_sds(f.shape, f.dtype) for f in fs],
        input_output_aliases={p: p for p in range(n)},
        scratch_shapes=[pltpu.SemaphoreType.DMA((total,)), pltpu.SemaphoreType.DMA((total,))],
        name=name)(*fs)


WEIGHTS = ["norm_mix", "norm_ffn", "norm_final", "ab_w_in", "gdn_conv_w", "gdn_a_log", "gdn_dt_bias", "gdn_norm",
           "hgrn_lower_bounds", "hgrn_norm", "ab_w_out", "c_w_in", "c_conv_w", "c_conv_b", "c_gate_a_w", "c_gate_a_b",
           "c_gate_x_w", "c_gate_x_b", "c_lambda", "c_w_out", "ffn_w_up", "ffn_conv_w", "ffn_conv_b", "ffn_w_down"]
BIG = {"ab_w_in": "col", "ab_w_out": "row", "c_w_in": "col", "c_gate_a_w": "gate", "c_gate_x_w": "gate",
       "c_w_out": "row", "ffn_w_up": "col", "ffn_w_down": "row"}
SMALL_SHARDED = ["gdn_conv_w", "c_conv_w", "c_conv_b", "c_gate_a_b", "c_gate_x_b", "c_lambda", "ffn_conv_w"]
SMALL = [n for n in WEIGHTS if n not in BIG]
FULL_SHAPES = {
    "norm_mix": (4, 1024), "norm_ffn": (4, 1024), "norm_final": (1024,), "ab_w_in": (2, 1024, 4104),
    "gdn_conv_w": (2, 4, 1536), "gdn_a_log": (2, 4), "gdn_dt_bias": (2, 4), "gdn_norm": (2, 128),
    "hgrn_lower_bounds": (2, 512), "hgrn_norm": (2, 128), "ab_w_out": (2, 1024, 1024), "c_w_in": (2, 1024, 2048),
    "c_conv_w": (2, 4, 1024), "c_conv_b": (2, 1024), "c_gate_a_w": (2, 4, 256, 256), "c_gate_a_b": (2, 1024),
    "c_gate_x_w": (2, 4, 256, 256), "c_gate_x_b": (2, 1024), "c_lambda": (2, 1024), "c_w_out": (2, 1024, 1024),
    "ffn_w_up": (4, 1024, 5632), "ffn_conv_w": (4, 3, 2816), "ffn_conv_b": (4, 2816), "ffn_w_down": (4, 2816, 1024)}


def _shard2d(name, shard):
    return shard.reshape(-1, shard.shape[-1])


def _full_from_slots(name, slots):
    full = FULL_SHAPES[name]
    kind = BIG[name]
    if kind == "col":
        l, r, cdim = full
        return slots.reshape(4, l, r, cdim // 4).transpose(1, 2, 0, 3).reshape(full)
    if kind == "row":
        l, r, cdim = full
        return slots.reshape(4, l, r // 4, cdim).transpose(1, 0, 2, 3).reshape(full)
    l, h, r, cdim = full
    return slots.reshape(4, l, h, r // 4, cdim).transpose(1, 2, 0, 3, 4).reshape(full)


def _slots_from_full(name, g):
    full = FULL_SHAPES[name]
    kind = BIG[name]
    if kind == "col":
        l, r, cdim = full
        return g.reshape(l, r, 4, cdim // 4).transpose(2, 0, 1, 3).reshape(4, l * r, cdim // 4)
    if kind == "row":
        l, r, cdim = full
        return g.reshape(l, 4, r // 4, cdim).transpose(1, 0, 2, 3).reshape(4, l * (r // 4), cdim)
    l, h, r, cdim = full
    return g.reshape(l, h, 4, r // 4, cdim).transpose(2, 0, 1, 3, 4).reshape(4, l * h * (r // 4), cdim)


def _pack_rows(arrs, rows):
    flat = jnp.concatenate([a.reshape(-1) for a in arrs])
    return jnp.pad(flat, (0, rows * 128 - flat.shape[0])).reshape(rows, 128)


def _unpack_rows(pack, shapes):
    flat = pack.reshape(-1)
    out, off = [], 0
    for s in shapes:
        size = 1
        for d in s:
            size *= d
        out.append(flat[off:off + size].reshape(s))
        off += size
    return out


def kernel(x, norm_mix, norm_ffn, norm_final, ab_w_in, gdn_conv_w, gdn_a_log, gdn_dt_bias, gdn_norm, hgrn_lower_bounds, hgrn_norm, ab_w_out, c_w_in, c_conv_w, c_conv_b, c_gate_a_w, c_gate_a_b, c_gate_x_w, c_gate_x_b, c_lambda, c_w_out, ffn_w_up, ffn_conv_w, ffn_conv_b, ffn_w_down, loss_target, m_norm_mix, m_norm_ffn, m_norm_final, m_ab_w_in, m_gdn_conv_w, m_gdn_a_log, m_gdn_dt_bias, m_gdn_norm, m_hgrn_lower_bounds, m_hgrn_norm, m_ab_w_out, m_c_w_in, m_c_conv_w, m_c_conv_b, m_c_gate_a_w, m_c_gate_a_b, m_c_gate_x_w, m_c_gate_x_b, m_c_lambda, m_c_w_out, m_ffn_w_up, m_ffn_conv_w, m_ffn_conv_b, m_ffn_w_down, v_norm_mix, v_norm_ffn, v_norm_final, v_ab_w_in, v_gdn_conv_w, v_gdn_a_log, v_gdn_dt_bias, v_gdn_norm, v_hgrn_lower_bounds, v_hgrn_norm, v_ab_w_out, v_c_w_in, v_c_conv_w, v_c_conv_b, v_c_gate_a_w, v_c_gate_a_b, v_c_gate_x_w, v_c_gate_x_b, v_c_lambda, v_c_w_out, v_ffn_w_up, v_ffn_conv_w, v_ffn_conv_b, v_ffn_w_down):
    w = dict(zip(WEIGHTS, (norm_mix, norm_ffn, norm_final, ab_w_in, gdn_conv_w, gdn_a_log, gdn_dt_bias, gdn_norm, hgrn_lower_bounds, hgrn_norm, ab_w_out, c_w_in, c_conv_w, c_conv_b, c_gate_a_w, c_gate_a_b, c_gate_x_w, c_gate_x_b, c_lambda, c_w_out, ffn_w_up, ffn_conv_w, ffn_conv_b, ffn_w_down)))
    m = dict(zip(WEIGHTS, (m_norm_mix, m_norm_ffn, m_norm_final, m_ab_w_in, m_gdn_conv_w, m_gdn_a_log, m_gdn_dt_bias, m_gdn_norm, m_hgrn_lower_bounds, m_hgrn_norm, m_ab_w_out, m_c_w_in, m_c_conv_w, m_c_conv_b, m_c_gate_a_w, m_c_gate_a_b, m_c_gate_x_w, m_c_gate_x_b, m_c_lambda, m_c_w_out, m_ffn_w_up, m_ffn_conv_w, m_ffn_conv_b, m_ffn_w_down)))
    v = dict(zip(WEIGHTS, (v_norm_mix, v_norm_ffn, v_norm_final, v_ab_w_in, v_gdn_conv_w, v_gdn_a_log, v_gdn_dt_bias, v_gdn_norm, v_hgrn_lower_bounds, v_hgrn_norm, v_ab_w_out, v_c_w_in, v_c_conv_w, v_c_conv_b, v_c_gate_a_w, v_c_gate_a_b, v_c_gate_x_w, v_c_gate_x_b, v_c_lambda, v_c_w_out, v_ffn_w_up, v_ffn_conv_w, v_ffn_conv_b, v_ffn_w_down)))
    big = list(BIG)
    chip = 2 * lax.axis_index("x") + lax.axis_index("y")
    ids = jnp.stack([chip, lax.axis_index("c")]).astype(jnp.int32)

    def layer_parts(l):
        j = l // 2
        if l % 2 == 0:
            mix = [("ab_w_in", j), ("ab_w_out", j)]
        else:
            mix = [("c_w_in", j), ("c_gate_a_w", j), ("c_gate_x_w", j), ("c_w_out", j)]
        return mix, [("ffn_w_up", l), ("ffn_w_down", l)]

    def layer_shard(n, i):
        s = w[n][i]
        return s.reshape(-1, s.shape[-1]).astype(bf16)

    small_shard_shapes = [w[n].shape for n in SMALL_SHARDED]
    small_pack = _pack_rows([w[n] for n in SMALL_SHARDED], 128)
    parts0 = sum(layer_parts(0), [])
    gathered0 = _all_gather_chips([layer_shard(n, i) for n, i in parts0] + [small_pack], "all_gather_layer0")
    gathers = {}
    for l in (1, 2, 3):
        shards = [layer_shard(n, i) for n, i in sum(layer_parts(l), [])]
        gathers[l] = _split_start(_gather_copies, shards, [_sds((N_CHIPS,) + s.shape, bf16) for s in shards],
                                  "gather_start_%d" % l)
    fw = {}
    per_chip = [_unpack_rows(gathered0[-1][k], small_shard_shapes) for k in range(N_CHIPS)]
    for i, n in enumerate(SMALL_SHARDED):
        fw[n] = jnp.concatenate([per_chip[k][i] for k in range(N_CHIPS)], axis=-1)
    for n in SMALL:
        if n not in fw:
            fw[n] = w[n]
    wt = _layout_weights(fw)
    for n in big:
        wt[n] = [None] * FULL_SHAPES[n][0]

    def pre_layer(l, x_l):
        parts = sum(layer_parts(l), [])
        if l == 0:
            lands, deps = gathered0[:len(parts)], tuple(gathers[k]["token"] for k in (1, 2, 3))
        else:
            _, lands = _split_wait(_gather_copies, gathers[l], [x_l], "gather_wait_%d" % l)
            deps = ()
        for (n, i), slots in zip(parts, lands):
            wt[n][i] = _layer_full(n, slots)
        return deps

    scatters = []

    def post_grads(l, part, g):
        parts = layer_parts(l)[0 if part == "mix" else 1]
        slots = [_layer_slots(n, g[n][i]) for n, i in parts]
        st = _split_start(_scatter_copies, slots, [_sds((N_REL, s.shape[1] // 2, s.shape[2]), bf16) for s in slots],
                          "scatter_start_%d_%s" % (l, part))
        scatters.append((parts, st, "scatter_wait_%d_%s" % (l, part)))
        return (st["token"],)

    loss, dx, g = _local_step(x[0], loss_target[0], wt, pre_layer, post_grads)
    gf = _unlayout_grads(g)
    loss = lax.psum(loss[0, 0], ("x", "y", "c"))

    f = {n: None for n in big}
    for parts, st, wait_name in scatters:
        gbs, lands = _split_wait(_scatter_copies, st, [dx], wait_name)
        for (n, i), gb, land in zip(parts, gbs, lands):
            f[n] = _sum_pieces(gb, land, ids, f[n], i, FULL_SHAPES[n][0], "rs_sum")
    filled = _sibling_fill_blocks([f[n] for n in big], [FULL_SHAPES[n][0] for n in big], "rs_sibling_fill")
    g_shard = dict(zip(big, filled))

    small_full_shapes = [FULL_SHAPES[n] for n in SMALL]
    small_sum = _all_reduce_small(_pack_rows([gf[n] for n in SMALL], 664), "all_reduce_small")
    g_small = dict(zip(SMALL, _unpack_rows(small_sum, small_full_shapes)))
    for n in SMALL_SHARDED:
        width = w[n].shape[-1]
        g_small[n] = lax.dynamic_slice_in_dim(g_small[n], chip * width, width, axis=-1)

    out_g, out_d, out_m, out_v = {}, {}, {}, {}
    for n in big:
        shp = w[n].shape
        res = _adamw_big(_shard2d(n, w[n]), g_shard[n], _shard2d(n, m[n]), _shard2d(n, v[n]), "adamw_" + n)
        out_g[n], out_d[n], out_m[n], out_v[n] = (t.reshape(shp) for t in res)
    as2d = lambda t: t.reshape(-1, t.shape[-1])
    sg, sd, sm, sv = _adamw_small([as2d(w[n]) for n in SMALL], [as2d(g_small[n]) for n in SMALL],
                                  [as2d(m[n]) for n in SMALL], [as2d(v[n]) for n in SMALL], "adamw_small")
    for i, n in enumerate(SMALL):
        out_g[n], out_d[n], out_m[n], out_v[n] = (t[i].reshape(w[n].shape) for t in (sg, sd, sm, sv))
    return (loss, dx[None], *[out_g[n] for n in WEIGHTS], *[out_d[n] for n in WEIGHTS],
            *[out_m[n] for n in WEIGHTS], *[out_v[n] for n in WEIGHTS])
```

```python
import functools

import jax
import jax.numpy as jnp
from jax import lax
from jax.experimental import pallas as pl
from jax.experimental.pallas import tpu as pltpu

f32 = jnp.float32
bf16 = jnp.bfloat16
HI = lax.Precision.HIGHEST

D = 1024
EPS = 1e-6
F_FLOOR = 1e-30
GDN_CHUNK = 64
GDN_INTRA_CHUNKS = 1
HGRN_CHUNK = 16
HGRN_STEP = 128
HEAD = 128
NH = 4
LRU_BLOCK = 256
D_FF = 2816
RG_C = 8.0
C_QKV, C_Z, C_QB, C_FB, C_IB, C_GB, C_BA = 0, 1536, 2048, 2560, 3072, 3584, 4096
AB_COLS_PAD = 4224
TT = 256
VMEM_LIMIT = 56 * 1024 * 1024

ADAM_LR, ADAM_B1, ADAM_B2, ADAM_EPS, ADAM_WD, ADAM_STEP = 0.001, 0.9, 0.999, 1e-08, 0.01, 10


def _cp(**kw):
    return pltpu.CompilerParams(vmem_limit_bytes=VMEM_LIMIT, **kw)


def _sds(shape, dtype):
    return jax.ShapeDtypeStruct(shape, dtype)


def _dot(a, b, dims, precision=None):
    return lax.dot_general(a, b, (dims, ((), ())), precision=precision, preferred_element_type=f32)


NN = ((1,), (0,))
NT = ((1,), (1,))
TN = ((0,), (0,))


def _rms(x, g):
    return x * lax.rsqrt(jnp.mean(x * x, axis=-1, keepdims=True) + EPS) * g


def _silu(x):
    return x * jax.nn.sigmoid(x)


def _mm(a, b, mode, tm, tn, out_dtype, name):
    if mode == "nn":
        (m, k), n = a.shape, b.shape[1]
        a_spec = pl.BlockSpec((tm, k), lambda i, j: (i, 0))
        b_spec = pl.BlockSpec((k, tn), lambda i, j: (0, j))
        dims = NN
    elif mode == "nt":
        (m, k), n = a.shape, b.shape[0]
        a_spec = pl.BlockSpec((tm, k), lambda i, j: (i, 0))
        b_spec = pl.BlockSpec((tn, k), lambda i, j: (j, 0))
        dims = NT
    else:
        (k, m), n = a.shape, b.shape[1]
        a_spec = pl.BlockSpec((k, tm), lambda i, j: (0, i))
        b_spec = pl.BlockSpec((k, tn), lambda i, j: (0, j))
        dims = TN
    assert m % tm == 0 and n % tn == 0, (name, m, n, tm, tn)

    def body(a_ref, b_ref, o_ref):
        o_ref[...] = _dot(a_ref[...], b_ref[...], dims).astype(out_dtype)

    return pl.pallas_call(
        body, grid=(m // tm, n // tn), in_specs=[a_spec, b_spec],
        out_specs=pl.BlockSpec((tm, tn), lambda i, j: (i, j)),
        out_shape=_sds((m, n), out_dtype), compiler_params=_cp(), name=name)(a, b)


def _mm_res_norm(a, b, res, gain, name, tm=512):
    (m, k), n = a.shape, b.shape[1]
    tm = min(tm, m)
    assert n == D and m % tm == 0, (name, m, n)

    def body(a_ref, b_ref, r_ref, *rest):
        xv = r_ref[...] + _dot(a_ref[...], b_ref[...], NN)
        if gain is None:
            rest[0][...] = xv
        else:
            g_ref, x_ref, h_ref = rest
            x_ref[...] = xv
            h_ref[...] = _rms(xv, g_ref[...]).astype(bf16)

    row = pl.BlockSpec((tm, D), lambda i: (i, 0))
    in_specs = [pl.BlockSpec((tm, k), lambda i: (i, 0)), pl.BlockSpec((k, D), lambda i: (0, 0)), row]
    args = [a, b, res]
    if gain is None:
        out_specs, out_shape = row, _sds((m, D), f32)
    else:
        in_specs.append(pl.BlockSpec((1, D), lambda i: (0, 0)))
        args.append(gain)
        out_specs, out_shape = [row, row], [_sds((m, D), f32), _sds((m, D), bf16)]
    return pl.pallas_call(body, grid=(m // tm,), in_specs=in_specs, out_specs=out_specs, out_shape=out_shape,
                          compiler_params=_cp(), name=name)(*args)


def _tile(n, cands):
    for c in cands:
        if n % c == 0:
            return c
    raise ValueError(n)


def _mm_auto(a, b, mode, out_dtype, name):
    m = a.shape[1] if mode == "tn" else a.shape[0]
    n = b.shape[0] if mode == "nt" else b.shape[1]
    return _mm(a, b, mode, _tile(m, (512, 256, 128)), _tile(n, (1024, 1408, 512, 384, 256, 128)), out_dtype, name)


def _rmsnorm_fwd(x, gain, name, after=()):
    t = x.shape[0]

    def body(x_ref, g_ref, *rest):
        h_ref = rest[len(after)]
        h_ref[...] = _rms(x_ref[...], g_ref[...]).astype(bf16)

    return pl.pallas_call(
        body, grid=(t // TT,),
        in_specs=[pl.BlockSpec((TT, D), lambda i: (i, 0)), pl.BlockSpec((1, D), lambda i: (0, 0))]
        + [pl.BlockSpec(memory_space=pl.ANY)] * len(after),
        out_specs=pl.BlockSpec((TT, D), lambda i: (i, 0)),
        out_shape=_sds((t, D), bf16), compiler_params=_cp(), name=name)(x, gain, *after)


def _rmsnorm_bwd(x, gain, dh, dres, name, after=()):
    t = x.shape[0]

    def body(x_ref, g_ref, dh_ref, dres_ref, *rest):
        dx_ref, dxb_ref, dg_ref = rest[len(after):]
        _, vjp = jax.vjp(_rms, x_ref[...], g_ref[...])
        dx, dg = vjp(dh_ref[...])
        dx = dx + dres_ref[...]
        dx_ref[...] = dx
        dxb_ref[...] = dx.astype(bf16)

        @pl.when(pl.program_id(0) == 0)
        def _():
            dg_ref[...] = jnp.zeros_like(dg_ref)

        dg_ref[...] += dg

    row = pl.BlockSpec((TT, D), lambda i: (i, 0))
    vec = pl.BlockSpec((1, D), lambda i: (0, 0))
    return pl.pallas_call(
        body, grid=(t // TT,), in_specs=[row, vec, row, row] + [pl.BlockSpec(memory_space=pl.ANY)] * len(after),
        out_specs=[row, row, vec],
        out_shape=[_sds((t, D), f32), _sds((t, D), bf16), _sds((1, D), f32)],
        compiler_params=_cp(), name=name)(x, gain, dh, dres, *after)


def _residual_add(x, y, name):
    t = x.shape[0]

    def body(x_ref, y_ref, o_ref):
        o_ref[...] = x_ref[...] + y_ref[...]

    row = pl.BlockSpec((TT, D), lambda i: (i, 0))
    return pl.pallas_call(body, grid=(t // TT,), in_specs=[row, row], out_specs=row,
                          out_shape=_sds((t, D), f32), compiler_params=_cp(), name=name)(x, y)


def _final_loss(x, gain, target, name):
    t = x.shape[0]

    def loss_fn(xv, gv, tv):
        e = _rms(xv, gv) - tv
        return 0.5 * jnp.sum(jnp.mean(e * e, axis=-1))

    def body(x_ref, g_ref, t_ref, loss_ref, dx_ref, dxb_ref, dg_ref):
        val, (dx, dg) = jax.value_and_grad(loss_fn, argnums=(0, 1))(x_ref[...], g_ref[...], t_ref[...])

        @pl.when(pl.program_id(0) == 0)
        def _():
            dg_ref[...] = jnp.zeros_like(dg_ref)
            loss_ref[...] = jnp.zeros_like(loss_ref)

        dx_ref[...] = dx
        dxb_ref[...] = dx.astype(bf16)
        dg_ref[...] += dg
        loss_ref[...] += jnp.full((1, 128), val, f32)

    row = pl.BlockSpec((TT, D), lambda i: (i, 0))
    vec = pl.BlockSpec((1, D), lambda i: (0, 0))
    return pl.pallas_call(
        body, grid=(t // TT,), in_specs=[row, vec, row],
        out_specs=[pl.BlockSpec((1, 128), lambda i: (0, 0)), row, row, vec],
        out_shape=[_sds((1, 128), f32), _sds((t, D), f32), _sds((t, D), bf16), _sds((1, D), f32)],
        compiler_params=_cp(), name=name)(x, gain, target)


def _conv_fwd(x, col0, c, w, b, name, tc=256, val=None, val_col0=0):
    t = x.shape[0]
    width = w.shape[0]
    nt = t // TT
    hb = TT // 8

    def body(*refs):
        if val is None:
            x_ref, xh_ref, w_ref, b_ref, o_ref, xp = refs
        else:
            x_ref, xh_ref, w_ref, b_ref, v_ref, o_ref, act_ref, xp = refs
        i = pl.program_id(1)
        xp[0:8, :] = jnp.where(i == 0, 0.0, xh_ref[...])
        xp[8:, :] = x_ref[...]
        acc = jnp.zeros((TT, tc), f32) + b_ref[...]
        for k in range(width):
            acc = acc + w_ref[k:k + 1, :] * xp[pl.ds(8 - (width - 1) + k, TT), :]
        o_ref[...] = acc
        if val is not None:
            act_ref[...] = (_silu(acc) * v_ref[...]).astype(bf16)

    in_specs = [
        pl.BlockSpec((TT, tc), lambda j, i: (i, j + col0)),
        pl.BlockSpec((8, tc), lambda j, i: (jnp.maximum(i * hb - 1, 0), j + col0)),
        pl.BlockSpec((width, tc), lambda j, i: (0, j)),
        pl.BlockSpec((1, tc), lambda j, i: (0, j)),
    ]
    args = [x, x, w, b]
    out_specs = [pl.BlockSpec((TT, tc), lambda j, i: (i, j))]
    out_shape = [_sds((t, c), f32)]
    if val is not None:
        in_specs.append(pl.BlockSpec((TT, tc), lambda j, i: (i, j + val_col0)))
        args.append(val)
        out_specs.append(pl.BlockSpec((TT, tc), lambda j, i: (i, j)))
        out_shape.append(_sds((t, c), bf16))
    res = pl.pallas_call(
        body, grid=(c // tc, nt), in_specs=in_specs, out_specs=out_specs, out_shape=out_shape,
        scratch_shapes=[pltpu.VMEM((TT + 8, tc), f32)], compiler_params=_cp(), name=name)(*args)
    return res[0] if val is None else res


def _conv_bwd(dc, x, col0, w, name, tc=256, dx_dtype=bf16):
    t, c = dc.shape
    width = w.shape[0]
    nt = t // TT
    hb = TT // 8

    def body(dc_ref, dcn_ref, x_ref, xh_ref, w_ref, dx_ref, dw_ref, db_ref, dcp, xp):
        i = pl.program_id(1)
        dcv = dc_ref[...]
        dcp[0:TT, :] = dcv
        dcp[TT:, :] = jnp.where(i == nt - 1, 0.0, dcn_ref[...])
        xp[0:8, :] = jnp.where(i == 0, 0.0, xh_ref[...])
        xp[8:, :] = x_ref[...]

        @pl.when(i == 0)
        def _():
            dw_ref[...] = jnp.zeros_like(dw_ref)
            db_ref[...] = jnp.zeros_like(db_ref)

        acc = jnp.zeros((TT, tc), f32)
        for k in range(width):
            acc = acc + w_ref[k:k + 1, :] * dcp[pl.ds((width - 1) - k, TT), :]
            dw_ref[k:k + 1, :] += jnp.sum(dcv * xp[pl.ds(8 - (width - 1) + k, TT), :], axis=0, keepdims=True)
        dx_ref[...] = acc.astype(dx_dtype)
        db_ref[...] += jnp.sum(dcv, axis=0, keepdims=True)

    in_specs = [
        pl.BlockSpec((TT, tc), lambda j, i: (i, j)),
        pl.BlockSpec((8, tc), lambda j, i: (jnp.minimum((i + 1) * hb, t // 8 - 1), j)),
        pl.BlockSpec((TT, tc), lambda j, i: (i, j + col0)),
        pl.BlockSpec((8, tc), lambda j, i: (jnp.maximum(i * hb - 1, 0), j + col0)),
        pl.BlockSpec((width, tc), lambda j, i: (0, j)),
    ]
    out_specs = [
        pl.BlockSpec((TT, tc), lambda j, i: (i, j)),
        pl.BlockSpec((8, tc), lambda j, i: (0, j)),
        pl.BlockSpec((1, tc), lambda j, i: (0, j)),
    ]
    return pl.pallas_call(
        body, grid=(c // tc, nt), in_specs=in_specs, out_specs=out_specs,
        out_shape=[_sds((t, c), dx_dtype), _sds((8, c), f32), _sds((1, c), f32)],
        scratch_shapes=[pltpu.VMEM((TT + 8, tc), f32), pltpu.VMEM((TT + 8, tc), f32)],
        compiler_params=_cp(), name=name)(dc, dc, x, x, w)


def _bdot_impl(a, b, dims):
    return _dot(a.astype(bf16), b.astype(bf16), dims)


@functools.partial(jax.custom_vjp, nondiff_argnums=(2,))
def _bdot(a, b, dims):
    return _bdot_impl(a, b, dims)


def _bdot_fwd(a, b, dims):
    return _bdot_impl(a, b, dims), (a, b)


def _bdot_bwd(dims, res, ct):
    a, b = res
    if dims == NN:
        return _bdot_impl(ct, b, NT), _bdot_impl(a, ct, TN)
    if dims == NT:
        return _bdot_impl(ct, b, NN), _bdot_impl(ct, a, TN)
    return _bdot_impl(b, ct, NT), _bdot_impl(a, ct, NN)


_bdot.defvjp(_bdot_fwd, _bdot_bwd)


def _split2(a):
    hi = a.astype(bf16)
    return hi, (a - hi.astype(f32)).astype(bf16)


def _dot3_impl(a, b, dims):
    a_hi, a_lo = _split2(a)
    b_hi, b_lo = _split2(b)
    return (_dot(a_hi, b_hi, dims) + _dot(a_hi, b_lo, dims)) + _dot(a_lo, b_hi, dims)


@functools.partial(jax.custom_vjp, nondiff_argnums=(2,))
def _dot3(a, b, dims):
    return _dot3_impl(a, b, dims)


def _dot3_fwd(a, b, dims):
    return _dot3_impl(a, b, dims), (a, b)


def _dot3_bwd(dims, res, ct):
    a, b = res
    if dims == NN:
        return _dot3_impl(ct, b, NT), _dot3_impl(a, ct, TN)
    if dims == NT:
        return _dot3_impl(ct, b, NN), _dot3_impl(ct, a, TN)
    return _dot3_impl(b, ct, NT), _dot3_impl(a, ct, NN)


_dot3.defvjp(_dot3_fwd, _dot3_bwd)


def _tril_dot_impl(tril, x, dims):
    t = tril.astype(bf16)
    x1 = x.astype(bf16)
    r1 = x - x1.astype(f32)
    x2 = r1.astype(bf16)
    x3 = (r1 - x2.astype(f32)).astype(bf16)
    return (_dot(t, x3, dims) + _dot(t, x2, dims)) + _dot(t, x1, dims)


@jax.custom_vjp
def _cumsum_rows(tril, x):
    return _tril_dot_impl(tril, x, NN)


def _cumsum_rows_fwd(tril, x):
    return _tril_dot_impl(tril, x, NN), tril


def _cumsum_rows_bwd(tril, ct):
    return jnp.zeros_like(tril), _tril_dot_impl(tril, ct, TN)


_cumsum_rows.defvjp(_cumsum_rows_fwd, _cumsum_rows_bwd)


def _gdn_intra(cq, ck, cv, ba, alog_v, dtb_v, hd):
    c = GDN_CHUNK
    lane = lax.broadcasted_iota(jnp.int32, (1, HEAD), 1)
    mb = (lane == hd).astype(f32)
    ma = (lane == hd + NH).astype(f32)
    beta = jax.nn.sigmoid(jnp.sum(ba * mb, axis=1, keepdims=True))
    alpha = jnp.sum(ba * ma, axis=1, keepdims=True)
    alog = jnp.sum(alog_v * ma, axis=1, keepdims=True)
    dtb = jnp.sum(dtb_v * ma, axis=1, keepdims=True)
    g = -jnp.exp(alog) * jax.nn.softplus(alpha + dtb)
    q = _silu(cq)
    q = q * lax.rsqrt(jnp.sum(q * q, axis=-1, keepdims=True) + EPS) * (HEAD ** -0.5)
    k = _silu(ck)
    k = k * lax.rsqrt(jnp.sum(k * k, axis=-1, keepdims=True) + EPS)
    v = _silu(cv)
    row = lax.broadcasted_iota(jnp.int32, (c, c), 0)
    col = lax.broadcasted_iota(jnp.int32, (c, c), 1)
    causal = row >= col
    tril = causal.astype(f32)
    gc = _cumsum_rows(tril, jnp.broadcast_to(g, (c, HEAD)))
    gcc = _cumsum_rows(tril, jnp.broadcast_to(g, (c, c)))
    diff = jnp.where(causal, gcc - gcc.T, 0.0)
    decay = jnp.where(causal, jnp.exp(diff), 0.0)
    kb = k * beta
    nmat = -jnp.where(row > col, _bdot(kb, k, NT) * decay, 0.0)
    egc = jnp.exp(gc)
    sol = jnp.concatenate([v * beta, kb * egc], axis=1)
    p = nmat
    for it in range(6):
        sol = sol + _dot3(p, sol, NN)
        if it < 5:
            p = _dot3(p, p, NN)
    u, w = sol[:, :HEAD], sol[:, HEAD:]
    attn = _bdot(q, k, NT) * decay
    rowv = lax.broadcasted_iota(jnp.int32, (c, 1), 0)
    gc_last = jnp.sum(jnp.where(rowv == c - 1, gc, 0.0), axis=0, keepdims=True)
    return u, w, q * egc, k * jnp.exp(gc_last - gc), attn, jnp.exp(gc_last)


def _gdn_intra4(cq, ck, cv, ba, alog_v, dtb_v):
    c = GDN_CHUNK
    hs = range(NH)
    lane = lax.broadcasted_iota(jnp.int32, (1, HEAD), 1)
    mb = [(lane == h).astype(f32) for h in hs]
    ma = [(lane == h + NH).astype(f32) for h in hs]
    beta = [jax.nn.sigmoid(jnp.sum(ba * mb[h], axis=1, keepdims=True)) for h in hs]
    alpha = [jnp.sum(ba * ma[h], axis=1, keepdims=True) for h in hs]
    alog = [jnp.sum(alog_v * ma[h], axis=1, keepdims=True) for h in hs]
    dtb = [jnp.sum(dtb_v * ma[h], axis=1, keepdims=True) for h in hs]
    g = [-jnp.exp(alog[h]) * jax.nn.softplus(alpha[h] + dtb[h]) for h in hs]
    q = [_silu(cq[h]) for h in hs]
    q = [q[h] * lax.rsqrt(jnp.sum(q[h] * q[h], axis=-1, keepdims=True) + EPS) * (HEAD ** -0.5) for h in hs]
    k = [_silu(ck[h]) for h in hs]
    k = [k[h] * lax.rsqrt(jnp.sum(k[h] * k[h], axis=-1, keepdims=True) + EPS) for h in hs]
    v = [_silu(cv[h]) for h in hs]
    row = lax.broadcasted_iota(jnp.int32, (c, c), 0)
    col = lax.broadcasted_iota(jnp.int32, (c, c), 1)
    causal = row >= col
    tril = causal.astype(f32)
    gc = [_cumsum_rows(tril, jnp.broadcast_to(g[h], (c, HEAD))) for h in hs]
    gcc = [_cumsum_rows(tril, jnp.broadcast_to(g[h], (c, c))) for h in hs]
    decay = [jnp.where(causal, jnp.exp(jnp.where(causal, gcc[h] - gcc[h].T, 0.0)), 0.0) for h in hs]
    kb = [k[h] * beta[h] for h in hs]
    kk = [_bdot(kb[h], k[h], NT) for h in hs]
    p = [-jnp.where(row > col, kk[h] * decay[h], 0.0) for h in hs]
    egc = [jnp.exp(gc[h]) for h in hs]
    sol = [jnp.concatenate([v[h] * beta[h], kb[h] * egc[h]], axis=1) for h in hs]
    for it in range(6):
        upd = [_dot3(p[h], sol[h], NN) for h in hs]
        sol = [sol[h] + upd[h] for h in hs]
        if it < 5:
            p = [_dot3(p[h], p[h], NN) for h in hs]
    qk = [_bdot(q[h], k[h], NT) for h in hs]
    attn = [qk[h] * decay[h] for h in hs]
    rowv = lax.broadcasted_iota(jnp.int32, (c, 1), 0)
    gc_last = [jnp.sum(jnp.where(rowv == c - 1, gc[h], 0.0), axis=0, keepdims=True) for h in hs]
    return ([sol[h][:, :HEAD] for h in hs], [sol[h][:, HEAD:] for h in hs], [q[h] * egc[h] for h in hs],
            [k[h] * jnp.exp(gc_last[h] - gc[h]) for h in hs], attn, [jnp.exp(gc_last[h]) for h in hs])


def _gdn_seq(u, w, q_dec, k_dec, attn, dl, z, s, gain):
    v_new = u - _bdot(w, s, NN)
    o = _bdot(q_dec, s, NN) + _bdot(attn, v_new, NN)
    s_new = s * dl + _bdot(k_dec, v_new, TN)
    return _rms(o, gain) * _silu(z), s_new


def _gdn_seq4(u, w, q_dec, k_dec, attn, dl, z, s, gain):
    hs = range(NH)
    ws = [_bdot(w[h], s[h], NN) for h in hs]
    qs = [_bdot(q_dec[h], s[h], NN) for h in hs]
    v_new = [u[h] - ws[h] for h in hs]
    av = [_bdot(attn[h], v_new[h], NN) for h in hs]
    kv = [_bdot(k_dec[h], v_new[h], TN) for h in hs]
    o = [_rms(qs[h] + av[h], gain) * _silu(z[h]) for h in hs]
    return o, [s[h] * dl[h] + kv[h] for h in hs]


def _hsl(h):
    return slice(h * HEAD, (h + 1) * HEAD)


def _gdn2_fwd(conv, p, alog_v, dtb_v, gain, name):
    t = conv.shape[0]
    c = GDN_CHUNK
    nch = t // c
    w512 = NH * HEAD
    wide = lambda off: pl.BlockSpec((c, w512), lambda n: (n, off))
    vec = pl.BlockSpec((1, HEAD), lambda n: (0, 0))
    attn_spec = pl.BlockSpec((1, NH, c, c), lambda n: (n, 0, 0, 0))
    dl_spec = pl.BlockSpec((1, NH, HEAD), lambda n: (n, 0, 0))

    cps = GDN_INTRA_CHUNKS
    iwide = lambda off: pl.BlockSpec((cps * c, w512), lambda n: (n, off))

    def intra(cq, ck, cv, ba, al, dt, u_ref, w_ref, qd_ref, kd_ref, at_ref, dl_ref):
        for ci in range(cps):
            rows = slice(ci * c, (ci + 1) * c)
            u, w, qd, kd, at, dl = _gdn_intra4([cq[rows, _hsl(h)] for h in range(NH)], [ck[rows, _hsl(h)] for h in range(NH)],
                                               [cv[rows, _hsl(h)] for h in range(NH)], ba[rows, :], al[...], dt[...])
            for h in range(NH):
                u_ref[rows, _hsl(h)] = u[h]
                w_ref[rows, _hsl(h)] = w[h]
                qd_ref[rows, _hsl(h)] = qd[h]
                kd_ref[rows, _hsl(h)] = kd[h]
                at_ref[ci, h] = at[h]
                dl_ref[ci, h:h + 1, :] = dl[h]

    u, w, qd, kd, at, dl = pl.pallas_call(
        intra, grid=(nch // cps,),
        in_specs=[iwide(0), iwide(1), iwide(2), pl.BlockSpec((cps * c, HEAD), lambda n: (n, C_BA // HEAD)), vec, vec],
        out_specs=[iwide(0)] * 4 + [pl.BlockSpec((cps, NH, c, c), lambda n: (n, 0, 0, 0)),
                                    pl.BlockSpec((cps, NH, HEAD), lambda n: (n, 0, 0))],
        out_shape=[_sds((t, w512), f32)] * 4 + [_sds((nch, NH, c, c), f32), _sds((nch, NH, HEAD), f32)],
        compiler_params=_cp(), name=name + "_intra")(conv, conv, conv, p, alog_v, dtb_v)

    def seq(u_ref, w_ref, qd_ref, kd_ref, at_ref, dl_ref, z_ref, gn, o_ref, ss_ref, s_scr):
        @pl.when(pl.program_id(0) == 0)
        def _():
            s_scr[...] = jnp.zeros_like(s_scr)

        hs = range(NH)
        s = [s_scr[h] for h in hs]
        for h in hs:
            ss_ref[0, h] = s[h]
        o, s_new = _gdn_seq4([u_ref[:, _hsl(h)] for h in hs], [w_ref[:, _hsl(h)] for h in hs], [qd_ref[:, _hsl(h)] for h in hs],
                             [kd_ref[:, _hsl(h)] for h in hs], [at_ref[0, h] for h in hs], [dl_ref[0, h:h + 1, :] for h in hs],
                             [z_ref[:, _hsl(h)] for h in hs], s, gn[...])
        for h in hs:
            o_ref[:, _hsl(h)] = o[h].astype(bf16)
            s_scr[h] = s_new[h]

    o, states = pl.pallas_call(
        seq, grid=(nch,),
        in_specs=[wide(0)] * 4 + [attn_spec, dl_spec, wide(C_Z // w512), vec],
        out_specs=[wide(0), pl.BlockSpec((1, NH, HEAD, HEAD), lambda n: (n, 0, 0, 0))],
        out_shape=[_sds((t, w512), bf16), _sds((nch, NH, HEAD, HEAD), f32)],
        scratch_shapes=[pltpu.VMEM((NH, HEAD, HEAD), f32)],
        compiler_params=_cp(), name=name + "_seq")(u, w, qd, kd, at, dl, p, gain)
    return o, dict(u=u, w=w, qd=qd, kd=kd, at=at, dl=dl, states=states)


def _gdn2_bwd(conv, p, alog_v, dtb_v, gain, saved, do, name):
    t = conv.shape[0]
    c = GDN_CHUNK
    nch = t // c
    w512 = NH * HEAD
    rwide = lambda off: pl.BlockSpec((c, w512), lambda n: (nch - 1 - n, off))
    rvec = pl.BlockSpec((1, HEAD), lambda n: (0, 0))
    rattn = pl.BlockSpec((1, NH, c, c), lambda n: (nch - 1 - n, 0, 0, 0))
    rdl = pl.BlockSpec((1, NH, HEAD), lambda n: (nch - 1 - n, 0, 0))

    def seq_bwd(u_ref, w_ref, qd_ref, kd_ref, at_ref, dl_ref, z_ref, gn, ss_ref, do_ref,
                du_ref, dw_ref, dqd_ref, dkd_ref, dat_ref, ddl_ref, dz_ref, dgn_ref, ds_scr):
        @pl.when(pl.program_id(0) == 0)
        def _():
            ds_scr[...] = jnp.zeros_like(ds_scr)
            dgn_ref[...] = jnp.zeros_like(dgn_ref)

        hs = range(NH)
        _, vjp = jax.vjp(_gdn_seq4, [u_ref[:, _hsl(h)] for h in hs], [w_ref[:, _hsl(h)] for h in hs],
                         [qd_ref[:, _hsl(h)] for h in hs], [kd_ref[:, _hsl(h)] for h in hs], [at_ref[0, h] for h in hs],
                         [dl_ref[0, h:h + 1, :] for h in hs], [z_ref[:, _hsl(h)] for h in hs], [ss_ref[0, h] for h in hs], gn[...])
        du, dw, dqd, dkd, dat, ddl, dz, ds, dg = vjp(([do_ref[:, _hsl(h)] for h in hs], [ds_scr[h] for h in hs]))
        for h in hs:
            du_ref[:, _hsl(h)] = du[h]
            dw_ref[:, _hsl(h)] = dw[h]
            dqd_ref[:, _hsl(h)] = dqd[h]
            dkd_ref[:, _hsl(h)] = dkd[h]
            dat_ref[0, h] = dat[h]
            ddl_ref[0, h:h + 1, :] = ddl[h]
            dz_ref[:, _hsl(h)] = dz[h]
            ds_scr[h] = ds[h]
        dgn_ref[...] += dg

    du, dw, dqd, dkd, dat, ddl, dz, dgn = pl.pallas_call(
        seq_bwd, grid=(nch,),
        in_specs=[rwide(0)] * 4 + [rattn, rdl, rwide(C_Z // w512), rvec,
                                   pl.BlockSpec((1, NH, HEAD, HEAD), lambda n: (nch - 1 - n, 0, 0, 0)), rwide(0)],
        out_specs=[rwide(0)] * 4 + [rattn, rdl, rwide(0), rvec],
        out_shape=[_sds((t, w512), f32)] * 4 + [_sds((nch, NH, c, c), f32), _sds((nch, NH, HEAD), f32),
                                                _sds((t, w512), f32), _sds((1, HEAD), f32)],
        scratch_shapes=[pltpu.VMEM((NH, HEAD, HEAD), f32)],
        compiler_params=_cp(), name=name + "_seq")(
            saved["u"], saved["w"], saved["qd"], saved["kd"], saved["at"], saved["dl"], p, gain, saved["states"], do)

    cps = GDN_INTRA_CHUNKS
    wide = lambda off: pl.BlockSpec((cps * c, w512), lambda n: (n, off))
    vec = pl.BlockSpec((1, HEAD), lambda n: (0, 0))
    attn_spec = pl.BlockSpec((cps, NH, c, c), lambda n: (n, 0, 0, 0))
    dl_spec = pl.BlockSpec((cps, NH, HEAD), lambda n: (n, 0, 0))

    def intra_bwd(cq, ck, cv, ba, al, dt, du_ref, dw_ref, dqd_ref, dkd_ref, dat_ref, ddl_ref,
                  dc_ref, dba_ref, dal_ref, ddt_ref):
        @pl.when(pl.program_id(0) == 0)
        def _():
            dal_ref[...] = jnp.zeros_like(dal_ref)
            ddt_ref[...] = jnp.zeros_like(ddt_ref)

        dal = jnp.zeros((1, HEAD), f32)
        ddt = jnp.zeros((1, HEAD), f32)
        for ci in range(cps):
            rows = slice(ci * c, (ci + 1) * c)
            hs = range(NH)
            _, vjp = jax.vjp(_gdn_intra4, [cq[rows, _hsl(h)] for h in hs], [ck[rows, _hsl(h)] for h in hs],
                             [cv[rows, _hsl(h)] for h in hs], ba[rows, :], al[...], dt[...])
            g_q, g_k, g_v, g_ba, g_al, g_dt = vjp((
                [du_ref[rows, _hsl(h)] for h in hs], [dw_ref[rows, _hsl(h)] for h in hs], [dqd_ref[rows, _hsl(h)] for h in hs],
                [dkd_ref[rows, _hsl(h)] for h in hs], [dat_ref[ci, h] for h in hs], [ddl_ref[ci, h:h + 1, :] for h in hs]))
            for h in hs:
                dc_ref[rows, _hsl(h)] = g_q[h]
                dc_ref[rows, _hsl(NH + h)] = g_k[h]
                dc_ref[rows, _hsl(2 * NH + h)] = g_v[h]
            dal = dal + g_al
            ddt = ddt + g_dt
            dba_ref[rows, :] = g_ba
        dal_ref[...] += dal
        ddt_ref[...] += ddt

    dconv, dba, dal, ddt = pl.pallas_call(
        intra_bwd, grid=(nch // cps,),
        in_specs=[wide(0), wide(1), wide(2), pl.BlockSpec((cps * c, HEAD), lambda n: (n, C_BA // HEAD)), vec, vec]
        + [wide(0)] * 4 + [attn_spec, dl_spec],
        out_specs=[pl.BlockSpec((cps * c, 3 * w512), lambda n: (n, 0)), pl.BlockSpec((cps * c, HEAD), lambda n: (n, 0)),
                   vec, vec],
        out_shape=[_sds((t, 3 * w512), f32), _sds((t, HEAD), f32), _sds((1, HEAD), f32), _sds((1, HEAD), f32)],
        compiler_params=_cp(), name=name + "_intra")(conv, conv, conv, p, alog_v, dtb_v, du, dw, dqd, dkd, dat, ddl)
    return dconv, dz, dba, dal, ddt, dgn


def _hgrn_intra(qb, fb, ib, lb):
    c = HGRN_CHUNK
    ns = range(len(qb))
    f = [lb + (1.0 - lb) * jax.nn.sigmoid(fb[i]) for i in ns]
    logf = [jnp.log(jnp.maximum(f[i], F_FLOOR)) for i in ns]
    k = [1.0 - f[i] for i in ns]
    q = [_silu(qb[i]) for i in ns]
    row = lax.broadcasted_iota(jnp.int32, (c, c), 0)
    col = lax.broadcasted_iota(jnp.int32, (c, c), 1)
    tril = (row >= col).astype(f32)
    b = [_cumsum_rows(tril, logf[i]) for i in ns]
    ri = lax.broadcasted_iota(jnp.int32, (c, 1), 0)
    o = [jnp.zeros((c, HEAD), f32) for _ in ns]
    for j in range(c):
        mj = ri == j
        ok = ri >= j
        bj = [jnp.sum(jnp.where(mj, b[i], 0.0), axis=0, keepdims=True) for i in ns]
        kj = [jnp.sum(jnp.where(mj, k[i], 0.0), axis=0, keepdims=True) for i in ns]
        vj = [jnp.sum(jnp.where(mj, ib[i], 0.0), axis=0, keepdims=True) for i in ns]
        e = [jnp.where(ok, jnp.exp(jnp.where(ok, b[i] - bj[i], 0.0)), 0.0) for i in ns]
        s = [jnp.sum(q[i] * kj[i] * e[i], axis=1, keepdims=True) for i in ns]
        o = [o[i] + s[i] * vj[i] for i in ns]
    b_last = [jnp.sum(jnp.where(ri == c - 1, b[i], 0.0), axis=0, keepdims=True) for i in ns]
    return (o, [q[i] * jnp.exp(b[i]) for i in ns], [k[i] * jnp.exp(b_last[i] - b[i]) for i in ns],
            [jnp.exp(b_last[i]) for i in ns])


def _hgrn_seq(o_intra, q_dec, k_dec, dl, v, gb, st, gain):
    o = o_intra + _bdot(q_dec, st, NT)
    st_new = st * dl + _bdot(v, k_dec, TN)
    return _rms(o, gain) * _silu(gb), st_new


def _hgrn_fwd(p, lb, gain, name):
    t = p.shape[0]
    r = HGRN_STEP
    ns = t // r
    nsub = r // HGRN_CHUNK
    blk = lambda off: pl.BlockSpec((r, HEAD), lambda n, h: (n, off // HEAD + h))

    def body(qb, fb, ib, gb, lb_ref, gn, o_ref, ss_ref, s_scr):
        n, h = pl.program_id(0), pl.program_id(1)

        @pl.when(n == 0)
        def _():
            s_scr[h] = jnp.zeros((HEAD, HEAD), f32)

        st = s_scr[h]
        ss_ref[0, 0] = st
        rows = [pl.ds(ch * HGRN_CHUNK, HGRN_CHUNK) for ch in range(nsub)]
        v = [ib[rw, :] for rw in rows]
        oi, qd, kd, dl = _hgrn_intra([qb[rw, :] for rw in rows], [fb[rw, :] for rw in rows], v, lb_ref[...])
        for ch in range(nsub):
            o, st = _hgrn_seq(oi[ch], qd[ch], kd[ch], dl[ch], v[ch], gb[rows[ch], :], st, gn[...])
            o_ref[rows[ch], :] = o.astype(bf16)
        s_scr[h] = st

    return pl.pallas_call(
        body, grid=(ns, NH),
        in_specs=[blk(C_QB), blk(C_FB), blk(C_IB), blk(C_GB),
                  pl.BlockSpec((1, HEAD), lambda n, h: (0, h)), pl.BlockSpec((1, HEAD), lambda n, h: (0, 0))],
        out_specs=[pl.BlockSpec((r, HEAD), lambda n, h: (n, h)),
                   pl.BlockSpec((1, 1, HEAD, HEAD), lambda n, h: (n, h, 0, 0))],
        out_shape=[_sds((t, NH * HEAD), bf16), _sds((ns, NH, HEAD, HEAD), f32)],
        scratch_shapes=[pltpu.VMEM((NH, HEAD, HEAD), f32)],
        compiler_params=_cp(), name=name)(p, p, p, p, lb, gain)


def _hgrn_bwd(p, lb, gain, states, do, name):
    t = p.shape[0]
    r = HGRN_STEP
    ns = t // r
    nsub = r // HGRN_CHUNK
    blk = lambda off: pl.BlockSpec((r, HEAD), lambda n, h: (ns - 1 - n, off // HEAD + h))
    hblk = pl.BlockSpec((r, HEAD), lambda n, h: (ns - 1 - n, h))

    def body(qb, fb, ib, gb, lb_ref, gn, ss_ref, do_ref,
             dqb, dfb, dib, dgb, dlb_ref, dgn_ref, ds_scr, st_scr):
        n, h = pl.program_id(0), pl.program_id(1)

        @pl.when(n == 0)
        def _():
            ds_scr[h] = jnp.zeros((HEAD, HEAD), f32)

        @pl.when((n == 0) & (h == 0))
        def _():
            dgn_ref[...] = jnp.zeros_like(dgn_ref)

        gnv = gn[...]
        rows = [pl.ds(ch * HGRN_CHUNK, HGRN_CHUNK) for ch in range(nsub)]
        v = [ib[rw, :] for rw in rows]
        (oi, qd, kd, dl), vjp_intra = jax.vjp(_hgrn_intra, [qb[rw, :] for rw in rows], [fb[rw, :] for rw in rows], v, lb_ref[...])
        st = ss_ref[0, 0]
        for ch in range(nsub):
            st_scr[ch] = st
            if ch < nsub - 1:
                st = st * dl[ch] + _bdot(v[ch], kd[ch], TN)
        ds = ds_scr[h]
        dgn = jnp.zeros((1, HEAD), f32)
        d_oi, d_qd, d_kd, d_dl, d_v = [None] * nsub, [None] * nsub, [None] * nsub, [None] * nsub, [None] * nsub
        for ch in reversed(range(nsub)):
            _, vjp = jax.vjp(_hgrn_seq, oi[ch], qd[ch], kd[ch], dl[ch], v[ch], gb[rows[ch], :], st_scr[ch], gnv)
            d_oi[ch], d_qd[ch], d_kd[ch], d_dl[ch], d_v[ch], g_g, ds, g_gn = vjp((do_ref[rows[ch], :], ds))
            dgb[rows[ch], :] = g_g
            dgn = dgn + g_gn
        g_q, g_f, g_i, g_lb = vjp_intra((d_oi, d_qd, d_kd, d_dl))
        for ch in range(nsub):
            dqb[rows[ch], :] = g_q[ch]
            dfb[rows[ch], :] = g_f[ch]
            dib[rows[ch], :] = g_i[ch] + d_v[ch]
        ds_scr[h] = ds
        dlb_ref[0] = g_lb
        dgn_ref[...] += dgn

    return pl.pallas_call(
        body, grid=(ns, NH),
        in_specs=[blk(C_QB), blk(C_FB), blk(C_IB), blk(C_GB),
                  pl.BlockSpec((1, HEAD), lambda n, h: (0, h)), pl.BlockSpec((1, HEAD), lambda n, h: (0, 0)),
                  pl.BlockSpec((1, 1, HEAD, HEAD), lambda n, h: (ns - 1 - n, h, 0, 0)),
                  pl.BlockSpec((r, HEAD), lambda n, h: (ns - 1 - n, NH + h))],
        out_specs=[hblk, hblk, hblk, hblk,
                   pl.BlockSpec((1, 1, HEAD), lambda n, h: (n, 0, h)),
                   pl.BlockSpec((1, HEAD), lambda n, h: (0, 0))],
        out_shape=[_sds((t, 512), f32)] * 4 + [_sds((ns, 1, 512), f32), _sds((1, HEAD), f32)],
        scratch_shapes=[pltpu.VMEM((NH, HEAD, HEAD), f32), pltpu.VMEM((nsub, HEAD, HEAD), f32)],
        compiler_params=_cp(), name=name)(p, p, p, p, lb, gain, states, do)


def _lru_gates(xb, wa, wx, ba, bx, lam):
    xh = xb.astype(bf16)
    r = jax.nn.sigmoid(_dot(xh, wa.astype(bf16), NN) + ba)
    i = jax.nn.sigmoid(_dot(xh, wx.astype(bf16), NN) + bx)
    log_a = -RG_C * r * jax.nn.softplus(-lam)
    a = jnp.exp(log_a)
    t2 = 2.0 * log_a
    series = -t2 * (1.0 + t2 * (0.5 + t2 * (1.0 / 6.0 + t2 * (1.0 / 24.0))))
    om = jnp.where(t2 > -1e-2, series, 1.0 - jnp.exp(t2))
    u = jnp.sqrt(jnp.maximum(om, 0.0)) * (i * xb)
    return a, u


def _lru_gates_fwd(xc, wa, wx, ba, bx, lam, name):
    t = xc.shape[0]
    blk = pl.BlockSpec((TT, LRU_BLOCK), lambda h, i: (i, h))
    wsp = pl.BlockSpec((1, LRU_BLOCK, LRU_BLOCK), lambda h, i: (h, 0, 0))
    vsp = pl.BlockSpec((1, LRU_BLOCK), lambda h, i: (0, h))

    def body(x_ref, wa_ref, wx_ref, ba_ref, bx_ref, lam_ref, a_ref, u_ref):
        a, u = _lru_gates(x_ref[...], wa_ref[0], wx_ref[0], ba_ref[...], bx_ref[...], lam_ref[...])
        a_ref[...] = a
        u_ref[...] = u

    return pl.pallas_call(
        body, grid=(NH, t // TT), in_specs=[blk, wsp, wsp, vsp, vsp, vsp], out_specs=[blk, blk],
        out_shape=[_sds((t, D), f32)] * 2, compiler_params=_cp(), name=name)(xc, wa, wx, ba, bx, lam)


def _lru_gates_bwd(xc, wa, wx, ba, bx, lam, da, du, name):
    t = xc.shape[0]
    blk = pl.BlockSpec((TT, LRU_BLOCK), lambda h, i: (i, h))
    wsp = pl.BlockSpec((1, LRU_BLOCK, LRU_BLOCK), lambda h, i: (h, 0, 0))
    vsp = pl.BlockSpec((1, LRU_BLOCK), lambda h, i: (0, h))

    def body(x_ref, wa_ref, wx_ref, ba_ref, bx_ref, lam_ref, da_ref, du_ref,
             dx_ref, dwa_ref, dwx_ref, dba_ref, dbx_ref, dlam_ref):
        @pl.when(pl.program_id(1) == 0)
        def _():
            for r in (dwa_ref, dwx_ref, dba_ref, dbx_ref, dlam_ref):
                r[...] = jnp.zeros_like(r)

        _, vjp = jax.vjp(_lru_gates, x_ref[...], wa_ref[0], wx_ref[0], ba_ref[...], bx_ref[...], lam_ref[...])
        dx, dwa, dwx, dba, dbx, dlam = vjp((da_ref[...], du_ref[...]))
        dx_ref[...] = dx
        dwa_ref[0] += dwa
        dwx_ref[0] += dwx
        dba_ref[...] += dba
        dbx_ref[...] += dbx
        dlam_ref[...] += dlam

    return pl.pallas_call(
        body, grid=(NH, t // TT), in_specs=[blk, wsp, wsp, vsp, vsp, vsp, blk, blk],
        out_specs=[blk, wsp, wsp, vsp, vsp, vsp],
        out_shape=[_sds((t, D), f32), _sds((NH, LRU_BLOCK, LRU_BLOCK), f32), _sds((NH, LRU_BLOCK, LRU_BLOCK), f32),
                   _sds((1, D), f32), _sds((1, D), f32), _sds((1, D), f32)],
        compiler_params=_cp(), name=name)(xc, wa, wx, ba, bx, lam, da, du)


_SCAN_SHIFTS = (1, 2, 4, 8, 16, 32, 64, 128)
_SCAN_PAD = 128


def _gelu(y):
    return jax.nn.gelu(y, approximate=True)


def _lru_scan_fwd(a, u, p2, name):
    t = a.shape[0]
    tc = 128
    blk = pl.BlockSpec((TT, tc), lambda j, i: (i, j))

    def body(a_ref, u_ref, y_ref, h_ref, hg_ref, a_s, b_s, carry):
        i = pl.program_id(1)

        @pl.when(i == 0)
        def _():
            carry[...] = jnp.zeros_like(carry)
            a_s[0:_SCAN_PAD, :] = jnp.ones((_SCAN_PAD, tc), f32)
            b_s[0:_SCAN_PAD, :] = jnp.zeros((_SCAN_PAD, tc), f32)

        av, bv = a_ref[...], u_ref[...]
        for s in _SCAN_SHIFTS:
            a_s[_SCAN_PAD:, :] = av
            b_s[_SCAN_PAD:, :] = bv
            ash = a_s[pl.ds(_SCAN_PAD - s, TT), :]
            bsh = b_s[pl.ds(_SCAN_PAD - s, TT), :]
            bv = bv + av * bsh
            av = av * ash
        h = bv + av * carry[7:8, :]
        h_ref[...] = h
        hg_ref[...] = (h * _gelu(y_ref[...])).astype(bf16)
        carry[...] = h[TT - 8:, :]

    return pl.pallas_call(
        body, grid=(D // tc, t // TT), in_specs=[blk, blk, blk], out_specs=[blk, blk],
        out_shape=[_sds((t, D), f32), _sds((t, D), bf16)],
        scratch_shapes=[pltpu.VMEM((_SCAN_PAD + TT, tc), f32), pltpu.VMEM((_SCAN_PAD + TT, tc), f32),
                        pltpu.VMEM((8, tc), f32)],
        compiler_params=_cp(), name=name)(a, u, p2)


def _lru_scan_bwd(a, h, p2, dhg, name):
    t = a.shape[0]
    tc = 128
    nt = t // TT
    hb = TT // 8
    rblk = pl.BlockSpec((TT, tc), lambda j, i: (nt - 1 - i, j))

    def body(a_ref, an_ref, h_ref, hp_ref, y_ref, dhg_ref, du_ref, da_ref, dy_ref, a_s, b_s, ap, hp, carry):
        i = pl.program_id(1)

        @pl.when(i == 0)
        def _():
            carry[...] = jnp.zeros_like(carry)
            a_s[TT:, :] = jnp.ones((_SCAN_PAD, tc), f32)
            b_s[TT:, :] = jnp.zeros((_SCAN_PAD, tc), f32)

        ap[0:TT, :] = a_ref[...]
        ap[TT:, :] = jnp.where(i == 0, 0.0, an_ref[...])
        hp[0:8, :] = jnp.where(i == nt - 1, 0.0, hp_ref[...])
        hp[8:, :] = h_ref[...]
        y = y_ref[...]
        gate, gvjp = jax.vjp(_gelu, y)
        dhg_v = dhg_ref[...]
        dy_ref[...] = gvjp(dhg_v * h_ref[...])[0]
        av = ap[pl.ds(1, TT), :]
        bv = dhg_v * gate
        for s in _SCAN_SHIFTS:
            a_s[0:TT, :] = av
            b_s[0:TT, :] = bv
            ash = a_s[pl.ds(s, TT), :]
            bsh = b_s[pl.ds(s, TT), :]
            bv = bv + av * bsh
            av = av * ash
        g = bv + av * carry[0:1, :]
        du_ref[...] = g
        da_ref[...] = g * hp[pl.ds(7, TT), :]
        carry[...] = g[0:8, :]

    in_specs = [
        rblk,
        pl.BlockSpec((8, tc), lambda j, i: (jnp.minimum((nt - i) * hb, t // 8 - 1), j)),
        rblk,
        pl.BlockSpec((8, tc), lambda j, i: (jnp.maximum((nt - 1 - i) * hb - 1, 0), j)),
        rblk, rblk,
    ]
    return pl.pallas_call(
        body, grid=(D // tc, nt), in_specs=in_specs, out_specs=[rblk, rblk, rblk],
        out_shape=[_sds((t, D), f32)] * 3,
        scratch_shapes=[pltpu.VMEM((TT + _SCAN_PAD, tc), f32), pltpu.VMEM((TT + _SCAN_PAD, tc), f32),
                        pltpu.VMEM((TT + 8, tc), f32), pltpu.VMEM((TT + 8, tc), f32), pltpu.VMEM((8, tc), f32)],
        compiler_params=_cp(), name=name)(a, a, h, h, p2, dhg)


def _ffn_act_bwd(gc, up, dact, name, tc=256):
    t = gc.shape[0]
    blk = pl.BlockSpec((TT, tc), lambda i, j: (i, j))

    def body(gc_ref, v_ref, da_ref, dgc_ref, dv_ref):
        _, vjp = jax.vjp(lambda g, v: _silu(g) * v, gc_ref[...], v_ref[...])
        dg, dv = vjp(da_ref[...])
        dgc_ref[...] = dg
        dv_ref[...] = dv.astype(bf16)

    return pl.pallas_call(
        body, grid=(t // TT, D_FF // tc),
        in_specs=[blk, pl.BlockSpec((TT, tc), lambda i, j: (i, j + D_FF // tc)), blk], out_specs=[blk, blk],
        out_shape=[_sds((t, D_FF), f32), _sds((t, D_FF), bf16)], compiler_params=_cp(), name=name)(gc, up, dact)


def _lower_bounds_fwd(w):
    def body(w_ref, o0_ref, o1_ref):
        wv = w_ref[...]
        o0, o1 = _lb_rows(wv[0:1, :], wv[1:2, :])
        o0_ref[...] = o0
        o1_ref[...] = o1

    return pl.pallas_call(body, out_shape=[_sds((1, 512), f32)] * 2, name="lower_bounds_fwd")(w)


def _lb_rows(w0, w1):
    m = jnp.maximum(w0, w1)
    e0, e1 = jnp.exp(w0 - m), jnp.exp(w1 - m)
    s = e0 + e1
    p0, p1 = e0 / s, e1 / s
    return p0 - p0, (p0 + p1) - p0


def _lower_bounds_bwd(w, d0, d1):
    def body(w_ref, d0_ref, d1_ref, g0_ref, g1_ref):
        wv = w_ref[...]
        _, vjp = jax.vjp(_lb_rows, wv[0:1, :], wv[1:2, :])
        g0, g1 = vjp((d0_ref[...], d1_ref[...]))
        g0_ref[...] = g0
        g1_ref[...] = g1

    return pl.pallas_call(body, out_shape=[_sds((1, 512), f32)] * 2, name="lower_bounds_bwd")(w, d0, d1)


def _local_step(x, target, wt, pre_layer=None, post_grads=None):
    depth = 4
    res = []
    lb0, lb1 = _lower_bounds_fwd(wt["hgrn_lower_bounds"])
    lbs = [lb0, lb1]
    for layer in range(depth):
        j = layer // 2
        sv = {"x_in": x}
        deps = pre_layer(layer, "mix", x) if pre_layer else ()
        if layer == 0:
            h1 = _rmsnorm_fwd(x, wt["norm_mix"][layer], "rms_fwd", deps)
        sv["h1"] = h1
        if layer % 2 == 0:
            p = _mm_auto(h1, wt["ab_w_in"][j], "nn", f32, "mm_ab_in")
            conv = _conv_fwd(p, 0, 1536, wt["gdn_conv_w"][j], jnp.zeros((1, 1536), f32), "gdn_conv_fwd", tc=768)
            o_a, s_a = _gdn2_fwd(conv, p, wt["alog_v"][j], wt["dtb_v"][j], wt["gdn_norm"][j], "gdn_fwd")
            o_b, s_b = _hgrn_fwd(p, lbs[j], wt["hgrn_norm"][j], "hgrn_fwd")
            o = jnp.concatenate([o_a, o_b], axis=1)
            x, h2 = _mm_res_norm(o, wt["ab_w_out"][j], x, wt["norm_ffn"][layer], "mm_ab_out")
            sv.update(p=p, conv=conv, s_a=s_a, s_b=s_b, o=o)
        else:
            p2 = _mm_auto(h1, wt["c_w_in"][j], "nn", f32, "mm_c_in")
            xc = _conv_fwd(p2, 1, D, wt["c_conv_w"][j], wt["c_conv_b"][j], "lru_conv_fwd", tc=D)
            a, u = _lru_gates_fwd(xc, wt["c_gate_a_w"][j], wt["c_gate_x_w"][j], wt["c_gate_a_b"][j],
                                  wt["c_gate_x_b"][j], wt["c_lambda"][j], "lru_gates_fwd")
            h, hg = _lru_scan_fwd(a, u, p2, "lru_scan_fwd")
            x, h2 = _mm_res_norm(hg, wt["c_w_out"][j], x, wt["norm_ffn"][layer], "mm_c_out")
            sv.update(p2=p2, xc=xc, a=a, h=h, hg=hg)
        sv["x_mid"] = x
        if pre_layer:
            pre_layer(layer, "ffn", x)
        up = _mm_auto(h2, wt["ffn_w_up"][layer], "nn", f32, "mm_up")
        gc, act = _conv_fwd(up, 0, D_FF, wt["ffn_conv_w"][layer], wt["ffn_conv_b"][layer], "ffn_conv_fwd",
                            tc=D_FF // 2, val=up, val_col0=2)
        if layer + 1 < depth:
            x, h1 = _mm_res_norm(act, wt["ffn_w_down"][layer], x, wt["norm_mix"][layer + 1], "mm_down")
        else:
            x = _mm_res_norm(act, wt["ffn_w_down"][layer], x, None, "mm_down_last")
        sv.update(h2=h2, up=up, gc=gc, act=act)
        res.append(sv)

    loss, dx, dxb, d_norm_final = _final_loss(x, wt["norm_final"], target, "final_loss")

    g = {k: [None] * len(v) for k, v in wt.items() if isinstance(v, list)}
    g["norm_final"] = d_norm_final
    d_lbs = [None, None]
    for layer in reversed(range(depth)):
        j = layer // 2
        sv = res[layer]
        dact = _mm_auto(dxb, wt["ffn_w_down"][layer], "nt", f32, "mm_down_dx")
        g["ffn_w_down"][layer] = _mm_auto(sv["act"], dxb, "tn", bf16, "mm_down_dw")
        dgc, dval = _ffn_act_bwd(sv["gc"], sv["up"], dact, "ffn_act_bwd", tc=D_FF // 2)
        dgate, dcw, dcb = _conv_bwd(dgc, sv["up"], 0, wt["ffn_conv_w"][layer], "ffn_conv_bwd", tc=D_FF // 2)
        g["ffn_conv_w"][layer] = dcw[:3]
        g["ffn_conv_b"][layer] = dcb
        dup = jnp.concatenate([dgate, dval], axis=1)
        g["ffn_w_up"][layer] = _mm_auto(sv["h2"], dup, "tn", bf16, "mm_up_dw")
        deps = post_grads(layer, "ffn", g) if post_grads else ()
        dh2 = _mm_auto(dup, wt["ffn_w_up"][layer], "nt", f32, "mm_up_dx")
        dx, dxb, g["norm_ffn"][layer] = _rmsnorm_bwd(sv["x_mid"], wt["norm_ffn"][layer], dh2, dx, "rms_bwd", deps)
        if layer % 2 == 0:
            do = _mm_auto(dxb, wt["ab_w_out"][j], "nt", f32, "mm_ab_out_dx")
            g["ab_w_out"][j] = _mm_auto(sv["o"], dxb, "tn", bf16, "mm_ab_out_dw")
            dconv, dz, dba, dal, ddt, dgn = _gdn2_bwd(
                sv["conv"], sv["p"], wt["alog_v"][j], wt["dtb_v"][j], wt["gdn_norm"][j], sv["s_a"], do, "gdn_bwd")
            g["alog_v"][j], g["dtb_v"][j], g["gdn_norm"][j] = dal, ddt, dgn
            dqkv, dcw, _ = _conv_bwd(dconv, sv["p"], 0, wt["gdn_conv_w"][j], "gdn_conv_bwd", tc=768)
            g["gdn_conv_w"][j] = dcw[:4]
            dqb, dfb, dib, dgb, dlb, dhn = _hgrn_bwd(sv["p"], lbs[j], wt["hgrn_norm"][j], sv["s_b"], do, "hgrn_bwd")
            g["hgrn_norm"][j] = dhn
            d_lbs[j] = jnp.sum(dlb, axis=0)
            dp = jnp.concatenate([dqkv] + [t_.astype(bf16) for t_ in (dz, dqb, dfb, dib, dgb, dba)], axis=1)
            g["ab_w_in"][j] = _mm_auto(sv["h1"], dp, "tn", bf16, "mm_ab_in_dw")
            dh1 = _mm_auto(dp, wt["ab_w_in"][j], "nt", f32, "mm_ab_in_dx")
        else:
            dhg = _mm_auto(dxb, wt["c_w_out"][j], "nt", f32, "mm_c_out_dx")
            g["c_w_out"][j] = _mm_auto(sv["hg"], dxb, "tn", bf16, "mm_c_out_dw")
            du, da, dy = _lru_scan_bwd(sv["a"], sv["h"], sv["p2"], dhg, "lru_scan_bwd")
            dxc, dwa, dwx, dba_, dbx_, dlam = _lru_gates_bwd(
                sv["xc"], wt["c_gate_a_w"][j], wt["c_gate_x_w"][j], wt["c_gate_a_b"][j], wt["c_gate_x_b"][j],
                wt["c_lambda"][j], da, du, "lru_gates_bwd")
            g["c_gate_a_w"][j], g["c_gate_x_w"][j] = dwa, dwx
            g["c_gate_a_b"][j], g["c_gate_x_b"][j], g["c_lambda"][j] = dba_, dbx_, dlam
            dxbr, dcw, dcb = _conv_bwd(dxc, sv["p2"], 1, wt["c_conv_w"][j], "lru_conv_bwd", tc=D)
            g["c_conv_w"][j] = dcw[:4]
            g["c_conv_b"][j] = dcb
            dp2 = jnp.concatenate([dy.astype(bf16), dxbr], axis=1)
            g["c_w_in"][j] = _mm_auto(sv["h1"], dp2, "tn", bf16, "mm_c_in_dw")
            dh1 = _mm_auto(dp2, wt["c_w_in"][j], "nt", f32, "mm_c_in_dx")
        deps = post_grads(layer, "mix", g) if post_grads else ()
        dx, dxb, g["norm_mix"][layer] = _rmsnorm_bwd(sv["x_in"], wt["norm_mix"][layer], dh1, dx, "rms_bwd", deps)
    g0, g1 = _lower_bounds_bwd(wt["hgrn_lower_bounds"], d_lbs[0], d_lbs[1])
    g["hgrn_lower_bounds"] = jnp.concatenate([g0, g1], axis=0)
    return loss, dx, g


def _ab_in_to_compute(w):
    return jnp.concatenate([w[:, :2048], w[:, 2056:4104], w[:, 2048:2056], jnp.zeros((D, 120), w.dtype)], axis=1)


def _ab_in_from_compute(g):
    return jnp.concatenate([g[:, :2048], g[:, 4096:4104], g[:, 2048:4096]], axis=1)


def _lane_vec(v4):
    return jnp.zeros((1, HEAD), f32).at[0, NH:2 * NH].set(v4)


def _layout_weights(fw):
    wt = {}
    wt["norm_mix"] = [fw["norm_mix"][l][None] for l in range(4)]
    wt["norm_ffn"] = [fw["norm_ffn"][l][None] for l in range(4)]
    wt["norm_final"] = fw["norm_final"][None]
    wt["gdn_conv_w"] = [fw["gdn_conv_w"][j] for j in range(2)]
    wt["alog_v"] = [_lane_vec(fw["gdn_a_log"][j]) for j in range(2)]
    wt["dtb_v"] = [_lane_vec(fw["gdn_dt_bias"][j]) for j in range(2)]
    wt["gdn_norm"] = [fw["gdn_norm"][j][None] for j in range(2)]
    wt["hgrn_lower_bounds"] = fw["hgrn_lower_bounds"]
    wt["hgrn_norm"] = [fw["hgrn_norm"][j][None] for j in range(2)]
    wt["c_conv_w"] = [fw["c_conv_w"][j] for j in range(2)]
    for k in ("c_conv_b", "c_gate_a_b", "c_gate_x_b", "c_lambda"):
        wt[k] = [fw[k][j][None] for j in range(2)]
    wt["ffn_conv_w"] = [fw["ffn_conv_w"][l] for l in range(4)]
    wt["ffn_conv_b"] = [fw["ffn_conv_b"][l][None] for l in range(4)]
    if "ab_w_in" in fw:
        wt["ab_w_in"] = [_ab_in_to_compute(fw["ab_w_in"][j].astype(bf16)) for j in range(2)]
        for k in ("ab_w_out", "c_w_in", "c_w_out"):
            wt[k] = [fw[k][j].astype(bf16) for j in range(2)]
        for k in ("c_gate_a_w", "c_gate_x_w"):
            wt[k] = [fw[k][j].astype(f32) for j in range(2)]
        for k in ("ffn_w_up", "ffn_w_down"):
            wt[k] = [fw[k][l].astype(bf16) for l in range(4)]
    return wt


def _layer_full(name, slots):
    kind = BIG[name]
    if kind == "col":
        full = slots.transpose(1, 0, 2).reshape(slots.shape[1], -1)
        return _ab_in_to_compute(full) if name == "ab_w_in" else full
    if kind == "row":
        return slots.reshape(-1, slots.shape[2])
    return slots.reshape(4, NH, LRU_BLOCK // 4, LRU_BLOCK).transpose(1, 0, 2, 3).reshape(NH, LRU_BLOCK, LRU_BLOCK).astype(f32)


def _layer_slots(name, g):
    kind = BIG[name]
    if kind == "col":
        if name == "ab_w_in":
            g = _ab_in_from_compute(g)
        r, cdim = g.shape
        return g.reshape(r, 4, cdim // 4).transpose(1, 0, 2).astype(bf16)
    if kind == "row":
        r, cdim = g.shape
        return g.reshape(4, r // 4, cdim).astype(bf16)
    return g.reshape(NH, 4, LRU_BLOCK // 4, LRU_BLOCK).transpose(1, 0, 2, 3).reshape(4, LRU_BLOCK, LRU_BLOCK).astype(bf16)


def _unlayout_grads(g):
    out = {}
    for k in ("norm_mix", "norm_ffn", "gdn_norm", "hgrn_norm", "c_conv_b", "c_gate_a_b", "c_gate_x_b", "c_lambda",
              "ffn_conv_b"):
        out[k] = jnp.concatenate(g[k], axis=0)
    out["norm_final"] = g["norm_final"][0]
    out["ab_w_in"] = jnp.stack([_ab_in_from_compute(t) for t in g["ab_w_in"]])
    out["gdn_a_log"] = jnp.stack([t[0, NH:2 * NH] for t in g["alog_v"]])
    out["gdn_dt_bias"] = jnp.stack([t[0, NH:2 * NH] for t in g["dtb_v"]])
    out["hgrn_lower_bounds"] = g["hgrn_lower_bounds"]
    for k in ("gdn_conv_w", "ab_w_out", "c_w_in", "c_conv_w", "c_gate_a_w", "c_gate_x_w", "c_w_out", "ffn_w_up",
              "ffn_conv_w", "ffn_w_down"):
        out[k] = jnp.stack(g[k])
    return out


MESH = pl.DeviceIdType.MESH
ANY = pl.BlockSpec(memory_space=pl.ANY)
CHIP_RELATIONS = ((1, 0), (0, 1), (1, 1))
N_CHIPS = 4


def _coords():
    return lax.axis_index("x"), lax.axis_index("y"), lax.axis_index("c")


def _flip(v, f):
    return 1 - v if f else v


def _half_rows(c, a, align):
    return pl.ds(pl.multiple_of(c * (a // 2), align), a // 2)


def _all_gather_chips(shards, name):
    n = len(shards)
    shapes = [s.shape for s in shards]

    def body(*refs):
        ins, outs = refs[:n], refs[n:2 * n]
        send_sems, recv_sems = refs[2 * n:]
        x, y, c = _coords()
        me = 2 * x + y
        sibling = (x, y, 1 - c)
        started = []
        for p in range(n):
            cp = pltpu.make_async_remote_copy(
                src_ref=ins[p], dst_ref=outs[p].at[me],
                send_sem=send_sems.at[p, 6], recv_sem=recv_sems.at[p, 6],
                device_id=sibling, device_id_type=MESH)
            cp.start()
            started.append(cp)
        for p in range(n):
            mine = _half_rows(c, shapes[p][0], 16)
            for r, (fx, fy) in enumerate(CHIP_RELATIONS):
                cp = pltpu.make_async_remote_copy(
                    src_ref=ins[p].at[mine], dst_ref=outs[p].at[me, mine],
                    send_sem=send_sems.at[p, r], recv_sem=recv_sems.at[p, r],
                    device_id=(_flip(x, fx), _flip(y, fy), c), device_id_type=MESH)
                cp.start()
                started.append(cp)
        for r, (fx, fy) in enumerate(CHIP_RELATIONS):
            k = 2 * _flip(x, fx) + _flip(y, fy)
            for p in range(n):
                mine = _half_rows(c, shapes[p][0], 16)
                pltpu.make_async_remote_copy(
                    src_ref=ins[p].at[mine], dst_ref=outs[p].at[k, mine],
                    send_sem=send_sems.at[p, r], recv_sem=recv_sems.at[p, r],
                    device_id=(_flip(x, fx), _flip(y, fy), c), device_id_type=MESH).wait_recv()
                fwd = pltpu.make_async_remote_copy(
                    src_ref=outs[p].at[k, mine], dst_ref=outs[p].at[k, mine],
                    send_sem=send_sems.at[p, 3 + r], recv_sem=recv_sems.at[p, 3 + r],
                    device_id=sibling, device_id_type=MESH)
                fwd.start()
                started.append(fwd)
        for r, (fx, fy) in enumerate(CHIP_RELATIONS):
            k = 2 * _flip(x, fx) + _flip(y, fy)
            for p in range(n):
                theirs = _half_rows(1 - c, shapes[p][0], 16)
                pltpu.make_async_remote_copy(
                    src_ref=outs[p].at[k, theirs], dst_ref=outs[p].at[k, theirs],
                    send_sem=send_sems.at[p, 3 + r], recv_sem=recv_sems.at[p, 3 + r],
                    device_id=sibling, device_id_type=MESH).wait_recv()
        for p in range(n):
            pltpu.make_async_remote_copy(
                src_ref=ins[p], dst_ref=outs[p].at[me],
                send_sem=send_sems.at[p, 6], recv_sem=recv_sems.at[p, 6],
                device_id=sibling, device_id_type=MESH).wait_recv()
        for cp in started:
            cp.wait_send()

    return pl.pallas_call(
        body, in_specs=[ANY] * n, out_specs=[ANY] * n,
        out_shape=[_sds((N_CHIPS,) + s.shape, s.dtype) for s in shards],
        scratch_shapes=[pltpu.SemaphoreType.DMA((n, 7)), pltpu.SemaphoreType.DMA((n, 7))],
        name=name)(*shards)


def _sibling_send_other_half(gs, name):
    n = len(gs)
    shapes = [g.shape for g in gs]

    def body(*refs):
        ins, outs = refs[:n], refs[n:2 * n]
        send_sems, recv_sems = refs[2 * n:]
        x, y, c = _coords()
        cps = []
        for p in range(n):
            theirs = _half_rows(1 - c, shapes[p][1], 8)
            cp = pltpu.make_async_remote_copy(
                src_ref=ins[p].at[:, theirs], dst_ref=outs[p],
                send_sem=send_sems.at[p], recv_sem=recv_sems.at[p],
                device_id=(x, y, 1 - c), device_id_type=MESH)
            cp.start()
            cps.append(cp)
        for cp in cps:
            cp.wait()

    return pl.pallas_call(
        body, in_specs=[ANY] * n, out_specs=[ANY] * n,
        out_shape=[_sds((s[0], s[1] // 2, s[2]), f32) for s in shapes],
        scratch_shapes=[pltpu.SemaphoreType.DMA((n,)), pltpu.SemaphoreType.DMA((n,))],
        name=name)(*gs)


def _chip_exchange(ps, name):
    n = len(ps)

    def body(*refs):
        ins, outs = refs[:n], refs[n:2 * n]
        send_sems, recv_sems = refs[2 * n:]
        x, y, c = _coords()
        cps = []
        for p in range(n):
            for r, (fx, fy) in enumerate(CHIP_RELATIONS):
                k = 2 * _flip(x, fx) + _flip(y, fy)
                cp = pltpu.make_async_remote_copy(
                    src_ref=ins[p].at[k], dst_ref=outs[p].at[r],
                    send_sem=send_sems.at[p, r], recv_sem=recv_sems.at[p, r],
                    device_id=(_flip(x, fx), _flip(y, fy), c), device_id_type=MESH)
                cp.start()
                cps.append(cp)
        for cp in cps:
            cp.wait_recv()
        for cp in cps:
            cp.wait_send()

    return pl.pallas_call(
        body, in_specs=[ANY] * n, out_specs=[ANY] * n,
        out_shape=[_sds((3,) + p.shape[1:], p.dtype) for p in ps],
        scratch_shapes=[pltpu.SemaphoreType.DMA((n, 3)), pltpu.SemaphoreType.DMA((n, 3))],
        name=name)(*ps)


def _sibling_fill_other_half(fs, name):
    n = len(fs)
    shapes = [f.shape for f in fs]

    def body(*refs):
        ins, outs = refs[:n], refs[n:2 * n]
        send_sems, recv_sems = refs[2 * n:]
        x, y, c = _coords()
        cps = []
        for p in range(n):
            mine = _half_rows(c, shapes[p][0], 8)
            cp = pltpu.make_async_remote_copy(
                src_ref=ins[p].at[mine], dst_ref=outs[p].at[mine],
                send_sem=send_sems.at[p], recv_sem=recv_sems.at[p],
                device_id=(x, y, 1 - c), device_id_type=MESH)
            cp.start()
            cps.append(cp)
        for p in range(n):
            theirs = _half_rows(1 - c, shapes[p][0], 8)
            pltpu.make_async_remote_copy(
                src_ref=ins[p].at[theirs], dst_ref=outs[p].at[theirs],
                send_sem=send_sems.at[p], recv_sem=recv_sems.at[p],
                device_id=(x, y, 1 - c), device_id_type=MESH).wait_recv()
        for cp in cps:
            cp.wait_send()

    return pl.pallas_call(
        body, in_specs=[ANY] * n, out_specs=[ANY] * n,
        out_shape=[_sds(f.shape, f.dtype) for f in fs],
        input_output_aliases={p: p for p in range(n)},
        scratch_shapes=[pltpu.SemaphoreType.DMA((n,)), pltpu.SemaphoreType.DMA((n,))],
        name=name)(*fs)


def _sibling_all_gather(ss, name):
    n = len(ss)
    shapes = [s.shape for s in ss]

    def body(*refs):
        ins, outs = refs[:n], refs[n:2 * n]
        send_sems, recv_sems, loc_sems = refs[2 * n:]
        x, y, c = _coords()
        locs, cps = [], []
        for p in range(n):
            mine = _half_rows(c, 2 * shapes[p][0], 8)
            loc = pltpu.make_async_copy(ins[p], outs[p].at[mine], loc_sems.at[p])
            loc.start()
            locs.append(loc)
            cp = pltpu.make_async_remote_copy(
                src_ref=ins[p], dst_ref=outs[p].at[mine],
                send_sem=send_sems.at[p], recv_sem=recv_sems.at[p],
                device_id=(x, y, 1 - c), device_id_type=MESH)
            cp.start()
            cps.append(cp)
        for p, cp in enumerate(cps):
            theirs = _half_rows(1 - c, 2 * shapes[p][0], 8)
            pltpu.make_async_remote_copy(
                src_ref=ins[p], dst_ref=outs[p].at[theirs],
                send_sem=send_sems.at[p], recv_sem=recv_sems.at[p],
                device_id=(x, y, 1 - c), device_id_type=MESH).wait_recv()
        for cp in cps:
            cp.wait_send()
        for loc in locs:
            loc.wait()

    return pl.pallas_call(
        body, in_specs=[ANY] * n, out_specs=[ANY] * n,
        out_shape=[_sds((2 * s[0], s[1]), f32) for s in shapes],
        scratch_shapes=[pltpu.SemaphoreType.DMA((n,)), pltpu.SemaphoreType.DMA((n,)), pltpu.SemaphoreType.DMA((n,))],
        name=name)(*ss)


N_DEV = 8


def _all_reduce_small(pack, name):
    rows = pack.shape[0]

    def body(in_ref, sum_ref, all_ref, send_sems, recv_sems):
        x, y, c = _coords()
        me = 4 * x + 2 * y + c
        all_ref[me] = in_ref[...]
        cps = []
        for r in range(1, N_DEV):
            fx, fy, fc = (r >> 2) & 1, (r >> 1) & 1, r & 1
            cp = pltpu.make_async_remote_copy(
                src_ref=in_ref, dst_ref=all_ref.at[me],
                send_sem=send_sems.at[r], recv_sem=recv_sems.at[r],
                device_id=(_flip(x, fx), _flip(y, fy), _flip(c, fc)), device_id_type=MESH)
            cp.start()
            cps.append(cp)
        for r in range(1, N_DEV):
            fx, fy, fc = (r >> 2) & 1, (r >> 1) & 1, r & 1
            peer = 4 * _flip(x, fx) + 2 * _flip(y, fy) + _flip(c, fc)
            pltpu.make_async_remote_copy(
                src_ref=in_ref, dst_ref=all_ref.at[peer],
                send_sem=send_sems.at[r], recv_sem=recv_sems.at[r],
                device_id=(x, y, c), device_id_type=MESH).wait_recv()
        for cp in cps:
            cp.wait_send()
        acc = all_ref[0]
        for d in range(1, N_DEV):
            acc = acc + all_ref[d]
        sum_ref[...] = acc

    vm = pl.BlockSpec(memory_space=pltpu.VMEM)
    return pl.pallas_call(
        body, in_specs=[vm], out_specs=[vm, vm],
        out_shape=[_sds((rows, 128), f32), _sds((N_DEV, rows, 128), f32)],
        scratch_shapes=[pltpu.SemaphoreType.DMA((N_DEV,)), pltpu.SemaphoreType.DMA((N_DEV,))],
        name=name)(pack)[0]


ROWS_EW = 128


def _add_own_half(g, rs, c_arr, name):
    s, a, b = g.shape
    nrt = (a // 2) // ROWS_EW

    def body(c_ref, g_ref, r_ref, o_ref):
        o_ref[...] = (g_ref[...] + r_ref[...]).astype(bf16)

    grid_spec = pltpu.PrefetchScalarGridSpec(
        num_scalar_prefetch=1, grid=(s, nrt),
        in_specs=[pl.BlockSpec((1, ROWS_EW, b), lambda k, i, c_ref: (k, c_ref[0] * nrt + i, 0)),
                  pl.BlockSpec((1, ROWS_EW, b), lambda k, i, c_ref: (k, i, 0))],
        out_specs=pl.BlockSpec((1, ROWS_EW, b), lambda k, i, c_ref: (k, i, 0)))
    return pl.pallas_call(body, grid_spec=grid_spec, out_shape=_sds((s, a // 2, b), bf16),
                          compiler_params=_cp(), name=name)(c_arr, g, rs)


def _sum_chips(g, rs, rc, ids, name):
    _, r, b = rc.shape
    nrt = r // ROWS_EW

    def body(ids_ref, g_ref, s_ref, r_ref, o_ref):
        own = g_ref[0] + s_ref[0]
        o_ref[...] = ((own + r_ref[0].astype(f32)) + r_ref[1].astype(f32)) + r_ref[2].astype(f32)

    grid_spec = pltpu.PrefetchScalarGridSpec(
        num_scalar_prefetch=1, grid=(nrt,),
        in_specs=[pl.BlockSpec((1, ROWS_EW, b), lambda i, ids_ref: (ids_ref[0], ids_ref[1] * nrt + i, 0)),
                  pl.BlockSpec((1, ROWS_EW, b), lambda i, ids_ref: (ids_ref[0], i, 0)),
                  pl.BlockSpec((3, ROWS_EW, b), lambda i, ids_ref: (0, i, 0))],
        out_specs=pl.BlockSpec((ROWS_EW, b), lambda i, ids_ref: (ids_ref[1] * nrt + i, 0)))
    return pl.pallas_call(body, grid_spec=grid_spec, out_shape=_sds((2 * r, b), f32),
                          compiler_params=_cp(), name=name)(ids, g, rs, rc)


def _adamw_math(w, g, m, v):
    m = ADAM_B1 * m + (1.0 - ADAM_B1) * g
    v = ADAM_B2 * v + (1.0 - ADAM_B2) * (g * g)
    m_hat = m / (1.0 - ADAM_B1 ** ADAM_STEP)
    v_hat = v / (1.0 - ADAM_B2 ** ADAM_STEP)
    delta = -ADAM_LR * (m_hat / (jnp.sqrt(v_hat) + ADAM_EPS) + ADAM_WD * w)
    return delta, m, v


def _adamw_big(w, g, m, v, name):
    a, b = w.shape

    def body(w_ref, g_ref, m_ref, v_ref, go_ref, d_ref, mo_ref, vo_ref):
        gv = g_ref[...]
        d, mn, vn = _adamw_math(w_ref[...], gv, m_ref[...], v_ref[...])
        go_ref[...] = gv
        d_ref[...] = d
        mo_ref[...] = mn
        vo_ref[...] = vn

    blk = pl.BlockSpec((ROWS_EW, b), lambda i: (i, 0))
    return pl.pallas_call(body, grid=(a // ROWS_EW,), in_specs=[blk] * 4, out_specs=[blk] * 4,
                          out_shape=[_sds((a, b), f32)] * 4, compiler_params=_cp(), name=name)(w, g, m, v)


def _adamw_small(ws, gs, ms, vs, name):
    n = len(ws)

    def body(*refs):
        w_r, g_r, m_r, v_r = refs[:n], refs[n:2 * n], refs[2 * n:3 * n], refs[3 * n:4 * n]
        go_r, d_r, mo_r, vo_r = refs[4 * n:5 * n], refs[5 * n:6 * n], refs[6 * n:7 * n], refs[7 * n:8 * n]
        for p in range(n):
            gv = g_r[p][...]
            d, mn, vn = _adamw_math(w_r[p][...], gv, m_r[p][...], v_r[p][...])
            go_r[p][...] = gv
            d_r[p][...] = d
            mo_r[p][...] = mn
            vo_r[p][...] = vn

    vm = pl.BlockSpec(memory_space=pltpu.VMEM)
    shp = [_sds(w.shape, f32) for w in ws]
    res = pl.pallas_call(body, in_specs=[vm] * (4 * n), out_specs=[vm] * (4 * n), out_shape=shp * 4,
                         name=name)(*ws, *gs, *ms, *vs)
    return res[:n], res[n:2 * n], res[2 * n:3 * n], res[3 * n:]


HBM = pl.BlockSpec(memory_space=pltpu.HBM)
SEM = pl.BlockSpec(memory_space=pltpu.SEMAPHORE)
EFFECT = pltpu.SideEffectType.DATAFLOW_SIDE_EFFECTING
N_REL = 8


def _rel(r):
    return (r >> 2) & 1, (r >> 1) & 1, r & 1


def _gather_copies(ins, lands, send_sems, recv_sems, shapes):
    x, y, c = _coords()
    me = 2 * x + y
    sends, recvs = [], []
    for p in range(len(ins)):
        for r in range(1, N_REL):
            fx, fy, fc = _rel(r)
            peer = (_flip(x, fx), _flip(y, fy), _flip(c, fc))
            if fx == 0 and fy == 0:
                src, dst, got = ins[p], lands[p].at[me], lands[p].at[me]
            else:
                mine = _half_rows(c, shapes[p][0], 16)
                theirs = _half_rows(_flip(c, fc), shapes[p][0], 16)
                src, dst = ins[p].at[mine], lands[p].at[me, mine]
                got = lands[p].at[2 * peer[0] + peer[1], theirs]
            sems = dict(send_sem=send_sems.at[p * N_REL + r], recv_sem=recv_sems.at[p * N_REL + r], device_id=peer,
                        device_id_type=MESH)
            sends.append(pltpu.make_async_remote_copy(src_ref=src, dst_ref=dst, **sems))
            recvs.append(pltpu.make_async_remote_copy(src_ref=src, dst_ref=got, **sems))
    return sends, recvs


def _scatter_copies(ins, lands, send_sems, recv_sems, shapes):
    x, y, c = _coords()
    sends, recvs = [], []
    for p in range(len(ins)):
        for r in range(1, N_REL):
            fx, fy, fc = _rel(r)
            peer = (_flip(x, fx), _flip(y, fy), _flip(c, fc))
            theirs = _half_rows(peer[2], shapes[p][1], 16)
            sems = dict(send_sem=send_sems.at[p * N_REL + r], recv_sem=recv_sems.at[p * N_REL + r], device_id=peer,
                        device_id_type=MESH)
            cp = pltpu.make_async_remote_copy(src_ref=ins[p].at[2 * peer[0] + peer[1], theirs], dst_ref=lands[p].at[r], **sems)
            sends.append(cp)
            recvs.append(cp)
    return sends, recvs


def _split_start(copies_fn, ins, land_shapes, name):
    n = len(ins)
    shapes = [a.shape for a in ins]

    def body(*refs):
        in_refs, land_refs = refs[:n], refs[n:2 * n]
        send_sems, recv_sems = refs[2 * n], refs[2 * n + 1]
        token = refs[-1]
        sends, _ = copies_fn(in_refs, land_refs, send_sems, recv_sems, shapes)
        for cp in sends:
            cp.start()
        token[...] = jnp.zeros_like(token)

    lands = [lax.empty(s.shape, s.dtype) for s in land_shapes]
    res = pl.pallas_call(
        body, name=name,
        out_shape=(pltpu.SemaphoreType.DMA((n * N_REL,)), pltpu.SemaphoreType.DMA((n * N_REL,)))
        + tuple(pltpu.HBM(a.shape, a.dtype) for a in ins) + tuple(pltpu.HBM(s.shape, s.dtype) for s in land_shapes)
        + (_sds((8, 128), f32),),
        in_specs=[HBM] * (2 * n), out_specs=(SEM, SEM) + (HBM,) * (2 * n) + (pl.BlockSpec(memory_space=pltpu.VMEM),),
        input_output_aliases={i: 2 + i for i in range(2 * n)},
        compiler_params=pltpu.CompilerParams(has_side_effects=EFFECT),
    )(*[pltpu.with_memory_space_constraint(a, pltpu.HBM) for a in ins],
      *[pltpu.with_memory_space_constraint(a, pltpu.HBM) for a in lands])
    return dict(sems=res[:2], ins=res[2:2 + n], lands=res[2 + n:2 + 2 * n], token=res[-1], shapes=shapes)


def _split_wait(copies_fn, started, after, name):
    n = len(started["ins"])
    shapes = started["shapes"]
    na = len(after)

    def body(*refs):
        in_refs, land_refs = refs[:n], refs[n:2 * n]
        send_sems, recv_sems = refs[2 * n], refs[2 * n + 1]
        sends, recvs = copies_fn(in_refs, land_refs, send_sems, recv_sems, shapes)
        for cp in sends:
            cp.wait_send()
        for cp in recvs:
            cp.wait_recv()

    arrs = list(started["ins"]) + list(started["lands"])
    res = pl.pallas_call(
        body, name=name,
        out_shape=tuple(pltpu.HBM(a.shape, a.dtype) for a in arrs),
        in_specs=[HBM] * (2 * n) + [SEM, SEM] + [pl.BlockSpec(memory_space=pl.ANY)] * na,
        out_specs=(HBM,) * (2 * n), input_output_aliases={i: i for i in range(2 * n)},
        compiler_params=pltpu.CompilerParams(has_side_effects=EFFECT),
    )(*arrs, *started["sems"], *after)
    return res[:n], res[n:]


def _sum_pieces(gb, land, ids, f_prev, blk, nblk, name):
    _, a, b = gb.shape
    rows = _tile(a // 2, (ROWS_EW, 176, 64, 32, 16))
    nrt = (a // 2) // rows

    def body(ids_ref, g_ref, l_ref, *rest):
        o_ref = rest[-1]
        acc = g_ref[0].astype(f32)
        for r in range(1, N_REL):
            acc = acc + l_ref[r].astype(f32)
        o_ref[...] = acc

    in_specs = [pl.BlockSpec((1, rows, b), lambda i, ids_ref: (ids_ref[0], ids_ref[1] * nrt + i, 0)),
                pl.BlockSpec((N_REL, rows, b), lambda i, ids_ref: (0, i, 0))]
    args = [ids, gb, land]
    aliases = {}
    if f_prev is not None:
        in_specs.append(pl.BlockSpec(memory_space=pl.ANY))
        args.append(f_prev)
        aliases = {3: 0}
    grid_spec = pltpu.PrefetchScalarGridSpec(
        num_scalar_prefetch=1, grid=(nrt,), in_specs=in_specs,
        out_specs=pl.BlockSpec((rows, b), lambda i, ids_ref: ((2 * blk + ids_ref[1]) * nrt + i, 0)))
    return pl.pallas_call(body, grid_spec=grid_spec, out_shape=_sds((nblk * a, b), f32),
                          input_output_aliases=aliases, compiler_params=_cp(), name=name)(*args)


def _sibling_fill_blocks(fs, nblks, name):
    n = len(fs)
    shapes = [f.shape for f in fs]

    def body(*refs):
        ins, outs = refs[:n], refs[n:2 * n]
        send_sems, recv_sems = refs[2 * n:]
        x, y, c = _coords()
        cps, waits = [], []
        k = 0
        for p in range(n):
            a = shapes[p][0] // nblks[p]
            for bi in range(nblks[p]):
                mine = pl.ds(pl.multiple_of(bi * a + c * (a // 2), 8), a // 2)
                theirs = pl.ds(pl.multiple_of(bi * a + (1 - c) * (a // 2), 8), a // 2)
                sems = dict(send_sem=send_sems.at[k], recv_sem=recv_sems.at[k], device_id=(x, y, 1 - c), device_id_type=MESH)
                cp = pltpu.make_async_remote_copy(src_ref=ins[p].at[mine], dst_ref=outs[p].at[mine], **sems)
                cp.start()
                cps.append(cp)
                waits.append(pltpu.make_async_remote_copy(src_ref=ins[p].at[theirs], dst_ref=outs[p].at[theirs], **sems))
                k += 1
        for wt_ in waits:
            wt_.wait_recv()
        for cp in cps:
            cp.wait_send()

    total = sum(nblks)
    return pl.pallas_call(
        body, in_specs=[ANY] * n, out_specs=[ANY] * n,
        out_shape=[_sds(f.shape, f.dtype) for f in fs],
        input_output_aliases={p: p for p in range(n)},
        scratch_shapes=[pltpu.SemaphoreType.DMA((total,)), pltpu.SemaphoreType.DMA((total,))],
        name=name)(*fs)


WEIGHTS = ["norm_mix", "norm_ffn", "norm_final", "ab_w_in", "gdn_conv_w", "gdn_a_log", "gdn_dt_bias", "gdn_norm",
           "hgrn_lower_bounds", "hgrn_norm", "ab_w_out", "c_w_in", "c_conv_w", "c_conv_b", "c_gate_a_w", "c_gate_a_b",
           "c_gate_x_w", "c_gate_x_b", "c_lambda", "c_w_out", "ffn_w_up", "ffn_conv_w", "ffn_conv_b", "ffn_w_down"]
BIG = {"ab_w_in": "col", "ab_w_out": "row", "c_w_in": "col", "c_gate_a_w": "gate", "c_gate_x_w": "gate",
       "c_w_out": "row", "ffn_w_up": "col", "ffn_w_down": "row"}
SMALL_SHARDED = ["gdn_conv_w", "c_conv_w", "c_conv_b", "c_gate_a_b", "c_gate_x_b", "c_lambda", "ffn_conv_w"]
SMALL = [n for n in WEIGHTS if n not in BIG]
FULL_SHAPES = {
    "norm_mix": (4, 1024), "norm_ffn": (4, 1024), "norm_final": (1024,), "ab_w_in": (2, 1024, 4104),
    "gdn_conv_w": (2, 4, 1536), "gdn_a_log": (2, 4), "gdn_dt_bias": (2, 4), "gdn_norm": (2, 128),
    "hgrn_lower_bounds": (2, 512), "hgrn_norm": (2, 128), "ab_w_out": (2, 1024, 1024), "c_w_in": (2, 1024, 2048),
    "c_conv_w": (2, 4, 1024), "c_conv_b": (2, 1024), "c_gate_a_w": (2, 4, 256, 256), "c_gate_a_b": (2, 1024),
    "c_gate_x_w": (2, 4, 256, 256), "c_gate_x_b": (2, 1024), "c_lambda": (2, 1024), "c_w_out": (2, 1024, 1024),
    "ffn_w_up": (4, 1024, 5632), "ffn_conv_w": (4, 3, 2816), "ffn_conv_b": (4, 2816), "ffn_w_down": (4, 2816, 1024)}


def _shard2d(name, shard):
    return shard.reshape(-1, shard.shape[-1])


def _full_from_slots(name, slots):
    full = FULL_SHAPES[name]
    kind = BIG[name]
    if kind == "col":
        l, r, cdim = full
        return slots.reshape(4, l, r, cdim // 4).transpose(1, 2, 0, 3).reshape(full)
    if kind == "row":
        l, r, cdim = full
        return slots.reshape(4, l, r // 4, cdim).transpose(1, 0, 2, 3).reshape(full)
    l, h, r, cdim = full
    return slots.reshape(4, l, h, r // 4, cdim).transpose(1, 2, 0, 3, 4).reshape(full)


def _slots_from_full(name, g):
    full = FULL_SHAPES[name]
    kind = BIG[name]
    if kind == "col":
        l, r, cdim = full
        return g.reshape(l, r, 4, cdim // 4).transpose(2, 0, 1, 3).reshape(4, l * r, cdim // 4)
    if kind == "row":
        l, r, cdim = full
        return g.reshape(l, 4, r // 4, cdim).transpose(1, 0, 2, 3).reshape(4, l * (r // 4), cdim)
    l, h, r, cdim = full
    return g.reshape(l, h, 4, r // 4, cdim).transpose(2, 0, 1, 3, 4).reshape(4, l * h * (r // 4), cdim)


def _pack_rows(arrs, rows):
    flat = jnp.concatenate([a.reshape(-1) for a in arrs])
    return jnp.pad(flat, (0, rows * 128 - flat.shape[0])).reshape(rows, 128)


def _unpack_rows(pack, shapes):
    flat = pack.reshape(-1)
    out, off = [], 0
    for s in shapes:
        size = 1
        for d in s:
            size *= d
        out.append(flat[off:off + size].reshape(s))
        off += size
    return out


def kernel(x, norm_mix, norm_ffn, norm_final, ab_w_in, gdn_conv_w, gdn_a_log, gdn_dt_bias, gdn_norm, hgrn_lower_bounds, hgrn_norm, ab_w_out, c_w_in, c_conv_w, c_conv_b, c_gate_a_w, c_gate_a_b, c_gate_x_w, c_gate_x_b, c_lambda, c_w_out, ffn_w_up, ffn_conv_w, ffn_conv_b, ffn_w_down, loss_target, m_norm_mix, m_norm_ffn, m_norm_final, m_ab_w_in, m_gdn_conv_w, m_gdn_a_log, m_gdn_dt_bias, m_gdn_norm, m_hgrn_lower_bounds, m_hgrn_norm, m_ab_w_out, m_c_w_in, m_c_conv_w, m_c_conv_b, m_c_gate_a_w, m_c_gate_a_b, m_c_gate_x_w, m_c_gate_x_b, m_c_lambda, m_c_w_out, m_ffn_w_up, m_ffn_conv_w, m_ffn_conv_b, m_ffn_w_down, v_norm_mix, v_norm_ffn, v_norm_final, v_ab_w_in, v_gdn_conv_w, v_gdn_a_log, v_gdn_dt_bias, v_gdn_norm, v_hgrn_lower_bounds, v_hgrn_norm, v_ab_w_out, v_c_w_in, v_c_conv_w, v_c_conv_b, v_c_gate_a_w, v_c_gate_a_b, v_c_gate_x_w, v_c_gate_x_b, v_c_lambda, v_c_w_out, v_ffn_w_up, v_ffn_conv_w, v_ffn_conv_b, v_ffn_w_down):
    w = dict(zip(WEIGHTS, (norm_mix, norm_ffn, norm_final, ab_w_in, gdn_conv_w, gdn_a_log, gdn_dt_bias, gdn_norm, hgrn_lower_bounds, hgrn_norm, ab_w_out, c_w_in, c_conv_w, c_conv_b, c_gate_a_w, c_gate_a_b, c_gate_x_w, c_gate_x_b, c_lambda, c_w_out, ffn_w_up, ffn_conv_w, ffn_conv_b, ffn_w_down)))
    m = dict(zip(WEIGHTS, (m_norm_mix, m_norm_ffn, m_norm_final, m_ab_w_in, m_gdn_conv_w, m_gdn_a_log, m_gdn_dt_bias, m_gdn_norm, m_hgrn_lower_bounds, m_hgrn_norm, m_ab_w_out, m_c_w_in, m_c_conv_w, m_c_conv_b, m_c_gate_a_w, m_c_gate_a_b, m_c_gate_x_w, m_c_gate_x_b, m_c_lambda, m_c_w_out, m_ffn_w_up, m_ffn_conv_w, m_ffn_conv_b, m_ffn_w_down)))
    v = dict(zip(WEIGHTS, (v_norm_mix, v_norm_ffn, v_norm_final, v_ab_w_in, v_gdn_conv_w, v_gdn_a_log, v_gdn_dt_bias, v_gdn_norm, v_hgrn_lower_bounds, v_hgrn_norm, v_ab_w_out, v_c_w_in, v_c_conv_w, v_c_conv_b, v_c_gate_a_w, v_c_gate_a_b, v_c_gate_x_w, v_c_gate_x_b, v_c_lambda, v_c_w_out, v_ffn_w_up, v_ffn_conv_w, v_ffn_conv_b, v_ffn_w_down)))
    big = list(BIG)
    chip = 2 * lax.axis_index("x") + lax.axis_index("y")
    ids = jnp.stack([chip, lax.axis_index("c")]).astype(jnp.int32)

    def layer_parts(l):
        j = l // 2
        if l % 2 == 0:
            mix = [("ab_w_in", j), ("ab_w_out", j)]
        else:
            mix = [("c_w_in", j), ("c_gate_a_w", j), ("c_gate_x_w", j), ("c_w_out", j)]
        return mix, [("ffn_w_up", l), ("ffn_w_down", l)]

    def layer_shard(n, i):
        s = w[n][i]
        return s.reshape(-1, s.shape[-1]).astype(bf16)

    small_shard_shapes = [w[n].shape for n in SMALL_SHARDED]
    small_pack = _pack_rows([w[n] for n in SMALL_SHARDED], 128)
    mix0, ffn0 = layer_parts(0)
    gathered0 = _all_gather_chips([layer_shard(n, i) for n, i in mix0] + [small_pack], "all_gather_mixer0")
    gathers = {}
    for l in (0, 1, 2, 3):
        parts = ffn0 if l == 0 else sum(layer_parts(l), [])
        shards = [layer_shard(n, i) for n, i in parts]
        gathers[l] = _split_start(_gather_copies, shards, [_sds((N_CHIPS,) + s.shape, bf16) for s in shards],
                                  "gather_start_%d" % l)
    fw = {}
    per_chip = [_unpack_rows(gathered0[-1][k], small_shard_shapes) for k in range(N_CHIPS)]
    for i, n in enumerate(SMALL_SHARDED):
        fw[n] = jnp.concatenate([per_chip[k][i] for k in range(N_CHIPS)], axis=-1)
    for n in SMALL:
        if n not in fw:
            fw[n] = w[n]
    wt = _layout_weights(fw)
    for n in big:
        wt[n] = [None] * FULL_SHAPES[n][0]

    def pre_layer(l, part, x_l):
        deps = ()
        if l == 0 and part == "mix":
            parts, lands, deps = mix0, gathered0[:len(mix0)], tuple(gathers[k]["token"] for k in (0, 1, 2, 3))
        elif (l == 0) == (part == "ffn"):
            parts = ffn0 if l == 0 else sum(layer_parts(l), [])
            _, lands = _split_wait(_gather_copies, gathers[l], [x_l], "gather_wait_%d" % l)
        else:
            return deps
        for (n, i), slots in zip(parts, lands):
            wt[n][i] = _layer_full(n, slots)
        return deps

    scatters = []

    def post_grads(l, part, g):
        parts = layer_parts(l)[0 if part == "mix" else 1]
        slots = [_layer_slots(n, g[n][i]) for n, i in parts]
        st = _split_start(_scatter_copies, slots, [_sds((N_REL, s.shape[1] // 2, s.shape[2]), bf16) for s in slots],
                          "scatter_start_%d_%s" % (l, part))
        scatters.append((parts, st, "scatter_wait_%d_%s" % (l, part)))
        return (st["token"],)

    loss, dx, g = _local_step(x[0], loss_target[0], wt, pre_layer, post_grads)
    gf = _unlayout_grads(g)
    loss = lax.psum(loss[0, 0], ("x", "y", "c"))

    f = {n: None for n in big}
    for parts, st, wait_name in scatters:
        gbs, lands = _split_wait(_scatter_copies, st, [dx], wait_name)
        for (n, i), gb, land in zip(parts, gbs, lands):
            f[n] = _sum_pieces(gb, land, ids, f[n], i, FULL_SHAPES[n][0], "rs_sum")
    filled = _sibling_fill_blocks([f[n] for n in big], [FULL_SHAPES[n][0] for n in big], "rs_sibling_fill")
    g_shard = dict(zip(big, filled))

    small_full_shapes = [FULL_SHAPES[n] for n in SMALL]
    small_sum = _all_reduce_small(_pack_rows([gf[n] for n in SMALL], 664), "all_reduce_small")
    g_small = dict(zip(SMALL, _unpack_rows(small_sum, small_full_shapes)))
    for n in SMALL_SHARDED:
        width = w[n].shape[-1]
        g_small[n] = lax.dynamic_slice_in_dim(g_small[n], chip * width, width, axis=-1)

    out_g, out_d, out_m, out_v = {}, {}, {}, {}
    for n in big:
        shp = w[n].shape
        res = _adamw_big(_shard2d(n, w[n]), g_shard[n], _shard2d(n, m[n]), _shard2d(n, v[n]), "adamw_" + n)
        out_g[n], out_d[n], out_m[n], out_v[n] = (t.reshape(shp) for t in res)
    as2d = lambda t: t.reshape(-1, t.shape[-1])
    sg, sd, sm, sv = _adamw_small([as2d(w[n]) for n in SMALL], [as2d(g_small[n]) for n in SMALL],
                                  [as2d(m[n]) for n in SMALL], [as2d(v[n]) for n in SMALL], "adamw_small")
    for i, n in enumerate(SMALL):
        out_g[n], out_d[n], out_m[n], out_v[n] = (t[i].reshape(w[n].shape) for t in (sg, sd, sm, sv))
    return (loss, dx[None], *[out_g[n] for n in WEIGHTS], *[out_d[n] for n in WEIGHTS],
            *[out_m[n] for n in WEIGHTS], *[out_v[n] for n in WEIGHTS])
```

```python
import functools

import jax
import jax.numpy as jnp
from jax import lax
from jax.experimental import pallas as pl
from jax.experimental.pallas import tpu as pltpu

f32 = jnp.float32
bf16 = jnp.bfloat16
HI = lax.Precision.HIGHEST

D = 1024
EPS = 1e-6
F_FLOOR = 1e-30
GDN_CHUNK = 64
GDN_INTRA_CHUNKS = 1
HGRN_CHUNK = 16
HGRN_STEP = 128
HEAD = 128
NH = 4
LRU_BLOCK = 256
D_FF = 2816
RG_C = 8.0
C_QKV, C_Z, C_QB, C_FB, C_IB, C_GB, C_BA = 0, 1536, 2048, 2560, 3072, 3584, 4096
AB_COLS_PAD = 4224
TT = 256
N_SLOTS = 4
VMEM_LIMIT = 56 * 1024 * 1024

ADAM_LR, ADAM_B1, ADAM_B2, ADAM_EPS, ADAM_WD, ADAM_STEP = 0.001, 0.9, 0.999, 1e-08, 0.01, 10


def _cp(**kw):
    return pltpu.CompilerParams(vmem_limit_bytes=VMEM_LIMIT, **kw)


def _sds(shape, dtype):
    return jax.ShapeDtypeStruct(shape, dtype)


def _dot(a, b, dims, precision=None):
    return lax.dot_general(a, b, (dims, ((), ())), precision=precision, preferred_element_type=f32)


NN = ((1,), (0,))
NT = ((1,), (1,))
TN = ((0,), (0,))


def _rms(x, g):
    return x * lax.rsqrt(jnp.mean(x * x, axis=-1, keepdims=True) + EPS) * g


def _silu(x):
    return x * jax.nn.sigmoid(x)


def _mm(a, b, mode, tm, tn, out_dtype, name):
    if mode == "nn":
        (m, k), n = a.shape, b.shape[1]
        a_spec = pl.BlockSpec((tm, k), lambda i, j: (i, 0))
        b_spec = pl.BlockSpec((k, tn), lambda i, j: (0, j))
        dims = NN
    elif mode == "nt":
        (m, k), n = a.shape, b.shape[0]
        a_spec = pl.BlockSpec((tm, k), lambda i, j: (i, 0))
        b_spec = pl.BlockSpec((tn, k), lambda i, j: (j, 0))
        dims = NT
    else:
        (k, m), n = a.shape, b.shape[1]
        a_spec = pl.BlockSpec((k, tm), lambda i, j: (0, i))
        b_spec = pl.BlockSpec((k, tn), lambda i, j: (0, j))
        dims = TN
    assert m % tm == 0 and n % tn == 0, (name, m, n, tm, tn)

    def body(a_ref, b_ref, o_ref):
        o_ref[...] = _dot(a_ref[...], b_ref[...], dims).astype(out_dtype)

    return pl.pallas_call(
        body, grid=(m // tm, n // tn), in_specs=[a_spec, b_spec],
        out_specs=pl.BlockSpec((tm, tn), lambda i, j: (i, j)),
        out_shape=_sds((m, n), out_dtype), compiler_params=_cp(), name=name)(a, b)


def _mm_res_norm(a, b, res, gain, name, tm=512):
    (m, k), n = a.shape, b.shape[1]
    tm = min(tm, m)
    assert n == D and m % tm == 0, (name, m, n)

    def body(a_ref, b_ref, r_ref, *rest):
        xv = r_ref[...] + _dot(a_ref[...], b_ref[...], NN)
        if gain is None:
            rest[0][...] = xv
        else:
            g_ref, x_ref, h_ref = rest
            x_ref[...] = xv
            h_ref[...] = _rms(xv, g_ref[...]).astype(bf16)

    row = pl.BlockSpec((tm, D), lambda i: (i, 0))
    in_specs = [pl.BlockSpec((tm, k), lambda i: (i, 0)), pl.BlockSpec((k, D), lambda i: (0, 0)), row]
    args = [a, b, res]
    if gain is None:
        out_specs, out_shape = row, _sds((m, D), f32)
    else:
        in_specs.append(pl.BlockSpec((1, D), lambda i: (0, 0)))
        args.append(gain)
        out_specs, out_shape = [row, row], [_sds((m, D), f32), _sds((m, D), bf16)]
    return pl.pallas_call(body, grid=(m // tm,), in_specs=in_specs, out_specs=out_specs, out_shape=out_shape,
                          compiler_params=_cp(), name=name)(*args)


def _tile(n, cands):
    for c in cands:
        if n % c == 0:
            return c
    raise ValueError(n)


def _mm_auto(a, b, mode, out_dtype, name):
    m = a.shape[1] if mode == "tn" else a.shape[0]
    n = b.shape[0] if mode == "nt" else b.shape[1]
    return _mm(a, b, mode, _tile(m, (512, 256, 128)), _tile(n, (1024, 1408, 512, 384, 256, 128)), out_dtype, name)


def _mm_nn_slots(a, bs, out_dtype, name, after=()):
    (m, k), w = a.shape, bs.shape[2]
    tm = _tile(m, (512, 256, 128))

    def body(a_ref, b_ref, *rest):
        o_ref = rest[len(after)]
        o_ref[...] = _dot(a_ref[...], b_ref[...], NN).astype(out_dtype)

    return pl.pallas_call(
        body, grid=(m // tm, N_SLOTS),
        in_specs=[pl.BlockSpec((tm, k), lambda i, j: (i, 0)), pl.BlockSpec((None, k, w), lambda i, j: (j, 0, 0))]
        + [pl.BlockSpec(memory_space=pl.ANY)] * len(after),
        out_specs=pl.BlockSpec((tm, w), lambda i, j: (i, j)),
        out_shape=_sds((m, N_SLOTS * w), out_dtype), compiler_params=_cp(), name=name)(a, bs, *after)


def _mm_nt_slots(a, bs, name):
    (m, _), (_, r, w) = a.shape, bs.shape
    tm, tn = _tile(m, (512, 256, 128)), _tile(r, (1024, 512, 256, 128))

    def body(a_ref, b_ref, o_ref):
        acc = _dot(a_ref[:, 0:w], b_ref[0], NT)
        for s in range(1, N_SLOTS):
            acc = acc + _dot(a_ref[:, s * w:(s + 1) * w], b_ref[s], NT)
        o_ref[...] = acc

    return pl.pallas_call(
        body, grid=(m // tm, r // tn),
        in_specs=[pl.BlockSpec((tm, N_SLOTS * w), lambda i, j: (i, 0)), pl.BlockSpec((N_SLOTS, tn, w), lambda i, j: (0, j, 0))],
        out_specs=pl.BlockSpec((tm, tn), lambda i, j: (i, j)),
        out_shape=_sds((m, r), f32), compiler_params=_cp(), name=name)(a, bs)


def _mm_tn_slots(a, b, name):
    (k, m), w = a.shape, b.shape[1] // N_SLOTS
    tm = _tile(m, (512, 256, 128))

    def body(a_ref, b_ref, o_ref):
        o_ref[...] = _dot(a_ref[...], b_ref[...], TN).astype(bf16)

    return pl.pallas_call(
        body, grid=(m // tm, N_SLOTS),
        in_specs=[pl.BlockSpec((k, tm), lambda i, j: (0, i)), pl.BlockSpec((k, w), lambda i, j: (0, j))],
        out_specs=pl.BlockSpec((None, tm, w), lambda i, j: (j, i, 0)),
        out_shape=_sds((N_SLOTS, m, w), bf16), compiler_params=_cp(), name=name)(a, b)


def _rmsnorm_fwd(x, gain, name, after=()):
    t = x.shape[0]

    def body(x_ref, g_ref, *rest):
        h_ref = rest[len(after)]
        h_ref[...] = _rms(x_ref[...], g_ref[...]).astype(bf16)

    return pl.pallas_call(
        body, grid=(t // TT,),
        in_specs=[pl.BlockSpec((TT, D), lambda i: (i, 0)), pl.BlockSpec((1, D), lambda i: (0, 0))]
        + [pl.BlockSpec(memory_space=pl.ANY)] * len(after),
        out_specs=pl.BlockSpec((TT, D), lambda i: (i, 0)),
        out_shape=_sds((t, D), bf16), compiler_params=_cp(), name=name)(x, gain, *after)


def _rmsnorm_bwd(x, gain, dh, dres, name, after=()):
    t = x.shape[0]

    def body(x_ref, g_ref, dh_ref, dres_ref, *rest):
        dx_ref, dxb_ref, dg_ref = rest[len(after):]
        _, vjp = jax.vjp(_rms, x_ref[...], g_ref[...])
        dx, dg = vjp(dh_ref[...])
        dx = dx + dres_ref[...]
        dx_ref[...] = dx
        dxb_ref[...] = dx.astype(bf16)

        @pl.when(pl.program_id(0) == 0)
        def _():
            dg_ref[...] = jnp.zeros_like(dg_ref)

        dg_ref[...] += dg

    row = pl.BlockSpec((TT, D), lambda i: (i, 0))
    vec = pl.BlockSpec((1, D), lambda i: (0, 0))
    return pl.pallas_call(
        body, grid=(t // TT,), in_specs=[row, vec, row, row] + [pl.BlockSpec(memory_space=pl.ANY)] * len(after),
        out_specs=[row, row, vec],
        out_shape=[_sds((t, D), f32), _sds((t, D), bf16), _sds((1, D), f32)],
        compiler_params=_cp(), name=name)(x, gain, dh, dres, *after)


def _residual_add(x, y, name):
    t = x.shape[0]

    def body(x_ref, y_ref, o_ref):
        o_ref[...] = x_ref[...] + y_ref[...]

    row = pl.BlockSpec((TT, D), lambda i: (i, 0))
    return pl.pallas_call(body, grid=(t // TT,), in_specs=[row, row], out_specs=row,
                          out_shape=_sds((t, D), f32), compiler_params=_cp(), name=name)(x, y)


def _final_loss(x, gain, target, name):
    t = x.shape[0]

    def loss_fn(xv, gv, tv):
        e = _rms(xv, gv) - tv
        return 0.5 * jnp.sum(jnp.mean(e * e, axis=-1))

    def body(x_ref, g_ref, t_ref, loss_ref, dx_ref, dxb_ref, dg_ref):
        val, (dx, dg) = jax.value_and_grad(loss_fn, argnums=(0, 1))(x_ref[...], g_ref[...], t_ref[...])

        @pl.when(pl.program_id(0) == 0)
        def _():
            dg_ref[...] = jnp.zeros_like(dg_ref)
            loss_ref[...] = jnp.zeros_like(loss_ref)

        dx_ref[...] = dx
        dxb_ref[...] = dx.astype(bf16)
        dg_ref[...] += dg
        loss_ref[...] += jnp.full((1, 128), val, f32)

    row = pl.BlockSpec((TT, D), lambda i: (i, 0))
    vec = pl.BlockSpec((1, D), lambda i: (0, 0))
    return pl.pallas_call(
        body, grid=(t // TT,), in_specs=[row, vec, row],
        out_specs=[pl.BlockSpec((1, 128), lambda i: (0, 0)), row, row, vec],
        out_shape=[_sds((1, 128), f32), _sds((t, D), f32), _sds((t, D), bf16), _sds((1, D), f32)],
        compiler_params=_cp(), name=name)(x, gain, target)


def _conv_fwd(x, col0, c, w, b, name, tc=256, val=None, val_col0=0):
    t = x.shape[0]
    width = w.shape[0]
    nt = t // TT
    hb = TT // 8

    def body(*refs):
        if val is None:
            x_ref, xh_ref, w_ref, b_ref, o_ref, xp = refs
        else:
            x_ref, xh_ref, w_ref, b_ref, v_ref, o_ref, act_ref, xp = refs
        i = pl.program_id(1)
        xp[0:8, :] = jnp.where(i == 0, 0.0, xh_ref[...])
        xp[8:, :] = x_ref[...]
        acc = jnp.zeros((TT, tc), f32) + b_ref[...]
        for k in range(width):
            acc = acc + w_ref[k:k + 1, :] * xp[pl.ds(8 - (width - 1) + k, TT), :]
        o_ref[...] = acc
        if val is not None:
            act_ref[...] = (_silu(acc) * v_ref[...]).astype(bf16)

    in_specs = [
        pl.BlockSpec((TT, tc), lambda j, i: (i, j + col0)),
        pl.BlockSpec((8, tc), lambda j, i: (jnp.maximum(i * hb - 1, 0), j + col0)),
        pl.BlockSpec((width, tc), lambda j, i: (0, j)),
        pl.BlockSpec((1, tc), lambda j, i: (0, j)),
    ]
    args = [x, x, w, b]
    out_specs = [pl.BlockSpec((TT, tc), lambda j, i: (i, j))]
    out_shape = [_sds((t, c), f32)]
    if val is not None:
        in_specs.append(pl.BlockSpec((TT, tc), lambda j, i: (i, j + val_col0)))
        args.append(val)
        out_specs.append(pl.BlockSpec((TT, tc), lambda j, i: (i, j)))
        out_shape.append(_sds((t, c), bf16))
    res = pl.pallas_call(
        body, grid=(c // tc, nt), in_specs=in_specs, out_specs=out_specs, out_shape=out_shape,
        scratch_shapes=[pltpu.VMEM((TT + 8, tc), f32)], compiler_params=_cp(), name=name)(*args)
    return res[0] if val is None else res


def _conv_bwd(dc, x, col0, w, name, tc=256, dx_dtype=bf16, into=None):
    t, c = dc.shape
    width = w.shape[0]
    nt = t // TT
    hb = TT // 8

    def body(dc_ref, dcn_ref, x_ref, xh_ref, w_ref, *rest):
        dx_ref, dw_ref, db_ref, dcp, xp = rest[(0 if into is None else 1):]
        i = pl.program_id(1)
        dcv = dc_ref[...]
        dcp[0:TT, :] = dcv
        dcp[TT:, :] = jnp.where(i == nt - 1, 0.0, dcn_ref[...])
        xp[0:8, :] = jnp.where(i == 0, 0.0, xh_ref[...])
        xp[8:, :] = x_ref[...]

        @pl.when(i == 0)
        def _():
            dw_ref[...] = jnp.zeros_like(dw_ref)
            db_ref[...] = jnp.zeros_like(db_ref)

        acc = jnp.zeros((TT, tc), f32)
        for k in range(width):
            acc = acc + w_ref[k:k + 1, :] * dcp[pl.ds((width - 1) - k, TT), :]
            dw_ref[k:k + 1, :] += jnp.sum(dcv * xp[pl.ds(8 - (width - 1) + k, TT), :], axis=0, keepdims=True)
        dx_ref[...] = acc.astype(dx_dtype)
        db_ref[...] += jnp.sum(dcv, axis=0, keepdims=True)

    in_specs = [
        pl.BlockSpec((TT, tc), lambda j, i: (i, j)),
        pl.BlockSpec((8, tc), lambda j, i: (jnp.minimum((i + 1) * hb, t // 8 - 1), j)),
        pl.BlockSpec((TT, tc), lambda j, i: (i, j + col0)),
        pl.BlockSpec((8, tc), lambda j, i: (jnp.maximum(i * hb - 1, 0), j + col0)),
        pl.BlockSpec((width, tc), lambda j, i: (0, j)),
    ]
    out_specs = [
        pl.BlockSpec((TT, tc), lambda j, i: (i, j)),
        pl.BlockSpec((8, tc), lambda j, i: (0, j)),
        pl.BlockSpec((1, tc), lambda j, i: (0, j)),
    ]
    args, aliases, dx_shape = [dc, dc, x, x, w], {}, _sds((t, c), dx_dtype)
    if into is not None:
        in_specs.append(pl.BlockSpec(memory_space=pl.ANY))
        args.append(into)
        aliases, dx_shape = {5: 0}, _sds(into.shape, into.dtype)
    return pl.pallas_call(
        body, grid=(c // tc, nt), in_specs=in_specs, out_specs=out_specs,
        out_shape=[dx_shape, _sds((8, c), f32), _sds((1, c), f32)], input_output_aliases=aliases,
        scratch_shapes=[pltpu.VMEM((TT + 8, tc), f32), pltpu.VMEM((TT + 8, tc), f32)],
        compiler_params=_cp(), name=name)(*args)


def _bdot_impl(a, b, dims):
    return _dot(a.astype(bf16), b.astype(bf16), dims)


@functools.partial(jax.custom_vjp, nondiff_argnums=(2,))
def _bdot(a, b, dims):
    return _bdot_impl(a, b, dims)


def _bdot_fwd(a, b, dims):
    return _bdot_impl(a, b, dims), (a, b)


def _bdot_bwd(dims, res, ct):
    a, b = res
    if dims == NN:
        return _bdot_impl(ct, b, NT), _bdot_impl(a, ct, TN)
    if dims == NT:
        return _bdot_impl(ct, b, NN), _bdot_impl(ct, a, TN)
    return _bdot_impl(b, ct, NT), _bdot_impl(a, ct, NN)


_bdot.defvjp(_bdot_fwd, _bdot_bwd)


def _split2(a):
    hi = a.astype(bf16)
    return hi, (a - hi.astype(f32)).astype(bf16)


def _dot3_impl(a, b, dims):
    a_hi, a_lo = _split2(a)
    b_hi, b_lo = _split2(b)
    return (_dot(a_hi, b_hi, dims) + _dot(a_hi, b_lo, dims)) + _dot(a_lo, b_hi, dims)


@functools.partial(jax.custom_vjp, nondiff_argnums=(2,))
def _dot3(a, b, dims):
    return _dot3_impl(a, b, dims)


def _dot3_fwd(a, b, dims):
    return _dot3_impl(a, b, dims), (a, b)


def _dot3_bwd(dims, res, ct):
    a, b = res
    if dims == NN:
        return _dot3_impl(ct, b, NT), _dot3_impl(a, ct, TN)
    if dims == NT:
        return _dot3_impl(ct, b, NN), _dot3_impl(ct, a, TN)
    return _dot3_impl(b, ct, NT), _dot3_impl(a, ct, NN)


_dot3.defvjp(_dot3_fwd, _dot3_bwd)


def _tril_dot_impl(tril, x, dims):
    t = tril.astype(bf16)
    x1 = x.astype(bf16)
    r1 = x - x1.astype(f32)
    x2 = r1.astype(bf16)
    x3 = (r1 - x2.astype(f32)).astype(bf16)
    return (_dot(t, x3, dims) + _dot(t, x2, dims)) + _dot(t, x1, dims)


@jax.custom_vjp
def _cumsum_rows(tril, x):
    return _tril_dot_impl(tril, x, NN)


def _cumsum_rows_fwd(tril, x):
    return _tril_dot_impl(tril, x, NN), tril


def _cumsum_rows_bwd(tril, ct):
    return jnp.zeros_like(tril), _tril_dot_impl(tril, ct, TN)


_cumsum_rows.defvjp(_cumsum_rows_fwd, _cumsum_rows_bwd)


def _gdn_intra(cq, ck, cv, ba, alog_v, dtb_v, hd):
    c = GDN_CHUNK
    lane = lax.broadcasted_iota(jnp.int32, (1, HEAD), 1)
    mb = (lane == hd).astype(f32)
    ma = (lane == hd + NH).astype(f32)
    beta = jax.nn.sigmoid(jnp.sum(ba * mb, axis=1, keepdims=True))
    alpha = jnp.sum(ba * ma, axis=1, keepdims=True)
    alog = jnp.sum(alog_v * ma, axis=1, keepdims=True)
    dtb = jnp.sum(dtb_v * ma, axis=1, keepdims=True)
    g = -jnp.exp(alog) * jax.nn.softplus(alpha + dtb)
    q = _silu(cq)
    q = q * lax.rsqrt(jnp.sum(q * q, axis=-1, keepdims=True) + EPS) * (HEAD ** -0.5)
    k = _silu(ck)
    k = k * lax.rsqrt(jnp.sum(k * k, axis=-1, keepdims=True) + EPS)
    v = _silu(cv)
    row = lax.broadcasted_iota(jnp.int32, (c, c), 0)
    col = lax.broadcasted_iota(jnp.int32, (c, c), 1)
    causal = row >= col
    tril = causal.astype(f32)
    gc = _cumsum_rows(tril, jnp.broadcast_to(g, (c, HEAD)))
    gcc = _cumsum_rows(tril, jnp.broadcast_to(g, (c, c)))
    diff = jnp.where(causal, gcc - gcc.T, 0.0)
    decay = jnp.where(causal, jnp.exp(diff), 0.0)
    kb = k * beta
    nmat = -jnp.where(row > col, _bdot(kb, k, NT) * decay, 0.0)
    egc = jnp.exp(gc)
    sol = jnp.concatenate([v * beta, kb * egc], axis=1)
    p = nmat
    for it in range(6):
        sol = sol + _dot3(p, sol, NN)
        if it < 5:
            p = _dot3(p, p, NN)
    u, w = sol[:, :HEAD], sol[:, HEAD:]
    attn = _bdot(q, k, NT) * decay
    rowv = lax.broadcasted_iota(jnp.int32, (c, 1), 0)
    gc_last = jnp.sum(jnp.where(rowv == c - 1, gc, 0.0), axis=0, keepdims=True)
    return u, w, q * egc, k * jnp.exp(gc_last - gc), attn, jnp.exp(gc_last)


def _gdn_intra4(cq, ck, cv, ba, alog_v, dtb_v):
    c = GDN_CHUNK
    hs = range(NH)
    lane = lax.broadcasted_iota(jnp.int32, (1, HEAD), 1)
    mb = [(lane == h).astype(f32) for h in hs]
    ma = [(lane == h + NH).astype(f32) for h in hs]
    beta = [jax.nn.sigmoid(jnp.sum(ba * mb[h], axis=1, keepdims=True)) for h in hs]
    alpha = [jnp.sum(ba * ma[h], axis=1, keepdims=True) for h in hs]
    alog = [jnp.sum(alog_v * ma[h], axis=1, keepdims=True) for h in hs]
    dtb = [jnp.sum(dtb_v * ma[h], axis=1, keepdims=True) for h in hs]
    g = [-jnp.exp(alog[h]) * jax.nn.softplus(alpha[h] + dtb[h]) for h in hs]
    q = [_silu(cq[h]) for h in hs]
    q = [q[h] * lax.rsqrt(jnp.sum(q[h] * q[h], axis=-1, keepdims=True) + EPS) * (HEAD ** -0.5) for h in hs]
    k = [_silu(ck[h]) for h in hs]
    k = [k[h] * lax.rsqrt(jnp.sum(k[h] * k[h], axis=-1, keepdims=True) + EPS) for h in hs]
    v = [_silu(cv[h]) for h in hs]
    row = lax.broadcasted_iota(jnp.int32, (c, c), 0)
    col = lax.broadcasted_iota(jnp.int32, (c, c), 1)
    causal = row >= col
    tril = causal.astype(f32)
    gc = [_cumsum_rows(tril, jnp.broadcast_to(g[h], (c, HEAD))) for h in hs]
    gcc = [_cumsum_rows(tril, jnp.broadcast_to(g[h], (c, c))) for h in hs]
    decay = [jnp.where(causal, jnp.exp(jnp.where(causal, gcc[h] - gcc[h].T, 0.0)), 0.0) for h in hs]
    kb = [k[h] * beta[h] for h in hs]
    kk = [_bdot(kb[h], k[h], NT) for h in hs]
    p = [-jnp.where(row > col, kk[h] * decay[h], 0.0) for h in hs]
    egc = [jnp.exp(gc[h]) for h in hs]
    sol = [jnp.concatenate([v[h] * beta[h], kb[h] * egc[h]], axis=1) for h in hs]
    for it in range(6):
        upd = [_dot3(p[h], sol[h], NN) for h in hs]
        sol = [sol[h] + upd[h] for h in hs]
        if it < 5:
            p = [_dot3(p[h], p[h], NN) for h in hs]
    qk = [_bdot(q[h], k[h], NT) for h in hs]
    attn = [qk[h] * decay[h] for h in hs]
    rowv = lax.broadcasted_iota(jnp.int32, (c, 1), 0)
    gc_last = [jnp.sum(jnp.where(rowv == c - 1, gc[h], 0.0), axis=0, keepdims=True) for h in hs]
    return ([sol[h][:, :HEAD] for h in hs], [sol[h][:, HEAD:] for h in hs], [q[h] * egc[h] for h in hs],
            [k[h] * jnp.exp(gc_last[h] - gc[h]) for h in hs], attn, [jnp.exp(gc_last[h]) for h in hs])


def _gdn_seq(u, w, q_dec, k_dec, attn, dl, z, s, gain):
    v_new = u - _bdot(w, s, NN)
    o = _bdot(q_dec, s, NN) + _bdot(attn, v_new, NN)
    s_new = s * dl + _bdot(k_dec, v_new, TN)
    return _rms(o, gain) * _silu(z), s_new


def _gdn_seq4(u, w, q_dec, k_dec, attn, dl, z, s, gain):
    hs = range(NH)
    ws = [_bdot(w[h], s[h], NN) for h in hs]
    qs = [_bdot(q_dec[h], s[h], NN) for h in hs]
    v_new = [u[h] - ws[h] for h in hs]
    av = [_bdot(attn[h], v_new[h], NN) for h in hs]
    kv = [_bdot(k_dec[h], v_new[h], TN) for h in hs]
    o = [_rms(qs[h] + av[h], gain) * _silu(z[h]) for h in hs]
    return o, [s[h] * dl[h] + kv[h] for h in hs]


def _hsl(h):
    return slice(h * HEAD, (h + 1) * HEAD)


def _gdn2_fwd(conv, p, alog_v, dtb_v, gain, name):
    t = conv.shape[0]
    c = GDN_CHUNK
    nch = t // c
    w512 = NH * HEAD
    wide = lambda off: pl.BlockSpec((c, w512), lambda n: (n, off))
    vec = pl.BlockSpec((1, HEAD), lambda n: (0, 0))
    attn_spec = pl.BlockSpec((1, NH, c, c), lambda n: (n, 0, 0, 0))
    dl_spec = pl.BlockSpec((1, NH, HEAD), lambda n: (n, 0, 0))

    cps = GDN_INTRA_CHUNKS
    iwide = lambda off: pl.BlockSpec((cps * c, w512), lambda n: (n, off))

    def intra(cq, ck, cv, ba, al, dt, u_ref, w_ref, qd_ref, kd_ref, at_ref, dl_ref):
        for ci in range(cps):
            rows = slice(ci * c, (ci + 1) * c)
            u, w, qd, kd, at, dl = _gdn_intra4([cq[rows, _hsl(h)] for h in range(NH)], [ck[rows, _hsl(h)] for h in range(NH)],
                                               [cv[rows, _hsl(h)] for h in range(NH)], ba[rows, :], al[...], dt[...])
            for h in range(NH):
                u_ref[rows, _hsl(h)] = u[h]
                w_ref[rows, _hsl(h)] = w[h]
                qd_ref[rows, _hsl(h)] = qd[h]
                kd_ref[rows, _hsl(h)] = kd[h]
                at_ref[ci, h] = at[h]
                dl_ref[ci, h:h + 1, :] = dl[h]

    u, w, qd, kd, at, dl = pl.pallas_call(
        intra, grid=(nch // cps,),
        in_specs=[iwide(0), iwide(1), iwide(2), pl.BlockSpec((cps * c, HEAD), lambda n: (n, C_BA // HEAD)), vec, vec],
        out_specs=[iwide(0)] * 4 + [pl.BlockSpec((cps, NH, c, c), lambda n: (n, 0, 0, 0)),
                                    pl.BlockSpec((cps, NH, HEAD), lambda n: (n, 0, 0))],
        out_shape=[_sds((t, w512), f32)] * 4 + [_sds((nch, NH, c, c), f32), _sds((nch, NH, HEAD), f32)],
        compiler_params=_cp(), name=name + "_intra")(conv, conv, conv, p, alog_v, dtb_v)

    def seq(u_ref, w_ref, qd_ref, kd_ref, at_ref, dl_ref, z_ref, gn, o_ref, ss_ref, s_scr):
        @pl.when(pl.program_id(0) == 0)
        def _():
            s_scr[...] = jnp.zeros_like(s_scr)

        hs = range(NH)
        s = [s_scr[h] for h in hs]
        for h in hs:
            ss_ref[0, h] = s[h]
        o, s_new = _gdn_seq4([u_ref[:, _hsl(h)] for h in hs], [w_ref[:, _hsl(h)] for h in hs], [qd_ref[:, _hsl(h)] for h in hs],
                             [kd_ref[:, _hsl(h)] for h in hs], [at_ref[0, h] for h in hs], [dl_ref[0, h:h + 1, :] for h in hs],
                             [z_ref[:, _hsl(h)] for h in hs], s, gn[...])
        for h in hs:
            o_ref[:, _hsl(h)] = o[h].astype(bf16)
            s_scr[h] = s_new[h]

    o, states = pl.pallas_call(
        seq, grid=(nch,),
        in_specs=[wide(0)] * 4 + [attn_spec, dl_spec, wide(C_Z // w512), vec],
        out_specs=[wide(0), pl.BlockSpec((1, NH, HEAD, HEAD), lambda n: (n, 0, 0, 0))],
        out_shape=[_sds((t, w512), bf16), _sds((nch, NH, HEAD, HEAD), f32)],
        scratch_shapes=[pltpu.VMEM((NH, HEAD, HEAD), f32)],
        compiler_params=_cp(), name=name + "_seq")(u, w, qd, kd, at, dl, p, gain)
    return o, dict(u=u, w=w, qd=qd, kd=kd, at=at, dl=dl, states=states)


def _gdn2_bwd(conv, p, alog_v, dtb_v, gain, saved, do, name):
    t = conv.shape[0]
    c = GDN_CHUNK
    nch = t // c
    w512 = NH * HEAD
    rwide = lambda off: pl.BlockSpec((c, w512), lambda n: (nch - 1 - n, off))
    rvec = pl.BlockSpec((1, HEAD), lambda n: (0, 0))
    rattn = pl.BlockSpec((1, NH, c, c), lambda n: (nch - 1 - n, 0, 0, 0))
    rdl = pl.BlockSpec((1, NH, HEAD), lambda n: (nch - 1 - n, 0, 0))

    def seq_bwd(u_ref, w_ref, qd_ref, kd_ref, at_ref, dl_ref, z_ref, gn, ss_ref, do_ref,
                du_ref, dw_ref, dqd_ref, dkd_ref, dat_ref, ddl_ref, dz_ref, dgn_ref, ds_scr):
        @pl.when(pl.program_id(0) == 0)
        def _():
            ds_scr[...] = jnp.zeros_like(ds_scr)
            dgn_ref[...] = jnp.zeros_like(dgn_ref)

        hs = range(NH)
        _, vjp = jax.vjp(_gdn_seq4, [u_ref[:, _hsl(h)] for h in hs], [w_ref[:, _hsl(h)] for h in hs],
                         [qd_ref[:, _hsl(h)] for h in hs], [kd_ref[:, _hsl(h)] for h in hs], [at_ref[0, h] for h in hs],
                         [dl_ref[0, h:h + 1, :] for h in hs], [z_ref[:, _hsl(h)] for h in hs], [ss_ref[0, h] for h in hs], gn[...])
        du, dw, dqd, dkd, dat, ddl, dz, ds, dg = vjp(([do_ref[:, _hsl(h)] for h in hs], [ds_scr[h] for h in hs]))
        for h in hs:
            du_ref[:, _hsl(h)] = du[h]
            dw_ref[:, _hsl(h)] = dw[h]
            dqd_ref[:, _hsl(h)] = dqd[h]
            dkd_ref[:, _hsl(h)] = dkd[h]
            dat_ref[0, h] = dat[h]
            ddl_ref[0, h:h + 1, :] = ddl[h]
            dz_ref[:, _hsl(h)] = dz[h]
            ds_scr[h] = ds[h]
        dgn_ref[...] += dg

    du, dw, dqd, dkd, dat, ddl, dz, dgn = pl.pallas_call(
        seq_bwd, grid=(nch,),
        in_specs=[rwide(0)] * 4 + [rattn, rdl, rwide(C_Z // w512), rvec,
                                   pl.BlockSpec((1, NH, HEAD, HEAD), lambda n: (nch - 1 - n, 0, 0, 0)), rwide(0)],
        out_specs=[rwide(0)] * 4 + [rattn, rdl, rwide(0), rvec],
        out_shape=[_sds((t, w512), f32)] * 4 + [_sds((nch, NH, c, c), f32), _sds((nch, NH, HEAD), f32),
                                                _sds((t, w512), f32), _sds((1, HEAD), f32)],
        scratch_shapes=[pltpu.VMEM((NH, HEAD, HEAD), f32)],
        compiler_params=_cp(), name=name + "_seq")(
            saved["u"], saved["w"], saved["qd"], saved["kd"], saved["at"], saved["dl"], p, gain, saved["states"], do)

    cps = GDN_INTRA_CHUNKS
    wide = lambda off: pl.BlockSpec((cps * c, w512), lambda n: (n, off))
    vec = pl.BlockSpec((1, HEAD), lambda n: (0, 0))
    attn_spec = pl.BlockSpec((cps, NH, c, c), lambda n: (n, 0, 0, 0))
    dl_spec = pl.BlockSpec((cps, NH, HEAD), lambda n: (n, 0, 0))

    def intra_bwd(cq, ck, cv, ba, al, dt, du_ref, dw_ref, dqd_ref, dkd_ref, dat_ref, ddl_ref,
                  dc_ref, dba_ref, dal_ref, ddt_ref):
        @pl.when(pl.program_id(0) == 0)
        def _():
            dal_ref[...] = jnp.zeros_like(dal_ref)
            ddt_ref[...] = jnp.zeros_like(ddt_ref)

        dal = jnp.zeros((1, HEAD), f32)
        ddt = jnp.zeros((1, HEAD), f32)
        for ci in range(cps):
            rows = slice(ci * c, (ci + 1) * c)
            hs = range(NH)
            _, vjp = jax.vjp(_gdn_intra4, [cq[rows, _hsl(h)] for h in hs], [ck[rows, _hsl(h)] for h in hs],
                             [cv[rows, _hsl(h)] for h in hs], ba[rows, :], al[...], dt[...])
            g_q, g_k, g_v, g_ba, g_al, g_dt = vjp((
                [du_ref[rows, _hsl(h)] for h in hs], [dw_ref[rows, _hsl(h)] for h in hs], [dqd_ref[rows, _hsl(h)] for h in hs],
                [dkd_ref[rows, _hsl(h)] for h in hs], [dat_ref[ci, h] for h in hs], [ddl_ref[ci, h:h + 1, :] for h in hs]))
            for h in hs:
                dc_ref[rows, _hsl(h)] = g_q[h]
                dc_ref[rows, _hsl(NH + h)] = g_k[h]
                dc_ref[rows, _hsl(2 * NH + h)] = g_v[h]
            dal = dal + g_al
            ddt = ddt + g_dt
            dba_ref[rows, :] = g_ba
        dal_ref[...] += dal
        ddt_ref[...] += ddt

    dconv, dba, dal, ddt = pl.pallas_call(
        intra_bwd, grid=(nch // cps,),
        in_specs=[wide(0), wide(1), wide(2), pl.BlockSpec((cps * c, HEAD), lambda n: (n, C_BA // HEAD)), vec, vec]
        + [wide(0)] * 4 + [attn_spec, dl_spec],
        out_specs=[pl.BlockSpec((cps * c, 3 * w512), lambda n: (n, 0)), pl.BlockSpec((cps * c, HEAD), lambda n: (n, 0)),
                   vec, vec],
        out_shape=[_sds((t, 3 * w512), f32), _sds((t, HEAD), f32), _sds((1, HEAD), f32), _sds((1, HEAD), f32)],
        compiler_params=_cp(), name=name + "_intra")(conv, conv, conv, p, alog_v, dtb_v, du, dw, dqd, dkd, dat, ddl)
    return dconv, dz, dba, dal, ddt, dgn


def _hgrn_intra(qb, fb, ib, lb):
    c = HGRN_CHUNK
    ns = range(len(qb))
    f = [lb + (1.0 - lb) * jax.nn.sigmoid(fb[i]) for i in ns]
    logf = [jnp.log(jnp.maximum(f[i], F_FLOOR)) for i in ns]
    k = [1.0 - f[i] for i in ns]
    q = [_silu(qb[i]) for i in ns]
    row = lax.broadcasted_iota(jnp.int32, (c, c), 0)
    col = lax.broadcasted_iota(jnp.int32, (c, c), 1)
    tril = (row >= col).astype(f32)
    b = [_cumsum_rows(tril, logf[i]) for i in ns]
    ri = lax.broadcasted_iota(jnp.int32, (c, 1), 0)
    o = [jnp.zeros((c, HEAD), f32) for _ in ns]
    for j in range(c):
        mj = ri == j
        ok = ri >= j
        bj = [jnp.sum(jnp.where(mj, b[i], 0.0), axis=0, keepdims=True) for i in ns]
        kj = [jnp.sum(jnp.where(mj, k[i], 0.0), axis=0, keepdims=True) for i in ns]
        vj = [jnp.sum(jnp.where(mj, ib[i], 0.0), axis=0, keepdims=True) for i in ns]
        e = [jnp.where(ok, jnp.exp(jnp.where(ok, b[i] - bj[i], 0.0)), 0.0) for i in ns]
        s = [jnp.sum(q[i] * kj[i] * e[i], axis=1, keepdims=True) for i in ns]
        o = [o[i] + s[i] * vj[i] for i in ns]
    b_last = [jnp.sum(jnp.where(ri == c - 1, b[i], 0.0), axis=0, keepdims=True) for i in ns]
    return (o, [q[i] * jnp.exp(b[i]) for i in ns], [k[i] * jnp.exp(b_last[i] - b[i]) for i in ns],
            [jnp.exp(b_last[i]) for i in ns])


def _hgrn_seq(o_intra, q_dec, k_dec, dl, v, gb, st, gain):
    o = o_intra + _bdot(q_dec, st, NT)
    st_new = st * dl + _bdot(v, k_dec, TN)
    return _rms(o, gain) * _silu(gb), st_new


def _hgrn_fwd(p, lb, gain, name):
    t = p.shape[0]
    r = HGRN_STEP
    ns = t // r
    nsub = r // HGRN_CHUNK
    blk = lambda off: pl.BlockSpec((r, HEAD), lambda n, h: (n, off // HEAD + h))

    def body(qb, fb, ib, gb, lb_ref, gn, o_ref, ss_ref, s_scr):
        n, h = pl.program_id(0), pl.program_id(1)

        @pl.when(n == 0)
        def _():
            s_scr[h] = jnp.zeros((HEAD, HEAD), f32)

        st = s_scr[h]
        ss_ref[0, 0] = st
        rows = [pl.ds(ch * HGRN_CHUNK, HGRN_CHUNK) for ch in range(nsub)]
        v = [ib[rw, :] for rw in rows]
        oi, qd, kd, dl = _hgrn_intra([qb[rw, :] for rw in rows], [fb[rw, :] for rw in rows], v, lb_ref[...])
        for ch in range(nsub):
            o, st = _hgrn_seq(oi[ch], qd[ch], kd[ch], dl[ch], v[ch], gb[rows[ch], :], st, gn[...])
            o_ref[rows[ch], :] = o.astype(bf16)
        s_scr[h] = st

    return pl.pallas_call(
        body, grid=(ns, NH),
        in_specs=[blk(C_QB), blk(C_FB), blk(C_IB), blk(C_GB),
                  pl.BlockSpec((1, HEAD), lambda n, h: (0, h)), pl.BlockSpec((1, HEAD), lambda n, h: (0, 0))],
        out_specs=[pl.BlockSpec((r, HEAD), lambda n, h: (n, h)),
                   pl.BlockSpec((1, 1, HEAD, HEAD), lambda n, h: (n, h, 0, 0))],
        out_shape=[_sds((t, NH * HEAD), bf16), _sds((ns, NH, HEAD, HEAD), f32)],
        scratch_shapes=[pltpu.VMEM((NH, HEAD, HEAD), f32)],
        compiler_params=_cp(), name=name)(p, p, p, p, lb, gain)


def _hgrn_bwd(p, lb, gain, states, do, name):
    t = p.shape[0]
    r = HGRN_STEP
    ns = t // r
    nsub = r // HGRN_CHUNK
    blk = lambda off: pl.BlockSpec((r, HEAD), lambda n, h: (ns - 1 - n, off // HEAD + h))
    hblk = pl.BlockSpec((r, HEAD), lambda n, h: (ns - 1 - n, h))

    def body(qb, fb, ib, gb, lb_ref, gn, ss_ref, do_ref,
             dqb, dfb, dib, dgb, dlb_ref, dgn_ref, ds_scr, st_scr):
        n, h = pl.program_id(0), pl.program_id(1)

        @pl.when(n == 0)
        def _():
            ds_scr[h] = jnp.zeros((HEAD, HEAD), f32)

        @pl.when((n == 0) & (h == 0))
        def _():
            dgn_ref[...] = jnp.zeros_like(dgn_ref)

        gnv = gn[...]
        rows = [pl.ds(ch * HGRN_CHUNK, HGRN_CHUNK) for ch in range(nsub)]
        v = [ib[rw, :] for rw in rows]
        (oi, qd, kd, dl), vjp_intra = jax.vjp(_hgrn_intra, [qb[rw, :] for rw in rows], [fb[rw, :] for rw in rows], v, lb_ref[...])
        st = ss_ref[0, 0]
        for ch in range(nsub):
            st_scr[ch] = st
            if ch < nsub - 1:
                st = st * dl[ch] + _bdot(v[ch], kd[ch], TN)
        ds = ds_scr[h]
        dgn = jnp.zeros((1, HEAD), f32)
        d_oi, d_qd, d_kd, d_dl, d_v = [None] * nsub, [None] * nsub, [None] * nsub, [None] * nsub, [None] * nsub
        for ch in reversed(range(nsub)):
            _, vjp = jax.vjp(_hgrn_seq, oi[ch], qd[ch], kd[ch], dl[ch], v[ch], gb[rows[ch], :], st_scr[ch], gnv)
            d_oi[ch], d_qd[ch], d_kd[ch], d_dl[ch], d_v[ch], g_g, ds, g_gn = vjp((do_ref[rows[ch], :], ds))
            dgb[rows[ch], :] = g_g
            dgn = dgn + g_gn
        g_q, g_f, g_i, g_lb = vjp_intra((d_oi, d_qd, d_kd, d_dl))
        for ch in range(nsub):
            dqb[rows[ch], :] = g_q[ch]
            dfb[rows[ch], :] = g_f[ch]
            dib[rows[ch], :] = g_i[ch] + d_v[ch]
        ds_scr[h] = ds
        dlb_ref[0] = g_lb
        dgn_ref[...] += dgn

    return pl.pallas_call(
        body, grid=(ns, NH),
        in_specs=[blk(C_QB), blk(C_FB), blk(C_IB), blk(C_GB),
                  pl.BlockSpec((1, HEAD), lambda n, h: (0, h)), pl.BlockSpec((1, HEAD), lambda n, h: (0, 0)),
                  pl.BlockSpec((1, 1, HEAD, HEAD), lambda n, h: (ns - 1 - n, h, 0, 0)),
                  pl.BlockSpec((r, HEAD), lambda n, h: (ns - 1 - n, NH + h))],
        out_specs=[hblk, hblk, hblk, hblk,
                   pl.BlockSpec((1, 1, HEAD), lambda n, h: (n, 0, h)),
                   pl.BlockSpec((1, HEAD), lambda n, h: (0, 0))],
        out_shape=[_sds((t, 512), f32)] * 4 + [_sds((ns, 1, 512), f32), _sds((1, HEAD), f32)],
        scratch_shapes=[pltpu.VMEM((NH, HEAD, HEAD), f32), pltpu.VMEM((nsub, HEAD, HEAD), f32)],
        compiler_params=_cp(), name=name)(p, p, p, p, lb, gain, states, do)


def _lru_gates(xb, wa, wx, ba, bx, lam):
    xh = xb.astype(bf16)
    r = jax.nn.sigmoid(_dot(xh, wa.astype(bf16), NN) + ba)
    i = jax.nn.sigmoid(_dot(xh, wx.astype(bf16), NN) + bx)
    log_a = -RG_C * r * jax.nn.softplus(-lam)
    a = jnp.exp(log_a)
    t2 = 2.0 * log_a
    series = -t2 * (1.0 + t2 * (0.5 + t2 * (1.0 / 6.0 + t2 * (1.0 / 24.0))))
    om = jnp.where(t2 > -1e-2, series, 1.0 - jnp.exp(t2))
    u = jnp.sqrt(jnp.maximum(om, 0.0)) * (i * xb)
    return a, u


def _lru_gates_fwd(xc, wa, wx, ba, bx, lam, name):
    t = xc.shape[0]
    blk = pl.BlockSpec((TT, LRU_BLOCK), lambda h, i: (i, h))
    wsp = pl.BlockSpec((1, LRU_BLOCK, LRU_BLOCK), lambda h, i: (h, 0, 0))
    vsp = pl.BlockSpec((1, LRU_BLOCK), lambda h, i: (0, h))

    def body(x_ref, wa_ref, wx_ref, ba_ref, bx_ref, lam_ref, a_ref, u_ref):
        a, u = _lru_gates(x_ref[...], wa_ref[0], wx_ref[0], ba_ref[...], bx_ref[...], lam_ref[...])
        a_ref[...] = a
        u_ref[...] = u

    return pl.pallas_call(
        body, grid=(NH, t // TT), in_specs=[blk, wsp, wsp, vsp, vsp, vsp], out_specs=[blk, blk],
        out_shape=[_sds((t, D), f32)] * 2, compiler_params=_cp(), name=name)(xc, wa, wx, ba, bx, lam)


def _lru_gates_bwd(xc, wa, wx, ba, bx, lam, da, du, name):
    t = xc.shape[0]
    blk = pl.BlockSpec((TT, LRU_BLOCK), lambda h, i: (i, h))
    wsp = pl.BlockSpec((1, LRU_BLOCK, LRU_BLOCK), lambda h, i: (h, 0, 0))
    vsp = pl.BlockSpec((1, LRU_BLOCK), lambda h, i: (0, h))

    def body(x_ref, wa_ref, wx_ref, ba_ref, bx_ref, lam_ref, da_ref, du_ref,
             dx_ref, dwa_ref, dwx_ref, dba_ref, dbx_ref, dlam_ref):
        @pl.when(pl.program_id(1) == 0)
        def _():
            for r in (dwa_ref, dwx_ref, dba_ref, dbx_ref, dlam_ref):
                r[...] = jnp.zeros_like(r)

        _, vjp = jax.vjp(_lru_gates, x_ref[...], wa_ref[0], wx_ref[0], ba_ref[...], bx_ref[...], lam_ref[...])
        dx, dwa, dwx, dba, dbx, dlam = vjp((da_ref[...], du_ref[...]))
        dx_ref[...] = dx
        dwa_ref[0] += dwa
        dwx_ref[0] += dwx
        dba_ref[...] += dba
        dbx_ref[...] += dbx
        dlam_ref[...] += dlam

    return pl.pallas_call(
        body, grid=(NH, t // TT), in_specs=[blk, wsp, wsp, vsp, vsp, vsp, blk, blk],
        out_specs=[blk, wsp, wsp, vsp, vsp, vsp],
        out_shape=[_sds((t, D), f32), _sds((NH, LRU_BLOCK, LRU_BLOCK), f32), _sds((NH, LRU_BLOCK, LRU_BLOCK), f32),
                   _sds((1, D), f32), _sds((1, D), f32), _sds((1, D), f32)],
        compiler_params=_cp(), name=name)(xc, wa, wx, ba, bx, lam, da, du)


_SCAN_SHIFTS = (1, 2, 4, 8, 16, 32, 64, 128)
_SCAN_PAD = 128


def _gelu(y):
    return jax.nn.gelu(y, approximate=True)


def _lru_scan_fwd(a, u, p2, name):
    t = a.shape[0]
    tc = 128
    blk = pl.BlockSpec((TT, tc), lambda j, i: (i, j))

    def body(a_ref, u_ref, y_ref, h_ref, hg_ref, a_s, b_s, carry):
        i = pl.program_id(1)

        @pl.when(i == 0)
        def _():
            carry[...] = jnp.zeros_like(carry)
            a_s[0:_SCAN_PAD, :] = jnp.ones((_SCAN_PAD, tc), f32)
            b_s[0:_SCAN_PAD, :] = jnp.zeros((_SCAN_PAD, tc), f32)

        av, bv = a_ref[...], u_ref[...]
        for s in _SCAN_SHIFTS:
            a_s[_SCAN_PAD:, :] = av
            b_s[_SCAN_PAD:, :] = bv
            ash = a_s[pl.ds(_SCAN_PAD - s, TT), :]
            bsh = b_s[pl.ds(_SCAN_PAD - s, TT), :]
            bv = bv + av * bsh
            av = av * ash
        h = bv + av * carry[7:8, :]
        h_ref[...] = h
        hg_ref[...] = (h * _gelu(y_ref[...])).astype(bf16)
        carry[...] = h[TT - 8:, :]

    return pl.pallas_call(
        body, grid=(D // tc, t // TT), in_specs=[blk, blk, blk], out_specs=[blk, blk],
        out_shape=[_sds((t, D), f32), _sds((t, D), bf16)],
        scratch_shapes=[pltpu.VMEM((_SCAN_PAD + TT, tc), f32), pltpu.VMEM((_SCAN_PAD + TT, tc), f32),
                        pltpu.VMEM((8, tc), f32)],
        compiler_params=_cp(), name=name)(a, u, p2)


def _lru_scan_bwd(a, h, p2, dhg, name):
    t = a.shape[0]
    tc = 128
    nt = t // TT
    hb = TT // 8
    rblk = pl.BlockSpec((TT, tc), lambda j, i: (nt - 1 - i, j))

    def body(a_ref, an_ref, h_ref, hp_ref, y_ref, dhg_ref, du_ref, da_ref, dy_ref, a_s, b_s, ap, hp, carry):
        i = pl.program_id(1)

        @pl.when(i == 0)
        def _():
            carry[...] = jnp.zeros_like(carry)
            a_s[TT:, :] = jnp.ones((_SCAN_PAD, tc), f32)
            b_s[TT:, :] = jnp.zeros((_SCAN_PAD, tc), f32)

        ap[0:TT, :] = a_ref[...]
        ap[TT:, :] = jnp.where(i == 0, 0.0, an_ref[...])
        hp[0:8, :] = jnp.where(i == nt - 1, 0.0, hp_ref[...])
        hp[8:, :] = h_ref[...]
        y = y_ref[...]
        gate, gvjp = jax.vjp(_gelu, y)
        dhg_v = dhg_ref[...]
        dy_ref[...] = gvjp(dhg_v * h_ref[...])[0]
        av = ap[pl.ds(1, TT), :]
        bv = dhg_v * gate
        for s in _SCAN_SHIFTS:
            a_s[0:TT, :] = av
            b_s[0:TT, :] = bv
            ash = a_s[pl.ds(s, TT), :]
            bsh = b_s[pl.ds(s, TT), :]
            bv = bv + av * bsh
            av = av * ash
        g = bv + av * carry[0:1, :]
        du_ref[...] = g
        da_ref[...] = g * hp[pl.ds(7, TT), :]
        carry[...] = g[0:8, :]

    in_specs = [
        rblk,
        pl.BlockSpec((8, tc), lambda j, i: (jnp.minimum((nt - i) * hb, t // 8 - 1), j)),
        rblk,
        pl.BlockSpec((8, tc), lambda j, i: (jnp.maximum((nt - 1 - i) * hb - 1, 0), j)),
        rblk, rblk,
    ]
    return pl.pallas_call(
        body, grid=(D // tc, nt), in_specs=in_specs, out_specs=[rblk, rblk, rblk],
        out_shape=[_sds((t, D), f32)] * 3,
        scratch_shapes=[pltpu.VMEM((TT + _SCAN_PAD, tc), f32), pltpu.VMEM((TT + _SCAN_PAD, tc), f32),
                        pltpu.VMEM((TT + 8, tc), f32), pltpu.VMEM((TT + 8, tc), f32), pltpu.VMEM((8, tc), f32)],
        compiler_params=_cp(), name=name)(a, a, h, h, p2, dhg)


def _ffn_act_bwd(gc, up, dact, name, tc=256):
    t = gc.shape[0]
    blk = pl.BlockSpec((TT, tc), lambda i, j: (i, j))
    vblk = pl.BlockSpec((TT, tc), lambda i, j: (i, j + D_FF // tc))

    def body(gc_ref, v_ref, da_ref, dgc_ref, dv_ref):
        _, vjp = jax.vjp(lambda g, v: _silu(g) * v, gc_ref[...], v_ref[...])
        dg, dv = vjp(da_ref[...])
        dgc_ref[...] = dg
        dv_ref[...] = dv.astype(bf16)

    return pl.pallas_call(
        body, grid=(t // TT, D_FF // tc), in_specs=[blk, vblk, blk], out_specs=[blk, vblk],
        out_shape=[_sds((t, D_FF), f32), _sds((t, 2 * D_FF), bf16)], compiler_params=_cp(), name=name)(gc, up, dact)


def _lower_bounds_fwd(w):
    def body(w_ref, o0_ref, o1_ref):
        wv = w_ref[...]
        o0, o1 = _lb_rows(wv[0:1, :], wv[1:2, :])
        o0_ref[...] = o0
        o1_ref[...] = o1

    return pl.pallas_call(body, out_shape=[_sds((1, 512), f32)] * 2, name="lower_bounds_fwd")(w)


def _lb_rows(w0, w1):
    m = jnp.maximum(w0, w1)
    e0, e1 = jnp.exp(w0 - m), jnp.exp(w1 - m)
    s = e0 + e1
    p0, p1 = e0 / s, e1 / s
    return p0 - p0, (p0 + p1) - p0


def _lower_bounds_bwd(w, d0, d1):
    def body(w_ref, d0_ref, d1_ref, g0_ref, g1_ref):
        wv = w_ref[...]
        _, vjp = jax.vjp(_lb_rows, wv[0:1, :], wv[1:2, :])
        g0, g1 = vjp((d0_ref[...], d1_ref[...]))
        g0_ref[...] = g0
        g1_ref[...] = g1

    return pl.pallas_call(body, out_shape=[_sds((1, 512), f32)] * 2, name="lower_bounds_bwd")(w, d0, d1)


def _local_step(x, target, wt, pre_layer=None, post_grads=None):
    depth = 4
    res = []
    lb0, lb1 = _lower_bounds_fwd(wt["hgrn_lower_bounds"])
    lbs = [lb0, lb1]
    for layer in range(depth):
        j = layer // 2
        sv = {"x_in": x}
        deps = pre_layer(layer, "mix", x) if pre_layer else ()
        if layer == 0:
            h1 = _rmsnorm_fwd(x, wt["norm_mix"][layer], "rms_fwd", deps)
        sv["h1"] = h1
        if layer % 2 == 0:
            p = _mm_auto(h1, wt["ab_w_in"][j], "nn", f32, "mm_ab_in")
            conv = _conv_fwd(p, 0, 1536, wt["gdn_conv_w"][j], jnp.zeros((1, 1536), f32), "gdn_conv_fwd", tc=768)
            o_a, s_a = _gdn2_fwd(conv, p, wt["alog_v"][j], wt["dtb_v"][j], wt["gdn_norm"][j], "gdn_fwd")
            o_b, s_b = _hgrn_fwd(p, lbs[j], wt["hgrn_norm"][j], "hgrn_fwd")
            o = jnp.concatenate([o_a, o_b], axis=1)
            x, h2 = _mm_res_norm(o, wt["ab_w_out"][j], x, wt["norm_ffn"][layer], "mm_ab_out")
            sv.update(p=p, conv=conv, s_a=s_a, s_b=s_b, o=o)
        else:
            p2 = _mm_nn_slots(h1, wt["c_w_in"][j], f32, "mm_c_in", deps)
            xc = _conv_fwd(p2, 1, D, wt["c_conv_w"][j], wt["c_conv_b"][j], "lru_conv_fwd", tc=D)
            a, u = _lru_gates_fwd(xc, wt["c_gate_a_w"][j], wt["c_gate_x_w"][j], wt["c_gate_a_b"][j],
                                  wt["c_gate_x_b"][j], wt["c_lambda"][j], "lru_gates_fwd")
            h, hg = _lru_scan_fwd(a, u, p2, "lru_scan_fwd")
            x, h2 = _mm_res_norm(hg, wt["c_w_out"][j], x, wt["norm_ffn"][layer], "mm_c_out")
            sv.update(p2=p2, xc=xc, a=a, h=h, hg=hg)
        sv["x_mid"] = x
        deps = pre_layer(layer, "ffn", x) if pre_layer else ()
        up = _mm_nn_slots(h2, wt["ffn_w_up"][layer], f32, "mm_up", deps)
        gc, act = _conv_fwd(up, 0, D_FF, wt["ffn_conv_w"][layer], wt["ffn_conv_b"][layer], "ffn_conv_fwd",
                            tc=D_FF // 2, val=up, val_col0=2)
        if layer + 1 < depth:
            x, h1 = _mm_res_norm(act, wt["ffn_w_down"][layer], x, wt["norm_mix"][layer + 1], "mm_down")
        else:
            x = _mm_res_norm(act, wt["ffn_w_down"][layer], x, None, "mm_down_last")
        sv.update(h2=h2, up=up, gc=gc, act=act)
        res.append(sv)

    loss, dx, dxb, d_norm_final = _final_loss(x, wt["norm_final"], target, "final_loss")

    g = {k: [None] * len(v) for k, v in wt.items() if isinstance(v, list)}
    g["norm_final"] = d_norm_final
    d_lbs = [None, None]
    for layer in reversed(range(depth)):
        j = layer // 2
        sv = res[layer]
        dact = _mm_auto(dxb, wt["ffn_w_down"][layer], "nt", f32, "mm_down_dx")
        g["ffn_w_down"][layer] = _mm_auto(sv["act"], dxb, "tn", bf16, "mm_down_dw")
        dgc, dup = _ffn_act_bwd(sv["gc"], sv["up"], dact, "ffn_act_bwd", tc=D_FF // 2)
        dup, dcw, dcb = _conv_bwd(dgc, sv["up"], 0, wt["ffn_conv_w"][layer], "ffn_conv_bwd", tc=D_FF // 2, into=dup)
        g["ffn_conv_w"][layer] = dcw[:3]
        g["ffn_conv_b"][layer] = dcb
        g["ffn_w_up"][layer] = _mm_tn_slots(sv["h2"], dup, "mm_up_dw")
        deps = post_grads(layer, "ffn", g) if post_grads else ()
        dh2 = _mm_nt_slots(dup, wt["ffn_w_up"][layer], "mm_up_dx")
        dx, dxb, g["norm_ffn"][layer] = _rmsnorm_bwd(sv["x_mid"], wt["norm_ffn"][layer], dh2, dx, "rms_bwd", deps)
        if layer % 2 == 0:
            do = _mm_auto(dxb, wt["ab_w_out"][j], "nt", f32, "mm_ab_out_dx")
            g["ab_w_out"][j] = _mm_auto(sv["o"], dxb, "tn", bf16, "mm_ab_out_dw")
            dconv, dz, dba, dal, ddt, dgn = _gdn2_bwd(
                sv["conv"], sv["p"], wt["alog_v"][j], wt["dtb_v"][j], wt["gdn_norm"][j], sv["s_a"], do, "gdn_bwd")
            g["alog_v"][j], g["dtb_v"][j], g["gdn_norm"][j] = dal, ddt, dgn
            dqkv, dcw, _ = _conv_bwd(dconv, sv["p"], 0, wt["gdn_conv_w"][j], "gdn_conv_bwd", tc=768)
            g["gdn_conv_w"][j] = dcw[:4]
            dqb, dfb, dib, dgb, dlb, dhn = _hgrn_bwd(sv["p"], lbs[j], wt["hgrn_norm"][j], sv["s_b"], do, "hgrn_bwd")
            g["hgrn_norm"][j] = dhn
            d_lbs[j] = jnp.sum(dlb, axis=0)
            dp = jnp.concatenate([dqkv] + [t_.astype(bf16) for t_ in (dz, dqb, dfb, dib, dgb, dba)], axis=1)
            g["ab_w_in"][j] = _mm_auto(sv["h1"], dp, "tn", bf16, "mm_ab_in_dw")
            dh1 = _mm_auto(dp, wt["ab_w_in"][j], "nt", f32, "mm_ab_in_dx")
        else:
            dhg = _mm_auto(dxb, wt["c_w_out"][j], "nt", f32, "mm_c_out_dx")
            g["c_w_out"][j] = _mm_auto(sv["hg"], dxb, "tn", bf16, "mm_c_out_dw")
            du, da, dy = _lru_scan_bwd(sv["a"], sv["h"], sv["p2"], dhg, "lru_scan_bwd")
            dxc, dwa, dwx, dba_, dbx_, dlam = _lru_gates_bwd(
                sv["xc"], wt["c_gate_a_w"][j], wt["c_gate_x_w"][j], wt["c_gate_a_b"][j], wt["c_gate_x_b"][j],
                wt["c_lambda"][j], da, du, "lru_gates_bwd")
            g["c_gate_a_w"][j], g["c_gate_x_w"][j] = dwa, dwx
            g["c_gate_a_b"][j], g["c_gate_x_b"][j], g["c_lambda"][j] = dba_, dbx_, dlam
            dxbr, dcw, dcb = _conv_bwd(dxc, sv["p2"], 1, wt["c_conv_w"][j], "lru_conv_bwd", tc=D)
            g["c_conv_w"][j] = dcw[:4]
            g["c_conv_b"][j] = dcb
            dp2 = jnp.concatenate([dy.astype(bf16), dxbr], axis=1)
            g["c_w_in"][j] = _mm_tn_slots(sv["h1"], dp2, "mm_c_in_dw")
            dh1 = _mm_nt_slots(dp2, wt["c_w_in"][j], "mm_c_in_dx")
        deps = post_grads(layer, "mix", g) if post_grads else ()
        dx, dxb, g["norm_mix"][layer] = _rmsnorm_bwd(sv["x_in"], wt["norm_mix"][layer], dh1, dx, "rms_bwd", deps)
    g0, g1 = _lower_bounds_bwd(wt["hgrn_lower_bounds"], d_lbs[0], d_lbs[1])
    g["hgrn_lower_bounds"] = jnp.concatenate([g0, g1], axis=0)
    return loss, dx, g


def _ab_in_to_compute(w):
    return jnp.concatenate([w[:, :2048], w[:, 2056:4104], w[:, 2048:2056], jnp.zeros((D, 120), w.dtype)], axis=1)


def _ab_in_from_compute(g):
    return jnp.concatenate([g[:, :2048], g[:, 4096:4104], g[:, 2048:4096]], axis=1)


def _lane_vec(v4):
    return jnp.zeros((1, HEAD), f32).at[0, NH:2 * NH].set(v4)


def _layout_weights(fw):
    wt = {}
    wt["norm_mix"] = [fw["norm_mix"][l][None] for l in range(4)]
    wt["norm_ffn"] = [fw["norm_ffn"][l][None] for l in range(4)]
    wt["norm_final"] = fw["norm_final"][None]
    wt["gdn_conv_w"] = [fw["gdn_conv_w"][j] for j in range(2)]
    wt["alog_v"] = [_lane_vec(fw["gdn_a_log"][j]) for j in range(2)]
    wt["dtb_v"] = [_lane_vec(fw["gdn_dt_bias"][j]) for j in range(2)]
    wt["gdn_norm"] = [fw["gdn_norm"][j][None] for j in range(2)]
    wt["hgrn_lower_bounds"] = fw["hgrn_lower_bounds"]
    wt["hgrn_norm"] = [fw["hgrn_norm"][j][None] for j in range(2)]
    wt["c_conv_w"] = [fw["c_conv_w"][j] for j in range(2)]
    for k in ("c_conv_b", "c_gate_a_b", "c_gate_x_b", "c_lambda"):
        wt[k] = [fw[k][j][None] for j in range(2)]
    wt["ffn_conv_w"] = [fw["ffn_conv_w"][l] for l in range(4)]
    wt["ffn_conv_b"] = [fw["ffn_conv_b"][l][None] for l in range(4)]
    if "ab_w_in" in fw:
        wt["ab_w_in"] = [_ab_in_to_compute(fw["ab_w_in"][j].astype(bf16)) for j in range(2)]
        for k in ("ab_w_out", "c_w_out"):
            wt[k] = [fw[k][j].astype(bf16) for j in range(2)]
        wt["c_w_in"] = [_to_slots(fw["c_w_in"][j].astype(bf16)) for j in range(2)]
        for k in ("c_gate_a_w", "c_gate_x_w"):
            wt[k] = [fw[k][j].astype(f32) for j in range(2)]
        wt["ffn_w_up"] = [_to_slots(fw["ffn_w_up"][l].astype(bf16)) for l in range(4)]
        wt["ffn_w_down"] = [fw["ffn_w_down"][l].astype(bf16) for l in range(4)]
    return wt


SLOT_MAJOR = ("c_w_in", "ffn_w_up")


def _to_slots(wfull):
    k, n = wfull.shape
    return wfull.reshape(k, N_SLOTS, n // N_SLOTS).transpose(1, 0, 2)


def _layer_full(name, slots):
    kind = BIG[name]
    if name in SLOT_MAJOR:
        return slots
    if kind == "col":
        return _ab_in_to_compute(slots.transpose(1, 0, 2).reshape(slots.shape[1], -1))
    if kind == "row":
        return slots.reshape(-1, slots.shape[2])
    return slots.reshape(4, NH, LRU_BLOCK // 4, LRU_BLOCK).transpose(1, 0, 2, 3).reshape(NH, LRU_BLOCK, LRU_BLOCK).astype(f32)


def _layer_slots(name, g):
    kind = BIG[name]
    if name in SLOT_MAJOR:
        return g
    if kind == "col":
        g = _ab_in_from_compute(g)
        r, cdim = g.shape
        return g.reshape(r, 4, cdim // 4).transpose(1, 0, 2).astype(bf16)
    if kind == "row":
        r, cdim = g.shape
        return g.reshape(4, r // 4, cdim).astype(bf16)
    return g.reshape(NH, 4, LRU_BLOCK // 4, LRU_BLOCK).transpose(1, 0, 2, 3).reshape(4, LRU_BLOCK, LRU_BLOCK).astype(bf16)


def _unlayout_grads(g):
    out = {}
    for k in ("norm_mix", "norm_ffn", "gdn_norm", "hgrn_norm", "c_conv_b", "c_gate_a_b", "c_gate_x_b", "c_lambda",
              "ffn_conv_b"):
        out[k] = jnp.concatenate(g[k], axis=0)
    out["norm_final"] = g["norm_final"][0]
    out["ab_w_in"] = jnp.stack([_ab_in_from_compute(t) for t in g["ab_w_in"]])
    out["gdn_a_log"] = jnp.stack([t[0, NH:2 * NH] for t in g["alog_v"]])
    out["gdn_dt_bias"] = jnp.stack([t[0, NH:2 * NH] for t in g["dtb_v"]])
    out["hgrn_lower_bounds"] = g["hgrn_lower_bounds"]
    for k in ("gdn_conv_w", "ab_w_out", "c_conv_w", "c_gate_a_w", "c_gate_x_w", "c_w_out", "ffn_conv_w", "ffn_w_down"):
        out[k] = jnp.stack(g[k])
    for k in SLOT_MAJOR:
        out[k] = jnp.stack([t.transpose(1, 0, 2).reshape(t.shape[1], -1) for t in g[k]])
    return out


MESH = pl.DeviceIdType.MESH
ANY = pl.BlockSpec(memory_space=pl.ANY)
CHIP_RELATIONS = ((1, 0), (0, 1), (1, 1))
N_CHIPS = 4


def _coords():
    return lax.axis_index("x"), lax.axis_index("y"), lax.axis_index("c")


def _flip(v, f):
    return 1 - v if f else v


def _half_rows(c, a, align):
    return pl.ds(pl.multiple_of(c * (a // 2), align), a // 2)


def _all_gather_chips(shards, name):
    n = len(shards)
    shapes = [s.shape for s in shards]

    def body(*refs):
        ins, outs = refs[:n], refs[n:2 * n]
        send_sems, recv_sems = refs[2 * n:]
        x, y, c = _coords()
        me = 2 * x + y
        sibling = (x, y, 1 - c)
        started = []
        for p in range(n):
            cp = pltpu.make_async_remote_copy(
                src_ref=ins[p], dst_ref=outs[p].at[me],
                send_sem=send_sems.at[p, 6], recv_sem=recv_sems.at[p, 6],
                device_id=sibling, device_id_type=MESH)
            cp.start()
            started.append(cp)
        for p in range(n):
            mine = _half_rows(c, shapes[p][0], 16)
            for r, (fx, fy) in enumerate(CHIP_RELATIONS):
                cp = pltpu.make_async_remote_copy(
                    src_ref=ins[p].at[mine], dst_ref=outs[p].at[me, mine],
                    send_sem=send_sems.at[p, r], recv_sem=recv_sems.at[p, r],
                    device_id=(_flip(x, fx), _flip(y, fy), c), device_id_type=MESH)
                cp.start()
                started.append(cp)
        for r, (fx, fy) in enumerate(CHIP_RELATIONS):
            k = 2 * _flip(x, fx) + _flip(y, fy)
            for p in range(n):
                mine = _half_rows(c, shapes[p][0], 16)
                pltpu.make_async_remote_copy(
                    src_ref=ins[p].at[mine], dst_ref=outs[p].at[k, mine],
                    send_sem=send_sems.at[p, r], recv_sem=recv_sems.at[p, r],
                    device_id=(_flip(x, fx), _flip(y, fy), c), device_id_type=MESH).wait_recv()
                fwd = pltpu.make_async_remote_copy(
                    src_ref=outs[p].at[k, mine], dst_ref=outs[p].at[k, mine],
                    send_sem=send_sems.at[p, 3 + r], recv_sem=recv_sems.at[p, 3 + r],
                    device_id=sibling, device_id_type=MESH)
                fwd.start()
                started.append(fwd)
        for r, (fx, fy) in enumerate(CHIP_RELATIONS):
            k = 2 * _flip(x, fx) + _flip(y, fy)
            for p in range(n):
                theirs = _half_rows(1 - c, shapes[p][0], 16)
                pltpu.make_async_remote_copy(
                    src_ref=outs[p].at[k, theirs], dst_ref=outs[p].at[k, theirs],
                    send_sem=send_sems.at[p, 3 + r], recv_sem=recv_sems.at[p, 3 + r],
                    device_id=sibling, device_id_type=MESH).wait_recv()
        for p in range(n):
            pltpu.make_async_remote_copy(
                src_ref=ins[p], dst_ref=outs[p].at[me],
                send_sem=send_sems.at[p, 6], recv_sem=recv_sems.at[p, 6],
                device_id=sibling, device_id_type=MESH).wait_recv()
        for cp in started:
            cp.wait_send()

    return pl.pallas_call(
        body, in_specs=[ANY] * n, out_specs=[ANY] * n,
        out_shape=[_sds((N_CHIPS,) + s.shape, s.dtype) for s in shards],
        scratch_shapes=[pltpu.SemaphoreType.DMA((n, 7)), pltpu.SemaphoreType.DMA((n, 7))],
        name=name)(*shards)


def _sibling_send_other_half(gs, name):
    n = len(gs)
    shapes = [g.shape for g in gs]

    def body(*refs):
        ins, outs = refs[:n], refs[n:2 * n]
        send_sems, recv_sems = refs[2 * n:]
        x, y, c = _coords()
        cps = []
        for p in range(n):
            theirs = _half_rows(1 - c, shapes[p][1], 8)
            cp = pltpu.make_async_remote_copy(
                src_ref=ins[p].at[:, theirs], dst_ref=outs[p],
                send_sem=send_sems.at[p], recv_sem=recv_sems.at[p],
                device_id=(x, y, 1 - c), device_id_type=MESH)
            cp.start()
            cps.append(cp)
        for cp in cps:
            cp.wait()

    return pl.pallas_call(
        body, in_specs=[ANY] * n, out_specs=[ANY] * n,
        out_shape=[_sds((s[0], s[1] // 2, s[2]), f32) for s in shapes],
        scratch_shapes=[pltpu.SemaphoreType.DMA((n,)), pltpu.SemaphoreType.DMA((n,))],
        name=name)(*gs)


def _chip_exchange(ps, name):
    n = len(ps)

    def body(*refs):
        ins, outs = refs[:n], refs[n:2 * n]
        send_sems, recv_sems = refs[2 * n:]
        x, y, c = _coords()
        cps = []
        for p in range(n):
            for r, (fx, fy) in enumerate(CHIP_RELATIONS):
                k = 2 * _flip(x, fx) + _flip(y, fy)
                cp = pltpu.make_async_remote_copy(
                    src_ref=ins[p].at[k], dst_ref=outs[p].at[r],
                    send_sem=send_sems.at[p, r], recv_sem=recv_sems.at[p, r],
                    device_id=(_flip(x, fx), _flip(y, fy), c), device_id_type=MESH)
                cp.start()
                cps.append(cp)
        for cp in cps:
            cp.wait_recv()
        for cp in cps:
            cp.wait_send()

    return pl.pallas_call(
        body, in_specs=[ANY] * n, out_specs=[ANY] * n,
        out_shape=[_sds((3,) + p.shape[1:], p.dtype) for p in ps],
        scratch_shapes=[pltpu.SemaphoreType.DMA((n, 3)), pltpu.SemaphoreType.DMA((n, 3))],
        name=name)(*ps)


def _sibling_fill_other_half(fs, name):
    n = len(fs)
    shapes = [f.shape for f in fs]

    def body(*refs):
        ins, outs = refs[:n], refs[n:2 * n]
        send_sems, recv_sems = refs[2 * n:]
        x, y, c = _coords()
        cps = []
        for p in range(n):
            mine = _half_rows(c, shapes[p][0], 8)
            cp = pltpu.make_async_remote_copy(
                src_ref=ins[p].at[mine], dst_ref=outs[p].at[mine],
                send_sem=send_sems.at[p], recv_sem=recv_sems.at[p],
                device_id=(x, y, 1 - c), device_id_type=MESH)
            cp.start()
            cps.append(cp)
        for p in range(n):
            theirs = _half_rows(1 - c, shapes[p][0], 8)
            pltpu.make_async_remote_copy(
                src_ref=ins[p].at[theirs], dst_ref=outs[p].at[theirs],
                send_sem=send_sems.at[p], recv_sem=recv_sems.at[p],
                device_id=(x, y, 1 - c), device_id_type=MESH).wait_recv()
        for cp in cps:
            cp.wait_send()

    return pl.pallas_call(
        body, in_specs=[ANY] * n, out_specs=[ANY] * n,
        out_shape=[_sds(f.shape, f.dtype) for f in fs],
        input_output_aliases={p: p for p in range(n)},
        scratch_shapes=[pltpu.SemaphoreType.DMA((n,)), pltpu.SemaphoreType.DMA((n,))],
        name=name)(*fs)


def _sibling_all_gather(ss, name):
    n = len(ss)
    shapes = [s.shape for s in ss]

    def body(*refs):
        ins, outs = refs[:n], refs[n:2 * n]
        send_sems, recv_sems, loc_sems = refs[2 * n:]
        x, y, c = _coords()
        locs, cps = [], []
        for p in range(n):
            mine = _half_rows(c, 2 * shapes[p][0], 8)
            loc = pltpu.make_async_copy(ins[p], outs[p].at[mine], loc_sems.at[p])
            loc.start()
            locs.append(loc)
            cp = pltpu.make_async_remote_copy(
                src_ref=ins[p], dst_ref=outs[p].at[mine],
                send_sem=send_sems.at[p], recv_sem=recv_sems.at[p],
                device_id=(x, y, 1 - c), device_id_type=MESH)
            cp.start()
            cps.append(cp)
        for p, cp in enumerate(cps):
            theirs = _half_rows(1 - c, 2 * shapes[p][0], 8)
            pltpu.make_async_remote_copy(
                src_ref=ins[p], dst_ref=outs[p].at[theirs],
                send_sem=send_sems.at[p], recv_sem=recv_sems.at[p],
                device_id=(x, y, 1 - c), device_id_type=MESH).wait_recv()
        for cp in cps:
            cp.wait_send()
        for loc in locs:
            loc.wait()

    return pl.pallas_call(
        body, in_specs=[ANY] * n, out_specs=[ANY] * n,
        out_shape=[_sds((2 * s[0], s[1]), f32) for s in shapes],
        scratch_shapes=[pltpu.SemaphoreType.DMA((n,)), pltpu.SemaphoreType.DMA((n,)), pltpu.SemaphoreType.DMA((n,))],
        name=name)(*ss)


N_DEV = 8


def _all_reduce_small(pack, name):
    rows = pack.shape[0]

    def body(in_ref, sum_ref, all_ref, send_sems, recv_sems):
        x, y, c = _coords()
        me = 4 * x + 2 * y + c
        all_ref[me] = in_ref[...]
        cps = []
        for r in range(1, N_DEV):
            fx, fy, fc = (r >> 2) & 1, (r >> 1) & 1, r & 1
            cp = pltpu.make_async_remote_copy(
                src_ref=in_ref, dst_ref=all_ref.at[me],
                send_sem=send_sems.at[r], recv_sem=recv_sems.at[r],
                device_id=(_flip(x, fx), _flip(y, fy), _flip(c, fc)), device_id_type=MESH)
            cp.start()
            cps.append(cp)
        for r in range(1, N_DEV):
            fx, fy, fc = (r >> 2) & 1, (r >> 1) & 1, r & 1
            peer = 4 * _flip(x, fx) + 2 * _flip(y, fy) + _flip(c, fc)
            pltpu.make_async_remote_copy(
                src_ref=in_ref, dst_ref=all_ref.at[peer],
                send_sem=send_sems.at[r], recv_sem=recv_sems.at[r],
                device_id=(x, y, c), device_id_type=MESH).wait_recv()
        for cp in cps:
            cp.wait_send()
        acc = all_ref[0]
        for d in range(1, N_DEV):
            acc = acc + all_ref[d]
        sum_ref[...] = acc

    vm = pl.BlockSpec(memory_space=pltpu.VMEM)
    return pl.pallas_call(
        body, in_specs=[vm], out_specs=[vm, vm],
        out_shape=[_sds((rows, 128), f32), _sds((N_DEV, rows, 128), f32)],
        scratch_shapes=[pltpu.SemaphoreType.DMA((N_DEV,)), pltpu.SemaphoreType.DMA((N_DEV,))],
        name=name)(pack)[0]


ROWS_EW = 128


def _add_own_half(g, rs, c_arr, name):
    s, a, b = g.shape
    nrt = (a // 2) // ROWS_EW

    def body(c_ref, g_ref, r_ref, o_ref):
        o_ref[...] = (g_ref[...] + r_ref[...]).astype(bf16)

    grid_spec = pltpu.PrefetchScalarGridSpec(
        num_scalar_prefetch=1, grid=(s, nrt),
        in_specs=[pl.BlockSpec((1, ROWS_EW, b), lambda k, i, c_ref: (k, c_ref[0] * nrt + i, 0)),
                  pl.BlockSpec((1, ROWS_EW, b), lambda k, i, c_ref: (k, i, 0))],
        out_specs=pl.BlockSpec((1, ROWS_EW, b), lambda k, i, c_ref: (k, i, 0)))
    return pl.pallas_call(body, grid_spec=grid_spec, out_shape=_sds((s, a // 2, b), bf16),
                          compiler_params=_cp(), name=name)(c_arr, g, rs)


def _sum_chips(g, rs, rc, ids, name):
    _, r, b = rc.shape
    nrt = r // ROWS_EW

    def body(ids_ref, g_ref, s_ref, r_ref, o_ref):
        own = g_ref[0] + s_ref[0]
        o_ref[...] = ((own + r_ref[0].astype(f32)) + r_ref[1].astype(f32)) + r_ref[2].astype(f32)

    grid_spec = pltpu.PrefetchScalarGridSpec(
        num_scalar_prefetch=1, grid=(nrt,),
        in_specs=[pl.BlockSpec((1, ROWS_EW, b), lambda i, ids_ref: (ids_ref[0], ids_ref[1] * nrt + i, 0)),
                  pl.BlockSpec((1, ROWS_EW, b), lambda i, ids_ref: (ids_ref[0], i, 0)),
                  pl.BlockSpec((3, ROWS_EW, b), lambda i, ids_ref: (0, i, 0))],
        out_specs=pl.BlockSpec((ROWS_EW, b), lambda i, ids_ref: (ids_ref[1] * nrt + i, 0)))
    return pl.pallas_call(body, grid_spec=grid_spec, out_shape=_sds((2 * r, b), f32),
                          compiler_params=_cp(), name=name)(ids, g, rs, rc)


def _adamw_math(w, g, m, v):
    m = ADAM_B1 * m + (1.0 - ADAM_B1) * g
    v = ADAM_B2 * v + (1.0 - ADAM_B2) * (g * g)
    m_hat = m / (1.0 - ADAM_B1 ** ADAM_STEP)
    v_hat = v / (1.0 - ADAM_B2 ** ADAM_STEP)
    delta = -ADAM_LR * (m_hat / (jnp.sqrt(v_hat) + ADAM_EPS) + ADAM_WD * w)
    return delta, m, v


def _adamw_big(w, g, m, v, name):
    a, b = w.shape

    def body(w_ref, g_ref, m_ref, v_ref, go_ref, d_ref, mo_ref, vo_ref):
        gv = g_ref[...]
        d, mn, vn = _adamw_math(w_ref[...], gv, m_ref[...], v_ref[...])
        go_ref[...] = gv
        d_ref[...] = d
        mo_ref[...] = mn
        vo_ref[...] = vn

    blk = pl.BlockSpec((ROWS_EW, b), lambda i: (i, 0))
    return pl.pallas_call(body, grid=(a // ROWS_EW,), in_specs=[blk] * 4, out_specs=[blk] * 4,
                          out_shape=[_sds((a, b), f32)] * 4, compiler_params=_cp(), name=name)(w, g, m, v)


def _adamw_small(ws, gs, ms, vs, name):
    n = len(ws)

    def body(*refs):
        w_r, g_r, m_r, v_r = refs[:n], refs[n:2 * n], refs[2 * n:3 * n], refs[3 * n:4 * n]
        go_r, d_r, mo_r, vo_r = refs[4 * n:5 * n], refs[5 * n:6 * n], refs[6 * n:7 * n], refs[7 * n:8 * n]
        for p in range(n):
            gv = g_r[p][...]
            d, mn, vn = _adamw_math(w_r[p][...], gv, m_r[p][...], v_r[p][...])
            go_r[p][...] = gv
            d_r[p][...] = d
            mo_r[p][...] = mn
            vo_r[p][...] = vn

    vm = pl.BlockSpec(memory_space=pltpu.VMEM)
    shp = [_sds(w.shape, f32) for w in ws]
    res = pl.pallas_call(body, in_specs=[vm] * (4 * n), out_specs=[vm] * (4 * n), out_shape=shp * 4,
                         name=name)(*ws, *gs, *ms, *vs)
    return res[:n], res[n:2 * n], res[2 * n:3 * n], res[3 * n:]


HBM = pl.BlockSpec(memory_space=pltpu.HBM)
SEM = pl.BlockSpec(memory_space=pltpu.SEMAPHORE)
EFFECT = pltpu.SideEffectType.DATAFLOW_SIDE_EFFECTING
N_REL = 8


def _rel(r):
    return (r >> 2) & 1, (r >> 1) & 1, r & 1


def _gather_copies(ins, lands, send_sems, recv_sems, shapes):
    x, y, c = _coords()
    me = 2 * x + y
    sends, recvs = [], []
    for p in range(len(ins)):
        for r in range(1, N_REL):
            fx, fy, fc = _rel(r)
            peer = (_flip(x, fx), _flip(y, fy), _flip(c, fc))
            if fx == 0 and fy == 0:
                src, dst, got = ins[p], lands[p].at[me], lands[p].at[me]
            else:
                mine = _half_rows(c, shapes[p][0], 16)
                theirs = _half_rows(_flip(c, fc), shapes[p][0], 16)
                src, dst = ins[p].at[mine], lands[p].at[me, mine]
                got = lands[p].at[2 * peer[0] + peer[1], theirs]
            sems = dict(send_sem=send_sems.at[p * N_REL + r], recv_sem=recv_sems.at[p * N_REL + r], device_id=peer,
                        device_id_type=MESH)
            sends.append(pltpu.make_async_remote_copy(src_ref=src, dst_ref=dst, **sems))
            recvs.append(pltpu.make_async_remote_copy(src_ref=src, dst_ref=got, **sems))
    return sends, recvs


def _scatter_copies(ins, lands, send_sems, recv_sems, shapes):
    x, y, c = _coords()
    sends, recvs = [], []
    for p in range(len(ins)):
        for r in range(1, N_REL):
            fx, fy, fc = _rel(r)
            peer = (_flip(x, fx), _flip(y, fy), _flip(c, fc))
            theirs = _half_rows(peer[2], shapes[p][1], 16)
            sems = dict(send_sem=send_sems.at[p * N_REL + r], recv_sem=recv_sems.at[p * N_REL + r], device_id=peer,
                        device_id_type=MESH)
            cp = pltpu.make_async_remote_copy(src_ref=ins[p].at[2 * peer[0] + peer[1], theirs], dst_ref=lands[p].at[r], **sems)
            sends.append(cp)
            recvs.append(cp)
    return sends, recvs


def _split_start(copies_fn, ins, land_shapes, name, after=()):
    n = len(ins)
    shapes = [a.shape for a in ins]

    def body(*refs):
        in_refs, land_refs = refs[:n], refs[n:2 * n]
        send_sems, recv_sems = refs[2 * n + len(after)], refs[2 * n + len(after) + 1]
        token = refs[-1]
        sends, _ = copies_fn(in_refs, land_refs, send_sems, recv_sems, shapes)
        for cp in sends:
            cp.start()
        token[...] = jnp.zeros_like(token)

    lands = [lax.empty(s.shape, s.dtype) for s in land_shapes]
    na = len(after)
    res = pl.pallas_call(
        body, name=name,
        out_shape=(pltpu.SemaphoreType.DMA((n * N_REL,)), pltpu.SemaphoreType.DMA((n * N_REL,)))
        + tuple(pltpu.HBM(a.shape, a.dtype) for a in ins) + tuple(pltpu.HBM(s.shape, s.dtype) for s in land_shapes)
        + (_sds((8, 128), f32),),
        in_specs=[HBM] * (2 * n) + [pl.BlockSpec(memory_space=pl.ANY)] * na,
        out_specs=(SEM, SEM) + (HBM,) * (2 * n) + (pl.BlockSpec(memory_space=pltpu.VMEM),),
        input_output_aliases={i: 2 + i for i in range(2 * n)},
        compiler_params=pltpu.CompilerParams(has_side_effects=EFFECT),
    )(*[pltpu.with_memory_space_constraint(a, pltpu.HBM) for a in ins],
      *[pltpu.with_memory_space_constraint(a, pltpu.HBM) for a in lands], *after)
    return dict(sems=res[:2], ins=res[2:2 + n], lands=res[2 + n:2 + 2 * n], token=res[-1], shapes=shapes)


def _split_wait(copies_fn, started, after, name):
    n = len(started["ins"])
    shapes = started["shapes"]
    na = len(after)

    def body(*refs):
        in_refs, land_refs = refs[:n], refs[n:2 * n]
        send_sems, recv_sems = refs[2 * n], refs[2 * n + 1]
        sends, recvs = copies_fn(in_refs, land_refs, send_sems, recv_sems, shapes)
        for cp in sends:
            cp.wait_send()
        for cp in recvs:
            cp.wait_recv()

    arrs = list(started["ins"]) + list(started["lands"])
    res = pl.pallas_call(
        body, name=name,
        out_shape=tuple(pltpu.HBM(a.shape, a.dtype) for a in arrs),
        in_specs=[HBM] * (2 * n) + [SEM, SEM] + [pl.BlockSpec(memory_space=pl.ANY)] * na,
        out_specs=(HBM,) * (2 * n), input_output_aliases={i: i for i in range(2 * n)},
        compiler_params=pltpu.CompilerParams(has_side_effects=EFFECT),
    )(*arrs, *started["sems"], *after)
    return res[:n], res[n:]


def _sum_pieces(gb, land, ids, f_prev, blk, nblk, name):
    _, a, b = gb.shape
    rows = _tile(a // 2, (ROWS_EW, 176, 64, 32, 16))
    nrt = (a // 2) // rows

    def body(ids_ref, g_ref, l_ref, *rest):
        o_ref = rest[-1]
        acc = g_ref[0].astype(f32)
        for r in range(1, N_REL):
            acc = acc + l_ref[r].astype(f32)
        o_ref[...] = acc

    in_specs = [pl.BlockSpec((1, rows, b), lambda i, ids_ref: (ids_ref[0], ids_ref[1] * nrt + i, 0)),
                pl.BlockSpec((N_REL, rows, b), lambda i, ids_ref: (0, i, 0))]
    args = [ids, gb, land]
    aliases = {}
    if f_prev is not None:
        in_specs.append(pl.BlockSpec(memory_space=pl.ANY))
        args.append(f_prev)
        aliases = {3: 0}
    grid_spec = pltpu.PrefetchScalarGridSpec(
        num_scalar_prefetch=1, grid=(nrt,), in_specs=in_specs,
        out_specs=pl.BlockSpec((rows, b), lambda i, ids_ref: ((2 * blk + ids_ref[1]) * nrt + i, 0)))
    return pl.pallas_call(body, grid_spec=grid_spec, out_shape=_sds((nblk * a, b), f32),
                          input_output_aliases=aliases, compiler_params=_cp(), name=name)(*args)


def _sibling_fill_blocks(fs, nblks, name):
    n = len(fs)
    shapes = [f.shape for f in fs]

    def body(*refs):
        ins, outs = refs[:n], refs[n:2 * n]
        send_sems, recv_sems = refs[2 * n:]
        x, y, c = _coords()
        cps, waits = [], []
        k = 0
        for p in range(n):
            a = shapes[p][0] // nblks[p]
            for bi in range(nblks[p]):
                mine = pl.ds(pl.multiple_of(bi * a + c * (a // 2), 8), a // 2)
                theirs = pl.ds(pl.multiple_of(bi * a + (1 - c) * (a // 2), 8), a // 2)
                sems = dict(send_sem=send_sems.at[k], recv_sem=recv_sems.at[k], device_id=(x, y, 1 - c), device_id_type=MESH)
                cp = pltpu.make_async_remote_copy(src_ref=ins[p].at[mine], dst_ref=outs[p].at[mine], **sems)
                cp.start()
                cps.append(cp)
                waits.append(pltpu.make_async_remote_copy(src_ref=ins[p].at[theirs], dst_ref=outs[p].at[theirs], **sems))
                k += 1
        for wt_ in waits:
            wt_.wait_recv()
        for cp in cps:
            cp.wait_send()

    total = sum(nblks)
    return pl.pallas_call(
        body, in_specs=[ANY] * n, out_specs=[ANY] * n,
        out_shape=[_sds(f.shape, f.dtype) for f in fs],
        input_output_aliases={p: p for p in range(n)},
        scratch_shapes=[pltpu.SemaphoreType.DMA((total,)), pltpu.SemaphoreType.DMA((total,))],
        name=name)(*fs)


WEIGHTS = ["norm_mix", "norm_ffn", "norm_final", "ab_w_in", "gdn_conv_w", "gdn_a_log", "gdn_dt_bias", "gdn_norm",
           "hgrn_lower_bounds", "hgrn_norm", "ab_w_out", "c_w_in", "c_conv_w", "c_conv_b", "c_gate_a_w", "c_gate_a_b",
           "c_gate_x_w", "c_gate_x_b", "c_lambda", "c_w_out", "ffn_w_up", "ffn_conv_w", "ffn_conv_b", "ffn_w_down"]
BIG = {"ab_w_in": "col", "ab_w_out": "row", "c_w_in": "col", "c_gate_a_w": "gate", "c_gate_x_w": "gate",
       "c_w_out": "row", "ffn_w_up": "col", "ffn_w_down": "row"}
SMALL_SHARDED = ["gdn_conv_w", "c_conv_w", "c_conv_b", "c_gate_a_b", "c_gate_x_b", "c_lambda", "ffn_conv_w"]
SMALL = [n for n in WEIGHTS if n not in BIG]
FULL_SHAPES = {
    "norm_mix": (4, 1024), "norm_ffn": (4, 1024), "norm_final": (1024,), "ab_w_in": (2, 1024, 4104),
    "gdn_conv_w": (2, 4, 1536), "gdn_a_log": (2, 4), "gdn_dt_bias": (2, 4), "gdn_norm": (2, 128),
    "hgrn_lower_bounds": (2, 512), "hgrn_norm": (2, 128), "ab_w_out": (2, 1024, 1024), "c_w_in": (2, 1024, 2048),
    "c_conv_w": (2, 4, 1024), "c_conv_b": (2, 1024), "c_gate_a_w": (2, 4, 256, 256), "c_gate_a_b": (2, 1024),
    "c_gate_x_w": (2, 4, 256, 256), "c_gate_x_b": (2, 1024), "c_lambda": (2, 1024), "c_w_out": (2, 1024, 1024),
    "ffn_w_up": (4, 1024, 5632), "ffn_conv_w": (4, 3, 2816), "ffn_conv_b": (4, 2816), "ffn_w_down": (4, 2816, 1024)}


def _shard2d(name, shard):
    return shard.reshape(-1, shard.shape[-1])


def _full_from_slots(name, slots):
    full = FULL_SHAPES[name]
    kind = BIG[name]
    if kind == "col":
        l, r, cdim = full
        return slots.reshape(4, l, r, cdim // 4).transpose(1, 2, 0, 3).reshape(full)
    if kind == "row":
        l, r, cdim = full
        return slots.reshape(4, l, r // 4, cdim).transpose(1, 0, 2, 3).reshape(full)
    l, h, r, cdim = full
    return slots.reshape(4, l, h, r // 4, cdim).transpose(1, 2, 0, 3, 4).reshape(full)


def _slots_from_full(name, g):
    full = FULL_SHAPES[name]
    kind = BIG[name]
    if kind == "col":
        l, r, cdim = full
        return g.reshape(l, r, 4, cdim // 4).transpose(2, 0, 1, 3).reshape(4, l * r, cdim // 4)
    if kind == "row":
        l, r, cdim = full
        return g.reshape(l, 4, r // 4, cdim).transpose(1, 0, 2, 3).reshape(4, l * (r // 4), cdim)
    l, h, r, cdim = full
    return g.reshape(l, h, 4, r // 4, cdim).transpose(2, 0, 1, 3, 4).reshape(4, l * h * (r // 4), cdim)


def _pack_rows(arrs, rows):
    flat = jnp.concatenate([a.reshape(-1) for a in arrs])
    return jnp.pad(flat, (0, rows * 128 - flat.shape[0])).reshape(rows, 128)


def _unpack_rows(pack, shapes):
    flat = pack.reshape(-1)
    out, off = [], 0
    for s in shapes:
        size = 1
        for d in s:
            size *= d
        out.append(flat[off:off + size].reshape(s))
        off += size
    return out


def kernel(x, norm_mix, norm_ffn, norm_final, ab_w_in, gdn_conv_w, gdn_a_log, gdn_dt_bias, gdn_norm, hgrn_lower_bounds, hgrn_norm, ab_w_out, c_w_in, c_conv_w, c_conv_b, c_gate_a_w, c_gate_a_b, c_gate_x_w, c_gate_x_b, c_lambda, c_w_out, ffn_w_up, ffn_conv_w, ffn_conv_b, ffn_w_down, loss_target, m_norm_mix, m_norm_ffn, m_norm_final, m_ab_w_in, m_gdn_conv_w, m_gdn_a_log, m_gdn_dt_bias, m_gdn_norm, m_hgrn_lower_bounds, m_hgrn_norm, m_ab_w_out, m_c_w_in, m_c_conv_w, m_c_conv_b, m_c_gate_a_w, m_c_gate_a_b, m_c_gate_x_w, m_c_gate_x_b, m_c_lambda, m_c_w_out, m_ffn_w_up, m_ffn_conv_w, m_ffn_conv_b, m_ffn_w_down, v_norm_mix, v_norm_ffn, v_norm_final, v_ab_w_in, v_gdn_conv_w, v_gdn_a_log, v_gdn_dt_bias, v_gdn_norm, v_hgrn_lower_bounds, v_hgrn_norm, v_ab_w_out, v_c_w_in, v_c_conv_w, v_c_conv_b, v_c_gate_a_w, v_c_gate_a_b, v_c_gate_x_w, v_c_gate_x_b, v_c_lambda, v_c_w_out, v_ffn_w_up, v_ffn_conv_w, v_ffn_conv_b, v_ffn_w_down):
    w = dict(zip(WEIGHTS, (norm_mix, norm_ffn, norm_final, ab_w_in, gdn_conv_w, gdn_a_log, gdn_dt_bias, gdn_norm, hgrn_lower_bounds, hgrn_norm, ab_w_out, c_w_in, c_conv_w, c_conv_b, c_gate_a_w, c_gate_a_b, c_gate_x_w, c_gate_x_b, c_lambda, c_w_out, ffn_w_up, ffn_conv_w, ffn_conv_b, ffn_w_down)))
    m = dict(zip(WEIGHTS, (m_norm_mix, m_norm_ffn, m_norm_final, m_ab_w_in, m_gdn_conv_w, m_gdn_a_log, m_gdn_dt_bias, m_gdn_norm, m_hgrn_lower_bounds, m_hgrn_norm, m_ab_w_out, m_c_w_in, m_c_conv_w, m_c_conv_b, m_c_gate_a_w, m_c_gate_a_b, m_c_gate_x_w, m_c_gate_x_b, m_c_lambda, m_c_w_out, m_ffn_w_up, m_ffn_conv_w, m_ffn_conv_b, m_ffn_w_down)))
    v = dict(zip(WEIGHTS, (v_norm_mix, v_norm_ffn, v_norm_final, v_ab_w_in, v_gdn_conv_w, v_gdn_a_log, v_gdn_dt_bias, v_gdn_norm, v_hgrn_lower_bounds, v_hgrn_norm, v_ab_w_out, v_c_w_in, v_c_conv_w, v_c_conv_b, v_c_gate_a_w, v_c_gate_a_b, v_c_gate_x_w, v_c_gate_x_b, v_c_lambda, v_c_w_out, v_ffn_w_up, v_ffn_conv_w, v_ffn_conv_b, v_ffn_w_down)))
    big = list(BIG)
    chip = 2 * lax.axis_index("x") + lax.axis_index("y")
    ids = jnp.stack([chip, lax.axis_index("c")]).astype(jnp.int32)

    def layer_parts(l):
        j = l // 2
        if l % 2 == 0:
            mix = [("ab_w_in", j), ("ab_w_out", j)]
        else:
            mix = [("c_w_in", j), ("c_gate_a_w", j), ("c_gate_x_w", j), ("c_w_out", j)]
        return mix, [("ffn_w_up", l), ("ffn_w_down", l)]

    def layer_shard(n, i):
        s = w[n][i]
        return s.reshape(-1, s.shape[-1]).astype(bf16)

    small_shard_shapes = [w[n].shape for n in SMALL_SHARDED]
    small_pack = _pack_rows([w[n] for n in SMALL_SHARDED], 128)
    mix0, ffn0 = layer_parts(0)
    gathered0 = _all_gather_chips([layer_shard(n, i) for n, i in mix0] + [small_pack], "all_gather_mixer0")
    gathers = {}

    def start_gather(l, after):
        parts = ffn0 if l == 0 else sum(layer_parts(l), [])
        shards = [layer_shard(n, i) for n, i in parts]
        gathers[l] = _split_start(_gather_copies, shards, [_sds((N_CHIPS,) + s.shape, bf16) for s in shards],
                                  "gather_start_%d" % l, after)
        return gathers[l]["token"]

    first_tokens = (start_gather(0, ()), start_gather(1, ()))
    fw = {}
    per_chip = [_unpack_rows(gathered0[-1][k], small_shard_shapes) for k in range(N_CHIPS)]
    for i, n in enumerate(SMALL_SHARDED):
        fw[n] = jnp.concatenate([per_chip[k][i] for k in range(N_CHIPS)], axis=-1)
    for n in SMALL:
        if n not in fw:
            fw[n] = w[n]
    wt = _layout_weights(fw)
    for n in big:
        wt[n] = [None] * FULL_SHAPES[n][0]

    def pre_layer(l, part, x_l):
        deps = ()
        if l == 0 and part == "mix":
            parts, lands, deps = mix0, gathered0[:len(mix0)], first_tokens
        elif (l == 0) == (part == "ffn"):
            parts = ffn0 if l == 0 else sum(layer_parts(l), [])
            _, lands = _split_wait(_gather_copies, gathers[l], [x_l], "gather_wait_%d" % l)
            if l + 2 <= 3:
                deps = (start_gather(l + 2, lands[:1]),)
        else:
            return deps
        for (n, i), slots in zip(parts, lands):
            wt[n][i] = _layer_full(n, slots)
        return deps

    scatters = []

    def post_grads(l, part, g):
        parts = layer_parts(l)[0 if part == "mix" else 1]
        slots = [_layer_slots(n, g[n][i]) for n, i in parts]
        st = _split_start(_scatter_copies, slots, [_sds((N_REL, s.shape[1] // 2, s.shape[2]), bf16) for s in slots],
                          "scatter_start_%d_%s" % (l, part))
        scatters.append((parts, st, "scatter_wait_%d_%s" % (l, part)))
        return (st["token"],)

    loss, dx, g = _local_step(x[0], loss_target[0], wt, pre_layer, post_grads)
    gf = _unlayout_grads(g)
    loss = lax.psum(loss[0, 0], ("x", "y", "c"))

    f = {n: None for n in big}
    for parts, st, wait_name in scatters:
        gbs, lands = _split_wait(_scatter_copies, st, [dx], wait_name)
        for (n, i), gb, land in zip(parts, gbs, lands):
            f[n] = _sum_pieces(gb, land, ids, f[n], i, FULL_SHAPES[n][0], "rs_sum")
    filled = _sibling_fill_blocks([f[n] for n in big], [FULL_SHAPES[n][0] for n in big], "rs_sibling_fill")
    g_shard = dict(zip(big, filled))

    small_full_shapes = [FULL_SHAPES[n] for n in SMALL]
    small_sum = _all_reduce_small(_pack_rows([gf[n] for n in SMALL], 664), "all_reduce_small")
    g_small = dict(zip(SMALL, _unpack_rows(small_sum, small_full_shapes)))
    for n in SMALL_SHARDED:
        width = w[n].shape[-1]
        g_small[n] = lax.dynamic_slice_in_dim(g_small[n], chip * width, width, axis=-1)

    out_g, out_d, out_m, out_v = {}, {}, {}, {}
    for n in big:
        shp = w[n].shape
        res = _adamw_big(_shard2d(n, w[n]), g_shard[n], _shard2d(n, m[n]), _shard2d(n, v[n]), "adamw_" + n)
        out_g[n], out_d[n], out_m[n], out_v[n] = (t.reshape(shp) for t in res)
    as2d = lambda t: t.reshape(-1, t.shape[-1])
    sg, sd, sm, sv = _adamw_small([as2d(w[n]) for n in SMALL], [as2d(g_small[n]) for n in SMALL],
                                  [as2d(m[n]) for n in SMALL], [as2d(v[n]) for n in SMALL], "adamw_small")
    for i, n in enumerate(SMALL):
        out_g[n], out_d[n], out_m[n], out_v[n] = (t[i].reshape(w[n].shape) for t in (sg, sd, sm, sv))
    return (loss, dx[None], *[out_g[n] for n in WEIGHTS], *[out_d[n] for n in WEIGHTS],
            *[out_m[n] for n in WEIGHTS], *[out_v[n] for n in WEIGHTS])
```

```python
import functools

import jax
import jax.numpy as jnp
from jax import lax
from jax.experimental import pallas as pl
from jax.experimental.pallas import tpu as pltpu

f32 = jnp.float32
bf16 = jnp.bfloat16
HI = lax.Precision.HIGHEST

D = 1024
EPS = 1e-6
F_FLOOR = 1e-30
GDN_CHUNK = 64
GDN_INTRA_CHUNKS = 1
HGRN_CHUNK = 16
HGRN_STEP = 128
HEAD = 128
NH = 4
LRU_BLOCK = 256
D_FF = 2816
RG_C = 8.0
C_QKV, C_Z, C_QB, C_FB, C_IB, C_GB, C_BA = 0, 1536, 2048, 2560, 3072, 3584, 4096
AB_COLS_PAD = 4224
TT = 256
N_SLOTS = 4
VMEM_LIMIT = 56 * 1024 * 1024

ADAM_LR, ADAM_B1, ADAM_B2, ADAM_EPS, ADAM_WD, ADAM_STEP = 0.001, 0.9, 0.999, 1e-08, 0.01, 10


def _cp(**kw):
    return pltpu.CompilerParams(vmem_limit_bytes=VMEM_LIMIT, **kw)


def _sds(shape, dtype):
    return jax.ShapeDtypeStruct(shape, dtype)


def _dot(a, b, dims, precision=None):
    return lax.dot_general(a, b, (dims, ((), ())), precision=precision, preferred_element_type=f32)


NN = ((1,), (0,))
NT = ((1,), (1,))
TN = ((0,), (0,))


def _rms(x, g):
    return x * lax.rsqrt(jnp.mean(x * x, axis=-1, keepdims=True) + EPS) * g


def _silu(x):
    return x * jax.nn.sigmoid(x)


def _mm(a, b, mode, tm, tn, out_dtype, name):
    if mode == "nn":
        (m, k), n = a.shape, b.shape[1]
        a_spec = pl.BlockSpec((tm, k), lambda i, j: (i, 0))
        b_spec = pl.BlockSpec((k, tn), lambda i, j: (0, j))
        dims = NN
    elif mode == "nt":
        (m, k), n = a.shape, b.shape[0]
        a_spec = pl.BlockSpec((tm, k), lambda i, j: (i, 0))
        b_spec = pl.BlockSpec((tn, k), lambda i, j: (j, 0))
        dims = NT
    else:
        (k, m), n = a.shape, b.shape[1]
        a_spec = pl.BlockSpec((k, tm), lambda i, j: (0, i))
        b_spec = pl.BlockSpec((k, tn), lambda i, j: (0, j))
        dims = TN
    assert m % tm == 0 and n % tn == 0, (name, m, n, tm, tn)

    def body(a_ref, b_ref, o_ref):
        o_ref[...] = _dot(a_ref[...], b_ref[...], dims).astype(out_dtype)

    return pl.pallas_call(
        body, grid=(m // tm, n // tn), in_specs=[a_spec, b_spec],
        out_specs=pl.BlockSpec((tm, tn), lambda i, j: (i, j)),
        out_shape=_sds((m, n), out_dtype), compiler_params=_cp(), name=name)(a, b)


def _mm_res_norm(a, b, res, gain, name, tm=512):
    (m, k), n = a.shape, b.shape[1]
    tm = min(tm, m)
    assert n == D and m % tm == 0, (name, m, n)

    def body(a_ref, b_ref, r_ref, *rest):
        xv = r_ref[...] + _dot(a_ref[...], b_ref[...], NN)
        if gain is None:
            rest[0][...] = xv
        else:
            g_ref, x_ref, h_ref = rest
            x_ref[...] = xv
            h_ref[...] = _rms(xv, g_ref[...]).astype(bf16)

    row = pl.BlockSpec((tm, D), lambda i: (i, 0))
    in_specs = [pl.BlockSpec((tm, k), lambda i: (i, 0)), pl.BlockSpec((k, D), lambda i: (0, 0)), row]
    args = [a, b, res]
    if gain is None:
        out_specs, out_shape = row, _sds((m, D), f32)
    else:
        in_specs.append(pl.BlockSpec((1, D), lambda i: (0, 0)))
        args.append(gain)
        out_specs, out_shape = [row, row], [_sds((m, D), f32), _sds((m, D), bf16)]
    return pl.pallas_call(body, grid=(m // tm,), in_specs=in_specs, out_specs=out_specs, out_shape=out_shape,
                          compiler_params=_cp(), name=name)(*args)


def _tile(n, cands):
    for c in cands:
        if n % c == 0:
            return c
    raise ValueError(n)


def _mm_auto(a, b, mode, out_dtype, name):
    m = a.shape[1] if mode == "tn" else a.shape[0]
    n = b.shape[0] if mode == "nt" else b.shape[1]
    return _mm(a, b, mode, _tile(m, (512, 256, 128)), _tile(n, (1024, 1408, 512, 384, 256, 128)), out_dtype, name)


def _mm_nn_slots(a, bs, out_dtype, name, after=()):
    (m, k), w = a.shape, bs.shape[2]
    tm = _tile(m, (1024, 512, 256, 128))

    def body(a_ref, b_ref, *rest):
        o_ref = rest[len(after)]
        o_ref[...] = _dot(a_ref[...], b_ref[...], NN).astype(out_dtype)

    return pl.pallas_call(
        body, grid=(m // tm, N_SLOTS),
        in_specs=[pl.BlockSpec((tm, k), lambda i, j: (i, 0)), pl.BlockSpec((None, k, w), lambda i, j: (j, 0, 0))]
        + [pl.BlockSpec(memory_space=pl.ANY)] * len(after),
        out_specs=pl.BlockSpec((tm, w), lambda i, j: (i, j)),
        out_shape=_sds((m, N_SLOTS * w), out_dtype), compiler_params=_cp(), name=name)(a, bs, *after)


def _mm_nt_slots(a, bs, name):
    (m, _), (_, r, w) = a.shape, bs.shape
    tm, tn = _tile(m, (512, 256, 128)), _tile(r, (1024, 512, 256, 128))

    def body(a_ref, b_ref, o_ref):
        acc = _dot(a_ref[:, 0:w], b_ref[0], NT)
        for s in range(1, N_SLOTS):
            acc = acc + _dot(a_ref[:, s * w:(s + 1) * w], b_ref[s], NT)
        o_ref[...] = acc

    return pl.pallas_call(
        body, grid=(m // tm, r // tn),
        in_specs=[pl.BlockSpec((tm, N_SLOTS * w), lambda i, j: (i, 0)), pl.BlockSpec((N_SLOTS, tn, w), lambda i, j: (0, j, 0))],
        out_specs=pl.BlockSpec((tm, tn), lambda i, j: (i, j)),
        out_shape=_sds((m, r), f32), compiler_params=_cp(), name=name)(a, bs)


def _mm_nt_rmsbwd(a, b, x, gain, dres, name, after=()):
    m, k = a.shape
    slots = b.ndim == 3
    assert b.shape[-2] == D
    tm = _tile(m, (512, 256, 128)) if k < 4096 else _tile(m, (256, 128))

    def body(a_ref, b_ref, x_ref, g_ref, r_ref, *rest):
        dx_ref, dxb_ref, dg_ref = rest[len(after):]
        if slots:
            w = b.shape[2]
            dh = _dot(a_ref[:, 0:w], b_ref[0], NT)
            for s in range(1, N_SLOTS):
                dh = dh + _dot(a_ref[:, s * w:(s + 1) * w], b_ref[s], NT)
        else:
            dh = _dot(a_ref[...], b_ref[...], NT)
        _, vjp = jax.vjp(_rms, x_ref[...], g_ref[...])
        dx, dg = vjp(dh)
        dx = dx + r_ref[...]
        dx_ref[...] = dx
        dxb_ref[...] = dx.astype(bf16)

        @pl.when(pl.program_id(0) == 0)
        def _():
            dg_ref[...] = jnp.zeros_like(dg_ref)

        dg_ref[...] += dg

    row = pl.BlockSpec((tm, D), lambda i: (i, 0))
    vec = pl.BlockSpec((1, D), lambda i: (0, 0))
    b_spec = pl.BlockSpec(b.shape, (lambda i: (0, 0, 0)) if slots else (lambda i: (0, 0)))
    return pl.pallas_call(
        body, grid=(m // tm,),
        in_specs=[pl.BlockSpec((tm, k), lambda i: (i, 0)), b_spec, row, vec, row] + [pl.BlockSpec(memory_space=pl.ANY)] * len(after),
        out_specs=[row, row, vec],
        out_shape=[_sds((m, D), f32), _sds((m, D), bf16), _sds((1, D), f32)],
        compiler_params=_cp(), name=name)(a, b, x, gain, dres, *after)


def _mm_tn_slots(a, b, name):
    (k, m), w = a.shape, b.shape[1] // N_SLOTS
    tm = _tile(m, (512, 256, 128))

    def body(a_ref, b_ref, o_ref):
        o_ref[...] = _dot(a_ref[...], b_ref[...], TN).astype(bf16)

    return pl.pallas_call(
        body, grid=(N_SLOTS, m // tm),
        in_specs=[pl.BlockSpec((k, tm), lambda j, i: (0, i)), pl.BlockSpec((k, w), lambda j, i: (0, j))],
        out_specs=pl.BlockSpec((None, tm, w), lambda j, i: (j, i, 0)),
        out_shape=_sds((N_SLOTS, m, w), bf16), compiler_params=_cp(), name=name)(a, b)


def _rmsnorm_fwd(x, gain, name, after=()):
    t = x.shape[0]

    def body(x_ref, g_ref, *rest):
        h_ref = rest[len(after)]
        h_ref[...] = _rms(x_ref[...], g_ref[...]).astype(bf16)

    return pl.pallas_call(
        body, grid=(t // TT,),
        in_specs=[pl.BlockSpec((TT, D), lambda i: (i, 0)), pl.BlockSpec((1, D), lambda i: (0, 0))]
        + [pl.BlockSpec(memory_space=pl.ANY)] * len(after),
        out_specs=pl.BlockSpec((TT, D), lambda i: (i, 0)),
        out_shape=_sds((t, D), bf16), compiler_params=_cp(), name=name)(x, gain, *after)


def _rmsnorm_bwd(x, gain, dh, dres, name, after=()):
    t = x.shape[0]

    def body(x_ref, g_ref, dh_ref, dres_ref, *rest):
        dx_ref, dxb_ref, dg_ref = rest[len(after):]
        _, vjp = jax.vjp(_rms, x_ref[...], g_ref[...])
        dx, dg = vjp(dh_ref[...])
        dx = dx + dres_ref[...]
        dx_ref[...] = dx
        dxb_ref[...] = dx.astype(bf16)

        @pl.when(pl.program_id(0) == 0)
        def _():
            dg_ref[...] = jnp.zeros_like(dg_ref)

        dg_ref[...] += dg

    row = pl.BlockSpec((TT, D), lambda i: (i, 0))
    vec = pl.BlockSpec((1, D), lambda i: (0, 0))
    return pl.pallas_call(
        body, grid=(t // TT,), in_specs=[row, vec, row, row] + [pl.BlockSpec(memory_space=pl.ANY)] * len(after),
        out_specs=[row, row, vec],
        out_shape=[_sds((t, D), f32), _sds((t, D), bf16), _sds((1, D), f32)],
        compiler_params=_cp(), name=name)(x, gain, dh, dres, *after)


def _residual_add(x, y, name):
    t = x.shape[0]

    def body(x_ref, y_ref, o_ref):
        o_ref[...] = x_ref[...] + y_ref[...]

    row = pl.BlockSpec((TT, D), lambda i: (i, 0))
    return pl.pallas_call(body, grid=(t // TT,), in_specs=[row, row], out_specs=row,
                          out_shape=_sds((t, D), f32), compiler_params=_cp(), name=name)(x, y)


def _final_loss(x, gain, target, name):
    t = x.shape[0]

    def loss_fn(xv, gv, tv):
        e = _rms(xv, gv) - tv
        return 0.5 * jnp.sum(jnp.mean(e * e, axis=-1))

    def body(x_ref, g_ref, t_ref, loss_ref, dx_ref, dxb_ref, dg_ref):
        val, (dx, dg) = jax.value_and_grad(loss_fn, argnums=(0, 1))(x_ref[...], g_ref[...], t_ref[...])

        @pl.when(pl.program_id(0) == 0)
        def _():
            dg_ref[...] = jnp.zeros_like(dg_ref)
            loss_ref[...] = jnp.zeros_like(loss_ref)

        dx_ref[...] = dx
        dxb_ref[...] = dx.astype(bf16)
        dg_ref[...] += dg
        loss_ref[...] += jnp.full((1, 128), val, f32)

    row = pl.BlockSpec((TT, D), lambda i: (i, 0))
    vec = pl.BlockSpec((1, D), lambda i: (0, 0))
    return pl.pallas_call(
        body, grid=(t // TT,), in_specs=[row, vec, row],
        out_specs=[pl.BlockSpec((1, 128), lambda i: (0, 0)), row, row, vec],
        out_shape=[_sds((1, 128), f32), _sds((t, D), f32), _sds((t, D), bf16), _sds((1, D), f32)],
        compiler_params=_cp(), name=name)(x, gain, target)


def _conv_fwd(x, col0, c, w, b, name, tc=256, val=None, val_col0=0):
    t = x.shape[0]
    width = w.shape[0]
    nt = t // TT
    hb = TT // 8

    def body(*refs):
        if val is None:
            x_ref, xh_ref, w_ref, b_ref, o_ref, xp = refs
        else:
            x_ref, xh_ref, w_ref, b_ref, v_ref, o_ref, act_ref, xp = refs
        i = pl.program_id(1)
        xp[0:8, :] = jnp.where(i == 0, 0.0, xh_ref[...])
        xp[8:, :] = x_ref[...]
        acc = jnp.zeros((TT, tc), f32) + b_ref[...]
        for k in range(width):
            acc = acc + w_ref[k:k + 1, :] * xp[pl.ds(8 - (width - 1) + k, TT), :]
        o_ref[...] = acc
        if val is not None:
            act_ref[...] = (_silu(acc) * v_ref[...]).astype(bf16)

    in_specs = [
        pl.BlockSpec((TT, tc), lambda j, i: (i, j + col0)),
        pl.BlockSpec((8, tc), lambda j, i: (jnp.maximum(i * hb - 1, 0), j + col0)),
        pl.BlockSpec((width, tc), lambda j, i: (0, j)),
        pl.BlockSpec((1, tc), lambda j, i: (0, j)),
    ]
    args = [x, x, w, b]
    out_specs = [pl.BlockSpec((TT, tc), lambda j, i: (i, j))]
    out_shape = [_sds((t, c), f32)]
    if val is not None:
        in_specs.append(pl.BlockSpec((TT, tc), lambda j, i: (i, j + val_col0)))
        args.append(val)
        out_specs.append(pl.BlockSpec((TT, tc), lambda j, i: (i, j)))
        out_shape.append(_sds((t, c), bf16))
    res = pl.pallas_call(
        body, grid=(c // tc, nt), in_specs=in_specs, out_specs=out_specs, out_shape=out_shape,
        scratch_shapes=[pltpu.VMEM((TT + 8, tc), f32)], compiler_params=_cp(), name=name)(*args)
    return res[0] if val is None else res


def _conv_bwd(dc, x, col0, w, name, tc=256, dx_dtype=bf16, into=None):
    t, c = dc.shape
    width = w.shape[0]
    nt = t // TT
    hb = TT // 8

    def body(dc_ref, dcn_ref, x_ref, xh_ref, w_ref, *rest):
        dx_ref, dw_ref, db_ref, dcp, xp = rest[(0 if into is None else 1):]
        i = pl.program_id(1)
        dcv = dc_ref[...]
        dcp[0:TT, :] = dcv
        dcp[TT:, :] = jnp.where(i == nt - 1, 0.0, dcn_ref[...])
        xp[0:8, :] = jnp.where(i == 0, 0.0, xh_ref[...])
        xp[8:, :] = x_ref[...]

        @pl.when(i == 0)
        def _():
            dw_ref[...] = jnp.zeros_like(dw_ref)
            db_ref[...] = jnp.zeros_like(db_ref)

        acc = jnp.zeros((TT, tc), f32)
        for k in range(width):
            acc = acc + w_ref[k:k + 1, :] * dcp[pl.ds((width - 1) - k, TT), :]
            dw_ref[k:k + 1, :] += jnp.sum(dcv * xp[pl.ds(8 - (width - 1) + k, TT), :], axis=0, keepdims=True)
        dx_ref[...] = acc.astype(dx_dtype)
        db_ref[...] += jnp.sum(dcv, axis=0, keepdims=True)

    in_specs = [
        pl.BlockSpec((TT, tc), lambda j, i: (i, j)),
        pl.BlockSpec((8, tc), lambda j, i: (jnp.minimum((i + 1) * hb, t // 8 - 1), j)),
        pl.BlockSpec((TT, tc), lambda j, i: (i, j + col0)),
        pl.BlockSpec((8, tc), lambda j, i: (jnp.maximum(i * hb - 1, 0), j + col0)),
        pl.BlockSpec((width, tc), lambda j, i: (0, j)),
    ]
    out_specs = [
        pl.BlockSpec((TT, tc), lambda j, i: (i, j)),
        pl.BlockSpec((8, tc), lambda j, i: (0, j)),
        pl.BlockSpec((1, tc), lambda j, i: (0, j)),
    ]
    args, aliases, dx_shape = [dc, dc, x, x, w], {}, _sds((t, c), dx_dtype)
    if into is not None:
        in_specs.append(pl.BlockSpec(memory_space=pl.ANY))
        args.append(into)
        aliases, dx_shape = {5: 0}, _sds(into.shape, into.dtype)
    return pl.pallas_call(
        body, grid=(c // tc, nt), in_specs=in_specs, out_specs=out_specs,
        out_shape=[dx_shape, _sds((8, c), f32), _sds((1, c), f32)], input_output_aliases=aliases,
        scratch_shapes=[pltpu.VMEM((TT + 8, tc), f32), pltpu.VMEM((TT + 8, tc), f32)],
        compiler_params=_cp(), name=name)(*args)


def _bdot_impl(a, b, dims):
    return _dot(a.astype(bf16), b.astype(bf16), dims)


@functools.partial(jax.custom_vjp, nondiff_argnums=(2,))
def _bdot(a, b, dims):
    return _bdot_impl(a, b, dims)


def _bdot_fwd(a, b, dims):
    return _bdot_impl(a, b, dims), (a, b)


def _bdot_bwd(dims, res, ct):
    a, b = res
    if dims == NN:
        return _bdot_impl(ct, b, NT), _bdot_impl(a, ct, TN)
    if dims == NT:
        return _bdot_impl(ct, b, NN), _bdot_impl(ct, a, TN)
    return _bdot_impl(b, ct, NT), _bdot_impl(a, ct, NN)


_bdot.defvjp(_bdot_fwd, _bdot_bwd)


def _split2(a):
    hi = a.astype(bf16)
    return hi, (a - hi.astype(f32)).astype(bf16)


def _dot3_impl(a, b, dims):
    a_hi, a_lo = _split2(a)
    b_hi, b_lo = _split2(b)
    return (_dot(a_hi, b_hi, dims) + _dot(a_hi, b_lo, dims)) + _dot(a_lo, b_hi, dims)


@functools.partial(jax.custom_vjp, nondiff_argnums=(2,))
def _dot3(a, b, dims):
    return _dot3_impl(a, b, dims)


def _dot3_fwd(a, b, dims):
    return _dot3_impl(a, b, dims), (a, b)


def _dot3_bwd(dims, res, ct):
    a, b = res
    if dims == NN:
        return _dot3_impl(ct, b, NT), _dot3_impl(a, ct, TN)
    if dims == NT:
        return _dot3_impl(ct, b, NN), _dot3_impl(ct, a, TN)
    return _dot3_impl(b, ct, NT), _dot3_impl(a, ct, NN)


_dot3.defvjp(_dot3_fwd, _dot3_bwd)


def _tril_dot_impl(tril, x, dims):
    t = tril.astype(bf16)
    x1 = x.astype(bf16)
    r1 = x - x1.astype(f32)
    x2 = r1.astype(bf16)
    x3 = (r1 - x2.astype(f32)).astype(bf16)
    return (_dot(t, x3, dims) + _dot(t, x2, dims)) + _dot(t, x1, dims)


@jax.custom_vjp
def _cumsum_rows(tril, x):
    return _tril_dot_impl(tril, x, NN)


def _cumsum_rows_fwd(tril, x):
    return _tril_dot_impl(tril, x, NN), tril


def _cumsum_rows_bwd(tril, ct):
    return jnp.zeros_like(tril), _tril_dot_impl(tril, ct, TN)


_cumsum_rows.defvjp(_cumsum_rows_fwd, _cumsum_rows_bwd)


def _gdn_intra(cq, ck, cv, ba, alog_v, dtb_v, hd):
    c = GDN_CHUNK
    lane = lax.broadcasted_iota(jnp.int32, (1, HEAD), 1)
    mb = (lane == hd).astype(f32)
    ma = (lane == hd + NH).astype(f32)
    beta = jax.nn.sigmoid(jnp.sum(ba * mb, axis=1, keepdims=True))
    alpha = jnp.sum(ba * ma, axis=1, keepdims=True)
    alog = jnp.sum(alog_v * ma, axis=1, keepdims=True)
    dtb = jnp.sum(dtb_v * ma, axis=1, keepdims=True)
    g = -jnp.exp(alog) * jax.nn.softplus(alpha + dtb)
    q = _silu(cq)
    q = q * lax.rsqrt(jnp.sum(q * q, axis=-1, keepdims=True) + EPS) * (HEAD ** -0.5)
    k = _silu(ck)
    k = k * lax.rsqrt(jnp.sum(k * k, axis=-1, keepdims=True) + EPS)
    v = _silu(cv)
    row = lax.broadcasted_iota(jnp.int32, (c, c), 0)
    col = lax.broadcasted_iota(jnp.int32, (c, c), 1)
    causal = row >= col
    tril = causal.astype(f32)
    gc = _cumsum_rows(tril, jnp.broadcast_to(g, (c, HEAD)))
    gcc = _cumsum_rows(tril, jnp.broadcast_to(g, (c, c)))
    diff = jnp.where(causal, gcc - gcc.T, 0.0)
    decay = jnp.where(causal, jnp.exp(diff), 0.0)
    kb = k * beta
    nmat = -jnp.where(row > col, _bdot(kb, k, NT) * decay, 0.0)
    egc = jnp.exp(gc)
    sol = jnp.concatenate([v * beta, kb * egc], axis=1)
    p = nmat
    for it in range(6):
        sol = sol + _dot3(p, sol, NN)
        if it < 5:
            p = _dot3(p, p, NN)
    u, w = sol[:, :HEAD], sol[:, HEAD:]
    attn = _bdot(q, k, NT) * decay
    rowv = lax.broadcasted_iota(jnp.int32, (c, 1), 0)
    gc_last = jnp.sum(jnp.where(rowv == c - 1, gc, 0.0), axis=0, keepdims=True)
    return u, w, q * egc, k * jnp.exp(gc_last - gc), attn, jnp.exp(gc_last)


def _gdn_intra4(cq, ck, cv, ba, alog_v, dtb_v):
    c = GDN_CHUNK
    hs = range(NH)
    lane = lax.broadcasted_iota(jnp.int32, (1, HEAD), 1)
    mb = [(lane == h).astype(f32) for h in hs]
    ma = [(lane == h + NH).astype(f32) for h in hs]
    beta = [jax.nn.sigmoid(jnp.sum(ba * mb[h], axis=1, keepdims=True)) for h in hs]
    alpha = [jnp.sum(ba * ma[h], axis=1, keepdims=True) for h in hs]
    alog = [jnp.sum(alog_v * ma[h], axis=1, keepdims=True) for h in hs]
    dtb = [jnp.sum(dtb_v * ma[h], axis=1, keepdims=True) for h in hs]
    g = [-jnp.exp(alog[h]) * jax.nn.softplus(alpha[h] + dtb[h]) for h in hs]
    q = [_silu(cq[h]) for h in hs]
    q = [q[h] * lax.rsqrt(jnp.sum(q[h] * q[h], axis=-1, keepdims=True) + EPS) * (HEAD ** -0.5) for h in hs]
    k = [_silu(ck[h]) for h in hs]
    k = [k[h] * lax.rsqrt(jnp.sum(k[h] * k[h], axis=-1, keepdims=True) + EPS) for h in hs]
    v = [_silu(cv[h]) for h in hs]
    row = lax.broadcasted_iota(jnp.int32, (c, c), 0)
    col = lax.broadcasted_iota(jnp.int32, (c, c), 1)
    causal = row >= col
    tril = causal.astype(f32)
    gc = [_cumsum_rows(tril, jnp.broadcast_to(g[h], (c, HEAD))) for h in hs]
    gcc = [_cumsum_rows(tril, jnp.broadcast_to(g[h], (c, c))) for h in hs]
    decay = [jnp.where(causal, jnp.exp(jnp.where(causal, gcc[h] - gcc[h].T, 0.0)), 0.0) for h in hs]
    kb = [k[h] * beta[h] for h in hs]
    kk = [_bdot(kb[h], k[h], NT) for h in hs]
    p = [-jnp.where(row > col, kk[h] * decay[h], 0.0) for h in hs]
    egc = [jnp.exp(gc[h]) for h in hs]
    sol = [jnp.concatenate([v[h] * beta[h], kb[h] * egc[h]], axis=1) for h in hs]
    for it in range(6):
        upd = [_dot3(p[h], sol[h], NN) for h in hs]
        sol = [sol[h] + upd[h] for h in hs]
        if it < 5:
            p = [_dot3(p[h], p[h], NN) for h in hs]
    qk = [_bdot(q[h], k[h], NT) for h in hs]
    attn = [qk[h] * decay[h] for h in hs]
    rowv = lax.broadcasted_iota(jnp.int32, (c, 1), 0)
    gc_last = [jnp.sum(jnp.where(rowv == c - 1, gc[h], 0.0), axis=0, keepdims=True) for h in hs]
    return ([sol[h][:, :HEAD] for h in hs], [sol[h][:, HEAD:] for h in hs], [q[h] * egc[h] for h in hs],
            [k[h] * jnp.exp(gc_last[h] - gc[h]) for h in hs], attn, [jnp.exp(gc_last[h]) for h in hs])


def _gdn_seq(u, w, q_dec, k_dec, attn, dl, z, s, gain):
    v_new = u - _bdot(w, s, NN)
    o = _bdot(q_dec, s, NN) + _bdot(attn, v_new, NN)
    s_new = s * dl + _bdot(k_dec, v_new, TN)
    return _rms(o, gain) * _silu(z), s_new


def _gdn_seq4(u, w, q_dec, k_dec, attn, dl, z, s, gain):
    hs = range(NH)
    ws = [_bdot(w[h], s[h], NN) for h in hs]
    qs = [_bdot(q_dec[h], s[h], NN) for h in hs]
    v_new = [u[h] - ws[h] for h in hs]
    av = [_bdot(attn[h], v_new[h], NN) for h in hs]
    kv = [_bdot(k_dec[h], v_new[h], TN) for h in hs]
    o = [_rms(qs[h] + av[h], gain) * _silu(z[h]) for h in hs]
    return o, [s[h] * dl[h] + kv[h] for h in hs]


def _hsl(h):
    return slice(h * HEAD, (h + 1) * HEAD)


def _gdn2_fwd(conv, p, alog_v, dtb_v, gain, name):
    t = conv.shape[0]
    c = GDN_CHUNK
    nch = t // c
    w512 = NH * HEAD
    wide = lambda off: pl.BlockSpec((c, w512), lambda n: (n, off))
    vec = pl.BlockSpec((1, HEAD), lambda n: (0, 0))
    attn_spec = pl.BlockSpec((1, NH, c, c), lambda n: (n, 0, 0, 0))
    dl_spec = pl.BlockSpec((1, NH, HEAD), lambda n: (n, 0, 0))

    cps = GDN_INTRA_CHUNKS
    iwide = lambda off: pl.BlockSpec((cps * c, w512), lambda n: (n, off))

    def intra(cq, ck, cv, ba, al, dt, u_ref, w_ref, qd_ref, kd_ref, at_ref, dl_ref):
        for ci in range(cps):
            rows = slice(ci * c, (ci + 1) * c)
            u, w, qd, kd, at, dl = _gdn_intra4([cq[rows, _hsl(h)] for h in range(NH)], [ck[rows, _hsl(h)] for h in range(NH)],
                                               [cv[rows, _hsl(h)] for h in range(NH)], ba[rows, :], al[...], dt[...])
            for h in range(NH):
                u_ref[rows, _hsl(h)] = u[h]
                w_ref[rows, _hsl(h)] = w[h]
                qd_ref[rows, _hsl(h)] = qd[h]
                kd_ref[rows, _hsl(h)] = kd[h]
                at_ref[ci, h] = at[h]
                dl_ref[ci, h:h + 1, :] = dl[h]

    u, w, qd, kd, at, dl = pl.pallas_call(
        intra, grid=(nch // cps,),
        in_specs=[iwide(0), iwide(1), iwide(2), pl.BlockSpec((cps * c, HEAD), lambda n: (n, C_BA // HEAD)), vec, vec],
        out_specs=[iwide(0)] * 4 + [pl.BlockSpec((cps, NH, c, c), lambda n: (n, 0, 0, 0)),
                                    pl.BlockSpec((cps, NH, HEAD), lambda n: (n, 0, 0))],
        out_shape=[_sds((t, w512), f32)] * 4 + [_sds((nch, NH, c, c), f32), _sds((nch, NH, HEAD), f32)],
        compiler_params=_cp(), name=name + "_intra")(conv, conv, conv, p, alog_v, dtb_v)

    def seq(u_ref, w_ref, qd_ref, kd_ref, at_ref, dl_ref, z_ref, gn, o_ref, ss_ref, s_scr):
        @pl.when(pl.program_id(0) == 0)
        def _():
            s_scr[...] = jnp.zeros_like(s_scr)

        hs = range(NH)
        s = [s_scr[h] for h in hs]
        for h in hs:
            ss_ref[0, h] = s[h]
        o, s_new = _gdn_seq4([u_ref[:, _hsl(h)] for h in hs], [w_ref[:, _hsl(h)] for h in hs], [qd_ref[:, _hsl(h)] for h in hs],
                             [kd_ref[:, _hsl(h)] for h in hs], [at_ref[0, h] for h in hs], [dl_ref[0, h:h + 1, :] for h in hs],
                             [z_ref[:, _hsl(h)] for h in hs], s, gn[...])
        for h in hs:
            o_ref[:, _hsl(h)] = o[h].astype(bf16)
            s_scr[h] = s_new[h]

    o, states = pl.pallas_call(
        seq, grid=(nch,),
        in_specs=[wide(0)] * 4 + [attn_spec, dl_spec, wide(C_Z // w512), vec],
        out_specs=[wide(0), pl.BlockSpec((1, NH, HEAD, HEAD), lambda n: (n, 0, 0, 0))],
        out_shape=[_sds((t, w512), bf16), _sds((nch, NH, HEAD, HEAD), f32)],
        scratch_shapes=[pltpu.VMEM((NH, HEAD, HEAD), f32)],
        compiler_params=_cp(), name=name + "_seq")(u, w, qd, kd, at, dl, p, gain)
    return o, dict(u=u, w=w, qd=qd, kd=kd, at=at, dl=dl, states=states)


def _gdn2_bwd(conv, p, alog_v, dtb_v, gain, saved, do, name):
    t = conv.shape[0]
    c = GDN_CHUNK
    nch = t // c
    w512 = NH * HEAD
    rwide = lambda off: pl.BlockSpec((c, w512), lambda n: (nch - 1 - n, off))
    rvec = pl.BlockSpec((1, HEAD), lambda n: (0, 0))
    rattn = pl.BlockSpec((1, NH, c, c), lambda n: (nch - 1 - n, 0, 0, 0))
    rdl = pl.BlockSpec((1, NH, HEAD), lambda n: (nch - 1 - n, 0, 0))

    def seq_bwd(u_ref, w_ref, qd_ref, kd_ref, at_ref, dl_ref, z_ref, gn, ss_ref, do_ref,
                du_ref, dw_ref, dqd_ref, dkd_ref, dat_ref, ddl_ref, dz_ref, dgn_ref, ds_scr):
        @pl.when(pl.program_id(0) == 0)
        def _():
            ds_scr[...] = jnp.zeros_like(ds_scr)
            dgn_ref[...] = jnp.zeros_like(dgn_ref)

        hs = range(NH)
        _, vjp = jax.vjp(_gdn_seq4, [u_ref[:, _hsl(h)] for h in hs], [w_ref[:, _hsl(h)] for h in hs],
                         [qd_ref[:, _hsl(h)] for h in hs], [kd_ref[:, _hsl(h)] for h in hs], [at_ref[0, h] for h in hs],
                         [dl_ref[0, h:h + 1, :] for h in hs], [z_ref[:, _hsl(h)] for h in hs], [ss_ref[0, h] for h in hs], gn[...])
        du, dw, dqd, dkd, dat, ddl, dz, ds, dg = vjp(([do_ref[:, _hsl(h)] for h in hs], [ds_scr[h] for h in hs]))
        for h in hs:
            du_ref[:, _hsl(h)] = du[h]
            dw_ref[:, _hsl(h)] = dw[h]
            dqd_ref[:, _hsl(h)] = dqd[h]
            dkd_ref[:, _hsl(h)] = dkd[h]
            dat_ref[0, h] = dat[h]
            ddl_ref[0, h:h + 1, :] = ddl[h]
            dz_ref[:, _hsl(h)] = dz[h]
            ds_scr[h] = ds[h]
        dgn_ref[...] += dg

    du, dw, dqd, dkd, dat, ddl, dz, dgn = pl.pallas_call(
        seq_bwd, grid=(nch,),
        in_specs=[rwide(0)] * 4 + [rattn, rdl, rwide(C_Z // w512), rvec,
                                   pl.BlockSpec((1, NH, HEAD, HEAD), lambda n: (nch - 1 - n, 0, 0, 0)), rwide(0)],
        out_specs=[rwide(0)] * 4 + [rattn, rdl, rwide(0), rvec],
        out_shape=[_sds((t, w512), f32)] * 4 + [_sds((nch, NH, c, c), f32), _sds((nch, NH, HEAD), f32),
                                                _sds((t, w512), f32), _sds((1, HEAD), f32)],
        scratch_shapes=[pltpu.VMEM((NH, HEAD, HEAD), f32)],
        compiler_params=_cp(), name=name + "_seq")(
            saved["u"], saved["w"], saved["qd"], saved["kd"], saved["at"], saved["dl"], p, gain, saved["states"], do)

    cps = GDN_INTRA_CHUNKS
    wide = lambda off: pl.BlockSpec((cps * c, w512), lambda n: (n, off))
    vec = pl.BlockSpec((1, HEAD), lambda n: (0, 0))
    attn_spec = pl.BlockSpec((cps, NH, c, c), lambda n: (n, 0, 0, 0))
    dl_spec = pl.BlockSpec((cps, NH, HEAD), lambda n: (n, 0, 0))

    def intra_bwd(cq, ck, cv, ba, al, dt, du_ref, dw_ref, dqd_ref, dkd_ref, dat_ref, ddl_ref,
                  dc_ref, dba_ref, dal_ref, ddt_ref):
        @pl.when(pl.program_id(0) == 0)
        def _():
            dal_ref[...] = jnp.zeros_like(dal_ref)
            ddt_ref[...] = jnp.zeros_like(ddt_ref)

        dal = jnp.zeros((1, HEAD), f32)
        ddt = jnp.zeros((1, HEAD), f32)
        for ci in range(cps):
            rows = slice(ci * c, (ci + 1) * c)
            hs = range(NH)
            _, vjp = jax.vjp(_gdn_intra4, [cq[rows, _hsl(h)] for h in hs], [ck[rows, _hsl(h)] for h in hs],
                             [cv[rows, _hsl(h)] for h in hs], ba[rows, :], al[...], dt[...])
            g_q, g_k, g_v, g_ba, g_al, g_dt = vjp((
                [du_ref[rows, _hsl(h)] for h in hs], [dw_ref[rows, _hsl(h)] for h in hs], [dqd_ref[rows, _hsl(h)] for h in hs],
                [dkd_ref[rows, _hsl(h)] for h in hs], [dat_ref[ci, h] for h in hs], [ddl_ref[ci, h:h + 1, :] for h in hs]))
            for h in hs:
                dc_ref[rows, _hsl(h)] = g_q[h]
                dc_ref[rows, _hsl(NH + h)] = g_k[h]
                dc_ref[rows, _hsl(2 * NH + h)] = g_v[h]
            dal = dal + g_al
            ddt = ddt + g_dt
            dba_ref[rows, :] = g_ba
        dal_ref[...] += dal
        ddt_ref[...] += ddt

    dconv, dba, dal, ddt = pl.pallas_call(
        intra_bwd, grid=(nch // cps,),
        in_specs=[wide(0), wide(1), wide(2), pl.BlockSpec((cps * c, HEAD), lambda n: (n, C_BA // HEAD)), vec, vec]
        + [wide(0)] * 4 + [attn_spec, dl_spec],
        out_specs=[pl.BlockSpec((cps * c, 3 * w512), lambda n: (n, 0)), pl.BlockSpec((cps * c, HEAD), lambda n: (n, 0)),
                   vec, vec],
        out_shape=[_sds((t, 3 * w512), f32), _sds((t, HEAD), f32), _sds((1, HEAD), f32), _sds((1, HEAD), f32)],
        compiler_params=_cp(), name=name + "_intra")(conv, conv, conv, p, alog_v, dtb_v, du, dw, dqd, dkd, dat, ddl)
    return dconv, dz, dba, dal, ddt, dgn


def _hgrn_intra(qb, fb, ib, lb):
    c = HGRN_CHUNK
    ns = range(len(qb))
    f = [lb + (1.0 - lb) * jax.nn.sigmoid(fb[i]) for i in ns]
    logf = [jnp.log(jnp.maximum(f[i], F_FLOOR)) for i in ns]
    k = [1.0 - f[i] for i in ns]
    q = [_silu(qb[i]) for i in ns]
    row = lax.broadcasted_iota(jnp.int32, (c, c), 0)
    col = lax.broadcasted_iota(jnp.int32, (c, c), 1)
    tril = (row >= col).astype(f32)
    b = [_cumsum_rows(tril, logf[i]) for i in ns]
    ri = lax.broadcasted_iota(jnp.int32, (c, 1), 0)
    o = [jnp.zeros((c, HEAD), f32) for _ in ns]
    for j in range(c):
        mj = ri == j
        ok = ri >= j
        bj = [jnp.sum(jnp.where(mj, b[i], 0.0), axis=0, keepdims=True) for i in ns]
        kj = [jnp.sum(jnp.where(mj, k[i], 0.0), axis=0, keepdims=True) for i in ns]
        vj = [jnp.sum(jnp.where(mj, ib[i], 0.0), axis=0, keepdims=True) for i in ns]
        e = [jnp.where(ok, jnp.exp(jnp.where(ok, b[i] - bj[i], 0.0)), 0.0) for i in ns]
        s = [jnp.sum(q[i] * kj[i] * e[i], axis=1, keepdims=True) for i in ns]
        o = [o[i] + s[i] * vj[i] for i in ns]
    b_last = [jnp.sum(jnp.where(ri == c - 1, b[i], 0.0), axis=0, keepdims=True) for i in ns]
    return (o, [q[i] * jnp.exp(b[i]) for i in ns], [k[i] * jnp.exp(b_last[i] - b[i]) for i in ns],
            [jnp.exp(b_last[i]) for i in ns])


def _hgrn_seq(o_intra, q_dec, k_dec, dl, v, gb, st, gain):
    o = o_intra + _bdot(q_dec, st, NT)
    st_new = st * dl + _bdot(v, k_dec, TN)
    return _rms(o, gain) * _silu(gb), st_new


def _hgrn_fwd(p, lb, gain, name):
    t = p.shape[0]
    r = HGRN_STEP
    ns = t // r
    nsub = r // HGRN_CHUNK
    blk = lambda off: pl.BlockSpec((r, HEAD), lambda n, h: (n, off // HEAD + h))

    def body(qb, fb, ib, gb, lb_ref, gn, o_ref, ss_ref, s_scr):
        n, h = pl.program_id(0), pl.program_id(1)

        @pl.when(n == 0)
        def _():
            s_scr[h] = jnp.zeros((HEAD, HEAD), f32)

        st = s_scr[h]
        ss_ref[0, 0] = st
        rows = [pl.ds(ch * HGRN_CHUNK, HGRN_CHUNK) for ch in range(nsub)]
        v = [ib[rw, :] for rw in rows]
        oi, qd, kd, dl = _hgrn_intra([qb[rw, :] for rw in rows], [fb[rw, :] for rw in rows], v, lb_ref[...])
        for ch in range(nsub):
            o, st = _hgrn_seq(oi[ch], qd[ch], kd[ch], dl[ch], v[ch], gb[rows[ch], :], st, gn[...])
            o_ref[rows[ch], :] = o.astype(bf16)
        s_scr[h] = st

    return pl.pallas_call(
        body, grid=(ns, NH),
        in_specs=[blk(C_QB), blk(C_FB), blk(C_IB), blk(C_GB),
                  pl.BlockSpec((1, HEAD), lambda n, h: (0, h)), pl.BlockSpec((1, HEAD), lambda n, h: (0, 0))],
        out_specs=[pl.BlockSpec((r, HEAD), lambda n, h: (n, h)),
                   pl.BlockSpec((1, 1, HEAD, HEAD), lambda n, h: (n, h, 0, 0))],
        out_shape=[_sds((t, NH * HEAD), bf16), _sds((ns, NH, HEAD, HEAD), f32)],
        scratch_shapes=[pltpu.VMEM((NH, HEAD, HEAD), f32)],
        compiler_params=_cp(), name=name)(p, p, p, p, lb, gain)


def _hgrn_bwd(p, lb, gain, states, do, name):
    t = p.shape[0]
    r = HGRN_STEP
    ns = t // r
    nsub = r // HGRN_CHUNK
    blk = lambda off: pl.BlockSpec((r, HEAD), lambda n, h: (ns - 1 - n, off // HEAD + h))
    hblk = pl.BlockSpec((r, HEAD), lambda n, h: (ns - 1 - n, h))

    def body(qb, fb, ib, gb, lb_ref, gn, ss_ref, do_ref,
             dqb, dfb, dib, dgb, dlb_ref, dgn_ref, ds_scr, st_scr):
        n, h = pl.program_id(0), pl.program_id(1)

        @pl.when(n == 0)
        def _():
            ds_scr[h] = jnp.zeros((HEAD, HEAD), f32)

        @pl.when((n == 0) & (h == 0))
        def _():
            dgn_ref[...] = jnp.zeros_like(dgn_ref)

        gnv = gn[...]
        rows = [pl.ds(ch * HGRN_CHUNK, HGRN_CHUNK) for ch in range(nsub)]
        v = [ib[rw, :] for rw in rows]
        (oi, qd, kd, dl), vjp_intra = jax.vjp(_hgrn_intra, [qb[rw, :] for rw in rows], [fb[rw, :] for rw in rows], v, lb_ref[...])
        st = ss_ref[0, 0]
        for ch in range(nsub):
            st_scr[ch] = st
            if ch < nsub - 1:
                st = st * dl[ch] + _bdot(v[ch], kd[ch], TN)
        ds = ds_scr[h]
        dgn = jnp.zeros((1, HEAD), f32)
        d_oi, d_qd, d_kd, d_dl, d_v = [None] * nsub, [None] * nsub, [None] * nsub, [None] * nsub, [None] * nsub
        for ch in reversed(range(nsub)):
            _, vjp = jax.vjp(_hgrn_seq, oi[ch], qd[ch], kd[ch], dl[ch], v[ch], gb[rows[ch], :], st_scr[ch], gnv)
            d_oi[ch], d_qd[ch], d_kd[ch], d_dl[ch], d_v[ch], g_g, ds, g_gn = vjp((do_ref[rows[ch], :], ds))
            dgb[rows[ch], :] = g_g
            dgn = dgn + g_gn
        g_q, g_f, g_i, g_lb = vjp_intra((d_oi, d_qd, d_kd, d_dl))
        for ch in range(nsub):
            dqb[rows[ch], :] = g_q[ch]
            dfb[rows[ch], :] = g_f[ch]
            dib[rows[ch], :] = g_i[ch] + d_v[ch]
        ds_scr[h] = ds
        dlb_ref[0] = g_lb
        dgn_ref[...] += dgn

    return pl.pallas_call(
        body, grid=(ns, NH),
        in_specs=[blk(C_QB), blk(C_FB), blk(C_IB), blk(C_GB),
                  pl.BlockSpec((1, HEAD), lambda n, h: (0, h)), pl.BlockSpec((1, HEAD), lambda n, h: (0, 0)),
                  pl.BlockSpec((1, 1, HEAD, HEAD), lambda n, h: (ns - 1 - n, h, 0, 0)),
                  pl.BlockSpec((r, HEAD), lambda n, h: (ns - 1 - n, NH + h))],
        out_specs=[hblk, hblk, hblk, hblk,
                   pl.BlockSpec((1, 1, HEAD), lambda n, h: (n, 0, h)),
                   pl.BlockSpec((1, HEAD), lambda n, h: (0, 0))],
        out_shape=[_sds((t, 512), f32)] * 4 + [_sds((ns, 1, 512), f32), _sds((1, HEAD), f32)],
        scratch_shapes=[pltpu.VMEM((NH, HEAD, HEAD), f32), pltpu.VMEM((nsub, HEAD, HEAD), f32)],
        compiler_params=_cp(), name=name)(p, p, p, p, lb, gain, states, do)


def _lru_gates(xb, wa, wx, ba, bx, lam):
    xh = xb.astype(bf16)
    r = jax.nn.sigmoid(_dot(xh, wa.astype(bf16), NN) + ba)
    i = jax.nn.sigmoid(_dot(xh, wx.astype(bf16), NN) + bx)
    log_a = -RG_C * r * jax.nn.softplus(-lam)
    a = jnp.exp(log_a)
    t2 = 2.0 * log_a
    series = -t2 * (1.0 + t2 * (0.5 + t2 * (1.0 / 6.0 + t2 * (1.0 / 24.0))))
    om = jnp.where(t2 > -1e-2, series, 1.0 - jnp.exp(t2))
    u = jnp.sqrt(jnp.maximum(om, 0.0)) * (i * xb)
    return a, u


def _lru_gates_fwd(xc, wa, wx, ba, bx, lam, name):
    t = xc.shape[0]
    blk = pl.BlockSpec((TT, LRU_BLOCK), lambda h, i: (i, h))
    wsp = pl.BlockSpec((1, LRU_BLOCK, LRU_BLOCK), lambda h, i: (h, 0, 0))
    vsp = pl.BlockSpec((1, LRU_BLOCK), lambda h, i: (0, h))

    def body(x_ref, wa_ref, wx_ref, ba_ref, bx_ref, lam_ref, a_ref, u_ref):
        a, u = _lru_gates(x_ref[...], wa_ref[0], wx_ref[0], ba_ref[...], bx_ref[...], lam_ref[...])
        a_ref[...] = a
        u_ref[...] = u

    return pl.pallas_call(
        body, grid=(NH, t // TT), in_specs=[blk, wsp, wsp, vsp, vsp, vsp], out_specs=[blk, blk],
        out_shape=[_sds((t, D), f32)] * 2, compiler_params=_cp(), name=name)(xc, wa, wx, ba, bx, lam)


def _lru_gates_bwd(xc, wa, wx, ba, bx, lam, da, du, name):
    t = xc.shape[0]
    blk = pl.BlockSpec((TT, LRU_BLOCK), lambda h, i: (i, h))
    wsp = pl.BlockSpec((1, LRU_BLOCK, LRU_BLOCK), lambda h, i: (h, 0, 0))
    vsp = pl.BlockSpec((1, LRU_BLOCK), lambda h, i: (0, h))

    def body(x_ref, wa_ref, wx_ref, ba_ref, bx_ref, lam_ref, da_ref, du_ref,
             dx_ref, dwa_ref, dwx_ref, dba_ref, dbx_ref, dlam_ref):
        @pl.when(pl.program_id(1) == 0)
        def _():
            for r in (dwa_ref, dwx_ref, dba_ref, dbx_ref, dlam_ref):
                r[...] = jnp.zeros_like(r)

        _, vjp = jax.vjp(_lru_gates, x_ref[...], wa_ref[0], wx_ref[0], ba_ref[...], bx_ref[...], lam_ref[...])
        dx, dwa, dwx, dba, dbx, dlam = vjp((da_ref[...], du_ref[...]))
        dx_ref[...] = dx
        dwa_ref[0] += dwa
        dwx_ref[0] += dwx
        dba_ref[...] += dba
        dbx_ref[...] += dbx
        dlam_ref[...] += dlam

    return pl.pallas_call(
        body, grid=(NH, t // TT), in_specs=[blk, wsp, wsp, vsp, vsp, vsp, blk, blk],
        out_specs=[blk, wsp, wsp, vsp, vsp, vsp],
        out_shape=[_sds((t, D), f32), _sds((NH, LRU_BLOCK, LRU_BLOCK), f32), _sds((NH, LRU_BLOCK, LRU_BLOCK), f32),
                   _sds((1, D), f32), _sds((1, D), f32), _sds((1, D), f32)],
        compiler_params=_cp(), name=name)(xc, wa, wx, ba, bx, lam, da, du)


_SCAN_SHIFTS = (1, 2, 4, 8, 16, 32, 64, 128)
_SCAN_PAD = 128


def _gelu(y):
    return jax.nn.gelu(y, approximate=True)


def _lru_scan_fwd(a, u, p2, name):
    t = a.shape[0]
    tc = 128
    blk = pl.BlockSpec((TT, tc), lambda j, i: (i, j))

    def body(a_ref, u_ref, y_ref, h_ref, hg_ref, a_s, b_s, carry):
        i = pl.program_id(1)

        @pl.when(i == 0)
        def _():
            carry[...] = jnp.zeros_like(carry)
            a_s[0:_SCAN_PAD, :] = jnp.ones((_SCAN_PAD, tc), f32)
            b_s[0:_SCAN_PAD, :] = jnp.zeros((_SCAN_PAD, tc), f32)

        av, bv = a_ref[...], u_ref[...]
        for s in _SCAN_SHIFTS:
            a_s[_SCAN_PAD:, :] = av
            b_s[_SCAN_PAD:, :] = bv
            ash = a_s[pl.ds(_SCAN_PAD - s, TT), :]
            bsh = b_s[pl.ds(_SCAN_PAD - s, TT), :]
            bv = bv + av * bsh
            av = av * ash
        h = bv + av * carry[7:8, :]
        h_ref[...] = h
        hg_ref[...] = (h * _gelu(y_ref[...])).astype(bf16)
        carry[...] = h[TT - 8:, :]

    return pl.pallas_call(
        body, grid=(D // tc, t // TT), in_specs=[blk, blk, blk], out_specs=[blk, blk],
        out_shape=[_sds((t, D), f32), _sds((t, D), bf16)],
        scratch_shapes=[pltpu.VMEM((_SCAN_PAD + TT, tc), f32), pltpu.VMEM((_SCAN_PAD + TT, tc), f32),
                        pltpu.VMEM((8, tc), f32)],
        compiler_params=_cp(), name=name)(a, u, p2)


def _lru_scan_bwd(a, h, p2, dhg, name):
    t = a.shape[0]
    tc = 128
    nt = t // TT
    hb = TT // 8
    rblk = pl.BlockSpec((TT, tc), lambda j, i: (nt - 1 - i, j))

    def body(a_ref, an_ref, h_ref, hp_ref, y_ref, dhg_ref, du_ref, da_ref, dy_ref, a_s, b_s, ap, hp, carry):
        i = pl.program_id(1)

        @pl.when(i == 0)
        def _():
            carry[...] = jnp.zeros_like(carry)
            a_s[TT:, :] = jnp.ones((_SCAN_PAD, tc), f32)
            b_s[TT:, :] = jnp.zeros((_SCAN_PAD, tc), f32)

        ap[0:TT, :] = a_ref[...]
        ap[TT:, :] = jnp.where(i == 0, 0.0, an_ref[...])
        hp[0:8, :] = jnp.where(i == nt - 1, 0.0, hp_ref[...])
        hp[8:, :] = h_ref[...]
        y = y_ref[...]
        gate, gvjp = jax.vjp(_gelu, y)
        dhg_v = dhg_ref[...]
        dy_ref[...] = gvjp(dhg_v * h_ref[...])[0]
        av = ap[pl.ds(1, TT), :]
        bv = dhg_v * gate
        for s in _SCAN_SHIFTS:
            a_s[0:TT, :] = av
            b_s[0:TT, :] = bv
            ash = a_s[pl.ds(s, TT), :]
            bsh = b_s[pl.ds(s, TT), :]
            bv = bv + av * bsh
            av = av * ash
        g = bv + av * carry[0:1, :]
        du_ref[...] = g
        da_ref[...] = g * hp[pl.ds(7, TT), :]
        carry[...] = g[0:8, :]

    in_specs = [
        rblk,
        pl.BlockSpec((8, tc), lambda j, i: (jnp.minimum((nt - i) * hb, t // 8 - 1), j)),
        rblk,
        pl.BlockSpec((8, tc), lambda j, i: (jnp.maximum((nt - 1 - i) * hb - 1, 0), j)),
        rblk, rblk,
    ]
    return pl.pallas_call(
        body, grid=(D // tc, nt), in_specs=in_specs, out_specs=[rblk, rblk, rblk],
        out_shape=[_sds((t, D), f32)] * 3,
        scratch_shapes=[pltpu.VMEM((TT + _SCAN_PAD, tc), f32), pltpu.VMEM((TT + _SCAN_PAD, tc), f32),
                        pltpu.VMEM((TT + 8, tc), f32), pltpu.VMEM((TT + 8, tc), f32), pltpu.VMEM((8, tc), f32)],
        compiler_params=_cp(), name=name)(a, a, h, h, p2, dhg)


def _ffn_act_bwd(gc, up, dact, name, tc=256):
    t = gc.shape[0]
    blk = pl.BlockSpec((TT, tc), lambda i, j: (i, j))
    vblk = pl.BlockSpec((TT, tc), lambda i, j: (i, j + D_FF // tc))

    def body(gc_ref, v_ref, da_ref, dgc_ref, dv_ref):
        _, vjp = jax.vjp(lambda g, v: _silu(g) * v, gc_ref[...], v_ref[...])
        dg, dv = vjp(da_ref[...])
        dgc_ref[...] = dg
        dv_ref[...] = dv.astype(bf16)

    return pl.pallas_call(
        body, grid=(t // TT, D_FF // tc), in_specs=[blk, vblk, blk], out_specs=[blk, vblk],
        out_shape=[_sds((t, D_FF), f32), _sds((t, 2 * D_FF), bf16)], compiler_params=_cp(), name=name)(gc, up, dact)


def _lower_bounds_fwd(w):
    def body(w_ref, o0_ref, o1_ref):
        wv = w_ref[...]
        o0, o1 = _lb_rows(wv[0:1, :], wv[1:2, :])
        o0_ref[...] = o0
        o1_ref[...] = o1

    return pl.pallas_call(body, out_shape=[_sds((1, 512), f32)] * 2, name="lower_bounds_fwd")(w)


def _lb_rows(w0, w1):
    m = jnp.maximum(w0, w1)
    e0, e1 = jnp.exp(w0 - m), jnp.exp(w1 - m)
    s = e0 + e1
    p0, p1 = e0 / s, e1 / s
    return p0 - p0, (p0 + p1) - p0


def _lower_bounds_bwd(w, d0, d1):
    def body(w_ref, d0_ref, d1_ref, g0_ref, g1_ref):
        wv = w_ref[...]
        _, vjp = jax.vjp(_lb_rows, wv[0:1, :], wv[1:2, :])
        g0, g1 = vjp((d0_ref[...], d1_ref[...]))
        g0_ref[...] = g0
        g1_ref[...] = g1

    return pl.pallas_call(body, out_shape=[_sds((1, 512), f32)] * 2, name="lower_bounds_bwd")(w, d0, d1)


def _local_step(x, target, wt, pre_layer=None, post_grads=None):
    depth = 4
    res = []
    lb0, lb1 = _lower_bounds_fwd(wt["hgrn_lower_bounds"])
    lbs = [lb0, lb1]
    for layer in range(depth):
        j = layer // 2
        sv = {"x_in": x}
        deps = pre_layer(layer, "mix", x) if pre_layer else ()
        if layer == 0:
            h1 = _rmsnorm_fwd(x, wt["norm_mix"][layer], "rms_fwd", deps)
        sv["h1"] = h1
        if layer % 2 == 0:
            p = _mm_auto(h1, wt["ab_w_in"][j], "nn", f32, "mm_ab_in")
            conv = _conv_fwd(p, 0, 1536, wt["gdn_conv_w"][j], jnp.zeros((1, 1536), f32), "gdn_conv_fwd", tc=768)
            o_a, s_a = _gdn2_fwd(conv, p, wt["alog_v"][j], wt["dtb_v"][j], wt["gdn_norm"][j], "gdn_fwd")
            o_b, s_b = _hgrn_fwd(p, lbs[j], wt["hgrn_norm"][j], "hgrn_fwd")
            o = jnp.concatenate([o_a, o_b], axis=1)
            x, h2 = _mm_res_norm(o, wt["ab_w_out"][j], x, wt["norm_ffn"][layer], "mm_ab_out")
            sv.update(p=p, conv=conv, s_a=s_a, s_b=s_b, o=o)
        else:
            p2 = _mm_nn_slots(h1, wt["c_w_in"][j], f32, "mm_c_in", deps)
            xc = _conv_fwd(p2, 1, D, wt["c_conv_w"][j], wt["c_conv_b"][j], "lru_conv_fwd", tc=D)
            a, u = _lru_gates_fwd(xc, wt["c_gate_a_w"][j], wt["c_gate_x_w"][j], wt["c_gate_a_b"][j],
                                  wt["c_gate_x_b"][j], wt["c_lambda"][j], "lru_gates_fwd")
            h, hg = _lru_scan_fwd(a, u, p2, "lru_scan_fwd")
            x, h2 = _mm_res_norm(hg, wt["c_w_out"][j], x, wt["norm_ffn"][layer], "mm_c_out")
            sv.update(p2=p2, xc=xc, a=a, h=h, hg=hg)
        sv["x_mid"] = x
        deps = pre_layer(layer, "ffn", x) if pre_layer else ()
        up = _mm_nn_slots(h2, wt["ffn_w_up"][layer], f32, "mm_up", deps)
        gc, act = _conv_fwd(up, 0, D_FF, wt["ffn_conv_w"][layer], wt["ffn_conv_b"][layer], "ffn_conv_fwd",
                            tc=D_FF // 2, val=up, val_col0=2)
        if layer + 1 < depth:
            x, h1 = _mm_res_norm(act, wt["ffn_w_down"][layer], x, wt["norm_mix"][layer + 1], "mm_down")
        else:
            x = _mm_res_norm(act, wt["ffn_w_down"][layer], x, None, "mm_down_last")
        sv.update(h2=h2, up=up, gc=gc, act=act)
        res.append(sv)

    loss, dx, dxb, d_norm_final = _final_loss(x, wt["norm_final"], target, "final_loss")

    g = {k: [None] * len(v) for k, v in wt.items() if isinstance(v, list)}
    g["norm_final"] = d_norm_final
    d_lbs = [None, None]
    for layer in reversed(range(depth)):
        j = layer // 2
        sv = res[layer]
        dact = _mm_auto(dxb, wt["ffn_w_down"][layer], "nt", f32, "mm_down_dx")
        g["ffn_w_down"][layer] = _mm_auto(sv["act"], dxb, "tn", bf16, "mm_down_dw")
        dgc, dup = _ffn_act_bwd(sv["gc"], sv["up"], dact, "ffn_act_bwd", tc=D_FF // 2)
        dup, dcw, dcb = _conv_bwd(dgc, sv["up"], 0, wt["ffn_conv_w"][layer], "ffn_conv_bwd", tc=D_FF // 2, into=dup)
        g["ffn_conv_w"][layer] = dcw[:3]
        g["ffn_conv_b"][layer] = dcb
        g["ffn_w_up"][layer] = _mm_tn_slots(sv["h2"], dup, "mm_up_dw")
        deps = post_grads(layer, "ffn", g) if post_grads else ()
        dx, dxb, g["norm_ffn"][layer] = _mm_nt_rmsbwd(dup, wt["ffn_w_up"][layer], sv["x_mid"], wt["norm_ffn"][layer], dx,
                                                      "mm_up_dx", deps)
        if layer % 2 == 0:
            do = _mm_auto(dxb, wt["ab_w_out"][j], "nt", f32, "mm_ab_out_dx")
            g["ab_w_out"][j] = _mm_auto(sv["o"], dxb, "tn", bf16, "mm_ab_out_dw")
            dconv, dz, dba, dal, ddt, dgn = _gdn2_bwd(
                sv["conv"], sv["p"], wt["alog_v"][j], wt["dtb_v"][j], wt["gdn_norm"][j], sv["s_a"], do, "gdn_bwd")
            g["alog_v"][j], g["dtb_v"][j], g["gdn_norm"][j] = dal, ddt, dgn
            dqkv, dcw, _ = _conv_bwd(dconv, sv["p"], 0, wt["gdn_conv_w"][j], "gdn_conv_bwd", tc=768)
            g["gdn_conv_w"][j] = dcw[:4]
            dqb, dfb, dib, dgb, dlb, dhn = _hgrn_bwd(sv["p"], lbs[j], wt["hgrn_norm"][j], sv["s_b"], do, "hgrn_bwd")
            g["hgrn_norm"][j] = dhn
            d_lbs[j] = jnp.sum(dlb, axis=0)
            dp = jnp.concatenate([dqkv] + [t_.astype(bf16) for t_ in (dz, dqb, dfb, dib, dgb, dba)], axis=1)
            g["ab_w_in"][j] = _mm_auto(sv["h1"], dp, "tn", bf16, "mm_ab_in_dw")
            dpre, wpre = dp, wt["ab_w_in"][j]
        else:
            dhg = _mm_auto(dxb, wt["c_w_out"][j], "nt", f32, "mm_c_out_dx")
            g["c_w_out"][j] = _mm_auto(sv["hg"], dxb, "tn", bf16, "mm_c_out_dw")
            du, da, dy = _lru_scan_bwd(sv["a"], sv["h"], sv["p2"], dhg, "lru_scan_bwd")
            dxc, dwa, dwx, dba_, dbx_, dlam = _lru_gates_bwd(
                sv["xc"], wt["c_gate_a_w"][j], wt["c_gate_x_w"][j], wt["c_gate_a_b"][j], wt["c_gate_x_b"][j],
                wt["c_lambda"][j], da, du, "lru_gates_bwd")
            g["c_gate_a_w"][j], g["c_gate_x_w"][j] = dwa, dwx
            g["c_gate_a_b"][j], g["c_gate_x_b"][j], g["c_lambda"][j] = dba_, dbx_, dlam
            dxbr, dcw, dcb = _conv_bwd(dxc, sv["p2"], 1, wt["c_conv_w"][j], "lru_conv_bwd", tc=D)
            g["c_conv_w"][j] = dcw[:4]
            g["c_conv_b"][j] = dcb
            dp2 = jnp.concatenate([dy.astype(bf16), dxbr], axis=1)
            g["c_w_in"][j] = _mm_tn_slots(sv["h1"], dp2, "mm_c_in_dw")
            dpre, wpre = dp2, wt["c_w_in"][j]
        deps = post_grads(layer, "mix", g) if post_grads else ()
        dx, dxb, g["norm_mix"][layer] = _mm_nt_rmsbwd(dpre, wpre, sv["x_in"], wt["norm_mix"][layer], dx, "mm_mix_in_dx", deps)
    g0, g1 = _lower_bounds_bwd(wt["hgrn_lower_bounds"], d_lbs[0], d_lbs[1])
    g["hgrn_lower_bounds"] = jnp.concatenate([g0, g1], axis=0)
    return loss, dx, g


def _ab_in_to_compute(w):
    return jnp.concatenate([w[:, :2048], w[:, 2056:4104], w[:, 2048:2056], jnp.zeros((D, 120), w.dtype)], axis=1)


def _ab_in_from_compute(g):
    return jnp.concatenate([g[:, :2048], g[:, 4096:4104], g[:, 2048:4096]], axis=1)


def _lane_vec(v4):
    return jnp.zeros((1, HEAD), f32).at[0, NH:2 * NH].set(v4)


def _layout_weights(fw):
    wt = {}
    wt["norm_mix"] = [fw["norm_mix"][l][None] for l in range(4)]
    wt["norm_ffn"] = [fw["norm_ffn"][l][None] for l in range(4)]
    wt["norm_final"] = fw["norm_final"][None]
    wt["gdn_conv_w"] = [fw["gdn_conv_w"][j] for j in range(2)]
    wt["alog_v"] = [_lane_vec(fw["gdn_a_log"][j]) for j in range(2)]
    wt["dtb_v"] = [_lane_vec(fw["gdn_dt_bias"][j]) for j in range(2)]
    wt["gdn_norm"] = [fw["gdn_norm"][j][None] for j in range(2)]
    wt["hgrn_lower_bounds"] = fw["hgrn_lower_bounds"]
    wt["hgrn_norm"] = [fw["hgrn_norm"][j][None] for j in range(2)]
    wt["c_conv_w"] = [fw["c_conv_w"][j] for j in range(2)]
    for k in ("c_conv_b", "c_gate_a_b", "c_gate_x_b", "c_lambda"):
        wt[k] = [fw[k][j][None] for j in range(2)]
    wt["ffn_conv_w"] = [fw["ffn_conv_w"][l] for l in range(4)]
    wt["ffn_conv_b"] = [fw["ffn_conv_b"][l][None] for l in range(4)]
    if "ab_w_in" in fw:
        wt["ab_w_in"] = [_ab_in_to_compute(fw["ab_w_in"][j].astype(bf16)) for j in range(2)]
        for k in ("ab_w_out", "c_w_out"):
            wt[k] = [fw[k][j].astype(bf16) for j in range(2)]
        wt["c_w_in"] = [_to_slots(fw["c_w_in"][j].astype(bf16)) for j in range(2)]
        for k in ("c_gate_a_w", "c_gate_x_w"):
            wt[k] = [fw[k][j].astype(f32) for j in range(2)]
        wt["ffn_w_up"] = [_to_slots(fw["ffn_w_up"][l].astype(bf16)) for l in range(4)]
        wt["ffn_w_down"] = [fw["ffn_w_down"][l].astype(bf16) for l in range(4)]
    return wt


SLOT_MAJOR = ("c_w_in", "ffn_w_up")


def _to_slots(wfull):
    k, n = wfull.shape
    return wfull.reshape(k, N_SLOTS, n // N_SLOTS).transpose(1, 0, 2)


def _layer_full(name, slots):
    kind = BIG[name]
    if name in SLOT_MAJOR:
        return slots
    if kind == "col":
        return _ab_in_to_compute(slots.transpose(1, 0, 2).reshape(slots.shape[1], -1))
    if kind == "row":
        return slots.reshape(-1, slots.shape[2])
    return slots.reshape(4, NH, LRU_BLOCK // 4, LRU_BLOCK).transpose(1, 0, 2, 3).reshape(NH, LRU_BLOCK, LRU_BLOCK).astype(f32)


def _layer_slots(name, g):
    kind = BIG[name]
    if name in SLOT_MAJOR:
        return g
    if kind == "col":
        g = _ab_in_from_compute(g)
        r, cdim = g.shape
        return g.reshape(r, 4, cdim // 4).transpose(1, 0, 2).astype(bf16)
    if kind == "row":
        r, cdim = g.shape
        return g.reshape(4, r // 4, cdim).astype(bf16)
    return g.reshape(NH, 4, LRU_BLOCK // 4, LRU_BLOCK).transpose(1, 0, 2, 3).reshape(4, LRU_BLOCK, LRU_BLOCK).astype(bf16)


def _unlayout_grads(g):
    out = {}
    for k in ("norm_mix", "norm_ffn", "gdn_norm", "hgrn_norm", "c_conv_b", "c_gate_a_b", "c_gate_x_b", "c_lambda",
              "ffn_conv_b"):
        out[k] = jnp.concatenate(g[k], axis=0)
    out["norm_final"] = g["norm_final"][0]
    out["ab_w_in"] = jnp.stack([_ab_in_from_compute(t) for t in g["ab_w_in"]])
    out["gdn_a_log"] = jnp.stack([t[0, NH:2 * NH] for t in g["alog_v"]])
    out["gdn_dt_bias"] = jnp.stack([t[0, NH:2 * NH] for t in g["dtb_v"]])
    out["hgrn_lower_bounds"] = g["hgrn_lower_bounds"]
    for k in ("gdn_conv_w", "ab_w_out", "c_conv_w", "c_gate_a_w", "c_gate_x_w", "c_w_out", "ffn_conv_w", "ffn_w_down"):
        out[k] = jnp.stack(g[k])
    for k in SLOT_MAJOR:
        out[k] = jnp.stack([t.transpose(1, 0, 2).reshape(t.shape[1], -1) for t in g[k]])
    return out


MESH = pl.DeviceIdType.MESH
ANY = pl.BlockSpec(memory_space=pl.ANY)
CHIP_RELATIONS = ((1, 0), (0, 1), (1, 1))
N_CHIPS = 4


def _coords():
    return lax.axis_index("x"), lax.axis_index("y"), lax.axis_index("c")


def _flip(v, f):
    return 1 - v if f else v


def _half_rows(c, a, align):
    return pl.ds(pl.multiple_of(c * (a // 2), align), a // 2)


def _all_gather_chips(shards, name):
    n = len(shards)
    shapes = [s.shape for s in shards]

    def body(*refs):
        ins, outs = refs[:n], refs[n:2 * n]
        send_sems, recv_sems = refs[2 * n:]
        x, y, c = _coords()
        me = 2 * x + y
        sibling = (x, y, 1 - c)
        started = []
        for p in range(n):
            cp = pltpu.make_async_remote_copy(
                src_ref=ins[p], dst_ref=outs[p].at[me],
                send_sem=send_sems.at[p, 6], recv_sem=recv_sems.at[p, 6],
                device_id=sibling, device_id_type=MESH)
            cp.start()
            started.append(cp)
        for p in range(n):
            mine = _half_rows(c, shapes[p][0], 16)
            for r, (fx, fy) in enumerate(CHIP_RELATIONS):
                cp = pltpu.make_async_remote_copy(
                    src_ref=ins[p].at[mine], dst_ref=outs[p].at[me, mine],
                    send_sem=send_sems.at[p, r], recv_sem=recv_sems.at[p, r],
                    device_id=(_flip(x, fx), _flip(y, fy), c), device_id_type=MESH)
                cp.start()
                started.append(cp)
        for r, (fx, fy) in enumerate(CHIP_RELATIONS):
            k = 2 * _flip(x, fx) + _flip(y, fy)
            for p in range(n):
                mine = _half_rows(c, shapes[p][0], 16)
                pltpu.make_async_remote_copy(
                    src_ref=ins[p].at[mine], dst_ref=outs[p].at[k, mine],
                    send_sem=send_sems.at[p, r], recv_sem=recv_sems.at[p, r],
                    device_id=(_flip(x, fx), _flip(y, fy), c), device_id_type=MESH).wait_recv()
                fwd = pltpu.make_async_remote_copy(
                    src_ref=outs[p].at[k, mine], dst_ref=outs[p].at[k, mine],
                    send_sem=send_sems.at[p, 3 + r], recv_sem=recv_sems.at[p, 3 + r],
                    device_id=sibling, device_id_type=MESH)
                fwd.start()
                started.append(fwd)
        for r, (fx, fy) in enumerate(CHIP_RELATIONS):
            k = 2 * _flip(x, fx) + _flip(y, fy)
            for p in range(n):
                theirs = _half_rows(1 - c, shapes[p][0], 16)
                pltpu.make_async_remote_copy(
                    src_ref=outs[p].at[k, theirs], dst_ref=outs[p].at[k, theirs],
                    send_sem=send_sems.at[p, 3 + r], recv_sem=recv_sems.at[p, 3 + r],
                    device_id=sibling, device_id_type=MESH).wait_recv()
        for p in range(n):
            pltpu.make_async_remote_copy(
                src_ref=ins[p], dst_ref=outs[p].at[me],
                send_sem=send_sems.at[p, 6], recv_sem=recv_sems.at[p, 6],
                device_id=sibling, device_id_type=MESH).wait_recv()
        for cp in started:
            cp.wait_send()

    return pl.pallas_call(
        body, in_specs=[ANY] * n, out_specs=[ANY] * n,
        out_shape=[_sds((N_CHIPS,) + s.shape, s.dtype) for s in shards],
        scratch_shapes=[pltpu.SemaphoreType.DMA((n, 7)), pltpu.SemaphoreType.DMA((n, 7))],
        name=name)(*shards)


def _sibling_send_other_half(gs, name):
    n = len(gs)
    shapes = [g.shape for g in gs]

    def body(*refs):
        ins, outs = refs[:n], refs[n:2 * n]
        send_sems, recv_sems = refs[2 * n:]
        x, y, c = _coords()
        cps = []
        for p in range(n):
            theirs = _half_rows(1 - c, shapes[p][1], 8)
            cp = pltpu.make_async_remote_copy(
                src_ref=ins[p].at[:, theirs], dst_ref=outs[p],
                send_sem=send_sems.at[p], recv_sem=recv_sems.at[p],
                device_id=(x, y, 1 - c), device_id_type=MESH)
            cp.start()
            cps.append(cp)
        for cp in cps:
            cp.wait()

    return pl.pallas_call(
        body, in_specs=[ANY] * n, out_specs=[ANY] * n,
        out_shape=[_sds((s[0], s[1] // 2, s[2]), f32) for s in shapes],
        scratch_shapes=[pltpu.SemaphoreType.DMA((n,)), pltpu.SemaphoreType.DMA((n,))],
        name=name)(*gs)


def _chip_exchange(ps, name):
    n = len(ps)

    def body(*refs):
        ins, outs = refs[:n], refs[n:2 * n]
        send_sems, recv_sems = refs[2 * n:]
        x, y, c = _coords()
        cps = []
        for p in range(n):
            for r, (fx, fy) in enumerate(CHIP_RELATIONS):
                k = 2 * _flip(x, fx) + _flip(y, fy)
                cp = pltpu.make_async_remote_copy(
                    src_ref=ins[p].at[k], dst_ref=outs[p].at[r],
                    send_sem=send_sems.at[p, r], recv_sem=recv_sems.at[p, r],
                    device_id=(_flip(x, fx), _flip(y, fy), c), device_id_type=MESH)
                cp.start()
                cps.append(cp)
        for cp in cps:
            cp.wait_recv()
        for cp in cps:
            cp.wait_send()

    return pl.pallas_call(
        body, in_specs=[ANY] * n, out_specs=[ANY] * n,
        out_shape=[_sds((3,) + p.shape[1:], p.dtype) for p in ps],
        scratch_shapes=[pltpu.SemaphoreType.DMA((n, 3)), pltpu.SemaphoreType.DMA((n, 3))],
        name=name)(*ps)


def _sibling_fill_other_half(fs, name):
    n = len(fs)
    shapes = [f.shape for f in fs]

    def body(*refs):
        ins, outs = refs[:n], refs[n:2 * n]
        send_sems, recv_sems = refs[2 * n:]
        x, y, c = _coords()
        cps = []
        for p in range(n):
            mine = _half_rows(c, shapes[p][0], 8)
            cp = pltpu.make_async_remote_copy(
                src_ref=ins[p].at[mine], dst_ref=outs[p].at[mine],
                send_sem=send_sems.at[p], recv_sem=recv_sems.at[p],
                device_id=(x, y, 1 - c), device_id_type=MESH)
            cp.start()
            cps.append(cp)
        for p in range(n):
            theirs = _half_rows(1 - c, shapes[p][0], 8)
            pltpu.make_async_remote_copy(
                src_ref=ins[p].at[theirs], dst_ref=outs[p].at[theirs],
                send_sem=send_sems.at[p], recv_sem=recv_sems.at[p],
                device_id=(x, y, 1 - c), device_id_type=MESH).wait_recv()
        for cp in cps:
            cp.wait_send()

    return pl.pallas_call(
        body, in_specs=[ANY] * n, out_specs=[ANY] * n,
        out_shape=[_sds(f.shape, f.dtype) for f in fs],
        input_output_aliases={p: p for p in range(n)},
        scratch_shapes=[pltpu.SemaphoreType.DMA((n,)), pltpu.SemaphoreType.DMA((n,))],
        name=name)(*fs)


def _sibling_all_gather(ss, name):
    n = len(ss)
    shapes = [s.shape for s in ss]

    def body(*refs):
        ins, outs = refs[:n], refs[n:2 * n]
        send_sems, recv_sems, loc_sems = refs[2 * n:]
        x, y, c = _coords()
        locs, cps = [], []
        for p in range(n):
            mine = _half_rows(c, 2 * shapes[p][0], 8)
            loc = pltpu.make_async_copy(ins[p], outs[p].at[mine], loc_sems.at[p])
            loc.start()
            locs.append(loc)
            cp = pltpu.make_async_remote_copy(
                src_ref=ins[p], dst_ref=outs[p].at[mine],
                send_sem=send_sems.at[p], recv_sem=recv_sems.at[p],
                device_id=(x, y, 1 - c), device_id_type=MESH)
            cp.start()
            cps.append(cp)
        for p, cp in enumerate(cps):
            theirs = _half_rows(1 - c, 2 * shapes[p][0], 8)
            pltpu.make_async_remote_copy(
                src_ref=ins[p], dst_ref=outs[p].at[theirs],
                send_sem=send_sems.at[p], recv_sem=recv_sems.at[p],
                device_id=(x, y, 1 - c), device_id_type=MESH).wait_recv()
        for cp in cps:
            cp.wait_send()
        for loc in locs:
            loc.wait()

    return pl.pallas_call(
        body, in_specs=[ANY] * n, out_specs=[ANY] * n,
        out_shape=[_sds((2 * s[0], s[1]), f32) for s in shapes],
        scratch_shapes=[pltpu.SemaphoreType.DMA((n,)), pltpu.SemaphoreType.DMA((n,)), pltpu.SemaphoreType.DMA((n,))],
        name=name)(*ss)


N_DEV = 8


def _all_reduce_small(pack, name):
    rows = pack.shape[0]

    def body(in_ref, sum_ref, all_ref, send_sems, recv_sems):
        x, y, c = _coords()
        me = 4 * x + 2 * y + c
        all_ref[me] = in_ref[...]
        cps = []
        for r in range(1, N_DEV):
            fx, fy, fc = (r >> 2) & 1, (r >> 1) & 1, r & 1
            cp = pltpu.make_async_remote_copy(
                src_ref=in_ref, dst_ref=all_ref.at[me],
                send_sem=send_sems.at[r], recv_sem=recv_sems.at[r],
                device_id=(_flip(x, fx), _flip(y, fy), _flip(c, fc)), device_id_type=MESH)
            cp.start()
            cps.append(cp)
        for r in range(1, N_DEV):
            fx, fy, fc = (r >> 2) & 1, (r >> 1) & 1, r & 1
            peer = 4 * _flip(x, fx) + 2 * _flip(y, fy) + _flip(c, fc)
            pltpu.make_async_remote_copy(
                src_ref=in_ref, dst_ref=all_ref.at[peer],
                send_sem=send_sems.at[r], recv_sem=recv_sems.at[r],
                device_id=(x, y, c), device_id_type=MESH).wait_recv()
        for cp in cps:
            cp.wait_send()
        acc = all_ref[0]
        for d in range(1, N_DEV):
            acc = acc + all_ref[d]
        sum_ref[...] = acc

    vm = pl.BlockSpec(memory_space=pltpu.VMEM)
    return pl.pallas_call(
        body, in_specs=[vm], out_specs=[vm, vm],
        out_shape=[_sds((rows, 128), f32), _sds((N_DEV, rows, 128), f32)],
        scratch_shapes=[pltpu.SemaphoreType.DMA((N_DEV,)), pltpu.SemaphoreType.DMA((N_DEV,))],
        name=name)(pack)[0]


ROWS_EW = 128


def _add_own_half(g, rs, c_arr, name):
    s, a, b = g.shape
    nrt = (a // 2) // ROWS_EW

    def body(c_ref, g_ref, r_ref, o_ref):
        o_ref[...] = (g_ref[...] + r_ref[...]).astype(bf16)

    grid_spec = pltpu.PrefetchScalarGridSpec(
        num_scalar_prefetch=1, grid=(s, nrt),
        in_specs=[pl.BlockSpec((1, ROWS_EW, b), lambda k, i, c_ref: (k, c_ref[0] * nrt + i, 0)),
                  pl.BlockSpec((1, ROWS_EW, b), lambda k, i, c_ref: (k, i, 0))],
        out_specs=pl.BlockSpec((1, ROWS_EW, b), lambda k, i, c_ref: (k, i, 0)))
    return pl.pallas_call(body, grid_spec=grid_spec, out_shape=_sds((s, a // 2, b), bf16),
                          compiler_params=_cp(), name=name)(c_arr, g, rs)


def _sum_chips(g, rs, rc, ids, name):
    _, r, b = rc.shape
    nrt = r // ROWS_EW

    def body(ids_ref, g_ref, s_ref, r_ref, o_ref):
        own = g_ref[0] + s_ref[0]
        o_ref[...] = ((own + r_ref[0].astype(f32)) + r_ref[1].astype(f32)) + r_ref[2].astype(f32)

    grid_spec = pltpu.PrefetchScalarGridSpec(
        num_scalar_prefetch=1, grid=(nrt,),
        in_specs=[pl.BlockSpec((1, ROWS_EW, b), lambda i, ids_ref: (ids_ref[0], ids_ref[1] * nrt + i, 0)),
                  pl.BlockSpec((1, ROWS_EW, b), lambda i, ids_ref: (ids_ref[0], i, 0)),
                  pl.BlockSpec((3, ROWS_EW, b), lambda i, ids_ref: (0, i, 0))],
        out_specs=pl.BlockSpec((ROWS_EW, b), lambda i, ids_ref: (ids_ref[1] * nrt + i, 0)))
    return pl.pallas_call(body, grid_spec=grid_spec, out_shape=_sds((2 * r, b), f32),
                          compiler_params=_cp(), name=name)(ids, g, rs, rc)


def _adamw_math(w, g, m, v):
    m = ADAM_B1 * m + (1.0 - ADAM_B1) * g
    v = ADAM_B2 * v + (1.0 - ADAM_B2) * (g * g)
    m_hat = m / (1.0 - ADAM_B1 ** ADAM_STEP)
    v_hat = v / (1.0 - ADAM_B2 ** ADAM_STEP)
    delta = -ADAM_LR * (m_hat / (jnp.sqrt(v_hat) + ADAM_EPS) + ADAM_WD * w)
    return delta, m, v


def _adamw_big(w, g, m, v, name):
    nl, r, b = w.shape
    rt = _tile(r, (ROWS_EW, 352, 64))
    per = r // rt

    def body(w_ref, g_ref, m_ref, v_ref, go_ref, d_ref, mo_ref, vo_ref):
        gv = g_ref[...]
        d, mn, vn = _adamw_math(w_ref[...], gv, m_ref[...], v_ref[...])
        go_ref[...] = gv
        d_ref[...] = d
        mo_ref[...] = mn
        vo_ref[...] = vn

    blk = pl.BlockSpec((None, rt, b), lambda l, i: (l, i, 0))
    gblk = pl.BlockSpec((rt, b), lambda l, i: (l * per + i, 0))
    return pl.pallas_call(body, grid=(nl, per), in_specs=[blk, gblk, blk, blk], out_specs=[blk] * 4,
                          out_shape=[_sds((nl, r, b), f32)] * 4, compiler_params=_cp(), name=name)(w, g, m, v)


def _adamw_small(ws, gs, ms, vs, name):
    n = len(ws)

    def body(*refs):
        w_r, g_r, m_r, v_r = refs[:n], refs[n:2 * n], refs[2 * n:3 * n], refs[3 * n:4 * n]
        go_r, d_r, mo_r, vo_r = refs[4 * n:5 * n], refs[5 * n:6 * n], refs[6 * n:7 * n], refs[7 * n:8 * n]
        for p in range(n):
            gv = g_r[p][...]
            d, mn, vn = _adamw_math(w_r[p][...], gv, m_r[p][...], v_r[p][...])
            go_r[p][...] = gv
            d_r[p][...] = d
            mo_r[p][...] = mn
            vo_r[p][...] = vn

    vm = pl.BlockSpec(memory_space=pltpu.VMEM)
    shp = [_sds(w.shape, f32) for w in ws]
    res = pl.pallas_call(body, in_specs=[vm] * (4 * n), out_specs=[vm] * (4 * n), out_shape=shp * 4,
                         name=name)(*ws, *gs, *ms, *vs)
    return res[:n], res[n:2 * n], res[2 * n:3 * n], res[3 * n:]


HBM = pl.BlockSpec(memory_space=pltpu.HBM)
SEM = pl.BlockSpec(memory_space=pltpu.SEMAPHORE)
EFFECT = pltpu.SideEffectType.DATAFLOW_SIDE_EFFECTING
N_REL = 8


def _rel(r):
    return (r >> 2) & 1, (r >> 1) & 1, r & 1


def _gather_copies(ins, lands, send_sems, recv_sems, shapes):
    x, y, c = _coords()
    me = 2 * x + y
    sends, recvs = [], []
    for p in range(len(ins)):
        for r in range(1, N_REL):
            fx, fy, fc = _rel(r)
            peer = (_flip(x, fx), _flip(y, fy), _flip(c, fc))
            if fx == 0 and fy == 0:
                src, dst, got = ins[p], lands[p].at[me], lands[p].at[me]
            else:
                mine = _half_rows(c, shapes[p][0], 16)
                theirs = _half_rows(_flip(c, fc), shapes[p][0], 16)
                src, dst = ins[p].at[mine], lands[p].at[me, mine]
                got = lands[p].at[2 * peer[0] + peer[1], theirs]
            sems = dict(send_sem=send_sems.at[p * N_REL + r], recv_sem=recv_sems.at[p * N_REL + r], device_id=peer,
                        device_id_type=MESH)
            sends.append(pltpu.make_async_remote_copy(src_ref=src, dst_ref=dst, **sems))
            recvs.append(pltpu.make_async_remote_copy(src_ref=src, dst_ref=got, **sems))
    return sends, recvs


def _scatter_copies(ins, lands, send_sems, recv_sems, shapes):
    x, y, c = _coords()
    sends, recvs = [], []
    for p in range(len(ins)):
        for r in range(1, N_REL):
            fx, fy, fc = _rel(r)
            peer = (_flip(x, fx), _flip(y, fy), _flip(c, fc))
            theirs = _half_rows(peer[2], shapes[p][1], 16)
            sems = dict(send_sem=send_sems.at[p * N_REL + r], recv_sem=recv_sems.at[p * N_REL + r], device_id=peer,
                        device_id_type=MESH)
            cp = pltpu.make_async_remote_copy(src_ref=ins[p].at[2 * peer[0] + peer[1], theirs], dst_ref=lands[p].at[r], **sems)
            sends.append(cp)
            recvs.append(cp)
    return sends, recvs


def _split_start(copies_fn, ins, land_shapes, name, after=()):
    n = len(ins)
    shapes = [a.shape for a in ins]

    def body(*refs):
        in_refs, land_refs = refs[:n], refs[n:2 * n]
        send_sems, recv_sems = refs[2 * n + len(after)], refs[2 * n + len(after) + 1]
        token = refs[-1]
        sends, _ = copies_fn(in_refs, land_refs, send_sems, recv_sems, shapes)
        for cp in sends:
            cp.start()
        token[...] = jnp.zeros_like(token)

    lands = [lax.empty(s.shape, s.dtype) for s in land_shapes]
    na = len(after)
    res = pl.pallas_call(
        body, name=name,
        out_shape=(pltpu.SemaphoreType.DMA((n * N_REL,)), pltpu.SemaphoreType.DMA((n * N_REL,)))
        + tuple(pltpu.HBM(a.shape, a.dtype) for a in ins) + tuple(pltpu.HBM(s.shape, s.dtype) for s in land_shapes)
        + (_sds((8, 128), f32),),
        in_specs=[HBM] * (2 * n) + [pl.BlockSpec(memory_space=pl.ANY)] * na,
        out_specs=(SEM, SEM) + (HBM,) * (2 * n) + (pl.BlockSpec(memory_space=pltpu.VMEM),),
        input_output_aliases={i: 2 + i for i in range(2 * n)},
        compiler_params=pltpu.CompilerParams(has_side_effects=EFFECT),
    )(*[pltpu.with_memory_space_constraint(a, pltpu.HBM) for a in ins],
      *[pltpu.with_memory_space_constraint(a, pltpu.HBM) for a in lands], *after)
    return dict(sems=res[:2], ins=res[2:2 + n], lands=res[2 + n:2 + 2 * n], token=res[-1], shapes=shapes)


def _split_wait(copies_fn, started, after, name):
    n = len(started["ins"])
    shapes = started["shapes"]
    na = len(after)

    def body(*refs):
        in_refs, land_refs = refs[:n], refs[n:2 * n]
        send_sems, recv_sems = refs[2 * n], refs[2 * n + 1]
        sends, recvs = copies_fn(in_refs, land_refs, send_sems, recv_sems, shapes)
        for cp in sends:
            cp.wait_send()
        for cp in recvs:
            cp.wait_recv()

    arrs = list(started["ins"]) + list(started["lands"])
    res = pl.pallas_call(
        body, name=name,
        out_shape=tuple(pltpu.HBM(a.shape, a.dtype) for a in arrs),
        in_specs=[HBM] * (2 * n) + [SEM, SEM] + [pl.BlockSpec(memory_space=pl.ANY)] * na,
        out_specs=(HBM,) * (2 * n), input_output_aliases={i: i for i in range(2 * n)},
        compiler_params=pltpu.CompilerParams(has_side_effects=EFFECT),
    )(*arrs, *started["sems"], *after)
    return res[:n], res[n:]


def _sum_pieces(gb, land, ids, f_prev, blk, nblk, name):
    _, a, b = gb.shape
    rows = _tile(a // 2, (ROWS_EW, 176, 64, 32, 16))
    nrt = (a // 2) // rows

    def body(ids_ref, g_ref, l_ref, *rest):
        o_ref = rest[-1]
        acc = g_ref[0].astype(f32)
        for r in range(1, N_REL):
            acc = acc + l_ref[r].astype(f32)
        o_ref[...] = acc

    in_specs = [pl.BlockSpec((1, rows, b), lambda i, ids_ref: (ids_ref[0], ids_ref[1] * nrt + i, 0)),
                pl.BlockSpec((N_REL, rows, b), lambda i, ids_ref: (0, i, 0))]
    args = [ids, gb, land]
    aliases = {}
    if f_prev is not None:
        in_specs.append(pl.BlockSpec(memory_space=pl.ANY))
        args.append(f_prev)
        aliases = {3: 0}
    grid_spec = pltpu.PrefetchScalarGridSpec(
        num_scalar_prefetch=1, grid=(nrt,), in_specs=in_specs,
        out_specs=pl.BlockSpec((rows, b), lambda i, ids_ref: ((2 * blk + ids_ref[1]) * nrt + i, 0)))
    return pl.pallas_call(body, grid_spec=grid_spec, out_shape=_sds((nblk * a, b), f32),
                          input_output_aliases=aliases, compiler_params=_cp(), name=name)(*args)


def _sibling_fill_blocks(fs, nblks, name):
    n = len(fs)
    shapes = [f.shape for f in fs]

    def body(*refs):
        ins, outs = refs[:n], refs[n:2 * n]
        send_sems, recv_sems = refs[2 * n:]
        x, y, c = _coords()
        cps, waits = [], []
        k = 0
        for p in range(n):
            a = shapes[p][0] // nblks[p]
            for bi in range(nblks[p]):
                mine = pl.ds(pl.multiple_of(bi * a + c * (a // 2), 8), a // 2)
                theirs = pl.ds(pl.multiple_of(bi * a + (1 - c) * (a // 2), 8), a // 2)
                sems = dict(send_sem=send_sems.at[k], recv_sem=recv_sems.at[k], device_id=(x, y, 1 - c), device_id_type=MESH)
                cp = pltpu.make_async_remote_copy(src_ref=ins[p].at[mine], dst_ref=outs[p].at[mine], **sems)
                cp.start()
                cps.append(cp)
                waits.append(pltpu.make_async_remote_copy(src_ref=ins[p].at[theirs], dst_ref=outs[p].at[theirs], **sems))
                k += 1
        for wt_ in waits:
            wt_.wait_recv()
        for cp in cps:
            cp.wait_send()

    total = sum(nblks)
    return pl.pallas_call(
        body, in_specs=[ANY] * n, out_specs=[ANY] * n,
        out_shape=[_sds(f.shape, f.dtype) for f in fs],
        input_output_aliases={p: p for p in range(n)},
        scratch_shapes=[pltpu.SemaphoreType.DMA((total,)), pltpu.SemaphoreType.DMA((total,))],
        name=name)(*fs)


WEIGHTS = ["norm_mix", "norm_ffn", "norm_final", "ab_w_in", "gdn_conv_w", "gdn_a_log", "gdn_dt_bias", "gdn_norm",
           "hgrn_lower_bounds", "hgrn_norm", "ab_w_out", "c_w_in", "c_conv_w", "c_conv_b", "c_gate_a_w", "c_gate_a_b",
           "c_gate_x_w", "c_gate_x_b", "c_lambda", "c_w_out", "ffn_w_up", "ffn_conv_w", "ffn_conv_b", "ffn_w_down"]
BIG = {"ab_w_in": "col", "ab_w_out": "row", "c_w_in": "col", "c_gate_a_w": "gate", "c_gate_x_w": "gate",
       "c_w_out": "row", "ffn_w_up": "col", "ffn_w_down": "row"}
SMALL_SHARDED = ["gdn_conv_w", "c_conv_w", "c_conv_b", "c_gate_a_b", "c_gate_x_b", "c_lambda", "ffn_conv_w"]
SMALL = [n for n in WEIGHTS if n not in BIG]
FULL_SHAPES = {
    "norm_mix": (4, 1024), "norm_ffn": (4, 1024), "norm_final": (1024,), "ab_w_in": (2, 1024, 4104),
    "gdn_conv_w": (2, 4, 1536), "gdn_a_log": (2, 4), "gdn_dt_bias": (2, 4), "gdn_norm": (2, 128),
    "hgrn_lower_bounds": (2, 512), "hgrn_norm": (2, 128), "ab_w_out": (2, 1024, 1024), "c_w_in": (2, 1024, 2048),
    "c_conv_w": (2, 4, 1024), "c_conv_b": (2, 1024), "c_gate_a_w": (2, 4, 256, 256), "c_gate_a_b": (2, 1024),
    "c_gate_x_w": (2, 4, 256, 256), "c_gate_x_b": (2, 1024), "c_lambda": (2, 1024), "c_w_out": (2, 1024, 1024),
    "ffn_w_up": (4, 1024, 5632), "ffn_conv_w": (4, 3, 2816), "ffn_conv_b": (4, 2816), "ffn_w_down": (4, 2816, 1024)}


def _shard2d(name, shard):
    return shard.reshape(-1, shard.shape[-1])


def _full_from_slots(name, slots):
    full = FULL_SHAPES[name]
    kind = BIG[name]
    if kind == "col":
        l, r, cdim = full
        return slots.reshape(4, l, r, cdim // 4).transpose(1, 2, 0, 3).reshape(full)
    if kind == "row":
        l, r, cdim = full
        return slots.reshape(4, l, r // 4, cdim).transpose(1, 0, 2, 3).reshape(full)
    l, h, r, cdim = full
    return slots.reshape(4, l, h, r // 4, cdim).transpose(1, 2, 0, 3, 4).reshape(full)


def _slots_from_full(name, g):
    full = FULL_SHAPES[name]
    kind = BIG[name]
    if kind == "col":
        l, r, cdim = full
        return g.reshape(l, r, 4, cdim // 4).transpose(2, 0, 1, 3).reshape(4, l * r, cdim // 4)
    if kind == "row":
        l, r, cdim = full
        return g.reshape(l, 4, r // 4, cdim).transpose(1, 0, 2, 3).reshape(4, l * (r // 4), cdim)
    l, h, r, cdim = full
    return g.reshape(l, h, 4, r // 4, cdim).transpose(2, 0, 1, 3, 4).reshape(4, l * h * (r // 4), cdim)


def _pack_rows(arrs, rows):
    parts, used = [], 0
    for a in arrs:
        r = _pack_tile_rows(a.size)
        parts.append(jnp.pad(a.reshape(-1), (0, r * 128 - a.size)).reshape(r, 128))
        used += r
    assert rows >= used and (rows - used) % 8 == 0, (rows, used)
    if rows > used:
        parts.append(jnp.zeros((rows - used, 128), f32))
    return jnp.concatenate(parts, axis=0)


def _pack_tile_rows(size):
    return -(-size // 1024) * 8


def _unpack_rows(pack, shapes):
    out, row = [], 0
    for s in shapes:
        size = 1
        for d in s:
            size *= d
        r = _pack_tile_rows(size)
        out.append(pack[row:row + r].reshape(-1)[:size].reshape(s))
        row += r
    return out


def kernel(x, norm_mix, norm_ffn, norm_final, ab_w_in, gdn_conv_w, gdn_a_log, gdn_dt_bias, gdn_norm, hgrn_lower_bounds, hgrn_norm, ab_w_out, c_w_in, c_conv_w, c_conv_b, c_gate_a_w, c_gate_a_b, c_gate_x_w, c_gate_x_b, c_lambda, c_w_out, ffn_w_up, ffn_conv_w, ffn_conv_b, ffn_w_down, loss_target, m_norm_mix, m_norm_ffn, m_norm_final, m_ab_w_in, m_gdn_conv_w, m_gdn_a_log, m_gdn_dt_bias, m_gdn_norm, m_hgrn_lower_bounds, m_hgrn_norm, m_ab_w_out, m_c_w_in, m_c_conv_w, m_c_conv_b, m_c_gate_a_w, m_c_gate_a_b, m_c_gate_x_w, m_c_gate_x_b, m_c_lambda, m_c_w_out, m_ffn_w_up, m_ffn_conv_w, m_ffn_conv_b, m_ffn_w_down, v_norm_mix, v_norm_ffn, v_norm_final, v_ab_w_in, v_gdn_conv_w, v_gdn_a_log, v_gdn_dt_bias, v_gdn_norm, v_hgrn_lower_bounds, v_hgrn_norm, v_ab_w_out, v_c_w_in, v_c_conv_w, v_c_conv_b, v_c_gate_a_w, v_c_gate_a_b, v_c_gate_x_w, v_c_gate_x_b, v_c_lambda, v_c_w_out, v_ffn_w_up, v_ffn_conv_w, v_ffn_conv_b, v_ffn_w_down):
    w = dict(zip(WEIGHTS, (norm_mix, norm_ffn, norm_final, ab_w_in, gdn_conv_w, gdn_a_log, gdn_dt_bias, gdn_norm, hgrn_lower_bounds, hgrn_norm, ab_w_out, c_w_in, c_conv_w, c_conv_b, c_gate_a_w, c_gate_a_b, c_gate_x_w, c_gate_x_b, c_lambda, c_w_out, ffn_w_up, ffn_conv_w, ffn_conv_b, ffn_w_down)))
    m = dict(zip(WEIGHTS, (m_norm_mix, m_norm_ffn, m_norm_final, m_ab_w_in, m_gdn_conv_w, m_gdn_a_log, m_gdn_dt_bias, m_gdn_norm, m_hgrn_lower_bounds, m_hgrn_norm, m_ab_w_out, m_c_w_in, m_c_conv_w, m_c_conv_b, m_c_gate_a_w, m_c_gate_a_b, m_c_gate_x_w, m_c_gate_x_b, m_c_lambda, m_c_w_out, m_ffn_w_up, m_ffn_conv_w, m_ffn_conv_b, m_ffn_w_down)))
    v = dict(zip(WEIGHTS, (v_norm_mix, v_norm_ffn, v_norm_final, v_ab_w_in, v_gdn_conv_w, v_gdn_a_log, v_gdn_dt_bias, v_gdn_norm, v_hgrn_lower_bounds, v_hgrn_norm, v_ab_w_out, v_c_w_in, v_c_conv_w, v_c_conv_b, v_c_gate_a_w, v_c_gate_a_b, v_c_gate_x_w, v_c_gate_x_b, v_c_lambda, v_c_w_out, v_ffn_w_up, v_ffn_conv_w, v_ffn_conv_b, v_ffn_w_down)))
    big = list(BIG)
    chip = 2 * lax.axis_index("x") + lax.axis_index("y")
    ids = jnp.stack([chip, lax.axis_index("c")]).astype(jnp.int32)

    def layer_parts(l):
        j = l // 2
        if l % 2 == 0:
            mix = [("ab_w_in", j), ("ab_w_out", j)]
        else:
            mix = [("c_w_in", j), ("c_gate_a_w", j), ("c_gate_x_w", j), ("c_w_out", j)]
        return mix, [("ffn_w_up", l), ("ffn_w_down", l)]

    def layer_shard(n, i):
        s = w[n][i]
        return s.reshape(-1, s.shape[-1]).astype(bf16)

    small_shard_shapes = [w[n].shape for n in SMALL_SHARDED]
    small_pack = _pack_rows([w[n] for n in SMALL_SHARDED], 160)
    mix0, ffn0 = layer_parts(0)
    gathered0 = _all_gather_chips([layer_shard(n, i) for n, i in mix0] + [small_pack], "all_gather_mixer0")
    gathers = {}

    def start_gather(l, after):
        parts = ffn0 if l == 0 else sum(layer_parts(l), [])
        shards = [layer_shard(n, i) for n, i in parts]
        gathers[l] = _split_start(_gather_copies, shards, [_sds((N_CHIPS,) + s.shape, bf16) for s in shards],
                                  "gather_start_%d" % l, after)
        return gathers[l]["token"]

    first_tokens = (start_gather(0, ()), start_gather(1, ()))
    fw = {}
    per_chip = [_unpack_rows(gathered0[-1][k], small_shard_shapes) for k in range(N_CHIPS)]
    for i, n in enumerate(SMALL_SHARDED):
        fw[n] = jnp.concatenate([per_chip[k][i] for k in range(N_CHIPS)], axis=-1)
    for n in SMALL:
        if n not in fw:
            fw[n] = w[n]
    wt = _layout_weights(fw)
    for n in big:
        wt[n] = [None] * FULL_SHAPES[n][0]

    def pre_layer(l, part, x_l):
        deps = ()
        if l == 0 and part == "mix":
            parts, lands, deps = mix0, gathered0[:len(mix0)], first_tokens
        elif (l == 0) == (part == "ffn"):
            parts = ffn0 if l == 0 else sum(layer_parts(l), [])
            _, lands = _split_wait(_gather_copies, gathers[l], [x_l], "gather_wait_%d" % l)
            if l + 2 <= 3:
                deps = (start_gather(l + 2, lands[:1]),)
        else:
            return deps
        for (n, i), slots in zip(parts, lands):
            wt[n][i] = _layer_full(n, slots)
        return deps

    scatters = []

    def post_grads(l, part, g):
        parts = layer_parts(l)[0 if part == "mix" else 1]
        slots = [_layer_slots(n, g[n][i]) for n, i in parts]
        st = _split_start(_scatter_copies, slots, [_sds((N_REL, s.shape[1] // 2, s.shape[2]), bf16) for s in slots],
                          "scatter_start_%d_%s" % (l, part))
        scatters.append((parts, st, "scatter_wait_%d_%s" % (l, part)))
        return (st["token"],)

    loss, dx, g = _local_step(x[0], loss_target[0], wt, pre_layer, post_grads)
    gf = _unlayout_grads(g)
    loss = lax.psum(loss[0, 0], ("x", "y", "c"))

    f = {n: None for n in big}
    for parts, st, wait_name in scatters:
        gbs, lands = _split_wait(_scatter_copies, st, [dx], wait_name)
        for (n, i), gb, land in zip(parts, gbs, lands):
            f[n] = _sum_pieces(gb, land, ids, f[n], i, FULL_SHAPES[n][0], "rs_sum")
    filled = _sibling_fill_blocks([f[n] for n in big], [FULL_SHAPES[n][0] for n in big], "rs_sibling_fill")
    g_shard = dict(zip(big, filled))

    small_full_shapes = [FULL_SHAPES[n] for n in SMALL]
    small_sum = _all_reduce_small(_pack_rows([gf[n] for n in SMALL], 688), "all_reduce_small")
    g_small = dict(zip(SMALL, _unpack_rows(small_sum, small_full_shapes)))
    for n in SMALL_SHARDED:
        width = w[n].shape[-1]
        g_small[n] = lax.dynamic_slice_in_dim(g_small[n], chip * width, width, axis=-1)

    out_g, out_d, out_m, out_v = {}, {}, {}, {}
    for n in big:
        shp = w[n].shape
        as3d = lambda t: t.reshape((-1,) + shp[-2:])
        res = _adamw_big(as3d(w[n]), g_shard[n], as3d(m[n]), as3d(v[n]), "adamw_" + n)
        out_g[n], out_d[n], out_m[n], out_v[n] = (t.reshape(shp) for t in res)
    as2d = lambda t: t.reshape(-1, t.shape[-1])
    sg, sd, sm, sv = _adamw_small([as2d(w[n]) for n in SMALL], [as2d(g_small[n]) for n in SMALL],
                                  [as2d(m[n]) for n in SMALL], [as2d(v[n]) for n in SMALL], "adamw_small")
    for i, n in enumerate(SMALL):
        out_g[n], out_d[n], out_m[n], out_v[n] = (t[i].reshape(w[n].shape) for t in (sg, sd, sm, sv))
    return (loss, dx[None], *[out_g[n] for n in WEIGHTS], *[out_d[n] for n in WEIGHTS],
            *[out_m[n] for n in WEIGHTS], *[out_v[n] for n in WEIGHTS])
```

```python
import functools

import jax
import jax.numpy as jnp
from jax import lax
from jax.experimental import pallas as pl
from jax.experimental.pallas import tpu as pltpu

f32 = jnp.float32
bf16 = jnp.bfloat16
HI = lax.Precision.HIGHEST

D = 1024
EPS = 1e-6
F_FLOOR = 1e-30
GDN_CHUNK = 64
GDN_INTRA_CHUNKS = 1
HGRN_CHUNK = 16
HGRN_STEP = 128
HEAD = 128
NH = 4
LRU_BLOCK = 256
D_FF = 2816
RG_C = 8.0
C_QKV, C_Z, C_QB, C_FB, C_IB, C_GB, C_BA = 0, 1536, 2048, 2560, 3072, 3584, 4096
AB_COLS_PAD = 4224
TT = 256
N_SLOTS = 4
VMEM_LIMIT = 56 * 1024 * 1024

ADAM_LR, ADAM_B1, ADAM_B2, ADAM_EPS, ADAM_WD, ADAM_STEP = 0.001, 0.9, 0.999, 1e-08, 0.01, 10


def _cp(**kw):
    return pltpu.CompilerParams(vmem_limit_bytes=VMEM_LIMIT, **kw)


def _sds(shape, dtype):
    return jax.ShapeDtypeStruct(shape, dtype)


def _dot(a, b, dims, precision=None):
    return lax.dot_general(a, b, (dims, ((), ())), precision=precision, preferred_element_type=f32)


NN = ((1,), (0,))
NT = ((1,), (1,))
TN = ((0,), (0,))


def _rms(x, g):
    return x * lax.rsqrt(jnp.mean(x * x, axis=-1, keepdims=True) + EPS) * g


def _silu(x):
    return x * jax.nn.sigmoid(x)


def _mm(a, b, mode, tm, tn, out_dtype, name):
    if mode == "nn":
        (m, k), n = a.shape, b.shape[1]
        a_spec = pl.BlockSpec((tm, k), lambda i, j: (i, 0))
        b_spec = pl.BlockSpec((k, tn), lambda i, j: (0, j))
        dims = NN
    elif mode == "nt":
        (m, k), n = a.shape, b.shape[0]
        a_spec = pl.BlockSpec((tm, k), lambda i, j: (i, 0))
        b_spec = pl.BlockSpec((tn, k), lambda i, j: (j, 0))
        dims = NT
    else:
        (k, m), n = a.shape, b.shape[1]
        a_spec = pl.BlockSpec((k, tm), lambda i, j: (0, i))
        b_spec = pl.BlockSpec((k, tn), lambda i, j: (0, j))
        dims = TN
    assert m % tm == 0 and n % tn == 0, (name, m, n, tm, tn)

    def body(a_ref, b_ref, o_ref):
        o_ref[...] = _dot(a_ref[...], b_ref[...], dims).astype(out_dtype)

    return pl.pallas_call(
        body, grid=(m // tm, n // tn), in_specs=[a_spec, b_spec],
        out_specs=pl.BlockSpec((tm, tn), lambda i, j: (i, j)),
        out_shape=_sds((m, n), out_dtype), compiler_params=_cp(), name=name)(a, b)


def _mm_res_norm(a, b, res, gain, name, tm=512):
    (m, k), n = a.shape, b.shape[1]
    tm = min(tm, m)
    assert n == D and m % tm == 0, (name, m, n)

    def body(a_ref, b_ref, r_ref, *rest):
        xv = r_ref[...] + _dot(a_ref[...], b_ref[...], NN)
        if gain is None:
            rest[0][...] = xv
        else:
            g_ref, x_ref, h_ref = rest
            x_ref[...] = xv
            h_ref[...] = _rms(xv, g_ref[...]).astype(bf16)

    row = pl.BlockSpec((tm, D), lambda i: (i, 0))
    in_specs = [pl.BlockSpec((tm, k), lambda i: (i, 0)), pl.BlockSpec((k, D), lambda i: (0, 0)), row]
    args = [a, b, res]
    if gain is None:
        out_specs, out_shape = row, _sds((m, D), f32)
    else:
        in_specs.append(pl.BlockSpec((1, D), lambda i: (0, 0)))
        args.append(gain)
        out_specs, out_shape = [row, row], [_sds((m, D), f32), _sds((m, D), bf16)]
    return pl.pallas_call(body, grid=(m // tm,), in_specs=in_specs, out_specs=out_specs, out_shape=out_shape,
                          compiler_params=_cp(), name=name)(*args)


def _tile(n, cands):
    for c in cands:
        if n % c == 0:
            return c
    raise ValueError(n)


def _mm_auto(a, b, mode, out_dtype, name):
    m = a.shape[1] if mode == "tn" else a.shape[0]
    n = b.shape[0] if mode == "nt" else b.shape[1]
    return _mm(a, b, mode, _tile(m, (512, 256, 128)), _tile(n, (1024, 1408, 512, 384, 256, 128)), out_dtype, name)


def _mm_nn_slots(a, bs, out_dtype, name, after=()):
    (m, k), w = a.shape, bs.shape[2]
    tm = _tile(m, (1024, 512, 256, 128))

    def body(a_ref, b_ref, *rest):
        o_ref = rest[len(after)]
        o_ref[...] = _dot(a_ref[...], b_ref[...], NN).astype(out_dtype)

    return pl.pallas_call(
        body, grid=(m // tm, N_SLOTS),
        in_specs=[pl.BlockSpec((tm, k), lambda i, j: (i, 0)), pl.BlockSpec((None, k, w), lambda i, j: (j, 0, 0))]
        + [pl.BlockSpec(memory_space=pl.ANY)] * len(after),
        out_specs=pl.BlockSpec((tm, w), lambda i, j: (i, j)),
        out_shape=_sds((m, N_SLOTS * w), out_dtype), compiler_params=_cp(), name=name)(a, bs, *after)


def _mm_nt_slots(a, bs, name):
    (m, _), (_, r, w) = a.shape, bs.shape
    tm, tn = _tile(m, (512, 256, 128)), _tile(r, (1024, 512, 256, 128))

    def body(a_ref, b_ref, o_ref):
        acc = _dot(a_ref[:, 0:w], b_ref[0], NT)
        for s in range(1, N_SLOTS):
            acc = acc + _dot(a_ref[:, s * w:(s + 1) * w], b_ref[s], NT)
        o_ref[...] = acc

    return pl.pallas_call(
        body, grid=(m // tm, r // tn),
        in_specs=[pl.BlockSpec((tm, N_SLOTS * w), lambda i, j: (i, 0)), pl.BlockSpec((N_SLOTS, tn, w), lambda i, j: (0, j, 0))],
        out_specs=pl.BlockSpec((tm, tn), lambda i, j: (i, j)),
        out_shape=_sds((m, r), f32), compiler_params=_cp(), name=name)(a, bs)


def _mm_nt_rmsbwd(a, b, x, gain, dres, name, after=()):
    m, k = a.shape
    slots = b.ndim == 3
    assert b.shape[-2] == D
    tm = _tile(m, (512, 256, 128)) if k < 4096 else _tile(m, (256, 128))

    def body(a_ref, b_ref, x_ref, g_ref, r_ref, *rest):
        dx_ref, dxb_ref, dg_ref = rest[len(after):]
        if slots:
            w = b.shape[2]
            dh = _dot(a_ref[:, 0:w], b_ref[0], NT)
            for s in range(1, N_SLOTS):
                dh = dh + _dot(a_ref[:, s * w:(s + 1) * w], b_ref[s], NT)
        else:
            dh = _dot(a_ref[...], b_ref[...], NT)
        _, vjp = jax.vjp(_rms, x_ref[...], g_ref[...])
        dx, dg = vjp(dh)
        dx = dx + r_ref[...]
        dx_ref[...] = dx
        dxb_ref[...] = dx.astype(bf16)

        @pl.when(pl.program_id(0) == 0)
        def _():
            dg_ref[...] = jnp.zeros_like(dg_ref)

        dg_ref[...] += dg

    row = pl.BlockSpec((tm, D), lambda i: (i, 0))
    vec = pl.BlockSpec((1, D), lambda i: (0, 0))
    b_spec = pl.BlockSpec(b.shape, (lambda i: (0, 0, 0)) if slots else (lambda i: (0, 0)))
    return pl.pallas_call(
        body, grid=(m // tm,),
        in_specs=[pl.BlockSpec((tm, k), lambda i: (i, 0)), b_spec, row, vec, row] + [pl.BlockSpec(memory_space=pl.ANY)] * len(after),
        out_specs=[row, row, vec],
        out_shape=[_sds((m, D), f32), _sds((m, D), bf16), _sds((1, D), f32)],
        compiler_params=_cp(), name=name)(a, b, x, gain, dres, *after)


def _mm_tn_slots(a, b, name):
    (k, m), w = a.shape, b.shape[1] // N_SLOTS
    tm = _tile(m, (512, 256, 128))

    def body(a_ref, b_ref, o_ref):
        o_ref[...] = _dot(a_ref[...], b_ref[...], TN).astype(bf16)

    return pl.pallas_call(
        body, grid=(N_SLOTS, m // tm),
        in_specs=[pl.BlockSpec((k, tm), lambda j, i: (0, i)), pl.BlockSpec((k, w), lambda j, i: (0, j))],
        out_specs=pl.BlockSpec((None, tm, w), lambda j, i: (j, i, 0)),
        out_shape=_sds((N_SLOTS, m, w), bf16), compiler_params=_cp(), name=name)(a, b)


def _rmsnorm_fwd(x, gain, name, after=()):
    t = x.shape[0]

    def body(x_ref, g_ref, *rest):
        h_ref = rest[len(after)]
        h_ref[...] = _rms(x_ref[...], g_ref[...]).astype(bf16)

    return pl.pallas_call(
        body, grid=(t // TT,),
        in_specs=[pl.BlockSpec((TT, D), lambda i: (i, 0)), pl.BlockSpec((1, D), lambda i: (0, 0))]
        + [pl.BlockSpec(memory_space=pl.ANY)] * len(after),
        out_specs=pl.BlockSpec((TT, D), lambda i: (i, 0)),
        out_shape=_sds((t, D), bf16), compiler_params=_cp(), name=name)(x, gain, *after)


def _rmsnorm_bwd(x, gain, dh, dres, name, after=()):
    t = x.shape[0]

    def body(x_ref, g_ref, dh_ref, dres_ref, *rest):
        dx_ref, dxb_ref, dg_ref = rest[len(after):]
        _, vjp = jax.vjp(_rms, x_ref[...], g_ref[...])
        dx, dg = vjp(dh_ref[...])
        dx = dx + dres_ref[...]
        dx_ref[...] = dx
        dxb_ref[...] = dx.astype(bf16)

        @pl.when(pl.program_id(0) == 0)
        def _():
            dg_ref[...] = jnp.zeros_like(dg_ref)

        dg_ref[...] += dg

    row = pl.BlockSpec((TT, D), lambda i: (i, 0))
    vec = pl.BlockSpec((1, D), lambda i: (0, 0))
    return pl.pallas_call(
        body, grid=(t // TT,), in_specs=[row, vec, row, row] + [pl.BlockSpec(memory_space=pl.ANY)] * len(after),
        out_specs=[row, row, vec],
        out_shape=[_sds((t, D), f32), _sds((t, D), bf16), _sds((1, D), f32)],
        compiler_params=_cp(), name=name)(x, gain, dh, dres, *after)


def _residual_add(x, y, name):
    t = x.shape[0]

    def body(x_ref, y_ref, o_ref):
        o_ref[...] = x_ref[...] + y_ref[...]

    row = pl.BlockSpec((TT, D), lambda i: (i, 0))
    return pl.pallas_call(body, grid=(t // TT,), in_specs=[row, row], out_specs=row,
                          out_shape=_sds((t, D), f32), compiler_params=_cp(), name=name)(x, y)


def _final_loss(x, gain, target, name):
    t = x.shape[0]

    def loss_fn(xv, gv, tv):
        e = _rms(xv, gv) - tv
        return 0.5 * jnp.sum(jnp.mean(e * e, axis=-1))

    def body(x_ref, g_ref, t_ref, loss_ref, dx_ref, dxb_ref, dg_ref):
        val, (dx, dg) = jax.value_and_grad(loss_fn, argnums=(0, 1))(x_ref[...], g_ref[...], t_ref[...])

        @pl.when(pl.program_id(0) == 0)
        def _():
            dg_ref[...] = jnp.zeros_like(dg_ref)
            loss_ref[...] = jnp.zeros_like(loss_ref)

        dx_ref[...] = dx
        dxb_ref[...] = dx.astype(bf16)
        dg_ref[...] += dg
        loss_ref[...] += jnp.full((1, 128), val, f32)

    row = pl.BlockSpec((TT, D), lambda i: (i, 0))
    vec = pl.BlockSpec((1, D), lambda i: (0, 0))
    return pl.pallas_call(
        body, grid=(t // TT,), in_specs=[row, vec, row],
        out_specs=[pl.BlockSpec((1, 128), lambda i: (0, 0)), row, row, vec],
        out_shape=[_sds((1, 128), f32), _sds((t, D), f32), _sds((t, D), bf16), _sds((1, D), f32)],
        compiler_params=_cp(), name=name)(x, gain, target)


def _conv_fwd(x, col0, c, w, b, name, tc=256, val=None, val_col0=0):
    t = x.shape[0]
    width = w.shape[0]
    nt = t // TT
    hb = TT // 8

    def body(*refs):
        if val is None:
            x_ref, xh_ref, w_ref, b_ref, o_ref, xp = refs
        else:
            x_ref, xh_ref, w_ref, b_ref, v_ref, o_ref, act_ref, xp = refs
        i = pl.program_id(1)
        xp[0:8, :] = jnp.where(i == 0, 0.0, xh_ref[...])
        xp[8:, :] = x_ref[...]
        acc = jnp.zeros((TT, tc), f32) + b_ref[...]
        for k in range(width):
            acc = acc + w_ref[k:k + 1, :] * xp[pl.ds(8 - (width - 1) + k, TT), :]
        o_ref[...] = acc
        if val is not None:
            act_ref[...] = (_silu(acc) * v_ref[...]).astype(bf16)

    in_specs = [
        pl.BlockSpec((TT, tc), lambda j, i: (i, j + col0)),
        pl.BlockSpec((8, tc), lambda j, i: (jnp.maximum(i * hb - 1, 0), j + col0)),
        pl.BlockSpec((width, tc), lambda j, i: (0, j)),
        pl.BlockSpec((1, tc), lambda j, i: (0, j)),
    ]
    args = [x, x, w, b]
    out_specs = [pl.BlockSpec((TT, tc), lambda j, i: (i, j))]
    out_shape = [_sds((t, c), f32)]
    if val is not None:
        in_specs.append(pl.BlockSpec((TT, tc), lambda j, i: (i, j + val_col0)))
        args.append(val)
        out_specs.append(pl.BlockSpec((TT, tc), lambda j, i: (i, j)))
        out_shape.append(_sds((t, c), bf16))
    res = pl.pallas_call(
        body, grid=(c // tc, nt), in_specs=in_specs, out_specs=out_specs, out_shape=out_shape,
        scratch_shapes=[pltpu.VMEM((TT + 8, tc), f32)], compiler_params=_cp(), name=name)(*args)
    return res[0] if val is None else res


def _conv_bwd(dc, x, col0, w, name, tc=256, dx_dtype=bf16, into=None):
    t, c = dc.shape
    width = w.shape[0]
    nt = t // TT
    hb = TT // 8

    def body(dc_ref, dcn_ref, x_ref, xh_ref, w_ref, *rest):
        dx_ref, dw_ref, db_ref, dcp, xp = rest[(0 if into is None else 1):]
        i = pl.program_id(1)
        dcv = dc_ref[...]
        dcp[0:TT, :] = dcv
        dcp[TT:, :] = jnp.where(i == nt - 1, 0.0, dcn_ref[...])
        xp[0:8, :] = jnp.where(i == 0, 0.0, xh_ref[...])
        xp[8:, :] = x_ref[...]

        @pl.when(i == 0)
        def _():
            dw_ref[...] = jnp.zeros_like(dw_ref)
            db_ref[...] = jnp.zeros_like(db_ref)

        acc = jnp.zeros((TT, tc), f32)
        for k in range(width):
            acc = acc + w_ref[k:k + 1, :] * dcp[pl.ds((width - 1) - k, TT), :]
            dw_ref[k:k + 1, :] += jnp.sum(dcv * xp[pl.ds(8 - (width - 1) + k, TT), :], axis=0, keepdims=True)
        dx_ref[...] = acc.astype(dx_dtype)
        db_ref[...] += jnp.sum(dcv, axis=0, keepdims=True)

    in_specs = [
        pl.BlockSpec((TT, tc), lambda j, i: (i, j)),
        pl.BlockSpec((8, tc), lambda j, i: (jnp.minimum((i + 1) * hb, t // 8 - 1), j)),
        pl.BlockSpec((TT, tc), lambda j, i: (i, j + col0)),
        pl.BlockSpec((8, tc), lambda j, i: (jnp.maximum(i * hb - 1, 0), j + col0)),
        pl.BlockSpec((width, tc), lambda j, i: (0, j)),
    ]
    out_specs = [
        pl.BlockSpec((TT, tc), lambda j, i: (i, j)),
        pl.BlockSpec((8, tc), lambda j, i: (0, j)),
        pl.BlockSpec((1, tc), lambda j, i: (0, j)),
    ]
    args, aliases, dx_shape = [dc, dc, x, x, w], {}, _sds((t, c), dx_dtype)
    if into is not None:
        in_specs.append(pl.BlockSpec(memory_space=pl.ANY))
        args.append(into)
        aliases, dx_shape = {5: 0}, _sds(into.shape, into.dtype)
    return pl.pallas_call(
        body, grid=(c // tc, nt), in_specs=in_specs, out_specs=out_specs,
        out_shape=[dx_shape, _sds((8, c), f32), _sds((1, c), f32)], input_output_aliases=aliases,
        scratch_shapes=[pltpu.VMEM((TT + 8, tc), f32), pltpu.VMEM((TT + 8, tc), f32)],
        compiler_params=_cp(), name=name)(*args)


def _bdot_impl(a, b, dims):
    return _dot(a.astype(bf16), b.astype(bf16), dims)


@functools.partial(jax.custom_vjp, nondiff_argnums=(2,))
def _bdot(a, b, dims):
    return _bdot_impl(a, b, dims)


def _bdot_fwd(a, b, dims):
    return _bdot_impl(a, b, dims), (a, b)


def _bdot_bwd(dims, res, ct):
    a, b = res
    if dims == NN:
        return _bdot_impl(ct, b, NT), _bdot_impl(a, ct, TN)
    if dims == NT:
        return _bdot_impl(ct, b, NN), _bdot_impl(ct, a, TN)
    return _bdot_impl(b, ct, NT), _bdot_impl(a, ct, NN)


_bdot.defvjp(_bdot_fwd, _bdot_bwd)


def _split2(a):
    hi = a.astype(bf16)
    return hi, (a - hi.astype(f32)).astype(bf16)


def _dot3_impl(a, b, dims):
    a_hi, a_lo = _split2(a)
    b_hi, b_lo = _split2(b)
    return (_dot(a_hi, b_hi, dims) + _dot(a_hi, b_lo, dims)) + _dot(a_lo, b_hi, dims)


@functools.partial(jax.custom_vjp, nondiff_argnums=(2,))
def _dot3(a, b, dims):
    return _dot3_impl(a, b, dims)


def _dot3_fwd(a, b, dims):
    return _dot3_impl(a, b, dims), (a, b)


def _dot3_bwd(dims, res, ct):
    a, b = res
    if dims == NN:
        return _dot3_impl(ct, b, NT), _dot3_impl(a, ct, TN)
    if dims == NT:
        return _dot3_impl(ct, b, NN), _dot3_impl(ct, a, TN)
    return _dot3_impl(b, ct, NT), _dot3_impl(a, ct, NN)


_dot3.defvjp(_dot3_fwd, _dot3_bwd)


def _tril_dot_impl(tril, x, dims):
    t = tril.astype(bf16)
    x1 = x.astype(bf16)
    r1 = x - x1.astype(f32)
    x2 = r1.astype(bf16)
    x3 = (r1 - x2.astype(f32)).astype(bf16)
    return (_dot(t, x3, dims) + _dot(t, x2, dims)) + _dot(t, x1, dims)


@jax.custom_vjp
def _cumsum_rows(tril, x):
    return _tril_dot_impl(tril, x, NN)


def _cumsum_rows_fwd(tril, x):
    return _tril_dot_impl(tril, x, NN), tril


def _cumsum_rows_bwd(tril, ct):
    return jnp.zeros_like(tril), _tril_dot_impl(tril, ct, TN)


_cumsum_rows.defvjp(_cumsum_rows_fwd, _cumsum_rows_bwd)


def _gdn_intra(cq, ck, cv, ba, alog_v, dtb_v, hd):
    c = GDN_CHUNK
    lane = lax.broadcasted_iota(jnp.int32, (1, HEAD), 1)
    mb = (lane == hd).astype(f32)
    ma = (lane == hd + NH).astype(f32)
    beta = jax.nn.sigmoid(jnp.sum(ba * mb, axis=1, keepdims=True))
    alpha = jnp.sum(ba * ma, axis=1, keepdims=True)
    alog = jnp.sum(alog_v * ma, axis=1, keepdims=True)
    dtb = jnp.sum(dtb_v * ma, axis=1, keepdims=True)
    g = -jnp.exp(alog) * jax.nn.softplus(alpha + dtb)
    q = _silu(cq)
    q = q * lax.rsqrt(jnp.sum(q * q, axis=-1, keepdims=True) + EPS) * (HEAD ** -0.5)
    k = _silu(ck)
    k = k * lax.rsqrt(jnp.sum(k * k, axis=-1, keepdims=True) + EPS)
    v = _silu(cv)
    row = lax.broadcasted_iota(jnp.int32, (c, c), 0)
    col = lax.broadcasted_iota(jnp.int32, (c, c), 1)
    causal = row >= col
    tril = causal.astype(f32)
    gc = _cumsum_rows(tril, jnp.broadcast_to(g, (c, HEAD)))
    gcc = _cumsum_rows(tril, jnp.broadcast_to(g, (c, c)))
    diff = jnp.where(causal, gcc - gcc.T, 0.0)
    decay = jnp.where(causal, jnp.exp(diff), 0.0)
    kb = k * beta
    nmat = -jnp.where(row > col, _bdot(kb, k, NT) * decay, 0.0)
    egc = jnp.exp(gc)
    sol = jnp.concatenate([v * beta, kb * egc], axis=1)
    p = nmat
    for it in range(6):
        sol = sol + _dot3(p, sol, NN)
        if it < 5:
            p = _dot3(p, p, NN)
    u, w = sol[:, :HEAD], sol[:, HEAD:]
    attn = _bdot(q, k, NT) * decay
    rowv = lax.broadcasted_iota(jnp.int32, (c, 1), 0)
    gc_last = jnp.sum(jnp.where(rowv == c - 1, gc, 0.0), axis=0, keepdims=True)
    return u, w, q * egc, k * jnp.exp(gc_last - gc), attn, jnp.exp(gc_last)


def _gdn_intra4(cq, ck, cv, ba, alog_v, dtb_v):
    c = GDN_CHUNK
    hs = range(NH)
    lane = lax.broadcasted_iota(jnp.int32, (1, HEAD), 1)
    mb = [(lane == h).astype(f32) for h in hs]
    ma = [(lane == h + NH).astype(f32) for h in hs]
    beta = [jax.nn.sigmoid(jnp.sum(ba * mb[h], axis=1, keepdims=True)) for h in hs]
    alpha = [jnp.sum(ba * ma[h], axis=1, keepdims=True) for h in hs]
    alog = [jnp.sum(alog_v * ma[h], axis=1, keepdims=True) for h in hs]
    dtb = [jnp.sum(dtb_v * ma[h], axis=1, keepdims=True) for h in hs]
    g = [-jnp.exp(alog[h]) * jax.nn.softplus(alpha[h] + dtb[h]) for h in hs]
    q = [_silu(cq[h]) for h in hs]
    q = [q[h] * lax.rsqrt(jnp.sum(q[h] * q[h], axis=-1, keepdims=True) + EPS) * (HEAD ** -0.5) for h in hs]
    k = [_silu(ck[h]) for h in hs]
    k = [k[h] * lax.rsqrt(jnp.sum(k[h] * k[h], axis=-1, keepdims=True) + EPS) for h in hs]
    v = [_silu(cv[h]) for h in hs]
    row = lax.broadcasted_iota(jnp.int32, (c, c), 0)
    col = lax.broadcasted_iota(jnp.int32, (c, c), 1)
    causal = row >= col
    tril = causal.astype(f32)
    gc = [_cumsum_rows(tril, jnp.broadcast_to(g[h], (c, HEAD))) for h in hs]
    gcc = [_cumsum_rows(tril, jnp.broadcast_to(g[h], (c, c))) for h in hs]
    decay = [jnp.where(causal, jnp.exp(jnp.where(causal, gcc[h] - gcc[h].T, 0.0)), 0.0) for h in hs]
    kb = [k[h] * beta[h] for h in hs]
    kk = [_bdot(kb[h], k[h], NT) for h in hs]
    p = [-jnp.where(row > col, kk[h] * decay[h], 0.0) for h in hs]
    egc = [jnp.exp(gc[h]) for h in hs]
    sol = [jnp.concatenate([v[h] * beta[h], kb[h] * egc[h]], axis=1) for h in hs]
    for it in range(6):
        upd = [_dot3(p[h], sol[h], NN) for h in hs]
        sol = [sol[h] + upd[h] for h in hs]
        if it < 5:
            p = [_dot3(p[h], p[h], NN) for h in hs]
    qk = [_bdot(q[h], k[h], NT) for h in hs]
    attn = [qk[h] * decay[h] for h in hs]
    rowv = lax.broadcasted_iota(jnp.int32, (c, 1), 0)
    gc_last = [jnp.sum(jnp.where(rowv == c - 1, gc[h], 0.0), axis=0, keepdims=True) for h in hs]
    return ([sol[h][:, :HEAD] for h in hs], [sol[h][:, HEAD:] for h in hs], [q[h] * egc[h] for h in hs],
            [k[h] * jnp.exp(gc_last[h] - gc[h]) for h in hs], attn, [jnp.exp(gc_last[h]) for h in hs])


def _gdn_seq(u, w, q_dec, k_dec, attn, dl, z, s, gain):
    v_new = u - _bdot(w, s, NN)
    o = _bdot(q_dec, s, NN) + _bdot(attn, v_new, NN)
    s_new = s * dl + _bdot(k_dec, v_new, TN)
    return _rms(o, gain) * _silu(z), s_new


def _gdn_seq4(u, w, q_dec, k_dec, attn, dl, z, s, gain):
    hs = range(NH)
    ws = [_bdot(w[h], s[h], NN) for h in hs]
    qs = [_bdot(q_dec[h], s[h], NN) for h in hs]
    v_new = [u[h] - ws[h] for h in hs]
    av = [_bdot(attn[h], v_new[h], NN) for h in hs]
    kv = [_bdot(k_dec[h], v_new[h], TN) for h in hs]
    o = [_rms(qs[h] + av[h], gain) * _silu(z[h]) for h in hs]
    return o, [s[h] * dl[h] + kv[h] for h in hs]


def _hsl(h):
    return slice(h * HEAD, (h + 1) * HEAD)


def _gdn2_fwd(conv, p, alog_v, dtb_v, gain, name):
    t = conv.shape[0]
    c = GDN_CHUNK
    nch = t // c
    w512 = NH * HEAD
    wide = lambda off: pl.BlockSpec((c, w512), lambda n: (n, off))
    vec = pl.BlockSpec((1, HEAD), lambda n: (0, 0))
    attn_spec = pl.BlockSpec((1, NH, c, c), lambda n: (n, 0, 0, 0))
    dl_spec = pl.BlockSpec((1, NH, HEAD), lambda n: (n, 0, 0))

    cps = GDN_INTRA_CHUNKS
    iwide = lambda off: pl.BlockSpec((cps * c, w512), lambda n: (n, off))

    def intra(cq, ck, cv, ba, al, dt, u_ref, w_ref, qd_ref, kd_ref, at_ref, dl_ref):
        for ci in range(cps):
            rows = slice(ci * c, (ci + 1) * c)
            u, w, qd, kd, at, dl = _gdn_intra4([cq[rows, _hsl(h)] for h in range(NH)], [ck[rows, _hsl(h)] for h in range(NH)],
                                               [cv[rows, _hsl(h)] for h in range(NH)], ba[rows, :], al[...], dt[...])
            for h in range(NH):
                u_ref[rows, _hsl(h)] = u[h]
                w_ref[rows, _hsl(h)] = w[h]
                qd_ref[rows, _hsl(h)] = qd[h]
                kd_ref[rows, _hsl(h)] = kd[h]
                at_ref[ci, h] = at[h]
                dl_ref[ci, h:h + 1, :] = dl[h]

    u, w, qd, kd, at, dl = pl.pallas_call(
        intra, grid=(nch // cps,),
        in_specs=[iwide(0), iwide(1), iwide(2), pl.BlockSpec((cps * c, HEAD), lambda n: (n, C_BA // HEAD)), vec, vec],
        out_specs=[iwide(0)] * 4 + [pl.BlockSpec((cps, NH, c, c), lambda n: (n, 0, 0, 0)),
                                    pl.BlockSpec((cps, NH, HEAD), lambda n: (n, 0, 0))],
        out_shape=[_sds((t, w512), f32)] * 4 + [_sds((nch, NH, c, c), f32), _sds((nch, NH, HEAD), f32)],
        compiler_params=_cp(), name=name + "_intra")(conv, conv, conv, p, alog_v, dtb_v)

    def seq(u_ref, w_ref, qd_ref, kd_ref, at_ref, dl_ref, z_ref, gn, o_ref, ss_ref, s_scr):
        @pl.when(pl.program_id(0) == 0)
        def _():
            s_scr[...] = jnp.zeros_like(s_scr)

        hs = range(NH)
        s = [s_scr[h] for h in hs]
        for h in hs:
            ss_ref[0, h] = s[h]
        o, s_new = _gdn_seq4([u_ref[:, _hsl(h)] for h in hs], [w_ref[:, _hsl(h)] for h in hs], [qd_ref[:, _hsl(h)] for h in hs],
                             [kd_ref[:, _hsl(h)] for h in hs], [at_ref[0, h] for h in hs], [dl_ref[0, h:h + 1, :] for h in hs],
                             [z_ref[:, _hsl(h)] for h in hs], s, gn[...])
        for h in hs:
            o_ref[:, _hsl(h)] = o[h].astype(bf16)
            s_scr[h] = s_new[h]

    o, states = pl.pallas_call(
        seq, grid=(nch,),
        in_specs=[wide(0)] * 4 + [attn_spec, dl_spec, wide(C_Z // w512), vec],
        out_specs=[wide(0), pl.BlockSpec((1, NH, HEAD, HEAD), lambda n: (n, 0, 0, 0))],
        out_shape=[_sds((t, w512), bf16), _sds((nch, NH, HEAD, HEAD), f32)],
        scratch_shapes=[pltpu.VMEM((NH, HEAD, HEAD), f32)],
        compiler_params=_cp(), name=name + "_seq")(u, w, qd, kd, at, dl, p, gain)
    return o, dict(u=u, w=w, qd=qd, kd=kd, at=at, dl=dl, states=states)


def _gdn2_bwd(conv, p, alog_v, dtb_v, gain, saved, do, name):
    t = conv.shape[0]
    c = GDN_CHUNK
    nch = t // c
    w512 = NH * HEAD
    rwide = lambda off: pl.BlockSpec((c, w512), lambda n: (nch - 1 - n, off))
    rvec = pl.BlockSpec((1, HEAD), lambda n: (0, 0))
    rattn = pl.BlockSpec((1, NH, c, c), lambda n: (nch - 1 - n, 0, 0, 0))
    rdl = pl.BlockSpec((1, NH, HEAD), lambda n: (nch - 1 - n, 0, 0))

    def seq_bwd(u_ref, w_ref, qd_ref, kd_ref, at_ref, dl_ref, z_ref, gn, ss_ref, do_ref,
                du_ref, dw_ref, dqd_ref, dkd_ref, dat_ref, ddl_ref, dz_ref, dgn_ref, ds_scr):
        @pl.when(pl.program_id(0) == 0)
        def _():
            ds_scr[...] = jnp.zeros_like(ds_scr)
            dgn_ref[...] = jnp.zeros_like(dgn_ref)

        hs = range(NH)
        _, vjp = jax.vjp(_gdn_seq4, [u_ref[:, _hsl(h)] for h in hs], [w_ref[:, _hsl(h)] for h in hs],
                         [qd_ref[:, _hsl(h)] for h in hs], [kd_ref[:, _hsl(h)] for h in hs], [at_ref[0, h] for h in hs],
                         [dl_ref[0, h:h + 1, :] for h in hs], [z_ref[:, _hsl(h)] for h in hs], [ss_ref[0, h] for h in hs], gn[...])
        du, dw, dqd, dkd, dat, ddl, dz, ds, dg = vjp(([do_ref[:, _hsl(h)] for h in hs], [ds_scr[h] for h in hs]))
        for h in hs:
            du_ref[:, _hsl(h)] = du[h]
            dw_ref[:, _hsl(h)] = dw[h]
            dqd_ref[:, _hsl(h)] = dqd[h]
            dkd_ref[:, _hsl(h)] = dkd[h]
            dat_ref[0, h] = dat[h]
            ddl_ref[0, h:h + 1, :] = ddl[h]
            dz_ref[:, _hsl(h)] = dz[h]
            ds_scr[h] = ds[h]
        dgn_ref[...] += dg

    du, dw, dqd, dkd, dat, ddl, dz, dgn = pl.pallas_call(
        seq_bwd, grid=(nch,),
        in_specs=[rwide(0)] * 4 + [rattn, rdl, rwide(C_Z // w512), rvec,
                                   pl.BlockSpec((1, NH, HEAD, HEAD), lambda n: (nch - 1 - n, 0, 0, 0)), rwide(0)],
        out_specs=[rwide(0)] * 4 + [rattn, rdl, rwide(0), rvec],
        out_shape=[_sds((t, w512), f32)] * 4 + [_sds((nch, NH, c, c), f32), _sds((nch, NH, HEAD), f32),
                                                _sds((t, w512), f32), _sds((1, HEAD), f32)],
        scratch_shapes=[pltpu.VMEM((NH, HEAD, HEAD), f32)],
        compiler_params=_cp(), name=name + "_seq")(
            saved["u"], saved["w"], saved["qd"], saved["kd"], saved["at"], saved["dl"], p, gain, saved["states"], do)

    cps = GDN_INTRA_CHUNKS
    wide = lambda off: pl.BlockSpec((cps * c, w512), lambda n: (n, off))
    vec = pl.BlockSpec((1, HEAD), lambda n: (0, 0))
    attn_spec = pl.BlockSpec((cps, NH, c, c), lambda n: (n, 0, 0, 0))
    dl_spec = pl.BlockSpec((cps, NH, HEAD), lambda n: (n, 0, 0))

    def intra_bwd(cq, ck, cv, ba, al, dt, du_ref, dw_ref, dqd_ref, dkd_ref, dat_ref, ddl_ref,
                  dc_ref, dba_ref, dal_ref, ddt_ref):
        @pl.when(pl.program_id(0) == 0)
        def _():
            dal_ref[...] = jnp.zeros_like(dal_ref)
            ddt_ref[...] = jnp.zeros_like(ddt_ref)

        dal = jnp.zeros((1, HEAD), f32)
        ddt = jnp.zeros((1, HEAD), f32)
        for ci in range(cps):
            rows = slice(ci * c, (ci + 1) * c)
            hs = range(NH)
            _, vjp = jax.vjp(_gdn_intra4, [cq[rows, _hsl(h)] for h in hs], [ck[rows, _hsl(h)] for h in hs],
                             [cv[rows, _hsl(h)] for h in hs], ba[rows, :], al[...], dt[...])
            g_q, g_k, g_v, g_ba, g_al, g_dt = vjp((
                [du_ref[rows, _hsl(h)] for h in hs], [dw_ref[rows, _hsl(h)] for h in hs], [dqd_ref[rows, _hsl(h)] for h in hs],
                [dkd_ref[rows, _hsl(h)] for h in hs], [dat_ref[ci, h] for h in hs], [ddl_ref[ci, h:h + 1, :] for h in hs]))
            for h in hs:
                dc_ref[rows, _hsl(h)] = g_q[h]
                dc_ref[rows, _hsl(NH + h)] = g_k[h]
                dc_ref[rows, _hsl(2 * NH + h)] = g_v[h]
            dal = dal + g_al
            ddt = ddt + g_dt
            dba_ref[rows, :] = g_ba
        dal_ref[...] += dal
        ddt_ref[...] += ddt

    dconv, dba, dal, ddt = pl.pallas_call(
        intra_bwd, grid=(nch // cps,),
        in_specs=[wide(0), wide(1), wide(2), pl.BlockSpec((cps * c, HEAD), lambda n: (n, C_BA // HEAD)), vec, vec]
        + [wide(0)] * 4 + [attn_spec, dl_spec],
        out_specs=[pl.BlockSpec((cps * c, 3 * w512), lambda n: (n, 0)), pl.BlockSpec((cps * c, HEAD), lambda n: (n, 0)),
                   vec, vec],
        out_shape=[_sds((t, 3 * w512), f32), _sds((t, HEAD), f32), _sds((1, HEAD), f32), _sds((1, HEAD), f32)],
        compiler_params=_cp(), name=name + "_intra")(conv, conv, conv, p, alog_v, dtb_v, du, dw, dqd, dkd, dat, ddl)
    return dconv, dz, dba, dal, ddt, dgn


def _hgrn_intra(qb, fb, ib, lb):
    c = HGRN_CHUNK
    ns = range(len(qb))
    f = [lb + (1.0 - lb) * jax.nn.sigmoid(fb[i]) for i in ns]
    logf = [jnp.log(jnp.maximum(f[i], F_FLOOR)) for i in ns]
    k = [1.0 - f[i] for i in ns]
    q = [_silu(qb[i]) for i in ns]
    row = lax.broadcasted_iota(jnp.int32, (c, c), 0)
    col = lax.broadcasted_iota(jnp.int32, (c, c), 1)
    tril = (row >= col).astype(f32)
    b = [_cumsum_rows(tril, logf[i]) for i in ns]
    ri = lax.broadcasted_iota(jnp.int32, (c, 1), 0)
    o = [jnp.zeros((c, HEAD), f32) for _ in ns]
    for j in range(c):
        mj = ri == j
        ok = ri >= j
        bj = [jnp.sum(jnp.where(mj, b[i], 0.0), axis=0, keepdims=True) for i in ns]
        kj = [jnp.sum(jnp.where(mj, k[i], 0.0), axis=0, keepdims=True) for i in ns]
        vj = [jnp.sum(jnp.where(mj, ib[i], 0.0), axis=0, keepdims=True) for i in ns]
        e = [jnp.where(ok, jnp.exp(jnp.where(ok, b[i] - bj[i], 0.0)), 0.0) for i in ns]
        s = [jnp.sum(q[i] * kj[i] * e[i], axis=1, keepdims=True) for i in ns]
        o = [o[i] + s[i] * vj[i] for i in ns]
    b_last = [jnp.sum(jnp.where(ri == c - 1, b[i], 0.0), axis=0, keepdims=True) for i in ns]
    return (o, [q[i] * jnp.exp(b[i]) for i in ns], [k[i] * jnp.exp(b_last[i] - b[i]) for i in ns],
            [jnp.exp(b_last[i]) for i in ns])


def _hgrn_seq(o_intra, q_dec, k_dec, dl, v, gb, st, gain):
    o = o_intra + _bdot(q_dec, st, NT)
    st_new = st * dl + _bdot(v, k_dec, TN)
    return _rms(o, gain) * _silu(gb), st_new


def _hgrn_fwd(p, lb, gain, name):
    t = p.shape[0]
    r = HGRN_STEP
    ns = t // r
    nsub = r // HGRN_CHUNK
    blk = lambda off: pl.BlockSpec((r, HEAD), lambda n, h: (n, off // HEAD + h))

    def body(qb, fb, ib, gb, lb_ref, gn, o_ref, ss_ref, s_scr):
        n, h = pl.program_id(0), pl.program_id(1)

        @pl.when(n == 0)
        def _():
            s_scr[h] = jnp.zeros((HEAD, HEAD), f32)

        st = s_scr[h]
        ss_ref[0, 0] = st
        rows = [pl.ds(ch * HGRN_CHUNK, HGRN_CHUNK) for ch in range(nsub)]
        v = [ib[rw, :] for rw in rows]
        oi, qd, kd, dl = _hgrn_intra([qb[rw, :] for rw in rows], [fb[rw, :] for rw in rows], v, lb_ref[...])
        for ch in range(nsub):
            o, st = _hgrn_seq(oi[ch], qd[ch], kd[ch], dl[ch], v[ch], gb[rows[ch], :], st, gn[...])
            o_ref[rows[ch], :] = o.astype(bf16)
        s_scr[h] = st

    return pl.pallas_call(
        body, grid=(ns, NH),
        in_specs=[blk(C_QB), blk(C_FB), blk(C_IB), blk(C_GB),
                  pl.BlockSpec((1, HEAD), lambda n, h: (0, h)), pl.BlockSpec((1, HEAD), lambda n, h: (0, 0))],
        out_specs=[pl.BlockSpec((r, HEAD), lambda n, h: (n, h)),
                   pl.BlockSpec((1, 1, HEAD, HEAD), lambda n, h: (n, h, 0, 0))],
        out_shape=[_sds((t, NH * HEAD), bf16), _sds((ns, NH, HEAD, HEAD), f32)],
        scratch_shapes=[pltpu.VMEM((NH, HEAD, HEAD), f32)],
        compiler_params=_cp(), name=name)(p, p, p, p, lb, gain)


def _hgrn_bwd(p, lb, gain, states, do, name):
    t = p.shape[0]
    r = HGRN_STEP
    ns = t // r
    nsub = r // HGRN_CHUNK
    blk = lambda off: pl.BlockSpec((r, HEAD), lambda n, h: (ns - 1 - n, off // HEAD + h))
    hblk = pl.BlockSpec((r, HEAD), lambda n, h: (ns - 1 - n, h))

    def body(qb, fb, ib, gb, lb_ref, gn, ss_ref, do_ref,
             dqb, dfb, dib, dgb, dlb_ref, dgn_ref, ds_scr, st_scr):
        n, h = pl.program_id(0), pl.program_id(1)

        @pl.when(n == 0)
        def _():
            ds_scr[h] = jnp.zeros((HEAD, HEAD), f32)

        @pl.when((n == 0) & (h == 0))
        def _():
            dgn_ref[...] = jnp.zeros_like(dgn_ref)

        gnv = gn[...]
        rows = [pl.ds(ch * HGRN_CHUNK, HGRN_CHUNK) for ch in range(nsub)]
        v = [ib[rw, :] for rw in rows]
        (oi, qd, kd, dl), vjp_intra = jax.vjp(_hgrn_intra, [qb[rw, :] for rw in rows], [fb[rw, :] for rw in rows], v, lb_ref[...])
        st = ss_ref[0, 0]
        for ch in range(nsub):
            st_scr[ch] = st
            if ch < nsub - 1:
                st = st * dl[ch] + _bdot(v[ch], kd[ch], TN)
        ds = ds_scr[h]
        dgn = jnp.zeros((1, HEAD), f32)
        d_oi, d_qd, d_kd, d_dl, d_v = [None] * nsub, [None] * nsub, [None] * nsub, [None] * nsub, [None] * nsub
        for ch in reversed(range(nsub)):
            _, vjp = jax.vjp(_hgrn_seq, oi[ch], qd[ch], kd[ch], dl[ch], v[ch], gb[rows[ch], :], st_scr[ch], gnv)
            d_oi[ch], d_qd[ch], d_kd[ch], d_dl[ch], d_v[ch], g_g, ds, g_gn = vjp((do_ref[rows[ch], :], ds))
            dgb[rows[ch], :] = g_g
            dgn = dgn + g_gn
        g_q, g_f, g_i, g_lb = vjp_intra((d_oi, d_qd, d_kd, d_dl))
        for ch in range(nsub):
            dqb[rows[ch], :] = g_q[ch]
            dfb[rows[ch], :] = g_f[ch]
            dib[rows[ch], :] = g_i[ch] + d_v[ch]
        ds_scr[h] = ds
        dlb_ref[0] = g_lb
        dgn_ref[...] += dgn

    return pl.pallas_call(
        body, grid=(ns, NH),
        in_specs=[blk(C_QB), blk(C_FB), blk(C_IB), blk(C_GB),
                  pl.BlockSpec((1, HEAD), lambda n, h: (0, h)), pl.BlockSpec((1, HEAD), lambda n, h: (0, 0)),
                  pl.BlockSpec((1, 1, HEAD, HEAD), lambda n, h: (ns - 1 - n, h, 0, 0)),
                  pl.BlockSpec((r, HEAD), lambda n, h: (ns - 1 - n, NH + h))],
        out_specs=[hblk, hblk, hblk, hblk,
                   pl.BlockSpec((1, 1, HEAD), lambda n, h: (n, 0, h)),
                   pl.BlockSpec((1, HEAD), lambda n, h: (0, 0))],
        out_shape=[_sds((t, 512), f32)] * 4 + [_sds((ns, 1, 512), f32), _sds((1, HEAD), f32)],
        scratch_shapes=[pltpu.VMEM((NH, HEAD, HEAD), f32), pltpu.VMEM((nsub, HEAD, HEAD), f32)],
        compiler_params=_cp(), name=name)(p, p, p, p, lb, gain, states, do)


def _lru_gates(xb, wa, wx, ba, bx, lam):
    xh = xb.astype(bf16)
    r = jax.nn.sigmoid(_dot(xh, wa.astype(bf16), NN) + ba)
    i = jax.nn.sigmoid(_dot(xh, wx.astype(bf16), NN) + bx)
    log_a = -RG_C * r * jax.nn.softplus(-lam)
    a = jnp.exp(log_a)
    t2 = 2.0 * log_a
    series = -t2 * (1.0 + t2 * (0.5 + t2 * (1.0 / 6.0 + t2 * (1.0 / 24.0))))
    om = jnp.where(t2 > -1e-2, series, 1.0 - jnp.exp(t2))
    u = jnp.sqrt(jnp.maximum(om, 0.0)) * (i * xb)
    return a, u


def _lru_gates_fwd(xc, wa, wx, ba, bx, lam, name):
    t = xc.shape[0]
    blk = pl.BlockSpec((TT, LRU_BLOCK), lambda h, i: (i, h))
    wsp = pl.BlockSpec((1, LRU_BLOCK, LRU_BLOCK), lambda h, i: (h, 0, 0))
    vsp = pl.BlockSpec((1, LRU_BLOCK), lambda h, i: (0, h))

    def body(x_ref, wa_ref, wx_ref, ba_ref, bx_ref, lam_ref, a_ref, u_ref):
        a, u = _lru_gates(x_ref[...], wa_ref[0], wx_ref[0], ba_ref[...], bx_ref[...], lam_ref[...])
        a_ref[...] = a
        u_ref[...] = u

    return pl.pallas_call(
        body, grid=(NH, t // TT), in_specs=[blk, wsp, wsp, vsp, vsp, vsp], out_specs=[blk, blk],
        out_shape=[_sds((t, D), f32)] * 2, compiler_params=_cp(), name=name)(xc, wa, wx, ba, bx, lam)


def _lru_gates_bwd(xc, wa, wx, ba, bx, lam, da, du, name):
    t = xc.shape[0]
    blk = pl.BlockSpec((TT, LRU_BLOCK), lambda h, i: (i, h))
    wsp = pl.BlockSpec((1, LRU_BLOCK, LRU_BLOCK), lambda h, i: (h, 0, 0))
    vsp = pl.BlockSpec((1, LRU_BLOCK), lambda h, i: (0, h))

    def body(x_ref, wa_ref, wx_ref, ba_ref, bx_ref, lam_ref, da_ref, du_ref,
             dx_ref, dwa_ref, dwx_ref, dba_ref, dbx_ref, dlam_ref):
        @pl.when(pl.program_id(1) == 0)
        def _():
            for r in (dwa_ref, dwx_ref, dba_ref, dbx_ref, dlam_ref):
                r[...] = jnp.zeros_like(r)

        _, vjp = jax.vjp(_lru_gates, x_ref[...], wa_ref[0], wx_ref[0], ba_ref[...], bx_ref[...], lam_ref[...])
        dx, dwa, dwx, dba, dbx, dlam = vjp((da_ref[...], du_ref[...]))
        dx_ref[...] = dx
        dwa_ref[0] += dwa
        dwx_ref[0] += dwx
        dba_ref[...] += dba
        dbx_ref[...] += dbx
        dlam_ref[...] += dlam

    return pl.pallas_call(
        body, grid=(NH, t // TT), in_specs=[blk, wsp, wsp, vsp, vsp, vsp, blk, blk],
        out_specs=[blk, wsp, wsp, vsp, vsp, vsp],
        out_shape=[_sds((t, D), f32), _sds((NH, LRU_BLOCK, LRU_BLOCK), f32), _sds((NH, LRU_BLOCK, LRU_BLOCK), f32),
                   _sds((1, D), f32), _sds((1, D), f32), _sds((1, D), f32)],
        compiler_params=_cp(), name=name)(xc, wa, wx, ba, bx, lam, da, du)


_SCAN_SHIFTS = (1, 2, 4, 8, 16, 32, 64, 128)
_SCAN_PAD = 128


def _gelu(y):
    return jax.nn.gelu(y, approximate=True)


def _lru_scan_fwd(a, u, p2, name):
    t = a.shape[0]
    tc = 128
    blk = pl.BlockSpec((TT, tc), lambda j, i: (i, j))

    def body(a_ref, u_ref, y_ref, h_ref, hg_ref, a_s, b_s, carry):
        i = pl.program_id(1)

        @pl.when(i == 0)
        def _():
            carry[...] = jnp.zeros_like(carry)
            a_s[0:_SCAN_PAD, :] = jnp.ones((_SCAN_PAD, tc), f32)
            b_s[0:_SCAN_PAD, :] = jnp.zeros((_SCAN_PAD, tc), f32)

        av, bv = a_ref[...], u_ref[...]
        for s in _SCAN_SHIFTS:
            a_s[_SCAN_PAD:, :] = av
            b_s[_SCAN_PAD:, :] = bv
            ash = a_s[pl.ds(_SCAN_PAD - s, TT), :]
            bsh = b_s[pl.ds(_SCAN_PAD - s, TT), :]
            bv = bv + av * bsh
            av = av * ash
        h = bv + av * carry[7:8, :]
        h_ref[...] = h
        hg_ref[...] = (h * _gelu(y_ref[...])).astype(bf16)
        carry[...] = h[TT - 8:, :]

    return pl.pallas_call(
        body, grid=(D // tc, t // TT), in_specs=[blk, blk, blk], out_specs=[blk, blk],
        out_shape=[_sds((t, D), f32), _sds((t, D), bf16)],
        scratch_shapes=[pltpu.VMEM((_SCAN_PAD + TT, tc), f32), pltpu.VMEM((_SCAN_PAD + TT, tc), f32),
                        pltpu.VMEM((8, tc), f32)],
        compiler_params=_cp(), name=name)(a, u, p2)


def _lru_scan_bwd(a, h, p2, dhg, name):
    t = a.shape[0]
    tc = 128
    nt = t // TT
    hb = TT // 8
    rblk = pl.BlockSpec((TT, tc), lambda j, i: (nt - 1 - i, j))

    def body(a_ref, an_ref, h_ref, hp_ref, y_ref, dhg_ref, du_ref, da_ref, dy_ref, a_s, b_s, ap, hp, carry):
        i = pl.program_id(1)

        @pl.when(i == 0)
        def _():
            carry[...] = jnp.zeros_like(carry)
            a_s[TT:, :] = jnp.ones((_SCAN_PAD, tc), f32)
            b_s[TT:, :] = jnp.zeros((_SCAN_PAD, tc), f32)

        ap[0:TT, :] = a_ref[...]
        ap[TT:, :] = jnp.where(i == 0, 0.0, an_ref[...])
        hp[0:8, :] = jnp.where(i == nt - 1, 0.0, hp_ref[...])
        hp[8:, :] = h_ref[...]
        y = y_ref[...]
        gate, gvjp = jax.vjp(_gelu, y)
        dhg_v = dhg_ref[...]
        dy_ref[...] = gvjp(dhg_v * h_ref[...])[0]
        av = ap[pl.ds(1, TT), :]
        bv = dhg_v * gate
        for s in _SCAN_SHIFTS:
            a_s[0:TT, :] = av
            b_s[0:TT, :] = bv
            ash = a_s[pl.ds(s, TT), :]
            bsh = b_s[pl.ds(s, TT), :]
            bv = bv + av * bsh
            av = av * ash
        g = bv + av * carry[0:1, :]
        du_ref[...] = g
        da_ref[...] = g * hp[pl.ds(7, TT), :]
        carry[...] = g[0:8, :]

    in_specs = [
        rblk,
        pl.BlockSpec((8, tc), lambda j, i: (jnp.minimum((nt - i) * hb, t // 8 - 1), j)),
        rblk,
        pl.BlockSpec((8, tc), lambda j, i: (jnp.maximum((nt - 1 - i) * hb - 1, 0), j)),
        rblk, rblk,
    ]
    return pl.pallas_call(
        body, grid=(D // tc, nt), in_specs=in_specs, out_specs=[rblk, rblk, rblk],
        out_shape=[_sds((t, D), f32)] * 3,
        scratch_shapes=[pltpu.VMEM((TT + _SCAN_PAD, tc), f32), pltpu.VMEM((TT + _SCAN_PAD, tc), f32),
                        pltpu.VMEM((TT + 8, tc), f32), pltpu.VMEM((TT + 8, tc), f32), pltpu.VMEM((8, tc), f32)],
        compiler_params=_cp(), name=name)(a, a, h, h, p2, dhg)


def _ffn_act_bwd(gc, up, dact, name, tc=256):
    t = gc.shape[0]
    blk = pl.BlockSpec((TT, tc), lambda i, j: (i, j))
    vblk = pl.BlockSpec((TT, tc), lambda i, j: (i, j + D_FF // tc))

    def body(gc_ref, v_ref, da_ref, dgc_ref, dv_ref):
        _, vjp = jax.vjp(lambda g, v: _silu(g) * v, gc_ref[...], v_ref[...])
        dg, dv = vjp(da_ref[...])
        dgc_ref[...] = dg
        dv_ref[...] = dv.astype(bf16)

    return pl.pallas_call(
        body, grid=(t // TT, D_FF // tc), in_specs=[blk, vblk, blk], out_specs=[blk, vblk],
        out_shape=[_sds((t, D_FF), f32), _sds((t, 2 * D_FF), bf16)], compiler_params=_cp(), name=name)(gc, up, dact)


def _lower_bounds_fwd(w):
    def body(w_ref, o0_ref, o1_ref):
        wv = w_ref[...]
        o0, o1 = _lb_rows(wv[0:1, :], wv[1:2, :])
        o0_ref[...] = o0
        o1_ref[...] = o1

    return pl.pallas_call(body, out_shape=[_sds((1, 512), f32)] * 2, name="lower_bounds_fwd")(w)


def _lb_rows(w0, w1):
    m = jnp.maximum(w0, w1)
    e0, e1 = jnp.exp(w0 - m), jnp.exp(w1 - m)
    s = e0 + e1
    p0, p1 = e0 / s, e1 / s
    return p0 - p0, (p0 + p1) - p0


def _lower_bounds_bwd(w, d0, d1):
    def body(w_ref, d0_ref, d1_ref, g0_ref, g1_ref):
        wv = w_ref[...]
        _, vjp = jax.vjp(_lb_rows, wv[0:1, :], wv[1:2, :])
        g0, g1 = vjp((d0_ref[...], d1_ref[...]))
        g0_ref[...] = g0
        g1_ref[...] = g1

    return pl.pallas_call(body, out_shape=[_sds((1, 512), f32)] * 2, name="lower_bounds_bwd")(w, d0, d1)


def _local_step(x, target, wt, pre_layer=None, post_grads=None):
    depth = 4
    res = []
    lb0, lb1 = _lower_bounds_fwd(wt["hgrn_lower_bounds"])
    lbs = [lb0, lb1]
    for layer in range(depth):
        j = layer // 2
        sv = {"x_in": x}
        deps = pre_layer(layer, "mix", x) if pre_layer else ()
        if layer == 0:
            h1 = _rmsnorm_fwd(x, wt["norm_mix"][layer], "rms_fwd", deps)
        sv["h1"] = h1
        if layer % 2 == 0:
            p = _mm_auto(h1, wt["ab_w_in"][j], "nn", f32, "mm_ab_in")
            conv = _conv_fwd(p, 0, 1536, wt["gdn_conv_w"][j], jnp.zeros((1, 1536), f32), "gdn_conv_fwd", tc=768)
            o_a, s_a = _gdn2_fwd(conv, p, wt["alog_v"][j], wt["dtb_v"][j], wt["gdn_norm"][j], "gdn_fwd")
            o_b, s_b = _hgrn_fwd(p, lbs[j], wt["hgrn_norm"][j], "hgrn_fwd")
            o = jnp.concatenate([o_a, o_b], axis=1)
            x, h2 = _mm_res_norm(o, wt["ab_w_out"][j], x, wt["norm_ffn"][layer], "mm_ab_out")
            sv.update(p=p, conv=conv, s_a=s_a, s_b=s_b, o=o)
        else:
            p2 = _mm_nn_slots(h1, wt["c_w_in"][j], f32, "mm_c_in", deps)
            xc = _conv_fwd(p2, 1, D, wt["c_conv_w"][j], wt["c_conv_b"][j], "lru_conv_fwd", tc=D)
            a, u = _lru_gates_fwd(xc, wt["c_gate_a_w"][j], wt["c_gate_x_w"][j], wt["c_gate_a_b"][j],
                                  wt["c_gate_x_b"][j], wt["c_lambda"][j], "lru_gates_fwd")
            h, hg = _lru_scan_fwd(a, u, p2, "lru_scan_fwd")
            x, h2 = _mm_res_norm(hg, wt["c_w_out"][j], x, wt["norm_ffn"][layer], "mm_c_out")
            sv.update(p2=p2, xc=xc, a=a, h=h, hg=hg)
        sv["x_mid"] = x
        deps = pre_layer(layer, "ffn", x) if pre_layer else ()
        up = _mm_nn_slots(h2, wt["ffn_w_up"][layer], f32, "mm_up", deps)
        gc, act = _conv_fwd(up, 0, D_FF, wt["ffn_conv_w"][layer], wt["ffn_conv_b"][layer], "ffn_conv_fwd",
                            tc=D_FF // 2, val=up, val_col0=2)
        if layer + 1 < depth:
            x, h1 = _mm_res_norm(act, wt["ffn_w_down"][layer], x, wt["norm_mix"][layer + 1], "mm_down")
        else:
            x = _mm_res_norm(act, wt["ffn_w_down"][layer], x, None, "mm_down_last")
        sv.update(h2=h2, up=up, gc=gc, act=act)
        res.append(sv)

    loss, dx, dxb, d_norm_final = _final_loss(x, wt["norm_final"], target, "final_loss")

    g = {k: [None] * len(v) for k, v in wt.items() if isinstance(v, list)}
    g["norm_final"] = d_norm_final
    d_lbs = [None, None]
    for layer in reversed(range(depth)):
        j = layer // 2
        sv = res[layer]
        dact = _mm_auto(dxb, wt["ffn_w_down"][layer], "nt", f32, "mm_down_dx")
        g["ffn_w_down"][layer] = _mm_auto(sv["act"], dxb, "tn", bf16, "mm_down_dw")
        dgc, dup = _ffn_act_bwd(sv["gc"], sv["up"], dact, "ffn_act_bwd", tc=D_FF // 2)
        dup, dcw, dcb = _conv_bwd(dgc, sv["up"], 0, wt["ffn_conv_w"][layer], "ffn_conv_bwd", tc=D_FF // 2, into=dup)
        g["ffn_conv_w"][layer] = dcw[:3]
        g["ffn_conv_b"][layer] = dcb
        g["ffn_w_up"][layer] = _mm_tn_slots(sv["h2"], dup, "mm_up_dw")
        deps = post_grads(layer, "ffn", g) if post_grads else ()
        dx, dxb, g["norm_ffn"][layer] = _mm_nt_rmsbwd(dup, wt["ffn_w_up"][layer], sv["x_mid"], wt["norm_ffn"][layer], dx,
                                                      "mm_up_dx", deps)
        if layer % 2 == 0:
            do = _mm_auto(dxb, wt["ab_w_out"][j], "nt", f32, "mm_ab_out_dx")
            g["ab_w_out"][j] = _mm_auto(sv["o"], dxb, "tn", bf16, "mm_ab_out_dw")
            dconv, dz, dba, dal, ddt, dgn = _gdn2_bwd(
                sv["conv"], sv["p"], wt["alog_v"][j], wt["dtb_v"][j], wt["gdn_norm"][j], sv["s_a"], do, "gdn_bwd")
            g["alog_v"][j], g["dtb_v"][j], g["gdn_norm"][j] = dal, ddt, dgn
            dqkv, dcw, _ = _conv_bwd(dconv, sv["p"], 0, wt["gdn_conv_w"][j], "gdn_conv_bwd", tc=768)
            g["gdn_conv_w"][j] = dcw[:4]
            dqb, dfb, dib, dgb, dlb, dhn = _hgrn_bwd(sv["p"], lbs[j], wt["hgrn_norm"][j], sv["s_b"], do, "hgrn_bwd")
            g["hgrn_norm"][j] = dhn
            d_lbs[j] = jnp.sum(dlb, axis=0)
            dp = jnp.concatenate([dqkv] + [t_.astype(bf16) for t_ in (dz, dqb, dfb, dib, dgb, dba)], axis=1)
            g["ab_w_in"][j] = _mm_auto(sv["h1"], dp, "tn", bf16, "mm_ab_in_dw")
            dpre, wpre = dp, wt["ab_w_in"][j]
        else:
            dhg = _mm_auto(dxb, wt["c_w_out"][j], "nt", f32, "mm_c_out_dx")
            g["c_w_out"][j] = _mm_auto(sv["hg"], dxb, "tn", bf16, "mm_c_out_dw")
            du, da, dy = _lru_scan_bwd(sv["a"], sv["h"], sv["p2"], dhg, "lru_scan_bwd")
            dxc, dwa, dwx, dba_, dbx_, dlam = _lru_gates_bwd(
                sv["xc"], wt["c_gate_a_w"][j], wt["c_gate_x_w"][j], wt["c_gate_a_b"][j], wt["c_gate_x_b"][j],
                wt["c_lambda"][j], da, du, "lru_gates_bwd")
            g["c_gate_a_w"][j], g["c_gate_x_w"][j] = dwa, dwx
            g["c_gate_a_b"][j], g["c_gate_x_b"][j], g["c_lambda"][j] = dba_, dbx_, dlam
            dxbr, dcw, dcb = _conv_bwd(dxc, sv["p2"], 1, wt["c_conv_w"][j], "lru_conv_bwd", tc=D)
            g["c_conv_w"][j] = dcw[:4]
            g["c_conv_b"][j] = dcb
            dp2 = jnp.concatenate([dy.astype(bf16), dxbr], axis=1)
            g["c_w_in"][j] = _mm_tn_slots(sv["h1"], dp2, "mm_c_in_dw")
            dpre, wpre = dp2, wt["c_w_in"][j]
        deps = post_grads(layer, "mix", g) if post_grads else ()
        dx, dxb, g["norm_mix"][layer] = _mm_nt_rmsbwd(dpre, wpre, sv["x_in"], wt["norm_mix"][layer], dx, "mm_mix_in_dx", deps)
    g0, g1 = _lower_bounds_bwd(wt["hgrn_lower_bounds"], d_lbs[0], d_lbs[1])
    g["hgrn_lower_bounds"] = jnp.concatenate([g0, g1], axis=0)
    return loss, dx, g


def _ab_in_to_compute(w):
    return jnp.concatenate([w[:, :2048], w[:, 2056:4104], w[:, 2048:2056], jnp.zeros((D, 120), w.dtype)], axis=1)


def _ab_in_from_compute(g):
    return jnp.concatenate([g[:, :2048], g[:, 4096:4104], g[:, 2048:4096]], axis=1)


def _lane_vec(v4):
    return jnp.zeros((1, HEAD), f32).at[0, NH:2 * NH].set(v4)


def _layout_weights(fw):
    wt = {}
    wt["norm_mix"] = [fw["norm_mix"][l][None] for l in range(4)]
    wt["norm_ffn"] = [fw["norm_ffn"][l][None] for l in range(4)]
    wt["norm_final"] = fw["norm_final"][None]
    wt["gdn_conv_w"] = [fw["gdn_conv_w"][j] for j in range(2)]
    wt["alog_v"] = [_lane_vec(fw["gdn_a_log"][j]) for j in range(2)]
    wt["dtb_v"] = [_lane_vec(fw["gdn_dt_bias"][j]) for j in range(2)]
    wt["gdn_norm"] = [fw["gdn_norm"][j][None] for j in range(2)]
    wt["hgrn_lower_bounds"] = fw["hgrn_lower_bounds"]
    wt["hgrn_norm"] = [fw["hgrn_norm"][j][None] for j in range(2)]
    wt["c_conv_w"] = [fw["c_conv_w"][j] for j in range(2)]
    for k in ("c_conv_b", "c_gate_a_b", "c_gate_x_b", "c_lambda"):
        wt[k] = [fw[k][j][None] for j in range(2)]
    wt["ffn_conv_w"] = [fw["ffn_conv_w"][l] for l in range(4)]
    wt["ffn_conv_b"] = [fw["ffn_conv_b"][l][None] for l in range(4)]
    if "ab_w_in" in fw:
        wt["ab_w_in"] = [_ab_in_to_compute(fw["ab_w_in"][j].astype(bf16)) for j in range(2)]
        for k in ("ab_w_out", "c_w_out"):
            wt[k] = [fw[k][j].astype(bf16) for j in range(2)]
        wt["c_w_in"] = [_to_slots(fw["c_w_in"][j].astype(bf16)) for j in range(2)]
        for k in ("c_gate_a_w", "c_gate_x_w"):
            wt[k] = [fw[k][j].astype(f32) for j in range(2)]
        wt["ffn_w_up"] = [_to_slots(fw["ffn_w_up"][l].astype(bf16)) for l in range(4)]
        wt["ffn_w_down"] = [fw["ffn_w_down"][l].astype(bf16) for l in range(4)]
    return wt


SLOT_MAJOR = ("c_w_in", "ffn_w_up")
NATIVE_PERM = {"ab_w_in": (2, 0, 1), "ffn_conv_w": (1, 0, 2)}


def _to_slots(wfull):
    k, n = wfull.shape
    return wfull.reshape(k, N_SLOTS, n // N_SLOTS).transpose(1, 0, 2)


def _layer_full(name, slots):
    kind = BIG[name]
    if name in SLOT_MAJOR:
        return slots
    if kind == "col":
        return _ab_in_to_compute(slots.transpose(1, 0, 2).reshape(slots.shape[1], -1))
    if kind == "row":
        return slots.reshape(-1, slots.shape[2])
    return slots.reshape(4, NH, LRU_BLOCK // 4, LRU_BLOCK).transpose(1, 0, 2, 3).reshape(NH, LRU_BLOCK, LRU_BLOCK).astype(f32)


def _layer_slots(name, g):
    kind = BIG[name]
    if name in SLOT_MAJOR:
        return g
    if kind == "col":
        g = _ab_in_from_compute(g)
        r, cdim = g.shape
        return g.reshape(r, 4, cdim // 4).transpose(1, 0, 2).astype(bf16)
    if kind == "row":
        r, cdim = g.shape
        return g.reshape(4, r // 4, cdim).astype(bf16)
    return g.reshape(NH, 4, LRU_BLOCK // 4, LRU_BLOCK).transpose(1, 0, 2, 3).reshape(4, LRU_BLOCK, LRU_BLOCK).astype(bf16)


def _unlayout_grads(g):
    out = {}
    for k in ("norm_mix", "norm_ffn", "gdn_norm", "hgrn_norm", "c_conv_b", "c_gate_a_b", "c_gate_x_b", "c_lambda",
              "ffn_conv_b"):
        out[k] = jnp.concatenate(g[k], axis=0)
    out["norm_final"] = g["norm_final"][0]
    out["ab_w_in"] = jnp.stack([_ab_in_from_compute(t) for t in g["ab_w_in"]])
    out["gdn_a_log"] = jnp.stack([t[0, NH:2 * NH] for t in g["alog_v"]])
    out["gdn_dt_bias"] = jnp.stack([t[0, NH:2 * NH] for t in g["dtb_v"]])
    out["hgrn_lower_bounds"] = g["hgrn_lower_bounds"]
    for k in ("gdn_conv_w", "ab_w_out", "c_conv_w", "c_gate_a_w", "c_gate_x_w", "c_w_out", "ffn_conv_w", "ffn_w_down"):
        out[k] = jnp.stack(g[k])
    for k in SLOT_MAJOR:
        out[k] = jnp.stack([t.transpose(1, 0, 2).reshape(t.shape[1], -1) for t in g[k]])
    return out


MESH = pl.DeviceIdType.MESH
ANY = pl.BlockSpec(memory_space=pl.ANY)
CHIP_RELATIONS = ((1, 0), (0, 1), (1, 1))
N_CHIPS = 4


def _coords():
    return lax.axis_index("x"), lax.axis_index("y"), lax.axis_index("c")


def _flip(v, f):
    return 1 - v if f else v


def _half_rows(c, a, align):
    return pl.ds(pl.multiple_of(c * (a // 2), align), a // 2)


def _all_gather_chips(shards, name):
    n = len(shards)
    shapes = [s.shape for s in shards]

    def body(*refs):
        ins, outs = refs[:n], refs[n:2 * n]
        send_sems, recv_sems = refs[2 * n:]
        x, y, c = _coords()
        me = 2 * x + y
        sibling = (x, y, 1 - c)
        started = []
        for p in range(n):
            cp = pltpu.make_async_remote_copy(
                src_ref=ins[p], dst_ref=outs[p].at[me],
                send_sem=send_sems.at[p, 6], recv_sem=recv_sems.at[p, 6],
                device_id=sibling, device_id_type=MESH)
            cp.start()
            started.append(cp)
        for p in range(n):
            mine = _half_rows(c, shapes[p][0], 16)
            for r, (fx, fy) in enumerate(CHIP_RELATIONS):
                cp = pltpu.make_async_remote_copy(
                    src_ref=ins[p].at[mine], dst_ref=outs[p].at[me, mine],
                    send_sem=send_sems.at[p, r], recv_sem=recv_sems.at[p, r],
                    device_id=(_flip(x, fx), _flip(y, fy), c), device_id_type=MESH)
                cp.start()
                started.append(cp)
        for r, (fx, fy) in enumerate(CHIP_RELATIONS):
            k = 2 * _flip(x, fx) + _flip(y, fy)
            for p in range(n):
                mine = _half_rows(c, shapes[p][0], 16)
                pltpu.make_async_remote_copy(
                    src_ref=ins[p].at[mine], dst_ref=outs[p].at[k, mine],
                    send_sem=send_sems.at[p, r], recv_sem=recv_sems.at[p, r],
                    device_id=(_flip(x, fx), _flip(y, fy), c), device_id_type=MESH).wait_recv()
                fwd = pltpu.make_async_remote_copy(
                    src_ref=outs[p].at[k, mine], dst_ref=outs[p].at[k, mine],
                    send_sem=send_sems.at[p, 3 + r], recv_sem=recv_sems.at[p, 3 + r],
                    device_id=sibling, device_id_type=MESH)
                fwd.start()
                started.append(fwd)
        for r, (fx, fy) in enumerate(CHIP_RELATIONS):
            k = 2 * _flip(x, fx) + _flip(y, fy)
            for p in range(n):
                theirs = _half_rows(1 - c, shapes[p][0], 16)
                pltpu.make_async_remote_copy(
                    src_ref=outs[p].at[k, theirs], dst_ref=outs[p].at[k, theirs],
                    send_sem=send_sems.at[p, 3 + r], recv_sem=recv_sems.at[p, 3 + r],
                    device_id=sibling, device_id_type=MESH).wait_recv()
        for p in range(n):
            pltpu.make_async_remote_copy(
                src_ref=ins[p], dst_ref=outs[p].at[me],
                send_sem=send_sems.at[p, 6], recv_sem=recv_sems.at[p, 6],
                device_id=sibling, device_id_type=MESH).wait_recv()
        for cp in started:
            cp.wait_send()

    return pl.pallas_call(
        body, in_specs=[ANY] * n, out_specs=[ANY] * n,
        out_shape=[_sds((N_CHIPS,) + s.shape, s.dtype) for s in shards],
        scratch_shapes=[pltpu.SemaphoreType.DMA((n, 7)), pltpu.SemaphoreType.DMA((n, 7))],
        name=name)(*shards)


def _sibling_send_other_half(gs, name):
    n = len(gs)
    shapes = [g.shape for g in gs]

    def body(*refs):
        ins, outs = refs[:n], refs[n:2 * n]
        send_sems, recv_sems = refs[2 * n:]
        x, y, c = _coords()
        cps = []
        for p in range(n):
            theirs = _half_rows(1 - c, shapes[p][1], 8)
            cp = pltpu.make_async_remote_copy(
                src_ref=ins[p].at[:, theirs], dst_ref=outs[p],
                send_sem=send_sems.at[p], recv_sem=recv_sems.at[p],
                device_id=(x, y, 1 - c), device_id_type=MESH)
            cp.start()
            cps.append(cp)
        for cp in cps:
            cp.wait()

    return pl.pallas_call(
        body, in_specs=[ANY] * n, out_specs=[ANY] * n,
        out_shape=[_sds((s[0], s[1] // 2, s[2]), f32) for s in shapes],
        scratch_shapes=[pltpu.SemaphoreType.DMA((n,)), pltpu.SemaphoreType.DMA((n,))],
        name=name)(*gs)


def _chip_exchange(ps, name):
    n = len(ps)

    def body(*refs):
        ins, outs = refs[:n], refs[n:2 * n]
        send_sems, recv_sems = refs[2 * n:]
        x, y, c = _coords()
        cps = []
        for p in range(n):
            for r, (fx, fy) in enumerate(CHIP_RELATIONS):
                k = 2 * _flip(x, fx) + _flip(y, fy)
                cp = pltpu.make_async_remote_copy(
                    src_ref=ins[p].at[k], dst_ref=outs[p].at[r],
                    send_sem=send_sems.at[p, r], recv_sem=recv_sems.at[p, r],
                    device_id=(_flip(x, fx), _flip(y, fy), c), device_id_type=MESH)
                cp.start()
                cps.append(cp)
        for cp in cps:
            cp.wait_recv()
        for cp in cps:
            cp.wait_send()

    return pl.pallas_call(
        body, in_specs=[ANY] * n, out_specs=[ANY] * n,
        out_shape=[_sds((3,) + p.shape[1:], p.dtype) for p in ps],
        scratch_shapes=[pltpu.SemaphoreType.DMA((n, 3)), pltpu.SemaphoreType.DMA((n, 3))],
        name=name)(*ps)


def _sibling_fill_other_half(fs, name):
    n = len(fs)
    shapes = [f.shape for f in fs]

    def body(*refs):
        ins, outs = refs[:n], refs[n:2 * n]
        send_sems, recv_sems = refs[2 * n:]
        x, y, c = _coords()
        cps = []
        for p in range(n):
            mine = _half_rows(c, shapes[p][0], 8)
            cp = pltpu.make_async_remote_copy(
                src_ref=ins[p].at[mine], dst_ref=outs[p].at[mine],
                send_sem=send_sems.at[p], recv_sem=recv_sems.at[p],
                device_id=(x, y, 1 - c), device_id_type=MESH)
            cp.start()
            cps.append(cp)
        for p in range(n):
            theirs = _half_rows(1 - c, shapes[p][0], 8)
            pltpu.make_async_remote_copy(
                src_ref=ins[p].at[theirs], dst_ref=outs[p].at[theirs],
                send_sem=send_sems.at[p], recv_sem=recv_sems.at[p],
                device_id=(x, y, 1 - c), device_id_type=MESH).wait_recv()
        for cp in cps:
            cp.wait_send()

    return pl.pallas_call(
        body, in_specs=[ANY] * n, out_specs=[ANY] * n,
        out_shape=[_sds(f.shape, f.dtype) for f in fs],
        input_output_aliases={p: p for p in range(n)},
        scratch_shapes=[pltpu.SemaphoreType.DMA((n,)), pltpu.SemaphoreType.DMA((n,))],
        name=name)(*fs)


def _sibling_all_gather(ss, name):
    n = len(ss)
    shapes = [s.shape for s in ss]

    def body(*refs):
        ins, outs = refs[:n], refs[n:2 * n]
        send_sems, recv_sems, loc_sems = refs[2 * n:]
        x, y, c = _coords()
        locs, cps = [], []
        for p in range(n):
            mine = _half_rows(c, 2 * shapes[p][0], 8)
            loc = pltpu.make_async_copy(ins[p], outs[p].at[mine], loc_sems.at[p])
            loc.start()
            locs.append(loc)
            cp = pltpu.make_async_remote_copy(
                src_ref=ins[p], dst_ref=outs[p].at[mine],
                send_sem=send_sems.at[p], recv_sem=recv_sems.at[p],
                device_id=(x, y, 1 - c), device_id_type=MESH)
            cp.start()
            cps.append(cp)
        for p, cp in enumerate(cps):
            theirs = _half_rows(1 - c, 2 * shapes[p][0], 8)
            pltpu.make_async_remote_copy(
                src_ref=ins[p], dst_ref=outs[p].at[theirs],
                send_sem=send_sems.at[p], recv_sem=recv_sems.at[p],
                device_id=(x, y, 1 - c), device_id_type=MESH).wait_recv()
        for cp in cps:
            cp.wait_send()
        for loc in locs:
            loc.wait()

    return pl.pallas_call(
        body, in_specs=[ANY] * n, out_specs=[ANY] * n,
        out_shape=[_sds((2 * s[0], s[1]), f32) for s in shapes],
        scratch_shapes=[pltpu.SemaphoreType.DMA((n,)), pltpu.SemaphoreType.DMA((n,)), pltpu.SemaphoreType.DMA((n,))],
        name=name)(*ss)


N_DEV = 8


def _all_reduce_small(pack, name, after=()):
    rows = pack.shape[0]

    def body(in_ref, *rest):
        sum_ref, all_ref, send_sems, recv_sems = rest[len(after):]
        x, y, c = _coords()
        me = 4 * x + 2 * y + c
        all_ref[me] = in_ref[...]
        cps = []
        for r in range(1, N_DEV):
            fx, fy, fc = (r >> 2) & 1, (r >> 1) & 1, r & 1
            cp = pltpu.make_async_remote_copy(
                src_ref=in_ref, dst_ref=all_ref.at[me],
                send_sem=send_sems.at[r], recv_sem=recv_sems.at[r],
                device_id=(_flip(x, fx), _flip(y, fy), _flip(c, fc)), device_id_type=MESH)
            cp.start()
            cps.append(cp)
        for r in range(1, N_DEV):
            fx, fy, fc = (r >> 2) & 1, (r >> 1) & 1, r & 1
            peer = 4 * _flip(x, fx) + 2 * _flip(y, fy) + _flip(c, fc)
            pltpu.make_async_remote_copy(
                src_ref=in_ref, dst_ref=all_ref.at[peer],
                send_sem=send_sems.at[r], recv_sem=recv_sems.at[r],
                device_id=(x, y, c), device_id_type=MESH).wait_recv()
        for cp in cps:
            cp.wait_send()
        acc = all_ref[0]
        for d in range(1, N_DEV):
            acc = acc + all_ref[d]
        sum_ref[...] = acc

    vm = pl.BlockSpec(memory_space=pltpu.VMEM)
    return pl.pallas_call(
        body, in_specs=[vm] + [pl.BlockSpec(memory_space=pl.ANY)] * len(after), out_specs=[vm, vm],
        out_shape=[_sds((rows, 128), f32), _sds((N_DEV, rows, 128), f32)],
        scratch_shapes=[pltpu.SemaphoreType.DMA((N_DEV,)), pltpu.SemaphoreType.DMA((N_DEV,))],
        name=name)(pack, *after)[0]


ROWS_EW = 128


def _add_own_half(g, rs, c_arr, name):
    s, a, b = g.shape
    nrt = (a // 2) // ROWS_EW

    def body(c_ref, g_ref, r_ref, o_ref):
        o_ref[...] = (g_ref[...] + r_ref[...]).astype(bf16)

    grid_spec = pltpu.PrefetchScalarGridSpec(
        num_scalar_prefetch=1, grid=(s, nrt),
        in_specs=[pl.BlockSpec((1, ROWS_EW, b), lambda k, i, c_ref: (k, c_ref[0] * nrt + i, 0)),
                  pl.BlockSpec((1, ROWS_EW, b), lambda k, i, c_ref: (k, i, 0))],
        out_specs=pl.BlockSpec((1, ROWS_EW, b), lambda k, i, c_ref: (k, i, 0)))
    return pl.pallas_call(body, grid_spec=grid_spec, out_shape=_sds((s, a // 2, b), bf16),
                          compiler_params=_cp(), name=name)(c_arr, g, rs)


def _sum_chips(g, rs, rc, ids, name):
    _, r, b = rc.shape
    nrt = r // ROWS_EW

    def body(ids_ref, g_ref, s_ref, r_ref, o_ref):
        own = g_ref[0] + s_ref[0]
        o_ref[...] = ((own + r_ref[0].astype(f32)) + r_ref[1].astype(f32)) + r_ref[2].astype(f32)

    grid_spec = pltpu.PrefetchScalarGridSpec(
        num_scalar_prefetch=1, grid=(nrt,),
        in_specs=[pl.BlockSpec((1, ROWS_EW, b), lambda i, ids_ref: (ids_ref[0], ids_ref[1] * nrt + i, 0)),
                  pl.BlockSpec((1, ROWS_EW, b), lambda i, ids_ref: (ids_ref[0], i, 0)),
                  pl.BlockSpec((3, ROWS_EW, b), lambda i, ids_ref: (0, i, 0))],
        out_specs=pl.BlockSpec((ROWS_EW, b), lambda i, ids_ref: (ids_ref[1] * nrt + i, 0)))
    return pl.pallas_call(body, grid_spec=grid_spec, out_shape=_sds((2 * r, b), f32),
                          compiler_params=_cp(), name=name)(ids, g, rs, rc)


def _adamw_math(w, g, m, v):
    m = ADAM_B1 * m + (1.0 - ADAM_B1) * g
    v = ADAM_B2 * v + (1.0 - ADAM_B2) * (g * g)
    m_hat = m / (1.0 - ADAM_B1 ** ADAM_STEP)
    v_hat = v / (1.0 - ADAM_B2 ** ADAM_STEP)
    delta = -ADAM_LR * (m_hat / (jnp.sqrt(v_hat) + ADAM_EPS) + ADAM_WD * w)
    return delta, m, v


def _adamw_big(w, g, m, v, name):
    nl, r, b = w.shape
    rt = _tile(r, (ROWS_EW, 352, 64))
    per = r // rt

    def body(w_ref, g_ref, m_ref, v_ref, go_ref, d_ref, mo_ref, vo_ref):
        gv = g_ref[...]
        d, mn, vn = _adamw_math(w_ref[...], gv, m_ref[...], v_ref[...])
        go_ref[...] = gv
        d_ref[...] = d
        mo_ref[...] = mn
        vo_ref[...] = vn

    blk = pl.BlockSpec((None, rt, b), lambda l, i: (l, i, 0))
    gblk = pl.BlockSpec((rt, b), lambda l, i: (l * per + i, 0))
    return pl.pallas_call(body, grid=(nl, per), in_specs=[blk, gblk, blk, blk], out_specs=[blk] * 4,
                          out_shape=[_sds((nl, r, b), f32)] * 4, compiler_params=_cp(), name=name)(w, g, m, v)


def _adamw_cols(w, g, m, v, name, tc=256):
    r, c = w.shape

    def body(w_ref, g_ref, m_ref, v_ref, go_ref, d_ref, mo_ref, vo_ref):
        gv = g_ref[...]
        d, mn, vn = _adamw_math(w_ref[...], gv, m_ref[...], v_ref[...])
        go_ref[...] = gv
        d_ref[...] = d
        mo_ref[...] = mn
        vo_ref[...] = vn

    blk = pl.BlockSpec((r, tc), lambda j: (0, j))
    return pl.pallas_call(body, grid=(c // tc,), in_specs=[blk] * 4, out_specs=[blk] * 4,
                          out_shape=[_sds((r, c), f32)] * 4, compiler_params=_cp(), name=name)(w, g, m, v)


def _adamw_small(ws, gs, ms, vs, name):
    n = len(ws)

    def body(*refs):
        w_r, g_r, m_r, v_r = refs[:n], refs[n:2 * n], refs[2 * n:3 * n], refs[3 * n:4 * n]
        go_r, d_r, mo_r, vo_r = refs[4 * n:5 * n], refs[5 * n:6 * n], refs[6 * n:7 * n], refs[7 * n:8 * n]
        for p in range(n):
            gv = g_r[p][...]
            d, mn, vn = _adamw_math(w_r[p][...], gv, m_r[p][...], v_r[p][...])
            go_r[p][...] = gv
            d_r[p][...] = d
            mo_r[p][...] = mn
            vo_r[p][...] = vn

    vm = pl.BlockSpec(memory_space=pltpu.VMEM)
    shp = [_sds(w.shape, f32) for w in ws]
    res = pl.pallas_call(body, in_specs=[vm] * (4 * n), out_specs=[vm] * (4 * n), out_shape=shp * 4,
                         name=name)(*ws, *gs, *ms, *vs)
    return res[:n], res[n:2 * n], res[2 * n:3 * n], res[3 * n:]


HBM = pl.BlockSpec(memory_space=pltpu.HBM)
SEM = pl.BlockSpec(memory_space=pltpu.SEMAPHORE)
EFFECT = pltpu.SideEffectType.DATAFLOW_SIDE_EFFECTING
N_REL = 8


def _rel(r):
    return (r >> 2) & 1, (r >> 1) & 1, r & 1


def _gather_copies(ins, lands, send_sems, recv_sems, shapes):
    x, y, c = _coords()
    me = 2 * x + y
    sends, recvs = [], []
    for p in range(len(ins)):
        for r in range(1, N_REL):
            fx, fy, fc = _rel(r)
            peer = (_flip(x, fx), _flip(y, fy), _flip(c, fc))
            if fx == 0 and fy == 0:
                src, dst, got = ins[p], lands[p].at[me], lands[p].at[me]
            else:
                mine = _half_rows(c, shapes[p][0], 16)
                theirs = _half_rows(_flip(c, fc), shapes[p][0], 16)
                src, dst = ins[p].at[mine], lands[p].at[me, mine]
                got = lands[p].at[2 * peer[0] + peer[1], theirs]
            sems = dict(send_sem=send_sems.at[p * N_REL + r], recv_sem=recv_sems.at[p * N_REL + r], device_id=peer,
                        device_id_type=MESH)
            sends.append(pltpu.make_async_remote_copy(src_ref=src, dst_ref=dst, **sems))
            recvs.append(pltpu.make_async_remote_copy(src_ref=src, dst_ref=got, **sems))
    return sends, recvs


def _scatter_copies(ins, lands, send_sems, recv_sems, shapes):
    x, y, c = _coords()
    sends, recvs = [], []
    for p in range(len(ins)):
        for r in range(1, N_REL):
            fx, fy, fc = _rel(r)
            peer = (_flip(x, fx), _flip(y, fy), _flip(c, fc))
            theirs = _half_rows(peer[2], shapes[p][1], 16)
            sems = dict(send_sem=send_sems.at[p * N_REL + r], recv_sem=recv_sems.at[p * N_REL + r], device_id=peer,
                        device_id_type=MESH)
            cp = pltpu.make_async_remote_copy(src_ref=ins[p].at[2 * peer[0] + peer[1], theirs], dst_ref=lands[p].at[r], **sems)
            sends.append(cp)
            recvs.append(cp)
    return sends, recvs


def _split_start(copies_fn, ins, land_shapes, name, after=()):
    n = len(ins)
    shapes = [a.shape for a in ins]

    def body(*refs):
        in_refs, land_refs = refs[:n], refs[n:2 * n]
        send_sems, recv_sems = refs[2 * n + len(after)], refs[2 * n + len(after) + 1]
        token = refs[-1]
        sends, _ = copies_fn(in_refs, land_refs, send_sems, recv_sems, shapes)
        for cp in sends:
            cp.start()
        token[...] = jnp.zeros_like(token)

    lands = [lax.empty(s.shape, s.dtype) for s in land_shapes]
    na = len(after)
    res = pl.pallas_call(
        body, name=name,
        out_shape=(pltpu.SemaphoreType.DMA((n * N_REL,)), pltpu.SemaphoreType.DMA((n * N_REL,)))
        + tuple(pltpu.HBM(a.shape, a.dtype) for a in ins) + tuple(pltpu.HBM(s.shape, s.dtype) for s in land_shapes)
        + (_sds((8, 128), f32),),
        in_specs=[HBM] * (2 * n) + [pl.BlockSpec(memory_space=pl.ANY)] * na,
        out_specs=(SEM, SEM) + (HBM,) * (2 * n) + (pl.BlockSpec(memory_space=pltpu.VMEM),),
        input_output_aliases={i: 2 + i for i in range(2 * n)},
        compiler_params=pltpu.CompilerParams(has_side_effects=EFFECT),
    )(*[pltpu.with_memory_space_constraint(a, pltpu.HBM) for a in ins],
      *[pltpu.with_memory_space_constraint(a, pltpu.HBM) for a in lands], *after)
    return dict(sems=res[:2], ins=res[2:2 + n], lands=res[2 + n:2 + 2 * n], token=res[-1], shapes=shapes)


def _split_wait(copies_fn, started, after, name):
    n = len(started["ins"])
    shapes = started["shapes"]
    na = len(after)

    def body(*refs):
        in_refs, land_refs = refs[:n], refs[n:2 * n]
        send_sems, recv_sems = refs[2 * n], refs[2 * n + 1]
        sends, recvs = copies_fn(in_refs, land_refs, send_sems, recv_sems, shapes)
        for cp in sends:
            cp.wait_send()
        for cp in recvs:
            cp.wait_recv()

    arrs = list(started["ins"]) + list(started["lands"])
    res = pl.pallas_call(
        body, name=name,
        out_shape=tuple(pltpu.HBM(a.shape, a.dtype) for a in arrs),
        in_specs=[HBM] * (2 * n) + [SEM, SEM] + [pl.BlockSpec(memory_space=pl.ANY)] * na,
        out_specs=(HBM,) * (2 * n), input_output_aliases={i: i for i in range(2 * n)},
        compiler_params=pltpu.CompilerParams(has_side_effects=EFFECT),
    )(*arrs, *started["sems"], *after)
    return res[:n], res[n:]


def _sum_pieces(gb, land, ids, f_prev, blk, nblk, name):
    _, a, b = gb.shape
    rows = _tile(a // 2, (ROWS_EW, 176, 64, 32, 16))
    nrt = (a // 2) // rows

    def body(ids_ref, g_ref, l_ref, *rest):
        o_ref = rest[-1]
        acc = g_ref[0].astype(f32)
        for r in range(1, N_REL):
            acc = acc + l_ref[r].astype(f32)
        o_ref[...] = acc

    in_specs = [pl.BlockSpec((1, rows, b), lambda i, ids_ref: (ids_ref[0], ids_ref[1] * nrt + i, 0)),
                pl.BlockSpec((N_REL, rows, b), lambda i, ids_ref: (0, i, 0))]
    args = [ids, gb, land]
    aliases = {}
    if f_prev is not None:
        in_specs.append(pl.BlockSpec(memory_space=pl.ANY))
        args.append(f_prev)
        aliases = {3: 0}
    grid_spec = pltpu.PrefetchScalarGridSpec(
        num_scalar_prefetch=1, grid=(nrt,), in_specs=in_specs,
        out_specs=pl.BlockSpec((rows, b), lambda i, ids_ref: ((2 * blk + ids_ref[1]) * nrt + i, 0)))
    return pl.pallas_call(body, grid_spec=grid_spec, out_shape=_sds((nblk * a, b), f32),
                          input_output_aliases=aliases, compiler_params=_cp(), name=name)(*args)


def _sibling_fill_blocks(fs, nblks, name):
    n = len(fs)
    shapes = [f.shape for f in fs]

    def body(*refs):
        ins, outs = refs[:n], refs[n:2 * n]
        send_sems, recv_sems = refs[2 * n:]
        x, y, c = _coords()
        cps, waits = [], []
        k = 0
        for p in range(n):
            a = shapes[p][0] // nblks[p]
            for bi in range(nblks[p]):
                mine = pl.ds(pl.multiple_of(bi * a + c * (a // 2), 8), a // 2)
                theirs = pl.ds(pl.multiple_of(bi * a + (1 - c) * (a // 2), 8), a // 2)
                sems = dict(send_sem=send_sems.at[k], recv_sem=recv_sems.at[k], device_id=(x, y, 1 - c), device_id_type=MESH)
                cp = pltpu.make_async_remote_copy(src_ref=ins[p].at[mine], dst_ref=outs[p].at[mine], **sems)
                cp.start()
                cps.append(cp)
                waits.append(pltpu.make_async_remote_copy(src_ref=ins[p].at[theirs], dst_ref=outs[p].at[theirs], **sems))
                k += 1
        for wt_ in waits:
            wt_.wait_recv()
        for cp in cps:
            cp.wait_send()

    total = sum(nblks)
    return pl.pallas_call(
        body, in_specs=[ANY] * n, out_specs=[ANY] * n,
        out_shape=[_sds(f.shape, f.dtype) for f in fs],
        input_output_aliases={p: p for p in range(n)},
        scratch_shapes=[pltpu.SemaphoreType.DMA((total,)), pltpu.SemaphoreType.DMA((total,))],
        name=name)(*fs)


WEIGHTS = ["norm_mix", "norm_ffn", "norm_final", "ab_w_in", "gdn_conv_w", "gdn_a_log", "gdn_dt_bias", "gdn_norm",
           "hgrn_lower_bounds", "hgrn_norm", "ab_w_out", "c_w_in", "c_conv_w", "c_conv_b", "c_gate_a_w", "c_gate_a_b",
           "c_gate_x_w", "c_gate_x_b", "c_lambda", "c_w_out", "ffn_w_up", "ffn_conv_w", "ffn_conv_b", "ffn_w_down"]
BIG = {"ab_w_in": "col", "ab_w_out": "row", "c_w_in": "col", "c_gate_a_w": "gate", "c_gate_x_w": "gate",
       "c_w_out": "row", "ffn_w_up": "col", "ffn_w_down": "row"}
SMALL_SHARDED = ["gdn_conv_w", "c_conv_w", "c_conv_b", "c_gate_a_b", "c_gate_x_b", "c_lambda", "ffn_conv_w"]
SMALL = [n for n in WEIGHTS if n not in BIG]
FULL_SHAPES = {
    "norm_mix": (4, 1024), "norm_ffn": (4, 1024), "norm_final": (1024,), "ab_w_in": (2, 1024, 4104),
    "gdn_conv_w": (2, 4, 1536), "gdn_a_log": (2, 4), "gdn_dt_bias": (2, 4), "gdn_norm": (2, 128),
    "hgrn_lower_bounds": (2, 512), "hgrn_norm": (2, 128), "ab_w_out": (2, 1024, 1024), "c_w_in": (2, 1024, 2048),
    "c_conv_w": (2, 4, 1024), "c_conv_b": (2, 1024), "c_gate_a_w": (2, 4, 256, 256), "c_gate_a_b": (2, 1024),
    "c_gate_x_w": (2, 4, 256, 256), "c_gate_x_b": (2, 1024), "c_lambda": (2, 1024), "c_w_out": (2, 1024, 1024),
    "ffn_w_up": (4, 1024, 5632), "ffn_conv_w": (4, 3, 2816), "ffn_conv_b": (4, 2816), "ffn_w_down": (4, 2816, 1024)}


def _shard2d(name, shard):
    return shard.reshape(-1, shard.shape[-1])


def _full_from_slots(name, slots):
    full = FULL_SHAPES[name]
    kind = BIG[name]
    if kind == "col":
        l, r, cdim = full
        return slots.reshape(4, l, r, cdim // 4).transpose(1, 2, 0, 3).reshape(full)
    if kind == "row":
        l, r, cdim = full
        return slots.reshape(4, l, r // 4, cdim).transpose(1, 0, 2, 3).reshape(full)
    l, h, r, cdim = full
    return slots.reshape(4, l, h, r // 4, cdim).transpose(1, 2, 0, 3, 4).reshape(full)


def _slots_from_full(name, g):
    full = FULL_SHAPES[name]
    kind = BIG[name]
    if kind == "col":
        l, r, cdim = full
        return g.reshape(l, r, 4, cdim // 4).transpose(2, 0, 1, 3).reshape(4, l * r, cdim // 4)
    if kind == "row":
        l, r, cdim = full
        return g.reshape(l, 4, r // 4, cdim).transpose(1, 0, 2, 3).reshape(4, l * (r // 4), cdim)
    l, h, r, cdim = full
    return g.reshape(l, h, 4, r // 4, cdim).transpose(2, 0, 1, 3, 4).reshape(4, l * h * (r // 4), cdim)


def _pack_rows(arrs, rows):
    parts, used = [], 0
    for a in arrs:
        r = _pack_tile_rows(a.size)
        parts.append(jnp.pad(a.reshape(-1), (0, r * 128 - a.size)).reshape(r, 128))
        used += r
    assert rows >= used and (rows - used) % 8 == 0, (rows, used)
    if rows > used:
        parts.append(jnp.zeros((rows - used, 128), f32))
    return jnp.concatenate(parts, axis=0)


def _pack_tile_rows(size):
    return -(-size // 1024) * 8


def _unpack_rows(pack, shapes):
    out, row = [], 0
    for s in shapes:
        size = 1
        for d in s:
            size *= d
        r = _pack_tile_rows(size)
        out.append(pack[row:row + r].reshape(-1)[:size].reshape(s))
        row += r
    return out


def kernel(x, norm_mix, norm_ffn, norm_final, ab_w_in, gdn_conv_w, gdn_a_log, gdn_dt_bias, gdn_norm, hgrn_lower_bounds, hgrn_norm, ab_w_out, c_w_in, c_conv_w, c_conv_b, c_gate_a_w, c_gate_a_b, c_gate_x_w, c_gate_x_b, c_lambda, c_w_out, ffn_w_up, ffn_conv_w, ffn_conv_b, ffn_w_down, loss_target, m_norm_mix, m_norm_ffn, m_norm_final, m_ab_w_in, m_gdn_conv_w, m_gdn_a_log, m_gdn_dt_bias, m_gdn_norm, m_hgrn_lower_bounds, m_hgrn_norm, m_ab_w_out, m_c_w_in, m_c_conv_w, m_c_conv_b, m_c_gate_a_w, m_c_gate_a_b, m_c_gate_x_w, m_c_gate_x_b, m_c_lambda, m_c_w_out, m_ffn_w_up, m_ffn_conv_w, m_ffn_conv_b, m_ffn_w_down, v_norm_mix, v_norm_ffn, v_norm_final, v_ab_w_in, v_gdn_conv_w, v_gdn_a_log, v_gdn_dt_bias, v_gdn_norm, v_hgrn_lower_bounds, v_hgrn_norm, v_ab_w_out, v_c_w_in, v_c_conv_w, v_c_conv_b, v_c_gate_a_w, v_c_gate_a_b, v_c_gate_x_w, v_c_gate_x_b, v_c_lambda, v_c_w_out, v_ffn_w_up, v_ffn_conv_w, v_ffn_conv_b, v_ffn_w_down):
    w = dict(zip(WEIGHTS, (norm_mix, norm_ffn, norm_final, ab_w_in, gdn_conv_w, gdn_a_log, gdn_dt_bias, gdn_norm, hgrn_lower_bounds, hgrn_norm, ab_w_out, c_w_in, c_conv_w, c_conv_b, c_gate_a_w, c_gate_a_b, c_gate_x_w, c_gate_x_b, c_lambda, c_w_out, ffn_w_up, ffn_conv_w, ffn_conv_b, ffn_w_down)))
    m = dict(zip(WEIGHTS, (m_norm_mix, m_norm_ffn, m_norm_final, m_ab_w_in, m_gdn_conv_w, m_gdn_a_log, m_gdn_dt_bias, m_gdn_norm, m_hgrn_lower_bounds, m_hgrn_norm, m_ab_w_out, m_c_w_in, m_c_conv_w, m_c_conv_b, m_c_gate_a_w, m_c_gate_a_b, m_c_gate_x_w, m_c_gate_x_b, m_c_lambda, m_c_w_out, m_ffn_w_up, m_ffn_conv_w, m_ffn_conv_b, m_ffn_w_down)))
    v = dict(zip(WEIGHTS, (v_norm_mix, v_norm_ffn, v_norm_final, v_ab_w_in, v_gdn_conv_w, v_gdn_a_log, v_gdn_dt_bias, v_gdn_norm, v_hgrn_lower_bounds, v_hgrn_norm, v_ab_w_out, v_c_w_in, v_c_conv_w, v_c_conv_b, v_c_gate_a_w, v_c_gate_a_b, v_c_gate_x_w, v_c_gate_x_b, v_c_lambda, v_c_w_out, v_ffn_w_up, v_ffn_conv_w, v_ffn_conv_b, v_ffn_w_down)))
    big = list(BIG)
    chip = 2 * lax.axis_index("x") + lax.axis_index("y")
    ids = jnp.stack([chip, lax.axis_index("c")]).astype(jnp.int32)

    def layer_parts(l):
        j = l // 2
        if l % 2 == 0:
            mix = [("ab_w_in", j), ("ab_w_out", j)]
        else:
            mix = [("c_w_in", j), ("c_gate_a_w", j), ("c_gate_x_w", j), ("c_w_out", j)]
        return mix, [("ffn_w_up", l), ("ffn_w_down", l)]

    def layer_shard(n, i):
        s = w[n][i]
        return s.reshape(-1, s.shape[-1]).astype(bf16)

    small_shard_shapes = [w[n].shape for n in SMALL_SHARDED]
    small_pack = _pack_rows([w[n] for n in SMALL_SHARDED], 160)
    mix0, ffn0 = layer_parts(0)
    gathered0 = _all_gather_chips([layer_shard(n, i) for n, i in mix0] + [small_pack], "all_gather_mixer0")
    gathers = {}

    def start_gather(l, after):
        parts = ffn0 if l == 0 else sum(layer_parts(l), [])
        shards = [layer_shard(n, i) for n, i in parts]
        gathers[l] = _split_start(_gather_copies, shards, [_sds((N_CHIPS,) + s.shape, bf16) for s in shards],
                                  "gather_start_%d" % l, after)
        return gathers[l]["token"]

    first_tokens = (start_gather(0, ()), start_gather(1, ()))
    fw = {}
    per_chip = [_unpack_rows(gathered0[-1][k], small_shard_shapes) for k in range(N_CHIPS)]
    for i, n in enumerate(SMALL_SHARDED):
        fw[n] = jnp.concatenate([per_chip[k][i] for k in range(N_CHIPS)], axis=-1)
    for n in SMALL:
        if n not in fw:
            fw[n] = w[n]
    wt = _layout_weights(fw)
    for n in big:
        wt[n] = [None] * FULL_SHAPES[n][0]

    def pre_layer(l, part, x_l):
        deps = ()
        if l == 0 and part == "mix":
            parts, lands, deps = mix0, gathered0[:len(mix0)], first_tokens
        elif (l == 0) == (part == "ffn"):
            parts = ffn0 if l == 0 else sum(layer_parts(l), [])
            _, lands = _split_wait(_gather_copies, gathers[l], [x_l], "gather_wait_%d" % l)
            if l + 2 <= 3:
                deps = (start_gather(l + 2, lands[:1]),)
        else:
            return deps
        for (n, i), slots in zip(parts, lands):
            wt[n][i] = _layer_full(n, slots)
        return deps

    scatters = []

    def post_grads(l, part, g):
        parts = layer_parts(l)[0 if part == "mix" else 1]
        slots = [_layer_slots(n, g[n][i]) for n, i in parts]
        st = _split_start(_scatter_copies, slots, [_sds((N_REL, s.shape[1] // 2, s.shape[2]), bf16) for s in slots],
                          "scatter_start_%d_%s" % (l, part))
        scatters.append((parts, st, "scatter_wait_%d_%s" % (l, part)))
        return (st["token"],)

    loss, dx, g = _local_step(x[0], loss_target[0], wt, pre_layer, post_grads)
    gf = _unlayout_grads(g)
    loss = lax.psum(loss[0, 0], ("x", "y", "c"))

    out_g, out_d, out_m, out_v = {}, {}, {}, {}
    f = {n: None for n in big}
    last = [n for n, _ in layer_parts(0)[0]]

    def finish(group, after):
        for parts, st, wait_name in group:
            gbs, lands = _split_wait(_scatter_copies, st, after, wait_name)
            for (n, i), gb, land in zip(parts, gbs, lands):
                f[n] = _sum_pieces(gb, land, ids, f[n], i, FULL_SHAPES[n][0], "rs_sum")

    def adamw(names, tag):
        filled = _sibling_fill_blocks([f[n] for n in names], [FULL_SHAPES[n][0] for n in names], "rs_sibling_fill_" + tag)
        for n, g_n in zip(names, filled):
            shp = w[n].shape
            if n in NATIVE_PERM:
                to_native = lambda t: t.transpose(NATIVE_PERM[n]).reshape(shp[2], shp[0] * shp[1])
                res = _adamw_cols(to_native(w[n]), to_native(g_n.reshape(shp)), to_native(m[n]), to_native(v[n]), "adamw_" + n)
                res = [t.reshape(shp[2], shp[0], shp[1]).transpose(1, 2, 0) for t in res]
            else:
                as3d = lambda t: t.reshape((-1,) + shp[-2:])
                res = _adamw_big(as3d(w[n]), g_n, as3d(m[n]), as3d(v[n]), "adamw_" + n)
            out_g[n], out_d[n], out_m[n], out_v[n] = (t.reshape(shp) for t in res)

    finish(scatters[:-1], [dx])
    adamw([n for n in big if n not in last], "a")
    finish(scatters[-1:], [out_v["ffn_w_up"]])
    adamw(last, "b")

    small_full_shapes = [FULL_SHAPES[n] for n in SMALL]
    small_sum = _all_reduce_small(_pack_rows([gf[n] for n in SMALL], 688), "all_reduce_small", [f[last[0]]])
    g_small = dict(zip(SMALL, _unpack_rows(small_sum, small_full_shapes)))
    for n in SMALL_SHARDED:
        width = w[n].shape[-1]
        g_small[n] = lax.dynamic_slice_in_dim(g_small[n], chip * width, width, axis=-1)

    def small2d(n, t):
        if n in NATIVE_PERM:
            t = t.transpose(NATIVE_PERM[n])
        return t.reshape(-1, t.shape[-1])

    sg, sd, sm, sv = _adamw_small([small2d(n, w[n]) for n in SMALL], [small2d(n, g_small[n]) for n in SMALL],
                                  [small2d(n, m[n]) for n in SMALL], [small2d(n, v[n]) for n in SMALL], "adamw_small")
    for i, n in enumerate(SMALL):
        for out, t in zip((out_g, out_d, out_m, out_v), (sg, sd, sm, sv)):
            if n in NATIVE_PERM:
                perm = NATIVE_PERM[n]
                shp_t = tuple(w[n].shape[p] for p in perm)
                out[n] = t[i].reshape(shp_t).transpose(tuple(perm.index(k) for k in range(len(perm))))
            else:
                out[n] = t[i].reshape(w[n].shape)
    return (loss, dx[None], *[out_g[n] for n in WEIGHTS], *[out_d[n] for n in WEIGHTS],
            *[out_m[n] for n in WEIGHTS], *[out_v[n] for n in WEIGHTS])
```

```python
import functools

import jax
import jax.numpy as jnp
from jax import lax
from jax.experimental import pallas as pl
from jax.experimental.pallas import tpu as pltpu

f32 = jnp.float32
bf16 = jnp.bfloat16
HI = lax.Precision.HIGHEST

D = 1024
EPS = 1e-6
F_FLOOR = 1e-30
GDN_CHUNK = 64
GDN_INTRA_CHUNKS = 1
HGRN_CHUNK = 16
HGRN_STEP = 128
HEAD = 128
NH = 4
LRU_BLOCK = 256
D_FF = 2816
RG_C = 8.0
C_QKV, C_Z, C_QB, C_FB, C_IB, C_GB, C_BA = 0, 1536, 2048, 2560, 3072, 3584, 4096
AB_COLS_PAD = 4224
TT = 256
N_SLOTS = 4
VMEM_LIMIT = 56 * 1024 * 1024

ADAM_LR, ADAM_B1, ADAM_B2, ADAM_EPS, ADAM_WD, ADAM_STEP = 0.001, 0.9, 0.999, 1e-08, 0.01, 10


def _cp(**kw):
    return pltpu.CompilerParams(vmem_limit_bytes=VMEM_LIMIT, **kw)


def _sds(shape, dtype):
    return jax.ShapeDtypeStruct(shape, dtype)


def _dot(a, b, dims, precision=None):
    return lax.dot_general(a, b, (dims, ((), ())), precision=precision, preferred_element_type=f32)


NN = ((1,), (0,))
NT = ((1,), (1,))
TN = ((0,), (0,))


def _rms(x, g):
    return x * lax.rsqrt(jnp.mean(x * x, axis=-1, keepdims=True) + EPS) * g


def _silu(x):
    return x * jax.nn.sigmoid(x)


def _mm(a, b, mode, tm, tn, out_dtype, name):
    if mode == "nn":
        (m, k), n = a.shape, b.shape[1]
        a_spec = pl.BlockSpec((tm, k), lambda i, j: (i, 0))
        b_spec = pl.BlockSpec((k, tn), lambda i, j: (0, j))
        dims = NN
    elif mode == "nt":
        (m, k), n = a.shape, b.shape[0]
        a_spec = pl.BlockSpec((tm, k), lambda i, j: (i, 0))
        b_spec = pl.BlockSpec((tn, k), lambda i, j: (j, 0))
        dims = NT
    else:
        (k, m), n = a.shape, b.shape[1]
        a_spec = pl.BlockSpec((k, tm), lambda i, j: (0, i))
        b_spec = pl.BlockSpec((k, tn), lambda i, j: (0, j))
        dims = TN
    assert m % tm == 0 and n % tn == 0, (name, m, n, tm, tn)

    def body(a_ref, b_ref, o_ref):
        o_ref[...] = _dot(a_ref[...], b_ref[...], dims).astype(out_dtype)

    return pl.pallas_call(
        body, grid=(m // tm, n // tn), in_specs=[a_spec, b_spec],
        out_specs=pl.BlockSpec((tm, tn), lambda i, j: (i, j)),
        out_shape=_sds((m, n), out_dtype), compiler_params=_cp(), name=name)(a, b)


def _mm_res_norm(a, b, res, gain, name, tm=512):
    (m, k), n = a.shape, b.shape[1]
    tm = min(tm, m)
    assert n == D and m % tm == 0, (name, m, n)

    def body(a_ref, b_ref, r_ref, *rest):
        xv = r_ref[...] + _dot(a_ref[...], b_ref[...], NN)
        if gain is None:
            rest[0][...] = xv
        else:
            g_ref, x_ref, h_ref = rest
            x_ref[...] = xv
            h_ref[...] = _rms(xv, g_ref[...]).astype(bf16)

    row = pl.BlockSpec((tm, D), lambda i: (i, 0))
    in_specs = [pl.BlockSpec((tm, k), lambda i: (i, 0)), pl.BlockSpec((k, D), lambda i: (0, 0)), row]
    args = [a, b, res]
    if gain is None:
        out_specs, out_shape = row, _sds((m, D), f32)
    else:
        in_specs.append(pl.BlockSpec((1, D), lambda i: (0, 0)))
        args.append(gain)
        out_specs, out_shape = [row, row], [_sds((m, D), f32), _sds((m, D), bf16)]
    return pl.pallas_call(body, grid=(m // tm,), in_specs=in_specs, out_specs=out_specs, out_shape=out_shape,
                          compiler_params=_cp(), name=name)(*args)


def _tile(n, cands):
    for c in cands:
        if n % c == 0:
            return c
    raise ValueError(n)


def _mm_auto(a, b, mode, out_dtype, name):
    m = a.shape[1] if mode == "tn" else a.shape[0]
    n = b.shape[0] if mode == "nt" else b.shape[1]
    return _mm(a, b, mode, _tile(m, (512, 256, 128)), _tile(n, (1024, 1408, 512, 384, 256, 128)), out_dtype, name)


def _mm_nn_slots(a, bs, out_dtype, name, after=()):
    (m, k), w = a.shape, bs.shape[2]
    tm = _tile(m, (1024, 512, 256, 128))

    def body(a_ref, b_ref, *rest):
        o_ref = rest[len(after)]
        o_ref[...] = _dot(a_ref[...], b_ref[...], NN).astype(out_dtype)

    return pl.pallas_call(
        body, grid=(m // tm, N_SLOTS),
        in_specs=[pl.BlockSpec((tm, k), lambda i, j: (i, 0)), pl.BlockSpec((None, k, w), lambda i, j: (j, 0, 0))]
        + [pl.BlockSpec(memory_space=pl.ANY)] * len(after),
        out_specs=pl.BlockSpec((tm, w), lambda i, j: (i, j)),
        out_shape=_sds((m, N_SLOTS * w), out_dtype), compiler_params=_cp(), name=name)(a, bs, *after)


def _mm_nt_slots(a, bs, name):
    (m, _), (_, r, w) = a.shape, bs.shape
    tm, tn = _tile(m, (512, 256, 128)), _tile(r, (1024, 512, 256, 128))

    def body(a_ref, b_ref, o_ref):
        acc = _dot(a_ref[:, 0:w], b_ref[0], NT)
        for s in range(1, N_SLOTS):
            acc = acc + _dot(a_ref[:, s * w:(s + 1) * w], b_ref[s], NT)
        o_ref[...] = acc

    return pl.pallas_call(
        body, grid=(m // tm, r // tn),
        in_specs=[pl.BlockSpec((tm, N_SLOTS * w), lambda i, j: (i, 0)), pl.BlockSpec((N_SLOTS, tn, w), lambda i, j: (0, j, 0))],
        out_specs=pl.BlockSpec((tm, tn), lambda i, j: (i, j)),
        out_shape=_sds((m, r), f32), compiler_params=_cp(), name=name)(a, bs)


def _mm_nt_rmsbwd(a, b, x, gain, dres, name, after=()):
    m, k = a.shape
    slots = b.ndim == 3
    assert b.shape[-2] == D
    tm = _tile(m, (512, 256, 128)) if k < 4096 else _tile(m, (256, 128))

    def body(a_ref, b_ref, x_ref, g_ref, r_ref, *rest):
        dx_ref, dxb_ref, dg_ref = rest[len(after):]
        if slots:
            w = b.shape[2]
            dh = _dot(a_ref[:, 0:w], b_ref[0], NT)
            for s in range(1, N_SLOTS):
                dh = dh + _dot(a_ref[:, s * w:(s + 1) * w], b_ref[s], NT)
        else:
            dh = _dot(a_ref[...], b_ref[...], NT)
        _, vjp = jax.vjp(_rms, x_ref[...], g_ref[...])
        dx, dg = vjp(dh)
        dx = dx + r_ref[...]
        dx_ref[...] = dx
        dxb_ref[...] = dx.astype(bf16)

        @pl.when(pl.program_id(0) == 0)
        def _():
            dg_ref[...] = jnp.zeros_like(dg_ref)

        dg_ref[...] += dg

    row = pl.BlockSpec((tm, D), lambda i: (i, 0))
    vec = pl.BlockSpec((1, D), lambda i: (0, 0))
    b_spec = pl.BlockSpec(b.shape, (lambda i: (0, 0, 0)) if slots else (lambda i: (0, 0)))
    return pl.pallas_call(
        body, grid=(m // tm,),
        in_specs=[pl.BlockSpec((tm, k), lambda i: (i, 0)), b_spec, row, vec, row] + [pl.BlockSpec(memory_space=pl.ANY)] * len(after),
        out_specs=[row, row, vec],
        out_shape=[_sds((m, D), f32), _sds((m, D), bf16), _sds((1, D), f32)],
        compiler_params=_cp(), name=name)(a, b, x, gain, dres, *after)


def _mm_tn_slots(a, b, name):
    (k, m), w = a.shape, b.shape[1] // N_SLOTS
    tm = _tile(m, (512, 256, 128))

    def body(a_ref, b_ref, o_ref):
        o_ref[...] = _dot(a_ref[...], b_ref[...], TN).astype(bf16)

    return pl.pallas_call(
        body, grid=(N_SLOTS, m // tm),
        in_specs=[pl.BlockSpec((k, tm), lambda j, i: (0, i)), pl.BlockSpec((k, w), lambda j, i: (0, j))],
        out_specs=pl.BlockSpec((None, tm, w), lambda j, i: (j, i, 0)),
        out_shape=_sds((N_SLOTS, m, w), bf16), compiler_params=_cp(), name=name)(a, b)


def _rmsnorm_fwd(x, gain, name, after=()):
    t = x.shape[0]

    def body(x_ref, g_ref, *rest):
        h_ref = rest[len(after)]
        h_ref[...] = _rms(x_ref[...], g_ref[...]).astype(bf16)

    return pl.pallas_call(
        body, grid=(t // TT,),
        in_specs=[pl.BlockSpec((TT, D), lambda i: (i, 0)), pl.BlockSpec((1, D), lambda i: (0, 0))]
        + [pl.BlockSpec(memory_space=pl.ANY)] * len(after),
        out_specs=pl.BlockSpec((TT, D), lambda i: (i, 0)),
        out_shape=_sds((t, D), bf16), compiler_params=_cp(), name=name)(x, gain, *after)


def _rmsnorm_bwd(x, gain, dh, dres, name, after=()):
    t = x.shape[0]

    def body(x_ref, g_ref, dh_ref, dres_ref, *rest):
        dx_ref, dxb_ref, dg_ref = rest[len(after):]
        _, vjp = jax.vjp(_rms, x_ref[...], g_ref[...])
        dx, dg = vjp(dh_ref[...])
        dx = dx + dres_ref[...]
        dx_ref[...] = dx
        dxb_ref[...] = dx.astype(bf16)

        @pl.when(pl.program_id(0) == 0)
        def _():
            dg_ref[...] = jnp.zeros_like(dg_ref)

        dg_ref[...] += dg

    row = pl.BlockSpec((TT, D), lambda i: (i, 0))
    vec = pl.BlockSpec((1, D), lambda i: (0, 0))
    return pl.pallas_call(
        body, grid=(t // TT,), in_specs=[row, vec, row, row] + [pl.BlockSpec(memory_space=pl.ANY)] * len(after),
        out_specs=[row, row, vec],
        out_shape=[_sds((t, D), f32), _sds((t, D), bf16), _sds((1, D), f32)],
        compiler_params=_cp(), name=name)(x, gain, dh, dres, *after)


def _residual_add(x, y, name):
    t = x.shape[0]

    def body(x_ref, y_ref, o_ref):
        o_ref[...] = x_ref[...] + y_ref[...]

    row = pl.BlockSpec((TT, D), lambda i: (i, 0))
    return pl.pallas_call(body, grid=(t // TT,), in_specs=[row, row], out_specs=row,
                          out_shape=_sds((t, D), f32), compiler_params=_cp(), name=name)(x, y)


def _final_loss(x, gain, target, name):
    t = x.shape[0]

    def loss_fn(xv, gv, tv):
        e = _rms(xv, gv) - tv
        return 0.5 * jnp.sum(jnp.mean(e * e, axis=-1))

    def body(x_ref, g_ref, t_ref, loss_ref, dx_ref, dxb_ref, dg_ref):
        val, (dx, dg) = jax.value_and_grad(loss_fn, argnums=(0, 1))(x_ref[...], g_ref[...], t_ref[...])

        @pl.when(pl.program_id(0) == 0)
        def _():
            dg_ref[...] = jnp.zeros_like(dg_ref)
            loss_ref[...] = jnp.zeros_like(loss_ref)

        dx_ref[...] = dx
        dxb_ref[...] = dx.astype(bf16)
        dg_ref[...] += dg
        loss_ref[...] += jnp.full((1, 128), val, f32)

    row = pl.BlockSpec((TT, D), lambda i: (i, 0))
    vec = pl.BlockSpec((1, D), lambda i: (0, 0))
    return pl.pallas_call(
        body, grid=(t // TT,), in_specs=[row, vec, row],
        out_specs=[pl.BlockSpec((1, 128), lambda i: (0, 0)), row, row, vec],
        out_shape=[_sds((1, 128), f32), _sds((t, D), f32), _sds((t, D), bf16), _sds((1, D), f32)],
        compiler_params=_cp(), name=name)(x, gain, target)


def _halo_rows(dtype):
    return 16 if dtype == bf16 else 8


def _conv_fwd(x, col0, c, w, b, name, tc=256, val=None, val_col0=0):
    t = x.shape[0]
    width = w.shape[0]
    nt = t // TT
    hr = _halo_rows(x.dtype)
    hb = TT // hr

    def body(*refs):
        if val is None:
            x_ref, xh_ref, w_ref, b_ref, o_ref, xp = refs
        else:
            x_ref, xh_ref, w_ref, b_ref, v_ref, o_ref, act_ref, xp = refs
        i = pl.program_id(1)
        xp[0:8, :] = jnp.where(i == 0, 0.0, xh_ref[hr - 8:hr, :].astype(f32))
        xp[8:, :] = x_ref[...].astype(f32)
        acc = jnp.zeros((TT, tc), f32) + b_ref[...]
        for k in range(width):
            acc = acc + w_ref[k:k + 1, :] * xp[pl.ds(8 - (width - 1) + k, TT), :]
        o_ref[...] = acc
        if val is not None:
            act_ref[...] = (_silu(acc) * v_ref[...]).astype(bf16)

    in_specs = [
        pl.BlockSpec((TT, tc), lambda j, i: (i, j + col0)),
        pl.BlockSpec((hr, tc), lambda j, i: (jnp.maximum(i * hb - 1, 0), j + col0)),
        pl.BlockSpec((width, tc), lambda j, i: (0, j)),
        pl.BlockSpec((1, tc), lambda j, i: (0, j)),
    ]
    args = [x, x, w, b]
    out_specs = [pl.BlockSpec((TT, tc), lambda j, i: (i, j))]
    out_shape = [_sds((t, c), f32)]
    if val is not None:
        in_specs.append(pl.BlockSpec((TT, tc), lambda j, i: (i, j + val_col0)))
        args.append(val)
        out_specs.append(pl.BlockSpec((TT, tc), lambda j, i: (i, j)))
        out_shape.append(_sds((t, c), bf16))
    res = pl.pallas_call(
        body, grid=(c // tc, nt), in_specs=in_specs, out_specs=out_specs, out_shape=out_shape,
        scratch_shapes=[pltpu.VMEM((TT + 8, tc), f32)], compiler_params=_cp(), name=name)(*args)
    return res[0] if val is None else res


def _conv_bwd(dc, x, col0, w, name, tc=256, dx_dtype=bf16, into=None):
    t, c = dc.shape
    width = w.shape[0]
    nt = t // TT
    hb = TT // 8
    hr = _halo_rows(x.dtype)

    def body(dc_ref, dcn_ref, x_ref, xh_ref, w_ref, *rest):
        dx_ref, dw_ref, db_ref, dcp, xp = rest[(0 if into is None else 1):]
        i = pl.program_id(1)
        dcv = dc_ref[...]
        dcp[0:TT, :] = dcv
        dcp[TT:, :] = jnp.where(i == nt - 1, 0.0, dcn_ref[...])
        xp[0:8, :] = jnp.where(i == 0, 0.0, xh_ref[hr - 8:hr, :].astype(f32))
        xp[8:, :] = x_ref[...].astype(f32)

        @pl.when(i == 0)
        def _():
            dw_ref[...] = jnp.zeros_like(dw_ref)
            db_ref[...] = jnp.zeros_like(db_ref)

        acc = jnp.zeros((TT, tc), f32)
        for k in range(width):
            acc = acc + w_ref[k:k + 1, :] * dcp[pl.ds((width - 1) - k, TT), :]
            dw_ref[k:k + 1, :] += jnp.sum(dcv * xp[pl.ds(8 - (width - 1) + k, TT), :], axis=0, keepdims=True)
        dx_ref[...] = acc.astype(dx_dtype)
        db_ref[...] += jnp.sum(dcv, axis=0, keepdims=True)

    in_specs = [
        pl.BlockSpec((TT, tc), lambda j, i: (i, j)),
        pl.BlockSpec((8, tc), lambda j, i: (jnp.minimum((i + 1) * hb, t // 8 - 1), j)),
        pl.BlockSpec((TT, tc), lambda j, i: (i, j + col0)),
        pl.BlockSpec((hr, tc), lambda j, i: (jnp.maximum(i * (TT // hr) - 1, 0), j + col0)),
        pl.BlockSpec((width, tc), lambda j, i: (0, j)),
    ]
    out_specs = [
        pl.BlockSpec((TT, tc), lambda j, i: (i, j)),
        pl.BlockSpec((8, tc), lambda j, i: (0, j)),
        pl.BlockSpec((1, tc), lambda j, i: (0, j)),
    ]
    args, aliases, dx_shape = [dc, dc, x, x, w], {}, _sds((t, c), dx_dtype)
    if into is not None:
        in_specs.append(pl.BlockSpec(memory_space=pl.ANY))
        args.append(into)
        aliases, dx_shape = {5: 0}, _sds(into.shape, into.dtype)
    return pl.pallas_call(
        body, grid=(c // tc, nt), in_specs=in_specs, out_specs=out_specs,
        out_shape=[dx_shape, _sds((8, c), f32), _sds((1, c), f32)], input_output_aliases=aliases,
        scratch_shapes=[pltpu.VMEM((TT + 8, tc), f32), pltpu.VMEM((TT + 8, tc), f32)],
        compiler_params=_cp(), name=name)(*args)


def _bdot_impl(a, b, dims):
    return _dot(a.astype(bf16), b.astype(bf16), dims)


@functools.partial(jax.custom_vjp, nondiff_argnums=(2,))
def _bdot(a, b, dims):
    return _bdot_impl(a, b, dims)


def _bdot_fwd(a, b, dims):
    return _bdot_impl(a, b, dims), (a, b)


def _bdot_bwd(dims, res, ct):
    a, b = res
    if dims == NN:
        return _bdot_impl(ct, b, NT), _bdot_impl(a, ct, TN)
    if dims == NT:
        return _bdot_impl(ct, b, NN), _bdot_impl(ct, a, TN)
    return _bdot_impl(b, ct, NT), _bdot_impl(a, ct, NN)


_bdot.defvjp(_bdot_fwd, _bdot_bwd)


def _split2(a):
    hi = a.astype(bf16)
    return hi, (a - hi.astype(f32)).astype(bf16)


def _dot3_impl(a, b, dims):
    a_hi, a_lo = _split2(a)
    b_hi, b_lo = _split2(b)
    return (_dot(a_hi, b_hi, dims) + _dot(a_hi, b_lo, dims)) + _dot(a_lo, b_hi, dims)


@functools.partial(jax.custom_vjp, nondiff_argnums=(2,))
def _dot3(a, b, dims):
    return _dot3_impl(a, b, dims)


def _dot3_fwd(a, b, dims):
    return _dot3_impl(a, b, dims), (a, b)


def _dot3_bwd(dims, res, ct):
    a, b = res
    if dims == NN:
        return _dot3_impl(ct, b, NT), _dot3_impl(a, ct, TN)
    if dims == NT:
        return _dot3_impl(ct, b, NN), _dot3_impl(ct, a, TN)
    return _dot3_impl(b, ct, NT), _dot3_impl(a, ct, NN)


_dot3.defvjp(_dot3_fwd, _dot3_bwd)


def _tril_dot_impl(tril, x, dims):
    t = tril.astype(bf16)
    x1 = x.astype(bf16)
    r1 = x - x1.astype(f32)
    x2 = r1.astype(bf16)
    x3 = (r1 - x2.astype(f32)).astype(bf16)
    return (_dot(t, x3, dims) + _dot(t, x2, dims)) + _dot(t, x1, dims)


@jax.custom_vjp
def _cumsum_rows(tril, x):
    return _tril_dot_impl(tril, x, NN)


def _cumsum_rows_fwd(tril, x):
    return _tril_dot_impl(tril, x, NN), tril


def _cumsum_rows_bwd(tril, ct):
    return jnp.zeros_like(tril), _tril_dot_impl(tril, ct, TN)


_cumsum_rows.defvjp(_cumsum_rows_fwd, _cumsum_rows_bwd)


def _gdn_intra(cq, ck, cv, ba, alog_v, dtb_v, hd):
    c = GDN_CHUNK
    lane = lax.broadcasted_iota(jnp.int32, (1, HEAD), 1)
    mb = (lane == hd).astype(f32)
    ma = (lane == hd + NH).astype(f32)
    beta = jax.nn.sigmoid(jnp.sum(ba * mb, axis=1, keepdims=True))
    alpha = jnp.sum(ba * ma, axis=1, keepdims=True)
    alog = jnp.sum(alog_v * ma, axis=1, keepdims=True)
    dtb = jnp.sum(dtb_v * ma, axis=1, keepdims=True)
    g = -jnp.exp(alog) * jax.nn.softplus(alpha + dtb)
    q = _silu(cq)
    q = q * lax.rsqrt(jnp.sum(q * q, axis=-1, keepdims=True) + EPS) * (HEAD ** -0.5)
    k = _silu(ck)
    k = k * lax.rsqrt(jnp.sum(k * k, axis=-1, keepdims=True) + EPS)
    v = _silu(cv)
    row = lax.broadcasted_iota(jnp.int32, (c, c), 0)
    col = lax.broadcasted_iota(jnp.int32, (c, c), 1)
    causal = row >= col
    tril = causal.astype(f32)
    gc = _cumsum_rows(tril, jnp.broadcast_to(g, (c, HEAD)))
    gcc = _cumsum_rows(tril, jnp.broadcast_to(g, (c, c)))
    diff = jnp.where(causal, gcc - gcc.T, 0.0)
    decay = jnp.where(causal, jnp.exp(diff), 0.0)
    kb = k * beta
    nmat = -jnp.where(row > col, _bdot(kb, k, NT) * decay, 0.0)
    egc = jnp.exp(gc)
    sol = jnp.concatenate([v * beta, kb * egc], axis=1)
    p = nmat
    for it in range(6):
        sol = sol + _dot3(p, sol, NN)
        if it < 5:
            p = _dot3(p, p, NN)
    u, w = sol[:, :HEAD], sol[:, HEAD:]
    attn = _bdot(q, k, NT) * decay
    rowv = lax.broadcasted_iota(jnp.int32, (c, 1), 0)
    gc_last = jnp.sum(jnp.where(rowv == c - 1, gc, 0.0), axis=0, keepdims=True)
    return u, w, q * egc, k * jnp.exp(gc_last - gc), attn, jnp.exp(gc_last)


def _gdn_intra4(cq, ck, cv, ba, alog_v, dtb_v):
    c = GDN_CHUNK
    hs = range(NH)
    lane = lax.broadcasted_iota(jnp.int32, (1, HEAD), 1)
    mb = [(lane == h).astype(f32) for h in hs]
    ma = [(lane == h + NH).astype(f32) for h in hs]
    beta = [jax.nn.sigmoid(jnp.sum(ba * mb[h], axis=1, keepdims=True)) for h in hs]
    alpha = [jnp.sum(ba * ma[h], axis=1, keepdims=True) for h in hs]
    alog = [jnp.sum(alog_v * ma[h], axis=1, keepdims=True) for h in hs]
    dtb = [jnp.sum(dtb_v * ma[h], axis=1, keepdims=True) for h in hs]
    g = [-jnp.exp(alog[h]) * jax.nn.softplus(alpha[h] + dtb[h]) for h in hs]
    q = [_silu(cq[h]) for h in hs]
    q = [q[h] * lax.rsqrt(jnp.sum(q[h] * q[h], axis=-1, keepdims=True) + EPS) * (HEAD ** -0.5) for h in hs]
    k = [_silu(ck[h]) for h in hs]
    k = [k[h] * lax.rsqrt(jnp.sum(k[h] * k[h], axis=-1, keepdims=True) + EPS) for h in hs]
    v = [_silu(cv[h]) for h in hs]
    row = lax.broadcasted_iota(jnp.int32, (c, c), 0)
    col = lax.broadcasted_iota(jnp.int32, (c, c), 1)
    causal = row >= col
    tril = causal.astype(f32)
    gc = [_cumsum_rows(tril, jnp.broadcast_to(g[h], (c, HEAD))) for h in hs]
    gcc = [_cumsum_rows(tril, jnp.broadcast_to(g[h], (c, c))) for h in hs]
    decay = [jnp.where(causal, jnp.exp(jnp.where(causal, gcc[h] - gcc[h].T, 0.0)), 0.0) for h in hs]
    kb = [k[h] * beta[h] for h in hs]
    kk = [_bdot(kb[h], k[h], NT) for h in hs]
    p = [-jnp.where(row > col, kk[h] * decay[h], 0.0) for h in hs]
    egc = [jnp.exp(gc[h]) for h in hs]
    sol = [jnp.concatenate([v[h] * beta[h], kb[h] * egc[h]], axis=1) for h in hs]
    for it in range(6):
        upd = [_dot3(p[h], sol[h], NN) for h in hs]
        sol = [sol[h] + upd[h] for h in hs]
        if it < 5:
            p = [_dot3(p[h], p[h], NN) for h in hs]
    qk = [_bdot(q[h], k[h], NT) for h in hs]
    attn = [qk[h] * decay[h] for h in hs]
    rowv = lax.broadcasted_iota(jnp.int32, (c, 1), 0)
    gc_last = [jnp.sum(jnp.where(rowv == c - 1, gc[h], 0.0), axis=0, keepdims=True) for h in hs]
    return ([sol[h][:, :HEAD] for h in hs], [sol[h][:, HEAD:] for h in hs], [q[h] * egc[h] for h in hs],
            [k[h] * jnp.exp(gc_last[h] - gc[h]) for h in hs], attn, [jnp.exp(gc_last[h]) for h in hs])


def _gdn_seq(u, w, q_dec, k_dec, attn, dl, z, s, gain):
    v_new = u - _bdot(w, s, NN)
    o = _bdot(q_dec, s, NN) + _bdot(attn, v_new, NN)
    s_new = s * dl + _bdot(k_dec, v_new, TN)
    return _rms(o, gain) * _silu(z), s_new


def _gdn_seq4(u, w, q_dec, k_dec, attn, dl, z, s, gain):
    hs = range(NH)
    ws = [_bdot(w[h], s[h], NN) for h in hs]
    qs = [_bdot(q_dec[h], s[h], NN) for h in hs]
    v_new = [u[h] - ws[h] for h in hs]
    av = [_bdot(attn[h], v_new[h], NN) for h in hs]
    kv = [_bdot(k_dec[h], v_new[h], TN) for h in hs]
    o = [_rms(qs[h] + av[h], gain) * _silu(z[h]) for h in hs]
    return o, [s[h] * dl[h] + kv[h] for h in hs]


def _hsl(h):
    return slice(h * HEAD, (h + 1) * HEAD)


def _gdn2_fwd(conv, p, alog_v, dtb_v, gain, name):
    t = conv.shape[0]
    c = GDN_CHUNK
    nch = t // c
    w512 = NH * HEAD
    wide = lambda off: pl.BlockSpec((c, w512), lambda n: (n, off))
    vec = pl.BlockSpec((1, HEAD), lambda n: (0, 0))
    attn_spec = pl.BlockSpec((1, NH, c, c), lambda n: (n, 0, 0, 0))
    dl_spec = pl.BlockSpec((1, NH, HEAD), lambda n: (n, 0, 0))

    cps = GDN_INTRA_CHUNKS
    iwide = lambda off: pl.BlockSpec((cps * c, w512), lambda n: (n, off))

    def intra(cq, ck, cv, ba, al, dt, u_ref, w_ref, qd_ref, kd_ref, at_ref, dl_ref):
        for ci in range(cps):
            rows = slice(ci * c, (ci + 1) * c)
            u, w, qd, kd, at, dl = _gdn_intra4([cq[rows, _hsl(h)] for h in range(NH)], [ck[rows, _hsl(h)] for h in range(NH)],
                                               [cv[rows, _hsl(h)] for h in range(NH)], ba[rows, :], al[...], dt[...])
            for h in range(NH):
                u_ref[rows, _hsl(h)] = u[h]
                w_ref[rows, _hsl(h)] = w[h]
                qd_ref[rows, _hsl(h)] = qd[h]
                kd_ref[rows, _hsl(h)] = kd[h]
                at_ref[ci, h] = at[h]
                dl_ref[ci, h:h + 1, :] = dl[h]

    u, w, qd, kd, at, dl = pl.pallas_call(
        intra, grid=(nch // cps,),
        in_specs=[iwide(0), iwide(1), iwide(2), pl.BlockSpec((cps * c, HEAD), lambda n: (n, C_BA // HEAD)), vec, vec],
        out_specs=[iwide(0)] * 4 + [pl.BlockSpec((cps, NH, c, c), lambda n: (n, 0, 0, 0)),
                                    pl.BlockSpec((cps, NH, HEAD), lambda n: (n, 0, 0))],
        out_shape=[_sds((t, w512), f32)] * 4 + [_sds((nch, NH, c, c), f32), _sds((nch, NH, HEAD), f32)],
        compiler_params=_cp(), name=name + "_intra")(conv, conv, conv, p, alog_v, dtb_v)

    def seq(u_ref, w_ref, qd_ref, kd_ref, at_ref, dl_ref, z_ref, gn, o_ref, ss_ref, s_scr):
        @pl.when(pl.program_id(0) == 0)
        def _():
            s_scr[...] = jnp.zeros_like(s_scr)

        hs = range(NH)
        s = [s_scr[h] for h in hs]
        for h in hs:
            ss_ref[0, h] = s[h]
        o, s_new = _gdn_seq4([u_ref[:, _hsl(h)] for h in hs], [w_ref[:, _hsl(h)] for h in hs], [qd_ref[:, _hsl(h)] for h in hs],
                             [kd_ref[:, _hsl(h)] for h in hs], [at_ref[0, h] for h in hs], [dl_ref[0, h:h + 1, :] for h in hs],
                             [z_ref[:, _hsl(h)] for h in hs], s, gn[...])
        for h in hs:
            o_ref[:, _hsl(h)] = o[h].astype(bf16)
            s_scr[h] = s_new[h]

    o, states = pl.pallas_call(
        seq, grid=(nch,),
        in_specs=[wide(0)] * 4 + [attn_spec, dl_spec, wide(C_Z // w512), vec],
        out_specs=[wide(0), pl.BlockSpec((1, NH, HEAD, HEAD), lambda n: (n, 0, 0, 0))],
        out_shape=[_sds((t, w512), bf16), _sds((nch, NH, HEAD, HEAD), f32)],
        scratch_shapes=[pltpu.VMEM((NH, HEAD, HEAD), f32)],
        compiler_params=_cp(), name=name + "_seq")(u, w, qd, kd, at, dl, p, gain)
    return o, dict(u=u, w=w, qd=qd, kd=kd, at=at, dl=dl, states=states)


def _gdn2_bwd(conv, p, alog_v, dtb_v, gain, saved, do, name):
    t = conv.shape[0]
    c = GDN_CHUNK
    nch = t // c
    w512 = NH * HEAD
    rwide = lambda off: pl.BlockSpec((c, w512), lambda n: (nch - 1 - n, off))
    rvec = pl.BlockSpec((1, HEAD), lambda n: (0, 0))
    rattn = pl.BlockSpec((1, NH, c, c), lambda n: (nch - 1 - n, 0, 0, 0))
    rdl = pl.BlockSpec((1, NH, HEAD), lambda n: (nch - 1 - n, 0, 0))

    def seq_bwd(u_ref, w_ref, qd_ref, kd_ref, at_ref, dl_ref, z_ref, gn, ss_ref, do_ref,
                du_ref, dw_ref, dqd_ref, dkd_ref, dat_ref, ddl_ref, dz_ref, dgn_ref, ds_scr):
        @pl.when(pl.program_id(0) == 0)
        def _():
            ds_scr[...] = jnp.zeros_like(ds_scr)
            dgn_ref[...] = jnp.zeros_like(dgn_ref)

        hs = range(NH)
        _, vjp = jax.vjp(_gdn_seq4, [u_ref[:, _hsl(h)] for h in hs], [w_ref[:, _hsl(h)] for h in hs],
                         [qd_ref[:, _hsl(h)] for h in hs], [kd_ref[:, _hsl(h)] for h in hs], [at_ref[0, h] for h in hs],
                         [dl_ref[0, h:h + 1, :] for h in hs], [z_ref[:, _hsl(h)] for h in hs], [ss_ref[0, h] for h in hs], gn[...])
        du, dw, dqd, dkd, dat, ddl, dz, ds, dg = vjp(([do_ref[:, _hsl(h)] for h in hs], [ds_scr[h] for h in hs]))
        for h in hs:
            du_ref[:, _hsl(h)] = du[h]
            dw_ref[:, _hsl(h)] = dw[h]
            dqd_ref[:, _hsl(h)] = dqd[h]
            dkd_ref[:, _hsl(h)] = dkd[h]
            dat_ref[0, h] = dat[h]
            ddl_ref[0, h:h + 1, :] = ddl[h]
            dz_ref[:, _hsl(h)] = dz[h]
            ds_scr[h] = ds[h]
        dgn_ref[...] += dg

    du, dw, dqd, dkd, dat, ddl, dz, dgn = pl.pallas_call(
        seq_bwd, grid=(nch,),
        in_specs=[rwide(0)] * 4 + [rattn, rdl, rwide(C_Z // w512), rvec,
                                   pl.BlockSpec((1, NH, HEAD, HEAD), lambda n: (nch - 1 - n, 0, 0, 0)), rwide(0)],
        out_specs=[rwide(0)] * 4 + [rattn, rdl, rwide(0), rvec],
        out_shape=[_sds((t, w512), f32)] * 4 + [_sds((nch, NH, c, c), f32), _sds((nch, NH, HEAD), f32),
                                                _sds((t, w512), f32), _sds((1, HEAD), f32)],
        scratch_shapes=[pltpu.VMEM((NH, HEAD, HEAD), f32)],
        compiler_params=_cp(), name=name + "_seq")(
            saved["u"], saved["w"], saved["qd"], saved["kd"], saved["at"], saved["dl"], p, gain, saved["states"], do)

    cps = GDN_INTRA_CHUNKS
    wide = lambda off: pl.BlockSpec((cps * c, w512), lambda n: (n, off))
    vec = pl.BlockSpec((1, HEAD), lambda n: (0, 0))
    attn_spec = pl.BlockSpec((cps, NH, c, c), lambda n: (n, 0, 0, 0))
    dl_spec = pl.BlockSpec((cps, NH, HEAD), lambda n: (n, 0, 0))

    def intra_bwd(cq, ck, cv, ba, al, dt, du_ref, dw_ref, dqd_ref, dkd_ref, dat_ref, ddl_ref,
                  dc_ref, dba_ref, dal_ref, ddt_ref):
        @pl.when(pl.program_id(0) == 0)
        def _():
            dal_ref[...] = jnp.zeros_like(dal_ref)
            ddt_ref[...] = jnp.zeros_like(ddt_ref)

        dal = jnp.zeros((1, HEAD), f32)
        ddt = jnp.zeros((1, HEAD), f32)
        for ci in range(cps):
            rows = slice(ci * c, (ci + 1) * c)
            hs = range(NH)
            _, vjp = jax.vjp(_gdn_intra4, [cq[rows, _hsl(h)] for h in hs], [ck[rows, _hsl(h)] for h in hs],
                             [cv[rows, _hsl(h)] for h in hs], ba[rows, :], al[...], dt[...])
            g_q, g_k, g_v, g_ba, g_al, g_dt = vjp((
                [du_ref[rows, _hsl(h)] for h in hs], [dw_ref[rows, _hsl(h)] for h in hs], [dqd_ref[rows, _hsl(h)] for h in hs],
                [dkd_ref[rows, _hsl(h)] for h in hs], [dat_ref[ci, h] for h in hs], [ddl_ref[ci, h:h + 1, :] for h in hs]))
            for h in hs:
                dc_ref[rows, _hsl(h)] = g_q[h]
                dc_ref[rows, _hsl(NH + h)] = g_k[h]
                dc_ref[rows, _hsl(2 * NH + h)] = g_v[h]
            dal = dal + g_al
            ddt = ddt + g_dt
            dba_ref[rows, :] = g_ba
        dal_ref[...] += dal
        ddt_ref[...] += ddt

    dconv, dba, dal, ddt = pl.pallas_call(
        intra_bwd, grid=(nch // cps,),
        in_specs=[wide(0), wide(1), wide(2), pl.BlockSpec((cps * c, HEAD), lambda n: (n, C_BA // HEAD)), vec, vec]
        + [wide(0)] * 4 + [attn_spec, dl_spec],
        out_specs=[pl.BlockSpec((cps * c, 3 * w512), lambda n: (n, 0)), pl.BlockSpec((cps * c, HEAD), lambda n: (n, 0)),
                   vec, vec],
        out_shape=[_sds((t, 3 * w512), f32), _sds((t, HEAD), f32), _sds((1, HEAD), f32), _sds((1, HEAD), f32)],
        compiler_params=_cp(), name=name + "_intra")(conv, conv, conv, p, alog_v, dtb_v, du, dw, dqd, dkd, dat, ddl)
    return dconv, dz, dba, dal, ddt, dgn


def _hgrn_intra(qb, fb, ib, lb):
    c = HGRN_CHUNK
    ns = range(len(qb))
    f = [lb + (1.0 - lb) * jax.nn.sigmoid(fb[i]) for i in ns]
    logf = [jnp.log(jnp.maximum(f[i], F_FLOOR)) for i in ns]
    k = [1.0 - f[i] for i in ns]
    q = [_silu(qb[i]) for i in ns]
    row = lax.broadcasted_iota(jnp.int32, (c, c), 0)
    col = lax.broadcasted_iota(jnp.int32, (c, c), 1)
    tril = (row >= col).astype(f32)
    b = [_cumsum_rows(tril, logf[i]) for i in ns]
    ri = lax.broadcasted_iota(jnp.int32, (c, 1), 0)
    o = [jnp.zeros((c, HEAD), f32) for _ in ns]
    for j in range(c):
        mj = ri == j
        ok = ri >= j
        bj = [jnp.sum(jnp.where(mj, b[i], 0.0), axis=0, keepdims=True) for i in ns]
        kj = [jnp.sum(jnp.where(mj, k[i], 0.0), axis=0, keepdims=True) for i in ns]
        vj = [jnp.sum(jnp.where(mj, ib[i], 0.0), axis=0, keepdims=True) for i in ns]
        e = [jnp.where(ok, jnp.exp(jnp.where(ok, b[i] - bj[i], 0.0)), 0.0) for i in ns]
        s = [jnp.sum(q[i] * kj[i] * e[i], axis=1, keepdims=True) for i in ns]
        o = [o[i] + s[i] * vj[i] for i in ns]
    b_last = [jnp.sum(jnp.where(ri == c - 1, b[i], 0.0), axis=0, keepdims=True) for i in ns]
    return (o, [q[i] * jnp.exp(b[i]) for i in ns], [k[i] * jnp.exp(b_last[i] - b[i]) for i in ns],
            [jnp.exp(b_last[i]) for i in ns])


def _hgrn_seq(o_intra, q_dec, k_dec, dl, v, gb, st, gain):
    o = o_intra + _bdot(q_dec, st, NT)
    st_new = st * dl + _bdot(v, k_dec, TN)
    return _rms(o, gain) * _silu(gb), st_new


def _hgrn_fwd(p, lb, gain, name):
    t = p.shape[0]
    r = HGRN_STEP
    ns = t // r
    nsub = r // HGRN_CHUNK
    blk = lambda off: pl.BlockSpec((r, HEAD), lambda n, h: (n, off // HEAD + h))

    def body(qb, fb, ib, gb, lb_ref, gn, o_ref, ss_ref, s_scr):
        n, h = pl.program_id(0), pl.program_id(1)

        @pl.when(n == 0)
        def _():
            s_scr[h] = jnp.zeros((HEAD, HEAD), f32)

        st = s_scr[h]
        ss_ref[0, 0] = st
        rows = [pl.ds(ch * HGRN_CHUNK, HGRN_CHUNK) for ch in range(nsub)]
        v = [ib[rw, :] for rw in rows]
        oi, qd, kd, dl = _hgrn_intra([qb[rw, :] for rw in rows], [fb[rw, :] for rw in rows], v, lb_ref[...])
        for ch in range(nsub):
            o, st = _hgrn_seq(oi[ch], qd[ch], kd[ch], dl[ch], v[ch], gb[rows[ch], :], st, gn[...])
            o_ref[rows[ch], :] = o.astype(bf16)
        s_scr[h] = st

    return pl.pallas_call(
        body, grid=(ns, NH),
        in_specs=[blk(C_QB), blk(C_FB), blk(C_IB), blk(C_GB),
                  pl.BlockSpec((1, HEAD), lambda n, h: (0, h)), pl.BlockSpec((1, HEAD), lambda n, h: (0, 0))],
        out_specs=[pl.BlockSpec((r, HEAD), lambda n, h: (n, h)),
                   pl.BlockSpec((1, 1, HEAD, HEAD), lambda n, h: (n, h, 0, 0))],
        out_shape=[_sds((t, NH * HEAD), bf16), _sds((ns, NH, HEAD, HEAD), f32)],
        scratch_shapes=[pltpu.VMEM((NH, HEAD, HEAD), f32)],
        compiler_params=_cp(), name=name)(p, p, p, p, lb, gain)


def _hgrn_bwd(p, lb, gain, states, do, name):
    t = p.shape[0]
    r = HGRN_STEP
    ns = t // r
    nsub = r // HGRN_CHUNK
    blk = lambda off: pl.BlockSpec((r, HEAD), lambda n, h: (ns - 1 - n, off // HEAD + h))
    hblk = pl.BlockSpec((r, HEAD), lambda n, h: (ns - 1 - n, h))

    def body(qb, fb, ib, gb, lb_ref, gn, ss_ref, do_ref,
             dqb, dfb, dib, dgb, dlb_ref, dgn_ref, ds_scr, st_scr):
        n, h = pl.program_id(0), pl.program_id(1)

        @pl.when(n == 0)
        def _():
            ds_scr[h] = jnp.zeros((HEAD, HEAD), f32)

        @pl.when((n == 0) & (h == 0))
        def _():
            dgn_ref[...] = jnp.zeros_like(dgn_ref)

        gnv = gn[...]
        rows = [pl.ds(ch * HGRN_CHUNK, HGRN_CHUNK) for ch in range(nsub)]
        v = [ib[rw, :] for rw in rows]
        (oi, qd, kd, dl), vjp_intra = jax.vjp(_hgrn_intra, [qb[rw, :] for rw in rows], [fb[rw, :] for rw in rows], v, lb_ref[...])
        st = ss_ref[0, 0]
        for ch in range(nsub):
            st_scr[ch] = st
            if ch < nsub - 1:
                st = st * dl[ch] + _bdot(v[ch], kd[ch], TN)
        ds = ds_scr[h]
        dgn = jnp.zeros((1, HEAD), f32)
        d_oi, d_qd, d_kd, d_dl, d_v = [None] * nsub, [None] * nsub, [None] * nsub, [None] * nsub, [None] * nsub
        for ch in reversed(range(nsub)):
            _, vjp = jax.vjp(_hgrn_seq, oi[ch], qd[ch], kd[ch], dl[ch], v[ch], gb[rows[ch], :], st_scr[ch], gnv)
            d_oi[ch], d_qd[ch], d_kd[ch], d_dl[ch], d_v[ch], g_g, ds, g_gn = vjp((do_ref[rows[ch], :], ds))
            dgb[rows[ch], :] = g_g
            dgn = dgn + g_gn
        g_q, g_f, g_i, g_lb = vjp_intra((d_oi, d_qd, d_kd, d_dl))
        for ch in range(nsub):
            dqb[rows[ch], :] = g_q[ch]
            dfb[rows[ch], :] = g_f[ch]
            dib[rows[ch], :] = g_i[ch] + d_v[ch]
        ds_scr[h] = ds
        dlb_ref[0] = g_lb
        dgn_ref[...] += dgn

    return pl.pallas_call(
        body, grid=(ns, NH),
        in_specs=[blk(C_QB), blk(C_FB), blk(C_IB), blk(C_GB),
                  pl.BlockSpec((1, HEAD), lambda n, h: (0, h)), pl.BlockSpec((1, HEAD), lambda n, h: (0, 0)),
                  pl.BlockSpec((1, 1, HEAD, HEAD), lambda n, h: (ns - 1 - n, h, 0, 0)),
                  pl.BlockSpec((r, HEAD), lambda n, h: (ns - 1 - n, NH + h))],
        out_specs=[hblk, hblk, hblk, hblk,
                   pl.BlockSpec((1, 1, HEAD), lambda n, h: (n, 0, h)),
                   pl.BlockSpec((1, HEAD), lambda n, h: (0, 0))],
        out_shape=[_sds((t, 512), f32)] * 4 + [_sds((ns, 1, 512), f32), _sds((1, HEAD), f32)],
        scratch_shapes=[pltpu.VMEM((NH, HEAD, HEAD), f32), pltpu.VMEM((nsub, HEAD, HEAD), f32)],
        compiler_params=_cp(), name=name)(p, p, p, p, lb, gain, states, do)


def _lru_gates(xb, wa, wx, ba, bx, lam):
    xh = xb.astype(bf16)
    r = jax.nn.sigmoid(_dot(xh, wa.astype(bf16), NN) + ba)
    i = jax.nn.sigmoid(_dot(xh, wx.astype(bf16), NN) + bx)
    log_a = -RG_C * r * jax.nn.softplus(-lam)
    a = jnp.exp(log_a)
    t2 = 2.0 * log_a
    series = -t2 * (1.0 + t2 * (0.5 + t2 * (1.0 / 6.0 + t2 * (1.0 / 24.0))))
    om = jnp.where(t2 > -1e-2, series, 1.0 - jnp.exp(t2))
    u = jnp.sqrt(jnp.maximum(om, 0.0)) * (i * xb)
    return a, u


def _lru_gates_fwd(xc, wa, wx, ba, bx, lam, name):
    t = xc.shape[0]
    blk = pl.BlockSpec((TT, LRU_BLOCK), lambda h, i: (i, h))
    wsp = pl.BlockSpec((1, LRU_BLOCK, LRU_BLOCK), lambda h, i: (h, 0, 0))
    vsp = pl.BlockSpec((1, LRU_BLOCK), lambda h, i: (0, h))

    def body(x_ref, wa_ref, wx_ref, ba_ref, bx_ref, lam_ref, a_ref, u_ref):
        a, u = _lru_gates(x_ref[...], wa_ref[0], wx_ref[0], ba_ref[...], bx_ref[...], lam_ref[...])
        a_ref[...] = a
        u_ref[...] = u

    return pl.pallas_call(
        body, grid=(NH, t // TT), in_specs=[blk, wsp, wsp, vsp, vsp, vsp], out_specs=[blk, blk],
        out_shape=[_sds((t, D), f32)] * 2, compiler_params=_cp(), name=name)(xc, wa, wx, ba, bx, lam)


def _lru_gates_bwd(xc, wa, wx, ba, bx, lam, da, du, name):
    t = xc.shape[0]
    blk = pl.BlockSpec((TT, LRU_BLOCK), lambda h, i: (i, h))
    wsp = pl.BlockSpec((1, LRU_BLOCK, LRU_BLOCK), lambda h, i: (h, 0, 0))
    vsp = pl.BlockSpec((1, LRU_BLOCK), lambda h, i: (0, h))

    def body(x_ref, wa_ref, wx_ref, ba_ref, bx_ref, lam_ref, da_ref, du_ref,
             dx_ref, dwa_ref, dwx_ref, dba_ref, dbx_ref, dlam_ref):
        @pl.when(pl.program_id(1) == 0)
        def _():
            for r in (dwa_ref, dwx_ref, dba_ref, dbx_ref, dlam_ref):
                r[...] = jnp.zeros_like(r)

        _, vjp = jax.vjp(_lru_gates, x_ref[...], wa_ref[0], wx_ref[0], ba_ref[...], bx_ref[...], lam_ref[...])
        dx, dwa, dwx, dba, dbx, dlam = vjp((da_ref[...], du_ref[...]))
        dx_ref[...] = dx
        dwa_ref[0] += dwa
        dwx_ref[0] += dwx
        dba_ref[...] += dba
        dbx_ref[...] += dbx
        dlam_ref[...] += dlam

    return pl.pallas_call(
        body, grid=(NH, t // TT), in_specs=[blk, wsp, wsp, vsp, vsp, vsp, blk, blk],
        out_specs=[blk, wsp, wsp, vsp, vsp, vsp],
        out_shape=[_sds((t, D), f32), _sds((NH, LRU_BLOCK, LRU_BLOCK), f32), _sds((NH, LRU_BLOCK, LRU_BLOCK), f32),
                   _sds((1, D), f32), _sds((1, D), f32), _sds((1, D), f32)],
        compiler_params=_cp(), name=name)(xc, wa, wx, ba, bx, lam, da, du)


_SCAN_SHIFTS = (1, 2, 4, 8, 16, 32, 64, 128)
_SCAN_PAD = 128


def _gelu(y):
    return jax.nn.gelu(y, approximate=True)


def _lru_scan_fwd(a, u, p2, name):
    t = a.shape[0]
    tc = 128
    blk = pl.BlockSpec((TT, tc), lambda j, i: (i, j))

    def body(a_ref, u_ref, y_ref, h_ref, hg_ref, a_s, b_s, carry):
        i = pl.program_id(1)

        @pl.when(i == 0)
        def _():
            carry[...] = jnp.zeros_like(carry)
            a_s[0:_SCAN_PAD, :] = jnp.ones((_SCAN_PAD, tc), f32)
            b_s[0:_SCAN_PAD, :] = jnp.zeros((_SCAN_PAD, tc), f32)

        av, bv = a_ref[...], u_ref[...]
        for s in _SCAN_SHIFTS:
            a_s[_SCAN_PAD:, :] = av
            b_s[_SCAN_PAD:, :] = bv
            ash = a_s[pl.ds(_SCAN_PAD - s, TT), :]
            bsh = b_s[pl.ds(_SCAN_PAD - s, TT), :]
            bv = bv + av * bsh
            av = av * ash
        h = bv + av * carry[7:8, :]
        h_ref[...] = h
        hg_ref[...] = (h * _gelu(y_ref[...])).astype(bf16)
        carry[...] = h[TT - 8:, :]

    return pl.pallas_call(
        body, grid=(D // tc, t // TT), in_specs=[blk, blk, blk], out_specs=[blk, blk],
        out_shape=[_sds((t, D), f32), _sds((t, D), bf16)],
        scratch_shapes=[pltpu.VMEM((_SCAN_PAD + TT, tc), f32), pltpu.VMEM((_SCAN_PAD + TT, tc), f32),
                        pltpu.VMEM((8, tc), f32)],
        compiler_params=_cp(), name=name)(a, u, p2)


def _lru_scan_bwd(a, h, p2, dhg, name):
    t = a.shape[0]
    tc = 128
    nt = t // TT
    hb = TT // 8
    rblk = pl.BlockSpec((TT, tc), lambda j, i: (nt - 1 - i, j))

    def body(a_ref, an_ref, h_ref, hp_ref, y_ref, dhg_ref, du_ref, da_ref, dy_ref, a_s, b_s, ap, hp, carry):
        i = pl.program_id(1)

        @pl.when(i == 0)
        def _():
            carry[...] = jnp.zeros_like(carry)
            a_s[TT:, :] = jnp.ones((_SCAN_PAD, tc), f32)
            b_s[TT:, :] = jnp.zeros((_SCAN_PAD, tc), f32)

        ap[0:TT, :] = a_ref[...]
        ap[TT:, :] = jnp.where(i == 0, 0.0, an_ref[...])
        hp[0:8, :] = jnp.where(i == nt - 1, 0.0, hp_ref[...])
        hp[8:, :] = h_ref[...]
        y = y_ref[...]
        gate, gvjp = jax.vjp(_gelu, y)
        dhg_v = dhg_ref[...]
        dy_ref[...] = gvjp(dhg_v * h_ref[...])[0]
        av = ap[pl.ds(1, TT), :]
        bv = dhg_v * gate
        for s in _SCAN_SHIFTS:
            a_s[0:TT, :] = av
            b_s[0:TT, :] = bv
            ash = a_s[pl.ds(s, TT), :]
            bsh = b_s[pl.ds(s, TT), :]
            bv = bv + av * bsh
            av = av * ash
        g = bv + av * carry[0:1, :]
        du_ref[...] = g
        da_ref[...] = g * hp[pl.ds(7, TT), :]
        carry[...] = g[0:8, :]

    in_specs = [
        rblk,
        pl.BlockSpec((8, tc), lambda j, i: (jnp.minimum((nt - i) * hb, t // 8 - 1), j)),
        rblk,
        pl.BlockSpec((8, tc), lambda j, i: (jnp.maximum((nt - 1 - i) * hb - 1, 0), j)),
        rblk, rblk,
    ]
    return pl.pallas_call(
        body, grid=(D // tc, nt), in_specs=in_specs, out_specs=[rblk, rblk, rblk],
        out_shape=[_sds((t, D), f32)] * 3,
        scratch_shapes=[pltpu.VMEM((TT + _SCAN_PAD, tc), f32), pltpu.VMEM((TT + _SCAN_PAD, tc), f32),
                        pltpu.VMEM((TT + 8, tc), f32), pltpu.VMEM((TT + 8, tc), f32), pltpu.VMEM((8, tc), f32)],
        compiler_params=_cp(), name=name)(a, a, h, h, p2, dhg)


def _ffn_act_bwd(gc, up, dact, name, tc=256):
    t = gc.shape[0]
    blk = pl.BlockSpec((TT, tc), lambda i, j: (i, j))
    vblk = pl.BlockSpec((TT, tc), lambda i, j: (i, j + D_FF // tc))

    def body(gc_ref, v_ref, da_ref, dgc_ref, dv_ref):
        _, vjp = jax.vjp(lambda g, v: _silu(g) * v, gc_ref[...], v_ref[...])
        dg, dv = vjp(da_ref[...].astype(f32))
        dgc_ref[...] = dg
        dv_ref[...] = dv.astype(bf16)

    return pl.pallas_call(
        body, grid=(t // TT, D_FF // tc), in_specs=[blk, vblk, blk], out_specs=[blk, vblk],
        out_shape=[_sds((t, D_FF), f32), _sds((t, 2 * D_FF), bf16)], compiler_params=_cp(), name=name)(gc, up, dact)


def _lower_bounds_fwd(w):
    def body(w_ref, o0_ref, o1_ref):
        wv = w_ref[...]
        o0, o1 = _lb_rows(wv[0:1, :], wv[1:2, :])
        o0_ref[...] = o0
        o1_ref[...] = o1

    return pl.pallas_call(body, out_shape=[_sds((1, 512), f32)] * 2, name="lower_bounds_fwd")(w)


def _lb_rows(w0, w1):
    m = jnp.maximum(w0, w1)
    e0, e1 = jnp.exp(w0 - m), jnp.exp(w1 - m)
    s = e0 + e1
    p0, p1 = e0 / s, e1 / s
    return p0 - p0, (p0 + p1) - p0


def _lower_bounds_bwd(w, d0, d1):
    def body(w_ref, d0_ref, d1_ref, g0_ref, g1_ref):
        wv = w_ref[...]
        _, vjp = jax.vjp(_lb_rows, wv[0:1, :], wv[1:2, :])
        g0, g1 = vjp((d0_ref[...], d1_ref[...]))
        g0_ref[...] = g0
        g1_ref[...] = g1

    return pl.pallas_call(body, out_shape=[_sds((1, 512), f32)] * 2, name="lower_bounds_bwd")(w, d0, d1)


def _local_step(x, target, wt, pre_layer=None, post_grads=None):
    depth = 4
    res = []
    lb0, lb1 = _lower_bounds_fwd(wt["hgrn_lower_bounds"])
    lbs = [lb0, lb1]
    for layer in range(depth):
        j = layer // 2
        sv = {"x_in": x}
        deps = pre_layer(layer, "mix", x) if pre_layer else ()
        if layer == 0:
            h1 = _rmsnorm_fwd(x, wt["norm_mix"][layer], "rms_fwd", deps)
        sv["h1"] = h1
        if layer % 2 == 0:
            p = _mm_auto(h1, wt["ab_w_in"][j], "nn", f32, "mm_ab_in")
            conv = _conv_fwd(p, 0, 1536, wt["gdn_conv_w"][j], jnp.zeros((1, 1536), f32), "gdn_conv_fwd", tc=768)
            o_a, s_a = _gdn2_fwd(conv, p, wt["alog_v"][j], wt["dtb_v"][j], wt["gdn_norm"][j], "gdn_fwd")
            o_b, s_b = _hgrn_fwd(p, lbs[j], wt["hgrn_norm"][j], "hgrn_fwd")
            o = jnp.concatenate([o_a, o_b], axis=1)
            x, h2 = _mm_res_norm(o, wt["ab_w_out"][j], x, wt["norm_ffn"][layer], "mm_ab_out")
            sv.update(p=p, conv=conv, s_a=s_a, s_b=s_b, o=o)
        else:
            p2 = _mm_nn_slots(h1, wt["c_w_in"][j], f32, "mm_c_in", deps)
            xc = _conv_fwd(p2, 1, D, wt["c_conv_w"][j], wt["c_conv_b"][j], "lru_conv_fwd", tc=D)
            a, u = _lru_gates_fwd(xc, wt["c_gate_a_w"][j], wt["c_gate_x_w"][j], wt["c_gate_a_b"][j],
                                  wt["c_gate_x_b"][j], wt["c_lambda"][j], "lru_gates_fwd")
            h, hg = _lru_scan_fwd(a, u, p2, "lru_scan_fwd")
            x, h2 = _mm_res_norm(hg, wt["c_w_out"][j], x, wt["norm_ffn"][layer], "mm_c_out")
            sv.update(p2=p2, xc=xc, a=a, h=h, hg=hg)
        sv["x_mid"] = x
        deps = pre_layer(layer, "ffn", x) if pre_layer else ()
        up = _mm_nn_slots(h2, wt["ffn_w_up"][layer], bf16, "mm_up", deps)
        gc, act = _conv_fwd(up, 0, D_FF, wt["ffn_conv_w"][layer], wt["ffn_conv_b"][layer], "ffn_conv_fwd",
                            tc=D_FF // 2, val=up, val_col0=2)
        if layer + 1 < depth:
            x, h1 = _mm_res_norm(act, wt["ffn_w_down"][layer], x, wt["norm_mix"][layer + 1], "mm_down")
        else:
            x = _mm_res_norm(act, wt["ffn_w_down"][layer], x, None, "mm_down_last")
        sv.update(h2=h2, up=up, gc=gc, act=act)
        res.append(sv)

    loss, dx, dxb, d_norm_final = _final_loss(x, wt["norm_final"], target, "final_loss")

    g = {k: [None] * len(v) for k, v in wt.items() if isinstance(v, list)}
    g["norm_final"] = d_norm_final
    d_lbs = [None, None]
    for layer in reversed(range(depth)):
        j = layer // 2
        sv = res[layer]
        dact = _mm_auto(dxb, wt["ffn_w_down"][layer], "nt", bf16, "mm_down_dx")
        g["ffn_w_down"][layer] = _mm_auto(sv["act"], dxb, "tn", bf16, "mm_down_dw")
        dgc, dup = _ffn_act_bwd(sv["gc"], sv["up"], dact, "ffn_act_bwd", tc=D_FF // 2)
        dup, dcw, dcb = _conv_bwd(dgc, sv["up"], 0, wt["ffn_conv_w"][layer], "ffn_conv_bwd", tc=D_FF // 2, into=dup)
        g["ffn_conv_w"][layer] = dcw[:3]
        g["ffn_conv_b"][layer] = dcb
        g["ffn_w_up"][layer] = _mm_tn_slots(sv["h2"], dup, "mm_up_dw")
        deps = post_grads(layer, "ffn", g) if post_grads else ()
        dx, dxb, g["norm_ffn"][layer] = _mm_nt_rmsbwd(dup, wt["ffn_w_up"][layer], sv["x_mid"], wt["norm_ffn"][layer], dx,
                                                      "mm_up_dx", deps)
        if layer % 2 == 0:
            do = _mm_auto(dxb, wt["ab_w_out"][j], "nt", f32, "mm_ab_out_dx")
            g["ab_w_out"][j] = _mm_auto(sv["o"], dxb, "tn", bf16, "mm_ab_out_dw")
            dconv, dz, dba, dal, ddt, dgn = _gdn2_bwd(
                sv["conv"], sv["p"], wt["alog_v"][j], wt["dtb_v"][j], wt["gdn_norm"][j], sv["s_a"], do, "gdn_bwd")
            g["alog_v"][j], g["dtb_v"][j], g["gdn_norm"][j] = dal, ddt, dgn
            dqkv, dcw, _ = _conv_bwd(dconv, sv["p"], 0, wt["gdn_conv_w"][j], "gdn_conv_bwd", tc=768)
            g["gdn_conv_w"][j] = dcw[:4]
            dqb, dfb, dib, dgb, dlb, dhn = _hgrn_bwd(sv["p"], lbs[j], wt["hgrn_norm"][j], sv["s_b"], do, "hgrn_bwd")
            g["hgrn_norm"][j] = dhn
            d_lbs[j] = jnp.sum(dlb, axis=0)
            dp = jnp.concatenate([dqkv] + [t_.astype(bf16) for t_ in (dz, dqb, dfb, dib, dgb, dba)], axis=1)
            g["ab_w_in"][j] = _mm_auto(sv["h1"], dp, "tn", bf16, "mm_ab_in_dw")
            dpre, wpre = dp, wt["ab_w_in"][j]
        else:
            dhg = _mm_auto(dxb, wt["c_w_out"][j], "nt", f32, "mm_c_out_dx")
            g["c_w_out"][j] = _mm_auto(sv["hg"], dxb, "tn", bf16, "mm_c_out_dw")
            du, da, dy = _lru_scan_bwd(sv["a"], sv["h"], sv["p2"], dhg, "lru_scan_bwd")
            dxc, dwa, dwx, dba_, dbx_, dlam = _lru_gates_bwd(
                sv["xc"], wt["c_gate_a_w"][j], wt["c_gate_x_w"][j], wt["c_gate_a_b"][j], wt["c_gate_x_b"][j],
                wt["c_lambda"][j], da, du, "lru_gates_bwd")
            g["c_gate_a_w"][j], g["c_gate_x_w"][j] = dwa, dwx
            g["c_gate_a_b"][j], g["c_gate_x_b"][j], g["c_lambda"][j] = dba_, dbx_, dlam
            dxbr, dcw, dcb = _conv_bwd(dxc, sv["p2"], 1, wt["c_conv_w"][j], "lru_conv_bwd", tc=D)
            g["c_conv_w"][j] = dcw[:4]
            g["c_conv_b"][j] = dcb
            dp2 = jnp.concatenate([dy.astype(bf16), dxbr], axis=1)
            g["c_w_in"][j] = _mm_tn_slots(sv["h1"], dp2, "mm_c_in_dw")
            dpre, wpre = dp2, wt["c_w_in"][j]
        deps = post_grads(layer, "mix", g) if post_grads else ()
        dx, dxb, g["norm_mix"][layer] = _mm_nt_rmsbwd(dpre, wpre, sv["x_in"], wt["norm_mix"][layer], dx, "mm_mix_in_dx", deps)
    g0, g1 = _lower_bounds_bwd(wt["hgrn_lower_bounds"], d_lbs[0], d_lbs[1])
    g["hgrn_lower_bounds"] = jnp.concatenate([g0, g1], axis=0)
    return loss, dx, g


def _ab_in_to_compute(w):
    return jnp.concatenate([w[:, :2048], w[:, 2056:4104], w[:, 2048:2056], jnp.zeros((D, 120), w.dtype)], axis=1)


def _ab_in_from_compute(g):
    return jnp.concatenate([g[:, :2048], g[:, 4096:4104], g[:, 2048:4096]], axis=1)


def _lane_vec(v4):
    return jnp.zeros((1, HEAD), f32).at[0, NH:2 * NH].set(v4)


def _layout_weights(fw):
    wt = {}
    wt["norm_mix"] = [fw["norm_mix"][l][None] for l in range(4)]
    wt["norm_ffn"] = [fw["norm_ffn"][l][None] for l in range(4)]
    wt["norm_final"] = fw["norm_final"][None]
    wt["gdn_conv_w"] = [fw["gdn_conv_w"][j] for j in range(2)]
    wt["alog_v"] = [_lane_vec(fw["gdn_a_log"][j]) for j in range(2)]
    wt["dtb_v"] = [_lane_vec(fw["gdn_dt_bias"][j]) for j in range(2)]
    wt["gdn_norm"] = [fw["gdn_norm"][j][None] for j in range(2)]
    wt["hgrn_lower_bounds"] = fw["hgrn_lower_bounds"]
    wt["hgrn_norm"] = [fw["hgrn_norm"][j][None] for j in range(2)]
    wt["c_conv_w"] = [fw["c_conv_w"][j] for j in range(2)]
    for k in ("c_conv_b", "c_gate_a_b", "c_gate_x_b", "c_lambda"):
        wt[k] = [fw[k][j][None] for j in range(2)]
    wt["ffn_conv_w"] = [fw["ffn_conv_w"][l] for l in range(4)]
    wt["ffn_conv_b"] = [fw["ffn_conv_b"][l][None] for l in range(4)]
    if "ab_w_in" in fw:
        wt["ab_w_in"] = [_ab_in_to_compute(fw["ab_w_in"][j].astype(bf16)) for j in range(2)]
        for k in ("ab_w_out", "c_w_out"):
            wt[k] = [fw[k][j].astype(bf16) for j in range(2)]
        wt["c_w_in"] = [_to_slots(fw["c_w_in"][j].astype(bf16)) for j in range(2)]
        for k in ("c_gate_a_w", "c_gate_x_w"):
            wt[k] = [fw[k][j].astype(f32) for j in range(2)]
        wt["ffn_w_up"] = [_to_slots(fw["ffn_w_up"][l].astype(bf16)) for l in range(4)]
        wt["ffn_w_down"] = [fw["ffn_w_down"][l].astype(bf16) for l in range(4)]
    return wt


SLOT_MAJOR = ("c_w_in", "ffn_w_up")
NATIVE_PERM = {"ab_w_in": (2, 0, 1), "ffn_conv_w": (1, 0, 2)}


def _to_slots(wfull):
    k, n = wfull.shape
    return wfull.reshape(k, N_SLOTS, n // N_SLOTS).transpose(1, 0, 2)


def _layer_full(name, slots):
    kind = BIG[name]
    if name in SLOT_MAJOR:
        return slots
    if kind == "col":
        return _ab_in_to_compute(slots.transpose(1, 0, 2).reshape(slots.shape[1], -1))
    if kind == "row":
        return slots.reshape(-1, slots.shape[2])
    return slots.reshape(4, NH, LRU_BLOCK // 4, LRU_BLOCK).transpose(1, 0, 2, 3).reshape(NH, LRU_BLOCK, LRU_BLOCK).astype(f32)


def _layer_slots(name, g):
    kind = BIG[name]
    if name in SLOT_MAJOR:
        return g
    if kind == "col":
        g = _ab_in_from_compute(g)
        r, cdim = g.shape
        return g.reshape(r, 4, cdim // 4).transpose(1, 0, 2).astype(bf16)
    if kind == "row":
        r, cdim = g.shape
        return g.reshape(4, r // 4, cdim).astype(bf16)
    return g.reshape(NH, 4, LRU_BLOCK // 4, LRU_BLOCK).transpose(1, 0, 2, 3).reshape(4, LRU_BLOCK, LRU_BLOCK).astype(bf16)


def _unlayout_grads(g):
    out = {}
    for k in ("norm_mix", "norm_ffn", "gdn_norm", "hgrn_norm", "c_conv_b", "c_gate_a_b", "c_gate_x_b", "c_lambda",
              "ffn_conv_b"):
        out[k] = jnp.concatenate(g[k], axis=0)
    out["norm_final"] = g["norm_final"][0]
    out["ab_w_in"] = jnp.stack([_ab_in_from_compute(t) for t in g["ab_w_in"]])
    out["gdn_a_log"] = jnp.stack([t[0, NH:2 * NH] for t in g["alog_v"]])
    out["gdn_dt_bias"] = jnp.stack([t[0, NH:2 * NH] for t in g["dtb_v"]])
    out["hgrn_lower_bounds"] = g["hgrn_lower_bounds"]
    for k in ("gdn_conv_w", "ab_w_out", "c_conv_w", "c_gate_a_w", "c_gate_x_w", "c_w_out", "ffn_conv_w", "ffn_w_down"):
        out[k] = jnp.stack(g[k])
    for k in SLOT_MAJOR:
        out[k] = jnp.stack([t.transpose(1, 0, 2).reshape(t.shape[1], -1) for t in g[k]])
    return out


MESH = pl.DeviceIdType.MESH
ANY = pl.BlockSpec(memory_space=pl.ANY)
CHIP_RELATIONS = ((1, 0), (0, 1), (1, 1))
N_CHIPS = 4


def _coords():
    return lax.axis_index("x"), lax.axis_index("y"), lax.axis_index("c")


def _flip(v, f):
    return 1 - v if f else v


def _half_rows(c, a, align):
    return pl.ds(pl.multiple_of(c * (a // 2), align), a // 2)


def _all_gather_chips(shards, name):
    n = len(shards)
    shapes = [s.shape for s in shards]

    def body(*refs):
        ins, outs = refs[:n], refs[n:2 * n]
        send_sems, recv_sems = refs[2 * n:]
        x, y, c = _coords()
        me = 2 * x + y
        sibling = (x, y, 1 - c)
        started = []
        for p in range(n):
            cp = pltpu.make_async_remote_copy(
                src_ref=ins[p], dst_ref=outs[p].at[me],
                send_sem=send_sems.at[p, 6], recv_sem=recv_sems.at[p, 6],
                device_id=sibling, device_id_type=MESH)
            cp.start()
            started.append(cp)
        for p in range(n):
            mine = _half_rows(c, shapes[p][0], 16)
            for r, (fx, fy) in enumerate(CHIP_RELATIONS):
                cp = pltpu.make_async_remote_copy(
                    src_ref=ins[p].at[mine], dst_ref=outs[p].at[me, mine],
                    send_sem=send_sems.at[p, r], recv_sem=recv_sems.at[p, r],
                    device_id=(_flip(x, fx), _flip(y, fy), c), device_id_type=MESH)
                cp.start()
                started.append(cp)
        for r, (fx, fy) in enumerate(CHIP_RELATIONS):
            k = 2 * _flip(x, fx) + _flip(y, fy)
            for p in range(n):
                mine = _half_rows(c, shapes[p][0], 16)
                pltpu.make_async_remote_copy(
                    src_ref=ins[p].at[mine], dst_ref=outs[p].at[k, mine],
                    send_sem=send_sems.at[p, r], recv_sem=recv_sems.at[p, r],
                    device_id=(_flip(x, fx), _flip(y, fy), c), device_id_type=MESH).wait_recv()
                fwd = pltpu.make_async_remote_copy(
                    src_ref=outs[p].at[k, mine], dst_ref=outs[p].at[k, mine],
                    send_sem=send_sems.at[p, 3 + r], recv_sem=recv_sems.at[p, 3 + r],
                    device_id=sibling, device_id_type=MESH)
                fwd.start()
                started.append(fwd)
        for r, (fx, fy) in enumerate(CHIP_RELATIONS):
            k = 2 * _flip(x, fx) + _flip(y, fy)
            for p in range(n):
                theirs = _half_rows(1 - c, shapes[p][0], 16)
                pltpu.make_async_remote_copy(
                    src_ref=outs[p].at[k, theirs], dst_ref=outs[p].at[k, theirs],
                    send_sem=send_sems.at[p, 3 + r], recv_sem=recv_sems.at[p, 3 + r],
                    device_id=sibling, device_id_type=MESH).wait_recv()
        for p in range(n):
            pltpu.make_async_remote_copy(
                src_ref=ins[p], dst_ref=outs[p].at[me],
                send_sem=send_sems.at[p, 6], recv_sem=recv_sems.at[p, 6],
                device_id=sibling, device_id_type=MESH).wait_recv()
        for cp in started:
            cp.wait_send()

    return pl.pallas_call(
        body, in_specs=[ANY] * n, out_specs=[ANY] * n,
        out_shape=[_sds((N_CHIPS,) + s.shape, s.dtype) for s in shards],
        scratch_shapes=[pltpu.SemaphoreType.DMA((n, 7)), pltpu.SemaphoreType.DMA((n, 7))],
        name=name)(*shards)


def _sibling_send_other_half(gs, name):
    n = len(gs)
    shapes = [g.shape for g in gs]

    def body(*refs):
        ins, outs = refs[:n], refs[n:2 * n]
        send_sems, recv_sems = refs[2 * n:]
        x, y, c = _coords()
        cps = []
        for p in range(n):
            theirs = _half_rows(1 - c, shapes[p][1], 8)
            cp = pltpu.make_async_remote_copy(
                src_ref=ins[p].at[:, theirs], dst_ref=outs[p],
                send_sem=send_sems.at[p], recv_sem=recv_sems.at[p],
                device_id=(x, y, 1 - c), device_id_type=MESH)
            cp.start()
            cps.append(cp)
        for cp in cps:
            cp.wait()

    return pl.pallas_call(
        body, in_specs=[ANY] * n, out_specs=[ANY] * n,
        out_shape=[_sds((s[0], s[1] // 2, s[2]), f32) for s in shapes],
        scratch_shapes=[pltpu.SemaphoreType.DMA((n,)), pltpu.SemaphoreType.DMA((n,))],
        name=name)(*gs)


def _chip_exchange(ps, name):
    n = len(ps)

    def body(*refs):
        ins, outs = refs[:n], refs[n:2 * n]
        send_sems, recv_sems = refs[2 * n:]
        x, y, c = _coords()
        cps = []
        for p in range(n):
            for r, (fx, fy) in enumerate(CHIP_RELATIONS):
                k = 2 * _flip(x, fx) + _flip(y, fy)
                cp = pltpu.make_async_remote_copy(
                    src_ref=ins[p].at[k], dst_ref=outs[p].at[r],
                    send_sem=send_sems.at[p, r], recv_sem=recv_sems.at[p, r],
                    device_id=(_flip(x, fx), _flip(y, fy), c), device_id_type=MESH)
                cp.start()
                cps.append(cp)
        for cp in cps:
            cp.wait_recv()
        for cp in cps:
            cp.wait_send()

    return pl.pallas_call(
        body, in_specs=[ANY] * n, out_specs=[ANY] * n,
        out_shape=[_sds((3,) + p.shape[1:], p.dtype) for p in ps],
        scratch_shapes=[pltpu.SemaphoreType.DMA((n, 3)), pltpu.SemaphoreType.DMA((n, 3))],
        name=name)(*ps)


def _sibling_fill_other_half(fs, name):
    n = len(fs)
    shapes = [f.shape for f in fs]

    def body(*refs):
        ins, outs = refs[:n], refs[n:2 * n]
        send_sems, recv_sems = refs[2 * n:]
        x, y, c = _coords()
        cps = []
        for p in range(n):
            mine = _half_rows(c, shapes[p][0], 8)
            cp = pltpu.make_async_remote_copy(
                src_ref=ins[p].at[mine], dst_ref=outs[p].at[mine],
                send_sem=send_sems.at[p], recv_sem=recv_sems.at[p],
                device_id=(x, y, 1 - c), device_id_type=MESH)
            cp.start()
            cps.append(cp)
        for p in range(n):
            theirs = _half_rows(1 - c, shapes[p][0], 8)
            pltpu.make_async_remote_copy(
                src_ref=ins[p].at[theirs], dst_ref=outs[p].at[theirs],
                send_sem=send_sems.at[p], recv_sem=recv_sems.at[p],
                device_id=(x, y, 1 - c), device_id_type=MESH).wait_recv()
        for cp in cps:
            cp.wait_send()

    return pl.pallas_call(
        body, in_specs=[ANY] * n, out_specs=[ANY] * n,
        out_shape=[_sds(f.shape, f.dtype) for f in fs],
        input_output_aliases={p: p for p in range(n)},
        scratch_shapes=[pltpu.SemaphoreType.DMA((n,)), pltpu.SemaphoreType.DMA((n,))],
        name=name)(*fs)


def _sibling_all_gather(ss, name):
    n = len(ss)
    shapes = [s.shape for s in ss]

    def body(*refs):
        ins, outs = refs[:n], refs[n:2 * n]
        send_sems, recv_sems, loc_sems = refs[2 * n:]
        x, y, c = _coords()
        locs, cps = [], []
        for p in range(n):
            mine = _half_rows(c, 2 * shapes[p][0], 8)
            loc = pltpu.make_async_copy(ins[p], outs[p].at[mine], loc_sems.at[p])
            loc.start()
            locs.append(loc)
            cp = pltpu.make_async_remote_copy(
                src_ref=ins[p], dst_ref=outs[p].at[mine],
                send_sem=send_sems.at[p], recv_sem=recv_sems.at[p],
                device_id=(x, y, 1 - c), device_id_type=MESH)
            cp.start()
            cps.append(cp)
        for p, cp in enumerate(cps):
            theirs = _half_rows(1 - c, 2 * shapes[p][0], 8)
            pltpu.make_async_remote_copy(
                src_ref=ins[p], dst_ref=outs[p].at[theirs],
                send_sem=send_sems.at[p], recv_sem=recv_sems.at[p],
                device_id=(x, y, 1 - c), device_id_type=MESH).wait_recv()
        for cp in cps:
            cp.wait_send()
        for loc in locs:
            loc.wait()

    return pl.pallas_call(
        body, in_specs=[ANY] * n, out_specs=[ANY] * n,
        out_shape=[_sds((2 * s[0], s[1]), f32) for s in shapes],
        scratch_shapes=[pltpu.SemaphoreType.DMA((n,)), pltpu.SemaphoreType.DMA((n,)), pltpu.SemaphoreType.DMA((n,))],
        name=name)(*ss)


N_DEV = 8


def _all_reduce_small(pack, name, after=()):
    rows = pack.shape[0]

    def body(in_ref, *rest):
        sum_ref, all_ref, send_sems, recv_sems = rest[len(after):]
        x, y, c = _coords()
        me = 4 * x + 2 * y + c
        all_ref[me] = in_ref[...]
        cps = []
        for r in range(1, N_DEV):
            fx, fy, fc = (r >> 2) & 1, (r >> 1) & 1, r & 1
            cp = pltpu.make_async_remote_copy(
                src_ref=in_ref, dst_ref=all_ref.at[me],
                send_sem=send_sems.at[r], recv_sem=recv_sems.at[r],
                device_id=(_flip(x, fx), _flip(y, fy), _flip(c, fc)), device_id_type=MESH)
            cp.start()
            cps.append(cp)
        for r in range(1, N_DEV):
            fx, fy, fc = (r >> 2) & 1, (r >> 1) & 1, r & 1
            peer = 4 * _flip(x, fx) + 2 * _flip(y, fy) + _flip(c, fc)
            pltpu.make_async_remote_copy(
                src_ref=in_ref, dst_ref=all_ref.at[peer],
                send_sem=send_sems.at[r], recv_sem=recv_sems.at[r],
                device_id=(x, y, c), device_id_type=MESH).wait_recv()
        for cp in cps:
            cp.wait_send()
        acc = all_ref[0]
        for d in range(1, N_DEV):
            acc = acc + all_ref[d]
        sum_ref[...] = acc

    vm = pl.BlockSpec(memory_space=pltpu.VMEM)
    return pl.pallas_call(
        body, in_specs=[vm] + [pl.BlockSpec(memory_space=pl.ANY)] * len(after), out_specs=[vm, vm],
        out_shape=[_sds((rows, 128), f32), _sds((N_DEV, rows, 128), f32)],
        scratch_shapes=[pltpu.SemaphoreType.DMA((N_DEV,)), pltpu.SemaphoreType.DMA((N_DEV,))],
        name=name)(pack, *after)[0]


ROWS_EW = 128


def _add_own_half(g, rs, c_arr, name):
    s, a, b = g.shape
    nrt = (a // 2) // ROWS_EW

    def body(c_ref, g_ref, r_ref, o_ref):
        o_ref[...] = (g_ref[...] + r_ref[...]).astype(bf16)

    grid_spec = pltpu.PrefetchScalarGridSpec(
        num_scalar_prefetch=1, grid=(s, nrt),
        in_specs=[pl.BlockSpec((1, ROWS_EW, b), lambda k, i, c_ref: (k, c_ref[0] * nrt + i, 0)),
                  pl.BlockSpec((1, ROWS_EW, b), lambda k, i, c_ref: (k, i, 0))],
        out_specs=pl.BlockSpec((1, ROWS_EW, b), lambda k, i, c_ref: (k, i, 0)))
    return pl.pallas_call(body, grid_spec=grid_spec, out_shape=_sds((s, a // 2, b), bf16),
                          compiler_params=_cp(), name=name)(c_arr, g, rs)


def _sum_chips(g, rs, rc, ids, name):
    _, r, b = rc.shape
    nrt = r // ROWS_EW

    def body(ids_ref, g_ref, s_ref, r_ref, o_ref):
        own = g_ref[0] + s_ref[0]
        o_ref[...] = ((own + r_ref[0].astype(f32)) + r_ref[1].astype(f32)) + r_ref[2].astype(f32)

    grid_spec = pltpu.PrefetchScalarGridSpec(
        num_scalar_prefetch=1, grid=(nrt,),
        in_specs=[pl.BlockSpec((1, ROWS_EW, b), lambda i, ids_ref: (ids_ref[0], ids_ref[1] * nrt + i, 0)),
                  pl.BlockSpec((1, ROWS_EW, b), lambda i, ids_ref: (ids_ref[0], i, 0)),
                  pl.BlockSpec((3, ROWS_EW, b), lambda i, ids_ref: (0, i, 0))],
        out_specs=pl.BlockSpec((ROWS_EW, b), lambda i, ids_ref: (ids_ref[1] * nrt + i, 0)))
    return pl.pallas_call(body, grid_spec=grid_spec, out_shape=_sds((2 * r, b), f32),
                          compiler_params=_cp(), name=name)(ids, g, rs, rc)


def _adamw_math(w, g, m, v):
    m = ADAM_B1 * m + (1.0 - ADAM_B1) * g
    v = ADAM_B2 * v + (1.0 - ADAM_B2) * (g * g)
    m_hat = m / (1.0 - ADAM_B1 ** ADAM_STEP)
    v_hat = v / (1.0 - ADAM_B2 ** ADAM_STEP)
    delta = -ADAM_LR * (m_hat / (jnp.sqrt(v_hat) + ADAM_EPS) + ADAM_WD * w)
    return delta, m, v


def _adamw_big(w, g, m, v, name):
    nl, r, b = w.shape
    rt = _tile(r, (ROWS_EW, 352, 64))
    per = r // rt

    def body(w_ref, g_ref, m_ref, v_ref, go_ref, d_ref, mo_ref, vo_ref):
        gv = g_ref[...]
        d, mn, vn = _adamw_math(w_ref[...], gv, m_ref[...], v_ref[...])
        go_ref[...] = gv
        d_ref[...] = d
        mo_ref[...] = mn
        vo_ref[...] = vn

    blk = pl.BlockSpec((None, rt, b), lambda l, i: (l, i, 0))
    gblk = pl.BlockSpec((rt, b), lambda l, i: (l * per + i, 0))
    return pl.pallas_call(body, grid=(nl, per), in_specs=[blk, gblk, blk, blk], out_specs=[blk] * 4,
                          out_shape=[_sds((nl, r, b), f32)] * 4, compiler_params=_cp(), name=name)(w, g, m, v)


def _adamw_lead(w, g, m, v, name):
    n, r, b = w.shape
    tn = _tile(n, (64, 54, 32, 16, 8, 1))

    def body(w_ref, g_ref, m_ref, v_ref, go_ref, d_ref, mo_ref, vo_ref):
        gv = g_ref[...]
        d, mn, vn = _adamw_math(w_ref[...], gv, m_ref[...], v_ref[...])
        go_ref[...] = gv
        d_ref[...] = d
        mo_ref[...] = mn
        vo_ref[...] = vn

    blk = pl.BlockSpec((tn, r, b), lambda i: (i, 0, 0))
    return pl.pallas_call(body, grid=(n // tn,), in_specs=[blk] * 4, out_specs=[blk] * 4,
                          out_shape=[_sds((n, r, b), f32)] * 4, compiler_params=_cp(), name=name)(w, g, m, v)


def _adamw_small(ws, gs, ms, vs, name):
    n = len(ws)

    def body(*refs):
        w_r, g_r, m_r, v_r = refs[:n], refs[n:2 * n], refs[2 * n:3 * n], refs[3 * n:4 * n]
        go_r, d_r, mo_r, vo_r = refs[4 * n:5 * n], refs[5 * n:6 * n], refs[6 * n:7 * n], refs[7 * n:8 * n]
        for p in range(n):
            gv = g_r[p][...]
            d, mn, vn = _adamw_math(w_r[p][...], gv, m_r[p][...], v_r[p][...])
            go_r[p][...] = gv
            d_r[p][...] = d
            mo_r[p][...] = mn
            vo_r[p][...] = vn

    vm = pl.BlockSpec(memory_space=pltpu.VMEM)
    shp = [_sds(w.shape, f32) for w in ws]
    res = pl.pallas_call(body, in_specs=[vm] * (4 * n), out_specs=[vm] * (4 * n), out_shape=shp * 4,
                         name=name)(*ws, *gs, *ms, *vs)
    return res[:n], res[n:2 * n], res[2 * n:3 * n], res[3 * n:]


HBM = pl.BlockSpec(memory_space=pltpu.HBM)
SEM = pl.BlockSpec(memory_space=pltpu.SEMAPHORE)
EFFECT = pltpu.SideEffectType.DATAFLOW_SIDE_EFFECTING
N_REL = 8


def _rel(r):
    return (r >> 2) & 1, (r >> 1) & 1, r & 1


def _gather_copies(ins, lands, send_sems, recv_sems, shapes):
    x, y, c = _coords()
    me = 2 * x + y
    sends, recvs = [], []
    for p in range(len(ins)):
        for r in range(1, N_REL):
            fx, fy, fc = _rel(r)
            peer = (_flip(x, fx), _flip(y, fy), _flip(c, fc))
            if fx == 0 and fy == 0:
                src, dst, got = ins[p], lands[p].at[me], lands[p].at[me]
            else:
                mine = _half_rows(c, shapes[p][0], 16)
                theirs = _half_rows(_flip(c, fc), shapes[p][0], 16)
                src, dst = ins[p].at[mine], lands[p].at[me, mine]
                got = lands[p].at[2 * peer[0] + peer[1], theirs]
            sems = dict(send_sem=send_sems.at[p * N_REL + r], recv_sem=recv_sems.at[p * N_REL + r], device_id=peer,
                        device_id_type=MESH)
            sends.append(pltpu.make_async_remote_copy(src_ref=src, dst_ref=dst, **sems))
            recvs.append(pltpu.make_async_remote_copy(src_ref=src, dst_ref=got, **sems))
    return sends, recvs


def _scatter_copies(ins, lands, send_sems, recv_sems, shapes):
    x, y, c = _coords()
    sends, recvs = [], []
    for p in range(len(ins)):
        for r in range(1, N_REL):
            fx, fy, fc = _rel(r)
            peer = (_flip(x, fx), _flip(y, fy), _flip(c, fc))
            theirs = _half_rows(peer[2], shapes[p][1], 16)
            sems = dict(send_sem=send_sems.at[p * N_REL + r], recv_sem=recv_sems.at[p * N_REL + r], device_id=peer,
                        device_id_type=MESH)
            cp = pltpu.make_async_remote_copy(src_ref=ins[p].at[2 * peer[0] + peer[1], theirs], dst_ref=lands[p].at[r], **sems)
            sends.append(cp)
            recvs.append(cp)
    return sends, recvs


def _split_start(copies_fn, ins, land_shapes, name, after=()):
    n = len(ins)
    shapes = [a.shape for a in ins]

    def body(*refs):
        in_refs, land_refs = refs[:n], refs[n:2 * n]
        send_sems, recv_sems = refs[2 * n + len(after)], refs[2 * n + len(after) + 1]
        token = refs[-1]
        sends, _ = copies_fn(in_refs, land_refs, send_sems, recv_sems, shapes)
        for cp in sends:
            cp.start()
        token[...] = jnp.zeros_like(token)

    lands = [lax.empty(s.shape, s.dtype) for s in land_shapes]
    na = len(after)
    res = pl.pallas_call(
        body, name=name,
        out_shape=(pltpu.SemaphoreType.DMA((n * N_REL,)), pltpu.SemaphoreType.DMA((n * N_REL,)))
        + tuple(pltpu.HBM(a.shape, a.dtype) for a in ins) + tuple(pltpu.HBM(s.shape, s.dtype) for s in land_shapes)
        + (_sds((8, 128), f32),),
        in_specs=[HBM] * (2 * n) + [pl.BlockSpec(memory_space=pl.ANY)] * na,
        out_specs=(SEM, SEM) + (HBM,) * (2 * n) + (pl.BlockSpec(memory_space=pltpu.VMEM),),
        input_output_aliases={i: 2 + i for i in range(2 * n)},
        compiler_params=pltpu.CompilerParams(has_side_effects=EFFECT),
    )(*[pltpu.with_memory_space_constraint(a, pltpu.HBM) for a in ins],
      *[pltpu.with_memory_space_constraint(a, pltpu.HBM) for a in lands], *after)
    return dict(sems=res[:2], ins=res[2:2 + n], lands=res[2 + n:2 + 2 * n], token=res[-1], shapes=shapes)


def _split_wait(copies_fn, started, after, name):
    n = len(started["ins"])
    shapes = started["shapes"]
    na = len(after)

    def body(*refs):
        in_refs, land_refs = refs[:n], refs[n:2 * n]
        send_sems, recv_sems = refs[2 * n], refs[2 * n + 1]
        sends, recvs = copies_fn(in_refs, land_refs, send_sems, recv_sems, shapes)
        for cp in sends:
            cp.wait_send()
        for cp in recvs:
            cp.wait_recv()

    arrs = list(started["ins"]) + list(started["lands"])
    res = pl.pallas_call(
        body, name=name,
        out_shape=tuple(pltpu.HBM(a.shape, a.dtype) for a in arrs),
        in_specs=[HBM] * (2 * n) + [SEM, SEM] + [pl.BlockSpec(memory_space=pl.ANY)] * na,
        out_specs=(HBM,) * (2 * n), input_output_aliases={i: i for i in range(2 * n)},
        compiler_params=pltpu.CompilerParams(has_side_effects=EFFECT),
    )(*arrs, *started["sems"], *after)
    return res[:n], res[n:]


def _sum_pieces(gb, land, ids, f_prev, blk, nblk, name):
    _, a, b = gb.shape
    rows = _tile(a // 2, (ROWS_EW, 176, 64, 32, 16))
    nrt = (a // 2) // rows

    def body(ids_ref, g_ref, l_ref, *rest):
        o_ref = rest[-1]
        acc = g_ref[0].astype(f32)
        for r in range(1, N_REL):
            acc = acc + l_ref[r].astype(f32)
        o_ref[...] = acc

    in_specs = [pl.BlockSpec((1, rows, b), lambda i, ids_ref: (ids_ref[0], ids_ref[1] * nrt + i, 0)),
                pl.BlockSpec((N_REL, rows, b), lambda i, ids_ref: (0, i, 0))]
    args = [ids, gb, land]
    aliases = {}
    if f_prev is not None:
        in_specs.append(pl.BlockSpec(memory_space=pl.ANY))
        args.append(f_prev)
        aliases = {3: 0}
    grid_spec = pltpu.PrefetchScalarGridSpec(
        num_scalar_prefetch=1, grid=(nrt,), in_specs=in_specs,
        out_specs=pl.BlockSpec((rows, b), lambda i, ids_ref: ((2 * blk + ids_ref[1]) * nrt + i, 0)))
    return pl.pallas_call(body, grid_spec=grid_spec, out_shape=_sds((nblk * a, b), f32),
                          input_output_aliases=aliases, compiler_params=_cp(), name=name)(*args)


def _sibling_fill_blocks(fs, nblks, name):
    n = len(fs)
    shapes = [f.shape for f in fs]

    def body(*refs):
        ins, outs = refs[:n], refs[n:2 * n]
        send_sems, recv_sems = refs[2 * n:]
        x, y, c = _coords()
        cps, waits = [], []
        k = 0
        for p in range(n):
            a = shapes[p][0] // nblks[p]
            for bi in range(nblks[p]):
                mine = pl.ds(pl.multiple_of(bi * a + c * (a // 2), 8), a // 2)
                theirs = pl.ds(pl.multiple_of(bi * a + (1 - c) * (a // 2), 8), a // 2)
                sems = dict(send_sem=send_sems.at[k], recv_sem=recv_sems.at[k], device_id=(x, y, 1 - c), device_id_type=MESH)
                cp = pltpu.make_async_remote_copy(src_ref=ins[p].at[mine], dst_ref=outs[p].at[mine], **sems)
                cp.start()
                cps.append(cp)
                waits.append(pltpu.make_async_remote_copy(src_ref=ins[p].at[theirs], dst_ref=outs[p].at[theirs], **sems))
                k += 1
        for wt_ in waits:
            wt_.wait_recv()
        for cp in cps:
            cp.wait_send()

    total = sum(nblks)
    return pl.pallas_call(
        body, in_specs=[ANY] * n, out_specs=[ANY] * n,
        out_shape=[_sds(f.shape, f.dtype) for f in fs],
        input_output_aliases={p: p for p in range(n)},
        scratch_shapes=[pltpu.SemaphoreType.DMA((total,)), pltpu.SemaphoreType.DMA((total,))],
        name=name)(*fs)


WEIGHTS = ["norm_mix", "norm_ffn", "norm_final", "ab_w_in", "gdn_conv_w", "gdn_a_log", "gdn_dt_bias", "gdn_norm",
           "hgrn_lower_bounds", "hgrn_norm", "ab_w_out", "c_w_in", "c_conv_w", "c_conv_b", "c_gate_a_w", "c_gate_a_b",
           "c_gate_x_w", "c_gate_x_b", "c_lambda", "c_w_out", "ffn_w_up", "ffn_conv_w", "ffn_conv_b", "ffn_w_down"]
BIG = {"ab_w_in": "col", "ab_w_out": "row", "c_w_in": "col", "c_gate_a_w": "gate", "c_gate_x_w": "gate",
       "c_w_out": "row", "ffn_w_up": "col", "ffn_w_down": "row"}
SMALL_SHARDED = ["gdn_conv_w", "c_conv_w", "c_conv_b", "c_gate_a_b", "c_gate_x_b", "c_lambda", "ffn_conv_w"]
SMALL = [n for n in WEIGHTS if n not in BIG]
FULL_SHAPES = {
    "norm_mix": (4, 1024), "norm_ffn": (4, 1024), "norm_final": (1024,), "ab_w_in": (2, 1024, 4104),
    "gdn_conv_w": (2, 4, 1536), "gdn_a_log": (2, 4), "gdn_dt_bias": (2, 4), "gdn_norm": (2, 128),
    "hgrn_lower_bounds": (2, 512), "hgrn_norm": (2, 128), "ab_w_out": (2, 1024, 1024), "c_w_in": (2, 1024, 2048),
    "c_conv_w": (2, 4, 1024), "c_conv_b": (2, 1024), "c_gate_a_w": (2, 4, 256, 256), "c_gate_a_b": (2, 1024),
    "c_gate_x_w": (2, 4, 256, 256), "c_gate_x_b": (2, 1024), "c_lambda": (2, 1024), "c_w_out": (2, 1024, 1024),
    "ffn_w_up": (4, 1024, 5632), "ffn_conv_w": (4, 3, 2816), "ffn_conv_b": (4, 2816), "ffn_w_down": (4, 2816, 1024)}


def _shard2d(name, shard):
    return shard.reshape(-1, shard.shape[-1])


def _full_from_slots(name, slots):
    full = FULL_SHAPES[name]
    kind = BIG[name]
    if kind == "col":
        l, r, cdim = full
        return slots.reshape(4, l, r, cdim // 4).transpose(1, 2, 0, 3).reshape(full)
    if kind == "row":
        l, r, cdim = full
        return slots.reshape(4, l, r // 4, cdim).transpose(1, 0, 2, 3).reshape(full)
    l, h, r, cdim = full
    return slots.reshape(4, l, h, r // 4, cdim).transpose(1, 2, 0, 3, 4).reshape(full)


def _slots_from_full(name, g):
    full = FULL_SHAPES[name]
    kind = BIG[name]
    if kind == "col":
        l, r, cdim = full
        return g.reshape(l, r, 4, cdim // 4).transpose(2, 0, 1, 3).reshape(4, l * r, cdim // 4)
    if kind == "row":
        l, r, cdim = full
        return g.reshape(l, 4, r // 4, cdim).transpose(1, 0, 2, 3).reshape(4, l * (r // 4), cdim)
    l, h, r, cdim = full
    return g.reshape(l, h, 4, r // 4, cdim).transpose(2, 0, 1, 3, 4).reshape(4, l * h * (r // 4), cdim)


def _pack_rows(arrs, rows):
    parts, used = [], 0
    for a in arrs:
        r = _pack_tile_rows(a.size)
        parts.append(jnp.pad(a.reshape(-1), (0, r * 128 - a.size)).reshape(r, 128))
        used += r
    assert rows >= used and (rows - used) % 8 == 0, (rows, used)
    if rows > used:
        parts.append(jnp.zeros((rows - used, 128), f32))
    return jnp.concatenate(parts, axis=0)


def _pack_tile_rows(size):
    return -(-size // 1024) * 8


def _unpack_rows(pack, shapes):
    out, row = [], 0
    for s in shapes:
        size = 1
        for d in s:
            size *= d
        r = _pack_tile_rows(size)
        out.append(pack[row:row + r].reshape(-1)[:size].reshape(s))
        row += r
    return out


def kernel(x, norm_mix, norm_ffn, norm_final, ab_w_in, gdn_conv_w, gdn_a_log, gdn_dt_bias, gdn_norm, hgrn_lower_bounds, hgrn_norm, ab_w_out, c_w_in, c_conv_w, c_conv_b, c_gate_a_w, c_gate_a_b, c_gate_x_w, c_gate_x_b, c_lambda, c_w_out, ffn_w_up, ffn_conv_w, ffn_conv_b, ffn_w_down, loss_target, m_norm_mix, m_norm_ffn, m_norm_final, m_ab_w_in, m_gdn_conv_w, m_gdn_a_log, m_gdn_dt_bias, m_gdn_norm, m_hgrn_lower_bounds, m_hgrn_norm, m_ab_w_out, m_c_w_in, m_c_conv_w, m_c_conv_b, m_c_gate_a_w, m_c_gate_a_b, m_c_gate_x_w, m_c_gate_x_b, m_c_lambda, m_c_w_out, m_ffn_w_up, m_ffn_conv_w, m_ffn_conv_b, m_ffn_w_down, v_norm_mix, v_norm_ffn, v_norm_final, v_ab_w_in, v_gdn_conv_w, v_gdn_a_log, v_gdn_dt_bias, v_gdn_norm, v_hgrn_lower_bounds, v_hgrn_norm, v_ab_w_out, v_c_w_in, v_c_conv_w, v_c_conv_b, v_c_gate_a_w, v_c_gate_a_b, v_c_gate_x_w, v_c_gate_x_b, v_c_lambda, v_c_w_out, v_ffn_w_up, v_ffn_conv_w, v_ffn_conv_b, v_ffn_w_down):
    w = dict(zip(WEIGHTS, (norm_mix, norm_ffn, norm_final, ab_w_in, gdn_conv_w, gdn_a_log, gdn_dt_bias, gdn_norm, hgrn_lower_bounds, hgrn_norm, ab_w_out, c_w_in, c_conv_w, c_conv_b, c_gate_a_w, c_gate_a_b, c_gate_x_w, c_gate_x_b, c_lambda, c_w_out, ffn_w_up, ffn_conv_w, ffn_conv_b, ffn_w_down)))
    m = dict(zip(WEIGHTS, (m_norm_mix, m_norm_ffn, m_norm_final, m_ab_w_in, m_gdn_conv_w, m_gdn_a_log, m_gdn_dt_bias, m_gdn_norm, m_hgrn_lower_bounds, m_hgrn_norm, m_ab_w_out, m_c_w_in, m_c_conv_w, m_c_conv_b, m_c_gate_a_w, m_c_gate_a_b, m_c_gate_x_w, m_c_gate_x_b, m_c_lambda, m_c_w_out, m_ffn_w_up, m_ffn_conv_w, m_ffn_conv_b, m_ffn_w_down)))
    v = dict(zip(WEIGHTS, (v_norm_mix, v_norm_ffn, v_norm_final, v_ab_w_in, v_gdn_conv_w, v_gdn_a_log, v_gdn_dt_bias, v_gdn_norm, v_hgrn_lower_bounds, v_hgrn_norm, v_ab_w_out, v_c_w_in, v_c_conv_w, v_c_conv_b, v_c_gate_a_w, v_c_gate_a_b, v_c_gate_x_w, v_c_gate_x_b, v_c_lambda, v_c_w_out, v_ffn_w_up, v_ffn_conv_w, v_ffn_conv_b, v_ffn_w_down)))
    big = list(BIG)
    chip = 2 * lax.axis_index("x") + lax.axis_index("y")
    ids = jnp.stack([chip, lax.axis_index("c")]).astype(jnp.int32)

    def layer_parts(l):
        j = l // 2
        if l % 2 == 0:
            mix = [("ab_w_in", j), ("ab_w_out", j)]
        else:
            mix = [("c_w_in", j), ("c_gate_a_w", j), ("c_gate_x_w", j), ("c_w_out", j)]
        return mix, [("ffn_w_up", l), ("ffn_w_down", l)]

    def layer_shard(n, i):
        s = w[n][i]
        return s.reshape(-1, s.shape[-1]).astype(bf16)

    small_shard_shapes = [w[n].shape for n in SMALL_SHARDED]
    small_pack = _pack_rows([w[n] for n in SMALL_SHARDED], 160)
    mix0, ffn0 = layer_parts(0)
    gathered0 = _all_gather_chips([layer_shard(n, i) for n, i in mix0] + [small_pack], "all_gather_mixer0")
    gathers = {}

    def start_gather(l, after):
        parts = ffn0 if l == 0 else sum(layer_parts(l), [])
        shards = [layer_shard(n, i) for n, i in parts]
        gathers[l] = _split_start(_gather_copies, shards, [_sds((N_CHIPS,) + s.shape, bf16) for s in shards],
                                  "gather_start_%d" % l, after)
        return gathers[l]["token"]

    first_tokens = (start_gather(0, ()), start_gather(1, ()))
    fw = {}
    per_chip = [_unpack_rows(gathered0[-1][k], small_shard_shapes) for k in range(N_CHIPS)]
    for i, n in enumerate(SMALL_SHARDED):
        fw[n] = jnp.concatenate([per_chip[k][i] for k in range(N_CHIPS)], axis=-1)
    for n in SMALL:
        if n not in fw:
            fw[n] = w[n]
    wt = _layout_weights(fw)
    for n in big:
        wt[n] = [None] * FULL_SHAPES[n][0]

    def pre_layer(l, part, x_l):
        deps = ()
        if l == 0 and part == "mix":
            parts, lands, deps = mix0, gathered0[:len(mix0)], first_tokens
        elif (l == 0) == (part == "ffn"):
            parts = ffn0 if l == 0 else sum(layer_parts(l), [])
            _, lands = _split_wait(_gather_copies, gathers[l], [x_l], "gather_wait_%d" % l)
            if l + 2 <= 3:
                deps = (start_gather(l + 2, lands[:1]),)
        else:
            return deps
        for (n, i), slots in zip(parts, lands):
            wt[n][i] = _layer_full(n, slots)
        return deps

    scatters = []

    def post_grads(l, part, g):
        parts = layer_parts(l)[0 if part == "mix" else 1]
        slots = [_layer_slots(n, g[n][i]) for n, i in parts]
        st = _split_start(_scatter_copies, slots, [_sds((N_REL, s.shape[1] // 2, s.shape[2]), bf16) for s in slots],
                          "scatter_start_%d_%s" % (l, part))
        scatters.append((parts, st, "scatter_wait_%d_%s" % (l, part)))
        return (st["token"],)

    loss, dx, g = _local_step(x[0], loss_target[0], wt, pre_layer, post_grads)
    gf = _unlayout_grads(g)
    loss = lax.psum(loss[0, 0], ("x", "y", "c"))

    out_g, out_d, out_m, out_v = {}, {}, {}, {}
    f = {n: None for n in big}
    last = [n for n, _ in layer_parts(0)[0]]

    def finish(group, after):
        for parts, st, wait_name in group:
            gbs, lands = _split_wait(_scatter_copies, st, after, wait_name)
            for (n, i), gb, land in zip(parts, gbs, lands):
                f[n] = _sum_pieces(gb, land, ids, f[n], i, FULL_SHAPES[n][0], "rs_sum")

    def adamw(names, tag):
        filled = _sibling_fill_blocks([f[n] for n in names], [FULL_SHAPES[n][0] for n in names], "rs_sibling_fill_" + tag)
        for n, g_n in zip(names, filled):
            shp = w[n].shape
            if n in NATIVE_PERM:
                to_native = lambda t: t.transpose(NATIVE_PERM[n])
                res = _adamw_lead(to_native(w[n]), to_native(g_n.reshape(shp)), to_native(m[n]), to_native(v[n]), "adamw_" + n)
                res = [t.transpose(1, 2, 0) for t in res]
            else:
                as3d = lambda t: t.reshape((-1,) + shp[-2:])
                res = _adamw_big(as3d(w[n]), g_n, as3d(m[n]), as3d(v[n]), "adamw_" + n)
            out_g[n], out_d[n], out_m[n], out_v[n] = (t.reshape(shp) for t in res)

    finish(scatters[:-1], [dx])
    adamw([n for n in big if n not in last], "a")
    finish(scatters[-1:], [out_v["ffn_w_up"]])
    adamw(last, "b")

    small_full_shapes = [FULL_SHAPES[n] for n in SMALL]
    small_sum = _all_reduce_small(_pack_rows([gf[n] for n in SMALL], 688), "all_reduce_small", [f[last[0]]])
    g_small = dict(zip(SMALL, _unpack_rows(small_sum, small_full_shapes)))
    for n in SMALL_SHARDED:
        width = w[n].shape[-1]
        g_small[n] = lax.dynamic_slice_in_dim(g_small[n], chip * width, width, axis=-1)

    def small2d(n, t):
        if n in NATIVE_PERM:
            t = t.transpose(NATIVE_PERM[n])
        return t.reshape(-1, t.shape[-1])

    sg, sd, sm, sv = _adamw_small([small2d(n, w[n]) for n in SMALL], [small2d(n, g_small[n]) for n in SMALL],
                                  [small2d(n, m[n]) for n in SMALL], [small2d(n, v[n]) for n in SMALL], "adamw_small")
    for i, n in enumerate(SMALL):
        for out, t in zip((out_g, out_d, out_m, out_v), (sg, sd, sm, sv)):
            if n in NATIVE_PERM:
                perm = NATIVE_PERM[n]
                shp_t = tuple(w[n].shape[p] for p in perm)
                out[n] = t[i].reshape(shp_t).transpose(tuple(perm.index(k) for k in range(len(perm))))
            else:
                out[n] = t[i].reshape(w[n].shape)
    return (loss, dx[None], *[out_g[n] for n in WEIGHTS], *[out_d[n] for n in WEIGHTS],
            *[out_m[n] for n in WEIGHTS], *[out_v[n] for n in WEIGHTS])
```

```python
import functools

import jax
import jax.numpy as jnp
from jax import lax
from jax.experimental import pallas as pl
from jax.experimental.pallas import tpu as pltpu

f32 = jnp.float32
bf16 = jnp.bfloat16
HI = lax.Precision.HIGHEST

D = 1024
EPS = 1e-6
F_FLOOR = 1e-30
GDN_CHUNK = 64
GDN_INTRA_CHUNKS = 1
HGRN_CHUNK = 16
HGRN_STEP = 128
HEAD = 128
NH = 4
LRU_BLOCK = 256
D_FF = 2816
RG_C = 8.0
C_QKV, C_Z, C_QB, C_FB, C_IB, C_GB, C_BA = 0, 1536, 2048, 2560, 3072, 3584, 4096
AB_COLS_PAD = 4224
TT = 256
N_SLOTS = 4
VMEM_LIMIT = 56 * 1024 * 1024

ADAM_LR, ADAM_B1, ADAM_B2, ADAM_EPS, ADAM_WD, ADAM_STEP = 0.001, 0.9, 0.999, 1e-08, 0.01, 10


def _cp(**kw):
    return pltpu.CompilerParams(vmem_limit_bytes=VMEM_LIMIT, **kw)


def _sds(shape, dtype):
    return jax.ShapeDtypeStruct(shape, dtype)


def _dot(a, b, dims, precision=None):
    return lax.dot_general(a, b, (dims, ((), ())), precision=precision, preferred_element_type=f32)


NN = ((1,), (0,))
NT = ((1,), (1,))
TN = ((0,), (0,))


def _rms(x, g):
    return x * lax.rsqrt(jnp.mean(x * x, axis=-1, keepdims=True) + EPS) * g


def _silu(x):
    return x * jax.nn.sigmoid(x)


def _mm(a, b, mode, tm, tn, out_dtype, name):
    if mode == "nn":
        (m, k), n = a.shape, b.shape[1]
        a_spec = pl.BlockSpec((tm, k), lambda i, j: (i, 0))
        b_spec = pl.BlockSpec((k, tn), lambda i, j: (0, j))
        dims = NN
    elif mode == "nt":
        (m, k), n = a.shape, b.shape[0]
        a_spec = pl.BlockSpec((tm, k), lambda i, j: (i, 0))
        b_spec = pl.BlockSpec((tn, k), lambda i, j: (j, 0))
        dims = NT
    else:
        (k, m), n = a.shape, b.shape[1]
        a_spec = pl.BlockSpec((k, tm), lambda i, j: (0, i))
        b_spec = pl.BlockSpec((k, tn), lambda i, j: (0, j))
        dims = TN
    assert m % tm == 0 and n % tn == 0, (name, m, n, tm, tn)

    def body(a_ref, b_ref, o_ref):
        o_ref[...] = _dot(a_ref[...], b_ref[...], dims).astype(out_dtype)

    return pl.pallas_call(
        body, grid=(m // tm, n // tn), in_specs=[a_spec, b_spec],
        out_specs=pl.BlockSpec((tm, tn), lambda i, j: (i, j)),
        out_shape=_sds((m, n), out_dtype), compiler_params=_cp(), name=name)(a, b)


def _mm_res_norm(a, b, res, gain, name, tm=512):
    (m, k), n = a.shape, b.shape[1]
    tm = min(tm, m)
    assert n == D and m % tm == 0, (name, m, n)

    def body(a_ref, b_ref, r_ref, *rest):
        xv = r_ref[...] + _dot(a_ref[...], b_ref[...], NN)
        if gain is None:
            rest[0][...] = xv
        else:
            g_ref, x_ref, h_ref = rest
            x_ref[...] = xv
            h_ref[...] = _rms(xv, g_ref[...]).astype(bf16)

    row = pl.BlockSpec((tm, D), lambda i: (i, 0))
    in_specs = [pl.BlockSpec((tm, k), lambda i: (i, 0)), pl.BlockSpec((k, D), lambda i: (0, 0)), row]
    args = [a, b, res]
    if gain is None:
        out_specs, out_shape = row, _sds((m, D), f32)
    else:
        in_specs.append(pl.BlockSpec((1, D), lambda i: (0, 0)))
        args.append(gain)
        out_specs, out_shape = [row, row], [_sds((m, D), f32), _sds((m, D), bf16)]
    return pl.pallas_call(body, grid=(m // tm,), in_specs=in_specs, out_specs=out_specs, out_shape=out_shape,
                          compiler_params=_cp(), name=name)(*args)


def _tile(n, cands):
    for c in cands:
        if n % c == 0:
            return c
    raise ValueError(n)


def _mm_auto(a, b, mode, out_dtype, name):
    m = a.shape[1] if mode == "tn" else a.shape[0]
    n = b.shape[0] if mode == "nt" else b.shape[1]
    return _mm(a, b, mode, _tile(m, (512, 256, 128)), _tile(n, (1024, 1408, 512, 384, 256, 128)), out_dtype, name)


def _mm_nn_slots(a, bs, out_dtype, name, after=()):
    (m, k), w = a.shape, bs.shape[2]
    tm = _tile(m, (1024, 512, 256, 128))

    def body(a_ref, b_ref, *rest):
        o_ref = rest[len(after)]
        o_ref[...] = _dot(a_ref[...], b_ref[...], NN).astype(out_dtype)

    return pl.pallas_call(
        body, grid=(m // tm, N_SLOTS),
        in_specs=[pl.BlockSpec((tm, k), lambda i, j: (i, 0)), pl.BlockSpec((None, k, w), lambda i, j: (j, 0, 0))]
        + [pl.BlockSpec(memory_space=pl.ANY)] * len(after),
        out_specs=pl.BlockSpec((tm, w), lambda i, j: (i, j)),
        out_shape=_sds((m, N_SLOTS * w), out_dtype), compiler_params=_cp(), name=name)(a, bs, *after)


def _mm_nt_slots(a, bs, name):
    (m, _), (_, r, w) = a.shape, bs.shape
    tm, tn = _tile(m, (512, 256, 128)), _tile(r, (1024, 512, 256, 128))

    def body(a_ref, b_ref, o_ref):
        acc = _dot(a_ref[:, 0:w], b_ref[0], NT)
        for s in range(1, N_SLOTS):
            acc = acc + _dot(a_ref[:, s * w:(s + 1) * w], b_ref[s], NT)
        o_ref[...] = acc

    return pl.pallas_call(
        body, grid=(m // tm, r // tn),
        in_specs=[pl.BlockSpec((tm, N_SLOTS * w), lambda i, j: (i, 0)), pl.BlockSpec((N_SLOTS, tn, w), lambda i, j: (0, j, 0))],
        out_specs=pl.BlockSpec((tm, tn), lambda i, j: (i, j)),
        out_shape=_sds((m, r), f32), compiler_params=_cp(), name=name)(a, bs)


def _mm_nt_rmsbwd(a, b, x, gain, dres, name, after=()):
    m, k = a.shape
    slots = b.ndim == 3
    assert b.shape[-2] == D
    tm = _tile(m, (512, 256, 128)) if k < 4096 else _tile(m, (256, 128))

    def body(a_ref, b_ref, x_ref, g_ref, r_ref, *rest):
        dx_ref, dxb_ref, dg_ref = rest[len(after):]
        if slots:
            w = b.shape[2]
            dh = _dot(a_ref[:, 0:w], b_ref[0], NT)
            for s in range(1, N_SLOTS):
                dh = dh + _dot(a_ref[:, s * w:(s + 1) * w], b_ref[s], NT)
        else:
            dh = _dot(a_ref[...], b_ref[...], NT)
        _, vjp = jax.vjp(_rms, x_ref[...], g_ref[...])
        dx, dg = vjp(dh)
        dx = dx + r_ref[...]
        dx_ref[...] = dx
        dxb_ref[...] = dx.astype(bf16)

        @pl.when(pl.program_id(0) == 0)
        def _():
            dg_ref[...] = jnp.zeros_like(dg_ref)

        dg_ref[...] += dg

    row = pl.BlockSpec((tm, D), lambda i: (i, 0))
    vec = pl.BlockSpec((1, D), lambda i: (0, 0))
    b_spec = pl.BlockSpec(b.shape, (lambda i: (0, 0, 0)) if slots else (lambda i: (0, 0)))
    return pl.pallas_call(
        body, grid=(m // tm,),
        in_specs=[pl.BlockSpec((tm, k), lambda i: (i, 0)), b_spec, row, vec, row] + [pl.BlockSpec(memory_space=pl.ANY)] * len(after),
        out_specs=[row, row, vec],
        out_shape=[_sds((m, D), f32), _sds((m, D), bf16), _sds((1, D), f32)],
        compiler_params=_cp(), name=name)(a, b, x, gain, dres, *after)


def _mm_tn_slots(a, b, name):
    (k, m), w = a.shape, b.shape[1] // N_SLOTS
    tm = _tile(m, (512, 256, 128))

    def body(a_ref, b_ref, o_ref):
        o_ref[...] = _dot(a_ref[...], b_ref[...], TN).astype(bf16)

    return pl.pallas_call(
        body, grid=(N_SLOTS, m // tm),
        in_specs=[pl.BlockSpec((k, tm), lambda j, i: (0, i)), pl.BlockSpec((k, w), lambda j, i: (0, j))],
        out_specs=pl.BlockSpec((None, tm, w), lambda j, i: (j, i, 0)),
        out_shape=_sds((N_SLOTS, m, w), bf16), compiler_params=_cp(), name=name)(a, b)


def _rmsnorm_fwd(x, gain, name, after=()):
    t = x.shape[0]

    def body(x_ref, g_ref, *rest):
        h_ref = rest[len(after)]
        h_ref[...] = _rms(x_ref[...], g_ref[...]).astype(bf16)

    return pl.pallas_call(
        body, grid=(t // TT,),
        in_specs=[pl.BlockSpec((TT, D), lambda i: (i, 0)), pl.BlockSpec((1, D), lambda i: (0, 0))]
        + [pl.BlockSpec(memory_space=pl.ANY)] * len(after),
        out_specs=pl.BlockSpec((TT, D), lambda i: (i, 0)),
        out_shape=_sds((t, D), bf16), compiler_params=_cp(), name=name)(x, gain, *after)


def _rmsnorm_bwd(x, gain, dh, dres, name, after=()):
    t = x.shape[0]

    def body(x_ref, g_ref, dh_ref, dres_ref, *rest):
        dx_ref, dxb_ref, dg_ref = rest[len(after):]
        _, vjp = jax.vjp(_rms, x_ref[...], g_ref[...])
        dx, dg = vjp(dh_ref[...])
        dx = dx + dres_ref[...]
        dx_ref[...] = dx
        dxb_ref[...] = dx.astype(bf16)

        @pl.when(pl.program_id(0) == 0)
        def _():
            dg_ref[...] = jnp.zeros_like(dg_ref)

        dg_ref[...] += dg

    row = pl.BlockSpec((TT, D), lambda i: (i, 0))
    vec = pl.BlockSpec((1, D), lambda i: (0, 0))
    return pl.pallas_call(
        body, grid=(t // TT,), in_specs=[row, vec, row, row] + [pl.BlockSpec(memory_space=pl.ANY)] * len(after),
        out_specs=[row, row, vec],
        out_shape=[_sds((t, D), f32), _sds((t, D), bf16), _sds((1, D), f32)],
        compiler_params=_cp(), name=name)(x, gain, dh, dres, *after)


def _residual_add(x, y, name):
    t = x.shape[0]

    def body(x_ref, y_ref, o_ref):
        o_ref[...] = x_ref[...] + y_ref[...]

    row = pl.BlockSpec((TT, D), lambda i: (i, 0))
    return pl.pallas_call(body, grid=(t // TT,), in_specs=[row, row], out_specs=row,
                          out_shape=_sds((t, D), f32), compiler_params=_cp(), name=name)(x, y)


def _final_loss(x, gain, target, name):
    t = x.shape[0]

    def loss_fn(xv, gv, tv):
        e = _rms(xv, gv) - tv
        return 0.5 * jnp.sum(jnp.mean(e * e, axis=-1))

    def body(x_ref, g_ref, t_ref, loss_ref, dx_ref, dxb_ref, dg_ref):
        val, (dx, dg) = jax.value_and_grad(loss_fn, argnums=(0, 1))(x_ref[...], g_ref[...], t_ref[...])

        @pl.when(pl.program_id(0) == 0)
        def _():
            dg_ref[...] = jnp.zeros_like(dg_ref)
            loss_ref[...] = jnp.zeros_like(loss_ref)

        dx_ref[...] = dx
        dxb_ref[...] = dx.astype(bf16)
        dg_ref[...] += dg
        loss_ref[...] += jnp.full((1, 128), val, f32)

    row = pl.BlockSpec((TT, D), lambda i: (i, 0))
    vec = pl.BlockSpec((1, D), lambda i: (0, 0))
    return pl.pallas_call(
        body, grid=(t // TT,), in_specs=[row, vec, row],
        out_specs=[pl.BlockSpec((1, 128), lambda i: (0, 0)), row, row, vec],
        out_shape=[_sds((1, 128), f32), _sds((t, D), f32), _sds((t, D), bf16), _sds((1, D), f32)],
        compiler_params=_cp(), name=name)(x, gain, target)


def _halo_rows(dtype):
    return 16 if dtype == bf16 else 8


def _conv_fwd(x, col0, c, w, b, name, tc=256, val=None, val_col0=0):
    t = x.shape[0]
    width = w.shape[0]
    nt = t // TT
    hr = _halo_rows(x.dtype)
    hb = TT // hr

    def body(*refs):
        if val is None:
            x_ref, xh_ref, w_ref, b_ref, o_ref, xp = refs
        else:
            x_ref, xh_ref, w_ref, b_ref, v_ref, o_ref, act_ref, xp = refs
        i = pl.program_id(1)
        xp[0:8, :] = jnp.where(i == 0, 0.0, xh_ref[hr - 8:hr, :].astype(f32))
        xp[8:, :] = x_ref[...].astype(f32)
        acc = jnp.zeros((TT, tc), f32) + b_ref[...]
        for k in range(width):
            acc = acc + w_ref[k:k + 1, :] * xp[pl.ds(8 - (width - 1) + k, TT), :]
        o_ref[...] = acc
        if val is not None:
            act_ref[...] = (_silu(acc) * v_ref[...]).astype(bf16)

    in_specs = [
        pl.BlockSpec((TT, tc), lambda j, i: (i, j + col0)),
        pl.BlockSpec((hr, tc), lambda j, i: (jnp.maximum(i * hb - 1, 0), j + col0)),
        pl.BlockSpec((width, tc), lambda j, i: (0, j)),
        pl.BlockSpec((1, tc), lambda j, i: (0, j)),
    ]
    args = [x, x, w, b]
    out_specs = [pl.BlockSpec((TT, tc), lambda j, i: (i, j))]
    out_shape = [_sds((t, c), f32)]
    if val is not None:
        in_specs.append(pl.BlockSpec((TT, tc), lambda j, i: (i, j + val_col0)))
        args.append(val)
        out_specs.append(pl.BlockSpec((TT, tc), lambda j, i: (i, j)))
        out_shape.append(_sds((t, c), bf16))
    res = pl.pallas_call(
        body, grid=(c // tc, nt), in_specs=in_specs, out_specs=out_specs, out_shape=out_shape,
        scratch_shapes=[pltpu.VMEM((TT + 8, tc), f32)], compiler_params=_cp(), name=name)(*args)
    return res[0] if val is None else res


def _conv_bwd(dc, x, col0, w, name, tc=256, dx_dtype=bf16, into=None):
    t, c = dc.shape
    width = w.shape[0]
    nt = t // TT
    hb = TT // 8
    hr = _halo_rows(x.dtype)

    def body(dc_ref, dcn_ref, x_ref, xh_ref, w_ref, *rest):
        dx_ref, dw_ref, db_ref, dcp, xp = rest[(0 if into is None else 1):]
        i = pl.program_id(1)
        dcv = dc_ref[...]
        dcp[0:TT, :] = dcv
        dcp[TT:, :] = jnp.where(i == nt - 1, 0.0, dcn_ref[...])
        xp[0:8, :] = jnp.where(i == 0, 0.0, xh_ref[hr - 8:hr, :].astype(f32))
        xp[8:, :] = x_ref[...].astype(f32)

        @pl.when(i == 0)
        def _():
            dw_ref[...] = jnp.zeros_like(dw_ref)
            db_ref[...] = jnp.zeros_like(db_ref)

        acc = jnp.zeros((TT, tc), f32)
        for k in range(width):
            acc = acc + w_ref[k:k + 1, :] * dcp[pl.ds((width - 1) - k, TT), :]
            dw_ref[k:k + 1, :] += jnp.sum(dcv * xp[pl.ds(8 - (width - 1) + k, TT), :], axis=0, keepdims=True)
        dx_ref[...] = acc.astype(dx_dtype)
        db_ref[...] += jnp.sum(dcv, axis=0, keepdims=True)

    in_specs = [
        pl.BlockSpec((TT, tc), lambda j, i: (i, j)),
        pl.BlockSpec((8, tc), lambda j, i: (jnp.minimum((i + 1) * hb, t // 8 - 1), j)),
        pl.BlockSpec((TT, tc), lambda j, i: (i, j + col0)),
        pl.BlockSpec((hr, tc), lambda j, i: (jnp.maximum(i * (TT // hr) - 1, 0), j + col0)),
        pl.BlockSpec((width, tc), lambda j, i: (0, j)),
    ]
    out_specs = [
        pl.BlockSpec((TT, tc), lambda j, i: (i, j)),
        pl.BlockSpec((8, tc), lambda j, i: (0, j)),
        pl.BlockSpec((1, tc), lambda j, i: (0, j)),
    ]
    args, aliases, dx_shape = [dc, dc, x, x, w], {}, _sds((t, c), dx_dtype)
    if into is not None:
        in_specs.append(pl.BlockSpec(memory_space=pl.ANY))
        args.append(into)
        aliases, dx_shape = {5: 0}, _sds(into.shape, into.dtype)
    return pl.pallas_call(
        body, grid=(c // tc, nt), in_specs=in_specs, out_specs=out_specs,
        out_shape=[dx_shape, _sds((8, c), f32), _sds((1, c), f32)], input_output_aliases=aliases,
        scratch_shapes=[pltpu.VMEM((TT + 8, tc), f32), pltpu.VMEM((TT + 8, tc), f32)],
        compiler_params=_cp(), name=name)(*args)


def _bdot_impl(a, b, dims):
    return _dot(a.astype(bf16), b.astype(bf16), dims)


@functools.partial(jax.custom_vjp, nondiff_argnums=(2,))
def _bdot(a, b, dims):
    return _bdot_impl(a, b, dims)


def _bdot_fwd(a, b, dims):
    return _bdot_impl(a, b, dims), (a, b)


def _bdot_bwd(dims, res, ct):
    a, b = res
    if dims == NN:
        return _bdot_impl(ct, b, NT), _bdot_impl(a, ct, TN)
    if dims == NT:
        return _bdot_impl(ct, b, NN), _bdot_impl(ct, a, TN)
    return _bdot_impl(b, ct, NT), _bdot_impl(a, ct, NN)


_bdot.defvjp(_bdot_fwd, _bdot_bwd)


def _split2(a):
    hi = a.astype(bf16)
    return hi, (a - hi.astype(f32)).astype(bf16)


def _dot3_impl(a, b, dims):
    a_hi, a_lo = _split2(a)
    b_hi, b_lo = _split2(b)
    return (_dot(a_hi, b_hi, dims) + _dot(a_hi, b_lo, dims)) + _dot(a_lo, b_hi, dims)


@functools.partial(jax.custom_vjp, nondiff_argnums=(2,))
def _dot3(a, b, dims):
    return _dot3_impl(a, b, dims)


def _dot3_fwd(a, b, dims):
    return _dot3_impl(a, b, dims), (a, b)


def _dot3_bwd(dims, res, ct):
    a, b = res
    if dims == NN:
        return _dot3_impl(ct, b, NT), _dot3_impl(a, ct, TN)
    if dims == NT:
        return _dot3_impl(ct, b, NN), _dot3_impl(ct, a, TN)
    return _dot3_impl(b, ct, NT), _dot3_impl(a, ct, NN)


_dot3.defvjp(_dot3_fwd, _dot3_bwd)


def _tril_dot_impl(tril, x, dims):
    t = tril.astype(bf16)
    x1 = x.astype(bf16)
    r1 = x - x1.astype(f32)
    x2 = r1.astype(bf16)
    x3 = (r1 - x2.astype(f32)).astype(bf16)
    return (_dot(t, x3, dims) + _dot(t, x2, dims)) + _dot(t, x1, dims)


@jax.custom_vjp
def _cumsum_rows(tril, x):
    return _tril_dot_impl(tril, x, NN)


def _cumsum_rows_fwd(tril, x):
    return _tril_dot_impl(tril, x, NN), tril


def _cumsum_rows_bwd(tril, ct):
    return jnp.zeros_like(tril), _tril_dot_impl(tril, ct, TN)


_cumsum_rows.defvjp(_cumsum_rows_fwd, _cumsum_rows_bwd)


def _gdn_intra(cq, ck, cv, ba, alog_v, dtb_v, hd):
    c = GDN_CHUNK
    lane = lax.broadcasted_iota(jnp.int32, (1, HEAD), 1)
    mb = (lane == hd).astype(f32)
    ma = (lane == hd + NH).astype(f32)
    beta = jax.nn.sigmoid(jnp.sum(ba * mb, axis=1, keepdims=True))
    alpha = jnp.sum(ba * ma, axis=1, keepdims=True)
    alog = jnp.sum(alog_v * ma, axis=1, keepdims=True)
    dtb = jnp.sum(dtb_v * ma, axis=1, keepdims=True)
    g = -jnp.exp(alog) * jax.nn.softplus(alpha + dtb)
    q = _silu(cq)
    q = q * lax.rsqrt(jnp.sum(q * q, axis=-1, keepdims=True) + EPS) * (HEAD ** -0.5)
    k = _silu(ck)
    k = k * lax.rsqrt(jnp.sum(k * k, axis=-1, keepdims=True) + EPS)
    v = _silu(cv)
    row = lax.broadcasted_iota(jnp.int32, (c, c), 0)
    col = lax.broadcasted_iota(jnp.int32, (c, c), 1)
    causal = row >= col
    tril = causal.astype(f32)
    gc = _cumsum_rows(tril, jnp.broadcast_to(g, (c, HEAD)))
    gcc = _cumsum_rows(tril, jnp.broadcast_to(g, (c, c)))
    diff = jnp.where(causal, gcc - gcc.T, 0.0)
    decay = jnp.where(causal, jnp.exp(diff), 0.0)
    kb = k * beta
    nmat = -jnp.where(row > col, _bdot(kb, k, NT) * decay, 0.0)
    egc = jnp.exp(gc)
    sol = jnp.concatenate([v * beta, kb * egc], axis=1)
    p = nmat
    for it in range(6):
        sol = sol + _dot3(p, sol, NN)
        if it < 5:
            p = _dot3(p, p, NN)
    u, w = sol[:, :HEAD], sol[:, HEAD:]
    attn = _bdot(q, k, NT) * decay
    rowv = lax.broadcasted_iota(jnp.int32, (c, 1), 0)
    gc_last = jnp.sum(jnp.where(rowv == c - 1, gc, 0.0), axis=0, keepdims=True)
    return u, w, q * egc, k * jnp.exp(gc_last - gc), attn, jnp.exp(gc_last)


def _gdn_intra4(cq, ck, cv, ba, alog_v, dtb_v):
    c = GDN_CHUNK
    hs = range(NH)
    lane = lax.broadcasted_iota(jnp.int32, (1, HEAD), 1)
    mb = [(lane == h).astype(f32) for h in hs]
    ma = [(lane == h + NH).astype(f32) for h in hs]
    beta = [jax.nn.sigmoid(jnp.sum(ba * mb[h], axis=1, keepdims=True)) for h in hs]
    alpha = [jnp.sum(ba * ma[h], axis=1, keepdims=True) for h in hs]
    alog = [jnp.sum(alog_v * ma[h], axis=1, keepdims=True) for h in hs]
    dtb = [jnp.sum(dtb_v * ma[h], axis=1, keepdims=True) for h in hs]
    g = [-jnp.exp(alog[h]) * jax.nn.softplus(alpha[h] + dtb[h]) for h in hs]
    q = [_silu(cq[h]) for h in hs]
    q = [q[h] * lax.rsqrt(jnp.sum(q[h] * q[h], axis=-1, keepdims=True) + EPS) * (HEAD ** -0.5) for h in hs]
    k = [_silu(ck[h]) for h in hs]
    k = [k[h] * lax.rsqrt(jnp.sum(k[h] * k[h], axis=-1, keepdims=True) + EPS) for h in hs]
    v = [_silu(cv[h]) for h in hs]
    row = lax.broadcasted_iota(jnp.int32, (c, c), 0)
    col = lax.broadcasted_iota(jnp.int32, (c, c), 1)
    causal = row >= col
    tril = causal.astype(f32)
    gc = [_cumsum_rows(tril, jnp.broadcast_to(g[h], (c, HEAD))) for h in hs]
    gcc = [_cumsum_rows(tril, jnp.broadcast_to(g[h], (c, c))) for h in hs]
    decay = [jnp.where(causal, jnp.exp(jnp.where(causal, gcc[h] - gcc[h].T, 0.0)), 0.0) for h in hs]
    kb = [k[h] * beta[h] for h in hs]
    kk = [_bdot(kb[h], k[h], NT) for h in hs]
    p = [-jnp.where(row > col, kk[h] * decay[h], 0.0) for h in hs]
    egc = [jnp.exp(gc[h]) for h in hs]
    sol = [jnp.concatenate([v[h] * beta[h], kb[h] * egc[h]], axis=1) for h in hs]
    for it in range(6):
        upd = [_dot3(p[h], sol[h], NN) for h in hs]
        sol = [sol[h] + upd[h] for h in hs]
        if it < 5:
            p = [_dot3(p[h], p[h], NN) for h in hs]
    qk = [_bdot(q[h], k[h], NT) for h in hs]
    attn = [qk[h] * decay[h] for h in hs]
    rowv = lax.broadcasted_iota(jnp.int32, (c, 1), 0)
    gc_last = [jnp.sum(jnp.where(rowv == c - 1, gc[h], 0.0), axis=0, keepdims=True) for h in hs]
    return ([sol[h][:, :HEAD] for h in hs], [sol[h][:, HEAD:] for h in hs], [q[h] * egc[h] for h in hs],
            [k[h] * jnp.exp(gc_last[h] - gc[h]) for h in hs], attn, [jnp.exp(gc_last[h]) for h in hs])


def _gdn_seq(u, w, q_dec, k_dec, attn, dl, z, s, gain):
    v_new = u - _bdot(w, s, NN)
    o = _bdot(q_dec, s, NN) + _bdot(attn, v_new, NN)
    s_new = s * dl + _bdot(k_dec, v_new, TN)
    return _rms(o, gain) * _silu(z), s_new


def _gdn_seq4(u, w, q_dec, k_dec, attn, dl, z, s, gain):
    hs = range(NH)
    ws = [_bdot(w[h], s[h], NN) for h in hs]
    qs = [_bdot(q_dec[h], s[h], NN) for h in hs]
    v_new = [u[h] - ws[h] for h in hs]
    av = [_bdot(attn[h], v_new[h], NN) for h in hs]
    kv = [_bdot(k_dec[h], v_new[h], TN) for h in hs]
    o = [_rms(qs[h] + av[h], gain) * _silu(z[h]) for h in hs]
    return o, [s[h] * dl[h] + kv[h] for h in hs]


def _hsl(h):
    return slice(h * HEAD, (h + 1) * HEAD)


def _gdn2_fwd(conv, p, alog_v, dtb_v, gain, name):
    t = conv.shape[0]
    c = GDN_CHUNK
    nch = t // c
    w512 = NH * HEAD
    wide = lambda off: pl.BlockSpec((c, w512), lambda n: (n, off))
    vec = pl.BlockSpec((1, HEAD), lambda n: (0, 0))
    attn_spec = pl.BlockSpec((1, NH, c, c), lambda n: (n, 0, 0, 0))
    dl_spec = pl.BlockSpec((1, NH, HEAD), lambda n: (n, 0, 0))

    cps = GDN_INTRA_CHUNKS
    iwide = lambda off: pl.BlockSpec((cps * c, w512), lambda n: (n, off))

    def intra(cq, ck, cv, ba, al, dt, u_ref, w_ref, qd_ref, kd_ref, at_ref, dl_ref):
        for ci in range(cps):
            rows = slice(ci * c, (ci + 1) * c)
            u, w, qd, kd, at, dl = _gdn_intra4([cq[rows, _hsl(h)] for h in range(NH)], [ck[rows, _hsl(h)] for h in range(NH)],
                                               [cv[rows, _hsl(h)] for h in range(NH)], ba[rows, :], al[...], dt[...])
            for h in range(NH):
                u_ref[rows, _hsl(h)] = u[h]
                w_ref[rows, _hsl(h)] = w[h]
                qd_ref[rows, _hsl(h)] = qd[h]
                kd_ref[rows, _hsl(h)] = kd[h]
                at_ref[ci, h] = at[h]
                dl_ref[ci, h:h + 1, :] = dl[h]

    u, w, qd, kd, at, dl = pl.pallas_call(
        intra, grid=(nch // cps,),
        in_specs=[iwide(0), iwide(1), iwide(2), pl.BlockSpec((cps * c, HEAD), lambda n: (n, C_BA // HEAD)), vec, vec],
        out_specs=[iwide(0)] * 4 + [pl.BlockSpec((cps, NH, c, c), lambda n: (n, 0, 0, 0)),
                                    pl.BlockSpec((cps, NH, HEAD), lambda n: (n, 0, 0))],
        out_shape=[_sds((t, w512), f32)] * 4 + [_sds((nch, NH, c, c), f32), _sds((nch, NH, HEAD), f32)],
        compiler_params=_cp(), name=name + "_intra")(conv, conv, conv, p, alog_v, dtb_v)

    def seq(u_ref, w_ref, qd_ref, kd_ref, at_ref, dl_ref, z_ref, gn, o_ref, ss_ref, s_scr):
        @pl.when(pl.program_id(0) == 0)
        def _():
            s_scr[...] = jnp.zeros_like(s_scr)

        hs = range(NH)
        s = [s_scr[h] for h in hs]
        for h in hs:
            ss_ref[0, h] = s[h]
        o, s_new = _gdn_seq4([u_ref[:, _hsl(h)] for h in hs], [w_ref[:, _hsl(h)] for h in hs], [qd_ref[:, _hsl(h)] for h in hs],
                             [kd_ref[:, _hsl(h)] for h in hs], [at_ref[0, h] for h in hs], [dl_ref[0, h:h + 1, :] for h in hs],
                             [z_ref[:, _hsl(h)] for h in hs], s, gn[...])
        for h in hs:
            o_ref[:, _hsl(h)] = o[h].astype(bf16)
            s_scr[h] = s_new[h]

    o, states = pl.pallas_call(
        seq, grid=(nch,),
        in_specs=[wide(0)] * 4 + [attn_spec, dl_spec, wide(C_Z // w512), vec],
        out_specs=[wide(0), pl.BlockSpec((1, NH, HEAD, HEAD), lambda n: (n, 0, 0, 0))],
        out_shape=[_sds((t, w512), bf16), _sds((nch, NH, HEAD, HEAD), f32)],
        scratch_shapes=[pltpu.VMEM((NH, HEAD, HEAD), f32)],
        compiler_params=_cp(), name=name + "_seq")(u, w, qd, kd, at, dl, p, gain)
    return o, dict(u=u, w=w, qd=qd, kd=kd, at=at, dl=dl, states=states)


def _gdn2_bwd(conv, p, alog_v, dtb_v, gain, saved, do, name):
    t = conv.shape[0]
    c = GDN_CHUNK
    nch = t // c
    w512 = NH * HEAD
    rwide = lambda off: pl.BlockSpec((c, w512), lambda n: (nch - 1 - n, off))
    rvec = pl.BlockSpec((1, HEAD), lambda n: (0, 0))
    rattn = pl.BlockSpec((1, NH, c, c), lambda n: (nch - 1 - n, 0, 0, 0))
    rdl = pl.BlockSpec((1, NH, HEAD), lambda n: (nch - 1 - n, 0, 0))

    def seq_bwd(u_ref, w_ref, qd_ref, kd_ref, at_ref, dl_ref, z_ref, gn, ss_ref, do_ref,
                du_ref, dw_ref, dqd_ref, dkd_ref, dat_ref, ddl_ref, dz_ref, dgn_ref, ds_scr):
        @pl.when(pl.program_id(0) == 0)
        def _():
            ds_scr[...] = jnp.zeros_like(ds_scr)
            dgn_ref[...] = jnp.zeros_like(dgn_ref)

        hs = range(NH)
        _, vjp = jax.vjp(_gdn_seq4, [u_ref[:, _hsl(h)] for h in hs], [w_ref[:, _hsl(h)] for h in hs],
                         [qd_ref[:, _hsl(h)] for h in hs], [kd_ref[:, _hsl(h)] for h in hs], [at_ref[0, h] for h in hs],
                         [dl_ref[0, h:h + 1, :] for h in hs], [z_ref[:, _hsl(h)] for h in hs], [ss_ref[0, h] for h in hs], gn[...])
        du, dw, dqd, dkd, dat, ddl, dz, ds, dg = vjp(([do_ref[:, _hsl(h)] for h in hs], [ds_scr[h] for h in hs]))
        for h in hs:
            du_ref[:, _hsl(h)] = du[h]
            dw_ref[:, _hsl(h)] = dw[h]
            dqd_ref[:, _hsl(h)] = dqd[h]
            dkd_ref[:, _hsl(h)] = dkd[h]
            dat_ref[0, h] = dat[h]
            ddl_ref[0, h:h + 1, :] = ddl[h]
            dz_ref[:, _hsl(h)] = dz[h]
            ds_scr[h] = ds[h]
        dgn_ref[...] += dg

    du, dw, dqd, dkd, dat, ddl, dz, dgn = pl.pallas_call(
        seq_bwd, grid=(nch,),
        in_specs=[rwide(0)] * 4 + [rattn, rdl, rwide(C_Z // w512), rvec,
                                   pl.BlockSpec((1, NH, HEAD, HEAD), lambda n: (nch - 1 - n, 0, 0, 0)), rwide(0)],
        out_specs=[rwide(0)] * 4 + [rattn, rdl, rwide(0), rvec],
        out_shape=[_sds((t, w512), f32)] * 4 + [_sds((nch, NH, c, c), f32), _sds((nch, NH, HEAD), f32),
                                                _sds((t, w512), f32), _sds((1, HEAD), f32)],
        scratch_shapes=[pltpu.VMEM((NH, HEAD, HEAD), f32)],
        compiler_params=_cp(), name=name + "_seq")(
            saved["u"], saved["w"], saved["qd"], saved["kd"], saved["at"], saved["dl"], p, gain, saved["states"], do)

    cps = GDN_INTRA_CHUNKS
    wide = lambda off: pl.BlockSpec((cps * c, w512), lambda n: (n, off))
    vec = pl.BlockSpec((1, HEAD), lambda n: (0, 0))
    attn_spec = pl.BlockSpec((cps, NH, c, c), lambda n: (n, 0, 0, 0))
    dl_spec = pl.BlockSpec((cps, NH, HEAD), lambda n: (n, 0, 0))

    def intra_bwd(cq, ck, cv, ba, al, dt, du_ref, dw_ref, dqd_ref, dkd_ref, dat_ref, ddl_ref,
                  dc_ref, dba_ref, dal_ref, ddt_ref):
        @pl.when(pl.program_id(0) == 0)
        def _():
            dal_ref[...] = jnp.zeros_like(dal_ref)
            ddt_ref[...] = jnp.zeros_like(ddt_ref)

        dal = jnp.zeros((1, HEAD), f32)
        ddt = jnp.zeros((1, HEAD), f32)
        for ci in range(cps):
            rows = slice(ci * c, (ci + 1) * c)
            hs = range(NH)
            _, vjp = jax.vjp(_gdn_intra4, [cq[rows, _hsl(h)] for h in hs], [ck[rows, _hsl(h)] for h in hs],
                             [cv[rows, _hsl(h)] for h in hs], ba[rows, :], al[...], dt[...])
            g_q, g_k, g_v, g_ba, g_al, g_dt = vjp((
                [du_ref[rows, _hsl(h)] for h in hs], [dw_ref[rows, _hsl(h)] for h in hs], [dqd_ref[rows, _hsl(h)] for h in hs],
                [dkd_ref[rows, _hsl(h)] for h in hs], [dat_ref[ci, h] for h in hs], [ddl_ref[ci, h:h + 1, :] for h in hs]))
            for h in hs:
                dc_ref[rows, _hsl(h)] = g_q[h]
                dc_ref[rows, _hsl(NH + h)] = g_k[h]
                dc_ref[rows, _hsl(2 * NH + h)] = g_v[h]
            dal = dal + g_al
            ddt = ddt + g_dt
            dba_ref[rows, :] = g_ba
        dal_ref[...] += dal
        ddt_ref[...] += ddt

    dconv, dba, dal, ddt = pl.pallas_call(
        intra_bwd, grid=(nch // cps,),
        in_specs=[wide(0), wide(1), wide(2), pl.BlockSpec((cps * c, HEAD), lambda n: (n, C_BA // HEAD)), vec, vec]
        + [wide(0)] * 4 + [attn_spec, dl_spec],
        out_specs=[pl.BlockSpec((cps * c, 3 * w512), lambda n: (n, 0)), pl.BlockSpec((cps * c, HEAD), lambda n: (n, 0)),
                   vec, vec],
        out_shape=[_sds((t, 3 * w512), f32), _sds((t, HEAD), f32), _sds((1, HEAD), f32), _sds((1, HEAD), f32)],
        compiler_params=_cp(), name=name + "_intra")(conv, conv, conv, p, alog_v, dtb_v, du, dw, dqd, dkd, dat, ddl)
    return dconv, dz, dba, dal, ddt, dgn


def _hgrn_intra(qb, fb, ib, lb):
    c = HGRN_CHUNK
    ns = range(len(qb))
    f = [lb + (1.0 - lb) * jax.nn.sigmoid(fb[i]) for i in ns]
    logf = [jnp.log(jnp.maximum(f[i], F_FLOOR)) for i in ns]
    k = [1.0 - f[i] for i in ns]
    q = [_silu(qb[i]) for i in ns]
    row = lax.broadcasted_iota(jnp.int32, (c, c), 0)
    col = lax.broadcasted_iota(jnp.int32, (c, c), 1)
    tril = (row >= col).astype(f32)
    b = [_cumsum_rows(tril, logf[i]) for i in ns]
    ri = lax.broadcasted_iota(jnp.int32, (c, 1), 0)
    o = [jnp.zeros((c, HEAD), f32) for _ in ns]
    for j in range(c):
        mj = ri == j
        ok = ri >= j
        bj = [jnp.sum(jnp.where(mj, b[i], 0.0), axis=0, keepdims=True) for i in ns]
        kj = [jnp.sum(jnp.where(mj, k[i], 0.0), axis=0, keepdims=True) for i in ns]
        vj = [jnp.sum(jnp.where(mj, ib[i], 0.0), axis=0, keepdims=True) for i in ns]
        e = [jnp.where(ok, jnp.exp(jnp.where(ok, b[i] - bj[i], 0.0)), 0.0) for i in ns]
        s = [jnp.sum(q[i] * kj[i] * e[i], axis=1, keepdims=True) for i in ns]
        o = [o[i] + s[i] * vj[i] for i in ns]
    b_last = [jnp.sum(jnp.where(ri == c - 1, b[i], 0.0), axis=0, keepdims=True) for i in ns]
    return (o, [q[i] * jnp.exp(b[i]) for i in ns], [k[i] * jnp.exp(b_last[i] - b[i]) for i in ns],
            [jnp.exp(b_last[i]) for i in ns])


def _hgrn_seq(o_intra, q_dec, k_dec, dl, v, gb, st, gain):
    o = o_intra + _bdot(q_dec, st, NT)
    st_new = st * dl + _bdot(v, k_dec, TN)
    return _rms(o, gain) * _silu(gb), st_new


def _hgrn_seq_heads(o_intra, q_dec, k_dec, dl, v, gb, st, gain):
    hs = range(len(st))
    qs = [_bdot(q_dec[h], st[h], NT) for h in hs]
    vk = [_bdot(v[h], k_dec[h], TN) for h in hs]
    o = [_rms(o_intra[h] + qs[h], gain) * _silu(gb[h]) for h in hs]
    return o, [st[h] * dl[h] + vk[h] for h in hs]


def _hgrn_tiles(ref, rows):
    return [[ref[rw, _hsl(h)] for rw in rows] for h in range(NH)]


def _hgrn_fwd(p, lb, gain, name):
    t = p.shape[0]
    r = HGRN_STEP
    ns = t // r
    nsub = r // HGRN_CHUNK
    w512 = NH * HEAD
    blk = lambda off: pl.BlockSpec((r, w512), lambda n: (n, off // w512))

    def body(qb, fb, ib, gb, lb_ref, gn, o_ref, ss_ref, s_scr):
        @pl.when(pl.program_id(0) == 0)
        def _():
            s_scr[...] = jnp.zeros_like(s_scr)

        hs = range(NH)
        rows = [pl.ds(ch * HGRN_CHUNK, HGRN_CHUNK) for ch in range(nsub)]
        q_t, f_t, v_t, g_t = (_hgrn_tiles(ref, rows) for ref in (qb, fb, ib, gb))
        intra = [_hgrn_intra(q_t[h], f_t[h], v_t[h], lb_ref[:, _hsl(h)]) for h in hs]
        st = [s_scr[h] for h in hs]
        for h in hs:
            ss_ref[0, h] = st[h]
        for ch in range(nsub):
            o, st = _hgrn_seq_heads([intra[h][0][ch] for h in hs], [intra[h][1][ch] for h in hs], [intra[h][2][ch] for h in hs],
                                    [intra[h][3][ch] for h in hs], [v_t[h][ch] for h in hs], [g_t[h][ch] for h in hs], st, gn[...])
            for h in hs:
                o_ref[rows[ch], _hsl(h)] = o[h].astype(bf16)
        for h in hs:
            s_scr[h] = st[h]

    return pl.pallas_call(
        body, grid=(ns,),
        in_specs=[blk(C_QB), blk(C_FB), blk(C_IB), blk(C_GB),
                  pl.BlockSpec((1, w512), lambda n: (0, 0)), pl.BlockSpec((1, HEAD), lambda n: (0, 0))],
        out_specs=[pl.BlockSpec((r, w512), lambda n: (n, 0)),
                   pl.BlockSpec((1, NH, HEAD, HEAD), lambda n: (n, 0, 0, 0))],
        out_shape=[_sds((t, w512), bf16), _sds((ns, NH, HEAD, HEAD), f32)],
        scratch_shapes=[pltpu.VMEM((NH, HEAD, HEAD), f32)],
        compiler_params=_cp(), name=name)(p, p, p, p, lb, gain)


def _hgrn_bwd(p, lb, gain, states, do, name):
    t = p.shape[0]
    r = HGRN_STEP
    ns = t // r
    nsub = r // HGRN_CHUNK
    w512 = NH * HEAD
    blk = lambda off: pl.BlockSpec((r, w512), lambda n: (ns - 1 - n, off // w512))

    def body(qb, fb, ib, gb, lb_ref, gn, ss_ref, do_ref, dp_ref, dlb_ref, dgn_ref, ds_scr, st_scr):
        @pl.when(pl.program_id(0) == 0)
        def _():
            ds_scr[...] = jnp.zeros_like(ds_scr)
            dgn_ref[...] = jnp.zeros_like(dgn_ref)

        hs = range(NH)
        gnv = gn[...]
        rows = [pl.ds(ch * HGRN_CHUNK, HGRN_CHUNK) for ch in range(nsub)]
        q_t, f_t, v_t, g_t = (_hgrn_tiles(ref, rows) for ref in (qb, fb, ib, gb))
        intra, vjp_intra = [], []
        for h in hs:
            out, fn = jax.vjp(_hgrn_intra, q_t[h], f_t[h], v_t[h], lb_ref[:, _hsl(h)])
            intra.append(out)
            vjp_intra.append(fn)
        at = lambda part, ch: [intra[h][part][ch] for h in hs]
        st = [ss_ref[0, h] for h in hs]
        for ch in range(nsub):
            for h in hs:
                st_scr[ch * NH + h] = st[h]
            if ch < nsub - 1:
                vk = [_bdot(v_t[h][ch], intra[h][2][ch], TN) for h in hs]
                st = [st[h] * intra[h][3][ch] + vk[h] for h in hs]
        ds = [ds_scr[h] for h in hs]
        dgn = jnp.zeros((1, HEAD), f32)
        cots = [[[None] * nsub for _ in range(4)] for _ in hs]
        d_v = [[None] * nsub for _ in hs]
        for ch in reversed(range(nsub)):
            _, vjp = jax.vjp(_hgrn_seq_heads, at(0, ch), at(1, ch), at(2, ch), at(3, ch), [v_t[h][ch] for h in hs],
                             [g_t[h][ch] for h in hs], [st_scr[ch * NH + h] for h in hs], gnv)
            d_oi, d_qd, d_kd, d_dl, d_vv, g_g, ds, g_gn = vjp(([do_ref[rows[ch], _hsl(h)] for h in hs], ds))
            for h in hs:
                cots[h][0][ch], cots[h][1][ch], cots[h][2][ch], cots[h][3][ch] = d_oi[h], d_qd[h], d_kd[h], d_dl[h]
                d_v[h][ch] = d_vv[h]
                dp_ref[rows[ch], pl.ds(3 * w512 + h * HEAD, HEAD)] = g_g[h].astype(bf16)
            dgn = dgn + g_gn
        for h in hs:
            g_q, g_f, g_i, g_lb = vjp_intra[h](tuple(cots[h]))
            for ch in range(nsub):
                dp_ref[rows[ch], pl.ds(h * HEAD, HEAD)] = g_q[ch].astype(bf16)
                dp_ref[rows[ch], pl.ds(w512 + h * HEAD, HEAD)] = g_f[ch].astype(bf16)
                dp_ref[rows[ch], pl.ds(2 * w512 + h * HEAD, HEAD)] = (g_i[ch] + d_v[h][ch]).astype(bf16)
            dlb_ref[0, :, _hsl(h)] = g_lb
            ds_scr[h] = ds[h]
        dgn_ref[...] += dgn

    return pl.pallas_call(
        body, grid=(ns,),
        in_specs=[blk(C_QB), blk(C_FB), blk(C_IB), blk(C_GB),
                  pl.BlockSpec((1, w512), lambda n: (0, 0)), pl.BlockSpec((1, HEAD), lambda n: (0, 0)),
                  pl.BlockSpec((1, NH, HEAD, HEAD), lambda n: (ns - 1 - n, 0, 0, 0)),
                  pl.BlockSpec((r, w512), lambda n: (ns - 1 - n, 1))],
        out_specs=[pl.BlockSpec((r, 4 * w512), lambda n: (ns - 1 - n, 0)),
                   pl.BlockSpec((1, 1, w512), lambda n: (n, 0, 0)),
                   pl.BlockSpec((1, HEAD), lambda n: (0, 0))],
        out_shape=[_sds((t, 4 * w512), bf16), _sds((ns, 1, w512), f32), _sds((1, HEAD), f32)],
        scratch_shapes=[pltpu.VMEM((NH, HEAD, HEAD), f32), pltpu.VMEM((nsub * NH, HEAD, HEAD), f32)],
        compiler_params=_cp(), name=name)(p, p, p, p, lb, gain, states, do)


def _lru_gates(xb, wa, wx, ba, bx, lam):
    xh = xb.astype(bf16)
    r = jax.nn.sigmoid(_dot(xh, wa.astype(bf16), NN) + ba)
    i = jax.nn.sigmoid(_dot(xh, wx.astype(bf16), NN) + bx)
    log_a = -RG_C * r * jax.nn.softplus(-lam)
    a = jnp.exp(log_a)
    t2 = 2.0 * log_a
    series = -t2 * (1.0 + t2 * (0.5 + t2 * (1.0 / 6.0 + t2 * (1.0 / 24.0))))
    om = jnp.where(t2 > -1e-2, series, 1.0 - jnp.exp(t2))
    u = jnp.sqrt(jnp.maximum(om, 0.0)) * (i * xb)
    return a, u


def _lru_gates_fwd(xc, wa, wx, ba, bx, lam, name):
    t = xc.shape[0]
    blk = pl.BlockSpec((TT, LRU_BLOCK), lambda h, i: (i, h))
    wsp = pl.BlockSpec((1, LRU_BLOCK, LRU_BLOCK), lambda h, i: (h, 0, 0))
    vsp = pl.BlockSpec((1, LRU_BLOCK), lambda h, i: (0, h))

    def body(x_ref, wa_ref, wx_ref, ba_ref, bx_ref, lam_ref, a_ref, u_ref):
        a, u = _lru_gates(x_ref[...], wa_ref[0], wx_ref[0], ba_ref[...], bx_ref[...], lam_ref[...])
        a_ref[...] = a
        u_ref[...] = u

    return pl.pallas_call(
        body, grid=(NH, t // TT), in_specs=[blk, wsp, wsp, vsp, vsp, vsp], out_specs=[blk, blk],
        out_shape=[_sds((t, D), f32)] * 2, compiler_params=_cp(), name=name)(xc, wa, wx, ba, bx, lam)


def _lru_gates_bwd(xc, wa, wx, ba, bx, lam, da, du, name):
    t = xc.shape[0]
    blk = pl.BlockSpec((TT, LRU_BLOCK), lambda h, i: (i, h))
    wsp = pl.BlockSpec((1, LRU_BLOCK, LRU_BLOCK), lambda h, i: (h, 0, 0))
    vsp = pl.BlockSpec((1, LRU_BLOCK), lambda h, i: (0, h))

    def body(x_ref, wa_ref, wx_ref, ba_ref, bx_ref, lam_ref, da_ref, du_ref,
             dx_ref, dwa_ref, dwx_ref, dba_ref, dbx_ref, dlam_ref):
        @pl.when(pl.program_id(1) == 0)
        def _():
            for r in (dwa_ref, dwx_ref, dba_ref, dbx_ref, dlam_ref):
                r[...] = jnp.zeros_like(r)

        _, vjp = jax.vjp(_lru_gates, x_ref[...], wa_ref[0], wx_ref[0], ba_ref[...], bx_ref[...], lam_ref[...])
        dx, dwa, dwx, dba, dbx, dlam = vjp((da_ref[...], du_ref[...]))
        dx_ref[...] = dx
        dwa_ref[0] += dwa
        dwx_ref[0] += dwx
        dba_ref[...] += dba
        dbx_ref[...] += dbx
        dlam_ref[...] += dlam

    return pl.pallas_call(
        body, grid=(NH, t // TT), in_specs=[blk, wsp, wsp, vsp, vsp, vsp, blk, blk],
        out_specs=[blk, wsp, wsp, vsp, vsp, vsp],
        out_shape=[_sds((t, D), f32), _sds((NH, LRU_BLOCK, LRU_BLOCK), f32), _sds((NH, LRU_BLOCK, LRU_BLOCK), f32),
                   _sds((1, D), f32), _sds((1, D), f32), _sds((1, D), f32)],
        compiler_params=_cp(), name=name)(xc, wa, wx, ba, bx, lam, da, du)


_SCAN_SHIFTS = (1, 2, 4, 8, 16, 32, 64, 128)
_SCAN_PAD = 128


def _gelu(y):
    return jax.nn.gelu(y, approximate=True)


def _lru_scan_fwd(a, u, p2, name):
    t = a.shape[0]
    tc = 128
    blk = pl.BlockSpec((TT, tc), lambda j, i: (i, j))

    def body(a_ref, u_ref, y_ref, h_ref, hg_ref, a_s, b_s, carry):
        i = pl.program_id(1)

        @pl.when(i == 0)
        def _():
            carry[...] = jnp.zeros_like(carry)
            a_s[0:_SCAN_PAD, :] = jnp.ones((_SCAN_PAD, tc), f32)
            b_s[0:_SCAN_PAD, :] = jnp.zeros((_SCAN_PAD, tc), f32)

        av, bv = a_ref[...], u_ref[...]
        for s in _SCAN_SHIFTS:
            if s < 8:
                a_s[_SCAN_PAD:, :] = av
                b_s[_SCAN_PAD:, :] = bv
                ash = a_s[pl.ds(_SCAN_PAD - s, TT), :]
                bsh = b_s[pl.ds(_SCAN_PAD - s, TT), :]
            else:
                ash = jnp.concatenate([jnp.ones((s, tc), f32), av[:TT - s]], axis=0)
                bsh = jnp.concatenate([jnp.zeros((s, tc), f32), bv[:TT - s]], axis=0)
            bv = bv + av * bsh
            av = av * ash
        h = bv + av * carry[7:8, :]
        h_ref[...] = h
        hg_ref[...] = (h * _gelu(y_ref[...])).astype(bf16)
        carry[...] = h[TT - 8:, :]

    return pl.pallas_call(
        body, grid=(D // tc, t // TT), in_specs=[blk, blk, blk], out_specs=[blk, blk],
        out_shape=[_sds((t, D), f32), _sds((t, D), bf16)],
        scratch_shapes=[pltpu.VMEM((_SCAN_PAD + TT, tc), f32), pltpu.VMEM((_SCAN_PAD + TT, tc), f32),
                        pltpu.VMEM((8, tc), f32)],
        compiler_params=_cp(), name=name)(a, u, p2)


def _lru_scan_bwd(a, h, p2, dhg, name):
    t = a.shape[0]
    tc = 128
    nt = t // TT
    hb = TT // 8
    rblk = pl.BlockSpec((TT, tc), lambda j, i: (nt - 1 - i, j))

    def body(a_ref, an_ref, h_ref, hp_ref, y_ref, dhg_ref, du_ref, da_ref, dy_ref, a_s, b_s, ap, hp, carry):
        i = pl.program_id(1)

        @pl.when(i == 0)
        def _():
            carry[...] = jnp.zeros_like(carry)
            a_s[TT:, :] = jnp.ones((_SCAN_PAD, tc), f32)
            b_s[TT:, :] = jnp.zeros((_SCAN_PAD, tc), f32)

        ap[0:TT, :] = a_ref[...]
        ap[TT:, :] = jnp.where(i == 0, 0.0, an_ref[...])
        hp[0:8, :] = jnp.where(i == nt - 1, 0.0, hp_ref[...])
        hp[8:, :] = h_ref[...]
        y = y_ref[...]
        gate, gvjp = jax.vjp(_gelu, y)
        dhg_v = dhg_ref[...]
        dy_ref[...] = gvjp(dhg_v * h_ref[...])[0]
        av = ap[pl.ds(1, TT), :]
        bv = dhg_v * gate
        for s in _SCAN_SHIFTS:
            if s < 8:
                a_s[0:TT, :] = av
                b_s[0:TT, :] = bv
                ash = a_s[pl.ds(s, TT), :]
                bsh = b_s[pl.ds(s, TT), :]
            else:
                ash = jnp.concatenate([av[s:], jnp.ones((s, tc), f32)], axis=0)
                bsh = jnp.concatenate([bv[s:], jnp.zeros((s, tc), f32)], axis=0)
            bv = bv + av * bsh
            av = av * ash
        g = bv + av * carry[0:1, :]
        du_ref[...] = g
        da_ref[...] = g * hp[pl.ds(7, TT), :]
        carry[...] = g[0:8, :]

    in_specs = [
        rblk,
        pl.BlockSpec((8, tc), lambda j, i: (jnp.minimum((nt - i) * hb, t // 8 - 1), j)),
        rblk,
        pl.BlockSpec((8, tc), lambda j, i: (jnp.maximum((nt - 1 - i) * hb - 1, 0), j)),
        rblk, rblk,
    ]
    return pl.pallas_call(
        body, grid=(D // tc, nt), in_specs=in_specs, out_specs=[rblk, rblk, rblk],
        out_shape=[_sds((t, D), f32)] * 3,
        scratch_shapes=[pltpu.VMEM((TT + _SCAN_PAD, tc), f32), pltpu.VMEM((TT + _SCAN_PAD, tc), f32),
                        pltpu.VMEM((TT + 8, tc), f32), pltpu.VMEM((TT + 8, tc), f32), pltpu.VMEM((8, tc), f32)],
        compiler_params=_cp(), name=name)(a, a, h, h, p2, dhg)


def _ffn_act_bwd(gc, up, dact, name, tc=256):
    t = gc.shape[0]
    blk = pl.BlockSpec((TT, tc), lambda i, j: (i, j))
    vblk = pl.BlockSpec((TT, tc), lambda i, j: (i, j + D_FF // tc))

    def body(gc_ref, v_ref, da_ref, dgc_ref, dv_ref):
        _, vjp = jax.vjp(lambda g, v: _silu(g) * v, gc_ref[...], v_ref[...])
        dg, dv = vjp(da_ref[...].astype(f32))
        dgc_ref[...] = dg
        dv_ref[...] = dv.astype(bf16)

    return pl.pallas_call(
        body, grid=(t // TT, D_FF // tc), in_specs=[blk, vblk, blk], out_specs=[blk, vblk],
        out_shape=[_sds((t, D_FF), f32), _sds((t, 2 * D_FF), bf16)], compiler_params=_cp(), name=name)(gc, up, dact)


def _lower_bounds_fwd(w):
    def body(w_ref, o0_ref, o1_ref):
        wv = w_ref[...]
        o0, o1 = _lb_rows(wv[0:1, :], wv[1:2, :])
        o0_ref[...] = o0
        o1_ref[...] = o1

    return pl.pallas_call(body, out_shape=[_sds((1, 512), f32)] * 2, name="lower_bounds_fwd")(w)


def _lb_rows(w0, w1):
    m = jnp.maximum(w0, w1)
    e0, e1 = jnp.exp(w0 - m), jnp.exp(w1 - m)
    s = e0 + e1
    p0, p1 = e0 / s, e1 / s
    return p0 - p0, (p0 + p1) - p0


def _lower_bounds_bwd(w, d0, d1):
    def body(w_ref, d0_ref, d1_ref, g0_ref, g1_ref):
        wv = w_ref[...]
        _, vjp = jax.vjp(_lb_rows, wv[0:1, :], wv[1:2, :])
        g0, g1 = vjp((d0_ref[...], d1_ref[...]))
        g0_ref[...] = g0
        g1_ref[...] = g1

    return pl.pallas_call(body, out_shape=[_sds((1, 512), f32)] * 2, name="lower_bounds_bwd")(w, d0, d1)


def _local_step(x, target, wt, pre_layer=None, post_grads=None):
    depth = 4
    res = []
    lb0, lb1 = _lower_bounds_fwd(wt["hgrn_lower_bounds"])
    lbs = [lb0, lb1]
    for layer in range(depth):
        j = layer // 2
        sv = {"x_in": x}
        deps = pre_layer(layer, "mix", x) if pre_layer else ()
        if layer == 0:
            h1 = _rmsnorm_fwd(x, wt["norm_mix"][layer], "rms_fwd", deps)
        sv["h1"] = h1
        if layer % 2 == 0:
            p = _mm_auto(h1, wt["ab_w_in"][j], "nn", f32, "mm_ab_in")
            conv = _conv_fwd(p, 0, 1536, wt["gdn_conv_w"][j], jnp.zeros((1, 1536), f32), "gdn_conv_fwd", tc=768)
            o_a, s_a = _gdn2_fwd(conv, p, wt["alog_v"][j], wt["dtb_v"][j], wt["gdn_norm"][j], "gdn_fwd")
            o_b, s_b = _hgrn_fwd(p, lbs[j], wt["hgrn_norm"][j], "hgrn_fwd")
            o = jnp.concatenate([o_a, o_b], axis=1)
            x, h2 = _mm_res_norm(o, wt["ab_w_out"][j], x, wt["norm_ffn"][layer], "mm_ab_out")
            sv.update(p=p, conv=conv, s_a=s_a, s_b=s_b, o=o)
        else:
            p2 = _mm_nn_slots(h1, wt["c_w_in"][j], f32, "mm_c_in", deps)
            xc = _conv_fwd(p2, 1, D, wt["c_conv_w"][j], wt["c_conv_b"][j], "lru_conv_fwd", tc=D)
            a, u = _lru_gates_fwd(xc, wt["c_gate_a_w"][j], wt["c_gate_x_w"][j], wt["c_gate_a_b"][j],
                                  wt["c_gate_x_b"][j], wt["c_lambda"][j], "lru_gates_fwd")
            h, hg = _lru_scan_fwd(a, u, p2, "lru_scan_fwd")
            x, h2 = _mm_res_norm(hg, wt["c_w_out"][j], x, wt["norm_ffn"][layer], "mm_c_out")
            sv.update(p2=p2, xc=xc, a=a, h=h, hg=hg)
        sv["x_mid"] = x
        deps = pre_layer(layer, "ffn", x) if pre_layer else ()
        up = _mm_nn_slots(h2, wt["ffn_w_up"][layer], bf16, "mm_up", deps)
        gc, act = _conv_fwd(up, 0, D_FF, wt["ffn_conv_w"][layer], wt["ffn_conv_b"][layer], "ffn_conv_fwd",
                            tc=D_FF // 2, val=up, val_col0=2)
        if layer + 1 < depth:
            x, h1 = _mm_res_norm(act, wt["ffn_w_down"][layer], x, wt["norm_mix"][layer + 1], "mm_down")
        else:
            x = _mm_res_norm(act, wt["ffn_w_down"][layer], x, None, "mm_down_last")
        sv.update(h2=h2, up=up, gc=gc, act=act)
        res.append(sv)

    loss, dx, dxb, d_norm_final = _final_loss(x, wt["norm_final"], target, "final_loss")

    g = {k: [None] * len(v) for k, v in wt.items() if isinstance(v, list)}
    g["norm_final"] = d_norm_final
    d_lbs = [None, None]
    for layer in reversed(range(depth)):
        j = layer // 2
        sv = res[layer]
        dact = _mm_auto(dxb, wt["ffn_w_down"][layer], "nt", bf16, "mm_down_dx")
        g["ffn_w_down"][layer] = _mm_auto(sv["act"], dxb, "tn", bf16, "mm_down_dw")
        dgc, dup = _ffn_act_bwd(sv["gc"], sv["up"], dact, "ffn_act_bwd", tc=D_FF // 2)
        dup, dcw, dcb = _conv_bwd(dgc, sv["up"], 0, wt["ffn_conv_w"][layer], "ffn_conv_bwd", tc=D_FF // 2, into=dup)
        g["ffn_conv_w"][layer] = dcw[:3]
        g["ffn_conv_b"][layer] = dcb
        g["ffn_w_up"][layer] = _mm_tn_slots(sv["h2"], dup, "mm_up_dw")
        deps = post_grads(layer, "ffn", g) if post_grads else ()
        dx, dxb, g["norm_ffn"][layer] = _mm_nt_rmsbwd(dup, wt["ffn_w_up"][layer], sv["x_mid"], wt["norm_ffn"][layer], dx,
                                                      "mm_up_dx", deps)
        if layer % 2 == 0:
            do = _mm_auto(dxb, wt["ab_w_out"][j], "nt", f32, "mm_ab_out_dx")
            g["ab_w_out"][j] = _mm_auto(sv["o"], dxb, "tn", bf16, "mm_ab_out_dw")
            dconv, dz, dba, dal, ddt, dgn = _gdn2_bwd(
                sv["conv"], sv["p"], wt["alog_v"][j], wt["dtb_v"][j], wt["gdn_norm"][j], sv["s_a"], do, "gdn_bwd")
            g["alog_v"][j], g["dtb_v"][j], g["gdn_norm"][j] = dal, ddt, dgn
            dqkv, dcw, _ = _conv_bwd(dconv, sv["p"], 0, wt["gdn_conv_w"][j], "gdn_conv_bwd", tc=768)
            g["gdn_conv_w"][j] = dcw[:4]
            dqfig, dlb, dhn = _hgrn_bwd(sv["p"], lbs[j], wt["hgrn_norm"][j], sv["s_b"], do, "hgrn_bwd")
            g["hgrn_norm"][j] = dhn
            d_lbs[j] = jnp.sum(dlb, axis=0)
            dp = jnp.concatenate([dqkv, dz.astype(bf16), dqfig, dba.astype(bf16)], axis=1)
            g["ab_w_in"][j] = _mm_auto(sv["h1"], dp, "tn", bf16, "mm_ab_in_dw")
            dpre, wpre = dp, wt["ab_w_in"][j]
        else:
            dhg = _mm_auto(dxb, wt["c_w_out"][j], "nt", f32, "mm_c_out_dx")
            g["c_w_out"][j] = _mm_auto(sv["hg"], dxb, "tn", bf16, "mm_c_out_dw")
            du, da, dy = _lru_scan_bwd(sv["a"], sv["h"], sv["p2"], dhg, "lru_scan_bwd")
            dxc, dwa, dwx, dba_, dbx_, dlam = _lru_gates_bwd(
                sv["xc"], wt["c_gate_a_w"][j], wt["c_gate_x_w"][j], wt["c_gate_a_b"][j], wt["c_gate_x_b"][j],
                wt["c_lambda"][j], da, du, "lru_gates_bwd")
            g["c_gate_a_w"][j], g["c_gate_x_w"][j] = dwa, dwx
            g["c_gate_a_b"][j], g["c_gate_x_b"][j], g["c_lambda"][j] = dba_, dbx_, dlam
            dxbr, dcw, dcb = _conv_bwd(dxc, sv["p2"], 1, wt["c_conv_w"][j], "lru_conv_bwd", tc=D)
            g["c_conv_w"][j] = dcw[:4]
            g["c_conv_b"][j] = dcb
            dp2 = jnp.concatenate([dy.astype(bf16), dxbr], axis=1)
            g["c_w_in"][j] = _mm_tn_slots(sv["h1"], dp2, "mm_c_in_dw")
            dpre, wpre = dp2, wt["c_w_in"][j]
        deps = post_grads(layer, "mix", g) if post_grads else ()
        dx, dxb, g["norm_mix"][layer] = _mm_nt_rmsbwd(dpre, wpre, sv["x_in"], wt["norm_mix"][layer], dx, "mm_mix_in_dx", deps)
    g0, g1 = _lower_bounds_bwd(wt["hgrn_lower_bounds"], d_lbs[0], d_lbs[1])
    g["hgrn_lower_bounds"] = jnp.concatenate([g0, g1], axis=0)
    return loss, dx, g


def _ab_in_to_compute(w):
    return jnp.concatenate([w[:, :2048], w[:, 2056:4104], w[:, 2048:2056], jnp.zeros((D, 120), w.dtype)], axis=1)


def _ab_in_from_compute(g):
    return jnp.concatenate([g[:, :2048], g[:, 4096:4104], g[:, 2048:4096]], axis=1)


def _lane_vec(v4):
    return jnp.zeros((1, HEAD), f32).at[0, NH:2 * NH].set(v4)


def _layout_weights(fw):
    wt = {}
    wt["norm_mix"] = [fw["norm_mix"][l][None] for l in range(4)]
    wt["norm_ffn"] = [fw["norm_ffn"][l][None] for l in range(4)]
    wt["norm_final"] = fw["norm_final"][None]
    wt["gdn_conv_w"] = [fw["gdn_conv_w"][j] for j in range(2)]
    wt["alog_v"] = [_lane_vec(fw["gdn_a_log"][j]) for j in range(2)]
    wt["dtb_v"] = [_lane_vec(fw["gdn_dt_bias"][j]) for j in range(2)]
    wt["gdn_norm"] = [fw["gdn_norm"][j][None] for j in range(2)]
    wt["hgrn_lower_bounds"] = fw["hgrn_lower_bounds"]
    wt["hgrn_norm"] = [fw["hgrn_norm"][j][None] for j in range(2)]
    wt["c_conv_w"] = [fw["c_conv_w"][j] for j in range(2)]
    for k in ("c_conv_b", "c_gate_a_b", "c_gate_x_b", "c_lambda"):
        wt[k] = [fw[k][j][None] for j in range(2)]
    wt["ffn_conv_w"] = [fw["ffn_conv_w"][l] for l in range(4)]
    wt["ffn_conv_b"] = [fw["ffn_conv_b"][l][None] for l in range(4)]
    if "ab_w_in" in fw:
        wt["ab_w_in"] = [_ab_in_to_compute(fw["ab_w_in"][j].astype(bf16)) for j in range(2)]
        for k in ("ab_w_out", "c_w_out"):
            wt[k] = [fw[k][j].astype(bf16) for j in range(2)]
        wt["c_w_in"] = [_to_slots(fw["c_w_in"][j].astype(bf16)) for j in range(2)]
        for k in ("c_gate_a_w", "c_gate_x_w"):
            wt[k] = [fw[k][j].astype(f32) for j in range(2)]
        wt["ffn_w_up"] = [_to_slots(fw["ffn_w_up"][l].astype(bf16)) for l in range(4)]
        wt["ffn_w_down"] = [fw["ffn_w_down"][l].astype(bf16) for l in range(4)]
    return wt


SLOT_MAJOR = ("c_w_in", "ffn_w_up")
NATIVE_PERM = {"ab_w_in": (2, 0, 1), "ffn_conv_w": (1, 0, 2)}


def _to_slots(wfull):
    k, n = wfull.shape
    return wfull.reshape(k, N_SLOTS, n // N_SLOTS).transpose(1, 0, 2)


def _layer_full(name, slots):
    kind = BIG[name]
    if name in SLOT_MAJOR:
        return slots
    if kind == "col":
        return _ab_in_to_compute(slots.transpose(1, 0, 2).reshape(slots.shape[1], -1))
    if kind == "row":
        return slots.reshape(-1, slots.shape[2])
    return slots.reshape(4, NH, LRU_BLOCK // 4, LRU_BLOCK).transpose(1, 0, 2, 3).reshape(NH, LRU_BLOCK, LRU_BLOCK).astype(f32)


def _layer_slots(name, g):
    kind = BIG[name]
    if name in SLOT_MAJOR:
        return g
    if kind == "col":
        g = _ab_in_from_compute(g)
        r, cdim = g.shape
        return g.reshape(r, 4, cdim // 4).transpose(1, 0, 2).astype(bf16)
    if kind == "row":
        r, cdim = g.shape
        return g.reshape(4, r // 4, cdim).astype(bf16)
    return g.reshape(NH, 4, LRU_BLOCK // 4, LRU_BLOCK).transpose(1, 0, 2, 3).reshape(4, LRU_BLOCK, LRU_BLOCK).astype(bf16)


def _unlayout_grads(g):
    out = {}
    for k in ("norm_mix", "norm_ffn", "gdn_norm", "hgrn_norm", "c_conv_b", "c_gate_a_b", "c_gate_x_b", "c_lambda",
              "ffn_conv_b"):
        out[k] = jnp.concatenate(g[k], axis=0)
    out["norm_final"] = g["norm_final"][0]
    out["ab_w_in"] = jnp.stack([_ab_in_from_compute(t) for t in g["ab_w_in"]])
    out["gdn_a_log"] = jnp.stack([t[0, NH:2 * NH] for t in g["alog_v"]])
    out["gdn_dt_bias"] = jnp.stack([t[0, NH:2 * NH] for t in g["dtb_v"]])
    out["hgrn_lower_bounds"] = g["hgrn_lower_bounds"]
    for k in ("gdn_conv_w", "ab_w_out", "c_conv_w", "c_gate_a_w", "c_gate_x_w", "c_w_out", "ffn_conv_w", "ffn_w_down"):
        out[k] = jnp.stack(g[k])
    for k in SLOT_MAJOR:
        out[k] = jnp.stack([t.transpose(1, 0, 2).reshape(t.shape[1], -1) for t in g[k]])
    return out


MESH = pl.DeviceIdType.MESH
ANY = pl.BlockSpec(memory_space=pl.ANY)
CHIP_RELATIONS = ((1, 0), (0, 1), (1, 1))
N_CHIPS = 4


def _coords():
    return lax.axis_index("x"), lax.axis_index("y"), lax.axis_index("c")


def _flip(v, f):
    return 1 - v if f else v


def _half_rows(c, a, align):
    return pl.ds(pl.multiple_of(c * (a // 2), align), a // 2)


def _all_gather_chips(shards, name):
    n = len(shards)
    shapes = [s.shape for s in shards]

    def body(*refs):
        ins, outs = refs[:n], refs[n:2 * n]
        send_sems, recv_sems = refs[2 * n:]
        x, y, c = _coords()
        me = 2 * x + y
        sibling = (x, y, 1 - c)
        started = []
        for p in range(n):
            cp = pltpu.make_async_remote_copy(
                src_ref=ins[p], dst_ref=outs[p].at[me],
                send_sem=send_sems.at[p, 6], recv_sem=recv_sems.at[p, 6],
                device_id=sibling, device_id_type=MESH)
            cp.start()
            started.append(cp)
        for p in range(n):
            mine = _half_rows(c, shapes[p][0], 16)
            for r, (fx, fy) in enumerate(CHIP_RELATIONS):
                cp = pltpu.make_async_remote_copy(
                    src_ref=ins[p].at[mine], dst_ref=outs[p].at[me, mine],
                    send_sem=send_sems.at[p, r], recv_sem=recv_sems.at[p, r],
                    device_id=(_flip(x, fx), _flip(y, fy), c), device_id_type=MESH)
                cp.start()
                started.append(cp)
        for r, (fx, fy) in enumerate(CHIP_RELATIONS):
            k = 2 * _flip(x, fx) + _flip(y, fy)
            for p in range(n):
                mine = _half_rows(c, shapes[p][0], 16)
                pltpu.make_async_remote_copy(
                    src_ref=ins[p].at[mine], dst_ref=outs[p].at[k, mine],
                    send_sem=send_sems.at[p, r], recv_sem=recv_sems.at[p, r],
                    device_id=(_flip(x, fx), _flip(y, fy), c), device_id_type=MESH).wait_recv()
                fwd = pltpu.make_async_remote_copy(
                    src_ref=outs[p].at[k, mine], dst_ref=outs[p].at[k, mine],
                    send_sem=send_sems.at[p, 3 + r], recv_sem=recv_sems.at[p, 3 + r],
                    device_id=sibling, device_id_type=MESH)
                fwd.start()
                started.append(fwd)
        for r, (fx, fy) in enumerate(CHIP_RELATIONS):
            k = 2 * _flip(x, fx) + _flip(y, fy)
            for p in range(n):
                theirs = _half_rows(1 - c, shapes[p][0], 16)
                pltpu.make_async_remote_copy(
                    src_ref=outs[p].at[k, theirs], dst_ref=outs[p].at[k, theirs],
                    send_sem=send_sems.at[p, 3 + r], recv_sem=recv_sems.at[p, 3 + r],
                    device_id=sibling, device_id_type=MESH).wait_recv()
        for p in range(n):
            pltpu.make_async_remote_copy(
                src_ref=ins[p], dst_ref=outs[p].at[me],
                send_sem=send_sems.at[p, 6], recv_sem=recv_sems.at[p, 6],
                device_id=sibling, device_id_type=MESH).wait_recv()
        for cp in started:
            cp.wait_send()

    return pl.pallas_call(
        body, in_specs=[ANY] * n, out_specs=[ANY] * n,
        out_shape=[_sds((N_CHIPS,) + s.shape, s.dtype) for s in shards],
        scratch_shapes=[pltpu.SemaphoreType.DMA((n, 7)), pltpu.SemaphoreType.DMA((n, 7))],
        name=name)(*shards)


def _sibling_send_other_half(gs, name):
    n = len(gs)
    shapes = [g.shape for g in gs]

    def body(*refs):
        ins, outs = refs[:n], refs[n:2 * n]
        send_sems, recv_sems = refs[2 * n:]
        x, y, c = _coords()
        cps = []
        for p in range(n):
            theirs = _half_rows(1 - c, shapes[p][1], 8)
            cp = pltpu.make_async_remote_copy(
                src_ref=ins[p].at[:, theirs], dst_ref=outs[p],
                send_sem=send_sems.at[p], recv_sem=recv_sems.at[p],
                device_id=(x, y, 1 - c), device_id_type=MESH)
            cp.start()
            cps.append(cp)
        for cp in cps:
            cp.wait()

    return pl.pallas_call(
        body, in_specs=[ANY] * n, out_specs=[ANY] * n,
        out_shape=[_sds((s[0], s[1] // 2, s[2]), f32) for s in shapes],
        scratch_shapes=[pltpu.SemaphoreType.DMA((n,)), pltpu.SemaphoreType.DMA((n,))],
        name=name)(*gs)


def _chip_exchange(ps, name):
    n = len(ps)

    def body(*refs):
        ins, outs = refs[:n], refs[n:2 * n]
        send_sems, recv_sems = refs[2 * n:]
        x, y, c = _coords()
        cps = []
        for p in range(n):
            for r, (fx, fy) in enumerate(CHIP_RELATIONS):
                k = 2 * _flip(x, fx) + _flip(y, fy)
                cp = pltpu.make_async_remote_copy(
                    src_ref=ins[p].at[k], dst_ref=outs[p].at[r],
                    send_sem=send_sems.at[p, r], recv_sem=recv_sems.at[p, r],
                    device_id=(_flip(x, fx), _flip(y, fy), c), device_id_type=MESH)
                cp.start()
                cps.append(cp)
        for cp in cps:
            cp.wait_recv()
        for cp in cps:
            cp.wait_send()

    return pl.pallas_call(
        body, in_specs=[ANY] * n, out_specs=[ANY] * n,
        out_shape=[_sds((3,) + p.shape[1:], p.dtype) for p in ps],
        scratch_shapes=[pltpu.SemaphoreType.DMA((n, 3)), pltpu.SemaphoreType.DMA((n, 3))],
        name=name)(*ps)


def _sibling_fill_other_half(fs, name):
    n = len(fs)
    shapes = [f.shape for f in fs]

    def body(*refs):
        ins, outs = refs[:n], refs[n:2 * n]
        send_sems, recv_sems = refs[2 * n:]
        x, y, c = _coords()
        cps = []
        for p in range(n):
            mine = _half_rows(c, shapes[p][0], 8)
            cp = pltpu.make_async_remote_copy(
                src_ref=ins[p].at[mine], dst_ref=outs[p].at[mine],
                send_sem=send_sems.at[p], recv_sem=recv_sems.at[p],
                device_id=(x, y, 1 - c), device_id_type=MESH)
            cp.start()
            cps.append(cp)
        for p in range(n):
            theirs = _half_rows(1 - c, shapes[p][0], 8)
            pltpu.make_async_remote_copy(
                src_ref=ins[p].at[theirs], dst_ref=outs[p].at[theirs],
                send_sem=send_sems.at[p], recv_sem=recv_sems.at[p],
                device_id=(x, y, 1 - c), device_id_type=MESH).wait_recv()
        for cp in cps:
            cp.wait_send()

    return pl.pallas_call(
        body, in_specs=[ANY] * n, out_specs=[ANY] * n,
        out_shape=[_sds(f.shape, f.dtype) for f in fs],
        input_output_aliases={p: p for p in range(n)},
        scratch_shapes=[pltpu.SemaphoreType.DMA((n,)), pltpu.SemaphoreType.DMA((n,))],
        name=name)(*fs)


def _sibling_all_gather(ss, name):
    n = len(ss)
    shapes = [s.shape for s in ss]

    def body(*refs):
        ins, outs = refs[:n], refs[n:2 * n]
        send_sems, recv_sems, loc_sems = refs[2 * n:]
        x, y, c = _coords()
        locs, cps = [], []
        for p in range(n):
            mine = _half_rows(c, 2 * shapes[p][0], 8)
            loc = pltpu.make_async_copy(ins[p], outs[p].at[mine], loc_sems.at[p])
            loc.start()
            locs.append(loc)
            cp = pltpu.make_async_remote_copy(
                src_ref=ins[p], dst_ref=outs[p].at[mine],
                send_sem=send_sems.at[p], recv_sem=recv_sems.at[p],
                device_id=(x, y, 1 - c), device_id_type=MESH)
            cp.start()
            cps.append(cp)
        for p, cp in enumerate(cps):
            theirs = _half_rows(1 - c, 2 * shapes[p][0], 8)
            pltpu.make_async_remote_copy(
                src_ref=ins[p], dst_ref=outs[p].at[theirs],
                send_sem=send_sems.at[p], recv_sem=recv_sems.at[p],
                device_id=(x, y, 1 - c), device_id_type=MESH).wait_recv()
        for cp in cps:
            cp.wait_send()
        for loc in locs:
            loc.wait()

    return pl.pallas_call(
        body, in_specs=[ANY] * n, out_specs=[ANY] * n,
        out_shape=[_sds((2 * s[0], s[1]), f32) for s in shapes],
        scratch_shapes=[pltpu.SemaphoreType.DMA((n,)), pltpu.SemaphoreType.DMA((n,)), pltpu.SemaphoreType.DMA((n,))],
        name=name)(*ss)


N_DEV = 8


def _all_reduce_small(pack, name, after=()):
    rows = pack.shape[0]

    def body(in_ref, *rest):
        sum_ref, all_ref, send_sems, recv_sems = rest[len(after):]
        x, y, c = _coords()
        me = 4 * x + 2 * y + c
        all_ref[me] = in_ref[...]
        cps = []
        for r in range(1, N_DEV):
            fx, fy, fc = (r >> 2) & 1, (r >> 1) & 1, r & 1
            cp = pltpu.make_async_remote_copy(
                src_ref=in_ref, dst_ref=all_ref.at[me],
                send_sem=send_sems.at[r], recv_sem=recv_sems.at[r],
                device_id=(_flip(x, fx), _flip(y, fy), _flip(c, fc)), device_id_type=MESH)
            cp.start()
            cps.append(cp)
        for r in range(1, N_DEV):
            fx, fy, fc = (r >> 2) & 1, (r >> 1) & 1, r & 1
            peer = 4 * _flip(x, fx) + 2 * _flip(y, fy) + _flip(c, fc)
            pltpu.make_async_remote_copy(
                src_ref=in_ref, dst_ref=all_ref.at[peer],
                send_sem=send_sems.at[r], recv_sem=recv_sems.at[r],
                device_id=(x, y, c), device_id_type=MESH).wait_recv()
        for cp in cps:
            cp.wait_send()
        acc = all_ref[0]
        for d in range(1, N_DEV):
            acc = acc + all_ref[d]
        sum_ref[...] = acc

    vm = pl.BlockSpec(memory_space=pltpu.VMEM)
    return pl.pallas_call(
        body, in_specs=[vm] + [pl.BlockSpec(memory_space=pl.ANY)] * len(after), out_specs=[vm, vm],
        out_shape=[_sds((rows, 128), f32), _sds((N_DEV, rows, 128), f32)],
        scratch_shapes=[pltpu.SemaphoreType.DMA((N_DEV,)), pltpu.SemaphoreType.DMA((N_DEV,))],
        name=name)(pack, *after)[0]


ROWS_EW = 128


def _add_own_half(g, rs, c_arr, name):
    s, a, b = g.shape
    nrt = (a // 2) // ROWS_EW

    def body(c_ref, g_ref, r_ref, o_ref):
        o_ref[...] = (g_ref[...] + r_ref[...]).astype(bf16)

    grid_spec = pltpu.PrefetchScalarGridSpec(
        num_scalar_prefetch=1, grid=(s, nrt),
        in_specs=[pl.BlockSpec((1, ROWS_EW, b), lambda k, i, c_ref: (k, c_ref[0] * nrt + i, 0)),
                  pl.BlockSpec((1, ROWS_EW, b), lambda k, i, c_ref: (k, i, 0))],
        out_specs=pl.BlockSpec((1, ROWS_EW, b), lambda k, i, c_ref: (k, i, 0)))
    return pl.pallas_call(body, grid_spec=grid_spec, out_shape=_sds((s, a // 2, b), bf16),
                          compiler_params=_cp(), name=name)(c_arr, g, rs)


def _sum_chips(g, rs, rc, ids, name):
    _, r, b = rc.shape
    nrt = r // ROWS_EW

    def body(ids_ref, g_ref, s_ref, r_ref, o_ref):
        own = g_ref[0] + s_ref[0]
        o_ref[...] = ((own + r_ref[0].astype(f32)) + r_ref[1].astype(f32)) + r_ref[2].astype(f32)

    grid_spec = pltpu.PrefetchScalarGridSpec(
        num_scalar_prefetch=1, grid=(nrt,),
        in_specs=[pl.BlockSpec((1, ROWS_EW, b), lambda i, ids_ref: (ids_ref[0], ids_ref[1] * nrt + i, 0)),
                  pl.BlockSpec((1, ROWS_EW, b), lambda i, ids_ref: (ids_ref[0], i, 0)),
                  pl.BlockSpec((3, ROWS_EW, b), lambda i, ids_ref: (0, i, 0))],
        out_specs=pl.BlockSpec((ROWS_EW, b), lambda i, ids_ref: (ids_ref[1] * nrt + i, 0)))
    return pl.pallas_call(body, grid_spec=grid_spec, out_shape=_sds((2 * r, b), f32),
                          compiler_params=_cp(), name=name)(ids, g, rs, rc)


def _adamw_math(w, g, m, v):
    m = ADAM_B1 * m + (1.0 - ADAM_B1) * g
    v = ADAM_B2 * v + (1.0 - ADAM_B2) * (g * g)
    m_hat = m / (1.0 - ADAM_B1 ** ADAM_STEP)
    v_hat = v / (1.0 - ADAM_B2 ** ADAM_STEP)
    delta = -ADAM_LR * (m_hat / (jnp.sqrt(v_hat) + ADAM_EPS) + ADAM_WD * w)
    return delta, m, v


def _adamw_big(w, g, m, v, name):
    nl, r, b = w.shape
    rt = _tile(r, (ROWS_EW, 352, 64))
    per = r // rt

    def body(w_ref, g_ref, m_ref, v_ref, go_ref, d_ref, mo_ref, vo_ref):
        gv = g_ref[...]
        d, mn, vn = _adamw_math(w_ref[...], gv, m_ref[...], v_ref[...])
        go_ref[...] = gv
        d_ref[...] = d
        mo_ref[...] = mn
        vo_ref[...] = vn

    blk = pl.BlockSpec((None, rt, b), lambda l, i: (l, i, 0))
    gblk = pl.BlockSpec((rt, b), lambda l, i: (l * per + i, 0))
    return pl.pallas_call(body, grid=(nl, per), in_specs=[blk, gblk, blk, blk], out_specs=[blk] * 4,
                          out_shape=[_sds((nl, r, b), f32)] * 4, compiler_params=_cp(), name=name)(w, g, m, v)


def _adamw_lead(w, g, m, v, name):
    n, r, b = w.shape
    tn = _tile(n, (64, 54, 32, 16, 8, 1))

    def body(w_ref, g_ref, m_ref, v_ref, go_ref, d_ref, mo_ref, vo_ref):
        gv = g_ref[...]
        d, mn, vn = _adamw_math(w_ref[...], gv, m_ref[...], v_ref[...])
        go_ref[...] = gv
        d_ref[...] = d
        mo_ref[...] = mn
        vo_ref[...] = vn

    blk = pl.BlockSpec((tn, r, b), lambda i: (i, 0, 0))
    return pl.pallas_call(body, grid=(n // tn,), in_specs=[blk] * 4, out_specs=[blk] * 4,
                          out_shape=[_sds((n, r, b), f32)] * 4, compiler_params=_cp(), name=name)(w, g, m, v)


def _adamw_small(ws, gs, ms, vs, name):
    n = len(ws)

    def body(*refs):
        w_r, g_r, m_r, v_r = refs[:n], refs[n:2 * n], refs[2 * n:3 * n], refs[3 * n:4 * n]
        go_r, d_r, mo_r, vo_r = refs[4 * n:5 * n], refs[5 * n:6 * n], refs[6 * n:7 * n], refs[7 * n:8 * n]
        for p in range(n):
            gv = g_r[p][...]
            d, mn, vn = _adamw_math(w_r[p][...], gv, m_r[p][...], v_r[p][...])
            go_r[p][...] = gv
            d_r[p][...] = d
            mo_r[p][...] = mn
            vo_r[p][...] = vn

    vm = pl.BlockSpec(memory_space=pltpu.VMEM)
    shp = [_sds(w.shape, f32) for w in ws]
    res = pl.pallas_call(body, in_specs=[vm] * (4 * n), out_specs=[vm] * (4 * n), out_shape=shp * 4,
                         name=name)(*ws, *gs, *ms, *vs)
    return res[:n], res[n:2 * n], res[2 * n:3 * n], res[3 * n:]


HBM = pl.BlockSpec(memory_space=pltpu.HBM)
SEM = pl.BlockSpec(memory_space=pltpu.SEMAPHORE)
EFFECT = pltpu.SideEffectType.DATAFLOW_SIDE_EFFECTING
N_REL = 8


def _rel(r):
    return (r >> 2) & 1, (r >> 1) & 1, r & 1


def _gather_copies(ins, lands, send_sems, recv_sems, shapes):
    x, y, c = _coords()
    me = 2 * x + y
    sends, recvs = [], []
    for p in range(len(ins)):
        for r in range(1, N_REL):
            fx, fy, fc = _rel(r)
            peer = (_flip(x, fx), _flip(y, fy), _flip(c, fc))
            if fx == 0 and fy == 0:
                src, dst, got = ins[p], lands[p].at[me], lands[p].at[me]
            else:
                mine = _half_rows(c, shapes[p][0], 16)
                theirs = _half_rows(_flip(c, fc), shapes[p][0], 16)
                src, dst = ins[p].at[mine], lands[p].at[me, mine]
                got = lands[p].at[2 * peer[0] + peer[1], theirs]
            sems = dict(send_sem=send_sems.at[p * N_REL + r], recv_sem=recv_sems.at[p * N_REL + r], device_id=peer,
                        device_id_type=MESH)
            sends.append(pltpu.make_async_remote_copy(src_ref=src, dst_ref=dst, **sems))
            recvs.append(pltpu.make_async_remote_copy(src_ref=src, dst_ref=got, **sems))
    return sends, recvs


def _scatter_copies(ins, lands, send_sems, recv_sems, shapes):
    x, y, c = _coords()
    sends, recvs = [], []
    for p in range(len(ins)):
        for r in range(1, N_REL):
            fx, fy, fc = _rel(r)
            peer = (_flip(x, fx), _flip(y, fy), _flip(c, fc))
            theirs = _half_rows(peer[2], shapes[p][1], 16)
            sems = dict(send_sem=send_sems.at[p * N_REL + r], recv_sem=recv_sems.at[p * N_REL + r], device_id=peer,
                        device_id_type=MESH)
            cp = pltpu.make_async_remote_copy(src_ref=ins[p].at[2 * peer[0] + peer[1], theirs], dst_ref=lands[p].at[r], **sems)
            sends.append(cp)
            recvs.append(cp)
    return sends, recvs


def _split_start(copies_fn, ins, land_shapes, name, after=()):
    n = len(ins)
    shapes = [a.shape for a in ins]

    def body(*refs):
        in_refs, land_refs = refs[:n], refs[n:2 * n]
        send_sems, recv_sems = refs[2 * n + len(after)], refs[2 * n + len(after) + 1]
        token = refs[-1]
        sends, _ = copies_fn(in_refs, land_refs, send_sems, recv_sems, shapes)
        for cp in sends:
            cp.start()
        token[...] = jnp.zeros_like(token)

    lands = [lax.empty(s.shape, s.dtype) for s in land_shapes]
    na = len(after)
    res = pl.pallas_call(
        body, name=name,
        out_shape=(pltpu.SemaphoreType.DMA((n * N_REL,)), pltpu.SemaphoreType.DMA((n * N_REL,)))
        + tuple(pltpu.HBM(a.shape, a.dtype) for a in ins) + tuple(pltpu.HBM(s.shape, s.dtype) for s in land_shapes)
        + (_sds((8, 128), f32),),
        in_specs=[HBM] * (2 * n) + [pl.BlockSpec(memory_space=pl.ANY)] * na,
        out_specs=(SEM, SEM) + (HBM,) * (2 * n) + (pl.BlockSpec(memory_space=pltpu.VMEM),),
        input_output_aliases={i: 2 + i for i in range(2 * n)},
        compiler_params=pltpu.CompilerParams(has_side_effects=EFFECT),
    )(*[pltpu.with_memory_space_constraint(a, pltpu.HBM) for a in ins],
      *[pltpu.with_memory_space_constraint(a, pltpu.HBM) for a in lands], *after)
    return dict(sems=res[:2], ins=res[2:2 + n], lands=res[2 + n:2 + 2 * n], token=res[-1], shapes=shapes)


def _split_wait(copies_fn, started, after, name):
    n = len(started["ins"])
    shapes = started["shapes"]
    na = len(after)

    def body(*refs):
        in_refs, land_refs = refs[:n], refs[n:2 * n]
        send_sems, recv_sems = refs[2 * n], refs[2 * n + 1]
        sends, recvs = copies_fn(in_refs, land_refs, send_sems, recv_sems, shapes)
        for cp in sends:
            cp.wait_send()
        for cp in recvs:
            cp.wait_recv()

    arrs = list(started["ins"]) + list(started["lands"])
    res = pl.pallas_call(
        body, name=name,
        out_shape=tuple(pltpu.HBM(a.shape, a.dtype) for a in arrs),
        in_specs=[HBM] * (2 * n) + [SEM, SEM] + [pl.BlockSpec(memory_space=pl.ANY)] * na,
        out_specs=(HBM,) * (2 * n), input_output_aliases={i: i for i in range(2 * n)},
        compiler_params=pltpu.CompilerParams(has_side_effects=EFFECT),
    )(*arrs, *started["sems"], *after)
    return res[:n], res[n:]


def _sum_pieces(gb, land, ids, f_prev, blk, nblk, name):
    _, a, b = gb.shape
    rows = _tile(a // 2, (ROWS_EW, 176, 64, 32, 16))
    nrt = (a // 2) // rows

    def body(ids_ref, g_ref, l_ref, *rest):
        o_ref = rest[-1]
        acc = g_ref[0].astype(f32)
        for r in range(1, N_REL):
            acc = acc + l_ref[r].astype(f32)
        o_ref[...] = acc

    in_specs = [pl.BlockSpec((1, rows, b), lambda i, ids_ref: (ids_ref[0], ids_ref[1] * nrt + i, 0)),
                pl.BlockSpec((N_REL, rows, b), lambda i, ids_ref: (0, i, 0))]
    args = [ids, gb, land]
    aliases = {}
    if f_prev is not None:
        in_specs.append(pl.BlockSpec(memory_space=pl.ANY))
        args.append(f_prev)
        aliases = {3: 0}
    grid_spec = pltpu.PrefetchScalarGridSpec(
        num_scalar_prefetch=1, grid=(nrt,), in_specs=in_specs,
        out_specs=pl.BlockSpec((rows, b), lambda i, ids_ref: ((2 * blk + ids_ref[1]) * nrt + i, 0)))
    return pl.pallas_call(body, grid_spec=grid_spec, out_shape=_sds((nblk * a, b), f32),
                          input_output_aliases=aliases, compiler_params=_cp(), name=name)(*args)


def _sibling_fill_blocks(fs, nblks, name):
    n = len(fs)
    shapes = [f.shape for f in fs]

    def body(*refs):
        ins, outs = refs[:n], refs[n:2 * n]
        send_sems, recv_sems = refs[2 * n:]
        x, y, c = _coords()
        cps, waits = [], []
        k = 0
        for p in range(n):
            a = shapes[p][0] // nblks[p]
            for bi in range(nblks[p]):
                mine = pl.ds(pl.multiple_of(bi * a + c * (a // 2), 8), a // 2)
                theirs = pl.ds(pl.multiple_of(bi * a + (1 - c) * (a // 2), 8), a // 2)
                sems = dict(send_sem=send_sems.at[k], recv_sem=recv_sems.at[k], device_id=(x, y, 1 - c), device_id_type=MESH)
                cp = pltpu.make_async_remote_copy(src_ref=ins[p].at[mine], dst_ref=outs[p].at[mine], **sems)
                cp.start()
                cps.append(cp)
                waits.append(pltpu.make_async_remote_copy(src_ref=ins[p].at[theirs], dst_ref=outs[p].at[theirs], **sems))
                k += 1
        for wt_ in waits:
            wt_.wait_recv()
        for cp in cps:
            cp.wait_send()

    total = sum(nblks)
    return pl.pallas_call(
        body, in_specs=[ANY] * n, out_specs=[ANY] * n,
        out_shape=[_sds(f.shape, f.dtype) for f in fs],
        input_output_aliases={p: p for p in range(n)},
        scratch_shapes=[pltpu.SemaphoreType.DMA((total,)), pltpu.SemaphoreType.DMA((total,))],
        name=name)(*fs)


WEIGHTS = ["norm_mix", "norm_ffn", "norm_final", "ab_w_in", "gdn_conv_w", "gdn_a_log", "gdn_dt_bias", "gdn_norm",
           "hgrn_lower_bounds", "hgrn_norm", "ab_w_out", "c_w_in", "c_conv_w", "c_conv_b", "c_gate_a_w", "c_gate_a_b",
           "c_gate_x_w", "c_gate_x_b", "c_lambda", "c_w_out", "ffn_w_up", "ffn_conv_w", "ffn_conv_b", "ffn_w_down"]
BIG = {"ab_w_in": "col", "ab_w_out": "row", "c_w_in": "col", "c_gate_a_w": "gate", "c_gate_x_w": "gate",
       "c_w_out": "row", "ffn_w_up": "col", "ffn_w_down": "row"}
SMALL_SHARDED = ["gdn_conv_w", "c_conv_w", "c_conv_b", "c_gate_a_b", "c_gate_x_b", "c_lambda", "ffn_conv_w"]
SMALL = [n for n in WEIGHTS if n not in BIG]
FULL_SHAPES = {
    "norm_mix": (4, 1024), "norm_ffn": (4, 1024), "norm_final": (1024,), "ab_w_in": (2, 1024, 4104),
    "gdn_conv_w": (2, 4, 1536), "gdn_a_log": (2, 4), "gdn_dt_bias": (2, 4), "gdn_norm": (2, 128),
    "hgrn_lower_bounds": (2, 512), "hgrn_norm": (2, 128), "ab_w_out": (2, 1024, 1024), "c_w_in": (2, 1024, 2048),
    "c_conv_w": (2, 4, 1024), "c_conv_b": (2, 1024), "c_gate_a_w": (2, 4, 256, 256), "c_gate_a_b": (2, 1024),
    "c_gate_x_w": (2, 4, 256, 256), "c_gate_x_b": (2, 1024), "c_lambda": (2, 1024), "c_w_out": (2, 1024, 1024),
    "ffn_w_up": (4, 1024, 5632), "ffn_conv_w": (4, 3, 2816), "ffn_conv_b": (4, 2816), "ffn_w_down": (4, 2816, 1024)}


def _shard2d(name, shard):
    return shard.reshape(-1, shard.shape[-1])


def _full_from_slots(name, slots):
    full = FULL_SHAPES[name]
    kind = BIG[name]
    if kind == "col":
        l, r, cdim = full
        return slots.reshape(4, l, r, cdim // 4).transpose(1, 2, 0, 3).reshape(full)
    if kind == "row":
        l, r, cdim = full
        return slots.reshape(4, l, r // 4, cdim).transpose(1, 0, 2, 3).reshape(full)
    l, h, r, cdim = full
    return slots.reshape(4, l, h, r // 4, cdim).transpose(1, 2, 0, 3, 4).reshape(full)


def _slots_from_full(name, g):
    full = FULL_SHAPES[name]
    kind = BIG[name]
    if kind == "col":
        l, r, cdim = full
        return g.reshape(l, r, 4, cdim // 4).transpose(2, 0, 1, 3).reshape(4, l * r, cdim // 4)
    if kind == "row":
        l, r, cdim = full
        return g.reshape(l, 4, r // 4, cdim).transpose(1, 0, 2, 3).reshape(4, l * (r // 4), cdim)
    l, h, r, cdim = full
    return g.reshape(l, h, 4, r // 4, cdim).transpose(2, 0, 1, 3, 4).reshape(4, l * h * (r // 4), cdim)


def _pack_rows(arrs, rows):
    parts, used = [], 0
    for a in arrs:
        r = _pack_tile_rows(a.size)
        parts.append(jnp.pad(a.reshape(-1), (0, r * 128 - a.size)).reshape(r, 128))
        used += r
    assert rows >= used and (rows - used) % 8 == 0, (rows, used)
    if rows > used:
        parts.append(jnp.zeros((rows - used, 128), f32))
    return jnp.concatenate(parts, axis=0)


def _pack_tile_rows(size):
    return -(-size // 1024) * 8


def _unpack_rows(pack, shapes):
    out, row = [], 0
    for s in shapes:
        size = 1
        for d in s:
            size *= d
        r = _pack_tile_rows(size)
        out.append(pack[row:row + r].reshape(-1)[:size].reshape(s))
        row += r
    return out


def kernel(x, norm_mix, norm_ffn, norm_final, ab_w_in, gdn_conv_w, gdn_a_log, gdn_dt_bias, gdn_norm, hgrn_lower_bounds, hgrn_norm, ab_w_out, c_w_in, c_conv_w, c_conv_b, c_gate_a_w, c_gate_a_b, c_gate_x_w, c_gate_x_b, c_lambda, c_w_out, ffn_w_up, ffn_conv_w, ffn_conv_b, ffn_w_down, loss_target, m_norm_mix, m_norm_ffn, m_norm_final, m_ab_w_in, m_gdn_conv_w, m_gdn_a_log, m_gdn_dt_bias, m_gdn_norm, m_hgrn_lower_bounds, m_hgrn_norm, m_ab_w_out, m_c_w_in, m_c_conv_w, m_c_conv_b, m_c_gate_a_w, m_c_gate_a_b, m_c_gate_x_w, m_c_gate_x_b, m_c_lambda, m_c_w_out, m_ffn_w_up, m_ffn_conv_w, m_ffn_conv_b, m_ffn_w_down, v_norm_mix, v_norm_ffn, v_norm_final, v_ab_w_in, v_gdn_conv_w, v_gdn_a_log, v_gdn_dt_bias, v_gdn_norm, v_hgrn_lower_bounds, v_hgrn_norm, v_ab_w_out, v_c_w_in, v_c_conv_w, v_c_conv_b, v_c_gate_a_w, v_c_gate_a_b, v_c_gate_x_w, v_c_gate_x_b, v_c_lambda, v_c_w_out, v_ffn_w_up, v_ffn_conv_w, v_ffn_conv_b, v_ffn_w_down):
    w = dict(zip(WEIGHTS, (norm_mix, norm_ffn, norm_final, ab_w_in, gdn_conv_w, gdn_a_log, gdn_dt_bias, gdn_norm, hgrn_lower_bounds, hgrn_norm, ab_w_out, c_w_in, c_conv_w, c_conv_b, c_gate_a_w, c_gate_a_b, c_gate_x_w, c_gate_x_b, c_lambda, c_w_out, ffn_w_up, ffn_conv_w, ffn_conv_b, ffn_w_down)))
    m = dict(zip(WEIGHTS, (m_norm_mix, m_norm_ffn, m_norm_final, m_ab_w_in, m_gdn_conv_w, m_gdn_a_log, m_gdn_dt_bias, m_gdn_norm, m_hgrn_lower_bounds, m_hgrn_norm, m_ab_w_out, m_c_w_in, m_c_conv_w, m_c_conv_b, m_c_gate_a_w, m_c_gate_a_b, m_c_gate_x_w, m_c_gate_x_b, m_c_lambda, m_c_w_out, m_ffn_w_up, m_ffn_conv_w, m_ffn_conv_b, m_ffn_w_down)))
    v = dict(zip(WEIGHTS, (v_norm_mix, v_norm_ffn, v_norm_final, v_ab_w_in, v_gdn_conv_w, v_gdn_a_log, v_gdn_dt_bias, v_gdn_norm, v_hgrn_lower_bounds, v_hgrn_norm, v_ab_w_out, v_c_w_in, v_c_conv_w, v_c_conv_b, v_c_gate_a_w, v_c_gate_a_b, v_c_gate_x_w, v_c_gate_x_b, v_c_lambda, v_c_w_out, v_ffn_w_up, v_ffn_conv_w, v_ffn_conv_b, v_ffn_w_down)))
    big = list(BIG)
    chip = 2 * lax.axis_index("x") + lax.axis_index("y")
    ids = jnp.stack([chip, lax.axis_index("c")]).astype(jnp.int32)

    def layer_parts(l):
        j = l // 2
        if l % 2 == 0:
            mix = [("ab_w_in", j), ("ab_w_out", j)]
        else:
            mix = [("c_w_in", j), ("c_gate_a_w", j), ("c_gate_x_w", j), ("c_w_out", j)]
        return mix, [("ffn_w_up", l), ("ffn_w_down", l)]

    def layer_shard(n, i):
        s = w[n][i]
        return s.reshape(-1, s.shape[-1]).astype(bf16)

    small_shard_shapes = [w[n].shape for n in SMALL_SHARDED]
    small_pack = _pack_rows([w[n] for n in SMALL_SHARDED], 160)
    mix0, ffn0 = layer_parts(0)
    gathered0 = _all_gather_chips([layer_shard(n, i) for n, i in mix0] + [small_pack], "all_gather_mixer0")
    gathers = {}

    def start_gather(l, after):
        parts = ffn0 if l == 0 else sum(layer_parts(l), [])
        shards = [layer_shard(n, i) for n, i in parts]
        gathers[l] = _split_start(_gather_copies, shards, [_sds((N_CHIPS,) + s.shape, bf16) for s in shards],
                                  "gather_start_%d" % l, after)
        return gathers[l]["token"]

    first_tokens = (start_gather(0, ()), start_gather(1, ()))
    fw = {}
    per_chip = [_unpack_rows(gathered0[-1][k], small_shard_shapes) for k in range(N_CHIPS)]
    for i, n in enumerate(SMALL_SHARDED):
        fw[n] = jnp.concatenate([per_chip[k][i] for k in range(N_CHIPS)], axis=-1)
    for n in SMALL:
        if n not in fw:
            fw[n] = w[n]
    wt = _layout_weights(fw)
    for n in big:
        wt[n] = [None] * FULL_SHAPES[n][0]

    def pre_layer(l, part, x_l):
        deps = ()
        if l == 0 and part == "mix":
            parts, lands, deps = mix0, gathered0[:len(mix0)], first_tokens
        elif (l == 0) == (part == "ffn"):
            parts = ffn0 if l == 0 else sum(layer_parts(l), [])
            _, lands = _split_wait(_gather_copies, gathers[l], [x_l], "gather_wait_%d" % l)
            if l + 2 <= 3:
                deps = (start_gather(l + 2, lands[:1]),)
        else:
            return deps
        for (n, i), slots in zip(parts, lands):
            wt[n][i] = _layer_full(n, slots)
        return deps

    scatters = []

    def post_grads(l, part, g):
        parts = layer_parts(l)[0 if part == "mix" else 1]
        slots = [_layer_slots(n, g[n][i]) for n, i in parts]
        st = _split_start(_scatter_copies, slots, [_sds((N_REL, s.shape[1] // 2, s.shape[2]), bf16) for s in slots],
                          "scatter_start_%d_%s" % (l, part))
        scatters.append((parts, st, "scatter_wait_%d_%s" % (l, part)))
        return (st["token"],)

    loss, dx, g = _local_step(x[0], loss_target[0], wt, pre_layer, post_grads)
    gf = _unlayout_grads(g)
    loss = lax.psum(loss[0, 0], ("x", "y", "c"))

    out_g, out_d, out_m, out_v = {}, {}, {}, {}
    f = {n: None for n in big}
    last = [n for n, _ in layer_parts(0)[0]]

    def finish(group, after):
        for parts, st, wait_name in group:
            gbs, lands = _split_wait(_scatter_copies, st, after, wait_name)
            for (n, i), gb, land in zip(parts, gbs, lands):
                f[n] = _sum_pieces(gb, land, ids, f[n], i, FULL_SHAPES[n][0], "rs_sum")

    def adamw(names, tag):
        filled = _sibling_fill_blocks([f[n] for n in names], [FULL_SHAPES[n][0] for n in names], "rs_sibling_fill_" + tag)
        for n, g_n in zip(names, filled):
            shp = w[n].shape
            if n in NATIVE_PERM:
                to_native = lambda t: t.transpose(NATIVE_PERM[n])
                res = _adamw_lead(to_native(w[n]), to_native(g_n.reshape(shp)), to_native(m[n]), to_native(v[n]), "adamw_" + n)
                res = [t.transpose(1, 2, 0) for t in res]
            else:
                as3d = lambda t: t.reshape((-1,) + shp[-2:])
                res = _adamw_big(as3d(w[n]), g_n, as3d(m[n]), as3d(v[n]), "adamw_" + n)
            out_g[n], out_d[n], out_m[n], out_v[n] = (t.reshape(shp) for t in res)

    finish(scatters[:-1], [dx])
    adamw([n for n in big if n not in last], "a")
    finish(scatters[-1:], [out_v["ffn_w_up"]])
    adamw(last, "b")

    small_full_shapes = [FULL_SHAPES[n] for n in SMALL]
    small_sum = _all_reduce_small(_pack_rows([gf[n] for n in SMALL], 688), "all_reduce_small", [f[last[0]]])
    g_small = dict(zip(SMALL, _unpack_rows(small_sum, small_full_shapes)))
    for n in SMALL_SHARDED:
        width = w[n].shape[-1]
        g_small[n] = lax.dynamic_slice_in_dim(g_small[n], chip * width, width, axis=-1)

    def small2d(n, t):
        if n in NATIVE_PERM:
            t = t.transpose(NATIVE_PERM[n])
        return t.reshape(-1, t.shape[-1])

    sg, sd, sm, sv = _adamw_small([small2d(n, w[n]) for n in SMALL], [small2d(n, g_small[n]) for n in SMALL],
                                  [small2d(n, m[n]) for n in SMALL], [small2d(n, v[n]) for n in SMALL], "adamw_small")
    for i, n in enumerate(SMALL):
        for out, t in zip((out_g, out_d, out_m, out_v), (sg, sd, sm, sv)):
            if n in NATIVE_PERM:
                perm = NATIVE_PERM[n]
                shp_t = tuple(w[n].shape[p] for p in perm)
                out[n] = t[i].reshape(shp_t).transpose(tuple(perm.index(k) for k in range(len(perm))))
            else:
                out[n] = t[i].reshape(w[n].shape)
    return (loss, dx[None], *[out_g[n] for n in WEIGHTS], *[out_d[n] for n in WEIGHTS],
            *[out_m[n] for n in WEIGHTS], *[out_v[n] for n in WEIGHTS])
```

```python
import functools

import jax
import jax.numpy as jnp
from jax import lax
from jax.experimental import pallas as pl
from jax.experimental.pallas import tpu as pltpu

f32 = jnp.float32
bf16 = jnp.bfloat16

D = 1024
EPS = 1e-6
F_FLOOR = 1e-30
GDN_CHUNK = 64
GDN_INTRA_CHUNKS = 1
HGRN_CHUNK = 16
HGRN_STEP = 128
HEAD = 128
NH = 4
LRU_BLOCK = 256
D_FF = 2816
RG_C = 8.0
C_QKV, C_Z, C_QB, C_FB, C_IB, C_GB, C_BA = 0, 1536, 2048, 2560, 3072, 3584, 4096
TT = 256
N_SLOTS = 4
VMEM_LIMIT = 56 * 1024 * 1024

ADAM_LR, ADAM_B1, ADAM_B2, ADAM_EPS, ADAM_WD, ADAM_STEP = 0.001, 0.9, 0.999, 1e-08, 0.01, 10


def _cp(**kw):
    return pltpu.CompilerParams(vmem_limit_bytes=VMEM_LIMIT, **kw)


def _sds(shape, dtype):
    return jax.ShapeDtypeStruct(shape, dtype)


def _dot(a, b, dims, precision=None):
    return lax.dot_general(a, b, (dims, ((), ())), precision=precision, preferred_element_type=f32)


NN = ((1,), (0,))
NT = ((1,), (1,))
TN = ((0,), (0,))


def _rms(x, g):
    return x * lax.rsqrt(jnp.mean(x * x, axis=-1, keepdims=True) + EPS) * g


def _silu(x):
    return x * jax.nn.sigmoid(x)


def _mm(a, b, mode, tm, tn, out_dtype, name):
    if mode == "nn":
        (m, k), n = a.shape, b.shape[1]
        a_spec = pl.BlockSpec((tm, k), lambda i, j: (i, 0))
        b_spec = pl.BlockSpec((k, tn), lambda i, j: (0, j))
        dims = NN
    elif mode == "nt":
        (m, k), n = a.shape, b.shape[0]
        a_spec = pl.BlockSpec((tm, k), lambda i, j: (i, 0))
        b_spec = pl.BlockSpec((tn, k), lambda i, j: (j, 0))
        dims = NT
    else:
        (k, m), n = a.shape, b.shape[1]
        a_spec = pl.BlockSpec((k, tm), lambda i, j: (0, i))
        b_spec = pl.BlockSpec((k, tn), lambda i, j: (0, j))
        dims = TN
    assert m % tm == 0 and n % tn == 0, (name, m, n, tm, tn)

    def body(a_ref, b_ref, o_ref):
        o_ref[...] = _dot(a_ref[...], b_ref[...], dims).astype(out_dtype)

    return pl.pallas_call(
        body, grid=(m // tm, n // tn), in_specs=[a_spec, b_spec],
        out_specs=pl.BlockSpec((tm, tn), lambda i, j: (i, j)),
        out_shape=_sds((m, n), out_dtype), compiler_params=_cp(), name=name)(a, b)


def _mm_res_norm(a, b, res, gain, name, tm=512):
    (m, k), n = a.shape, b.shape[1]
    tm = min(tm, m)
    assert n == D and m % tm == 0, (name, m, n)

    def body(a_ref, b_ref, r_ref, *rest):
        xv = r_ref[...] + _dot(a_ref[...], b_ref[...], NN)
        if gain is None:
            rest[0][...] = xv
        else:
            g_ref, x_ref, h_ref = rest
            x_ref[...] = xv
            h_ref[...] = _rms(xv, g_ref[...]).astype(bf16)

    row = pl.BlockSpec((tm, D), lambda i: (i, 0))
    in_specs = [pl.BlockSpec((tm, k), lambda i: (i, 0)), pl.BlockSpec((k, D), lambda i: (0, 0)), row]
    args = [a, b, res]
    if gain is None:
        out_specs, out_shape = row, _sds((m, D), f32)
    else:
        in_specs.append(pl.BlockSpec((1, D), lambda i: (0, 0)))
        args.append(gain)
        out_specs, out_shape = [row, row], [_sds((m, D), f32), _sds((m, D), bf16)]
    return pl.pallas_call(body, grid=(m // tm,), in_specs=in_specs, out_specs=out_specs, out_shape=out_shape,
                          compiler_params=_cp(), name=name)(*args)


def _tile(n, cands):
    for c in cands:
        if n % c == 0:
            return c
    raise ValueError(n)


def _mm_auto(a, b, mode, out_dtype, name):
    m = a.shape[1] if mode == "tn" else a.shape[0]
    n = b.shape[0] if mode == "nt" else b.shape[1]
    return _mm(a, b, mode, _tile(m, (512, 256, 128)), _tile(n, (1024, 1408, 512, 384, 256, 128)), out_dtype, name)


def _mm_nn_slots(a, bs, out_dtype, name, after=()):
    (m, k), w = a.shape, bs.shape[2]
    tm = _tile(m, (1024, 512, 256, 128))

    def body(a_ref, b_ref, *rest):
        o_ref = rest[len(after)]
        o_ref[...] = _dot(a_ref[...], b_ref[...], NN).astype(out_dtype)

    return pl.pallas_call(
        body, grid=(m // tm, N_SLOTS),
        in_specs=[pl.BlockSpec((tm, k), lambda i, j: (i, 0)), pl.BlockSpec((None, k, w), lambda i, j: (j, 0, 0))]
        + [pl.BlockSpec(memory_space=pl.ANY)] * len(after),
        out_specs=pl.BlockSpec((tm, w), lambda i, j: (i, j)),
        out_shape=_sds((m, N_SLOTS * w), out_dtype), compiler_params=_cp(), name=name)(a, bs, *after)


def _mm_nt_rmsbwd(a, b, x, gain, dres, name, after=()):
    m, k = a.shape
    slots = b.ndim == 3
    assert b.shape[-2] == D
    tm = _tile(m, (512, 256, 128)) if k < 4096 else _tile(m, (256, 128))

    def body(a_ref, b_ref, x_ref, g_ref, r_ref, *rest):
        dx_ref, dxb_ref, dg_ref = rest[len(after):]
        if slots:
            w = b.shape[2]
            dh = _dot(a_ref[:, 0:w], b_ref[0], NT)
            for s in range(1, N_SLOTS):
                dh = dh + _dot(a_ref[:, s * w:(s + 1) * w], b_ref[s], NT)
        else:
            dh = _dot(a_ref[...], b_ref[...], NT)
        _, vjp = jax.vjp(_rms, x_ref[...], g_ref[...])
        dx, dg = vjp(dh)
        dx = dx + r_ref[...]
        dx_ref[...] = dx
        dxb_ref[...] = dx.astype(bf16)

        @pl.when(pl.program_id(0) == 0)
        def _():
            dg_ref[...] = jnp.zeros_like(dg_ref)

        dg_ref[...] += dg

    row = pl.BlockSpec((tm, D), lambda i: (i, 0))
    vec = pl.BlockSpec((1, D), lambda i: (0, 0))
    b_spec = pl.BlockSpec(b.shape, (lambda i: (0, 0, 0)) if slots else (lambda i: (0, 0)))
    return pl.pallas_call(
        body, grid=(m // tm,),
        in_specs=[pl.BlockSpec((tm, k), lambda i: (i, 0)), b_spec, row, vec, row] + [pl.BlockSpec(memory_space=pl.ANY)] * len(after),
        out_specs=[row, row, vec],
        out_shape=[_sds((m, D), f32), _sds((m, D), bf16), _sds((1, D), f32)],
        compiler_params=_cp(), name=name)(a, b, x, gain, dres, *after)


def _mm_tn_slots(a, b, name):
    (k, m), w = a.shape, b.shape[1] // N_SLOTS
    tm = _tile(m, (512, 256, 128))

    def body(a_ref, b_ref, o_ref):
        o_ref[...] = _dot(a_ref[...], b_ref[...], TN).astype(bf16)

    return pl.pallas_call(
        body, grid=(N_SLOTS, m // tm),
        in_specs=[pl.BlockSpec((k, tm), lambda j, i: (0, i)), pl.BlockSpec((k, w), lambda j, i: (0, j))],
        out_specs=pl.BlockSpec((None, tm, w), lambda j, i: (j, i, 0)),
        out_shape=_sds((N_SLOTS, m, w), bf16), compiler_params=_cp(), name=name)(a, b)


def _rmsnorm_fwd(x, gain, name, after=()):
    t = x.shape[0]

    def body(x_ref, g_ref, *rest):
        h_ref = rest[len(after)]
        h_ref[...] = _rms(x_ref[...], g_ref[...]).astype(bf16)

    return pl.pallas_call(
        body, grid=(t // TT,),
        in_specs=[pl.BlockSpec((TT, D), lambda i: (i, 0)), pl.BlockSpec((1, D), lambda i: (0, 0))]
        + [pl.BlockSpec(memory_space=pl.ANY)] * len(after),
        out_specs=pl.BlockSpec((TT, D), lambda i: (i, 0)),
        out_shape=_sds((t, D), bf16), compiler_params=_cp(), name=name)(x, gain, *after)


def _final_loss(x, gain, target, name):
    t = x.shape[0]

    def loss_fn(xv, gv, tv):
        e = _rms(xv, gv) - tv
        return 0.5 * jnp.sum(jnp.mean(e * e, axis=-1))

    def body(x_ref, g_ref, t_ref, loss_ref, dx_ref, dxb_ref, dg_ref):
        val, (dx, dg) = jax.value_and_grad(loss_fn, argnums=(0, 1))(x_ref[...], g_ref[...], t_ref[...])

        @pl.when(pl.program_id(0) == 0)
        def _():
            dg_ref[...] = jnp.zeros_like(dg_ref)
            loss_ref[...] = jnp.zeros_like(loss_ref)

        dx_ref[...] = dx
        dxb_ref[...] = dx.astype(bf16)
        dg_ref[...] += dg
        loss_ref[...] += jnp.full((1, 128), val, f32)

    row = pl.BlockSpec((TT, D), lambda i: (i, 0))
    vec = pl.BlockSpec((1, D), lambda i: (0, 0))
    return pl.pallas_call(
        body, grid=(t // TT,), in_specs=[row, vec, row],
        out_specs=[pl.BlockSpec((1, 128), lambda i: (0, 0)), row, row, vec],
        out_shape=[_sds((1, 128), f32), _sds((t, D), f32), _sds((t, D), bf16), _sds((1, D), f32)],
        compiler_params=_cp(), name=name)(x, gain, target)


def _halo_rows(dtype):
    return 16 if dtype == bf16 else 8


def _conv_fwd(x, col0, c, w, b, name, tc=256, val=None, val_col0=0):
    t = x.shape[0]
    width = w.shape[0]
    nt = t // TT
    hr = _halo_rows(x.dtype)
    hb = TT // hr

    def body(*refs):
        if val is None:
            x_ref, xh_ref, w_ref, b_ref, o_ref, xp = refs
        else:
            x_ref, xh_ref, w_ref, b_ref, v_ref, o_ref, act_ref, xp = refs
        i = pl.program_id(1)
        xp[0:8, :] = jnp.where(i == 0, 0.0, xh_ref[hr - 8:hr, :].astype(f32))
        xp[8:, :] = x_ref[...].astype(f32)
        acc = jnp.zeros((TT, tc), f32) + b_ref[...]
        for k in range(width):
            acc = acc + w_ref[k:k + 1, :] * xp[pl.ds(8 - (width - 1) + k, TT), :]
        o_ref[...] = acc
        if val is not None:
            act_ref[...] = (_silu(acc) * v_ref[...]).astype(bf16)

    in_specs = [
        pl.BlockSpec((TT, tc), lambda j, i: (i, j + col0)),
        pl.BlockSpec((hr, tc), lambda j, i: (jnp.maximum(i * hb - 1, 0), j + col0)),
        pl.BlockSpec((width, tc), lambda j, i: (0, j)),
        pl.BlockSpec((1, tc), lambda j, i: (0, j)),
    ]
    args = [x, x, w, b]
    out_specs = [pl.BlockSpec((TT, tc), lambda j, i: (i, j))]
    out_shape = [_sds((t, c), f32)]
    if val is not None:
        in_specs.append(pl.BlockSpec((TT, tc), lambda j, i: (i, j + val_col0)))
        args.append(val)
        out_specs.append(pl.BlockSpec((TT, tc), lambda j, i: (i, j)))
        out_shape.append(_sds((t, c), bf16))
    res = pl.pallas_call(
        body, grid=(c // tc, nt), in_specs=in_specs, out_specs=out_specs, out_shape=out_shape,
        scratch_shapes=[pltpu.VMEM((TT + 8, tc), f32)], compiler_params=_cp(), name=name)(*args)
    return res[0] if val is None else res


def _conv_bwd(dc, x, col0, w, name, tc=256, dx_dtype=bf16, into=None):
    t, c = dc.shape
    width = w.shape[0]
    nt = t // TT
    hb = TT // 8
    hr = _halo_rows(x.dtype)

    def body(dc_ref, dcn_ref, x_ref, xh_ref, w_ref, *rest):
        dx_ref, dw_ref, db_ref, dcp, xp = rest[(0 if into is None else 1):]
        i = pl.program_id(1)
        dcv = dc_ref[...]
        dcp[0:TT, :] = dcv
        dcp[TT:, :] = jnp.where(i == nt - 1, 0.0, dcn_ref[...])
        xp[0:8, :] = jnp.where(i == 0, 0.0, xh_ref[hr - 8:hr, :].astype(f32))
        xp[8:, :] = x_ref[...].astype(f32)

        @pl.when(i == 0)
        def _():
            dw_ref[...] = jnp.zeros_like(dw_ref)
            db_ref[...] = jnp.zeros_like(db_ref)

        acc = jnp.zeros((TT, tc), f32)
        for k in range(width):
            acc = acc + w_ref[k:k + 1, :] * dcp[pl.ds((width - 1) - k, TT), :]
            dw_ref[k:k + 1, :] += jnp.sum(dcv * xp[pl.ds(8 - (width - 1) + k, TT), :], axis=0, keepdims=True)
        dx_ref[...] = acc.astype(dx_dtype)
        db_ref[...] += jnp.sum(dcv, axis=0, keepdims=True)

    in_specs = [
        pl.BlockSpec((TT, tc), lambda j, i: (i, j)),
        pl.BlockSpec((8, tc), lambda j, i: (jnp.minimum((i + 1) * hb, t // 8 - 1), j)),
        pl.BlockSpec((TT, tc), lambda j, i: (i, j + col0)),
        pl.BlockSpec((hr, tc), lambda j, i: (jnp.maximum(i * (TT // hr) - 1, 0), j + col0)),
        pl.BlockSpec((width, tc), lambda j, i: (0, j)),
    ]
    out_specs = [
        pl.BlockSpec((TT, tc), lambda j, i: (i, j)),
        pl.BlockSpec((8, tc), lambda j, i: (0, j)),
        pl.BlockSpec((1, tc), lambda j, i: (0, j)),
    ]
    args, aliases, dx_shape = [dc, dc, x, x, w], {}, _sds((t, c), dx_dtype)
    if into is not None:
        in_specs.append(pl.BlockSpec(memory_space=pl.ANY))
        args.append(into)
        aliases, dx_shape = {5: 0}, _sds(into.shape, into.dtype)
    return pl.pallas_call(
        body, grid=(c // tc, nt), in_specs=in_specs, out_specs=out_specs,
        out_shape=[dx_shape, _sds((8, c), f32), _sds((1, c), f32)], input_output_aliases=aliases,
        scratch_shapes=[pltpu.VMEM((TT + 8, tc), f32), pltpu.VMEM((TT + 8, tc), f32)],
        compiler_params=_cp(), name=name)(*args)


def _bdot_impl(a, b, dims):
    return _dot(a.astype(bf16), b.astype(bf16), dims)


@functools.partial(jax.custom_vjp, nondiff_argnums=(2,))
def _bdot(a, b, dims):
    return _bdot_impl(a, b, dims)


def _bdot_fwd(a, b, dims):
    return _bdot_impl(a, b, dims), (a, b)


def _bdot_bwd(dims, res, ct):
    a, b = res
    if dims == NN:
        return _bdot_impl(ct, b, NT), _bdot_impl(a, ct, TN)
    if dims == NT:
        return _bdot_impl(ct, b, NN), _bdot_impl(ct, a, TN)
    return _bdot_impl(b, ct, NT), _bdot_impl(a, ct, NN)


_bdot.defvjp(_bdot_fwd, _bdot_bwd)


def _split2(a):
    hi = a.astype(bf16)
    return hi, (a - hi.astype(f32)).astype(bf16)


def _dot3_impl(a, b, dims):
    a_hi, a_lo = _split2(a)
    b_hi, b_lo = _split2(b)
    return (_dot(a_hi, b_hi, dims) + _dot(a_hi, b_lo, dims)) + _dot(a_lo, b_hi, dims)


@functools.partial(jax.custom_vjp, nondiff_argnums=(2,))
def _dot3(a, b, dims):
    return _dot3_impl(a, b, dims)


def _dot3_fwd(a, b, dims):
    return _dot3_impl(a, b, dims), (a, b)


def _dot3_bwd(dims, res, ct):
    a, b = res
    if dims == NN:
        return _dot3_impl(ct, b, NT), _dot3_impl(a, ct, TN)
    if dims == NT:
        return _dot3_impl(ct, b, NN), _dot3_impl(ct, a, TN)
    return _dot3_impl(b, ct, NT), _dot3_impl(a, ct, NN)


_dot3.defvjp(_dot3_fwd, _dot3_bwd)


def _tril_dot_impl(tril, x, dims):
    t = tril.astype(bf16)
    x1 = x.astype(bf16)
    r1 = x - x1.astype(f32)
    x2 = r1.astype(bf16)
    x3 = (r1 - x2.astype(f32)).astype(bf16)
    return (_dot(t, x3, dims) + _dot(t, x2, dims)) + _dot(t, x1, dims)


@jax.custom_vjp
def _cumsum_rows(tril, x):
    return _tril_dot_impl(tril, x, NN)


def _cumsum_rows_fwd(tril, x):
    return _tril_dot_impl(tril, x, NN), tril


def _cumsum_rows_bwd(tril, ct):
    return jnp.zeros_like(tril), _tril_dot_impl(tril, ct, TN)


_cumsum_rows.defvjp(_cumsum_rows_fwd, _cumsum_rows_bwd)


def _gdn_intra4(cq, ck, cv, ba, alog_v, dtb_v):
    c = GDN_CHUNK
    hs = range(NH)
    lane = lax.broadcasted_iota(jnp.int32, (1, HEAD), 1)
    mb = [(lane == h).astype(f32) for h in hs]
    ma = [(lane == h + NH).astype(f32) for h in hs]
    beta = [jax.nn.sigmoid(jnp.sum(ba * mb[h], axis=1, keepdims=True)) for h in hs]
    alpha = [jnp.sum(ba * ma[h], axis=1, keepdims=True) for h in hs]
    alog = [jnp.sum(alog_v * ma[h], axis=1, keepdims=True) for h in hs]
    dtb = [jnp.sum(dtb_v * ma[h], axis=1, keepdims=True) for h in hs]
    g = [-jnp.exp(alog[h]) * jax.nn.softplus(alpha[h] + dtb[h]) for h in hs]
    q = [_silu(cq[h]) for h in hs]
    q = [q[h] * lax.rsqrt(jnp.sum(q[h] * q[h], axis=-1, keepdims=True) + EPS) * (HEAD ** -0.5) for h in hs]
    k = [_silu(ck[h]) for h in hs]
    k = [k[h] * lax.rsqrt(jnp.sum(k[h] * k[h], axis=-1, keepdims=True) + EPS) for h in hs]
    v = [_silu(cv[h]) for h in hs]
    row = lax.broadcasted_iota(jnp.int32, (c, c), 0)
    col = lax.broadcasted_iota(jnp.int32, (c, c), 1)
    causal = row >= col
    tril = causal.astype(f32)
    gc = [_cumsum_rows(tril, jnp.broadcast_to(g[h], (c, HEAD))) for h in hs]
    gcc = [_cumsum_rows(tril, jnp.broadcast_to(g[h], (c, c))) for h in hs]
    decay = [jnp.where(causal, jnp.exp(jnp.where(causal, gcc[h] - gcc[h].T, 0.0)), 0.0) for h in hs]
    kb = [k[h] * beta[h] for h in hs]
    kk = [_bdot(kb[h], k[h], NT) for h in hs]
    p = [-jnp.where(row > col, kk[h] * decay[h], 0.0) for h in hs]
    egc = [jnp.exp(gc[h]) for h in hs]
    sol = [jnp.concatenate([v[h] * beta[h], kb[h] * egc[h]], axis=1) for h in hs]
    for it in range(6):
        upd = [_dot3(p[h], sol[h], NN) for h in hs]
        sol = [sol[h] + upd[h] for h in hs]
        if it < 5:
            p = [_dot3(p[h], p[h], NN) for h in hs]
    qk = [_bdot(q[h], k[h], NT) for h in hs]
    attn = [qk[h] * decay[h] for h in hs]
    rowv = lax.broadcasted_iota(jnp.int32, (c, 1), 0)
    gc_last = [jnp.sum(jnp.where(rowv == c - 1, gc[h], 0.0), axis=0, keepdims=True) for h in hs]
    return ([sol[h][:, :HEAD] for h in hs], [sol[h][:, HEAD:] for h in hs], [q[h] * egc[h] for h in hs],
            [k[h] * jnp.exp(gc_last[h] - gc[h]) for h in hs], attn, [jnp.exp(gc_last[h]) for h in hs])


def _gdn_seq4(u, w, q_dec, k_dec, attn, dl, z, s, gain):
    hs = range(NH)
    ws = [_bdot(w[h], s[h], NN) for h in hs]
    qs = [_bdot(q_dec[h], s[h], NN) for h in hs]
    v_new = [u[h] - ws[h] for h in hs]
    av = [_bdot(attn[h], v_new[h], NN) for h in hs]
    kv = [_bdot(k_dec[h], v_new[h], TN) for h in hs]
    o = [_rms(qs[h] + av[h], gain) * _silu(z[h]) for h in hs]
    return o, [s[h] * dl[h] + kv[h] for h in hs]


def _hsl(h):
    return slice(h * HEAD, (h + 1) * HEAD)


def _gdn2_fwd(conv, p, alog_v, dtb_v, gain, name):
    t = conv.shape[0]
    c = GDN_CHUNK
    nch = t // c
    w512 = NH * HEAD
    wide = lambda off: pl.BlockSpec((c, w512), lambda n: (n, off))
    vec = pl.BlockSpec((1, HEAD), lambda n: (0, 0))
    attn_spec = pl.BlockSpec((1, NH, c, c), lambda n: (n, 0, 0, 0))
    dl_spec = pl.BlockSpec((1, NH, HEAD), lambda n: (n, 0, 0))

    cps = GDN_INTRA_CHUNKS
    iwide = lambda off: pl.BlockSpec((cps * c, w512), lambda n: (n, off))

    def intra(cq, ck, cv, ba, al, dt, u_ref, w_ref, qd_ref, kd_ref, at_ref, dl_ref):
        for ci in range(cps):
            rows = slice(ci * c, (ci + 1) * c)
            u, w, qd, kd, at, dl = _gdn_intra4([cq[rows, _hsl(h)] for h in range(NH)], [ck[rows, _hsl(h)] for h in range(NH)],
                                               [cv[rows, _hsl(h)] for h in range(NH)], ba[rows, :], al[...], dt[...])
            for h in range(NH):
                u_ref[rows, _hsl(h)] = u[h]
                w_ref[rows, _hsl(h)] = w[h]
                qd_ref[rows, _hsl(h)] = qd[h]
                kd_ref[rows, _hsl(h)] = kd[h]
                at_ref[ci, h] = at[h]
                dl_ref[ci, h:h + 1, :] = dl[h]

    u, w, qd, kd, at, dl = pl.pallas_call(
        intra, grid=(nch // cps,),
        in_specs=[iwide(0), iwide(1), iwide(2), pl.BlockSpec((cps * c, HEAD), lambda n: (n, C_BA // HEAD)), vec, vec],
        out_specs=[iwide(0)] * 4 + [pl.BlockSpec((cps, NH, c, c), lambda n: (n, 0, 0, 0)),
                                    pl.BlockSpec((cps, NH, HEAD), lambda n: (n, 0, 0))],
        out_shape=[_sds((t, w512), f32)] * 4 + [_sds((nch, NH, c, c), f32), _sds((nch, NH, HEAD), f32)],
        compiler_params=_cp(), name=name + "_intra")(conv, conv, conv, p, alog_v, dtb_v)

    def seq(u_ref, w_ref, qd_ref, kd_ref, at_ref, dl_ref, z_ref, gn, o_ref, ss_ref, s_scr):
        @pl.when(pl.program_id(0) == 0)
        def _():
            s_scr[...] = jnp.zeros_like(s_scr)

        hs = range(NH)
        s = [s_scr[h] for h in hs]
        for h in hs:
            ss_ref[0, h] = s[h]
        o, s_new = _gdn_seq4([u_ref[:, _hsl(h)] for h in hs], [w_ref[:, _hsl(h)] for h in hs], [qd_ref[:, _hsl(h)] for h in hs],
                             [kd_ref[:, _hsl(h)] for h in hs], [at_ref[0, h] for h in hs], [dl_ref[0, h:h + 1, :] for h in hs],
                             [z_ref[:, _hsl(h)] for h in hs], s, gn[...])
        for h in hs:
            o_ref[:, _hsl(h)] = o[h].astype(bf16)
            s_scr[h] = s_new[h]

    o, states = pl.pallas_call(
        seq, grid=(nch,),
        in_specs=[wide(0)] * 4 + [attn_spec, dl_spec, wide(C_Z // w512), vec],
        out_specs=[wide(0), pl.BlockSpec((1, NH, HEAD, HEAD), lambda n: (n, 0, 0, 0))],
        out_shape=[_sds((t, w512), bf16), _sds((nch, NH, HEAD, HEAD), f32)],
        scratch_shapes=[pltpu.VMEM((NH, HEAD, HEAD), f32)],
        compiler_params=_cp(), name=name + "_seq")(u, w, qd, kd, at, dl, p, gain)
    return o, dict(u=u, w=w, qd=qd, kd=kd, at=at, dl=dl, states=states)


def _gdn2_bwd(conv, p, alog_v, dtb_v, gain, saved, do, name):
    t = conv.shape[0]
    c = GDN_CHUNK
    nch = t // c
    w512 = NH * HEAD
    rwide = lambda off: pl.BlockSpec((c, w512), lambda n: (nch - 1 - n, off))
    rvec = pl.BlockSpec((1, HEAD), lambda n: (0, 0))
    rattn = pl.BlockSpec((1, NH, c, c), lambda n: (nch - 1 - n, 0, 0, 0))
    rdl = pl.BlockSpec((1, NH, HEAD), lambda n: (nch - 1 - n, 0, 0))

    def seq_bwd(u_ref, w_ref, qd_ref, kd_ref, at_ref, dl_ref, z_ref, gn, ss_ref, do_ref,
                du_ref, dw_ref, dqd_ref, dkd_ref, dat_ref, ddl_ref, dz_ref, dgn_ref, ds_scr):
        @pl.when(pl.program_id(0) == 0)
        def _():
            ds_scr[...] = jnp.zeros_like(ds_scr)
            dgn_ref[...] = jnp.zeros_like(dgn_ref)

        hs = range(NH)
        _, vjp = jax.vjp(_gdn_seq4, [u_ref[:, _hsl(h)] for h in hs], [w_ref[:, _hsl(h)] for h in hs],
                         [qd_ref[:, _hsl(h)] for h in hs], [kd_ref[:, _hsl(h)] for h in hs], [at_ref[0, h] for h in hs],
                         [dl_ref[0, h:h + 1, :] for h in hs], [z_ref[:, _hsl(h)] for h in hs], [ss_ref[0, h] for h in hs], gn[...])
        du, dw, dqd, dkd, dat, ddl, dz, ds, dg = vjp(([do_ref[:, _hsl(h)] for h in hs], [ds_scr[h] for h in hs]))
        for h in hs:
            du_ref[:, _hsl(h)] = du[h]
            dw_ref[:, _hsl(h)] = dw[h]
            dqd_ref[:, _hsl(h)] = dqd[h]
            dkd_ref[:, _hsl(h)] = dkd[h]
            dat_ref[0, h] = dat[h]
            ddl_ref[0, h:h + 1, :] = ddl[h]
            dz_ref[:, _hsl(h)] = dz[h]
            ds_scr[h] = ds[h]
        dgn_ref[...] += dg

    du, dw, dqd, dkd, dat, ddl, dz, dgn = pl.pallas_call(
        seq_bwd, grid=(nch,),
        in_specs=[rwide(0)] * 4 + [rattn, rdl, rwide(C_Z // w512), rvec,
                                   pl.BlockSpec((1, NH, HEAD, HEAD), lambda n: (nch - 1 - n, 0, 0, 0)), rwide(0)],
        out_specs=[rwide(0)] * 4 + [rattn, rdl, rwide(0), rvec],
        out_shape=[_sds((t, w512), f32)] * 4 + [_sds((nch, NH, c, c), f32), _sds((nch, NH, HEAD), f32),
                                                _sds((t, w512), f32), _sds((1, HEAD), f32)],
        scratch_shapes=[pltpu.VMEM((NH, HEAD, HEAD), f32)],
        compiler_params=_cp(), name=name + "_seq")(
            saved["u"], saved["w"], saved["qd"], saved["kd"], saved["at"], saved["dl"], p, gain, saved["states"], do)

    cps = GDN_INTRA_CHUNKS
    wide = lambda off: pl.BlockSpec((cps * c, w512), lambda n: (n, off))
    vec = pl.BlockSpec((1, HEAD), lambda n: (0, 0))
    attn_spec = pl.BlockSpec((cps, NH, c, c), lambda n: (n, 0, 0, 0))
    dl_spec = pl.BlockSpec((cps, NH, HEAD), lambda n: (n, 0, 0))

    def intra_bwd(cq, ck, cv, ba, al, dt, du_ref, dw_ref, dqd_ref, dkd_ref, dat_ref, ddl_ref,
                  dc_ref, dba_ref, dal_ref, ddt_ref):
        @pl.when(pl.program_id(0) == 0)
        def _():
            dal_ref[...] = jnp.zeros_like(dal_ref)
            ddt_ref[...] = jnp.zeros_like(ddt_ref)

        dal = jnp.zeros((1, HEAD), f32)
        ddt = jnp.zeros((1, HEAD), f32)
        for ci in range(cps):
            rows = slice(ci * c, (ci + 1) * c)
            hs = range(NH)
            _, vjp = jax.vjp(_gdn_intra4, [cq[rows, _hsl(h)] for h in hs], [ck[rows, _hsl(h)] for h in hs],
                             [cv[rows, _hsl(h)] for h in hs], ba[rows, :], al[...], dt[...])
            g_q, g_k, g_v, g_ba, g_al, g_dt = vjp((
                [du_ref[rows, _hsl(h)] for h in hs], [dw_ref[rows, _hsl(h)] for h in hs], [dqd_ref[rows, _hsl(h)] for h in hs],
                [dkd_ref[rows, _hsl(h)] for h in hs], [dat_ref[ci, h] for h in hs], [ddl_ref[ci, h:h + 1, :] for h in hs]))
            for h in hs:
                dc_ref[rows, _hsl(h)] = g_q[h]
                dc_ref[rows, _hsl(NH + h)] = g_k[h]
                dc_ref[rows, _hsl(2 * NH + h)] = g_v[h]
            dal = dal + g_al
            ddt = ddt + g_dt
            dba_ref[rows, :] = g_ba
        dal_ref[...] += dal
        ddt_ref[...] += ddt

    dconv, dba, dal, ddt = pl.pallas_call(
        intra_bwd, grid=(nch // cps,),
        in_specs=[wide(0), wide(1), wide(2), pl.BlockSpec((cps * c, HEAD), lambda n: (n, C_BA // HEAD)), vec, vec]
        + [wide(0)] * 4 + [attn_spec, dl_spec],
        out_specs=[pl.BlockSpec((cps * c, 3 * w512), lambda n: (n, 0)), pl.BlockSpec((cps * c, HEAD), lambda n: (n, 0)),
                   vec, vec],
        out_shape=[_sds((t, 3 * w512), f32), _sds((t, HEAD), f32), _sds((1, HEAD), f32), _sds((1, HEAD), f32)],
        compiler_params=_cp(), name=name + "_intra")(conv, conv, conv, p, alog_v, dtb_v, du, dw, dqd, dkd, dat, ddl)
    return dconv, dz, dba, dal, ddt, dgn


def _hgrn_intra(qb, fb, ib, lb):
    c = HGRN_CHUNK
    ns = range(len(qb))
    f = [lb + (1.0 - lb) * jax.nn.sigmoid(fb[i]) for i in ns]
    logf = [jnp.log(jnp.maximum(f[i], F_FLOOR)) for i in ns]
    k = [1.0 - f[i] for i in ns]
    q = [_silu(qb[i]) for i in ns]
    row = lax.broadcasted_iota(jnp.int32, (c, c), 0)
    col = lax.broadcasted_iota(jnp.int32, (c, c), 1)
    tril = (row >= col).astype(f32)
    b = [_cumsum_rows(tril, logf[i]) for i in ns]
    ri = lax.broadcasted_iota(jnp.int32, (c, 1), 0)
    o = [jnp.zeros((c, HEAD), f32) for _ in ns]
    for j in range(c):
        mj = ri == j
        ok = ri >= j
        bj = [jnp.sum(jnp.where(mj, b[i], 0.0), axis=0, keepdims=True) for i in ns]
        kj = [jnp.sum(jnp.where(mj, k[i], 0.0), axis=0, keepdims=True) for i in ns]
        vj = [jnp.sum(jnp.where(mj, ib[i], 0.0), axis=0, keepdims=True) for i in ns]
        e = [jnp.where(ok, jnp.exp(jnp.where(ok, b[i] - bj[i], 0.0)), 0.0) for i in ns]
        s = [jnp.sum(q[i] * kj[i] * e[i], axis=1, keepdims=True) for i in ns]
        o = [o[i] + s[i] * vj[i] for i in ns]
    b_last = [jnp.sum(jnp.where(ri == c - 1, b[i], 0.0), axis=0, keepdims=True) for i in ns]
    return (o, [q[i] * jnp.exp(b[i]) for i in ns], [k[i] * jnp.exp(b_last[i] - b[i]) for i in ns],
            [jnp.exp(b_last[i]) for i in ns])


def _hgrn_seq_heads(o_intra, q_dec, k_dec, dl, v, gb, st, gain):
    hs = range(len(st))
    qs = [_bdot(q_dec[h], st[h], NT) for h in hs]
    vk = [_bdot(v[h], k_dec[h], TN) for h in hs]
    o = [_rms(o_intra[h] + qs[h], gain) * _silu(gb[h]) for h in hs]
    return o, [st[h] * dl[h] + vk[h] for h in hs]


def _hgrn_tiles(ref, rows):
    return [[ref[rw, _hsl(h)] for rw in rows] for h in range(NH)]


def _hgrn_fwd(p, lb, gain, name):
    t = p.shape[0]
    r = HGRN_STEP
    ns = t // r
    nsub = r // HGRN_CHUNK
    w512 = NH * HEAD
    blk = lambda off: pl.BlockSpec((r, w512), lambda n: (n, off // w512))

    def body(qb, fb, ib, gb, lb_ref, gn, o_ref, ss_ref, s_scr):
        @pl.when(pl.program_id(0) == 0)
        def _():
            s_scr[...] = jnp.zeros_like(s_scr)

        hs = range(NH)
        rows = [pl.ds(ch * HGRN_CHUNK, HGRN_CHUNK) for ch in range(nsub)]
        q_t, f_t, v_t, g_t = (_hgrn_tiles(ref, rows) for ref in (qb, fb, ib, gb))
        intra = [_hgrn_intra(q_t[h], f_t[h], v_t[h], lb_ref[:, _hsl(h)]) for h in hs]
        st = [s_scr[h] for h in hs]
        for h in hs:
            ss_ref[0, h] = st[h]
        for ch in range(nsub):
            o, st = _hgrn_seq_heads([intra[h][0][ch] for h in hs], [intra[h][1][ch] for h in hs], [intra[h][2][ch] for h in hs],
                                    [intra[h][3][ch] for h in hs], [v_t[h][ch] for h in hs], [g_t[h][ch] for h in hs], st, gn[...])
            for h in hs:
                o_ref[rows[ch], _hsl(h)] = o[h].astype(bf16)
        for h in hs:
            s_scr[h] = st[h]

    return pl.pallas_call(
        body, grid=(ns,),
        in_specs=[blk(C_QB), blk(C_FB), blk(C_IB), blk(C_GB),
                  pl.BlockSpec((1, w512), lambda n: (0, 0)), pl.BlockSpec((1, HEAD), lambda n: (0, 0))],
        out_specs=[pl.BlockSpec((r, w512), lambda n: (n, 0)),
                   pl.BlockSpec((1, NH, HEAD, HEAD), lambda n: (n, 0, 0, 0))],
        out_shape=[_sds((t, w512), bf16), _sds((ns, NH, HEAD, HEAD), f32)],
        scratch_shapes=[pltpu.VMEM((NH, HEAD, HEAD), f32)],
        compiler_params=_cp(), name=name)(p, p, p, p, lb, gain)


def _hgrn_bwd(p, lb, gain, states, do, name):
    t = p.shape[0]
    r = HGRN_STEP
    ns = t // r
    nsub = r // HGRN_CHUNK
    w512 = NH * HEAD
    blk = lambda off: pl.BlockSpec((r, w512), lambda n: (ns - 1 - n, off // w512))

    def body(qb, fb, ib, gb, lb_ref, gn, ss_ref, do_ref, dp_ref, dlb_ref, dgn_ref, ds_scr, st_scr):
        @pl.when(pl.program_id(0) == 0)
        def _():
            ds_scr[...] = jnp.zeros_like(ds_scr)
            dgn_ref[...] = jnp.zeros_like(dgn_ref)

        hs = range(NH)
        gnv = gn[...]
        rows = [pl.ds(ch * HGRN_CHUNK, HGRN_CHUNK) for ch in range(nsub)]
        q_t, f_t, v_t, g_t = (_hgrn_tiles(ref, rows) for ref in (qb, fb, ib, gb))
        intra, vjp_intra = [], []
        for h in hs:
            out, fn = jax.vjp(_hgrn_intra, q_t[h], f_t[h], v_t[h], lb_ref[:, _hsl(h)])
            intra.append(out)
            vjp_intra.append(fn)
        at = lambda part, ch: [intra[h][part][ch] for h in hs]
        st = [ss_ref[0, h] for h in hs]
        for ch in range(nsub):
            for h in hs:
                st_scr[ch * NH + h] = st[h]
            if ch < nsub - 1:
                vk = [_bdot(v_t[h][ch], intra[h][2][ch], TN) for h in hs]
                st = [st[h] * intra[h][3][ch] + vk[h] for h in hs]
        ds = [ds_scr[h] for h in hs]
        dgn = jnp.zeros((1, HEAD), f32)
        cots = [[[None] * nsub for _ in range(4)] for _ in hs]
        d_v = [[None] * nsub for _ in hs]
        for ch in reversed(range(nsub)):
            _, vjp = jax.vjp(_hgrn_seq_heads, at(0, ch), at(1, ch), at(2, ch), at(3, ch), [v_t[h][ch] for h in hs],
                             [g_t[h][ch] for h in hs], [st_scr[ch * NH + h] for h in hs], gnv)
            d_oi, d_qd, d_kd, d_dl, d_vv, g_g, ds, g_gn = vjp(([do_ref[rows[ch], _hsl(h)] for h in hs], ds))
            for h in hs:
                cots[h][0][ch], cots[h][1][ch], cots[h][2][ch], cots[h][3][ch] = d_oi[h], d_qd[h], d_kd[h], d_dl[h]
                d_v[h][ch] = d_vv[h]
                dp_ref[rows[ch], pl.ds(3 * w512 + h * HEAD, HEAD)] = g_g[h].astype(bf16)
            dgn = dgn + g_gn
        for h in hs:
            g_q, g_f, g_i, g_lb = vjp_intra[h](tuple(cots[h]))
            for ch in range(nsub):
                dp_ref[rows[ch], pl.ds(h * HEAD, HEAD)] = g_q[ch].astype(bf16)
                dp_ref[rows[ch], pl.ds(w512 + h * HEAD, HEAD)] = g_f[ch].astype(bf16)
                dp_ref[rows[ch], pl.ds(2 * w512 + h * HEAD, HEAD)] = (g_i[ch] + d_v[h][ch]).astype(bf16)
            dlb_ref[0, :, _hsl(h)] = g_lb
            ds_scr[h] = ds[h]
        dgn_ref[...] += dgn

    return pl.pallas_call(
        body, grid=(ns,),
        in_specs=[blk(C_QB), blk(C_FB), blk(C_IB), blk(C_GB),
                  pl.BlockSpec((1, w512), lambda n: (0, 0)), pl.BlockSpec((1, HEAD), lambda n: (0, 0)),
                  pl.BlockSpec((1, NH, HEAD, HEAD), lambda n: (ns - 1 - n, 0, 0, 0)),
                  pl.BlockSpec((r, w512), lambda n: (ns - 1 - n, 1))],
        out_specs=[pl.BlockSpec((r, 4 * w512), lambda n: (ns - 1 - n, 0)),
                   pl.BlockSpec((1, 1, w512), lambda n: (n, 0, 0)),
                   pl.BlockSpec((1, HEAD), lambda n: (0, 0))],
        out_shape=[_sds((t, 4 * w512), bf16), _sds((ns, 1, w512), f32), _sds((1, HEAD), f32)],
        scratch_shapes=[pltpu.VMEM((NH, HEAD, HEAD), f32), pltpu.VMEM((nsub * NH, HEAD, HEAD), f32)],
        compiler_params=_cp(), name=name)(p, p, p, p, lb, gain, states, do)


def _lru_gates(xb, wa, wx, ba, bx, lam):
    xh = xb.astype(bf16)
    r = jax.nn.sigmoid(_dot(xh, wa.astype(bf16), NN) + ba)
    i = jax.nn.sigmoid(_dot(xh, wx.astype(bf16), NN) + bx)
    log_a = -RG_C * r * jax.nn.softplus(-lam)
    a = jnp.exp(log_a)
    t2 = 2.0 * log_a
    series = -t2 * (1.0 + t2 * (0.5 + t2 * (1.0 / 6.0 + t2 * (1.0 / 24.0))))
    om = jnp.where(t2 > -1e-2, series, 1.0 - jnp.exp(t2))
    u = jnp.sqrt(jnp.maximum(om, 0.0)) * (i * xb)
    return a, u


def _lru_gates_fwd(xc, wa, wx, ba, bx, lam, name):
    t = xc.shape[0]
    blk = pl.BlockSpec((TT, LRU_BLOCK), lambda h, i: (i, h))
    wsp = pl.BlockSpec((1, LRU_BLOCK, LRU_BLOCK), lambda h, i: (h, 0, 0))
    vsp = pl.BlockSpec((1, LRU_BLOCK), lambda h, i: (0, h))

    def body(x_ref, wa_ref, wx_ref, ba_ref, bx_ref, lam_ref, a_ref, u_ref):
        a, u = _lru_gates(x_ref[...], wa_ref[0], wx_ref[0], ba_ref[...], bx_ref[...], lam_ref[...])
        a_ref[...] = a
        u_ref[...] = u

    return pl.pallas_call(
        body, grid=(NH, t // TT), in_specs=[blk, wsp, wsp, vsp, vsp, vsp], out_specs=[blk, blk],
        out_shape=[_sds((t, D), f32)] * 2, compiler_params=_cp(), name=name)(xc, wa, wx, ba, bx, lam)


def _lru_gates_bwd(xc, wa, wx, ba, bx, lam, da, du, name):
    t = xc.shape[0]
    blk = pl.BlockSpec((TT, LRU_BLOCK), lambda h, i: (i, h))
    wsp = pl.BlockSpec((1, LRU_BLOCK, LRU_BLOCK), lambda h, i: (h, 0, 0))
    vsp = pl.BlockSpec((1, LRU_BLOCK), lambda h, i: (0, h))

    def body(x_ref, wa_ref, wx_ref, ba_ref, bx_ref, lam_ref, da_ref, du_ref,
             dx_ref, dwa_ref, dwx_ref, dba_ref, dbx_ref, dlam_ref):
        @pl.when(pl.program_id(1) == 0)
        def _():
            for r in (dwa_ref, dwx_ref, dba_ref, dbx_ref, dlam_ref):
                r[...] = jnp.zeros_like(r)

        _, vjp = jax.vjp(_lru_gates, x_ref[...], wa_ref[0], wx_ref[0], ba_ref[...], bx_ref[...], lam_ref[...])
        dx, dwa, dwx, dba, dbx, dlam = vjp((da_ref[...], du_ref[...]))
        dx_ref[...] = dx
        dwa_ref[0] += dwa
        dwx_ref[0] += dwx
        dba_ref[...] += dba
        dbx_ref[...] += dbx
        dlam_ref[...] += dlam

    return pl.pallas_call(
        body, grid=(NH, t // TT), in_specs=[blk, wsp, wsp, vsp, vsp, vsp, blk, blk],
        out_specs=[blk, wsp, wsp, vsp, vsp, vsp],
        out_shape=[_sds((t, D), f32), _sds((NH, LRU_BLOCK, LRU_BLOCK), f32), _sds((NH, LRU_BLOCK, LRU_BLOCK), f32),
                   _sds((1, D), f32), _sds((1, D), f32), _sds((1, D), f32)],
        compiler_params=_cp(), name=name)(xc, wa, wx, ba, bx, lam, da, du)


_SCAN_SHIFTS = (1, 2, 4, 8, 16, 32, 64, 128)
_SCAN_PAD = 128


def _gelu(y):
    return jax.nn.gelu(y, approximate=True)


def _lru_scan_fwd(a, u, p2, name):
    t = a.shape[0]
    tc = 128
    blk = pl.BlockSpec((TT, tc), lambda j, i: (i, j))

    def body(a_ref, u_ref, y_ref, h_ref, hg_ref, a_s, b_s, carry):
        i = pl.program_id(1)

        @pl.when(i == 0)
        def _():
            carry[...] = jnp.zeros_like(carry)
            a_s[0:_SCAN_PAD, :] = jnp.ones((_SCAN_PAD, tc), f32)
            b_s[0:_SCAN_PAD, :] = jnp.zeros((_SCAN_PAD, tc), f32)

        av, bv = a_ref[...], u_ref[...]
        for s in _SCAN_SHIFTS:
            if s < 8:
                a_s[_SCAN_PAD:, :] = av
                b_s[_SCAN_PAD:, :] = bv
                ash = a_s[pl.ds(_SCAN_PAD - s, TT), :]
                bsh = b_s[pl.ds(_SCAN_PAD - s, TT), :]
            else:
                ash = jnp.concatenate([jnp.ones((s, tc), f32), av[:TT - s]], axis=0)
                bsh = jnp.concatenate([jnp.zeros((s, tc), f32), bv[:TT - s]], axis=0)
            bv = bv + av * bsh
            av = av * ash
        h = bv + av * carry[7:8, :]
        h_ref[...] = h
        hg_ref[...] = (h * _gelu(y_ref[...])).astype(bf16)
        carry[...] = h[TT - 8:, :]

    return pl.pallas_call(
        body, grid=(D // tc, t // TT), in_specs=[blk, blk, blk], out_specs=[blk, blk],
        out_shape=[_sds((t, D), f32), _sds((t, D), bf16)],
        scratch_shapes=[pltpu.VMEM((_SCAN_PAD + TT, tc), f32), pltpu.VMEM((_SCAN_PAD + TT, tc), f32),
                        pltpu.VMEM((8, tc), f32)],
        compiler_params=_cp(), name=name)(a, u, p2)


def _lru_scan_bwd(a, h, p2, dhg, name):
    t = a.shape[0]
    tc = 128
    nt = t // TT
    hb = TT // 8
    rblk = pl.BlockSpec((TT, tc), lambda j, i: (nt - 1 - i, j))

    def body(a_ref, an_ref, h_ref, hp_ref, y_ref, dhg_ref, du_ref, da_ref, dy_ref, a_s, b_s, ap, hp, carry):
        i = pl.program_id(1)

        @pl.when(i == 0)
        def _():
            carry[...] = jnp.zeros_like(carry)
            a_s[TT:, :] = jnp.ones((_SCAN_PAD, tc), f32)
            b_s[TT:, :] = jnp.zeros((_SCAN_PAD, tc), f32)

        ap[0:TT, :] = a_ref[...]
        ap[TT:, :] = jnp.where(i == 0, 0.0, an_ref[...])
        hp[0:8, :] = jnp.where(i == nt - 1, 0.0, hp_ref[...])
        hp[8:, :] = h_ref[...]
        y = y_ref[...]
        gate, gvjp = jax.vjp(_gelu, y)
        dhg_v = dhg_ref[...]
        dy_ref[...] = gvjp(dhg_v * h_ref[...])[0]
        av = ap[pl.ds(1, TT), :]
        bv = dhg_v * gate
        for s in _SCAN_SHIFTS:
            if s < 8:
                a_s[0:TT, :] = av
                b_s[0:TT, :] = bv
                ash = a_s[pl.ds(s, TT), :]
                bsh = b_s[pl.ds(s, TT), :]
            else:
                ash = jnp.concatenate([av[s:], jnp.ones((s, tc), f32)], axis=0)
                bsh = jnp.concatenate([bv[s:], jnp.zeros((s, tc), f32)], axis=0)
            bv = bv + av * bsh
            av = av * ash
        g = bv + av * carry[0:1, :]
        du_ref[...] = g
        da_ref[...] = g * hp[pl.ds(7, TT), :]
        carry[...] = g[0:8, :]

    in_specs = [
        rblk,
        pl.BlockSpec((8, tc), lambda j, i: (jnp.minimum((nt - i) * hb, t // 8 - 1), j)),
        rblk,
        pl.BlockSpec((8, tc), lambda j, i: (jnp.maximum((nt - 1 - i) * hb - 1, 0), j)),
        rblk, rblk,
    ]
    return pl.pallas_call(
        body, grid=(D // tc, nt), in_specs=in_specs, out_specs=[rblk, rblk, rblk],
        out_shape=[_sds((t, D), f32)] * 3,
        scratch_shapes=[pltpu.VMEM((TT + _SCAN_PAD, tc), f32), pltpu.VMEM((TT + _SCAN_PAD, tc), f32),
                        pltpu.VMEM((TT + 8, tc), f32), pltpu.VMEM((TT + 8, tc), f32), pltpu.VMEM((8, tc), f32)],
        compiler_params=_cp(), name=name)(a, a, h, h, p2, dhg)


def _ffn_act_bwd(gc, up, dact, name, tc=256):
    t = gc.shape[0]
    blk = pl.BlockSpec((TT, tc), lambda i, j: (i, j))
    vblk = pl.BlockSpec((TT, tc), lambda i, j: (i, j + D_FF // tc))

    def body(gc_ref, v_ref, da_ref, dgc_ref, dv_ref):
        _, vjp = jax.vjp(lambda g, v: _silu(g) * v, gc_ref[...], v_ref[...])
        dg, dv = vjp(da_ref[...].astype(f32))
        dgc_ref[...] = dg
        dv_ref[...] = dv.astype(bf16)

    return pl.pallas_call(
        body, grid=(t // TT, D_FF // tc), in_specs=[blk, vblk, blk], out_specs=[blk, vblk],
        out_shape=[_sds((t, D_FF), f32), _sds((t, 2 * D_FF), bf16)], compiler_params=_cp(), name=name)(gc, up, dact)


def _lower_bounds_fwd(w):
    def body(w_ref, o0_ref, o1_ref):
        wv = w_ref[...]
        o0, o1 = _lb_rows(wv[0:1, :], wv[1:2, :])
        o0_ref[...] = o0
        o1_ref[...] = o1

    return pl.pallas_call(body, out_shape=[_sds((1, 512), f32)] * 2, name="lower_bounds_fwd")(w)


def _lb_rows(w0, w1):
    m = jnp.maximum(w0, w1)
    e0, e1 = jnp.exp(w0 - m), jnp.exp(w1 - m)
    s = e0 + e1
    p0, p1 = e0 / s, e1 / s
    return p0 - p0, (p0 + p1) - p0


def _lower_bounds_bwd(w, d0, d1):
    def body(w_ref, d0_ref, d1_ref, g0_ref, g1_ref):
        wv = w_ref[...]
        _, vjp = jax.vjp(_lb_rows, wv[0:1, :], wv[1:2, :])
        g0, g1 = vjp((d0_ref[...], d1_ref[...]))
        g0_ref[...] = g0
        g1_ref[...] = g1

    return pl.pallas_call(body, out_shape=[_sds((1, 512), f32)] * 2, name="lower_bounds_bwd")(w, d0, d1)


def _local_step(x, target, wt, pre_layer=None, post_grads=None):
    depth = 4
    res = []
    lb0, lb1 = _lower_bounds_fwd(wt["hgrn_lower_bounds"])
    lbs = [lb0, lb1]
    for layer in range(depth):
        j = layer // 2
        sv = {"x_in": x}
        deps = pre_layer(layer, "mix", x) if pre_layer else ()
        if layer == 0:
            h1 = _rmsnorm_fwd(x, wt["norm_mix"][layer], "rms_fwd", deps)
        sv["h1"] = h1
        if layer % 2 == 0:
            p = _mm_auto(h1, wt["ab_w_in"][j], "nn", f32, "mm_ab_in")
            conv = _conv_fwd(p, 0, 1536, wt["gdn_conv_w"][j], jnp.zeros((1, 1536), f32), "gdn_conv_fwd", tc=768)
            o_a, s_a = _gdn2_fwd(conv, p, wt["alog_v"][j], wt["dtb_v"][j], wt["gdn_norm"][j], "gdn_fwd")
            o_b, s_b = _hgrn_fwd(p, lbs[j], wt["hgrn_norm"][j], "hgrn_fwd")
            o = jnp.concatenate([o_a, o_b], axis=1)
            x, h2 = _mm_res_norm(o, wt["ab_w_out"][j], x, wt["norm_ffn"][layer], "mm_ab_out")
            sv.update(p=p, conv=conv, s_a=s_a, s_b=s_b, o=o)
        else:
            p2 = _mm_nn_slots(h1, wt["c_w_in"][j], f32, "mm_c_in", deps)
            xc = _conv_fwd(p2, 1, D, wt["c_conv_w"][j], wt["c_conv_b"][j], "lru_conv_fwd", tc=D)
            a, u = _lru_gates_fwd(xc, wt["c_gate_a_w"][j], wt["c_gate_x_w"][j], wt["c_gate_a_b"][j],
                                  wt["c_gate_x_b"][j], wt["c_lambda"][j], "lru_gates_fwd")
            h, hg = _lru_scan_fwd(a, u, p2, "lru_scan_fwd")
            x, h2 = _mm_res_norm(hg, wt["c_w_out"][j], x, wt["norm_ffn"][layer], "mm_c_out")
            sv.update(p2=p2, xc=xc, a=a, h=h, hg=hg)
        sv["x_mid"] = x
        deps = pre_layer(layer, "ffn", x) if pre_layer else ()
        up = _mm_nn_slots(h2, wt["ffn_w_up"][layer], bf16, "mm_up", deps)
        gc, act = _conv_fwd(up, 0, D_FF, wt["ffn_conv_w"][layer], wt["ffn_conv_b"][layer], "ffn_conv_fwd",
                            tc=D_FF // 2, val=up, val_col0=2)
        if layer + 1 < depth:
            x, h1 = _mm_res_norm(act, wt["ffn_w_down"][layer], x, wt["norm_mix"][layer + 1], "mm_down")
        else:
            x = _mm_res_norm(act, wt["ffn_w_down"][layer], x, None, "mm_down_last")
        sv.update(h2=h2, up=up, gc=gc, act=act)
        res.append(sv)

    loss, dx, dxb, d_norm_final = _final_loss(x, wt["norm_final"], target, "final_loss")

    g = {k: [None] * len(v) for k, v in wt.items() if isinstance(v, list)}
    g["norm_final"] = d_norm_final
    d_lbs = [None, None]
    for layer in reversed(range(depth)):
        j = layer // 2
        sv = res[layer]
        dact = _mm_auto(dxb, wt["ffn_w_down"][layer], "nt", bf16, "mm_down_dx")
        g["ffn_w_down"][layer] = _mm_auto(sv["act"], dxb, "tn", bf16, "mm_down_dw")
        dgc, dup = _ffn_act_bwd(sv["gc"], sv["up"], dact, "ffn_act_bwd", tc=D_FF // 2)
        dup, dcw, dcb = _conv_bwd(dgc, sv["up"], 0, wt["ffn_conv_w"][layer], "ffn_conv_bwd", tc=D_FF // 2, into=dup)
        g["ffn_conv_w"][layer] = dcw[:3]
        g["ffn_conv_b"][layer] = dcb
        g["ffn_w_up"][layer] = _mm_tn_slots(sv["h2"], dup, "mm_up_dw")
        deps = post_grads(layer, "ffn", g) if post_grads else ()
        dx, dxb, g["norm_ffn"][layer] = _mm_nt_rmsbwd(dup, wt["ffn_w_up"][layer], sv["x_mid"], wt["norm_ffn"][layer], dx,
                                                      "mm_up_dx", deps)
        if layer % 2 == 0:
            do = _mm_auto(dxb, wt["ab_w_out"][j], "nt", f32, "mm_ab_out_dx")
            g["ab_w_out"][j] = _mm_auto(sv["o"], dxb, "tn", bf16, "mm_ab_out_dw")
            dconv, dz, dba, dal, ddt, dgn = _gdn2_bwd(
                sv["conv"], sv["p"], wt["alog_v"][j], wt["dtb_v"][j], wt["gdn_norm"][j], sv["s_a"], do, "gdn_bwd")
            g["alog_v"][j], g["dtb_v"][j], g["gdn_norm"][j] = dal, ddt, dgn
            dqkv, dcw, _ = _conv_bwd(dconv, sv["p"], 0, wt["gdn_conv_w"][j], "gdn_conv_bwd", tc=768)
            g["gdn_conv_w"][j] = dcw[:4]
            dqfig, dlb, dhn = _hgrn_bwd(sv["p"], lbs[j], wt["hgrn_norm"][j], sv["s_b"], do, "hgrn_bwd")
            g["hgrn_norm"][j] = dhn
            d_lbs[j] = jnp.sum(dlb, axis=0)
            dp = jnp.concatenate([dqkv, dz.astype(bf16), dqfig, dba.astype(bf16)], axis=1)
            g["ab_w_in"][j] = _mm_auto(sv["h1"], dp, "tn", bf16, "mm_ab_in_dw")
            dpre, wpre = dp, wt["ab_w_in"][j]
        else:
            dhg = _mm_auto(dxb, wt["c_w_out"][j], "nt", f32, "mm_c_out_dx")
            g["c_w_out"][j] = _mm_auto(sv["hg"], dxb, "tn", bf16, "mm_c_out_dw")
            du, da, dy = _lru_scan_bwd(sv["a"], sv["h"], sv["p2"], dhg, "lru_scan_bwd")
            dxc, dwa, dwx, dba_, dbx_, dlam = _lru_gates_bwd(
                sv["xc"], wt["c_gate_a_w"][j], wt["c_gate_x_w"][j], wt["c_gate_a_b"][j], wt["c_gate_x_b"][j],
                wt["c_lambda"][j], da, du, "lru_gates_bwd")
            g["c_gate_a_w"][j], g["c_gate_x_w"][j] = dwa, dwx
            g["c_gate_a_b"][j], g["c_gate_x_b"][j], g["c_lambda"][j] = dba_, dbx_, dlam
            dxbr, dcw, dcb = _conv_bwd(dxc, sv["p2"], 1, wt["c_conv_w"][j], "lru_conv_bwd", tc=D)
            g["c_conv_w"][j] = dcw[:4]
            g["c_conv_b"][j] = dcb
            dp2 = jnp.concatenate([dy.astype(bf16), dxbr], axis=1)
            g["c_w_in"][j] = _mm_tn_slots(sv["h1"], dp2, "mm_c_in_dw")
            dpre, wpre = dp2, wt["c_w_in"][j]
        deps = post_grads(layer, "mix", g) if post_grads else ()
        dx, dxb, g["norm_mix"][layer] = _mm_nt_rmsbwd(dpre, wpre, sv["x_in"], wt["norm_mix"][layer], dx, "mm_mix_in_dx", deps)
    g0, g1 = _lower_bounds_bwd(wt["hgrn_lower_bounds"], d_lbs[0], d_lbs[1])
    g["hgrn_lower_bounds"] = jnp.concatenate([g0, g1], axis=0)
    return loss, dx, g


def _ab_in_to_compute(w):
    return jnp.concatenate([w[:, :2048], w[:, 2056:4104], w[:, 2048:2056], jnp.zeros((D, 120), w.dtype)], axis=1)


def _ab_in_from_compute(g):
    return jnp.concatenate([g[:, :2048], g[:, 4096:4104], g[:, 2048:4096]], axis=1)


def _lane_vec(v4):
    return jnp.zeros((1, HEAD), f32).at[0, NH:2 * NH].set(v4)


def _layout_weights(fw):
    wt = {}
    wt["norm_mix"] = [fw["norm_mix"][l][None] for l in range(4)]
    wt["norm_ffn"] = [fw["norm_ffn"][l][None] for l in range(4)]
    wt["norm_final"] = fw["norm_final"][None]
    wt["gdn_conv_w"] = [fw["gdn_conv_w"][j] for j in range(2)]
    wt["alog_v"] = [_lane_vec(fw["gdn_a_log"][j]) for j in range(2)]
    wt["dtb_v"] = [_lane_vec(fw["gdn_dt_bias"][j]) for j in range(2)]
    wt["gdn_norm"] = [fw["gdn_norm"][j][None] for j in range(2)]
    wt["hgrn_lower_bounds"] = fw["hgrn_lower_bounds"]
    wt["hgrn_norm"] = [fw["hgrn_norm"][j][None] for j in range(2)]
    wt["c_conv_w"] = [fw["c_conv_w"][j] for j in range(2)]
    for k in ("c_conv_b", "c_gate_a_b", "c_gate_x_b", "c_lambda"):
        wt[k] = [fw[k][j][None] for j in range(2)]
    wt["ffn_conv_w"] = [fw["ffn_conv_w"][l] for l in range(4)]
    wt["ffn_conv_b"] = [fw["ffn_conv_b"][l][None] for l in range(4)]
    if "ab_w_in" in fw:
        wt["ab_w_in"] = [_ab_in_to_compute(fw["ab_w_in"][j].astype(bf16)) for j in range(2)]
        for k in ("ab_w_out", "c_w_out"):
            wt[k] = [fw[k][j].astype(bf16) for j in range(2)]
        wt["c_w_in"] = [_to_slots(fw["c_w_in"][j].astype(bf16)) for j in range(2)]
        for k in ("c_gate_a_w", "c_gate_x_w"):
            wt[k] = [fw[k][j].astype(f32) for j in range(2)]
        wt["ffn_w_up"] = [_to_slots(fw["ffn_w_up"][l].astype(bf16)) for l in range(4)]
        wt["ffn_w_down"] = [fw["ffn_w_down"][l].astype(bf16) for l in range(4)]
    return wt


SLOT_MAJOR = ("c_w_in", "ffn_w_up")
NATIVE_PERM = {"ab_w_in": (2, 0, 1), "ffn_conv_w": (1, 0, 2)}


def _to_slots(wfull):
    k, n = wfull.shape
    return wfull.reshape(k, N_SLOTS, n // N_SLOTS).transpose(1, 0, 2)


def _layer_full(name, slots):
    kind = BIG[name]
    if name in SLOT_MAJOR:
        return slots
    if kind == "col":
        return _ab_in_to_compute(slots.transpose(1, 0, 2).reshape(slots.shape[1], -1))
    if kind == "row":
        return slots.reshape(-1, slots.shape[2])
    return slots.reshape(4, NH, LRU_BLOCK // 4, LRU_BLOCK).transpose(1, 0, 2, 3).reshape(NH, LRU_BLOCK, LRU_BLOCK).astype(f32)


def _layer_slots(name, g):
    kind = BIG[name]
    if name in SLOT_MAJOR:
        return g
    if kind == "col":
        g = _ab_in_from_compute(g)
        r, cdim = g.shape
        return g.reshape(r, 4, cdim // 4).transpose(1, 0, 2).astype(bf16)
    if kind == "row":
        r, cdim = g.shape
        return g.reshape(4, r // 4, cdim).astype(bf16)
    return g.reshape(NH, 4, LRU_BLOCK // 4, LRU_BLOCK).transpose(1, 0, 2, 3).reshape(4, LRU_BLOCK, LRU_BLOCK).astype(bf16)


def _unlayout_grads(g):
    out = {}
    for k in ("norm_mix", "norm_ffn", "gdn_norm", "hgrn_norm", "c_conv_b", "c_gate_a_b", "c_gate_x_b", "c_lambda",
              "ffn_conv_b"):
        out[k] = jnp.concatenate(g[k], axis=0)
    out["norm_final"] = g["norm_final"][0]
    out["ab_w_in"] = jnp.stack([_ab_in_from_compute(t) for t in g["ab_w_in"]])
    out["gdn_a_log"] = jnp.stack([t[0, NH:2 * NH] for t in g["alog_v"]])
    out["gdn_dt_bias"] = jnp.stack([t[0, NH:2 * NH] for t in g["dtb_v"]])
    out["hgrn_lower_bounds"] = g["hgrn_lower_bounds"]
    for k in ("gdn_conv_w", "ab_w_out", "c_conv_w", "c_gate_a_w", "c_gate_x_w", "c_w_out", "ffn_conv_w", "ffn_w_down"):
        out[k] = jnp.stack(g[k])
    for k in SLOT_MAJOR:
        out[k] = jnp.stack([t.transpose(1, 0, 2).reshape(t.shape[1], -1) for t in g[k]])
    return out


MESH = pl.DeviceIdType.MESH
ANY = pl.BlockSpec(memory_space=pl.ANY)
CHIP_RELATIONS = ((1, 0), (0, 1), (1, 1))
N_CHIPS = 4


def _coords():
    return lax.axis_index("x"), lax.axis_index("y"), lax.axis_index("c")


def _flip(v, f):
    return 1 - v if f else v


def _half_rows(c, a, align):
    return pl.ds(pl.multiple_of(c * (a // 2), align), a // 2)


def _all_gather_chips(shards, name):
    n = len(shards)
    shapes = [s.shape for s in shards]

    def body(*refs):
        ins, outs = refs[:n], refs[n:2 * n]
        send_sems, recv_sems = refs[2 * n:]
        x, y, c = _coords()
        me = 2 * x + y
        sibling = (x, y, 1 - c)
        started = []
        for p in range(n):
            cp = pltpu.make_async_remote_copy(
                src_ref=ins[p], dst_ref=outs[p].at[me],
                send_sem=send_sems.at[p, 6], recv_sem=recv_sems.at[p, 6],
                device_id=sibling, device_id_type=MESH)
            cp.start()
            started.append(cp)
        for p in range(n):
            mine = _half_rows(c, shapes[p][0], 16)
            for r, (fx, fy) in enumerate(CHIP_RELATIONS):
                cp = pltpu.make_async_remote_copy(
                    src_ref=ins[p].at[mine], dst_ref=outs[p].at[me, mine],
                    send_sem=send_sems.at[p, r], recv_sem=recv_sems.at[p, r],
                    device_id=(_flip(x, fx), _flip(y, fy), c), device_id_type=MESH)
                cp.start()
                started.append(cp)
        for r, (fx, fy) in enumerate(CHIP_RELATIONS):
            k = 2 * _flip(x, fx) + _flip(y, fy)
            for p in range(n):
                mine = _half_rows(c, shapes[p][0], 16)
                pltpu.make_async_remote_copy(
                    src_ref=ins[p].at[mine], dst_ref=outs[p].at[k, mine],
                    send_sem=send_sems.at[p, r], recv_sem=recv_sems.at[p, r],
                    device_id=(_flip(x, fx), _flip(y, fy), c), device_id_type=MESH).wait_recv()
                fwd = pltpu.make_async_remote_copy(
                    src_ref=outs[p].at[k, mine], dst_ref=outs[p].at[k, mine],
                    send_sem=send_sems.at[p, 3 + r], recv_sem=recv_sems.at[p, 3 + r],
                    device_id=sibling, device_id_type=MESH)
                fwd.start()
                started.append(fwd)
        for r, (fx, fy) in enumerate(CHIP_RELATIONS):
            k = 2 * _flip(x, fx) + _flip(y, fy)
            for p in range(n):
                theirs = _half_rows(1 - c, shapes[p][0], 16)
                pltpu.make_async_remote_copy(
                    src_ref=outs[p].at[k, theirs], dst_ref=outs[p].at[k, theirs],
                    send_sem=send_sems.at[p, 3 + r], recv_sem=recv_sems.at[p, 3 + r],
                    device_id=sibling, device_id_type=MESH).wait_recv()
        for p in range(n):
            pltpu.make_async_remote_copy(
                src_ref=ins[p], dst_ref=outs[p].at[me],
                send_sem=send_sems.at[p, 6], recv_sem=recv_sems.at[p, 6],
                device_id=sibling, device_id_type=MESH).wait_recv()
        for cp in started:
            cp.wait_send()

    return pl.pallas_call(
        body, in_specs=[ANY] * n, out_specs=[ANY] * n,
        out_shape=[_sds((N_CHIPS,) + s.shape, s.dtype) for s in shards],
        scratch_shapes=[pltpu.SemaphoreType.DMA((n, 7)), pltpu.SemaphoreType.DMA((n, 7))],
        name=name)(*shards)


N_DEV = 8


def _all_reduce_small(pack, name, after=()):
    rows = pack.shape[0]

    def body(in_ref, *rest):
        sum_ref, all_ref, send_sems, recv_sems = rest[len(after):]
        x, y, c = _coords()
        me = 4 * x + 2 * y + c
        all_ref[me] = in_ref[...]
        cps = []
        for r in range(1, N_DEV):
            fx, fy, fc = (r >> 2) & 1, (r >> 1) & 1, r & 1
            cp = pltpu.make_async_remote_copy(
                src_ref=in_ref, dst_ref=all_ref.at[me],
                send_sem=send_sems.at[r], recv_sem=recv_sems.at[r],
                device_id=(_flip(x, fx), _flip(y, fy), _flip(c, fc)), device_id_type=MESH)
            cp.start()
            cps.append(cp)
        for r in range(1, N_DEV):
            fx, fy, fc = (r >> 2) & 1, (r >> 1) & 1, r & 1
            peer = 4 * _flip(x, fx) + 2 * _flip(y, fy) + _flip(c, fc)
            pltpu.make_async_remote_copy(
                src_ref=in_ref, dst_ref=all_ref.at[peer],
                send_sem=send_sems.at[r], recv_sem=recv_sems.at[r],
                device_id=(x, y, c), device_id_type=MESH).wait_recv()
        for cp in cps:
            cp.wait_send()
        acc = all_ref[0]
        for d in range(1, N_DEV):
            acc = acc + all_ref[d]
        sum_ref[...] = acc

    vm = pl.BlockSpec(memory_space=pltpu.VMEM)
    return pl.pallas_call(
        body, in_specs=[vm] + [pl.BlockSpec(memory_space=pl.ANY)] * len(after), out_specs=[vm, vm],
        out_shape=[_sds((rows, 128), f32), _sds((N_DEV, rows, 128), f32)],
        scratch_shapes=[pltpu.SemaphoreType.DMA((N_DEV,)), pltpu.SemaphoreType.DMA((N_DEV,))],
        name=name)(pack, *after)[0]


ROWS_EW = 128


def _adamw_math(w, g, m, v):
    m = ADAM_B1 * m + (1.0 - ADAM_B1) * g
    v = ADAM_B2 * v + (1.0 - ADAM_B2) * (g * g)
    m_hat = m / (1.0 - ADAM_B1 ** ADAM_STEP)
    v_hat = v / (1.0 - ADAM_B2 ** ADAM_STEP)
    delta = -ADAM_LR * (m_hat / (jnp.sqrt(v_hat) + ADAM_EPS) + ADAM_WD * w)
    return delta, m, v


def _adamw_big(w, g, m, v, name):
    nl, r, b = w.shape
    rt = _tile(r, (ROWS_EW, 352, 64))
    per = r // rt

    def body(w_ref, g_ref, m_ref, v_ref, go_ref, d_ref, mo_ref, vo_ref):
        gv = g_ref[...]
        d, mn, vn = _adamw_math(w_ref[...], gv, m_ref[...], v_ref[...])
        go_ref[...] = gv
        d_ref[...] = d
        mo_ref[...] = mn
        vo_ref[...] = vn

    blk = pl.BlockSpec((None, rt, b), lambda l, i: (l, i, 0))
    gblk = pl.BlockSpec((rt, b), lambda l, i: (l * per + i, 0))
    return pl.pallas_call(body, grid=(nl, per), in_specs=[blk, gblk, blk, blk], out_specs=[blk] * 4,
                          out_shape=[_sds((nl, r, b), f32)] * 4, compiler_params=_cp(), name=name)(w, g, m, v)


def _adamw_lead(w, g, m, v, name):
    n, r, b = w.shape
    tn = _tile(n, (64, 54, 32, 16, 8, 1))

    def body(w_ref, g_ref, m_ref, v_ref, go_ref, d_ref, mo_ref, vo_ref):
        gv = g_ref[...]
        d, mn, vn = _adamw_math(w_ref[...], gv, m_ref[...], v_ref[...])
        go_ref[...] = gv
        d_ref[...] = d
        mo_ref[...] = mn
        vo_ref[...] = vn

    blk = pl.BlockSpec((tn, r, b), lambda i: (i, 0, 0))
    return pl.pallas_call(body, grid=(n // tn,), in_specs=[blk] * 4, out_specs=[blk] * 4,
                          out_shape=[_sds((n, r, b), f32)] * 4, compiler_params=_cp(), name=name)(w, g, m, v)


def _adamw_small(ws, gs, ms, vs, name):
    n = len(ws)

    def body(*refs):
        w_r, g_r, m_r, v_r = refs[:n], refs[n:2 * n], refs[2 * n:3 * n], refs[3 * n:4 * n]
        go_r, d_r, mo_r, vo_r = refs[4 * n:5 * n], refs[5 * n:6 * n], refs[6 * n:7 * n], refs[7 * n:8 * n]
        for p in range(n):
            gv = g_r[p][...]
            d, mn, vn = _adamw_math(w_r[p][...], gv, m_r[p][...], v_r[p][...])
            go_r[p][...] = gv
            d_r[p][...] = d
            mo_r[p][...] = mn
            vo_r[p][...] = vn

    vm = pl.BlockSpec(memory_space=pltpu.VMEM)
    shp = [_sds(w.shape, f32) for w in ws]
    res = pl.pallas_call(body, in_specs=[vm] * (4 * n), out_specs=[vm] * (4 * n), out_shape=shp * 4,
                         name=name)(*ws, *gs, *ms, *vs)
    return res[:n], res[n:2 * n], res[2 * n:3 * n], res[3 * n:]


HBM = pl.BlockSpec(memory_space=pltpu.HBM)
SEM = pl.BlockSpec(memory_space=pltpu.SEMAPHORE)
EFFECT = pltpu.SideEffectType.DATAFLOW_SIDE_EFFECTING
N_REL = 8


def _rel(r):
    return (r >> 2) & 1, (r >> 1) & 1, r & 1


def _gather_copies(ins, lands, send_sems, recv_sems, shapes):
    x, y, c = _coords()
    me = 2 * x + y
    sends, recvs = [], []
    for p in range(len(ins)):
        for r in range(1, N_REL):
            fx, fy, fc = _rel(r)
            peer = (_flip(x, fx), _flip(y, fy), _flip(c, fc))
            if fx == 0 and fy == 0:
                src, dst, got = ins[p], lands[p].at[me], lands[p].at[me]
            else:
                mine = _half_rows(c, shapes[p][0], 16)
                theirs = _half_rows(_flip(c, fc), shapes[p][0], 16)
                src, dst = ins[p].at[mine], lands[p].at[me, mine]
                got = lands[p].at[2 * peer[0] + peer[1], theirs]
            sems = dict(send_sem=send_sems.at[p * N_REL + r], recv_sem=recv_sems.at[p * N_REL + r], device_id=peer,
                        device_id_type=MESH)
            sends.append(pltpu.make_async_remote_copy(src_ref=src, dst_ref=dst, **sems))
            recvs.append(pltpu.make_async_remote_copy(src_ref=src, dst_ref=got, **sems))
    return sends, recvs


def _scatter_copies(ins, lands, send_sems, recv_sems, shapes):
    x, y, c = _coords()
    sends, recvs = [], []
    for p in range(len(ins)):
        for r in range(1, N_REL):
            fx, fy, fc = _rel(r)
            peer = (_flip(x, fx), _flip(y, fy), _flip(c, fc))
            theirs = _half_rows(peer[2], shapes[p][1], 16)
            sems = dict(send_sem=send_sems.at[p * N_REL + r], recv_sem=recv_sems.at[p * N_REL + r], device_id=peer,
                        device_id_type=MESH)
            cp = pltpu.make_async_remote_copy(src_ref=ins[p].at[2 * peer[0] + peer[1], theirs], dst_ref=lands[p].at[r], **sems)
            sends.append(cp)
            recvs.append(cp)
    return sends, recvs


def _split_start(copies_fn, ins, land_shapes, name, after=()):
    n = len(ins)
    shapes = [a.shape for a in ins]

    def body(*refs):
        in_refs, land_refs = refs[:n], refs[n:2 * n]
        send_sems, recv_sems = refs[2 * n + len(after)], refs[2 * n + len(after) + 1]
        token = refs[-1]
        sends, _ = copies_fn(in_refs, land_refs, send_sems, recv_sems, shapes)
        for cp in sends:
            cp.start()
        token[...] = jnp.zeros_like(token)

    lands = [lax.empty(s.shape, s.dtype) for s in land_shapes]
    na = len(after)
    res = pl.pallas_call(
        body, name=name,
        out_shape=(pltpu.SemaphoreType.DMA((n * N_REL,)), pltpu.SemaphoreType.DMA((n * N_REL,)))
        + tuple(pltpu.HBM(a.shape, a.dtype) for a in ins) + tuple(pltpu.HBM(s.shape, s.dtype) for s in land_shapes)
        + (_sds((8, 128), f32),),
        in_specs=[HBM] * (2 * n) + [pl.BlockSpec(memory_space=pl.ANY)] * na,
        out_specs=(SEM, SEM) + (HBM,) * (2 * n) + (pl.BlockSpec(memory_space=pltpu.VMEM),),
        input_output_aliases={i: 2 + i for i in range(2 * n)},
        compiler_params=pltpu.CompilerParams(has_side_effects=EFFECT),
    )(*[pltpu.with_memory_space_constraint(a, pltpu.HBM) for a in ins],
      *[pltpu.with_memory_space_constraint(a, pltpu.HBM) for a in lands], *after)
    return dict(sems=res[:2], ins=res[2:2 + n], lands=res[2 + n:2 + 2 * n], token=res[-1], shapes=shapes)


def _split_wait(copies_fn, started, after, name):
    n = len(started["ins"])
    shapes = started["shapes"]
    na = len(after)

    def body(*refs):
        in_refs, land_refs = refs[:n], refs[n:2 * n]
        send_sems, recv_sems = refs[2 * n], refs[2 * n + 1]
        sends, recvs = copies_fn(in_refs, land_refs, send_sems, recv_sems, shapes)
        for cp in sends:
            cp.wait_send()
        for cp in recvs:
            cp.wait_recv()

    arrs = list(started["ins"]) + list(started["lands"])
    res = pl.pallas_call(
        body, name=name,
        out_shape=tuple(pltpu.HBM(a.shape, a.dtype) for a in arrs),
        in_specs=[HBM] * (2 * n) + [SEM, SEM] + [pl.BlockSpec(memory_space=pl.ANY)] * na,
        out_specs=(HBM,) * (2 * n), input_output_aliases={i: i for i in range(2 * n)},
        compiler_params=pltpu.CompilerParams(has_side_effects=EFFECT),
    )(*arrs, *started["sems"], *after)
    return res[:n], res[n:]


def _sum_pieces(gb, land, ids, f_prev, blk, nblk, name):
    _, a, b = gb.shape
    rows = _tile(a // 2, (ROWS_EW, 176, 64, 32, 16))
    nrt = (a // 2) // rows

    def body(ids_ref, g_ref, l_ref, *rest):
        o_ref = rest[-1]
        acc = g_ref[0].astype(f32)
        for r in range(1, N_REL):
            acc = acc + l_ref[r].astype(f32)
        o_ref[...] = acc

    in_specs = [pl.BlockSpec((1, rows, b), lambda i, ids_ref: (ids_ref[0], ids_ref[1] * nrt + i, 0)),
                pl.BlockSpec((N_REL, rows, b), lambda i, ids_ref: (0, i, 0))]
    args = [ids, gb, land]
    aliases = {}
    if f_prev is not None:
        in_specs.append(pl.BlockSpec(memory_space=pl.ANY))
        args.append(f_prev)
        aliases = {3: 0}
    grid_spec = pltpu.PrefetchScalarGridSpec(
        num_scalar_prefetch=1, grid=(nrt,), in_specs=in_specs,
        out_specs=pl.BlockSpec((rows, b), lambda i, ids_ref: ((2 * blk + ids_ref[1]) * nrt + i, 0)))
    return pl.pallas_call(body, grid_spec=grid_spec, out_shape=_sds((nblk * a, b), f32),
                          input_output_aliases=aliases, compiler_params=_cp(), name=name)(*args)


def _sibling_fill_blocks(fs, nblks, name):
    n = len(fs)
    shapes = [f.shape for f in fs]

    def body(*refs):
        ins, outs = refs[:n], refs[n:2 * n]
        send_sems, recv_sems = refs[2 * n:]
        x, y, c = _coords()
        cps, waits = [], []
        k = 0
        for p in range(n):
            a = shapes[p][0] // nblks[p]
            for bi in range(nblks[p]):
                mine = pl.ds(pl.multiple_of(bi * a + c * (a // 2), 8), a // 2)
                theirs = pl.ds(pl.multiple_of(bi * a + (1 - c) * (a // 2), 8), a // 2)
                sems = dict(send_sem=send_sems.at[k], recv_sem=recv_sems.at[k], device_id=(x, y, 1 - c), device_id_type=MESH)
                cp = pltpu.make_async_remote_copy(src_ref=ins[p].at[mine], dst_ref=outs[p].at[mine], **sems)
                cp.start()
                cps.append(cp)
                waits.append(pltpu.make_async_remote_copy(src_ref=ins[p].at[theirs], dst_ref=outs[p].at[theirs], **sems))
                k += 1
        for wt_ in waits:
            wt_.wait_recv()
        for cp in cps:
            cp.wait_send()

    total = sum(nblks)
    return pl.pallas_call(
        body, in_specs=[ANY] * n, out_specs=[ANY] * n,
        out_shape=[_sds(f.shape, f.dtype) for f in fs],
        input_output_aliases={p: p for p in range(n)},
        scratch_shapes=[pltpu.SemaphoreType.DMA((total,)), pltpu.SemaphoreType.DMA((total,))],
        name=name)(*fs)


WEIGHTS = ["norm_mix", "norm_ffn", "norm_final", "ab_w_in", "gdn_conv_w", "gdn_a_log", "gdn_dt_bias", "gdn_norm",
           "hgrn_lower_bounds", "hgrn_norm", "ab_w_out", "c_w_in", "c_conv_w", "c_conv_b", "c_gate_a_w", "c_gate_a_b",
           "c_gate_x_w", "c_gate_x_b", "c_lambda", "c_w_out", "ffn_w_up", "ffn_conv_w", "ffn_conv_b", "ffn_w_down"]
BIG = {"ab_w_in": "col", "ab_w_out": "row", "c_w_in": "col", "c_gate_a_w": "gate", "c_gate_x_w": "gate",
       "c_w_out": "row", "ffn_w_up": "col", "ffn_w_down": "row"}
SMALL_SHARDED = ["gdn_conv_w", "c_conv_w", "c_conv_b", "c_gate_a_b", "c_gate_x_b", "c_lambda", "ffn_conv_w"]
SMALL = [n for n in WEIGHTS if n not in BIG]
FULL_SHAPES = {
    "norm_mix": (4, 1024), "norm_ffn": (4, 1024), "norm_final": (1024,), "ab_w_in": (2, 1024, 4104),
    "gdn_conv_w": (2, 4, 1536), "gdn_a_log": (2, 4), "gdn_dt_bias": (2, 4), "gdn_norm": (2, 128),
    "hgrn_lower_bounds": (2, 512), "hgrn_norm": (2, 128), "ab_w_out": (2, 1024, 1024), "c_w_in": (2, 1024, 2048),
    "c_conv_w": (2, 4, 1024), "c_conv_b": (2, 1024), "c_gate_a_w": (2, 4, 256, 256), "c_gate_a_b": (2, 1024),
    "c_gate_x_w": (2, 4, 256, 256), "c_gate_x_b": (2, 1024), "c_lambda": (2, 1024), "c_w_out": (2, 1024, 1024),
    "ffn_w_up": (4, 1024, 5632), "ffn_conv_w": (4, 3, 2816), "ffn_conv_b": (4, 2816), "ffn_w_down": (4, 2816, 1024)}


def _pack_rows(arrs, rows):
    parts, used = [], 0
    for a in arrs:
        r = _pack_tile_rows(a.size)
        parts.append(jnp.pad(a.reshape(-1), (0, r * 128 - a.size)).reshape(r, 128))
        used += r
    assert rows >= used and (rows - used) % 8 == 0, (rows, used)
    if rows > used:
        parts.append(jnp.zeros((rows - used, 128), f32))
    return jnp.concatenate(parts, axis=0)


def _pack_tile_rows(size):
    return -(-size // 1024) * 8


def _unpack_rows(pack, shapes):
    out, row = [], 0
    for s in shapes:
        size = 1
        for d in s:
            size *= d
        r = _pack_tile_rows(size)
        out.append(pack[row:row + r].reshape(-1)[:size].reshape(s))
        row += r
    return out


def kernel(x, norm_mix, norm_ffn, norm_final, ab_w_in, gdn_conv_w, gdn_a_log, gdn_dt_bias, gdn_norm, hgrn_lower_bounds, hgrn_norm, ab_w_out, c_w_in, c_conv_w, c_conv_b, c_gate_a_w, c_gate_a_b, c_gate_x_w, c_gate_x_b, c_lambda, c_w_out, ffn_w_up, ffn_conv_w, ffn_conv_b, ffn_w_down, loss_target, m_norm_mix, m_norm_ffn, m_norm_final, m_ab_w_in, m_gdn_conv_w, m_gdn_a_log, m_gdn_dt_bias, m_gdn_norm, m_hgrn_lower_bounds, m_hgrn_norm, m_ab_w_out, m_c_w_in, m_c_conv_w, m_c_conv_b, m_c_gate_a_w, m_c_gate_a_b, m_c_gate_x_w, m_c_gate_x_b, m_c_lambda, m_c_w_out, m_ffn_w_up, m_ffn_conv_w, m_ffn_conv_b, m_ffn_w_down, v_norm_mix, v_norm_ffn, v_norm_final, v_ab_w_in, v_gdn_conv_w, v_gdn_a_log, v_gdn_dt_bias, v_gdn_norm, v_hgrn_lower_bounds, v_hgrn_norm, v_ab_w_out, v_c_w_in, v_c_conv_w, v_c_conv_b, v_c_gate_a_w, v_c_gate_a_b, v_c_gate_x_w, v_c_gate_x_b, v_c_lambda, v_c_w_out, v_ffn_w_up, v_ffn_conv_w, v_ffn_conv_b, v_ffn_w_down):
    w = dict(zip(WEIGHTS, (norm_mix, norm_ffn, norm_final, ab_w_in, gdn_conv_w, gdn_a_log, gdn_dt_bias, gdn_norm, hgrn_lower_bounds, hgrn_norm, ab_w_out, c_w_in, c_conv_w, c_conv_b, c_gate_a_w, c_gate_a_b, c_gate_x_w, c_gate_x_b, c_lambda, c_w_out, ffn_w_up, ffn_conv_w, ffn_conv_b, ffn_w_down)))
    m = dict(zip(WEIGHTS, (m_norm_mix, m_norm_ffn, m_norm_final, m_ab_w_in, m_gdn_conv_w, m_gdn_a_log, m_gdn_dt_bias, m_gdn_norm, m_hgrn_lower_bounds, m_hgrn_norm, m_ab_w_out, m_c_w_in, m_c_conv_w, m_c_conv_b, m_c_gate_a_w, m_c_gate_a_b, m_c_gate_x_w, m_c_gate_x_b, m_c_lambda, m_c_w_out, m_ffn_w_up, m_ffn_conv_w, m_ffn_conv_b, m_ffn_w_down)))
    v = dict(zip(WEIGHTS, (v_norm_mix, v_norm_ffn, v_norm_final, v_ab_w_in, v_gdn_conv_w, v_gdn_a_log, v_gdn_dt_bias, v_gdn_norm, v_hgrn_lower_bounds, v_hgrn_norm, v_ab_w_out, v_c_w_in, v_c_conv_w, v_c_conv_b, v_c_gate_a_w, v_c_gate_a_b, v_c_gate_x_w, v_c_gate_x_b, v_c_lambda, v_c_w_out, v_ffn_w_up, v_ffn_conv_w, v_ffn_conv_b, v_ffn_w_down)))
    big = list(BIG)
    chip = 2 * lax.axis_index("x") + lax.axis_index("y")
    ids = jnp.stack([chip, lax.axis_index("c")]).astype(jnp.int32)

    def layer_parts(l):
        j = l // 2
        if l % 2 == 0:
            mix = [("ab_w_in", j), ("ab_w_out", j)]
        else:
            mix = [("c_w_in", j), ("c_gate_a_w", j), ("c_gate_x_w", j), ("c_w_out", j)]
        return mix, [("ffn_w_up", l), ("ffn_w_down", l)]

    def layer_shard(n, i):
        s = w[n][i]
        return s.reshape(-1, s.shape[-1]).astype(bf16)

    small_shard_shapes = [w[n].shape for n in SMALL_SHARDED]
    small_pack = _pack_rows([w[n] for n in SMALL_SHARDED], 160)
    mix0, ffn0 = layer_parts(0)
    gathered0 = _all_gather_chips([layer_shard(n, i) for n, i in mix0] + [small_pack], "all_gather_mixer0")
    gathers = {}

    def start_gather(key, after):
        shards = [layer_shard(n, i) for n, i in layer_parts(key[0])[0 if key[1] == "mix" else 1]]
        gathers[key] = _split_start(_gather_copies, shards, [_sds((N_CHIPS,) + s.shape, bf16) for s in shards],
                                    "gather_start_%d_%s" % key, after)
        return gathers[key]["token"]

    tok = ()
    for key in ((0, "ffn"), (1, "mix"), (1, "ffn")):
        tok = (start_gather(key, tok),)
    first_tokens = tok
    start_next = {(0, "ffn"): ((2, "mix"), (2, "ffn")), (1, "mix"): ((3, "mix"),), (1, "ffn"): ((3, "ffn"),)}
    fw = {}
    per_chip = [_unpack_rows(gathered0[-1][k], small_shard_shapes) for k in range(N_CHIPS)]
    for i, n in enumerate(SMALL_SHARDED):
        fw[n] = jnp.concatenate([per_chip[k][i] for k in range(N_CHIPS)], axis=-1)
    for n in SMALL:
        if n not in fw:
            fw[n] = w[n]
    wt = _layout_weights(fw)
    for n in big:
        wt[n] = [None] * FULL_SHAPES[n][0]

    def pre_layer(l, part, x_l):
        parts = layer_parts(l)[0 if part == "mix" else 1]
        if l == 0 and part == "mix":
            lands, deps = gathered0[:len(mix0)], first_tokens
        else:
            _, lands = _split_wait(_gather_copies, gathers[(l, part)], [x_l], "gather_wait_%d_%s" % (l, part))
            deps = tuple(lands[:1])
            for key in start_next.get((l, part), ()):
                deps = (start_gather(key, deps),)
            if (l, part) not in start_next:
                deps = ()
        for (n, i), slots in zip(parts, lands):
            wt[n][i] = _layer_full(n, slots)
        return deps

    scatters = []

    def post_grads(l, part, g):
        parts = layer_parts(l)[0 if part == "mix" else 1]
        slots = [_layer_slots(n, g[n][i]) for n, i in parts]
        st = _split_start(_scatter_copies, slots, [_sds((N_REL, s.shape[1] // 2, s.shape[2]), bf16) for s in slots],
                          "scatter_start_%d_%s" % (l, part))
        scatters.append((parts, st, "scatter_wait_%d_%s" % (l, part)))
        return (st["token"],)

    loss, dx, g = _local_step(x[0], loss_target[0], wt, pre_layer, post_grads)
    gf = _unlayout_grads(g)
    loss = lax.psum(loss[0, 0], ("x", "y", "c"))

    out_g, out_d, out_m, out_v = {}, {}, {}, {}
    f = {n: None for n in big}
    last = [n for n, _ in layer_parts(0)[0]]

    def finish(group, after):
        for parts, st, wait_name in group:
            gbs, lands = _split_wait(_scatter_copies, st, after, wait_name)
            for (n, i), gb, land in zip(parts, gbs, lands):
                f[n] = _sum_pieces(gb, land, ids, f[n], i, FULL_SHAPES[n][0], "rs_sum")

    def adamw(names, tag):
        filled = _sibling_fill_blocks([f[n] for n in names], [FULL_SHAPES[n][0] for n in names], "rs_sibling_fill_" + tag)
        for n, g_n in zip(names, filled):
            shp = w[n].shape
            if n in NATIVE_PERM:
                to_native = lambda t: t.transpose(NATIVE_PERM[n])
                res = _adamw_lead(to_native(w[n]), to_native(g_n.reshape(shp)), to_native(m[n]), to_native(v[n]), "adamw_" + n)
                res = [t.transpose(1, 2, 0) for t in res]
            else:
                as3d = lambda t: t.reshape((-1,) + shp[-2:])
                res = _adamw_big(as3d(w[n]), g_n, as3d(m[n]), as3d(v[n]), "adamw_" + n)
            out_g[n], out_d[n], out_m[n], out_v[n] = (t.reshape(shp) for t in res)

    finish(scatters[:-1], [dx])
    adamw([n for n in big if n not in last], "a")
    finish(scatters[-1:], [out_v["ffn_w_up"]])
    adamw(last, "b")

    small_full_shapes = [FULL_SHAPES[n] for n in SMALL]
    small_sum = _all_reduce_small(_pack_rows([gf[n] for n in SMALL], 688), "all_reduce_small", [f[last[0]]])
    g_small = dict(zip(SMALL, _unpack_rows(small_sum, small_full_shapes)))
    for n in SMALL_SHARDED:
        width = w[n].shape[-1]
        g_small[n] = lax.dynamic_slice_in_dim(g_small[n], chip * width, width, axis=-1)

    def small2d(n, t):
        if n in NATIVE_PERM:
            t = t.transpose(NATIVE_PERM[n])
        return t.reshape(-1, t.shape[-1])

    sg, sd, sm, sv = _adamw_small([small2d(n, w[n]) for n in SMALL], [small2d(n, g_small[n]) for n in SMALL],
                                  [small2d(n, m[n]) for n in SMALL], [small2d(n, v[n]) for n in SMALL], "adamw_small")
    for i, n in enumerate(SMALL):
        for out, t in zip((out_g, out_d, out_m, out_v), (sg, sd, sm, sv)):
            if n in NATIVE_PERM:
                perm = NATIVE_PERM[n]
                shp_t = tuple(w[n].shape[p] for p in perm)
                out[n] = t[i].reshape(shp_t).transpose(tuple(perm.index(k) for k in range(len(perm))))
            else:
                out[n] = t[i].reshape(w[n].shape)
    return (loss, dx[None], *[out_g[n] for n in WEIGHTS], *[out_d[n] for n in WEIGHTS],
            *[out_m[n] for n in WEIGHTS], *[out_v[n] for n in WEIGHTS])
```

```python
import functools

import jax
import jax.numpy as jnp
from jax import lax
from jax.experimental import pallas as pl
from jax.experimental.pallas import tpu as pltpu

f32 = jnp.float32
bf16 = jnp.bfloat16

D = 1024
EPS = 1e-6
F_FLOOR = 1e-30
GDN_CHUNK = 64
GDN_INTRA_CHUNKS = 1
HGRN_CHUNK = 16
HGRN_STEP = 128
HEAD = 128
NH = 4
LRU_BLOCK = 256
D_FF = 2816
RG_C = 8.0
C_QKV, C_Z, C_QB, C_FB, C_IB, C_GB, C_BA = 0, 1536, 2048, 2560, 3072, 3584, 4096
TT = 256
N_SLOTS = 4
VMEM_LIMIT = 56 * 1024 * 1024

ADAM_LR, ADAM_B1, ADAM_B2, ADAM_EPS, ADAM_WD, ADAM_STEP = 0.001, 0.9, 0.999, 1e-08, 0.01, 10


def _cp(**kw):
    return pltpu.CompilerParams(vmem_limit_bytes=VMEM_LIMIT, **kw)


def _sds(shape, dtype):
    return jax.ShapeDtypeStruct(shape, dtype)


def _dot(a, b, dims, precision=None):
    return lax.dot_general(a, b, (dims, ((), ())), precision=precision, preferred_element_type=f32)


NN = ((1,), (0,))
NT = ((1,), (1,))
TN = ((0,), (0,))


def _rms(x, g):
    return x * lax.rsqrt(jnp.mean(x * x, axis=-1, keepdims=True) + EPS) * g


def _silu(x):
    return x * jax.nn.sigmoid(x)


def _mm(a, b, mode, tm, tn, out_dtype, name):
    if mode == "nn":
        (m, k), n = a.shape, b.shape[1]
        a_spec = pl.BlockSpec((tm, k), lambda i, j: (i, 0))
        b_spec = pl.BlockSpec((k, tn), lambda i, j: (0, j))
        dims = NN
    elif mode == "nt":
        (m, k), n = a.shape, b.shape[0]
        a_spec = pl.BlockSpec((tm, k), lambda i, j: (i, 0))
        b_spec = pl.BlockSpec((tn, k), lambda i, j: (j, 0))
        dims = NT
    else:
        (k, m), n = a.shape, b.shape[1]
        a_spec = pl.BlockSpec((k, tm), lambda i, j: (0, i))
        b_spec = pl.BlockSpec((k, tn), lambda i, j: (0, j))
        dims = TN
    assert m % tm == 0 and n % tn == 0, (name, m, n, tm, tn)

    def body(a_ref, b_ref, o_ref):
        o_ref[...] = _dot(a_ref[...], b_ref[...], dims).astype(out_dtype)

    return pl.pallas_call(
        body, grid=(m // tm, n // tn), in_specs=[a_spec, b_spec],
        out_specs=pl.BlockSpec((tm, tn), lambda i, j: (i, j)),
        out_shape=_sds((m, n), out_dtype), compiler_params=_cp(), name=name)(a, b)


def _mm_res_norm(a, b, res, gain, name, tm=512):
    (m, k), n = a.shape, b.shape[1]
    tm = min(tm, m)
    assert n == D and m % tm == 0, (name, m, n)

    def body(a_ref, b_ref, r_ref, *rest):
        xv = r_ref[...] + _dot(a_ref[...], b_ref[...], NN)
        if gain is None:
            rest[0][...] = xv
        else:
            g_ref, x_ref, h_ref = rest
            x_ref[...] = xv
            h_ref[...] = _rms(xv, g_ref[...]).astype(bf16)

    row = pl.BlockSpec((tm, D), lambda i: (i, 0))
    in_specs = [pl.BlockSpec((tm, k), lambda i: (i, 0)), pl.BlockSpec((k, D), lambda i: (0, 0)), row]
    args = [a, b, res]
    if gain is None:
        out_specs, out_shape = row, _sds((m, D), f32)
    else:
        in_specs.append(pl.BlockSpec((1, D), lambda i: (0, 0)))
        args.append(gain)
        out_specs, out_shape = [row, row], [_sds((m, D), f32), _sds((m, D), bf16)]
    return pl.pallas_call(body, grid=(m // tm,), in_specs=in_specs, out_specs=out_specs, out_shape=out_shape,
                          compiler_params=_cp(), name=name)(*args)


def _tile(n, cands):
    for c in cands:
        if n % c == 0:
            return c
    raise ValueError(n)


def _mm_auto(a, b, mode, out_dtype, name):
    m = a.shape[1] if mode == "tn" else a.shape[0]
    n = b.shape[0] if mode == "nt" else b.shape[1]
    return _mm(a, b, mode, _tile(m, (512, 256, 128)), _tile(n, (1024, 1408, 512, 384, 256, 128)), out_dtype, name)


def _mm_nn_slots(a, bs, out_dtype, name, after=()):
    (m, k), w = a.shape, bs.shape[2]
    tm = _tile(m, (1024, 512, 256, 128))

    def body(a_ref, b_ref, *rest):
        o_ref = rest[len(after)]
        o_ref[...] = _dot(a_ref[...], b_ref[...], NN).astype(out_dtype)

    return pl.pallas_call(
        body, grid=(m // tm, N_SLOTS),
        in_specs=[pl.BlockSpec((tm, k), lambda i, j: (i, 0)), pl.BlockSpec((None, k, w), lambda i, j: (j, 0, 0))]
        + [pl.BlockSpec(memory_space=pl.ANY)] * len(after),
        out_specs=pl.BlockSpec((tm, w), lambda i, j: (i, j)),
        out_shape=_sds((m, N_SLOTS * w), out_dtype), compiler_params=_cp(), name=name)(a, bs, *after)


def _mm_nt_rmsbwd(a, b, x, gain, dres, name, after=()):
    m, k = a.shape
    slots = b.ndim == 3
    assert b.shape[-2] == D
    tm = _tile(m, (512, 256, 128)) if k < 4096 else _tile(m, (256, 128))

    def body(a_ref, b_ref, x_ref, g_ref, r_ref, *rest):
        dx_ref, dxb_ref, dg_ref = rest[len(after):]
        if slots:
            w = b.shape[2]
            dh = _dot(a_ref[:, 0:w], b_ref[0], NT)
            for s in range(1, N_SLOTS):
                dh = dh + _dot(a_ref[:, s * w:(s + 1) * w], b_ref[s], NT)
        else:
            dh = _dot(a_ref[...], b_ref[...], NT)
        _, vjp = jax.vjp(_rms, x_ref[...], g_ref[...])
        dx, dg = vjp(dh)
        dx = dx + r_ref[...]
        dx_ref[...] = dx
        dxb_ref[...] = dx.astype(bf16)

        @pl.when(pl.program_id(0) == 0)
        def _():
            dg_ref[...] = jnp.zeros_like(dg_ref)

        dg_ref[...] += dg

    row = pl.BlockSpec((tm, D), lambda i: (i, 0))
    vec = pl.BlockSpec((1, D), lambda i: (0, 0))
    b_spec = pl.BlockSpec(b.shape, (lambda i: (0, 0, 0)) if slots else (lambda i: (0, 0)))
    return pl.pallas_call(
        body, grid=(m // tm,),
        in_specs=[pl.BlockSpec((tm, k), lambda i: (i, 0)), b_spec, row, vec, row] + [pl.BlockSpec(memory_space=pl.ANY)] * len(after),
        out_specs=[row, row, vec],
        out_shape=[_sds((m, D), f32), _sds((m, D), bf16), _sds((1, D), f32)],
        compiler_params=_cp(), name=name)(a, b, x, gain, dres, *after)


def _mm_tn_slots(a, b, name):
    (k, m), w = a.shape, b.shape[1] // N_SLOTS
    tm = _tile(m, (512, 256, 128))

    def body(a_ref, b_ref, o_ref):
        o_ref[...] = _dot(a_ref[...], b_ref[...], TN).astype(bf16)

    return pl.pallas_call(
        body, grid=(N_SLOTS, m // tm),
        in_specs=[pl.BlockSpec((k, tm), lambda j, i: (0, i)), pl.BlockSpec((k, w), lambda j, i: (0, j))],
        out_specs=pl.BlockSpec((None, tm, w), lambda j, i: (j, i, 0)),
        out_shape=_sds((N_SLOTS, m, w), bf16), compiler_params=_cp(), name=name)(a, b)


def _rmsnorm_fwd(x, gain, name, after=()):
    t = x.shape[0]

    def body(x_ref, g_ref, *rest):
        h_ref = rest[len(after)]
        h_ref[...] = _rms(x_ref[...], g_ref[...]).astype(bf16)

    return pl.pallas_call(
        body, grid=(t // TT,),
        in_specs=[pl.BlockSpec((TT, D), lambda i: (i, 0)), pl.BlockSpec((1, D), lambda i: (0, 0))]
        + [pl.BlockSpec(memory_space=pl.ANY)] * len(after),
        out_specs=pl.BlockSpec((TT, D), lambda i: (i, 0)),
        out_shape=_sds((t, D), bf16), compiler_params=_cp(), name=name)(x, gain, *after)


def _final_loss(x, gain, target, name):
    t = x.shape[0]

    def loss_fn(xv, gv, tv):
        e = _rms(xv, gv) - tv
        return 0.5 * jnp.sum(jnp.mean(e * e, axis=-1))

    def body(x_ref, g_ref, t_ref, loss_ref, dx_ref, dxb_ref, dg_ref):
        val, (dx, dg) = jax.value_and_grad(loss_fn, argnums=(0, 1))(x_ref[...], g_ref[...], t_ref[...])

        @pl.when(pl.program_id(0) == 0)
        def _():
            dg_ref[...] = jnp.zeros_like(dg_ref)
            loss_ref[...] = jnp.zeros_like(loss_ref)

        dx_ref[...] = dx
        dxb_ref[...] = dx.astype(bf16)
        dg_ref[...] += dg
        loss_ref[...] += jnp.full((1, 128), val, f32)

    row = pl.BlockSpec((TT, D), lambda i: (i, 0))
    vec = pl.BlockSpec((1, D), lambda i: (0, 0))
    return pl.pallas_call(
        body, grid=(t // TT,), in_specs=[row, vec, row],
        out_specs=[pl.BlockSpec((1, 128), lambda i: (0, 0)), row, row, vec],
        out_shape=[_sds((1, 128), f32), _sds((t, D), f32), _sds((t, D), bf16), _sds((1, D), f32)],
        compiler_params=_cp(), name=name)(x, gain, target)


def _halo_rows(dtype):
    return 16 if dtype == bf16 else 8


def _conv_fwd(x, col0, c, w, b, name, tc=256, val=None, val_col0=0):
    t = x.shape[0]
    width = w.shape[0]
    nt = t // TT
    hr = _halo_rows(x.dtype)
    hb = TT // hr

    def body(*refs):
        if val is None:
            x_ref, xh_ref, w_ref, b_ref, o_ref, xp = refs
        else:
            x_ref, xh_ref, w_ref, b_ref, v_ref, o_ref, act_ref, xp = refs
        i = pl.program_id(1)
        xp[0:8, :] = jnp.where(i == 0, 0.0, xh_ref[hr - 8:hr, :].astype(f32))
        xp[8:, :] = x_ref[...].astype(f32)
        acc = jnp.zeros((TT, tc), f32) + b_ref[...]
        for k in range(width):
            acc = acc + w_ref[k:k + 1, :] * xp[pl.ds(8 - (width - 1) + k, TT), :]
        o_ref[...] = acc
        if val is not None:
            act_ref[...] = (_silu(acc) * v_ref[...]).astype(bf16)

    in_specs = [
        pl.BlockSpec((TT, tc), lambda j, i: (i, j + col0)),
        pl.BlockSpec((hr, tc), lambda j, i: (jnp.maximum(i * hb - 1, 0), j + col0)),
        pl.BlockSpec((width, tc), lambda j, i: (0, j)),
        pl.BlockSpec((1, tc), lambda j, i: (0, j)),
    ]
    args = [x, x, w, b]
    out_specs = [pl.BlockSpec((TT, tc), lambda j, i: (i, j))]
    out_shape = [_sds((t, c), f32)]
    if val is not None:
        in_specs.append(pl.BlockSpec((TT, tc), lambda j, i: (i, j + val_col0)))
        args.append(val)
        out_specs.append(pl.BlockSpec((TT, tc), lambda j, i: (i, j)))
        out_shape.append(_sds((t, c), bf16))
    res = pl.pallas_call(
        body, grid=(c // tc, nt), in_specs=in_specs, out_specs=out_specs, out_shape=out_shape,
        scratch_shapes=[pltpu.VMEM((TT + 8, tc), f32)], compiler_params=_cp(), name=name)(*args)
    return res[0] if val is None else res


def _conv_bwd(dc, x, col0, w, name, tc=256, dx_dtype=bf16, into=None):
    t, c = dc.shape
    width = w.shape[0]
    nt = t // TT
    hb = TT // 8
    hr = _halo_rows(x.dtype)

    def body(dc_ref, dcn_ref, x_ref, xh_ref, w_ref, *rest):
        dx_ref, dw_ref, db_ref, dcp, xp = rest[(0 if into is None else 1):]
        i = pl.program_id(1)
        dcv = dc_ref[...]
        dcp[0:TT, :] = dcv
        dcp[TT:, :] = jnp.where(i == nt - 1, 0.0, dcn_ref[...])
        xp[0:8, :] = jnp.where(i == 0, 0.0, xh_ref[hr - 8:hr, :].astype(f32))
        xp[8:, :] = x_ref[...].astype(f32)

        @pl.when(i == 0)
        def _():
            dw_ref[...] = jnp.zeros_like(dw_ref)
            db_ref[...] = jnp.zeros_like(db_ref)

        acc = jnp.zeros((TT, tc), f32)
        for k in range(width):
            acc = acc + w_ref[k:k + 1, :] * dcp[pl.ds((width - 1) - k, TT), :]
            dw_ref[k:k + 1, :] += jnp.sum(dcv * xp[pl.ds(8 - (width - 1) + k, TT), :], axis=0, keepdims=True)
        dx_ref[...] = acc.astype(dx_dtype)
        db_ref[...] += jnp.sum(dcv, axis=0, keepdims=True)

    in_specs = [
        pl.BlockSpec((TT, tc), lambda j, i: (i, j)),
        pl.BlockSpec((8, tc), lambda j, i: (jnp.minimum((i + 1) * hb, t // 8 - 1), j)),
        pl.BlockSpec((TT, tc), lambda j, i: (i, j + col0)),
        pl.BlockSpec((hr, tc), lambda j, i: (jnp.maximum(i * (TT // hr) - 1, 0), j + col0)),
        pl.BlockSpec((width, tc), lambda j, i: (0, j)),
    ]
    out_specs = [
        pl.BlockSpec((TT, tc), lambda j, i: (i, j)),
        pl.BlockSpec((8, tc), lambda j, i: (0, j)),
        pl.BlockSpec((1, tc), lambda j, i: (0, j)),
    ]
    args, aliases, dx_shape = [dc, dc, x, x, w], {}, _sds((t, c), dx_dtype)
    if into is not None:
        in_specs.append(pl.BlockSpec(memory_space=pl.ANY))
        args.append(into)
        aliases, dx_shape = {5: 0}, _sds(into.shape, into.dtype)
    return pl.pallas_call(
        body, grid=(c // tc, nt), in_specs=in_specs, out_specs=out_specs,
        out_shape=[dx_shape, _sds((8, c), f32), _sds((1, c), f32)], input_output_aliases=aliases,
        scratch_shapes=[pltpu.VMEM((TT + 8, tc), f32), pltpu.VMEM((TT + 8, tc), f32)],
        compiler_params=_cp(), name=name)(*args)


def _bdot_impl(a, b, dims):
    return _dot(a.astype(bf16), b.astype(bf16), dims)


@functools.partial(jax.custom_vjp, nondiff_argnums=(2,))
def _bdot(a, b, dims):
    return _bdot_impl(a, b, dims)


def _bdot_fwd(a, b, dims):
    return _bdot_impl(a, b, dims), (a, b)


def _bdot_bwd(dims, res, ct):
    a, b = res
    if dims == NN:
        return _bdot_impl(ct, b, NT), _bdot_impl(a, ct, TN)
    if dims == NT:
        return _bdot_impl(ct, b, NN), _bdot_impl(ct, a, TN)
    return _bdot_impl(b, ct, NT), _bdot_impl(a, ct, NN)


_bdot.defvjp(_bdot_fwd, _bdot_bwd)


def _split2(a):
    hi = a.astype(bf16)
    return hi, (a - hi.astype(f32)).astype(bf16)


def _dot3_impl(a, b, dims):
    a_hi, a_lo = _split2(a)
    b_hi, b_lo = _split2(b)
    return (_dot(a_hi, b_hi, dims) + _dot(a_hi, b_lo, dims)) + _dot(a_lo, b_hi, dims)


def _tril_dot_impl(tril, x, dims):
    t = tril.astype(bf16)
    x1 = x.astype(bf16)
    r1 = x - x1.astype(f32)
    x2 = r1.astype(bf16)
    x3 = (r1 - x2.astype(f32)).astype(bf16)
    return (_dot(t, x3, dims) + _dot(t, x2, dims)) + _dot(t, x1, dims)


@jax.custom_vjp
def _cumsum_rows(tril, x):
    return _tril_dot_impl(tril, x, NN)


def _cumsum_rows_fwd(tril, x):
    return _tril_dot_impl(tril, x, NN), tril


def _cumsum_rows_bwd(tril, ct):
    return jnp.zeros_like(tril), _tril_dot_impl(tril, ct, TN)


_cumsum_rows.defvjp(_cumsum_rows_fwd, _cumsum_rows_bwd)


def _nilpotent_inverse(n):
    hs = range(len(n))
    c = n[0].shape[0]
    eye = (lax.broadcasted_iota(jnp.int32, (c, c), 0) == lax.broadcasted_iota(jnp.int32, (c, c), 1)).astype(f32)
    t = [eye + n[h] for h in hs]
    p = n
    for _ in range(5):
        p = [_dot3_impl(p[h], p[h], NN) for h in hs]
        tp = [_dot3_impl(t[h], p[h], NN) for h in hs]
        t = [t[h] + tp[h] for h in hs]
    return t


@jax.custom_vjp
def _nilpotent_solve(n, rhs):
    t = _nilpotent_inverse(n)
    return [_dot3_impl(t[h], rhs[h], NN) for h in range(len(n))]


def _nilpotent_solve_fwd(n, rhs):
    t = _nilpotent_inverse(n)
    sol = [_dot3_impl(t[h], rhs[h], NN) for h in range(len(n))]
    return sol, (t, sol)


def _nilpotent_solve_bwd(res, ct):
    t, sol = res
    hs = range(len(t))
    d_rhs = [_dot3_impl(t[h], ct[h], TN) for h in hs]
    d_n = [_dot3_impl(d_rhs[h], sol[h], NT) for h in hs]
    return d_n, d_rhs


_nilpotent_solve.defvjp(_nilpotent_solve_fwd, _nilpotent_solve_bwd)


def _gdn_intra4(cq, ck, cv, ba, alog_v, dtb_v):
    c = GDN_CHUNK
    hs = range(NH)
    lane = lax.broadcasted_iota(jnp.int32, (1, HEAD), 1)
    mb = [(lane == h).astype(f32) for h in hs]
    ma = [(lane == h + NH).astype(f32) for h in hs]
    beta = [jax.nn.sigmoid(jnp.sum(ba * mb[h], axis=1, keepdims=True)) for h in hs]
    alpha = [jnp.sum(ba * ma[h], axis=1, keepdims=True) for h in hs]
    alog = [jnp.sum(alog_v * ma[h], axis=1, keepdims=True) for h in hs]
    dtb = [jnp.sum(dtb_v * ma[h], axis=1, keepdims=True) for h in hs]
    g = [-jnp.exp(alog[h]) * jax.nn.softplus(alpha[h] + dtb[h]) for h in hs]
    q = [_silu(cq[h]) for h in hs]
    q = [q[h] * lax.rsqrt(jnp.sum(q[h] * q[h], axis=-1, keepdims=True) + EPS) * (HEAD ** -0.5) for h in hs]
    k = [_silu(ck[h]) for h in hs]
    k = [k[h] * lax.rsqrt(jnp.sum(k[h] * k[h], axis=-1, keepdims=True) + EPS) for h in hs]
    v = [_silu(cv[h]) for h in hs]
    row = lax.broadcasted_iota(jnp.int32, (c, c), 0)
    col = lax.broadcasted_iota(jnp.int32, (c, c), 1)
    causal = row >= col
    tril = causal.astype(f32)
    gc = [_cumsum_rows(tril, jnp.broadcast_to(g[h], (c, HEAD))) for h in hs]
    gcc = [_cumsum_rows(tril, jnp.broadcast_to(g[h], (c, c))) for h in hs]
    decay = [jnp.where(causal, jnp.exp(jnp.where(causal, gcc[h] - gcc[h].T, 0.0)), 0.0) for h in hs]
    kb = [k[h] * beta[h] for h in hs]
    kk = [_bdot(kb[h], k[h], NT) for h in hs]
    p = [-jnp.where(row > col, kk[h] * decay[h], 0.0) for h in hs]
    egc = [jnp.exp(gc[h]) for h in hs]
    sol = _nilpotent_solve(p, [jnp.concatenate([v[h] * beta[h], kb[h] * egc[h]], axis=1) for h in hs])
    qk = [_bdot(q[h], k[h], NT) for h in hs]
    attn = [qk[h] * decay[h] for h in hs]
    rowv = lax.broadcasted_iota(jnp.int32, (c, 1), 0)
    gc_last = [jnp.sum(jnp.where(rowv == c - 1, gc[h], 0.0), axis=0, keepdims=True) for h in hs]
    return ([sol[h][:, :HEAD] for h in hs], [sol[h][:, HEAD:] for h in hs], [q[h] * egc[h] for h in hs],
            [k[h] * jnp.exp(gc_last[h] - gc[h]) for h in hs], attn, [jnp.exp(gc_last[h]) for h in hs])


def _gdn_seq4(u, w, q_dec, k_dec, attn, dl, z, s, gain):
    hs = range(NH)
    ws = [_bdot(w[h], s[h], NN) for h in hs]
    qs = [_bdot(q_dec[h], s[h], NN) for h in hs]
    v_new = [u[h] - ws[h] for h in hs]
    av = [_bdot(attn[h], v_new[h], NN) for h in hs]
    kv = [_bdot(k_dec[h], v_new[h], TN) for h in hs]
    o = [_rms(qs[h] + av[h], gain) * _silu(z[h]) for h in hs]
    return o, [s[h] * dl[h] + kv[h] for h in hs]


def _hsl(h):
    return slice(h * HEAD, (h + 1) * HEAD)


def _gdn2_fwd(conv, p, alog_v, dtb_v, gain, name):
    t = conv.shape[0]
    c = GDN_CHUNK
    nch = t // c
    w512 = NH * HEAD
    wide = lambda off: pl.BlockSpec((c, w512), lambda n: (n, off))
    vec = pl.BlockSpec((1, HEAD), lambda n: (0, 0))
    attn_spec = pl.BlockSpec((1, NH, c, c), lambda n: (n, 0, 0, 0))
    dl_spec = pl.BlockSpec((1, NH, HEAD), lambda n: (n, 0, 0))

    cps = GDN_INTRA_CHUNKS
    iwide = lambda off: pl.BlockSpec((cps * c, w512), lambda n: (n, off))

    def intra(cq, ck, cv, ba, al, dt, u_ref, w_ref, qd_ref, kd_ref, at_ref, dl_ref):
        for ci in range(cps):
            rows = slice(ci * c, (ci + 1) * c)
            u, w, qd, kd, at, dl = _gdn_intra4([cq[rows, _hsl(h)] for h in range(NH)], [ck[rows, _hsl(h)] for h in range(NH)],
                                               [cv[rows, _hsl(h)] for h in range(NH)], ba[rows, :], al[...], dt[...])
            for h in range(NH):
                u_ref[rows, _hsl(h)] = u[h]
                w_ref[rows, _hsl(h)] = w[h]
                qd_ref[rows, _hsl(h)] = qd[h]
                kd_ref[rows, _hsl(h)] = kd[h]
                at_ref[ci, h] = at[h]
                dl_ref[ci, h:h + 1, :] = dl[h]

    u, w, qd, kd, at, dl = pl.pallas_call(
        intra, grid=(nch // cps,),
        in_specs=[iwide(0), iwide(1), iwide(2), pl.BlockSpec((cps * c, HEAD), lambda n: (n, C_BA // HEAD)), vec, vec],
        out_specs=[iwide(0)] * 4 + [pl.BlockSpec((cps, NH, c, c), lambda n: (n, 0, 0, 0)),
                                    pl.BlockSpec((cps, NH, HEAD), lambda n: (n, 0, 0))],
        out_shape=[_sds((t, w512), f32)] * 4 + [_sds((nch, NH, c, c), f32), _sds((nch, NH, HEAD), f32)],
        compiler_params=_cp(), name=name + "_intra")(conv, conv, conv, p, alog_v, dtb_v)

    def seq(u_ref, w_ref, qd_ref, kd_ref, at_ref, dl_ref, z_ref, gn, o_ref, ss_ref, s_scr):
        @pl.when(pl.program_id(0) == 0)
        def _():
            s_scr[...] = jnp.zeros_like(s_scr)

        hs = range(NH)
        s = [s_scr[h] for h in hs]
        for h in hs:
            ss_ref[0, h] = s[h]
        o, s_new = _gdn_seq4([u_ref[:, _hsl(h)] for h in hs], [w_ref[:, _hsl(h)] for h in hs], [qd_ref[:, _hsl(h)] for h in hs],
                             [kd_ref[:, _hsl(h)] for h in hs], [at_ref[0, h] for h in hs], [dl_ref[0, h:h + 1, :] for h in hs],
                             [z_ref[:, _hsl(h)] for h in hs], s, gn[...])
        for h in hs:
            o_ref[:, _hsl(h)] = o[h].astype(bf16)
            s_scr[h] = s_new[h]

    o, states = pl.pallas_call(
        seq, grid=(nch,),
        in_specs=[wide(0)] * 4 + [attn_spec, dl_spec, wide(C_Z // w512), vec],
        out_specs=[wide(0), pl.BlockSpec((1, NH, HEAD, HEAD), lambda n: (n, 0, 0, 0))],
        out_shape=[_sds((t, w512), bf16), _sds((nch, NH, HEAD, HEAD), f32)],
        scratch_shapes=[pltpu.VMEM((NH, HEAD, HEAD), f32)],
        compiler_params=_cp(), name=name + "_seq")(u, w, qd, kd, at, dl, p, gain)
    return o, dict(u=u, w=w, qd=qd, kd=kd, at=at, dl=dl, states=states)


def _gdn2_bwd(conv, p, alog_v, dtb_v, gain, saved, do, name):
    t = conv.shape[0]
    c = GDN_CHUNK
    nch = t // c
    w512 = NH * HEAD
    rwide = lambda off: pl.BlockSpec((c, w512), lambda n: (nch - 1 - n, off))
    rvec = pl.BlockSpec((1, HEAD), lambda n: (0, 0))
    rattn = pl.BlockSpec((1, NH, c, c), lambda n: (nch - 1 - n, 0, 0, 0))
    rdl = pl.BlockSpec((1, NH, HEAD), lambda n: (nch - 1 - n, 0, 0))

    def seq_bwd(u_ref, w_ref, qd_ref, kd_ref, at_ref, dl_ref, z_ref, gn, ss_ref, do_ref,
                du_ref, dw_ref, dqd_ref, dkd_ref, dat_ref, ddl_ref, dz_ref, dgn_ref, ds_scr):
        @pl.when(pl.program_id(0) == 0)
        def _():
            ds_scr[...] = jnp.zeros_like(ds_scr)
            dgn_ref[...] = jnp.zeros_like(dgn_ref)

        hs = range(NH)
        _, vjp = jax.vjp(_gdn_seq4, [u_ref[:, _hsl(h)] for h in hs], [w_ref[:, _hsl(h)] for h in hs],
                         [qd_ref[:, _hsl(h)] for h in hs], [kd_ref[:, _hsl(h)] for h in hs], [at_ref[0, h] for h in hs],
                         [dl_ref[0, h:h + 1, :] for h in hs], [z_ref[:, _hsl(h)] for h in hs], [ss_ref[0, h] for h in hs], gn[...])
        du, dw, dqd, dkd, dat, ddl, dz, ds, dg = vjp(([do_ref[:, _hsl(h)] for h in hs], [ds_scr[h] for h in hs]))
        for h in hs:
            du_ref[:, _hsl(h)] = du[h]
            dw_ref[:, _hsl(h)] = dw[h]
            dqd_ref[:, _hsl(h)] = dqd[h]
            dkd_ref[:, _hsl(h)] = dkd[h]
            dat_ref[0, h] = dat[h]
            ddl_ref[0, h:h + 1, :] = ddl[h]
            dz_ref[:, _hsl(h)] = dz[h]
            ds_scr[h] = ds[h]
        dgn_ref[...] += dg

    du, dw, dqd, dkd, dat, ddl, dz, dgn = pl.pallas_call(
        seq_bwd, grid=(nch,),
        in_specs=[rwide(0)] * 4 + [rattn, rdl, rwide(C_Z // w512), rvec,
                                   pl.BlockSpec((1, NH, HEAD, HEAD), lambda n: (nch - 1 - n, 0, 0, 0)), rwide(0)],
        out_specs=[rwide(0)] * 4 + [rattn, rdl, rwide(0), rvec],
        out_shape=[_sds((t, w512), f32)] * 4 + [_sds((nch, NH, c, c), f32), _sds((nch, NH, HEAD), f32),
                                                _sds((t, w512), f32), _sds((1, HEAD), f32)],
        scratch_shapes=[pltpu.VMEM((NH, HEAD, HEAD), f32)],
        compiler_params=_cp(), name=name + "_seq")(
            saved["u"], saved["w"], saved["qd"], saved["kd"], saved["at"], saved["dl"], p, gain, saved["states"], do)

    cps = GDN_INTRA_CHUNKS
    wide = lambda off: pl.BlockSpec((cps * c, w512), lambda n: (n, off))
    vec = pl.BlockSpec((1, HEAD), lambda n: (0, 0))
    attn_spec = pl.BlockSpec((cps, NH, c, c), lambda n: (n, 0, 0, 0))
    dl_spec = pl.BlockSpec((cps, NH, HEAD), lambda n: (n, 0, 0))

    def intra_bwd(cq, ck, cv, ba, al, dt, du_ref, dw_ref, dqd_ref, dkd_ref, dat_ref, ddl_ref,
                  dc_ref, dba_ref, dal_ref, ddt_ref):
        @pl.when(pl.program_id(0) == 0)
        def _():
            dal_ref[...] = jnp.zeros_like(dal_ref)
            ddt_ref[...] = jnp.zeros_like(ddt_ref)

        dal = jnp.zeros((1, HEAD), f32)
        ddt = jnp.zeros((1, HEAD), f32)
        for ci in range(cps):
            rows = slice(ci * c, (ci + 1) * c)
            hs = range(NH)
            _, vjp = jax.vjp(_gdn_intra4, [cq[rows, _hsl(h)] for h in hs], [ck[rows, _hsl(h)] for h in hs],
                             [cv[rows, _hsl(h)] for h in hs], ba[rows, :], al[...], dt[...])
            g_q, g_k, g_v, g_ba, g_al, g_dt = vjp((
                [du_ref[rows, _hsl(h)] for h in hs], [dw_ref[rows, _hsl(h)] for h in hs], [dqd_ref[rows, _hsl(h)] for h in hs],
                [dkd_ref[rows, _hsl(h)] for h in hs], [dat_ref[ci, h] for h in hs], [ddl_ref[ci, h:h + 1, :] for h in hs]))
            for h in hs:
                dc_ref[rows, _hsl(h)] = g_q[h]
                dc_ref[rows, _hsl(NH + h)] = g_k[h]
                dc_ref[rows, _hsl(2 * NH + h)] = g_v[h]
            dal = dal + g_al
            ddt = ddt + g_dt
            dba_ref[rows, :] = g_ba
        dal_ref[...] += dal
        ddt_ref[...] += ddt

    dconv, dba, dal, ddt = pl.pallas_call(
        intra_bwd, grid=(nch // cps,),
        in_specs=[wide(0), wide(1), wide(2), pl.BlockSpec((cps * c, HEAD), lambda n: (n, C_BA // HEAD)), vec, vec]
        + [wide(0)] * 4 + [attn_spec, dl_spec],
        out_specs=[pl.BlockSpec((cps * c, 3 * w512), lambda n: (n, 0)), pl.BlockSpec((cps * c, HEAD), lambda n: (n, 0)),
                   vec, vec],
        out_shape=[_sds((t, 3 * w512), f32), _sds((t, HEAD), f32), _sds((1, HEAD), f32), _sds((1, HEAD), f32)],
        compiler_params=_cp(), name=name + "_intra")(conv, conv, conv, p, alog_v, dtb_v, du, dw, dqd, dkd, dat, ddl)
    return dconv, dz, dba, dal, ddt, dgn


def _hgrn_intra(qb, fb, ib, lb):
    c = HGRN_CHUNK
    ns = range(len(qb))
    f = [lb + (1.0 - lb) * jax.nn.sigmoid(fb[i]) for i in ns]
    logf = [jnp.log(jnp.maximum(f[i], F_FLOOR)) for i in ns]
    k = [1.0 - f[i] for i in ns]
    q = [_silu(qb[i]) for i in ns]
    row = lax.broadcasted_iota(jnp.int32, (c, c), 0)
    col = lax.broadcasted_iota(jnp.int32, (c, c), 1)
    tril = (row >= col).astype(f32)
    b = [_cumsum_rows(tril, logf[i]) for i in ns]
    ri = lax.broadcasted_iota(jnp.int32, (c, 1), 0)
    o = [jnp.zeros((c, HEAD), f32) for _ in ns]
    for j in range(c):
        mj = ri == j
        ok = ri >= j
        bj = [jnp.sum(jnp.where(mj, b[i], 0.0), axis=0, keepdims=True) for i in ns]
        kj = [jnp.sum(jnp.where(mj, k[i], 0.0), axis=0, keepdims=True) for i in ns]
        vj = [jnp.sum(jnp.where(mj, ib[i], 0.0), axis=0, keepdims=True) for i in ns]
        e = [jnp.where(ok, jnp.exp(jnp.where(ok, b[i] - bj[i], 0.0)), 0.0) for i in ns]
        s = [jnp.sum(q[i] * kj[i] * e[i], axis=1, keepdims=True) for i in ns]
        o = [o[i] + s[i] * vj[i] for i in ns]
    b_last = [jnp.sum(jnp.where(ri == c - 1, b[i], 0.0), axis=0, keepdims=True) for i in ns]
    return (o, [q[i] * jnp.exp(b[i]) for i in ns], [k[i] * jnp.exp(b_last[i] - b[i]) for i in ns],
            [jnp.exp(b_last[i]) for i in ns])


def _hgrn_seq_heads(o_intra, q_dec, k_dec, dl, v, gb, st, gain):
    hs = range(len(st))
    qs = [_bdot(q_dec[h], st[h], NT) for h in hs]
    vk = [_bdot(v[h], k_dec[h], TN) for h in hs]
    o = [_rms(o_intra[h] + qs[h], gain) * _silu(gb[h]) for h in hs]
    return o, [st[h] * dl[h] + vk[h] for h in hs]


def _hgrn_tiles(ref, rows):
    return [[ref[rw, _hsl(h)] for rw in rows] for h in range(NH)]


def _hgrn_fwd(p, lb, gain, name):
    t = p.shape[0]
    r = HGRN_STEP
    ns = t // r
    nsub = r // HGRN_CHUNK
    w512 = NH * HEAD
    blk = lambda off: pl.BlockSpec((r, w512), lambda n: (n, off // w512))

    def body(qb, fb, ib, gb, lb_ref, gn, o_ref, ss_ref, s_scr):
        @pl.when(pl.program_id(0) == 0)
        def _():
            s_scr[...] = jnp.zeros_like(s_scr)

        hs = range(NH)
        rows = [pl.ds(ch * HGRN_CHUNK, HGRN_CHUNK) for ch in range(nsub)]
        q_t, f_t, v_t, g_t = (_hgrn_tiles(ref, rows) for ref in (qb, fb, ib, gb))
        intra = [_hgrn_intra(q_t[h], f_t[h], v_t[h], lb_ref[:, _hsl(h)]) for h in hs]
        st = [s_scr[h] for h in hs]
        for h in hs:
            ss_ref[0, h] = st[h]
        for ch in range(nsub):
            o, st = _hgrn_seq_heads([intra[h][0][ch] for h in hs], [intra[h][1][ch] for h in hs], [intra[h][2][ch] for h in hs],
                                    [intra[h][3][ch] for h in hs], [v_t[h][ch] for h in hs], [g_t[h][ch] for h in hs], st, gn[...])
            for h in hs:
                o_ref[rows[ch], _hsl(h)] = o[h].astype(bf16)
        for h in hs:
            s_scr[h] = st[h]

    return pl.pallas_call(
        body, grid=(ns,),
        in_specs=[blk(C_QB), blk(C_FB), blk(C_IB), blk(C_GB),
                  pl.BlockSpec((1, w512), lambda n: (0, 0)), pl.BlockSpec((1, HEAD), lambda n: (0, 0))],
        out_specs=[pl.BlockSpec((r, w512), lambda n: (n, 0)),
                   pl.BlockSpec((1, NH, HEAD, HEAD), lambda n: (n, 0, 0, 0))],
        out_shape=[_sds((t, w512), bf16), _sds((ns, NH, HEAD, HEAD), f32)],
        scratch_shapes=[pltpu.VMEM((NH, HEAD, HEAD), f32)],
        compiler_params=_cp(), name=name)(p, p, p, p, lb, gain)


def _hgrn_bwd(p, lb, gain, states, do, name):
    t = p.shape[0]
    r = HGRN_STEP
    ns = t // r
    nsub = r // HGRN_CHUNK
    w512 = NH * HEAD
    blk = lambda off: pl.BlockSpec((r, w512), lambda n: (ns - 1 - n, off // w512))

    def body(qb, fb, ib, gb, lb_ref, gn, ss_ref, do_ref, dp_ref, dlb_ref, dgn_ref, ds_scr, st_scr):
        @pl.when(pl.program_id(0) == 0)
        def _():
            ds_scr[...] = jnp.zeros_like(ds_scr)
            dgn_ref[...] = jnp.zeros_like(dgn_ref)

        hs = range(NH)
        gnv = gn[...]
        rows = [pl.ds(ch * HGRN_CHUNK, HGRN_CHUNK) for ch in range(nsub)]
        q_t, f_t, v_t, g_t = (_hgrn_tiles(ref, rows) for ref in (qb, fb, ib, gb))
        intra, vjp_intra = [], []
        for h in hs:
            out, fn = jax.vjp(_hgrn_intra, q_t[h], f_t[h], v_t[h], lb_ref[:, _hsl(h)])
            intra.append(out)
            vjp_intra.append(fn)
        at = lambda part, ch: [intra[h][part][ch] for h in hs]
        st = [ss_ref[0, h] for h in hs]
        for ch in range(nsub):
            for h in hs:
                st_scr[ch * NH + h] = st[h]
            if ch < nsub - 1:
                vk = [_bdot(v_t[h][ch], intra[h][2][ch], TN) for h in hs]
                st = [st[h] * intra[h][3][ch] + vk[h] for h in hs]
        ds = [ds_scr[h] for h in hs]
        dgn = jnp.zeros((1, HEAD), f32)
        cots = [[[None] * nsub for _ in range(4)] for _ in hs]
        d_v = [[None] * nsub for _ in hs]
        for ch in reversed(range(nsub)):
            _, vjp = jax.vjp(_hgrn_seq_heads, at(0, ch), at(1, ch), at(2, ch), at(3, ch), [v_t[h][ch] for h in hs],
                             [g_t[h][ch] for h in hs], [st_scr[ch * NH + h] for h in hs], gnv)
            d_oi, d_qd, d_kd, d_dl, d_vv, g_g, ds, g_gn = vjp(([do_ref[rows[ch], _hsl(h)] for h in hs], ds))
            for h in hs:
                cots[h][0][ch], cots[h][1][ch], cots[h][2][ch], cots[h][3][ch] = d_oi[h], d_qd[h], d_kd[h], d_dl[h]
                d_v[h][ch] = d_vv[h]
                dp_ref[rows[ch], pl.ds(3 * w512 + h * HEAD, HEAD)] = g_g[h].astype(bf16)
            dgn = dgn + g_gn
        for h in hs:
            g_q, g_f, g_i, g_lb = vjp_intra[h](tuple(cots[h]))
            for ch in range(nsub):
                dp_ref[rows[ch], pl.ds(h * HEAD, HEAD)] = g_q[ch].astype(bf16)
                dp_ref[rows[ch], pl.ds(w512 + h * HEAD, HEAD)] = g_f[ch].astype(bf16)
                dp_ref[rows[ch], pl.ds(2 * w512 + h * HEAD, HEAD)] = (g_i[ch] + d_v[h][ch]).astype(bf16)
            dlb_ref[0, :, _hsl(h)] = g_lb
            ds_scr[h] = ds[h]
        dgn_ref[...] += dgn

    return pl.pallas_call(
        body, grid=(ns,),
        in_specs=[blk(C_QB), blk(C_FB), blk(C_IB), blk(C_GB),
                  pl.BlockSpec((1, w512), lambda n: (0, 0)), pl.BlockSpec((1, HEAD), lambda n: (0, 0)),
                  pl.BlockSpec((1, NH, HEAD, HEAD), lambda n: (ns - 1 - n, 0, 0, 0)),
                  pl.BlockSpec((r, w512), lambda n: (ns - 1 - n, 1))],
        out_specs=[pl.BlockSpec((r, 4 * w512), lambda n: (ns - 1 - n, 0)),
                   pl.BlockSpec((1, 1, w512), lambda n: (n, 0, 0)),
                   pl.BlockSpec((1, HEAD), lambda n: (0, 0))],
        out_shape=[_sds((t, 4 * w512), bf16), _sds((ns, 1, w512), f32), _sds((1, HEAD), f32)],
        scratch_shapes=[pltpu.VMEM((NH, HEAD, HEAD), f32), pltpu.VMEM((nsub * NH, HEAD, HEAD), f32)],
        compiler_params=_cp(), name=name)(p, p, p, p, lb, gain, states, do)


def _lru_gates(xb, wa, wx, ba, bx, lam):
    xh = xb.astype(bf16)
    r = jax.nn.sigmoid(_dot(xh, wa.astype(bf16), NN) + ba)
    i = jax.nn.sigmoid(_dot(xh, wx.astype(bf16), NN) + bx)
    log_a = -RG_C * r * jax.nn.softplus(-lam)
    a = jnp.exp(log_a)
    t2 = 2.0 * log_a
    series = -t2 * (1.0 + t2 * (0.5 + t2 * (1.0 / 6.0 + t2 * (1.0 / 24.0))))
    om = jnp.where(t2 > -1e-2, series, 1.0 - jnp.exp(t2))
    u = jnp.sqrt(jnp.maximum(om, 0.0)) * (i * xb)
    return a, u


def _lru_gates_fwd(xc, wa, wx, ba, bx, lam, name):
    t = xc.shape[0]
    blk = pl.BlockSpec((TT, LRU_BLOCK), lambda h, i: (i, h))
    wsp = pl.BlockSpec((1, LRU_BLOCK, LRU_BLOCK), lambda h, i: (h, 0, 0))
    vsp = pl.BlockSpec((1, LRU_BLOCK), lambda h, i: (0, h))

    def body(x_ref, wa_ref, wx_ref, ba_ref, bx_ref, lam_ref, a_ref, u_ref):
        a, u = _lru_gates(x_ref[...], wa_ref[0], wx_ref[0], ba_ref[...], bx_ref[...], lam_ref[...])
        a_ref[...] = a
        u_ref[...] = u

    return pl.pallas_call(
        body, grid=(NH, t // TT), in_specs=[blk, wsp, wsp, vsp, vsp, vsp], out_specs=[blk, blk],
        out_shape=[_sds((t, D), f32)] * 2, compiler_params=_cp(), name=name)(xc, wa, wx, ba, bx, lam)


def _lru_gates_bwd(xc, wa, wx, ba, bx, lam, da, du, name):
    t = xc.shape[0]
    blk = pl.BlockSpec((TT, LRU_BLOCK), lambda h, i: (i, h))
    wsp = pl.BlockSpec((1, LRU_BLOCK, LRU_BLOCK), lambda h, i: (h, 0, 0))
    vsp = pl.BlockSpec((1, LRU_BLOCK), lambda h, i: (0, h))

    def body(x_ref, wa_ref, wx_ref, ba_ref, bx_ref, lam_ref, da_ref, du_ref,
             dx_ref, dwa_ref, dwx_ref, dba_ref, dbx_ref, dlam_ref):
        @pl.when(pl.program_id(1) == 0)
        def _():
            for r in (dwa_ref, dwx_ref, dba_ref, dbx_ref, dlam_ref):
                r[...] = jnp.zeros_like(r)

        _, vjp = jax.vjp(_lru_gates, x_ref[...], wa_ref[0], wx_ref[0], ba_ref[...], bx_ref[...], lam_ref[...])
        dx, dwa, dwx, dba, dbx, dlam = vjp((da_ref[...], du_ref[...]))
        dx_ref[...] = dx
        dwa_ref[0] += dwa
        dwx_ref[0] += dwx
        dba_ref[...] += dba
        dbx_ref[...] += dbx
        dlam_ref[...] += dlam

    return pl.pallas_call(
        body, grid=(NH, t // TT), in_specs=[blk, wsp, wsp, vsp, vsp, vsp, blk, blk],
        out_specs=[blk, wsp, wsp, vsp, vsp, vsp],
        out_shape=[_sds((t, D), f32), _sds((NH, LRU_BLOCK, LRU_BLOCK), f32), _sds((NH, LRU_BLOCK, LRU_BLOCK), f32),
                   _sds((1, D), f32), _sds((1, D), f32), _sds((1, D), f32)],
        compiler_params=_cp(), name=name)(xc, wa, wx, ba, bx, lam, da, du)


_SCAN_SHIFTS = (1, 2, 4, 8, 16, 32, 64, 128)
_SCAN_PAD = 128


def _gelu(y):
    return jax.nn.gelu(y, approximate=True)


def _lru_scan_fwd(a, u, p2, name):
    t = a.shape[0]
    tc = 128
    blk = pl.BlockSpec((TT, tc), lambda j, i: (i, j))

    def body(a_ref, u_ref, y_ref, h_ref, hg_ref, a_s, b_s, carry):
        i = pl.program_id(1)

        @pl.when(i == 0)
        def _():
            carry[...] = jnp.zeros_like(carry)
            a_s[0:_SCAN_PAD, :] = jnp.ones((_SCAN_PAD, tc), f32)
            b_s[0:_SCAN_PAD, :] = jnp.zeros((_SCAN_PAD, tc), f32)

        av, bv = a_ref[...], u_ref[...]
        for s in _SCAN_SHIFTS:
            if s < 8:
                a_s[_SCAN_PAD:, :] = av
                b_s[_SCAN_PAD:, :] = bv
                ash = a_s[pl.ds(_SCAN_PAD - s, TT), :]
                bsh = b_s[pl.ds(_SCAN_PAD - s, TT), :]
            else:
                ash = jnp.concatenate([jnp.ones((s, tc), f32), av[:TT - s]], axis=0)
                bsh = jnp.concatenate([jnp.zeros((s, tc), f32), bv[:TT - s]], axis=0)
            bv = bv + av * bsh
            av = av * ash
        h = bv + av * carry[7:8, :]
        h_ref[...] = h
        hg_ref[...] = (h * _gelu(y_ref[...])).astype(bf16)
        carry[...] = h[TT - 8:, :]

    return pl.pallas_call(
        body, grid=(D // tc, t // TT), in_specs=[blk, blk, blk], out_specs=[blk, blk],
        out_shape=[_sds((t, D), f32), _sds((t, D), bf16)],
        scratch_shapes=[pltpu.VMEM((_SCAN_PAD + TT, tc), f32), pltpu.VMEM((_SCAN_PAD + TT, tc), f32),
                        pltpu.VMEM((8, tc), f32)],
        compiler_params=_cp(), name=name)(a, u, p2)


def _lru_scan_bwd(a, h, p2, dhg, name):
    t = a.shape[0]
    tc = 128
    nt = t // TT
    hb = TT // 8
    rblk = pl.BlockSpec((TT, tc), lambda j, i: (nt - 1 - i, j))

    def body(a_ref, an_ref, h_ref, hp_ref, y_ref, dhg_ref, du_ref, da_ref, dy_ref, a_s, b_s, ap, hp, carry):
        i = pl.program_id(1)

        @pl.when(i == 0)
        def _():
            carry[...] = jnp.zeros_like(carry)
            a_s[TT:, :] = jnp.ones((_SCAN_PAD, tc), f32)
            b_s[TT:, :] = jnp.zeros((_SCAN_PAD, tc), f32)

        ap[0:TT, :] = a_ref[...]
        ap[TT:, :] = jnp.where(i == 0, 0.0, an_ref[...])
        hp[0:8, :] = jnp.where(i == nt - 1, 0.0, hp_ref[...])
        hp[8:, :] = h_ref[...]
        y = y_ref[...]
        gate, gvjp = jax.vjp(_gelu, y)
        dhg_v = dhg_ref[...]
        dy_ref[...] = gvjp(dhg_v * h_ref[...])[0]
        av = ap[pl.ds(1, TT), :]
        bv = dhg_v * gate
        for s in _SCAN_SHIFTS:
            if s < 8:
                a_s[0:TT, :] = av
                b_s[0:TT, :] = bv
                ash = a_s[pl.ds(s, TT), :]
                bsh = b_s[pl.ds(s, TT), :]
            else:
                ash = jnp.concatenate([av[s:], jnp.ones((s, tc), f32)], axis=0)
                bsh = jnp.concatenate([bv[s:], jnp.zeros((s, tc), f32)], axis=0)
            bv = bv + av * bsh
            av = av * ash
        g = bv + av * carry[0:1, :]
        du_ref[...] = g
        da_ref[...] = g * hp[pl.ds(7, TT), :]
        carry[...] = g[0:8, :]

    in_specs = [
        rblk,
        pl.BlockSpec((8, tc), lambda j, i: (jnp.minimum((nt - i) * hb, t // 8 - 1), j)),
        rblk,
        pl.BlockSpec((8, tc), lambda j, i: (jnp.maximum((nt - 1 - i) * hb - 1, 0), j)),
        rblk, rblk,
    ]
    return pl.pallas_call(
        body, grid=(D // tc, nt), in_specs=in_specs, out_specs=[rblk, rblk, rblk],
        out_shape=[_sds((t, D), f32)] * 3,
        scratch_shapes=[pltpu.VMEM((TT + _SCAN_PAD, tc), f32), pltpu.VMEM((TT + _SCAN_PAD, tc), f32),
                        pltpu.VMEM((TT + 8, tc), f32), pltpu.VMEM((TT + 8, tc), f32), pltpu.VMEM((8, tc), f32)],
        compiler_params=_cp(), name=name)(a, a, h, h, p2, dhg)


def _ffn_act_bwd(gc, up, dact, name, tc=256):
    t = gc.shape[0]
    blk = pl.BlockSpec((TT, tc), lambda i, j: (i, j))
    vblk = pl.BlockSpec((TT, tc), lambda i, j: (i, j + D_FF // tc))

    def body(gc_ref, v_ref, da_ref, dgc_ref, dv_ref):
        _, vjp = jax.vjp(lambda g, v: _silu(g) * v, gc_ref[...], v_ref[...])
        dg, dv = vjp(da_ref[...].astype(f32))
        dgc_ref[...] = dg
        dv_ref[...] = dv.astype(bf16)

    return pl.pallas_call(
        body, grid=(t // TT, D_FF // tc), in_specs=[blk, vblk, blk], out_specs=[blk, vblk],
        out_shape=[_sds((t, D_FF), f32), _sds((t, 2 * D_FF), bf16)], compiler_params=_cp(), name=name)(gc, up, dact)


def _lower_bounds_fwd(w):
    def body(w_ref, o0_ref, o1_ref):
        wv = w_ref[...]
        o0, o1 = _lb_rows(wv[0:1, :], wv[1:2, :])
        o0_ref[...] = o0
        o1_ref[...] = o1

    return pl.pallas_call(body, out_shape=[_sds((1, 512), f32)] * 2, name="lower_bounds_fwd")(w)


def _lb_rows(w0, w1):
    m = jnp.maximum(w0, w1)
    e0, e1 = jnp.exp(w0 - m), jnp.exp(w1 - m)
    s = e0 + e1
    p0, p1 = e0 / s, e1 / s
    return p0 - p0, (p0 + p1) - p0


def _lower_bounds_bwd(w, d0, d1):
    def body(w_ref, d0_ref, d1_ref, g0_ref, g1_ref):
        wv = w_ref[...]
        _, vjp = jax.vjp(_lb_rows, wv[0:1, :], wv[1:2, :])
        g0, g1 = vjp((d0_ref[...], d1_ref[...]))
        g0_ref[...] = g0
        g1_ref[...] = g1

    return pl.pallas_call(body, out_shape=[_sds((1, 512), f32)] * 2, name="lower_bounds_bwd")(w, d0, d1)


def _local_step(x, target, wt, pre_layer=None, post_grads=None):
    depth = 4
    res = []
    lb0, lb1 = _lower_bounds_fwd(wt["hgrn_lower_bounds"])
    lbs = [lb0, lb1]
    for layer in range(depth):
        j = layer // 2
        sv = {"x_in": x}
        deps = pre_layer(layer, "mix", x) if pre_layer else ()
        if layer == 0:
            h1 = _rmsnorm_fwd(x, wt["norm_mix"][layer], "rms_fwd", deps)
        sv["h1"] = h1
        if layer % 2 == 0:
            p = _mm_auto(h1, wt["ab_w_in"][j], "nn", f32, "mm_ab_in")
            conv = _conv_fwd(p, 0, 1536, wt["gdn_conv_w"][j], jnp.zeros((1, 1536), f32), "gdn_conv_fwd", tc=768)
            o_a, s_a = _gdn2_fwd(conv, p, wt["alog_v"][j], wt["dtb_v"][j], wt["gdn_norm"][j], "gdn_fwd")
            o_b, s_b = _hgrn_fwd(p, lbs[j], wt["hgrn_norm"][j], "hgrn_fwd")
            o = jnp.concatenate([o_a, o_b], axis=1)
            x, h2 = _mm_res_norm(o, wt["ab_w_out"][j], x, wt["norm_ffn"][layer], "mm_ab_out")
            sv.update(p=p, conv=conv, s_a=s_a, s_b=s_b, o=o)
        else:
            p2 = _mm_nn_slots(h1, wt["c_w_in"][j], f32, "mm_c_in", deps)
            xc = _conv_fwd(p2, 1, D, wt["c_conv_w"][j], wt["c_conv_b"][j], "lru_conv_fwd", tc=D)
            a, u = _lru_gates_fwd(xc, wt["c_gate_a_w"][j], wt["c_gate_x_w"][j], wt["c_gate_a_b"][j],
                                  wt["c_gate_x_b"][j], wt["c_lambda"][j], "lru_gates_fwd")
            h, hg = _lru_scan_fwd(a, u, p2, "lru_scan_fwd")
            x, h2 = _mm_res_norm(hg, wt["c_w_out"][j], x, wt["norm_ffn"][layer], "mm_c_out")
            sv.update(p2=p2, xc=xc, a=a, h=h, hg=hg)
        sv["x_mid"] = x
        deps = pre_layer(layer, "ffn", x) if pre_layer else ()
        up = _mm_nn_slots(h2, wt["ffn_w_up"][layer], bf16, "mm_up", deps)
        gc, act = _conv_fwd(up, 0, D_FF, wt["ffn_conv_w"][layer], wt["ffn_conv_b"][layer], "ffn_conv_fwd",
                            tc=D_FF // 2, val=up, val_col0=2)
        if layer + 1 < depth:
            x, h1 = _mm_res_norm(act, wt["ffn_w_down"][layer], x, wt["norm_mix"][layer + 1], "mm_down")
        else:
            x = _mm_res_norm(act, wt["ffn_w_down"][layer], x, None, "mm_down_last")
        sv.update(h2=h2, up=up, gc=gc, act=act)
        res.append(sv)

    loss, dx, dxb, d_norm_final = _final_loss(x, wt["norm_final"], target, "final_loss")

    g = {k: [None] * len(v) for k, v in wt.items() if isinstance(v, list)}
    g["norm_final"] = d_norm_final
    d_lbs = [None, None]
    for layer in reversed(range(depth)):
        j = layer // 2
        sv = res[layer]
        dact = _mm_auto(dxb, wt["ffn_w_down"][layer], "nt", bf16, "mm_down_dx")
        g["ffn_w_down"][layer] = _mm_auto(sv["act"], dxb, "tn", bf16, "mm_down_dw")
        dgc, dup = _ffn_act_bwd(sv["gc"], sv["up"], dact, "ffn_act_bwd", tc=D_FF // 2)
        dup, dcw, dcb = _conv_bwd(dgc, sv["up"], 0, wt["ffn_conv_w"][layer], "ffn_conv_bwd", tc=D_FF // 2, into=dup)
        g["ffn_conv_w"][layer] = dcw[:3]
        g["ffn_conv_b"][layer] = dcb
        g["ffn_w_up"][layer] = _mm_tn_slots(sv["h2"], dup, "mm_up_dw")
        deps = post_grads(layer, "ffn", g) if post_grads else ()
        dx, dxb, g["norm_ffn"][layer] = _mm_nt_rmsbwd(dup, wt["ffn_w_up"][layer], sv["x_mid"], wt["norm_ffn"][layer], dx,
                                                      "mm_up_dx", deps)
        if layer % 2 == 0:
            do = _mm_auto(dxb, wt["ab_w_out"][j], "nt", f32, "mm_ab_out_dx")
            g["ab_w_out"][j] = _mm_auto(sv["o"], dxb, "tn", bf16, "mm_ab_out_dw")
            dconv, dz, dba, dal, ddt, dgn = _gdn2_bwd(
                sv["conv"], sv["p"], wt["alog_v"][j], wt["dtb_v"][j], wt["gdn_norm"][j], sv["s_a"], do, "gdn_bwd")
            g["alog_v"][j], g["dtb_v"][j], g["gdn_norm"][j] = dal, ddt, dgn
            dqkv, dcw, _ = _conv_bwd(dconv, sv["p"], 0, wt["gdn_conv_w"][j], "gdn_conv_bwd", tc=768)
            g["gdn_conv_w"][j] = dcw[:4]
            dqfig, dlb, dhn = _hgrn_bwd(sv["p"], lbs[j], wt["hgrn_norm"][j], sv["s_b"], do, "hgrn_bwd")
            g["hgrn_norm"][j] = dhn
            d_lbs[j] = jnp.sum(dlb, axis=0)
            dp = jnp.concatenate([dqkv, dz.astype(bf16), dqfig, dba.astype(bf16)], axis=1)
            g["ab_w_in"][j] = _mm_auto(sv["h1"], dp, "tn", bf16, "mm_ab_in_dw")
            dpre, wpre = dp, wt["ab_w_in"][j]
        else:
            dhg = _mm_auto(dxb, wt["c_w_out"][j], "nt", f32, "mm_c_out_dx")
            g["c_w_out"][j] = _mm_auto(sv["hg"], dxb, "tn", bf16, "mm_c_out_dw")
            du, da, dy = _lru_scan_bwd(sv["a"], sv["h"], sv["p2"], dhg, "lru_scan_bwd")
            dxc, dwa, dwx, dba_, dbx_, dlam = _lru_gates_bwd(
                sv["xc"], wt["c_gate_a_w"][j], wt["c_gate_x_w"][j], wt["c_gate_a_b"][j], wt["c_gate_x_b"][j],
                wt["c_lambda"][j], da, du, "lru_gates_bwd")
            g["c_gate_a_w"][j], g["c_gate_x_w"][j] = dwa, dwx
            g["c_gate_a_b"][j], g["c_gate_x_b"][j], g["c_lambda"][j] = dba_, dbx_, dlam
            dxbr, dcw, dcb = _conv_bwd(dxc, sv["p2"], 1, wt["c_conv_w"][j], "lru_conv_bwd", tc=D)
            g["c_conv_w"][j] = dcw[:4]
            g["c_conv_b"][j] = dcb
            dp2 = jnp.concatenate([dy.astype(bf16), dxbr], axis=1)
            g["c_w_in"][j] = _mm_tn_slots(sv["h1"], dp2, "mm_c_in_dw")
            dpre, wpre = dp2, wt["c_w_in"][j]
        deps = post_grads(layer, "mix", g) if post_grads else ()
        dx, dxb, g["norm_mix"][layer] = _mm_nt_rmsbwd(dpre, wpre, sv["x_in"], wt["norm_mix"][layer], dx, "mm_mix_in_dx", deps)
    g0, g1 = _lower_bounds_bwd(wt["hgrn_lower_bounds"], d_lbs[0], d_lbs[1])
    g["hgrn_lower_bounds"] = jnp.concatenate([g0, g1], axis=0)
    return loss, dx, g


def _ab_in_to_compute(w):
    return jnp.concatenate([w[:, :2048], w[:, 2056:4104], w[:, 2048:2056], jnp.zeros((D, 120), w.dtype)], axis=1)


def _ab_in_from_compute(g):
    return jnp.concatenate([g[:, :2048], g[:, 4096:4104], g[:, 2048:4096]], axis=1)


def _lane_vec(v4):
    return jnp.zeros((1, HEAD), f32).at[0, NH:2 * NH].set(v4)


def _layout_weights(fw):
    wt = {}
    wt["norm_mix"] = [fw["norm_mix"][l][None] for l in range(4)]
    wt["norm_ffn"] = [fw["norm_ffn"][l][None] for l in range(4)]
    wt["norm_final"] = fw["norm_final"][None]
    wt["gdn_conv_w"] = [fw["gdn_conv_w"][j] for j in range(2)]
    wt["alog_v"] = [_lane_vec(fw["gdn_a_log"][j]) for j in range(2)]
    wt["dtb_v"] = [_lane_vec(fw["gdn_dt_bias"][j]) for j in range(2)]
    wt["gdn_norm"] = [fw["gdn_norm"][j][None] for j in range(2)]
    wt["hgrn_lower_bounds"] = fw["hgrn_lower_bounds"]
    wt["hgrn_norm"] = [fw["hgrn_norm"][j][None] for j in range(2)]
    wt["c_conv_w"] = [fw["c_conv_w"][j] for j in range(2)]
    for k in ("c_conv_b", "c_gate_a_b", "c_gate_x_b", "c_lambda"):
        wt[k] = [fw[k][j][None] for j in range(2)]
    wt["ffn_conv_w"] = [fw["ffn_conv_w"][l] for l in range(4)]
    wt["ffn_conv_b"] = [fw["ffn_conv_b"][l][None] for l in range(4)]
    if "ab_w_in" in fw:
        wt["ab_w_in"] = [_ab_in_to_compute(fw["ab_w_in"][j].astype(bf16)) for j in range(2)]
        for k in ("ab_w_out", "c_w_out"):
            wt[k] = [fw[k][j].astype(bf16) for j in range(2)]
        wt["c_w_in"] = [_to_slots(fw["c_w_in"][j].astype(bf16)) for j in range(2)]
        for k in ("c_gate_a_w", "c_gate_x_w"):
            wt[k] = [fw[k][j].astype(f32) for j in range(2)]
        wt["ffn_w_up"] = [_to_slots(fw["ffn_w_up"][l].astype(bf16)) for l in range(4)]
        wt["ffn_w_down"] = [fw["ffn_w_down"][l].astype(bf16) for l in range(4)]
    return wt


SLOT_MAJOR = ("c_w_in", "ffn_w_up")
NATIVE_PERM = {"ab_w_in": (2, 0, 1), "ffn_conv_w": (1, 0, 2)}


def _to_slots(wfull):
    k, n = wfull.shape
    return wfull.reshape(k, N_SLOTS, n // N_SLOTS).transpose(1, 0, 2)


def _layer_full(name, slots):
    kind = BIG[name]
    if name in SLOT_MAJOR:
        return slots
    if kind == "col":
        return _ab_in_to_compute(slots.transpose(1, 0, 2).reshape(slots.shape[1], -1))
    if kind == "row":
        return slots.reshape(-1, slots.shape[2])
    return slots.reshape(4, NH, LRU_BLOCK // 4, LRU_BLOCK).transpose(1, 0, 2, 3).reshape(NH, LRU_BLOCK, LRU_BLOCK).astype(f32)


def _layer_slots(name, g):
    kind = BIG[name]
    if name in SLOT_MAJOR:
        return g
    if kind == "col":
        g = _ab_in_from_compute(g)
        r, cdim = g.shape
        return g.reshape(r, 4, cdim // 4).transpose(1, 0, 2).astype(bf16)
    if kind == "row":
        r, cdim = g.shape
        return g.reshape(4, r // 4, cdim).astype(bf16)
    return g.reshape(NH, 4, LRU_BLOCK // 4, LRU_BLOCK).transpose(1, 0, 2, 3).reshape(4, LRU_BLOCK, LRU_BLOCK).astype(bf16)


def _unlayout_grads(g):
    out = {}
    for k in ("norm_mix", "norm_ffn", "gdn_norm", "hgrn_norm", "c_conv_b", "c_gate_a_b", "c_gate_x_b", "c_lambda",
              "ffn_conv_b"):
        out[k] = jnp.concatenate(g[k], axis=0)
    out["norm_final"] = g["norm_final"][0]
    out["ab_w_in"] = jnp.stack([_ab_in_from_compute(t) for t in g["ab_w_in"]])
    out["gdn_a_log"] = jnp.stack([t[0, NH:2 * NH] for t in g["alog_v"]])
    out["gdn_dt_bias"] = jnp.stack([t[0, NH:2 * NH] for t in g["dtb_v"]])
    out["hgrn_lower_bounds"] = g["hgrn_lower_bounds"]
    for k in ("gdn_conv_w", "ab_w_out", "c_conv_w", "c_gate_a_w", "c_gate_x_w", "c_w_out", "ffn_conv_w", "ffn_w_down"):
        out[k] = jnp.stack(g[k])
    for k in SLOT_MAJOR:
        out[k] = jnp.stack([t.transpose(1, 0, 2).reshape(t.shape[1], -1) for t in g[k]])
    return out


MESH = pl.DeviceIdType.MESH
ANY = pl.BlockSpec(memory_space=pl.ANY)
CHIP_RELATIONS = ((1, 0), (0, 1), (1, 1))
N_CHIPS = 4


def _coords():
    return lax.axis_index("x"), lax.axis_index("y"), lax.axis_index("c")


def _flip(v, f):
    return 1 - v if f else v


def _half_rows(c, a, align):
    return pl.ds(pl.multiple_of(c * (a // 2), align), a // 2)


def _all_gather_chips(shards, name):
    n = len(shards)
    shapes = [s.shape for s in shards]

    def body(*refs):
        ins, outs = refs[:n], refs[n:2 * n]
        send_sems, recv_sems = refs[2 * n:]
        x, y, c = _coords()
        me = 2 * x + y
        sibling = (x, y, 1 - c)
        started = []
        for p in range(n):
            cp = pltpu.make_async_remote_copy(
                src_ref=ins[p], dst_ref=outs[p].at[me],
                send_sem=send_sems.at[p, 6], recv_sem=recv_sems.at[p, 6],
                device_id=sibling, device_id_type=MESH)
            cp.start()
            started.append(cp)
        for p in range(n):
            mine = _half_rows(c, shapes[p][0], 16)
            for r, (fx, fy) in enumerate(CHIP_RELATIONS):
                cp = pltpu.make_async_remote_copy(
                    src_ref=ins[p].at[mine], dst_ref=outs[p].at[me, mine],
                    send_sem=send_sems.at[p, r], recv_sem=recv_sems.at[p, r],
                    device_id=(_flip(x, fx), _flip(y, fy), c), device_id_type=MESH)
                cp.start()
                started.append(cp)
        for r, (fx, fy) in enumerate(CHIP_RELATIONS):
            k = 2 * _flip(x, fx) + _flip(y, fy)
            for p in range(n):
                mine = _half_rows(c, shapes[p][0], 16)
                pltpu.make_async_remote_copy(
                    src_ref=ins[p].at[mine], dst_ref=outs[p].at[k, mine],
                    send_sem=send_sems.at[p, r], recv_sem=recv_sems.at[p, r],
                    device_id=(_flip(x, fx), _flip(y, fy), c), device_id_type=MESH).wait_recv()
                fwd = pltpu.make_async_remote_copy(
                    src_ref=outs[p].at[k, mine], dst_ref=outs[p].at[k, mine],
                    send_sem=send_sems.at[p, 3 + r], recv_sem=recv_sems.at[p, 3 + r],
                    device_id=sibling, device_id_type=MESH)
                fwd.start()
                started.append(fwd)
        for r, (fx, fy) in enumerate(CHIP_RELATIONS):
            k = 2 * _flip(x, fx) + _flip(y, fy)
            for p in range(n):
                theirs = _half_rows(1 - c, shapes[p][0], 16)
                pltpu.make_async_remote_copy(
                    src_ref=outs[p].at[k, theirs], dst_ref=outs[p].at[k, theirs],
                    send_sem=send_sems.at[p, 3 + r], recv_sem=recv_sems.at[p, 3 + r],
                    device_id=sibling, device_id_type=MESH).wait_recv()
        for p in range(n):
            pltpu.make_async_remote_copy(
                src_ref=ins[p], dst_ref=outs[p].at[me],
                send_sem=send_sems.at[p, 6], recv_sem=recv_sems.at[p, 6],
                device_id=sibling, device_id_type=MESH).wait_recv()
        for cp in started:
            cp.wait_send()

    return pl.pallas_call(
        body, in_specs=[ANY] * n, out_specs=[ANY] * n,
        out_shape=[_sds((N_CHIPS,) + s.shape, s.dtype) for s in shards],
        scratch_shapes=[pltpu.SemaphoreType.DMA((n, 7)), pltpu.SemaphoreType.DMA((n, 7))],
        name=name)(*shards)


N_DEV = 8


def _all_reduce_small(pack, name, after=()):
    rows = pack.shape[0]

    def body(in_ref, *rest):
        sum_ref, all_ref, send_sems, recv_sems = rest[len(after):]
        x, y, c = _coords()
        me = 4 * x + 2 * y + c
        all_ref[me] = in_ref[...]
        cps = []
        for r in range(1, N_DEV):
            fx, fy, fc = (r >> 2) & 1, (r >> 1) & 1, r & 1
            cp = pltpu.make_async_remote_copy(
                src_ref=in_ref, dst_ref=all_ref.at[me],
                send_sem=send_sems.at[r], recv_sem=recv_sems.at[r],
                device_id=(_flip(x, fx), _flip(y, fy), _flip(c, fc)), device_id_type=MESH)
            cp.start()
            cps.append(cp)
        for r in range(1, N_DEV):
            fx, fy, fc = (r >> 2) & 1, (r >> 1) & 1, r & 1
            peer = 4 * _flip(x, fx) + 2 * _flip(y, fy) + _flip(c, fc)
            pltpu.make_async_remote_copy(
                src_ref=in_ref, dst_ref=all_ref.at[peer],
                send_sem=send_sems.at[r], recv_sem=recv_sems.at[r],
                device_id=(x, y, c), device_id_type=MESH).wait_recv()
        for cp in cps:
            cp.wait_send()
        acc = all_ref[0]
        for d in range(1, N_DEV):
            acc = acc + all_ref[d]
        sum_ref[...] = acc

    vm = pl.BlockSpec(memory_space=pltpu.VMEM)
    return pl.pallas_call(
        body, in_specs=[vm] + [pl.BlockSpec(memory_space=pl.ANY)] * len(after), out_specs=[vm, vm],
        out_shape=[_sds((rows, 128), f32), _sds((N_DEV, rows, 128), f32)],
        scratch_shapes=[pltpu.SemaphoreType.DMA((N_DEV,)), pltpu.SemaphoreType.DMA((N_DEV,))],
        name=name)(pack, *after)[0]


ROWS_EW = 128


def _adamw_math(w, g, m, v):
    m = ADAM_B1 * m + (1.0 - ADAM_B1) * g
    v = ADAM_B2 * v + (1.0 - ADAM_B2) * (g * g)
    m_hat = m / (1.0 - ADAM_B1 ** ADAM_STEP)
    v_hat = v / (1.0 - ADAM_B2 ** ADAM_STEP)
    delta = -ADAM_LR * (m_hat / (jnp.sqrt(v_hat) + ADAM_EPS) + ADAM_WD * w)
    return delta, m, v


def _adamw_big(w, g, m, v, name):
    nl, r, b = w.shape
    rt = _tile(r, (ROWS_EW, 352, 64))
    per = r // rt

    def body(w_ref, g_ref, m_ref, v_ref, go_ref, d_ref, mo_ref, vo_ref):
        gv = g_ref[...]
        d, mn, vn = _adamw_math(w_ref[...], gv, m_ref[...], v_ref[...])
        go_ref[...] = gv
        d_ref[...] = d
        mo_ref[...] = mn
        vo_ref[...] = vn

    blk = pl.BlockSpec((None, rt, b), lambda l, i: (l, i, 0))
    gblk = pl.BlockSpec((rt, b), lambda l, i: (l * per + i, 0))
    return pl.pallas_call(body, grid=(nl, per), in_specs=[blk, gblk, blk, blk], out_specs=[blk] * 4,
                          out_shape=[_sds((nl, r, b), f32)] * 4, compiler_params=_cp(), name=name)(w, g, m, v)


def _adamw_lead(w, g, m, v, name):
    n, r, b = w.shape
    tn = _tile(n, (64, 54, 32, 16, 8, 1))

    def body(w_ref, g_ref, m_ref, v_ref, go_ref, d_ref, mo_ref, vo_ref):
        gv = g_ref[...]
        d, mn, vn = _adamw_math(w_ref[...], gv, m_ref[...], v_ref[...])
        go_ref[...] = gv
        d_ref[...] = d
        mo_ref[...] = mn
        vo_ref[...] = vn

    blk = pl.BlockSpec((tn, r, b), lambda i: (i, 0, 0))
    return pl.pallas_call(body, grid=(n // tn,), in_specs=[blk] * 4, out_specs=[blk] * 4,
                          out_shape=[_sds((n, r, b), f32)] * 4, compiler_params=_cp(), name=name)(w, g, m, v)


def _adamw_small(ws, gs, ms, vs, name):
    n = len(ws)

    def body(*refs):
        w_r, g_r, m_r, v_r = refs[:n], refs[n:2 * n], refs[2 * n:3 * n], refs[3 * n:4 * n]
        go_r, d_r, mo_r, vo_r = refs[4 * n:5 * n], refs[5 * n:6 * n], refs[6 * n:7 * n], refs[7 * n:8 * n]
        for p in range(n):
            gv = g_r[p][...]
            d, mn, vn = _adamw_math(w_r[p][...], gv, m_r[p][...], v_r[p][...])
            go_r[p][...] = gv
            d_r[p][...] = d
            mo_r[p][...] = mn
            vo_r[p][...] = vn

    vm = pl.BlockSpec(memory_space=pltpu.VMEM)
    shp = [_sds(w.shape, f32) for w in ws]
    res = pl.pallas_call(body, in_specs=[vm] * (4 * n), out_specs=[vm] * (4 * n), out_shape=shp * 4,
                         name=name)(*ws, *gs, *ms, *vs)
    return res[:n], res[n:2 * n], res[2 * n:3 * n], res[3 * n:]


HBM = pl.BlockSpec(memory_space=pltpu.HBM)
SEM = pl.BlockSpec(memory_space=pltpu.SEMAPHORE)
EFFECT = pltpu.SideEffectType.DATAFLOW_SIDE_EFFECTING
N_REL = 8


def _rel(r):
    return (r >> 2) & 1, (r >> 1) & 1, r & 1


def _gather_copies(ins, lands, send_sems, recv_sems, shapes):
    x, y, c = _coords()
    me = 2 * x + y
    sends, recvs = [], []
    for p in range(len(ins)):
        for r in range(1, N_REL):
            fx, fy, fc = _rel(r)
            peer = (_flip(x, fx), _flip(y, fy), _flip(c, fc))
            if fx == 0 and fy == 0:
                src, dst, got = ins[p], lands[p].at[me], lands[p].at[me]
            else:
                mine = _half_rows(c, shapes[p][0], 16)
                theirs = _half_rows(_flip(c, fc), shapes[p][0], 16)
                src, dst = ins[p].at[mine], lands[p].at[me, mine]
                got = lands[p].at[2 * peer[0] + peer[1], theirs]
            sems = dict(send_sem=send_sems.at[p * N_REL + r], recv_sem=recv_sems.at[p * N_REL + r], device_id=peer,
                        device_id_type=MESH)
            sends.append(pltpu.make_async_remote_copy(src_ref=src, dst_ref=dst, **sems))
            recvs.append(pltpu.make_async_remote_copy(src_ref=src, dst_ref=got, **sems))
    return sends, recvs


def _scatter_copies(ins, lands, send_sems, recv_sems, shapes):
    x, y, c = _coords()
    sends, recvs = [], []
    for p in range(len(ins)):
        for r in range(1, N_REL):
            fx, fy, fc = _rel(r)
            peer = (_flip(x, fx), _flip(y, fy), _flip(c, fc))
            theirs = _half_rows(peer[2], shapes[p][1], 16)
            sems = dict(send_sem=send_sems.at[p * N_REL + r], recv_sem=recv_sems.at[p * N_REL + r], device_id=peer,
                        device_id_type=MESH)
            cp = pltpu.make_async_remote_copy(src_ref=ins[p].at[2 * peer[0] + peer[1], theirs], dst_ref=lands[p].at[r], **sems)
            sends.append(cp)
            recvs.append(cp)
    return sends, recvs


def _split_start(copies_fn, ins, land_shapes, name, after=()):
    n = len(ins)
    shapes = [a.shape for a in ins]

    def body(*refs):
        in_refs, land_refs = refs[:n], refs[n:2 * n]
        send_sems, recv_sems = refs[2 * n + len(after)], refs[2 * n + len(after) + 1]
        token = refs[-1]
        sends, _ = copies_fn(in_refs, land_refs, send_sems, recv_sems, shapes)
        for cp in sends:
            cp.start()
        token[...] = jnp.zeros_like(token)

    lands = [lax.empty(s.shape, s.dtype) for s in land_shapes]
    na = len(after)
    res = pl.pallas_call(
        body, name=name,
        out_shape=(pltpu.SemaphoreType.DMA((n * N_REL,)), pltpu.SemaphoreType.DMA((n * N_REL,)))
        + tuple(pltpu.HBM(a.shape, a.dtype) for a in ins) + tuple(pltpu.HBM(s.shape, s.dtype) for s in land_shapes)
        + (_sds((8, 128), f32),),
        in_specs=[HBM] * (2 * n) + [pl.BlockSpec(memory_space=pl.ANY)] * na,
        out_specs=(SEM, SEM) + (HBM,) * (2 * n) + (pl.BlockSpec(memory_space=pltpu.VMEM),),
        input_output_aliases={i: 2 + i for i in range(2 * n)},
        compiler_params=pltpu.CompilerParams(has_side_effects=EFFECT),
    )(*[pltpu.with_memory_space_constraint(a, pltpu.HBM) for a in ins],
      *[pltpu.with_memory_space_constraint(a, pltpu.HBM) for a in lands], *after)
    return dict(sems=res[:2], ins=res[2:2 + n], lands=res[2 + n:2 + 2 * n], token=res[-1], shapes=shapes)


def _split_wait(copies_fn, started, after, name):
    n = len(started["ins"])
    shapes = started["shapes"]
    na = len(after)

    def body(*refs):
        in_refs, land_refs = refs[:n], refs[n:2 * n]
        send_sems, recv_sems = refs[2 * n], refs[2 * n + 1]
        sends, recvs = copies_fn(in_refs, land_refs, send_sems, recv_sems, shapes)
        for cp in sends:
            cp.wait_send()
        for cp in recvs:
            cp.wait_recv()

    arrs = list(started["ins"]) + list(started["lands"])
    res = pl.pallas_call(
        body, name=name,
        out_shape=tuple(pltpu.HBM(a.shape, a.dtype) for a in arrs),
        in_specs=[HBM] * (2 * n) + [SEM, SEM] + [pl.BlockSpec(memory_space=pl.ANY)] * na,
        out_specs=(HBM,) * (2 * n), input_output_aliases={i: i for i in range(2 * n)},
        compiler_params=pltpu.CompilerParams(has_side_effects=EFFECT),
    )(*arrs, *started["sems"], *after)
    return res[:n], res[n:]


def _sum_pieces(gb, land, ids, f_prev, blk, nblk, name):
    _, a, b = gb.shape
    rows = _tile(a // 2, (ROWS_EW, 176, 64, 32, 16))
    nrt = (a // 2) // rows

    def body(ids_ref, g_ref, l_ref, *rest):
        o_ref = rest[-1]
        acc = g_ref[0].astype(f32)
        for r in range(1, N_REL):
            acc = acc + l_ref[r].astype(f32)
        o_ref[...] = acc

    in_specs = [pl.BlockSpec((1, rows, b), lambda i, ids_ref: (ids_ref[0], ids_ref[1] * nrt + i, 0)),
                pl.BlockSpec((N_REL, rows, b), lambda i, ids_ref: (0, i, 0))]
    args = [ids, gb, land]
    aliases = {}
    if f_prev is not None:
        in_specs.append(pl.BlockSpec(memory_space=pl.ANY))
        args.append(f_prev)
        aliases = {3: 0}
    grid_spec = pltpu.PrefetchScalarGridSpec(
        num_scalar_prefetch=1, grid=(nrt,), in_specs=in_specs,
        out_specs=pl.BlockSpec((rows, b), lambda i, ids_ref: ((2 * blk + ids_ref[1]) * nrt + i, 0)))
    return pl.pallas_call(body, grid_spec=grid_spec, out_shape=_sds((nblk * a, b), f32),
                          input_output_aliases=aliases, compiler_params=_cp(), name=name)(*args)


def _sibling_fill_blocks(fs, nblks, name):
    n = len(fs)
    shapes = [f.shape for f in fs]

    def body(*refs):
        ins, outs = refs[:n], refs[n:2 * n]
        send_sems, recv_sems = refs[2 * n:]
        x, y, c = _coords()
        cps, waits = [], []
        k = 0
        for p in range(n):
            a = shapes[p][0] // nblks[p]
            for bi in range(nblks[p]):
                mine = pl.ds(pl.multiple_of(bi * a + c * (a // 2), 8), a // 2)
                theirs = pl.ds(pl.multiple_of(bi * a + (1 - c) * (a // 2), 8), a // 2)
                sems = dict(send_sem=send_sems.at[k], recv_sem=recv_sems.at[k], device_id=(x, y, 1 - c), device_id_type=MESH)
                cp = pltpu.make_async_remote_copy(src_ref=ins[p].at[mine], dst_ref=outs[p].at[mine], **sems)
                cp.start()
                cps.append(cp)
                waits.append(pltpu.make_async_remote_copy(src_ref=ins[p].at[theirs], dst_ref=outs[p].at[theirs], **sems))
                k += 1
        for wt_ in waits:
            wt_.wait_recv()
        for cp in cps:
            cp.wait_send()

    total = sum(nblks)
    return pl.pallas_call(
        body, in_specs=[ANY] * n, out_specs=[ANY] * n,
        out_shape=[_sds(f.shape, f.dtype) for f in fs],
        input_output_aliases={p: p for p in range(n)},
        scratch_shapes=[pltpu.SemaphoreType.DMA((total,)), pltpu.SemaphoreType.DMA((total,))],
        name=name)(*fs)


WEIGHTS = ["norm_mix", "norm_ffn", "norm_final", "ab_w_in", "gdn_conv_w", "gdn_a_log", "gdn_dt_bias", "gdn_norm",
           "hgrn_lower_bounds", "hgrn_norm", "ab_w_out", "c_w_in", "c_conv_w", "c_conv_b", "c_gate_a_w", "c_gate_a_b",
           "c_gate_x_w", "c_gate_x_b", "c_lambda", "c_w_out", "ffn_w_up", "ffn_conv_w", "ffn_conv_b", "ffn_w_down"]
BIG = {"ab_w_in": "col", "ab_w_out": "row", "c_w_in": "col", "c_gate_a_w": "gate", "c_gate_x_w": "gate",
       "c_w_out": "row", "ffn_w_up": "col", "ffn_w_down": "row"}
SMALL_SHARDED = ["gdn_conv_w", "c_conv_w", "c_conv_b", "c_gate_a_b", "c_gate_x_b", "c_lambda", "ffn_conv_w"]
SMALL = [n for n in WEIGHTS if n not in BIG]
FULL_SHAPES = {
    "norm_mix": (4, 1024), "norm_ffn": (4, 1024), "norm_final": (1024,), "ab_w_in": (2, 1024, 4104),
    "gdn_conv_w": (2, 4, 1536), "gdn_a_log": (2, 4), "gdn_dt_bias": (2, 4), "gdn_norm": (2, 128),
    "hgrn_lower_bounds": (2, 512), "hgrn_norm": (2, 128), "ab_w_out": (2, 1024, 1024), "c_w_in": (2, 1024, 2048),
    "c_conv_w": (2, 4, 1024), "c_conv_b": (2, 1024), "c_gate_a_w": (2, 4, 256, 256), "c_gate_a_b": (2, 1024),
    "c_gate_x_w": (2, 4, 256, 256), "c_gate_x_b": (2, 1024), "c_lambda": (2, 1024), "c_w_out": (2, 1024, 1024),
    "ffn_w_up": (4, 1024, 5632), "ffn_conv_w": (4, 3, 2816), "ffn_conv_b": (4, 2816), "ffn_w_down": (4, 2816, 1024)}


def _pack_rows(arrs, rows):
    parts, used = [], 0
    for a in arrs:
        r = _pack_tile_rows(a.size)
        parts.append(jnp.pad(a.reshape(-1), (0, r * 128 - a.size)).reshape(r, 128))
        used += r
    assert rows >= used and (rows - used) % 8 == 0, (rows, used)
    if rows > used:
        parts.append(jnp.zeros((rows - used, 128), f32))
    return jnp.concatenate(parts, axis=0)


def _pack_tile_rows(size):
    return -(-size // 1024) * 8


def _unpack_rows(pack, shapes):
    out, row = [], 0
    for s in shapes:
        size = 1
        for d in s:
            size *= d
        r = _pack_tile_rows(size)
        out.append(pack[row:row + r].reshape(-1)[:size].reshape(s))
        row += r
    return out


def kernel(x, norm_mix, norm_ffn, norm_final, ab_w_in, gdn_conv_w, gdn_a_log, gdn_dt_bias, gdn_norm, hgrn_lower_bounds, hgrn_norm, ab_w_out, c_w_in, c_conv_w, c_conv_b, c_gate_a_w, c_gate_a_b, c_gate_x_w, c_gate_x_b, c_lambda, c_w_out, ffn_w_up, ffn_conv_w, ffn_conv_b, ffn_w_down, loss_target, m_norm_mix, m_norm_ffn, m_norm_final, m_ab_w_in, m_gdn_conv_w, m_gdn_a_log, m_gdn_dt_bias, m_gdn_norm, m_hgrn_lower_bounds, m_hgrn_norm, m_ab_w_out, m_c_w_in, m_c_conv_w, m_c_conv_b, m_c_gate_a_w, m_c_gate_a_b, m_c_gate_x_w, m_c_gate_x_b, m_c_lambda, m_c_w_out, m_ffn_w_up, m_ffn_conv_w, m_ffn_conv_b, m_ffn_w_down, v_norm_mix, v_norm_ffn, v_norm_final, v_ab_w_in, v_gdn_conv_w, v_gdn_a_log, v_gdn_dt_bias, v_gdn_norm, v_hgrn_lower_bounds, v_hgrn_norm, v_ab_w_out, v_c_w_in, v_c_conv_w, v_c_conv_b, v_c_gate_a_w, v_c_gate_a_b, v_c_gate_x_w, v_c_gate_x_b, v_c_lambda, v_c_w_out, v_ffn_w_up, v_ffn_conv_w, v_ffn_conv_b, v_ffn_w_down):
    w = dict(zip(WEIGHTS, (norm_mix, norm_ffn, norm_final, ab_w_in, gdn_conv_w, gdn_a_log, gdn_dt_bias, gdn_norm, hgrn_lower_bounds, hgrn_norm, ab_w_out, c_w_in, c_conv_w, c_conv_b, c_gate_a_w, c_gate_a_b, c_gate_x_w, c_gate_x_b, c_lambda, c_w_out, ffn_w_up, ffn_conv_w, ffn_conv_b, ffn_w_down)))
    m = dict(zip(WEIGHTS, (m_norm_mix, m_norm_ffn, m_norm_final, m_ab_w_in, m_gdn_conv_w, m_gdn_a_log, m_gdn_dt_bias, m_gdn_norm, m_hgrn_lower_bounds, m_hgrn_norm, m_ab_w_out, m_c_w_in, m_c_conv_w, m_c_conv_b, m_c_gate_a_w, m_c_gate_a_b, m_c_gate_x_w, m_c_gate_x_b, m_c_lambda, m_c_w_out, m_ffn_w_up, m_ffn_conv_w, m_ffn_conv_b, m_ffn_w_down)))
    v = dict(zip(WEIGHTS, (v_norm_mix, v_norm_ffn, v_norm_final, v_ab_w_in, v_gdn_conv_w, v_gdn_a_log, v_gdn_dt_bias, v_gdn_norm, v_hgrn_lower_bounds, v_hgrn_norm, v_ab_w_out, v_c_w_in, v_c_conv_w, v_c_conv_b, v_c_gate_a_w, v_c_gate_a_b, v_c_gate_x_w, v_c_gate_x_b, v_c_lambda, v_c_w_out, v_ffn_w_up, v_ffn_conv_w, v_ffn_conv_b, v_ffn_w_down)))
    big = list(BIG)
    chip = 2 * lax.axis_index("x") + lax.axis_index("y")
    ids = jnp.stack([chip, lax.axis_index("c")]).astype(jnp.int32)

    def layer_parts(l):
        j = l // 2
        if l % 2 == 0:
            mix = [("ab_w_in", j), ("ab_w_out", j)]
        else:
            mix = [("c_w_in", j), ("c_gate_a_w", j), ("c_gate_x_w", j), ("c_w_out", j)]
        return mix, [("ffn_w_up", l), ("ffn_w_down", l)]

    def layer_shard(n, i):
        s = w[n][i]
        return s.reshape(-1, s.shape[-1]).astype(bf16)

    small_shard_shapes = [w[n].shape for n in SMALL_SHARDED]
    small_pack = _pack_rows([w[n] for n in SMALL_SHARDED], 160)
    mix0, ffn0 = layer_parts(0)
    gathered0 = _all_gather_chips([layer_shard(n, i) for n, i in mix0] + [small_pack], "all_gather_mixer0")
    gathers = {}

    def start_gather(key, after):
        shards = [layer_shard(n, i) for n, i in layer_parts(key[0])[0 if key[1] == "mix" else 1]]
        gathers[key] = _split_start(_gather_copies, shards, [_sds((N_CHIPS,) + s.shape, bf16) for s in shards],
                                    "gather_start_%d_%s" % key, after)
        return gathers[key]["token"]

    tok = ()
    for key in ((0, "ffn"), (1, "mix"), (1, "ffn")):
        tok = (start_gather(key, tok),)
    first_tokens = tok
    start_next = {(0, "ffn"): ((2, "mix"), (2, "ffn")), (1, "mix"): ((3, "mix"),), (1, "ffn"): ((3, "ffn"),)}
    fw = {}
    per_chip = [_unpack_rows(gathered0[-1][k], small_shard_shapes) for k in range(N_CHIPS)]
    for i, n in enumerate(SMALL_SHARDED):
        fw[n] = jnp.concatenate([per_chip[k][i] for k in range(N_CHIPS)], axis=-1)
    for n in SMALL:
        if n not in fw:
            fw[n] = w[n]
    wt = _layout_weights(fw)
    for n in big:
        wt[n] = [None] * FULL_SHAPES[n][0]

    def pre_layer(l, part, x_l):
        parts = layer_parts(l)[0 if part == "mix" else 1]
        if l == 0 and part == "mix":
            lands, deps = gathered0[:len(mix0)], first_tokens
        else:
            _, lands = _split_wait(_gather_copies, gathers[(l, part)], [x_l], "gather_wait_%d_%s" % (l, part))
            deps = tuple(lands[:1])
            for key in start_next.get((l, part), ()):
                deps = (start_gather(key, deps),)
            if (l, part) not in start_next:
                deps = ()
        for (n, i), slots in zip(parts, lands):
            wt[n][i] = _layer_full(n, slots)
        return deps

    scatters = []

    def post_grads(l, part, g):
        parts = layer_parts(l)[0 if part == "mix" else 1]
        slots = [_layer_slots(n, g[n][i]) for n, i in parts]
        st = _split_start(_scatter_copies, slots, [_sds((N_REL, s.shape[1] // 2, s.shape[2]), bf16) for s in slots],
                          "scatter_start_%d_%s" % (l, part))
        scatters.append((parts, st, "scatter_wait_%d_%s" % (l, part)))
        return (st["token"],)

    loss, dx, g = _local_step(x[0], loss_target[0], wt, pre_layer, post_grads)
    gf = _unlayout_grads(g)
    loss = lax.psum(loss[0, 0], ("x", "y", "c"))

    out_g, out_d, out_m, out_v = {}, {}, {}, {}
    f = {n: None for n in big}
    last = [n for n, _ in layer_parts(0)[0]]

    def finish(group, after):
        for parts, st, wait_name in group:
            gbs, lands = _split_wait(_scatter_copies, st, after, wait_name)
            for (n, i), gb, land in zip(parts, gbs, lands):
                f[n] = _sum_pieces(gb, land, ids, f[n], i, FULL_SHAPES[n][0], "rs_sum")

    def adamw(names, tag):
        filled = _sibling_fill_blocks([f[n] for n in names], [FULL_SHAPES[n][0] for n in names], "rs_sibling_fill_" + tag)
        for n, g_n in zip(names, filled):
            shp = w[n].shape
            if n in NATIVE_PERM:
                to_native = lambda t: t.transpose(NATIVE_PERM[n])
                res = _adamw_lead(to_native(w[n]), to_native(g_n.reshape(shp)), to_native(m[n]), to_native(v[n]), "adamw_" + n)
                res = [t.transpose(1, 2, 0) for t in res]
            else:
                as3d = lambda t: t.reshape((-1,) + shp[-2:])
                res = _adamw_big(as3d(w[n]), g_n, as3d(m[n]), as3d(v[n]), "adamw_" + n)
            out_g[n], out_d[n], out_m[n], out_v[n] = (t.reshape(shp) for t in res)

    finish(scatters[:-1], [dx])
    adamw([n for n in big if n not in last], "a")
    finish(scatters[-1:], [out_v["ffn_w_up"]])
    adamw(last, "b")

    small_full_shapes = [FULL_SHAPES[n] for n in SMALL]
    small_sum = _all_reduce_small(_pack_rows([gf[n] for n in SMALL], 688), "all_reduce_small", [f[last[0]]])
    g_small = dict(zip(SMALL, _unpack_rows(small_sum, small_full_shapes)))
    for n in SMALL_SHARDED:
        width = w[n].shape[-1]
        g_small[n] = lax.dynamic_slice_in_dim(g_small[n], chip * width, width, axis=-1)

    def small2d(n, t):
        if n in NATIVE_PERM:
            t = t.transpose(NATIVE_PERM[n])
        return t.reshape(-1, t.shape[-1])

    sg, sd, sm, sv = _adamw_small([small2d(n, w[n]) for n in SMALL], [small2d(n, g_small[n]) for n in SMALL],
                                  [small2d(n, m[n]) for n in SMALL], [small2d(n, v[n]) for n in SMALL], "adamw_small")
    for i, n in enumerate(SMALL):
        for out, t in zip((out_g, out_d, out_m, out_v), (sg, sd, sm, sv)):
            if n in NATIVE_PERM:
                perm = NATIVE_PERM[n]
                shp_t = tuple(w[n].shape[p] for p in perm)
                out[n] = t[i].reshape(shp_t).transpose(tuple(perm.index(k) for k in range(len(perm))))
            else:
                out[n] = t[i].reshape(w[n].shape)
    return (loss, dx[None], *[out_g[n] for n in WEIGHTS], *[out_d[n] for n in WEIGHTS],
            *[out_m[n] for n in WEIGHTS], *[out_v[n] for n in WEIGHTS])
```

```python
import functools

import jax
import jax.numpy as jnp
from jax import lax
from jax.experimental import pallas as pl
from jax.experimental.pallas import tpu as pltpu

f32 = jnp.float32
bf16 = jnp.bfloat16

D = 1024
EPS = 1e-6
F_FLOOR = 1e-30
GDN_CHUNK = 64
GDN_INTRA_CHUNKS = 1
HGRN_CHUNK = 16
HGRN_STEP = 128
HEAD = 128
NH = 4
LRU_BLOCK = 256
D_FF = 2816
RG_C = 8.0
C_QKV, C_Z, C_QB, C_FB, C_IB, C_GB, C_BA = 0, 1536, 2048, 2560, 3072, 3584, 4096
TT = 256
N_SLOTS = 4
VMEM_LIMIT = 56 * 1024 * 1024

ADAM_LR, ADAM_B1, ADAM_B2, ADAM_EPS, ADAM_WD, ADAM_STEP = 0.001, 0.9, 0.999, 1e-08, 0.01, 10


def _cp(**kw):
    return pltpu.CompilerParams(vmem_limit_bytes=VMEM_LIMIT, **kw)


def _sds(shape, dtype):
    return jax.ShapeDtypeStruct(shape, dtype)


def _dot(a, b, dims, precision=None):
    return lax.dot_general(a, b, (dims, ((), ())), precision=precision, preferred_element_type=f32)


NN = ((1,), (0,))
NT = ((1,), (1,))
TN = ((0,), (0,))


def _rms(x, g):
    return x * lax.rsqrt(jnp.mean(x * x, axis=-1, keepdims=True) + EPS) * g


def _silu(x):
    return x * jax.nn.sigmoid(x)


def _mm(a, b, mode, tm, tn, out_dtype, name):
    if mode == "nn":
        (m, k), n = a.shape, b.shape[1]
        a_spec = pl.BlockSpec((tm, k), lambda i, j: (i, 0))
        b_spec = pl.BlockSpec((k, tn), lambda i, j: (0, j))
        dims = NN
    elif mode == "nt":
        (m, k), n = a.shape, b.shape[0]
        a_spec = pl.BlockSpec((tm, k), lambda i, j: (i, 0))
        b_spec = pl.BlockSpec((tn, k), lambda i, j: (j, 0))
        dims = NT
    else:
        (k, m), n = a.shape, b.shape[1]
        a_spec = pl.BlockSpec((k, tm), lambda i, j: (0, i))
        b_spec = pl.BlockSpec((k, tn), lambda i, j: (0, j))
        dims = TN
    assert m % tm == 0 and n % tn == 0, (name, m, n, tm, tn)

    def body(a_ref, b_ref, o_ref):
        o_ref[...] = _dot(a_ref[...], b_ref[...], dims).astype(out_dtype)

    return pl.pallas_call(
        body, grid=(m // tm, n // tn), in_specs=[a_spec, b_spec],
        out_specs=pl.BlockSpec((tm, tn), lambda i, j: (i, j)),
        out_shape=_sds((m, n), out_dtype), compiler_params=_cp(), name=name)(a, b)


def _mm_res_norm(a, b, res, gain, name, tm=512):
    (m, k), n = a.shape, b.shape[1]
    tm = min(tm, m)
    assert n == D and m % tm == 0, (name, m, n)

    def body(a_ref, b_ref, r_ref, *rest):
        xv = r_ref[...] + _dot(a_ref[...], b_ref[...], NN)
        if gain is None:
            rest[0][...] = xv
        else:
            g_ref, x_ref, h_ref = rest
            x_ref[...] = xv
            h_ref[...] = _rms(xv, g_ref[...]).astype(bf16)

    row = pl.BlockSpec((tm, D), lambda i: (i, 0))
    in_specs = [pl.BlockSpec((tm, k), lambda i: (i, 0)), pl.BlockSpec((k, D), lambda i: (0, 0)), row]
    args = [a, b, res]
    if gain is None:
        out_specs, out_shape = row, _sds((m, D), f32)
    else:
        in_specs.append(pl.BlockSpec((1, D), lambda i: (0, 0)))
        args.append(gain)
        out_specs, out_shape = [row, row], [_sds((m, D), f32), _sds((m, D), bf16)]
    return pl.pallas_call(body, grid=(m // tm,), in_specs=in_specs, out_specs=out_specs, out_shape=out_shape,
                          compiler_params=_cp(), name=name)(*args)


def _tile(n, cands):
    for c in cands:
        if n % c == 0:
            return c
    raise ValueError(n)


def _mm_auto(a, b, mode, out_dtype, name):
    m = a.shape[1] if mode == "tn" else a.shape[0]
    n = b.shape[0] if mode == "nt" else b.shape[1]
    return _mm(a, b, mode, _tile(m, (512, 256, 128)), _tile(n, (1024, 1408, 512, 384, 256, 128)), out_dtype, name)


def _mm_nn_slots(a, bs, out_dtype, name, after=()):
    (m, k), w = a.shape, bs.shape[2]
    tm = _tile(m, (1024, 512, 256, 128))

    def body(a_ref, b_ref, *rest):
        o_ref = rest[len(after)]
        o_ref[...] = _dot(a_ref[...], b_ref[...], NN).astype(out_dtype)

    return pl.pallas_call(
        body, grid=(m // tm, N_SLOTS),
        in_specs=[pl.BlockSpec((tm, k), lambda i, j: (i, 0)), pl.BlockSpec((None, k, w), lambda i, j: (j, 0, 0))]
        + [pl.BlockSpec(memory_space=pl.ANY)] * len(after),
        out_specs=pl.BlockSpec((tm, w), lambda i, j: (i, j)),
        out_shape=_sds((m, N_SLOTS * w), out_dtype), compiler_params=_cp(), name=name)(a, bs, *after)


def _mm_nt_rmsbwd(a, b, x, gain, dres, name, after=()):
    m, k = a.shape
    slots = b.ndim == 3
    assert b.shape[-2] == D
    tm = _tile(m, (512, 256, 128))

    def body(a_ref, b_ref, x_ref, g_ref, r_ref, *rest):
        dx_ref, dxb_ref, dg_ref = rest[len(after):]
        if slots:
            w = b.shape[2]
            dh = _dot(a_ref[:, 0:w], b_ref[0], NT)
            for s in range(1, N_SLOTS):
                dh = dh + _dot(a_ref[:, s * w:(s + 1) * w], b_ref[s], NT)
        else:
            dh = _dot(a_ref[...], b_ref[...], NT)
        _, vjp = jax.vjp(_rms, x_ref[...], g_ref[...])
        dx, dg = vjp(dh)
        dx = dx + r_ref[...]
        dx_ref[...] = dx
        dxb_ref[...] = dx.astype(bf16)

        @pl.when(pl.program_id(0) == 0)
        def _():
            dg_ref[...] = jnp.zeros_like(dg_ref)

        dg_ref[...] += dg

    row = pl.BlockSpec((tm, D), lambda i: (i, 0))
    vec = pl.BlockSpec((1, D), lambda i: (0, 0))
    b_spec = pl.BlockSpec(b.shape, (lambda i: (0, 0, 0)) if slots else (lambda i: (0, 0)), pipeline_mode=pl.Buffered(1))
    return pl.pallas_call(
        body, grid=(m // tm,),
        in_specs=[pl.BlockSpec((tm, k), lambda i: (i, 0)), b_spec, row, vec, row] + [pl.BlockSpec(memory_space=pl.ANY)] * len(after),
        out_specs=[row, row, vec],
        out_shape=[_sds((m, D), f32), _sds((m, D), bf16), _sds((1, D), f32)],
        compiler_params=_cp(), name=name)(a, b, x, gain, dres, *after)


def _mm_tn_slots(a, b, name):
    (k, m), w = a.shape, b.shape[1] // N_SLOTS
    tm = _tile(m, (512, 256, 128))

    def body(a_ref, b_ref, o_ref):
        o_ref[...] = _dot(a_ref[...], b_ref[...], TN).astype(bf16)

    return pl.pallas_call(
        body, grid=(N_SLOTS, m // tm),
        in_specs=[pl.BlockSpec((k, tm), lambda j, i: (0, i)), pl.BlockSpec((k, w), lambda j, i: (0, j))],
        out_specs=pl.BlockSpec((None, tm, w), lambda j, i: (j, i, 0)),
        out_shape=_sds((N_SLOTS, m, w), bf16), compiler_params=_cp(), name=name)(a, b)


def _rmsnorm_fwd(x, gain, name, after=()):
    t = x.shape[0]

    def body(x_ref, g_ref, *rest):
        h_ref = rest[len(after)]
        h_ref[...] = _rms(x_ref[...], g_ref[...]).astype(bf16)

    return pl.pallas_call(
        body, grid=(t // TT,),
        in_specs=[pl.BlockSpec((TT, D), lambda i: (i, 0)), pl.BlockSpec((1, D), lambda i: (0, 0))]
        + [pl.BlockSpec(memory_space=pl.ANY)] * len(after),
        out_specs=pl.BlockSpec((TT, D), lambda i: (i, 0)),
        out_shape=_sds((t, D), bf16), compiler_params=_cp(), name=name)(x, gain, *after)


def _final_loss(x, gain, target, name):
    t = x.shape[0]

    def loss_fn(xv, gv, tv):
        e = _rms(xv, gv) - tv
        return 0.5 * jnp.sum(jnp.mean(e * e, axis=-1))

    def body(x_ref, g_ref, t_ref, loss_ref, dx_ref, dxb_ref, dg_ref):
        val, (dx, dg) = jax.value_and_grad(loss_fn, argnums=(0, 1))(x_ref[...], g_ref[...], t_ref[...])

        @pl.when(pl.program_id(0) == 0)
        def _():
            dg_ref[...] = jnp.zeros_like(dg_ref)
            loss_ref[...] = jnp.zeros_like(loss_ref)

        dx_ref[...] = dx
        dxb_ref[...] = dx.astype(bf16)
        dg_ref[...] += dg
        loss_ref[...] += jnp.full((1, 128), val, f32)

    row = pl.BlockSpec((TT, D), lambda i: (i, 0))
    vec = pl.BlockSpec((1, D), lambda i: (0, 0))
    return pl.pallas_call(
        body, grid=(t // TT,), in_specs=[row, vec, row],
        out_specs=[pl.BlockSpec((1, 128), lambda i: (0, 0)), row, row, vec],
        out_shape=[_sds((1, 128), f32), _sds((t, D), f32), _sds((t, D), bf16), _sds((1, D), f32)],
        compiler_params=_cp(), name=name)(x, gain, target)


def _halo_rows(dtype):
    return 16 if dtype == bf16 else 8


def _conv_fwd(x, col0, c, w, b, name, tc=256, val=None, val_col0=0, out_dtype=f32):
    t = x.shape[0]
    width = w.shape[0]
    nt = t // TT
    hr = _halo_rows(x.dtype)
    hb = TT // hr

    def body(*refs):
        if val is None:
            x_ref, xh_ref, w_ref, b_ref, o_ref, xp = refs
        else:
            x_ref, xh_ref, w_ref, b_ref, v_ref, o_ref, act_ref, xp = refs
        i = pl.program_id(1)
        xp[0:8, :] = jnp.where(i == 0, 0.0, xh_ref[hr - 8:hr, :].astype(f32))
        xp[8:, :] = x_ref[...].astype(f32)
        acc = jnp.zeros((TT, tc), f32) + b_ref[...]
        for k in range(width):
            acc = acc + w_ref[k:k + 1, :] * xp[pl.ds(8 - (width - 1) + k, TT), :]
        o_ref[...] = acc.astype(out_dtype)
        if val is not None:
            act_ref[...] = (_silu(acc) * v_ref[...]).astype(bf16)

    in_specs = [
        pl.BlockSpec((TT, tc), lambda j, i: (i, j + col0)),
        pl.BlockSpec((hr, tc), lambda j, i: (jnp.maximum(i * hb - 1, 0), j + col0)),
        pl.BlockSpec((width, tc), lambda j, i: (0, j)),
        pl.BlockSpec((1, tc), lambda j, i: (0, j)),
    ]
    args = [x, x, w, b]
    out_specs = [pl.BlockSpec((TT, tc), lambda j, i: (i, j))]
    out_shape = [_sds((t, c), out_dtype)]
    if val is not None:
        in_specs.append(pl.BlockSpec((TT, tc), lambda j, i: (i, j + val_col0)))
        args.append(val)
        out_specs.append(pl.BlockSpec((TT, tc), lambda j, i: (i, j)))
        out_shape.append(_sds((t, c), bf16))
    res = pl.pallas_call(
        body, grid=(c // tc, nt), in_specs=in_specs, out_specs=out_specs, out_shape=out_shape,
        scratch_shapes=[pltpu.VMEM((TT + 8, tc), f32)], compiler_params=_cp(), name=name)(*args)
    return res[0] if val is None else res


def _conv_bwd(dc, x, col0, w, name, tc=256, dx_dtype=bf16, into=None):
    t, c = dc.shape
    width = w.shape[0]
    nt = t // TT
    hr = _halo_rows(x.dtype)
    hd = _halo_rows(dc.dtype)

    def body(dc_ref, dcn_ref, x_ref, xh_ref, w_ref, *rest):
        dx_ref, dw_ref, db_ref, dcp, xp = rest[(0 if into is None else 1):]
        i = pl.program_id(1)
        dcv = dc_ref[...].astype(f32)
        dcp[0:TT, :] = dcv
        dcp[TT:, :] = jnp.where(i == nt - 1, 0.0, dcn_ref[0:8, :].astype(f32))
        xp[0:8, :] = jnp.where(i == 0, 0.0, xh_ref[hr - 8:hr, :].astype(f32))
        xp[8:, :] = x_ref[...].astype(f32)

        @pl.when(i == 0)
        def _():
            dw_ref[...] = jnp.zeros_like(dw_ref)
            db_ref[...] = jnp.zeros_like(db_ref)

        acc = jnp.zeros((TT, tc), f32)
        for k in range(width):
            acc = acc + w_ref[k:k + 1, :] * dcp[pl.ds((width - 1) - k, TT), :]
            dw_ref[k:k + 1, :] += jnp.sum(dcv * xp[pl.ds(8 - (width - 1) + k, TT), :], axis=0, keepdims=True)
        dx_ref[...] = acc.astype(dx_dtype)
        db_ref[...] += jnp.sum(dcv, axis=0, keepdims=True)

    in_specs = [
        pl.BlockSpec((TT, tc), lambda j, i: (i, j)),
        pl.BlockSpec((hd, tc), lambda j, i: (jnp.minimum((i + 1) * (TT // hd), t // hd - 1), j)),
        pl.BlockSpec((TT, tc), lambda j, i: (i, j + col0)),
        pl.BlockSpec((hr, tc), lambda j, i: (jnp.maximum(i * (TT // hr) - 1, 0), j + col0)),
        pl.BlockSpec((width, tc), lambda j, i: (0, j)),
    ]
    out_specs = [
        pl.BlockSpec((TT, tc), lambda j, i: (i, j)),
        pl.BlockSpec((8, tc), lambda j, i: (0, j)),
        pl.BlockSpec((1, tc), lambda j, i: (0, j)),
    ]
    args, aliases, dx_shape = [dc, dc, x, x, w], {}, _sds((t, c), dx_dtype)
    if into is not None:
        in_specs.append(pl.BlockSpec(memory_space=pl.ANY))
        args.append(into)
        aliases, dx_shape = {5: 0}, _sds(into.shape, into.dtype)
    return pl.pallas_call(
        body, grid=(c // tc, nt), in_specs=in_specs, out_specs=out_specs,
        out_shape=[dx_shape, _sds((8, c), f32), _sds((1, c), f32)], input_output_aliases=aliases,
        scratch_shapes=[pltpu.VMEM((TT + 8, tc), f32), pltpu.VMEM((TT + 8, tc), f32)],
        compiler_params=_cp(), name=name)(*args)


def _bdot_impl(a, b, dims):
    return _dot(a.astype(bf16), b.astype(bf16), dims)


@functools.partial(jax.custom_vjp, nondiff_argnums=(2,))
def _bdot(a, b, dims):
    return _bdot_impl(a, b, dims)


def _bdot_fwd(a, b, dims):
    return _bdot_impl(a, b, dims), (a, b)


def _bdot_bwd(dims, res, ct):
    a, b = res
    if dims == NN:
        return _bdot_impl(ct, b, NT), _bdot_impl(a, ct, TN)
    if dims == NT:
        return _bdot_impl(ct, b, NN), _bdot_impl(ct, a, TN)
    return _bdot_impl(b, ct, NT), _bdot_impl(a, ct, NN)


_bdot.defvjp(_bdot_fwd, _bdot_bwd)


def _split2(a):
    hi = a.astype(bf16)
    return hi, (a - hi.astype(f32)).astype(bf16)


def _dot3_impl(a, b, dims):
    a_hi, a_lo = _split2(a)
    b_hi, b_lo = _split2(b)
    return (_dot(a_hi, b_hi, dims) + _dot(a_hi, b_lo, dims)) + _dot(a_lo, b_hi, dims)


def _tril_dot_impl(tril, x, dims):
    t = tril.astype(bf16)
    x1 = x.astype(bf16)
    r1 = x - x1.astype(f32)
    x2 = r1.astype(bf16)
    x3 = (r1 - x2.astype(f32)).astype(bf16)
    return (_dot(t, x3, dims) + _dot(t, x2, dims)) + _dot(t, x1, dims)


@jax.custom_vjp
def _cumsum_rows(tril, x):
    return _tril_dot_impl(tril, x, NN)


def _cumsum_rows_fwd(tril, x):
    return _tril_dot_impl(tril, x, NN), tril


def _cumsum_rows_bwd(tril, ct):
    return jnp.zeros_like(tril), _tril_dot_impl(tril, ct, TN)


_cumsum_rows.defvjp(_cumsum_rows_fwd, _cumsum_rows_bwd)


def _nilpotent_inverse(n):
    hs = range(len(n))
    c = n[0].shape[0]
    eye = (lax.broadcasted_iota(jnp.int32, (c, c), 0) == lax.broadcasted_iota(jnp.int32, (c, c), 1)).astype(f32)
    t = [eye + n[h] for h in hs]
    p = n
    for _ in range(5):
        p = [_dot3_impl(p[h], p[h], NN) for h in hs]
        tp = [_dot3_impl(t[h], p[h], NN) for h in hs]
        t = [t[h] + tp[h] for h in hs]
    return t


@jax.custom_vjp
def _nilpotent_solve(n, rhs):
    t = _nilpotent_inverse(n)
    return [_dot3_impl(t[h], rhs[h], NN) for h in range(len(n))]


def _nilpotent_solve_fwd(n, rhs):
    t = _nilpotent_inverse(n)
    sol = [_dot3_impl(t[h], rhs[h], NN) for h in range(len(n))]
    return sol, (t, sol)


def _nilpotent_solve_bwd(res, ct):
    t, sol = res
    hs = range(len(t))
    d_rhs = [_dot3_impl(t[h], ct[h], TN) for h in hs]
    d_n = [_dot3_impl(d_rhs[h], sol[h], NT) for h in hs]
    return d_n, d_rhs


_nilpotent_solve.defvjp(_nilpotent_solve_fwd, _nilpotent_solve_bwd)


def _gdn_intra4(cq, ck, cv, ba, alog_v, dtb_v):
    c = GDN_CHUNK
    hs = range(NH)
    lane = lax.broadcasted_iota(jnp.int32, (1, HEAD), 1)
    mb = [(lane == h).astype(f32) for h in hs]
    ma = [(lane == h + NH).astype(f32) for h in hs]
    beta = [jax.nn.sigmoid(jnp.sum(ba * mb[h], axis=1, keepdims=True)) for h in hs]
    alpha = [jnp.sum(ba * ma[h], axis=1, keepdims=True) for h in hs]
    alog = [jnp.sum(alog_v * ma[h], axis=1, keepdims=True) for h in hs]
    dtb = [jnp.sum(dtb_v * ma[h], axis=1, keepdims=True) for h in hs]
    g = [-jnp.exp(alog[h]) * jax.nn.softplus(alpha[h] + dtb[h]) for h in hs]
    q = [_silu(cq[h]) for h in hs]
    q = [q[h] * lax.rsqrt(jnp.sum(q[h] * q[h], axis=-1, keepdims=True) + EPS) * (HEAD ** -0.5) for h in hs]
    k = [_silu(ck[h]) for h in hs]
    k = [k[h] * lax.rsqrt(jnp.sum(k[h] * k[h], axis=-1, keepdims=True) + EPS) for h in hs]
    v = [_silu(cv[h]) for h in hs]
    row = lax.broadcasted_iota(jnp.int32, (c, c), 0)
    col = lax.broadcasted_iota(jnp.int32, (c, c), 1)
    causal = row >= col
    tril = causal.astype(f32)
    gc = [_cumsum_rows(tril, jnp.broadcast_to(g[h], (c, HEAD))) for h in hs]
    gcc = [_cumsum_rows(tril, jnp.broadcast_to(g[h], (c, c))) for h in hs]
    decay = [jnp.where(causal, jnp.exp(jnp.where(causal, gcc[h] - gcc[h].T, 0.0)), 0.0) for h in hs]
    kb = [k[h] * beta[h] for h in hs]
    kk = [_bdot(kb[h], k[h], NT) for h in hs]
    p = [-jnp.where(row > col, kk[h] * decay[h], 0.0) for h in hs]
    egc = [jnp.exp(gc[h]) for h in hs]
    sol = _nilpotent_solve(p, [jnp.concatenate([v[h] * beta[h], kb[h] * egc[h]], axis=1) for h in hs])
    qk = [_bdot(q[h], k[h], NT) for h in hs]
    attn = [qk[h] * decay[h] for h in hs]
    rowv = lax.broadcasted_iota(jnp.int32, (c, 1), 0)
    gc_last = [jnp.sum(jnp.where(rowv == c - 1, gc[h], 0.0), axis=0, keepdims=True) for h in hs]
    return ([sol[h][:, :HEAD] for h in hs], [sol[h][:, HEAD:] for h in hs], [q[h] * egc[h] for h in hs],
            [k[h] * jnp.exp(gc_last[h] - gc[h]) for h in hs], attn, [jnp.exp(gc_last[h]) for h in hs])


def _gdn_seq4(u, w, q_dec, k_dec, attn, dl, z, s, gain):
    hs = range(NH)
    ws = [_bdot(w[h], s[h], NN) for h in hs]
    qs = [_bdot(q_dec[h], s[h], NN) for h in hs]
    v_new = [u[h] - ws[h] for h in hs]
    av = [_bdot(attn[h], v_new[h], NN) for h in hs]
    kv = [_bdot(k_dec[h], v_new[h], TN) for h in hs]
    o = [_rms(qs[h] + av[h], gain) * _silu(z[h]) for h in hs]
    return o, [s[h] * dl[h] + kv[h] for h in hs]


def _hsl(h):
    return slice(h * HEAD, (h + 1) * HEAD)


def _gdn2_fwd(conv, p, alog_v, dtb_v, gain, name):
    t = conv.shape[0]
    c = GDN_CHUNK
    nch = t // c
    w512 = NH * HEAD
    wide = lambda off: pl.BlockSpec((c, w512), lambda n: (n, off))
    vec = pl.BlockSpec((1, HEAD), lambda n: (0, 0))
    attn_spec = pl.BlockSpec((1, NH, c, c), lambda n: (n, 0, 0, 0))
    dl_spec = pl.BlockSpec((1, NH, HEAD), lambda n: (n, 0, 0))

    cps = GDN_INTRA_CHUNKS
    iwide = lambda off: pl.BlockSpec((cps * c, w512), lambda n: (n, off))

    def intra(cq, ck, cv, ba, al, dt, u_ref, w_ref, qd_ref, kd_ref, at_ref, dl_ref):
        for ci in range(cps):
            rows = slice(ci * c, (ci + 1) * c)
            u, w, qd, kd, at, dl = _gdn_intra4([cq[rows, _hsl(h)] for h in range(NH)], [ck[rows, _hsl(h)] for h in range(NH)],
                                               [cv[rows, _hsl(h)] for h in range(NH)], ba[rows, :], al[...], dt[...])
            for h in range(NH):
                u_ref[rows, _hsl(h)] = u[h]
                w_ref[rows, _hsl(h)] = w[h]
                qd_ref[rows, _hsl(h)] = qd[h]
                kd_ref[rows, _hsl(h)] = kd[h]
                at_ref[ci, h] = at[h]
                dl_ref[ci, h:h + 1, :] = dl[h]

    u, w, qd, kd, at, dl = pl.pallas_call(
        intra, grid=(nch // cps,),
        in_specs=[iwide(0), iwide(1), iwide(2), pl.BlockSpec((cps * c, HEAD), lambda n: (n, C_BA // HEAD)), vec, vec],
        out_specs=[iwide(0)] * 4 + [pl.BlockSpec((cps, NH, c, c), lambda n: (n, 0, 0, 0)),
                                    pl.BlockSpec((cps, NH, HEAD), lambda n: (n, 0, 0))],
        out_shape=[_sds((t, w512), f32)] * 4 + [_sds((nch, NH, c, c), f32), _sds((nch, NH, HEAD), f32)],
        compiler_params=_cp(), name=name + "_intra")(conv, conv, conv, p, alog_v, dtb_v)

    def seq(u_ref, w_ref, qd_ref, kd_ref, at_ref, dl_ref, z_ref, gn, o_ref, ss_ref, s_scr):
        @pl.when(pl.program_id(0) == 0)
        def _():
            s_scr[...] = jnp.zeros_like(s_scr)

        hs = range(NH)
        s = [s_scr[h] for h in hs]
        for h in hs:
            ss_ref[0, h] = s[h]
        o, s_new = _gdn_seq4([u_ref[:, _hsl(h)] for h in hs], [w_ref[:, _hsl(h)] for h in hs], [qd_ref[:, _hsl(h)] for h in hs],
                             [kd_ref[:, _hsl(h)] for h in hs], [at_ref[0, h] for h in hs], [dl_ref[0, h:h + 1, :] for h in hs],
                             [z_ref[:, _hsl(h)] for h in hs], s, gn[...])
        for h in hs:
            o_ref[:, _hsl(h)] = o[h].astype(bf16)
            s_scr[h] = s_new[h]

    o, states = pl.pallas_call(
        seq, grid=(nch,),
        in_specs=[wide(0)] * 4 + [attn_spec, dl_spec, wide(C_Z // w512), vec],
        out_specs=[wide(0), pl.BlockSpec((1, NH, HEAD, HEAD), lambda n: (n, 0, 0, 0))],
        out_shape=[_sds((t, w512), bf16), _sds((nch, NH, HEAD, HEAD), f32)],
        scratch_shapes=[pltpu.VMEM((NH, HEAD, HEAD), f32)],
        compiler_params=_cp(), name=name + "_seq")(u, w, qd, kd, at, dl, p, gain)
    return o, dict(u=u, w=w, qd=qd, kd=kd, at=at, dl=dl, states=states)


def _gdn2_bwd(conv, p, alog_v, dtb_v, gain, saved, do, name):
    t = conv.shape[0]
    c = GDN_CHUNK
    nch = t // c
    w512 = NH * HEAD
    rwide = lambda off: pl.BlockSpec((c, w512), lambda n: (nch - 1 - n, off))
    rvec = pl.BlockSpec((1, HEAD), lambda n: (0, 0))
    rattn = pl.BlockSpec((1, NH, c, c), lambda n: (nch - 1 - n, 0, 0, 0))
    rdl = pl.BlockSpec((1, NH, HEAD), lambda n: (nch - 1 - n, 0, 0))

    def seq_bwd(u_ref, w_ref, qd_ref, kd_ref, at_ref, dl_ref, z_ref, gn, ss_ref, do_ref,
                du_ref, dw_ref, dqd_ref, dkd_ref, dat_ref, ddl_ref, dz_ref, dgn_ref, ds_scr):
        @pl.when(pl.program_id(0) == 0)
        def _():
            ds_scr[...] = jnp.zeros_like(ds_scr)
            dgn_ref[...] = jnp.zeros_like(dgn_ref)

        hs = range(NH)
        _, vjp = jax.vjp(_gdn_seq4, [u_ref[:, _hsl(h)] for h in hs], [w_ref[:, _hsl(h)] for h in hs],
                         [qd_ref[:, _hsl(h)] for h in hs], [kd_ref[:, _hsl(h)] for h in hs], [at_ref[0, h] for h in hs],
                         [dl_ref[0, h:h + 1, :] for h in hs], [z_ref[:, _hsl(h)] for h in hs], [ss_ref[0, h] for h in hs], gn[...])
        du, dw, dqd, dkd, dat, ddl, dz, ds, dg = vjp(([do_ref[:, _hsl(h)] for h in hs], [ds_scr[h] for h in hs]))
        for h in hs:
            du_ref[:, _hsl(h)] = du[h]
            dw_ref[:, _hsl(h)] = dw[h]
            dqd_ref[:, _hsl(h)] = dqd[h]
            dkd_ref[:, _hsl(h)] = dkd[h]
            dat_ref[0, h] = dat[h]
            ddl_ref[0, h:h + 1, :] = ddl[h]
            dz_ref[:, _hsl(h)] = dz[h]
            ds_scr[h] = ds[h]
        dgn_ref[...] += dg

    du, dw, dqd, dkd, dat, ddl, dz, dgn = pl.pallas_call(
        seq_bwd, grid=(nch,),
        in_specs=[rwide(0)] * 4 + [rattn, rdl, rwide(C_Z // w512), rvec,
                                   pl.BlockSpec((1, NH, HEAD, HEAD), lambda n: (nch - 1 - n, 0, 0, 0)), rwide(0)],
        out_specs=[rwide(0)] * 4 + [rattn, rdl, rwide(0), rvec],
        out_shape=[_sds((t, w512), f32)] * 4 + [_sds((nch, NH, c, c), f32), _sds((nch, NH, HEAD), f32),
                                                _sds((t, w512), f32), _sds((1, HEAD), f32)],
        scratch_shapes=[pltpu.VMEM((NH, HEAD, HEAD), f32)],
        compiler_params=_cp(), name=name + "_seq")(
            saved["u"], saved["w"], saved["qd"], saved["kd"], saved["at"], saved["dl"], p, gain, saved["states"], do)

    cps = GDN_INTRA_CHUNKS
    wide = lambda off: pl.BlockSpec((cps * c, w512), lambda n: (n, off))
    vec = pl.BlockSpec((1, HEAD), lambda n: (0, 0))
    attn_spec = pl.BlockSpec((cps, NH, c, c), lambda n: (n, 0, 0, 0))
    dl_spec = pl.BlockSpec((cps, NH, HEAD), lambda n: (n, 0, 0))

    def intra_bwd(cq, ck, cv, ba, al, dt, du_ref, dw_ref, dqd_ref, dkd_ref, dat_ref, ddl_ref,
                  dc_ref, dba_ref, dal_ref, ddt_ref):
        @pl.when(pl.program_id(0) == 0)
        def _():
            dal_ref[...] = jnp.zeros_like(dal_ref)
            ddt_ref[...] = jnp.zeros_like(ddt_ref)

        dal = jnp.zeros((1, HEAD), f32)
        ddt = jnp.zeros((1, HEAD), f32)
        for ci in range(cps):
            rows = slice(ci * c, (ci + 1) * c)
            hs = range(NH)
            _, vjp = jax.vjp(_gdn_intra4, [cq[rows, _hsl(h)] for h in hs], [ck[rows, _hsl(h)] for h in hs],
                             [cv[rows, _hsl(h)] for h in hs], ba[rows, :], al[...], dt[...])
            g_q, g_k, g_v, g_ba, g_al, g_dt = vjp((
                [du_ref[rows, _hsl(h)] for h in hs], [dw_ref[rows, _hsl(h)] for h in hs], [dqd_ref[rows, _hsl(h)] for h in hs],
                [dkd_ref[rows, _hsl(h)] for h in hs], [dat_ref[ci, h] for h in hs], [ddl_ref[ci, h:h + 1, :] for h in hs]))
            for h in hs:
                dc_ref[rows, _hsl(h)] = g_q[h]
                dc_ref[rows, _hsl(NH + h)] = g_k[h]
                dc_ref[rows, _hsl(2 * NH + h)] = g_v[h]
            dal = dal + g_al
            ddt = ddt + g_dt
            dba_ref[rows, :] = g_ba
        dal_ref[...] += dal
        ddt_ref[...] += ddt

    dconv, dba, dal, ddt = pl.pallas_call(
        intra_bwd, grid=(nch // cps,),
        in_specs=[wide(0), wide(1), wide(2), pl.BlockSpec((cps * c, HEAD), lambda n: (n, C_BA // HEAD)), vec, vec]
        + [wide(0)] * 4 + [attn_spec, dl_spec],
        out_specs=[pl.BlockSpec((cps * c, 3 * w512), lambda n: (n, 0)), pl.BlockSpec((cps * c, HEAD), lambda n: (n, 0)),
                   vec, vec],
        out_shape=[_sds((t, 3 * w512), f32), _sds((t, HEAD), f32), _sds((1, HEAD), f32), _sds((1, HEAD), f32)],
        compiler_params=_cp(), name=name + "_intra")(conv, conv, conv, p, alog_v, dtb_v, du, dw, dqd, dkd, dat, ddl)
    return dconv, dz, dba, dal, ddt, dgn


def _hgrn_intra(qb, fb, ib, lb):
    c = HGRN_CHUNK
    ns = range(len(qb))
    f = [lb + (1.0 - lb) * jax.nn.sigmoid(fb[i]) for i in ns]
    logf = [jnp.log(jnp.maximum(f[i], F_FLOOR)) for i in ns]
    k = [1.0 - f[i] for i in ns]
    q = [_silu(qb[i]) for i in ns]
    row = lax.broadcasted_iota(jnp.int32, (c, c), 0)
    col = lax.broadcasted_iota(jnp.int32, (c, c), 1)
    tril = (row >= col).astype(f32)
    b = [_cumsum_rows(tril, logf[i]) for i in ns]
    ri = lax.broadcasted_iota(jnp.int32, (c, 1), 0)
    o = [jnp.zeros((c, HEAD), f32) for _ in ns]
    for j in range(c):
        mj = ri == j
        ok = ri >= j
        bj = [jnp.sum(jnp.where(mj, b[i], 0.0), axis=0, keepdims=True) for i in ns]
        kj = [jnp.sum(jnp.where(mj, k[i], 0.0), axis=0, keepdims=True) for i in ns]
        vj = [jnp.sum(jnp.where(mj, ib[i], 0.0), axis=0, keepdims=True) for i in ns]
        e = [jnp.where(ok, jnp.exp(jnp.where(ok, b[i] - bj[i], 0.0)), 0.0) for i in ns]
        s = [jnp.sum(q[i] * kj[i] * e[i], axis=1, keepdims=True) for i in ns]
        o = [o[i] + s[i] * vj[i] for i in ns]
    b_last = [jnp.sum(jnp.where(ri == c - 1, b[i], 0.0), axis=0, keepdims=True) for i in ns]
    return (o, [q[i] * jnp.exp(b[i]) for i in ns], [k[i] * jnp.exp(b_last[i] - b[i]) for i in ns],
            [jnp.exp(b_last[i]) for i in ns])


def _hgrn_seq_heads(o_intra, q_dec, k_dec, dl, v, gb, st, gain):
    hs = range(len(st))
    qs = [_bdot(q_dec[h], st[h], NT) for h in hs]
    vk = [_bdot(v[h], k_dec[h], TN) for h in hs]
    o = [_rms(o_intra[h] + qs[h], gain) * _silu(gb[h]) for h in hs]
    return o, [st[h] * dl[h] + vk[h] for h in hs]


def _hgrn_tiles(ref, rows):
    return [[ref[rw, _hsl(h)] for rw in rows] for h in range(NH)]


def _hgrn_fwd(p, lb, gain, name):
    t = p.shape[0]
    r = HGRN_STEP
    ns = t // r
    nsub = r // HGRN_CHUNK
    w512 = NH * HEAD
    blk = lambda off: pl.BlockSpec((r, w512), lambda n: (n, off // w512))

    def body(qb, fb, ib, gb, lb_ref, gn, o_ref, ss_ref, s_scr):
        @pl.when(pl.program_id(0) == 0)
        def _():
            s_scr[...] = jnp.zeros_like(s_scr)

        hs = range(NH)
        rows = [pl.ds(ch * HGRN_CHUNK, HGRN_CHUNK) for ch in range(nsub)]
        q_t, f_t, v_t, g_t = (_hgrn_tiles(ref, rows) for ref in (qb, fb, ib, gb))
        intra = [_hgrn_intra(q_t[h], f_t[h], v_t[h], lb_ref[:, _hsl(h)]) for h in hs]
        st = [s_scr[h] for h in hs]
        for h in hs:
            ss_ref[0, h] = st[h]
        for ch in range(nsub):
            o, st = _hgrn_seq_heads([intra[h][0][ch] for h in hs], [intra[h][1][ch] for h in hs], [intra[h][2][ch] for h in hs],
                                    [intra[h][3][ch] for h in hs], [v_t[h][ch] for h in hs], [g_t[h][ch] for h in hs], st, gn[...])
            for h in hs:
                o_ref[rows[ch], _hsl(h)] = o[h].astype(bf16)
        for h in hs:
            s_scr[h] = st[h]

    return pl.pallas_call(
        body, grid=(ns,),
        in_specs=[blk(C_QB), blk(C_FB), blk(C_IB), blk(C_GB),
                  pl.BlockSpec((1, w512), lambda n: (0, 0)), pl.BlockSpec((1, HEAD), lambda n: (0, 0))],
        out_specs=[pl.BlockSpec((r, w512), lambda n: (n, 0)),
                   pl.BlockSpec((1, NH, HEAD, HEAD), lambda n: (n, 0, 0, 0))],
        out_shape=[_sds((t, w512), bf16), _sds((ns, NH, HEAD, HEAD), f32)],
        scratch_shapes=[pltpu.VMEM((NH, HEAD, HEAD), f32)],
        compiler_params=_cp(), name=name)(p, p, p, p, lb, gain)


def _hgrn_bwd(p, lb, gain, states, do, name):
    t = p.shape[0]
    r = HGRN_STEP
    ns = t // r
    nsub = r // HGRN_CHUNK
    w512 = NH * HEAD
    blk = lambda off: pl.BlockSpec((r, w512), lambda n: (ns - 1 - n, off // w512))

    def body(qb, fb, ib, gb, lb_ref, gn, ss_ref, do_ref, dp_ref, dlb_ref, dgn_ref, ds_scr, st_scr):
        @pl.when(pl.program_id(0) == 0)
        def _():
            ds_scr[...] = jnp.zeros_like(ds_scr)
            dgn_ref[...] = jnp.zeros_like(dgn_ref)

        hs = range(NH)
        gnv = gn[...]
        rows = [pl.ds(ch * HGRN_CHUNK, HGRN_CHUNK) for ch in range(nsub)]
        q_t, f_t, v_t, g_t = (_hgrn_tiles(ref, rows) for ref in (qb, fb, ib, gb))
        intra, vjp_intra = [], []
        for h in hs:
            out, fn = jax.vjp(_hgrn_intra, q_t[h], f_t[h], v_t[h], lb_ref[:, _hsl(h)])
            intra.append(out)
            vjp_intra.append(fn)
        at = lambda part, ch: [intra[h][part][ch] for h in hs]
        st = [ss_ref[0, h] for h in hs]
        for ch in range(nsub):
            for h in hs:
                st_scr[ch * NH + h] = st[h]
            if ch < nsub - 1:
                vk = [_bdot(v_t[h][ch], intra[h][2][ch], TN) for h in hs]
                st = [st[h] * intra[h][3][ch] + vk[h] for h in hs]
        ds = [ds_scr[h] for h in hs]
        dgn = jnp.zeros((1, HEAD), f32)
        cots = [[[None] * nsub for _ in range(4)] for _ in hs]
        d_v = [[None] * nsub for _ in hs]
        for ch in reversed(range(nsub)):
            _, vjp = jax.vjp(_hgrn_seq_heads, at(0, ch), at(1, ch), at(2, ch), at(3, ch), [v_t[h][ch] for h in hs],
                             [g_t[h][ch] for h in hs], [st_scr[ch * NH + h] for h in hs], gnv)
            d_oi, d_qd, d_kd, d_dl, d_vv, g_g, ds, g_gn = vjp(([do_ref[rows[ch], _hsl(h)] for h in hs], ds))
            for h in hs:
                cots[h][0][ch], cots[h][1][ch], cots[h][2][ch], cots[h][3][ch] = d_oi[h], d_qd[h], d_kd[h], d_dl[h]
                d_v[h][ch] = d_vv[h]
                dp_ref[rows[ch], pl.ds(3 * w512 + h * HEAD, HEAD)] = g_g[h].astype(bf16)
            dgn = dgn + g_gn
        for h in hs:
            g_q, g_f, g_i, g_lb = vjp_intra[h](tuple(cots[h]))
            for ch in range(nsub):
                dp_ref[rows[ch], pl.ds(h * HEAD, HEAD)] = g_q[ch].astype(bf16)
                dp_ref[rows[ch], pl.ds(w512 + h * HEAD, HEAD)] = g_f[ch].astype(bf16)
                dp_ref[rows[ch], pl.ds(2 * w512 + h * HEAD, HEAD)] = (g_i[ch] + d_v[h][ch]).astype(bf16)
            dlb_ref[0, :, _hsl(h)] = g_lb
            ds_scr[h] = ds[h]
        dgn_ref[...] += dgn

    return pl.pallas_call(
        body, grid=(ns,),
        in_specs=[blk(C_QB), blk(C_FB), blk(C_IB), blk(C_GB),
                  pl.BlockSpec((1, w512), lambda n: (0, 0)), pl.BlockSpec((1, HEAD), lambda n: (0, 0)),
                  pl.BlockSpec((1, NH, HEAD, HEAD), lambda n: (ns - 1 - n, 0, 0, 0)),
                  pl.BlockSpec((r, w512), lambda n: (ns - 1 - n, 1))],
        out_specs=[pl.BlockSpec((r, 4 * w512), lambda n: (ns - 1 - n, 0)),
                   pl.BlockSpec((1, 1, w512), lambda n: (n, 0, 0)),
                   pl.BlockSpec((1, HEAD), lambda n: (0, 0))],
        out_shape=[_sds((t, 4 * w512), bf16), _sds((ns, 1, w512), f32), _sds((1, HEAD), f32)],
        scratch_shapes=[pltpu.VMEM((NH, HEAD, HEAD), f32), pltpu.VMEM((nsub * NH, HEAD, HEAD), f32)],
        compiler_params=_cp(), name=name)(p, p, p, p, lb, gain, states, do)


def _lru_gates(xb, wa, wx, ba, bx, lam):
    xh = xb.astype(bf16)
    r = jax.nn.sigmoid(_dot(xh, wa.astype(bf16), NN) + ba)
    i = jax.nn.sigmoid(_dot(xh, wx.astype(bf16), NN) + bx)
    log_a = -RG_C * r * jax.nn.softplus(-lam)
    a = jnp.exp(log_a)
    t2 = 2.0 * log_a
    series = -t2 * (1.0 + t2 * (0.5 + t2 * (1.0 / 6.0 + t2 * (1.0 / 24.0))))
    om = jnp.where(t2 > -1e-2, series, 1.0 - jnp.exp(t2))
    u = jnp.sqrt(jnp.maximum(om, 0.0)) * (i * xb)
    return a, u


def _lru_gates_fwd(xc, wa, wx, ba, bx, lam, name):
    t = xc.shape[0]
    blk = pl.BlockSpec((TT, LRU_BLOCK), lambda h, i: (i, h))
    wsp = pl.BlockSpec((1, LRU_BLOCK, LRU_BLOCK), lambda h, i: (h, 0, 0))
    vsp = pl.BlockSpec((1, LRU_BLOCK), lambda h, i: (0, h))

    def body(x_ref, wa_ref, wx_ref, ba_ref, bx_ref, lam_ref, a_ref, u_ref):
        a, u = _lru_gates(x_ref[...], wa_ref[0], wx_ref[0], ba_ref[...], bx_ref[...], lam_ref[...])
        a_ref[...] = a
        u_ref[...] = u

    return pl.pallas_call(
        body, grid=(NH, t // TT), in_specs=[blk, wsp, wsp, vsp, vsp, vsp], out_specs=[blk, blk],
        out_shape=[_sds((t, D), f32)] * 2, compiler_params=_cp(), name=name)(xc, wa, wx, ba, bx, lam)


def _lru_gates_bwd(xc, wa, wx, ba, bx, lam, da, du, name):
    t = xc.shape[0]
    blk = pl.BlockSpec((TT, LRU_BLOCK), lambda h, i: (i, h))
    wsp = pl.BlockSpec((1, LRU_BLOCK, LRU_BLOCK), lambda h, i: (h, 0, 0))
    vsp = pl.BlockSpec((1, LRU_BLOCK), lambda h, i: (0, h))

    def body(x_ref, wa_ref, wx_ref, ba_ref, bx_ref, lam_ref, da_ref, du_ref,
             dx_ref, dwa_ref, dwx_ref, dba_ref, dbx_ref, dlam_ref):
        @pl.when(pl.program_id(1) == 0)
        def _():
            for r in (dwa_ref, dwx_ref, dba_ref, dbx_ref, dlam_ref):
                r[...] = jnp.zeros_like(r)

        _, vjp = jax.vjp(_lru_gates, x_ref[...], wa_ref[0], wx_ref[0], ba_ref[...], bx_ref[...], lam_ref[...])
        dx, dwa, dwx, dba, dbx, dlam = vjp((da_ref[...], du_ref[...]))
        dx_ref[...] = dx
        dwa_ref[0] += dwa
        dwx_ref[0] += dwx
        dba_ref[...] += dba
        dbx_ref[...] += dbx
        dlam_ref[...] += dlam

    return pl.pallas_call(
        body, grid=(NH, t // TT), in_specs=[blk, wsp, wsp, vsp, vsp, vsp, blk, blk],
        out_specs=[blk, wsp, wsp, vsp, vsp, vsp],
        out_shape=[_sds((t, D), f32), _sds((NH, LRU_BLOCK, LRU_BLOCK), f32), _sds((NH, LRU_BLOCK, LRU_BLOCK), f32),
                   _sds((1, D), f32), _sds((1, D), f32), _sds((1, D), f32)],
        compiler_params=_cp(), name=name)(xc, wa, wx, ba, bx, lam, da, du)


_SCAN_SHIFTS = (1, 2, 4, 8, 16, 32, 64, 128)
_SCAN_PAD = 128


def _gelu(y):
    return jax.nn.gelu(y, approximate=True)


def _lru_scan_fwd(a, u, p2, name):
    t = a.shape[0]
    tc = 128
    blk = pl.BlockSpec((TT, tc), lambda j, i: (i, j))

    def body(a_ref, u_ref, y_ref, h_ref, hg_ref, a_s, b_s, carry):
        i = pl.program_id(1)

        @pl.when(i == 0)
        def _():
            carry[...] = jnp.zeros_like(carry)
            a_s[0:_SCAN_PAD, :] = jnp.ones((_SCAN_PAD, tc), f32)
            b_s[0:_SCAN_PAD, :] = jnp.zeros((_SCAN_PAD, tc), f32)

        av, bv = a_ref[...], u_ref[...]
        for s in _SCAN_SHIFTS:
            if s < 8:
                a_s[_SCAN_PAD:, :] = av
                b_s[_SCAN_PAD:, :] = bv
                ash = a_s[pl.ds(_SCAN_PAD - s, TT), :]
                bsh = b_s[pl.ds(_SCAN_PAD - s, TT), :]
            else:
                ash = jnp.concatenate([jnp.ones((s, tc), f32), av[:TT - s]], axis=0)
                bsh = jnp.concatenate([jnp.zeros((s, tc), f32), bv[:TT - s]], axis=0)
            bv = bv + av * bsh
            av = av * ash
        h = bv + av * carry[7:8, :]
        h_ref[...] = h
        hg_ref[...] = (h * _gelu(y_ref[...])).astype(bf16)
        carry[...] = h[TT - 8:, :]

    return pl.pallas_call(
        body, grid=(D // tc, t // TT), in_specs=[blk, blk, blk], out_specs=[blk, blk],
        out_shape=[_sds((t, D), f32), _sds((t, D), bf16)],
        scratch_shapes=[pltpu.VMEM((_SCAN_PAD + TT, tc), f32), pltpu.VMEM((_SCAN_PAD + TT, tc), f32),
                        pltpu.VMEM((8, tc), f32)],
        compiler_params=_cp(), name=name)(a, u, p2)


def _lru_scan_bwd(a, h, p2, dhg, name):
    t = a.shape[0]
    tc = 128
    nt = t // TT
    hb = TT // 8
    rblk = pl.BlockSpec((TT, tc), lambda j, i: (nt - 1 - i, j))

    def body(a_ref, an_ref, h_ref, hp_ref, y_ref, dhg_ref, du_ref, da_ref, dy_ref, a_s, b_s, ap, hp, carry):
        i = pl.program_id(1)

        @pl.when(i == 0)
        def _():
            carry[...] = jnp.zeros_like(carry)
            a_s[TT:, :] = jnp.ones((_SCAN_PAD, tc), f32)
            b_s[TT:, :] = jnp.zeros((_SCAN_PAD, tc), f32)

        ap[0:TT, :] = a_ref[...]
        ap[TT:, :] = jnp.where(i == 0, 0.0, an_ref[...])
        hp[0:8, :] = jnp.where(i == nt - 1, 0.0, hp_ref[...])
        hp[8:, :] = h_ref[...]
        y = y_ref[...]
        gate, gvjp = jax.vjp(_gelu, y)
        dhg_v = dhg_ref[...]
        dy_ref[...] = gvjp(dhg_v * h_ref[...])[0]
        av = ap[pl.ds(1, TT), :]
        bv = dhg_v * gate
        for s in _SCAN_SHIFTS:
            if s < 8:
                a_s[0:TT, :] = av
                b_s[0:TT, :] = bv
                ash = a_s[pl.ds(s, TT), :]
                bsh = b_s[pl.ds(s, TT), :]
            else:
                ash = jnp.concatenate([av[s:], jnp.ones((s, tc), f32)], axis=0)
                bsh = jnp.concatenate([bv[s:], jnp.zeros((s, tc), f32)], axis=0)
            bv = bv + av * bsh
            av = av * ash
        g = bv + av * carry[0:1, :]
        du_ref[...] = g
        da_ref[...] = g * hp[pl.ds(7, TT), :]
        carry[...] = g[0:8, :]

    in_specs = [
        rblk,
        pl.BlockSpec((8, tc), lambda j, i: (jnp.minimum((nt - i) * hb, t // 8 - 1), j)),
        rblk,
        pl.BlockSpec((8, tc), lambda j, i: (jnp.maximum((nt - 1 - i) * hb - 1, 0), j)),
        rblk, rblk,
    ]
    return pl.pallas_call(
        body, grid=(D // tc, nt), in_specs=in_specs, out_specs=[rblk, rblk, rblk],
        out_shape=[_sds((t, D), f32)] * 3,
        scratch_shapes=[pltpu.VMEM((TT + _SCAN_PAD, tc), f32), pltpu.VMEM((TT + _SCAN_PAD, tc), f32),
                        pltpu.VMEM((TT + 8, tc), f32), pltpu.VMEM((TT + 8, tc), f32), pltpu.VMEM((8, tc), f32)],
        compiler_params=_cp(), name=name)(a, a, h, h, p2, dhg)


def _ffn_act_bwd(gc, up, dact, name, tc=256):
    t = gc.shape[0]
    blk = pl.BlockSpec((TT, tc), lambda i, j: (i, j))
    vblk = pl.BlockSpec((TT, tc), lambda i, j: (i, j + D_FF // tc))

    def body(gc_ref, v_ref, da_ref, dgc_ref, dv_ref):
        _, vjp = jax.vjp(lambda g, v: _silu(g) * v, gc_ref[...].astype(f32), v_ref[...])
        dg, dv = vjp(da_ref[...].astype(f32))
        dgc_ref[...] = dg.astype(bf16)
        dv_ref[...] = dv.astype(bf16)

    return pl.pallas_call(
        body, grid=(t // TT, D_FF // tc), in_specs=[blk, vblk, blk], out_specs=[blk, vblk],
        out_shape=[_sds((t, D_FF), bf16), _sds((t, 2 * D_FF), bf16)], compiler_params=_cp(), name=name)(gc, up, dact)


def _lower_bounds_fwd(w):
    def body(w_ref, o0_ref, o1_ref):
        wv = w_ref[...]
        o0, o1 = _lb_rows(wv[0:1, :], wv[1:2, :])
        o0_ref[...] = o0
        o1_ref[...] = o1

    return pl.pallas_call(body, out_shape=[_sds((1, 512), f32)] * 2, name="lower_bounds_fwd")(w)


def _lb_rows(w0, w1):
    m = jnp.maximum(w0, w1)
    e0, e1 = jnp.exp(w0 - m), jnp.exp(w1 - m)
    s = e0 + e1
    p0, p1 = e0 / s, e1 / s
    return p0 - p0, (p0 + p1) - p0


def _lower_bounds_bwd(w, d0, d1):
    def body(w_ref, d0_ref, d1_ref, g0_ref, g1_ref):
        wv = w_ref[...]
        _, vjp = jax.vjp(_lb_rows, wv[0:1, :], wv[1:2, :])
        g0, g1 = vjp((d0_ref[...], d1_ref[...]))
        g0_ref[...] = g0
        g1_ref[...] = g1

    return pl.pallas_call(body, out_shape=[_sds((1, 512), f32)] * 2, name="lower_bounds_bwd")(w, d0, d1)


def _local_step(x, target, wt, pre_layer=None, post_grads=None):
    depth = 4
    res = []
    lb0, lb1 = _lower_bounds_fwd(wt["hgrn_lower_bounds"])
    lbs = [lb0, lb1]
    for layer in range(depth):
        j = layer // 2
        sv = {"x_in": x}
        deps = pre_layer(layer, "mix", x) if pre_layer else ()
        if layer == 0:
            h1 = _rmsnorm_fwd(x, wt["norm_mix"][layer], "rms_fwd", deps)
        sv["h1"] = h1
        if layer % 2 == 0:
            p = _mm_auto(h1, wt["ab_w_in"][j], "nn", f32, "mm_ab_in")
            conv = _conv_fwd(p, 0, 1536, wt["gdn_conv_w"][j], jnp.zeros((1, 1536), f32), "gdn_conv_fwd", tc=768)
            o_a, s_a = _gdn2_fwd(conv, p, wt["alog_v"][j], wt["dtb_v"][j], wt["gdn_norm"][j], "gdn_fwd")
            o_b, s_b = _hgrn_fwd(p, lbs[j], wt["hgrn_norm"][j], "hgrn_fwd")
            o = jnp.concatenate([o_a, o_b], axis=1)
            x, h2 = _mm_res_norm(o, wt["ab_w_out"][j], x, wt["norm_ffn"][layer], "mm_ab_out")
            sv.update(p=p, conv=conv, s_a=s_a, s_b=s_b, o=o)
        else:
            p2 = _mm_nn_slots(h1, wt["c_w_in"][j], f32, "mm_c_in", deps)
            xc = _conv_fwd(p2, 1, D, wt["c_conv_w"][j], wt["c_conv_b"][j], "lru_conv_fwd", tc=D)
            a, u = _lru_gates_fwd(xc, wt["c_gate_a_w"][j], wt["c_gate_x_w"][j], wt["c_gate_a_b"][j],
                                  wt["c_gate_x_b"][j], wt["c_lambda"][j], "lru_gates_fwd")
            h, hg = _lru_scan_fwd(a, u, p2, "lru_scan_fwd")
            x, h2 = _mm_res_norm(hg, wt["c_w_out"][j], x, wt["norm_ffn"][layer], "mm_c_out")
            sv.update(p2=p2, xc=xc, a=a, h=h, hg=hg)
        sv["x_mid"] = x
        deps = pre_layer(layer, "ffn", x) if pre_layer else ()
        up = _mm_nn_slots(h2, wt["ffn_w_up"][layer], bf16, "mm_up", deps)
        gc, act = _conv_fwd(up, 0, D_FF, wt["ffn_conv_w"][layer], wt["ffn_conv_b"][layer], "ffn_conv_fwd",
                            tc=D_FF // 2, val=up, val_col0=2, out_dtype=bf16)
        if layer + 1 < depth:
            x, h1 = _mm_res_norm(act, wt["ffn_w_down"][layer], x, wt["norm_mix"][layer + 1], "mm_down")
        else:
            x = _mm_res_norm(act, wt["ffn_w_down"][layer], x, None, "mm_down_last")
        sv.update(h2=h2, up=up, gc=gc, act=act)
        res.append(sv)

    loss, dx, dxb, d_norm_final = _final_loss(x, wt["norm_final"], target, "final_loss")

    g = {k: [None] * len(v) for k, v in wt.items() if isinstance(v, list)}
    g["norm_final"] = d_norm_final
    d_lbs = [None, None]
    for layer in reversed(range(depth)):
        j = layer // 2
        sv = res[layer]
        dact = _mm_auto(dxb, wt["ffn_w_down"][layer], "nt", bf16, "mm_down_dx")
        g["ffn_w_down"][layer] = _mm_auto(sv["act"], dxb, "tn", bf16, "mm_down_dw")
        dgc, dup = _ffn_act_bwd(sv["gc"], sv["up"], dact, "ffn_act_bwd", tc=D_FF // 2)
        dup, dcw, dcb = _conv_bwd(dgc, sv["up"], 0, wt["ffn_conv_w"][layer], "ffn_conv_bwd", tc=D_FF // 2, into=dup)
        g["ffn_conv_w"][layer] = dcw[:3]
        g["ffn_conv_b"][layer] = dcb
        g["ffn_w_up"][layer] = _mm_tn_slots(sv["h2"], dup, "mm_up_dw")
        deps = post_grads(layer, "ffn", g) if post_grads else ()
        dx, dxb, g["norm_ffn"][layer] = _mm_nt_rmsbwd(dup, wt["ffn_w_up"][layer], sv["x_mid"], wt["norm_ffn"][layer], dx,
                                                      "mm_up_dx", deps)
        if layer % 2 == 0:
            do = _mm_auto(dxb, wt["ab_w_out"][j], "nt", f32, "mm_ab_out_dx")
            g["ab_w_out"][j] = _mm_auto(sv["o"], dxb, "tn", bf16, "mm_ab_out_dw")
            dconv, dz, dba, dal, ddt, dgn = _gdn2_bwd(
                sv["conv"], sv["p"], wt["alog_v"][j], wt["dtb_v"][j], wt["gdn_norm"][j], sv["s_a"], do, "gdn_bwd")
            g["alog_v"][j], g["dtb_v"][j], g["gdn_norm"][j] = dal, ddt, dgn
            dqkv, dcw, _ = _conv_bwd(dconv, sv["p"], 0, wt["gdn_conv_w"][j], "gdn_conv_bwd", tc=768)
            g["gdn_conv_w"][j] = dcw[:4]
            dqfig, dlb, dhn = _hgrn_bwd(sv["p"], lbs[j], wt["hgrn_norm"][j], sv["s_b"], do, "hgrn_bwd")
            g["hgrn_norm"][j] = dhn
            d_lbs[j] = jnp.sum(dlb, axis=0)
            dp = jnp.concatenate([dqkv, dz.astype(bf16), dqfig, dba.astype(bf16)], axis=1)
            g["ab_w_in"][j] = _mm_auto(sv["h1"], dp, "tn", bf16, "mm_ab_in_dw")
            dpre, wpre = dp, wt["ab_w_in"][j]
        else:
            dhg = _mm_auto(dxb, wt["c_w_out"][j], "nt", f32, "mm_c_out_dx")
            g["c_w_out"][j] = _mm_auto(sv["hg"], dxb, "tn", bf16, "mm_c_out_dw")
            du, da, dy = _lru_scan_bwd(sv["a"], sv["h"], sv["p2"], dhg, "lru_scan_bwd")
            dxc, dwa, dwx, dba_, dbx_, dlam = _lru_gates_bwd(
                sv["xc"], wt["c_gate_a_w"][j], wt["c_gate_x_w"][j], wt["c_gate_a_b"][j], wt["c_gate_x_b"][j],
                wt["c_lambda"][j], da, du, "lru_gates_bwd")
            g["c_gate_a_w"][j], g["c_gate_x_w"][j] = dwa, dwx
            g["c_gate_a_b"][j], g["c_gate_x_b"][j], g["c_lambda"][j] = dba_, dbx_, dlam
            dxbr, dcw, dcb = _conv_bwd(dxc, sv["p2"], 1, wt["c_conv_w"][j], "lru_conv_bwd", tc=D)
            g["c_conv_w"][j] = dcw[:4]
            g["c_conv_b"][j] = dcb
            dp2 = jnp.concatenate([dy.astype(bf16), dxbr], axis=1)
            g["c_w_in"][j] = _mm_tn_slots(sv["h1"], dp2, "mm_c_in_dw")
            dpre, wpre = dp2, wt["c_w_in"][j]
        deps = post_grads(layer, "mix", g) if post_grads else ()
        dx, dxb, g["norm_mix"][layer] = _mm_nt_rmsbwd(dpre, wpre, sv["x_in"], wt["norm_mix"][layer], dx, "mm_mix_in_dx", deps)
    g0, g1 = _lower_bounds_bwd(wt["hgrn_lower_bounds"], d_lbs[0], d_lbs[1])
    g["hgrn_lower_bounds"] = jnp.concatenate([g0, g1], axis=0)
    return loss, dx, g


def _ab_in_to_compute(w):
    return jnp.concatenate([w[:, :2048], w[:, 2056:4104], w[:, 2048:2056], jnp.zeros((D, 120), w.dtype)], axis=1)


def _ab_in_from_compute(g):
    return jnp.concatenate([g[:, :2048], g[:, 4096:4104], g[:, 2048:4096]], axis=1)


def _lane_vec(v4):
    return jnp.zeros((1, HEAD), f32).at[0, NH:2 * NH].set(v4)


def _layout_weights(fw):
    wt = {}
    wt["norm_mix"] = [fw["norm_mix"][l][None] for l in range(4)]
    wt["norm_ffn"] = [fw["norm_ffn"][l][None] for l in range(4)]
    wt["norm_final"] = fw["norm_final"][None]
    wt["gdn_conv_w"] = [fw["gdn_conv_w"][j] for j in range(2)]
    wt["alog_v"] = [_lane_vec(fw["gdn_a_log"][j]) for j in range(2)]
    wt["dtb_v"] = [_lane_vec(fw["gdn_dt_bias"][j]) for j in range(2)]
    wt["gdn_norm"] = [fw["gdn_norm"][j][None] for j in range(2)]
    wt["hgrn_lower_bounds"] = fw["hgrn_lower_bounds"]
    wt["hgrn_norm"] = [fw["hgrn_norm"][j][None] for j in range(2)]
    wt["c_conv_w"] = [fw["c_conv_w"][j] for j in range(2)]
    for k in ("c_conv_b", "c_gate_a_b", "c_gate_x_b", "c_lambda"):
        wt[k] = [fw[k][j][None] for j in range(2)]
    wt["ffn_conv_w"] = [fw["ffn_conv_w"][l] for l in range(4)]
    wt["ffn_conv_b"] = [fw["ffn_conv_b"][l][None] for l in range(4)]
    if "ab_w_in" in fw:
        wt["ab_w_in"] = [_ab_in_to_compute(fw["ab_w_in"][j].astype(bf16)) for j in range(2)]
        for k in ("ab_w_out", "c_w_out"):
            wt[k] = [fw[k][j].astype(bf16) for j in range(2)]
        wt["c_w_in"] = [_to_slots(fw["c_w_in"][j].astype(bf16)) for j in range(2)]
        for k in ("c_gate_a_w", "c_gate_x_w"):
            wt[k] = [fw[k][j].astype(f32) for j in range(2)]
        wt["ffn_w_up"] = [_to_slots(fw["ffn_w_up"][l].astype(bf16)) for l in range(4)]
        wt["ffn_w_down"] = [fw["ffn_w_down"][l].astype(bf16) for l in range(4)]
    return wt


SLOT_MAJOR = ("c_w_in", "ffn_w_up")
NATIVE_PERM = {"ab_w_in": (2, 0, 1), "ffn_conv_w": (1, 0, 2)}


def _to_slots(wfull):
    k, n = wfull.shape
    return wfull.reshape(k, N_SLOTS, n // N_SLOTS).transpose(1, 0, 2)


def _layer_full(name, slots):
    kind = BIG[name]
    if name in SLOT_MAJOR:
        return slots
    if kind == "col":
        return _ab_in_to_compute(slots.transpose(1, 0, 2).reshape(slots.shape[1], -1))
    if kind == "row":
        return slots.reshape(-1, slots.shape[2])
    return slots.reshape(4, NH, LRU_BLOCK // 4, LRU_BLOCK).transpose(1, 0, 2, 3).reshape(NH, LRU_BLOCK, LRU_BLOCK).astype(f32)


def _layer_slots(name, g):
    kind = BIG[name]
    if name in SLOT_MAJOR:
        return g
    if kind == "col":
        g = _ab_in_from_compute(g)
        r, cdim = g.shape
        return g.reshape(r, 4, cdim // 4).transpose(1, 0, 2).astype(bf16)
    if kind == "row":
        r, cdim = g.shape
        return g.reshape(4, r // 4, cdim).astype(bf16)
    return g.reshape(NH, 4, LRU_BLOCK // 4, LRU_BLOCK).transpose(1, 0, 2, 3).reshape(4, LRU_BLOCK, LRU_BLOCK).astype(bf16)


def _unlayout_grads(g):
    out = {}
    for k in ("norm_mix", "norm_ffn", "gdn_norm", "hgrn_norm", "c_conv_b", "c_gate_a_b", "c_gate_x_b", "c_lambda",
              "ffn_conv_b"):
        out[k] = jnp.concatenate(g[k], axis=0)
    out["norm_final"] = g["norm_final"][0]
    out["ab_w_in"] = jnp.stack([_ab_in_from_compute(t) for t in g["ab_w_in"]])
    out["gdn_a_log"] = jnp.stack([t[0, NH:2 * NH] for t in g["alog_v"]])
    out["gdn_dt_bias"] = jnp.stack([t[0, NH:2 * NH] for t in g["dtb_v"]])
    out["hgrn_lower_bounds"] = g["hgrn_lower_bounds"]
    for k in ("gdn_conv_w", "ab_w_out", "c_conv_w", "c_gate_a_w", "c_gate_x_w", "c_w_out", "ffn_conv_w", "ffn_w_down"):
        out[k] = jnp.stack(g[k])
    for k in SLOT_MAJOR:
        out[k] = jnp.stack([t.transpose(1, 0, 2).reshape(t.shape[1], -1) for t in g[k]])
    return out


MESH = pl.DeviceIdType.MESH
ANY = pl.BlockSpec(memory_space=pl.ANY)
CHIP_RELATIONS = ((1, 0), (0, 1), (1, 1))
N_CHIPS = 4


def _coords():
    return lax.axis_index("x"), lax.axis_index("y"), lax.axis_index("c")


def _flip(v, f):
    return 1 - v if f else v


def _half_rows(c, a, align):
    return pl.ds(pl.multiple_of(c * (a // 2), align), a // 2)


def _all_gather_chips(shards, name):
    n = len(shards)
    shapes = [s.shape for s in shards]

    def body(*refs):
        ins, outs = refs[:n], refs[n:2 * n]
        send_sems, recv_sems = refs[2 * n:]
        x, y, c = _coords()
        me = 2 * x + y
        sibling = (x, y, 1 - c)
        started = []
        for p in range(n):
            cp = pltpu.make_async_remote_copy(
                src_ref=ins[p], dst_ref=outs[p].at[me],
                send_sem=send_sems.at[p, 6], recv_sem=recv_sems.at[p, 6],
                device_id=sibling, device_id_type=MESH)
            cp.start()
            started.append(cp)
        for p in range(n):
            mine = _half_rows(c, shapes[p][0], 16)
            for r, (fx, fy) in enumerate(CHIP_RELATIONS):
                cp = pltpu.make_async_remote_copy(
                    src_ref=ins[p].at[mine], dst_ref=outs[p].at[me, mine],
                    send_sem=send_sems.at[p, r], recv_sem=recv_sems.at[p, r],
                    device_id=(_flip(x, fx), _flip(y, fy), c), device_id_type=MESH)
                cp.start()
                started.append(cp)
        for r, (fx, fy) in enumerate(CHIP_RELATIONS):
            k = 2 * _flip(x, fx) + _flip(y, fy)
            for p in range(n):
                mine = _half_rows(c, shapes[p][0], 16)
                pltpu.make_async_remote_copy(
                    src_ref=ins[p].at[mine], dst_ref=outs[p].at[k, mine],
                    send_sem=send_sems.at[p, r], recv_sem=recv_sems.at[p, r],
                    device_id=(_flip(x, fx), _flip(y, fy), c), device_id_type=MESH).wait_recv()
                fwd = pltpu.make_async_remote_copy(
                    src_ref=outs[p].at[k, mine], dst_ref=outs[p].at[k, mine],
                    send_sem=send_sems.at[p, 3 + r], recv_sem=recv_sems.at[p, 3 + r],
                    device_id=sibling, device_id_type=MESH)
                fwd.start()
                started.append(fwd)
        for r, (fx, fy) in enumerate(CHIP_RELATIONS):
            k = 2 * _flip(x, fx) + _flip(y, fy)
            for p in range(n):
                theirs = _half_rows(1 - c, shapes[p][0], 16)
                pltpu.make_async_remote_copy(
                    src_ref=outs[p].at[k, theirs], dst_ref=outs[p].at[k, theirs],
                    send_sem=send_sems.at[p, 3 + r], recv_sem=recv_sems.at[p, 3 + r],
                    device_id=sibling, device_id_type=MESH).wait_recv()
        for p in range(n):
            pltpu.make_async_remote_copy(
                src_ref=ins[p], dst_ref=outs[p].at[me],
                send_sem=send_sems.at[p, 6], recv_sem=recv_sems.at[p, 6],
                device_id=sibling, device_id_type=MESH).wait_recv()
        for cp in started:
            cp.wait_send()

    return pl.pallas_call(
        body, in_specs=[ANY] * n, out_specs=[ANY] * n,
        out_shape=[_sds((N_CHIPS,) + s.shape, s.dtype) for s in shards],
        scratch_shapes=[pltpu.SemaphoreType.DMA((n, 7)), pltpu.SemaphoreType.DMA((n, 7))],
        name=name)(*shards)


N_DEV = 8


def _all_reduce_small(pack, name, after=()):
    rows = pack.shape[0]

    def body(in_ref, *rest):
        sum_ref, all_ref, send_sems, recv_sems = rest[len(after):]
        x, y, c = _coords()
        me = 4 * x + 2 * y + c
        all_ref[me] = in_ref[...]
        cps = []
        for r in range(1, N_DEV):
            fx, fy, fc = (r >> 2) & 1, (r >> 1) & 1, r & 1
            cp = pltpu.make_async_remote_copy(
                src_ref=in_ref, dst_ref=all_ref.at[me],
                send_sem=send_sems.at[r], recv_sem=recv_sems.at[r],
                device_id=(_flip(x, fx), _flip(y, fy), _flip(c, fc)), device_id_type=MESH)
            cp.start()
            cps.append(cp)
        for r in range(1, N_DEV):
            fx, fy, fc = (r >> 2) & 1, (r >> 1) & 1, r & 1
            peer = 4 * _flip(x, fx) + 2 * _flip(y, fy) + _flip(c, fc)
            pltpu.make_async_remote_copy(
                src_ref=in_ref, dst_ref=all_ref.at[peer],
                send_sem=send_sems.at[r], recv_sem=recv_sems.at[r],
                device_id=(x, y, c), device_id_type=MESH).wait_recv()
        for cp in cps:
            cp.wait_send()
        acc = all_ref[0]
        for d in range(1, N_DEV):
            acc = acc + all_ref[d]
        sum_ref[...] = acc

    vm = pl.BlockSpec(memory_space=pltpu.VMEM)
    return pl.pallas_call(
        body, in_specs=[vm] + [pl.BlockSpec(memory_space=pl.ANY)] * len(after), out_specs=[vm, vm],
        out_shape=[_sds((rows, 128), f32), _sds((N_DEV, rows, 128), f32)],
        scratch_shapes=[pltpu.SemaphoreType.DMA((N_DEV,)), pltpu.SemaphoreType.DMA((N_DEV,))],
        name=name)(pack, *after)[0]


ROWS_EW = 128


def _adamw_math(w, g, m, v):
    m = ADAM_B1 * m + (1.0 - ADAM_B1) * g
    v = ADAM_B2 * v + (1.0 - ADAM_B2) * (g * g)
    m_hat = m / (1.0 - ADAM_B1 ** ADAM_STEP)
    v_hat = v / (1.0 - ADAM_B2 ** ADAM_STEP)
    delta = -ADAM_LR * (m_hat / (jnp.sqrt(v_hat) + ADAM_EPS) + ADAM_WD * w)
    return delta, m, v


def _adamw_big(w, g, m, v, name):
    nl, r, b = w.shape
    rt = _tile(r, (ROWS_EW, 352, 64))
    per = r // rt

    def body(w_ref, g_ref, m_ref, v_ref, go_ref, d_ref, mo_ref, vo_ref):
        gv = g_ref[...]
        d, mn, vn = _adamw_math(w_ref[...], gv, m_ref[...], v_ref[...])
        go_ref[...] = gv
        d_ref[...] = d
        mo_ref[...] = mn
        vo_ref[...] = vn

    blk = pl.BlockSpec((None, rt, b), lambda l, i: (l, i, 0))
    gblk = pl.BlockSpec((rt, b), lambda l, i: (l * per + i, 0))
    return pl.pallas_call(body, grid=(nl, per), in_specs=[blk, gblk, blk, blk], out_specs=[blk] * 4,
                          out_shape=[_sds((nl, r, b), f32)] * 4, compiler_params=_cp(), name=name)(w, g, m, v)


def _adamw_lead(w, g, m, v, name):
    n, r, b = w.shape
    tn = _tile(n, (64, 54, 32, 16, 8, 1))

    def body(w_ref, g_ref, m_ref, v_ref, go_ref, d_ref, mo_ref, vo_ref):
        gv = g_ref[...]
        d, mn, vn = _adamw_math(w_ref[...], gv, m_ref[...], v_ref[...])
        go_ref[...] = gv
        d_ref[...] = d
        mo_ref[...] = mn
        vo_ref[...] = vn

    blk = pl.BlockSpec((tn, r, b), lambda i: (i, 0, 0))
    return pl.pallas_call(body, grid=(n // tn,), in_specs=[blk] * 4, out_specs=[blk] * 4,
                          out_shape=[_sds((n, r, b), f32)] * 4, compiler_params=_cp(), name=name)(w, g, m, v)


def _adamw_small(ws, gs, ms, vs, name):
    n = len(ws)

    def body(*refs):
        w_r, g_r, m_r, v_r = refs[:n], refs[n:2 * n], refs[2 * n:3 * n], refs[3 * n:4 * n]
        go_r, d_r, mo_r, vo_r = refs[4 * n:5 * n], refs[5 * n:6 * n], refs[6 * n:7 * n], refs[7 * n:8 * n]
        for p in range(n):
            gv = g_r[p][...]
            d, mn, vn = _adamw_math(w_r[p][...], gv, m_r[p][...], v_r[p][...])
            go_r[p][...] = gv
            d_r[p][...] = d
            mo_r[p][...] = mn
            vo_r[p][...] = vn

    vm = pl.BlockSpec(memory_space=pltpu.VMEM)
    shp = [_sds(w.shape, f32) for w in ws]
    res = pl.pallas_call(body, in_specs=[vm] * (4 * n), out_specs=[vm] * (4 * n), out_shape=shp * 4,
                         name=name)(*ws, *gs, *ms, *vs)
    return res[:n], res[n:2 * n], res[2 * n:3 * n], res[3 * n:]


HBM = pl.BlockSpec(memory_space=pltpu.HBM)
SEM = pl.BlockSpec(memory_space=pltpu.SEMAPHORE)
EFFECT = pltpu.SideEffectType.DATAFLOW_SIDE_EFFECTING
N_REL = 8


def _rel(r):
    return (r >> 2) & 1, (r >> 1) & 1, r & 1


def _gather_copies(ins, lands, send_sems, recv_sems, shapes):
    x, y, c = _coords()
    me = 2 * x + y
    sends, recvs = [], []
    for p in range(len(ins)):
        for r in range(1, N_REL):
            fx, fy, fc = _rel(r)
            peer = (_flip(x, fx), _flip(y, fy), _flip(c, fc))
            if fx == 0 and fy == 0:
                src, dst, got = ins[p], lands[p].at[me], lands[p].at[me]
            else:
                mine = _half_rows(c, shapes[p][0], 16)
                theirs = _half_rows(_flip(c, fc), shapes[p][0], 16)
                src, dst = ins[p].at[mine], lands[p].at[me, mine]
                got = lands[p].at[2 * peer[0] + peer[1], theirs]
            sems = dict(send_sem=send_sems.at[p * N_REL + r], recv_sem=recv_sems.at[p * N_REL + r], device_id=peer,
                        device_id_type=MESH)
            sends.append(pltpu.make_async_remote_copy(src_ref=src, dst_ref=dst, **sems))
            recvs.append(pltpu.make_async_remote_copy(src_ref=src, dst_ref=got, **sems))
    return sends, recvs


def _scatter_copies(ins, lands, send_sems, recv_sems, shapes):
    x, y, c = _coords()
    sends, recvs = [], []
    for p in range(len(ins)):
        for r in range(1, N_REL):
            fx, fy, fc = _rel(r)
            peer = (_flip(x, fx), _flip(y, fy), _flip(c, fc))
            theirs = _half_rows(peer[2], shapes[p][1], 16)
            sems = dict(send_sem=send_sems.at[p * N_REL + r], recv_sem=recv_sems.at[p * N_REL + r], device_id=peer,
                        device_id_type=MESH)
            cp = pltpu.make_async_remote_copy(src_ref=ins[p].at[2 * peer[0] + peer[1], theirs], dst_ref=lands[p].at[r], **sems)
            sends.append(cp)
            recvs.append(cp)
    return sends, recvs


def _split_start(copies_fn, ins, land_shapes, name, after=()):
    n = len(ins)
    shapes = [a.shape for a in ins]

    def body(*refs):
        in_refs, land_refs = refs[:n], refs[n:2 * n]
        send_sems, recv_sems = refs[2 * n + len(after)], refs[2 * n + len(after) + 1]
        token = refs[-1]
        sends, _ = copies_fn(in_refs, land_refs, send_sems, recv_sems, shapes)
        for cp in sends:
            cp.start()
        token[...] = jnp.zeros_like(token)

    lands = [lax.empty(s.shape, s.dtype) for s in land_shapes]
    na = len(after)
    res = pl.pallas_call(
        body, name=name,
        out_shape=(pltpu.SemaphoreType.DMA((n * N_REL,)), pltpu.SemaphoreType.DMA((n * N_REL,)))
        + tuple(pltpu.HBM(a.shape, a.dtype) for a in ins) + tuple(pltpu.HBM(s.shape, s.dtype) for s in land_shapes)
        + (_sds((8, 128), f32),),
        in_specs=[HBM] * (2 * n) + [pl.BlockSpec(memory_space=pl.ANY)] * na,
        out_specs=(SEM, SEM) + (HBM,) * (2 * n) + (pl.BlockSpec(memory_space=pltpu.VMEM),),
        input_output_aliases={i: 2 + i for i in range(2 * n)},
        compiler_params=pltpu.CompilerParams(has_side_effects=EFFECT),
    )(*[pltpu.with_memory_space_constraint(a, pltpu.HBM) for a in ins],
      *[pltpu.with_memory_space_constraint(a, pltpu.HBM) for a in lands], *after)
    return dict(sems=res[:2], ins=res[2:2 + n], lands=res[2 + n:2 + 2 * n], token=res[-1], shapes=shapes)


def _split_wait(copies_fn, started, after, name):
    n = len(started["ins"])
    shapes = started["shapes"]
    na = len(after)

    def body(*refs):
        in_refs, land_refs = refs[:n], refs[n:2 * n]
        send_sems, recv_sems = refs[2 * n], refs[2 * n + 1]
        sends, recvs = copies_fn(in_refs, land_refs, send_sems, recv_sems, shapes)
        for cp in sends:
            cp.wait_send()
        for cp in recvs:
            cp.wait_recv()

    arrs = list(started["ins"]) + list(started["lands"])
    res = pl.pallas_call(
        body, name=name,
        out_shape=tuple(pltpu.HBM(a.shape, a.dtype) for a in arrs),
        in_specs=[HBM] * (2 * n) + [SEM, SEM] + [pl.BlockSpec(memory_space=pl.ANY)] * na,
        out_specs=(HBM,) * (2 * n), input_output_aliases={i: i for i in range(2 * n)},
        compiler_params=pltpu.CompilerParams(has_side_effects=EFFECT),
    )(*arrs, *started["sems"], *after)
    return res[:n], res[n:]


def _sum_pieces(gb, land, ids, f_prev, blk, nblk, name):
    _, a, b = gb.shape
    rows = _tile(a // 2, (ROWS_EW, 176, 64, 32, 16))
    nrt = (a // 2) // rows

    def body(ids_ref, g_ref, l_ref, *rest):
        o_ref = rest[-1]
        acc = g_ref[0].astype(f32)
        for r in range(1, N_REL):
            acc = acc + l_ref[r].astype(f32)
        o_ref[...] = acc

    in_specs = [pl.BlockSpec((1, rows, b), lambda i, ids_ref: (ids_ref[0], ids_ref[1] * nrt + i, 0)),
                pl.BlockSpec((N_REL, rows, b), lambda i, ids_ref: (0, i, 0))]
    args = [ids, gb, land]
    aliases = {}
    if f_prev is not None:
        in_specs.append(pl.BlockSpec(memory_space=pl.ANY))
        args.append(f_prev)
        aliases = {3: 0}
    grid_spec = pltpu.PrefetchScalarGridSpec(
        num_scalar_prefetch=1, grid=(nrt,), in_specs=in_specs,
        out_specs=pl.BlockSpec((rows, b), lambda i, ids_ref: ((2 * blk + ids_ref[1]) * nrt + i, 0)))
    return pl.pallas_call(body, grid_spec=grid_spec, out_shape=_sds((nblk * a, b), f32),
                          input_output_aliases=aliases, compiler_params=_cp(), name=name)(*args)


def _sibling_fill_blocks(fs, nblks, name):
    n = len(fs)
    shapes = [f.shape for f in fs]

    def body(*refs):
        ins, outs = refs[:n], refs[n:2 * n]
        send_sems, recv_sems = refs[2 * n:]
        x, y, c = _coords()
        cps, waits = [], []
        k = 0
        for p in range(n):
            a = shapes[p][0] // nblks[p]
            for bi in range(nblks[p]):
                mine = pl.ds(pl.multiple_of(bi * a + c * (a // 2), 8), a // 2)
                theirs = pl.ds(pl.multiple_of(bi * a + (1 - c) * (a // 2), 8), a // 2)
                sems = dict(send_sem=send_sems.at[k], recv_sem=recv_sems.at[k], device_id=(x, y, 1 - c), device_id_type=MESH)
                cp = pltpu.make_async_remote_copy(src_ref=ins[p].at[mine], dst_ref=outs[p].at[mine], **sems)
                cp.start()
                cps.append(cp)
                waits.append(pltpu.make_async_remote_copy(src_ref=ins[p].at[theirs], dst_ref=outs[p].at[theirs], **sems))
                k += 1
        for wt_ in waits:
            wt_.wait_recv()
        for cp in cps:
            cp.wait_send()

    total = sum(nblks)
    return pl.pallas_call(
        body, in_specs=[ANY] * n, out_specs=[ANY] * n,
        out_shape=[_sds(f.shape, f.dtype) for f in fs],
        input_output_aliases={p: p for p in range(n)},
        scratch_shapes=[pltpu.SemaphoreType.DMA((total,)), pltpu.SemaphoreType.DMA((total,))],
        name=name)(*fs)


WEIGHTS = ["norm_mix", "norm_ffn", "norm_final", "ab_w_in", "gdn_conv_w", "gdn_a_log", "gdn_dt_bias", "gdn_norm",
           "hgrn_lower_bounds", "hgrn_norm", "ab_w_out", "c_w_in", "c_conv_w", "c_conv_b", "c_gate_a_w", "c_gate_a_b",
           "c_gate_x_w", "c_gate_x_b", "c_lambda", "c_w_out", "ffn_w_up", "ffn_conv_w", "ffn_conv_b", "ffn_w_down"]
BIG = {"ab_w_in": "col", "ab_w_out": "row", "c_w_in": "col", "c_gate_a_w": "gate", "c_gate_x_w": "gate",
       "c_w_out": "row", "ffn_w_up": "col", "ffn_w_down": "row"}
SMALL_SHARDED = ["gdn_conv_w", "c_conv_w", "c_conv_b", "c_gate_a_b", "c_gate_x_b", "c_lambda", "ffn_conv_w"]
SMALL = [n for n in WEIGHTS if n not in BIG]
FULL_SHAPES = {
    "norm_mix": (4, 1024), "norm_ffn": (4, 1024), "norm_final": (1024,), "ab_w_in": (2, 1024, 4104),
    "gdn_conv_w": (2, 4, 1536), "gdn_a_log": (2, 4), "gdn_dt_bias": (2, 4), "gdn_norm": (2, 128),
    "hgrn_lower_bounds": (2, 512), "hgrn_norm": (2, 128), "ab_w_out": (2, 1024, 1024), "c_w_in": (2, 1024, 2048),
    "c_conv_w": (2, 4, 1024), "c_conv_b": (2, 1024), "c_gate_a_w": (2, 4, 256, 256), "c_gate_a_b": (2, 1024),
    "c_gate_x_w": (2, 4, 256, 256), "c_gate_x_b": (2, 1024), "c_lambda": (2, 1024), "c_w_out": (2, 1024, 1024),
    "ffn_w_up": (4, 1024, 5632), "ffn_conv_w": (4, 3, 2816), "ffn_conv_b": (4, 2816), "ffn_w_down": (4, 2816, 1024)}


def _pack_rows(arrs, rows):
    parts, used = [], 0
    for a in arrs:
        r = _pack_tile_rows(a.size)
        parts.append(jnp.pad(a.reshape(-1), (0, r * 128 - a.size)).reshape(r, 128))
        used += r
    assert rows >= used and (rows - used) % 8 == 0, (rows, used)
    if rows > used:
        parts.append(jnp.zeros((rows - used, 128), f32))
    return jnp.concatenate(parts, axis=0)


def _pack_tile_rows(size):
    return -(-size // 1024) * 8


def _unpack_rows(pack, shapes):
    out, row = [], 0
    for s in shapes:
        size = 1
        for d in s:
            size *= d
        r = _pack_tile_rows(size)
        out.append(pack[row:row + r].reshape(-1)[:size].reshape(s))
        row += r
    return out


def kernel(x, norm_mix, norm_ffn, norm_final, ab_w_in, gdn_conv_w, gdn_a_log, gdn_dt_bias, gdn_norm, hgrn_lower_bounds, hgrn_norm, ab_w_out, c_w_in, c_conv_w, c_conv_b, c_gate_a_w, c_gate_a_b, c_gate_x_w, c_gate_x_b, c_lambda, c_w_out, ffn_w_up, ffn_conv_w, ffn_conv_b, ffn_w_down, loss_target, m_norm_mix, m_norm_ffn, m_norm_final, m_ab_w_in, m_gdn_conv_w, m_gdn_a_log, m_gdn_dt_bias, m_gdn_norm, m_hgrn_lower_bounds, m_hgrn_norm, m_ab_w_out, m_c_w_in, m_c_conv_w, m_c_conv_b, m_c_gate_a_w, m_c_gate_a_b, m_c_gate_x_w, m_c_gate_x_b, m_c_lambda, m_c_w_out, m_ffn_w_up, m_ffn_conv_w, m_ffn_conv_b, m_ffn_w_down, v_norm_mix, v_norm_ffn, v_norm_final, v_ab_w_in, v_gdn_conv_w, v_gdn_a_log, v_gdn_dt_bias, v_gdn_norm, v_hgrn_lower_bounds, v_hgrn_norm, v_ab_w_out, v_c_w_in, v_c_conv_w, v_c_conv_b, v_c_gate_a_w, v_c_gate_a_b, v_c_gate_x_w, v_c_gate_x_b, v_c_lambda, v_c_w_out, v_ffn_w_up, v_ffn_conv_w, v_ffn_conv_b, v_ffn_w_down):
    w = dict(zip(WEIGHTS, (norm_mix, norm_ffn, norm_final, ab_w_in, gdn_conv_w, gdn_a_log, gdn_dt_bias, gdn_norm, hgrn_lower_bounds, hgrn_norm, ab_w_out, c_w_in, c_conv_w, c_conv_b, c_gate_a_w, c_gate_a_b, c_gate_x_w, c_gate_x_b, c_lambda, c_w_out, ffn_w_up, ffn_conv_w, ffn_conv_b, ffn_w_down)))
    m = dict(zip(WEIGHTS, (m_norm_mix, m_norm_ffn, m_norm_final, m_ab_w_in, m_gdn_conv_w, m_gdn_a_log, m_gdn_dt_bias, m_gdn_norm, m_hgrn_lower_bounds, m_hgrn_norm, m_ab_w_out, m_c_w_in, m_c_conv_w, m_c_conv_b, m_c_gate_a_w, m_c_gate_a_b, m_c_gate_x_w, m_c_gate_x_b, m_c_lambda, m_c_w_out, m_ffn_w_up, m_ffn_conv_w, m_ffn_conv_b, m_ffn_w_down)))
    v = dict(zip(WEIGHTS, (v_norm_mix, v_norm_ffn, v_norm_final, v_ab_w_in, v_gdn_conv_w, v_gdn_a_log, v_gdn_dt_bias, v_gdn_norm, v_hgrn_lower_bounds, v_hgrn_norm, v_ab_w_out, v_c_w_in, v_c_conv_w, v_c_conv_b, v_c_gate_a_w, v_c_gate_a_b, v_c_gate_x_w, v_c_gate_x_b, v_c_lambda, v_c_w_out, v_ffn_w_up, v_ffn_conv_w, v_ffn_conv_b, v_ffn_w_down)))
    big = list(BIG)
    chip = 2 * lax.axis_index("x") + lax.axis_index("y")
    ids = jnp.stack([chip, lax.axis_index("c")]).astype(jnp.int32)

    def layer_parts(l):
        j = l // 2
        if l % 2 == 0:
            mix = [("ab_w_in", j), ("ab_w_out", j)]
        else:
            mix = [("c_w_in", j), ("c_gate_a_w", j), ("c_gate_x_w", j), ("c_w_out", j)]
        return mix, [("ffn_w_up", l), ("ffn_w_down", l)]

    def layer_shard(n, i):
        s = w[n][i]
        return s.reshape(-1, s.shape[-1]).astype(bf16)

    small_shard_shapes = [w[n].shape for n in SMALL_SHARDED]
    small_pack = _pack_rows([w[n] for n in SMALL_SHARDED], 160)
    mix0, ffn0 = layer_parts(0)
    gathered0 = _all_gather_chips([layer_shard(n, i) for n, i in mix0] + [small_pack], "all_gather_mixer0")
    gathers = {}

    def start_gather(key, after):
        shards = [layer_shard(n, i) for n, i in layer_parts(key[0])[0 if key[1] == "mix" else 1]]
        gathers[key] = _split_start(_gather_copies, shards, [_sds((N_CHIPS,) + s.shape, bf16) for s in shards],
                                    "gather_start_%d_%s" % key, after)
        return gathers[key]["token"]

    tok = ()
    for key in ((0, "ffn"), (1, "mix"), (1, "ffn")):
        tok = (start_gather(key, tok),)
    first_tokens = tok
    start_next = {(0, "ffn"): ((2, "mix"), (2, "ffn")), (1, "mix"): ((3, "mix"),), (1, "ffn"): ((3, "ffn"),)}
    fw = {}
    per_chip = [_unpack_rows(gathered0[-1][k], small_shard_shapes) for k in range(N_CHIPS)]
    for i, n in enumerate(SMALL_SHARDED):
        fw[n] = jnp.concatenate([per_chip[k][i] for k in range(N_CHIPS)], axis=-1)
    for n in SMALL:
        if n not in fw:
            fw[n] = w[n]
    wt = _layout_weights(fw)
    for n in big:
        wt[n] = [None] * FULL_SHAPES[n][0]

    def pre_layer(l, part, x_l):
        parts = layer_parts(l)[0 if part == "mix" else 1]
        if l == 0 and part == "mix":
            lands, deps = gathered0[:len(mix0)], first_tokens
        else:
            _, lands = _split_wait(_gather_copies, gathers[(l, part)], [x_l], "gather_wait_%d_%s" % (l, part))
            deps = tuple(lands[:1])
            for key in start_next.get((l, part), ()):
                deps = (start_gather(key, deps),)
            if (l, part) not in start_next:
                deps = ()
        for (n, i), slots in zip(parts, lands):
            wt[n][i] = _layer_full(n, slots)
        return deps

    scatters = []

    def post_grads(l, part, g):
        parts = layer_parts(l)[0 if part == "mix" else 1]
        slots = [_layer_slots(n, g[n][i]) for n, i in parts]
        st = _split_start(_scatter_copies, slots, [_sds((N_REL, s.shape[1] // 2, s.shape[2]), bf16) for s in slots],
                          "scatter_start_%d_%s" % (l, part))
        scatters.append((parts, st, "scatter_wait_%d_%s" % (l, part)))
        return (st["token"],)

    loss, dx, g = _local_step(x[0], loss_target[0], wt, pre_layer, post_grads)
    gf = _unlayout_grads(g)
    loss = lax.psum(loss[0, 0], ("x", "y", "c"))

    out_g, out_d, out_m, out_v = {}, {}, {}, {}
    f = {n: None for n in big}
    last = [n for n, _ in layer_parts(0)[0]]

    def finish(group, after):
        for parts, st, wait_name in group:
            gbs, lands = _split_wait(_scatter_copies, st, after, wait_name)
            for (n, i), gb, land in zip(parts, gbs, lands):
                f[n] = _sum_pieces(gb, land, ids, f[n], i, FULL_SHAPES[n][0], "rs_sum")

    def adamw(names, tag):
        filled = _sibling_fill_blocks([f[n] for n in names], [FULL_SHAPES[n][0] for n in names], "rs_sibling_fill_" + tag)
        for n, g_n in zip(names, filled):
            shp = w[n].shape
            if n in NATIVE_PERM:
                to_native = lambda t: t.transpose(NATIVE_PERM[n])
                res = _adamw_lead(to_native(w[n]), to_native(g_n.reshape(shp)), to_native(m[n]), to_native(v[n]), "adamw_" + n)
                res = [t.transpose(1, 2, 0) for t in res]
            else:
                as3d = lambda t: t.reshape((-1,) + shp[-2:])
                res = _adamw_big(as3d(w[n]), g_n, as3d(m[n]), as3d(v[n]), "adamw_" + n)
            out_g[n], out_d[n], out_m[n], out_v[n] = (t.reshape(shp) for t in res)

    finish(scatters[:-1], [dx])
    adamw([n for n in big if n not in last], "a")
    finish(scatters[-1:], [out_v["ffn_w_up"]])
    adamw(last, "b")

    small_full_shapes = [FULL_SHAPES[n] for n in SMALL]
    small_sum = _all_reduce_small(_pack_rows([gf[n] for n in SMALL], 688), "all_reduce_small", [f[last[0]]])
    g_small = dict(zip(SMALL, _unpack_rows(small_sum, small_full_shapes)))
    for n in SMALL_SHARDED:
        width = w[n].shape[-1]
        g_small[n] = lax.dynamic_slice_in_dim(g_small[n], chip * width, width, axis=-1)

    def small2d(n, t):
        if n in NATIVE_PERM:
            t = t.transpose(NATIVE_PERM[n])
        return t.reshape(-1, t.shape[-1])

    sg, sd, sm, sv = _adamw_small([small2d(n, w[n]) for n in SMALL], [small2d(n, g_small[n]) for n in SMALL],
                                  [small2d(n, m[n]) for n in SMALL], [small2d(n, v[n]) for n in SMALL], "adamw_small")
    for i, n in enumerate(SMALL):
        for out, t in zip((out_g, out_d, out_m, out_v), (sg, sd, sm, sv)):
            if n in NATIVE_PERM:
                perm = NATIVE_PERM[n]
                shp_t = tuple(w[n].shape[p] for p in perm)
                out[n] = t[i].reshape(shp_t).transpose(tuple(perm.index(k) for k in range(len(perm))))
            else:
                out[n] = t[i].reshape(w[n].shape)
    return (loss, dx[None], *[out_g[n] for n in WEIGHTS], *[out_d[n] for n in WEIGHTS],
            *[out_m[n] for n in WEIGHTS], *[out_v[n] for n in WEIGHTS])
```

```python
import functools

import jax
import jax.numpy as jnp
from jax import lax
from jax.experimental import pallas as pl
from jax.experimental.pallas import tpu as pltpu

f32 = jnp.float32
bf16 = jnp.bfloat16

D = 1024
EPS = 1e-6
F_FLOOR = 1e-30
GDN_CHUNK = 64
GDN_INTRA_CHUNKS = 1
HGRN_CHUNK = 16
HGRN_STEP = 128
HEAD = 128
NH = 4
LRU_BLOCK = 256
D_FF = 2816
RG_C = 8.0
C_QKV, C_Z, C_QB, C_FB, C_IB, C_GB, C_BA = 0, 1536, 2048, 2560, 3072, 3584, 4096
TT = 256
ROW_CHUNK = 16
N_SLOTS = 4
VMEM_LIMIT = 56 * 1024 * 1024

ADAM_LR, ADAM_B1, ADAM_B2, ADAM_EPS, ADAM_WD, ADAM_STEP = 0.001, 0.9, 0.999, 1e-08, 0.01, 10


def _cp(**kw):
    return pltpu.CompilerParams(vmem_limit_bytes=VMEM_LIMIT, **kw)


def _sds(shape, dtype):
    return jax.ShapeDtypeStruct(shape, dtype)


def _dot(a, b, dims, precision=None):
    return lax.dot_general(a, b, (dims, ((), ())), precision=precision, preferred_element_type=f32)


NN = ((1,), (0,))
NT = ((1,), (1,))
TN = ((0,), (0,))


def _rms(x, g):
    return x * lax.rsqrt(jnp.mean(x * x, axis=-1, keepdims=True) + EPS) * g


def _silu(x):
    return x * jax.nn.sigmoid(x)


def _mm(a, b, mode, tm, tn, out_dtype, name):
    if mode == "nn":
        (m, k), n = a.shape, b.shape[1]
        a_spec = pl.BlockSpec((tm, k), lambda i, j: (i, 0))
        b_spec = pl.BlockSpec((k, tn), lambda i, j: (0, j))
        dims = NN
    elif mode == "nt":
        (m, k), n = a.shape, b.shape[0]
        a_spec = pl.BlockSpec((tm, k), lambda i, j: (i, 0))
        b_spec = pl.BlockSpec((tn, k), lambda i, j: (j, 0))
        dims = NT
    else:
        (k, m), n = a.shape, b.shape[1]
        a_spec = pl.BlockSpec((k, tm), lambda i, j: (0, i))
        b_spec = pl.BlockSpec((k, tn), lambda i, j: (0, j))
        dims = TN
    assert m % tm == 0 and n % tn == 0, (name, m, n, tm, tn)

    def body(a_ref, b_ref, o_ref):
        o_ref[...] = _dot(a_ref[...], b_ref[...], dims).astype(out_dtype)

    return pl.pallas_call(
        body, grid=(m // tm, n // tn), in_specs=[a_spec, b_spec],
        out_specs=pl.BlockSpec((tm, tn), lambda i, j: (i, j)),
        out_shape=_sds((m, n), out_dtype), compiler_params=_cp(), name=name)(a, b)


def _mm_res_norm(a, b, res, gain, name, tm=512):
    (m, k), n = a.shape, b.shape[1]
    tm = min(tm, m)
    assert n == D and m % tm == 0, (name, m, n)

    def body(a_ref, b_ref, r_ref, *rest):
        xv = r_ref[...] + _dot(a_ref[...], b_ref[...], NN)
        if gain is None:
            rest[0][...] = xv
        else:
            g_ref, x_ref, h_ref = rest
            x_ref[...] = xv
            h_ref[...] = _rms(xv, g_ref[...]).astype(bf16)

    row = pl.BlockSpec((tm, D), lambda i: (i, 0))
    in_specs = [pl.BlockSpec((tm, k), lambda i: (i, 0)), pl.BlockSpec((k, D), lambda i: (0, 0)), row]
    args = [a, b, res]
    if gain is None:
        out_specs, out_shape = row, _sds((m, D), f32)
    else:
        in_specs.append(pl.BlockSpec((1, D), lambda i: (0, 0)))
        args.append(gain)
        out_specs, out_shape = [row, row], [_sds((m, D), f32), _sds((m, D), bf16)]
    return pl.pallas_call(body, grid=(m // tm,), in_specs=in_specs, out_specs=out_specs, out_shape=out_shape,
                          compiler_params=_cp(), name=name)(*args)


def _tile(n, cands):
    for c in cands:
        if n % c == 0:
            return c
    raise ValueError(n)


def _mm_auto(a, b, mode, out_dtype, name):
    m = a.shape[1] if mode == "tn" else a.shape[0]
    n = b.shape[0] if mode == "nt" else b.shape[1]
    return _mm(a, b, mode, _tile(m, (512, 256, 128)), _tile(n, (1024, 1408, 512, 384, 256, 128)), out_dtype, name)


def _mm_nn_slots(a, bs, out_dtype, name, after=()):
    (m, k), w = a.shape, bs.shape[2]
    tm = _tile(m, (1024, 512, 256, 128))

    def body(a_ref, b_ref, *rest):
        o_ref = rest[len(after)]
        o_ref[...] = _dot(a_ref[...], b_ref[...], NN).astype(out_dtype)

    return pl.pallas_call(
        body, grid=(m // tm, N_SLOTS),
        in_specs=[pl.BlockSpec((tm, k), lambda i, j: (i, 0)), pl.BlockSpec((None, k, w), lambda i, j: (j, 0, 0))]
        + [pl.BlockSpec(memory_space=pl.ANY)] * len(after),
        out_specs=pl.BlockSpec((tm, w), lambda i, j: (i, j)),
        out_shape=_sds((m, N_SLOTS * w), out_dtype), compiler_params=_cp(), name=name)(a, bs, *after)


def _mm_nt_rmsbwd(a, b, x, gain, dres, name, after=()):
    m, k = a.shape
    slots = b.ndim == 3
    assert b.shape[-2] == D
    tm = _tile(m, (512, 256, 128)) if k < 4096 else _tile(m, (256, 128))

    def body(a_ref, b_ref, x_ref, g_ref, r_ref, *rest):
        dx_ref, dxb_ref, dg_ref = rest[len(after):]
        if slots:
            w = b.shape[2]
            dh = _dot(a_ref[:, 0:w], b_ref[0], NT)
            for s in range(1, N_SLOTS):
                dh = dh + _dot(a_ref[:, s * w:(s + 1) * w], b_ref[s], NT)
        else:
            dh = _dot(a_ref[...], b_ref[...], NT)
        _, vjp = jax.vjp(_rms, x_ref[...], g_ref[...])
        dx, dg = vjp(dh)
        dx = dx + r_ref[...]
        dx_ref[...] = dx
        dxb_ref[...] = dx.astype(bf16)

        @pl.when(pl.program_id(0) == 0)
        def _():
            dg_ref[...] = jnp.zeros_like(dg_ref)

        dg_ref[...] += dg

    row = pl.BlockSpec((tm, D), lambda i: (i, 0))
    vec = pl.BlockSpec((1, D), lambda i: (0, 0))
    b_spec = pl.BlockSpec(b.shape, (lambda i: (0, 0, 0)) if slots else (lambda i: (0, 0)), pipeline_mode=pl.Buffered(1))
    return pl.pallas_call(
        body, grid=(m // tm,),
        in_specs=[pl.BlockSpec((tm, k), lambda i: (i, 0)), b_spec, row, vec, row] + [pl.BlockSpec(memory_space=pl.ANY)] * len(after),
        out_specs=[row, row, vec],
        out_shape=[_sds((m, D), f32), _sds((m, D), bf16), _sds((1, D), f32)],
        compiler_params=_cp(), name=name)(a, b, x, gain, dres, *after)


def _mm_tn_slots(a, b, name):
    (k, m), w = a.shape, b.shape[1] // N_SLOTS
    tm = _tile(m, (512, 256, 128))

    def body(a_ref, b_ref, o_ref):
        o_ref[...] = _dot(a_ref[...], b_ref[...], TN).astype(bf16)

    return pl.pallas_call(
        body, grid=(N_SLOTS, m // tm),
        in_specs=[pl.BlockSpec((k, tm), lambda j, i: (0, i)), pl.BlockSpec((k, w), lambda j, i: (0, j))],
        out_specs=pl.BlockSpec((None, tm, w), lambda j, i: (j, i, 0)),
        out_shape=_sds((N_SLOTS, m, w), bf16), compiler_params=_cp(), name=name)(a, b)


def _rmsnorm_fwd(x, gain, name, after=()):
    t = x.shape[0]

    def body(x_ref, g_ref, *rest):
        h_ref = rest[len(after)]
        h_ref[...] = _rms(x_ref[...], g_ref[...]).astype(bf16)

    return pl.pallas_call(
        body, grid=(t // TT,),
        in_specs=[pl.BlockSpec((TT, D), lambda i: (i, 0)), pl.BlockSpec((1, D), lambda i: (0, 0))]
        + [pl.BlockSpec(memory_space=pl.ANY)] * len(after),
        out_specs=pl.BlockSpec((TT, D), lambda i: (i, 0)),
        out_shape=_sds((t, D), bf16), compiler_params=_cp(), name=name)(x, gain, *after)


def _final_loss(x, gain, target, name):
    t = x.shape[0]

    def loss_fn(xv, gv, tv):
        e = _rms(xv, gv) - tv
        return 0.5 * jnp.sum(jnp.mean(e * e, axis=-1))

    def body(x_ref, g_ref, t_ref, loss_ref, dx_ref, dxb_ref, dg_ref):
        val, (dx, dg) = jax.value_and_grad(loss_fn, argnums=(0, 1))(x_ref[...], g_ref[...], t_ref[...])

        @pl.when(pl.program_id(0) == 0)
        def _():
            dg_ref[...] = jnp.zeros_like(dg_ref)
            loss_ref[...] = jnp.zeros_like(loss_ref)

        dx_ref[...] = dx
        dxb_ref[...] = dx.astype(bf16)
        dg_ref[...] += dg
        loss_ref[...] += jnp.full((1, 128), val, f32)

    row = pl.BlockSpec((TT, D), lambda i: (i, 0))
    vec = pl.BlockSpec((1, D), lambda i: (0, 0))
    return pl.pallas_call(
        body, grid=(t // TT,), in_specs=[row, vec, row],
        out_specs=[pl.BlockSpec((1, 128), lambda i: (0, 0)), row, row, vec],
        out_shape=[_sds((1, 128), f32), _sds((t, D), f32), _sds((t, D), bf16), _sds((1, D), f32)],
        compiler_params=_cp(), name=name)(x, gain, target)


def _halo_rows(dtype):
    return 16 if dtype == bf16 else 8


def _conv_fwd(x, col0, c, w, b, name, tc=256, val=None, val_col0=0, out_dtype=f32):
    t = x.shape[0]
    width = w.shape[0]
    nt = t // TT
    hr = _halo_rows(x.dtype)
    hb = TT // hr

    def body(*refs):
        if val is None:
            x_ref, xh_ref, w_ref, b_ref, o_ref, xp = refs
        else:
            x_ref, xh_ref, w_ref, b_ref, v_ref, o_ref, act_ref, xp = refs
        i = pl.program_id(1)
        xp[0:8, :] = jnp.where(i == 0, 0.0, xh_ref[hr - 8:hr, :].astype(f32))
        xp[8:, :] = x_ref[...].astype(f32)
        bias = b_ref[...]
        taps = [w_ref[k:k + 1, :] for k in range(width)]
        for r0 in range(0, TT, ROW_CHUNK):
            rows = pl.ds(r0, ROW_CHUNK)
            acc = bias + taps[0] * xp[pl.ds(r0 + 8 - (width - 1), ROW_CHUNK), :]
            for k in range(1, width):
                acc = acc + taps[k] * xp[pl.ds(r0 + 8 - (width - 1) + k, ROW_CHUNK), :]
            o_ref[rows, :] = acc.astype(out_dtype)
            if val is not None:
                act_ref[rows, :] = (_silu(acc) * v_ref[rows, :]).astype(bf16)

    in_specs = [
        pl.BlockSpec((TT, tc), lambda j, i: (i, j + col0)),
        pl.BlockSpec((hr, tc), lambda j, i: (jnp.maximum(i * hb - 1, 0), j + col0)),
        pl.BlockSpec((width, tc), lambda j, i: (0, j)),
        pl.BlockSpec((1, tc), lambda j, i: (0, j)),
    ]
    args = [x, x, w, b]
    out_specs = [pl.BlockSpec((TT, tc), lambda j, i: (i, j))]
    out_shape = [_sds((t, c), out_dtype)]
    if val is not None:
        in_specs.append(pl.BlockSpec((TT, tc), lambda j, i: (i, j + val_col0)))
        args.append(val)
        out_specs.append(pl.BlockSpec((TT, tc), lambda j, i: (i, j)))
        out_shape.append(_sds((t, c), bf16))
    res = pl.pallas_call(
        body, grid=(c // tc, nt), in_specs=in_specs, out_specs=out_specs, out_shape=out_shape,
        scratch_shapes=[pltpu.VMEM((TT + 8, tc), f32)], compiler_params=_cp(), name=name)(*args)
    return res[0] if val is None else res


def _conv_bwd(dc, x, col0, w, name, tc=256, dx_dtype=bf16, into=None):
    t, c = dc.shape
    width = w.shape[0]
    nt = t // TT
    hr = _halo_rows(x.dtype)
    hd = _halo_rows(dc.dtype)

    def body(dc_ref, dcn_ref, x_ref, xh_ref, w_ref, *rest):
        dx_ref, dw_ref, db_ref, dcp, xp = rest[(0 if into is None else 1):]
        i = pl.program_id(1)
        dcv = dc_ref[...].astype(f32)
        dcp[0:TT, :] = dcv
        dcp[TT:, :] = jnp.where(i == nt - 1, 0.0, dcn_ref[0:8, :].astype(f32))
        xp[0:8, :] = jnp.where(i == 0, 0.0, xh_ref[hr - 8:hr, :].astype(f32))
        xp[8:, :] = x_ref[...].astype(f32)

        @pl.when(i == 0)
        def _():
            dw_ref[...] = jnp.zeros_like(dw_ref)
            db_ref[...] = jnp.zeros_like(db_ref)

        acc = jnp.zeros((TT, tc), f32)
        for k in range(width):
            acc = acc + w_ref[k:k + 1, :] * dcp[pl.ds((width - 1) - k, TT), :]
            dw_ref[k:k + 1, :] += jnp.sum(dcv * xp[pl.ds(8 - (width - 1) + k, TT), :], axis=0, keepdims=True)
        dx_ref[...] = acc.astype(dx_dtype)
        db_ref[...] += jnp.sum(dcv, axis=0, keepdims=True)

    in_specs = [
        pl.BlockSpec((TT, tc), lambda j, i: (i, j)),
        pl.BlockSpec((hd, tc), lambda j, i: (jnp.minimum((i + 1) * (TT // hd), t // hd - 1), j)),
        pl.BlockSpec((TT, tc), lambda j, i: (i, j + col0)),
        pl.BlockSpec((hr, tc), lambda j, i: (jnp.maximum(i * (TT // hr) - 1, 0), j + col0)),
        pl.BlockSpec((width, tc), lambda j, i: (0, j)),
    ]
    out_specs = [
        pl.BlockSpec((TT, tc), lambda j, i: (i, j)),
        pl.BlockSpec((8, tc), lambda j, i: (0, j)),
        pl.BlockSpec((1, tc), lambda j, i: (0, j)),
    ]
    args, aliases, dx_shape = [dc, dc, x, x, w], {}, _sds((t, c), dx_dtype)
    if into is not None:
        in_specs.append(pl.BlockSpec(memory_space=pl.ANY))
        args.append(into)
        aliases, dx_shape = {5: 0}, _sds(into.shape, into.dtype)
    return pl.pallas_call(
        body, grid=(c // tc, nt), in_specs=in_specs, out_specs=out_specs,
        out_shape=[dx_shape, _sds((8, c), f32), _sds((1, c), f32)], input_output_aliases=aliases,
        scratch_shapes=[pltpu.VMEM((TT + 8, tc), f32), pltpu.VMEM((TT + 8, tc), f32)],
        compiler_params=_cp(), name=name)(*args)


def _bdot_impl(a, b, dims):
    return _dot(a.astype(bf16), b.astype(bf16), dims)


@functools.partial(jax.custom_vjp, nondiff_argnums=(2,))
def _bdot(a, b, dims):
    return _bdot_impl(a, b, dims)


def _bdot_fwd(a, b, dims):
    return _bdot_impl(a, b, dims), (a, b)


def _bdot_bwd(dims, res, ct):
    a, b = res
    if dims == NN:
        return _bdot_impl(ct, b, NT), _bdot_impl(a, ct, TN)
    if dims == NT:
        return _bdot_impl(ct, b, NN), _bdot_impl(ct, a, TN)
    return _bdot_impl(b, ct, NT), _bdot_impl(a, ct, NN)


_bdot.defvjp(_bdot_fwd, _bdot_bwd)


def _split2(a):
    hi = a.astype(bf16)
    return hi, (a - hi.astype(f32)).astype(bf16)


def _dot3_impl(a, b, dims):
    a_hi, a_lo = _split2(a)
    b_hi, b_lo = _split2(b)
    return (_dot(a_hi, b_hi, dims) + _dot(a_hi, b_lo, dims)) + _dot(a_lo, b_hi, dims)


def _tril_dot_impl(tril, x, dims):
    t = tril.astype(bf16)
    x1 = x.astype(bf16)
    r1 = x - x1.astype(f32)
    x2 = r1.astype(bf16)
    x3 = (r1 - x2.astype(f32)).astype(bf16)
    return (_dot(t, x3, dims) + _dot(t, x2, dims)) + _dot(t, x1, dims)


@jax.custom_vjp
def _cumsum_rows(tril, x):
    return _tril_dot_impl(tril, x, NN)


def _cumsum_rows_fwd(tril, x):
    return _tril_dot_impl(tril, x, NN), tril


def _cumsum_rows_bwd(tril, ct):
    return jnp.zeros_like(tril), _tril_dot_impl(tril, ct, TN)


_cumsum_rows.defvjp(_cumsum_rows_fwd, _cumsum_rows_bwd)


def _nilpotent_inverse(n):
    hs = range(len(n))
    c = n[0].shape[0]
    eye = (lax.broadcasted_iota(jnp.int32, (c, c), 0) == lax.broadcasted_iota(jnp.int32, (c, c), 1)).astype(f32)
    t = [eye + n[h] for h in hs]
    p = n
    for _ in range(5):
        p = [_dot3_impl(p[h], p[h], NN) for h in hs]
        tp = [_dot3_impl(t[h], p[h], NN) for h in hs]
        t = [t[h] + tp[h] for h in hs]
    return t


@jax.custom_vjp
def _nilpotent_solve(n, rhs):
    t = _nilpotent_inverse(n)
    return [_dot3_impl(t[h], rhs[h], NN) for h in range(len(n))]


def _nilpotent_solve_fwd(n, rhs):
    t = _nilpotent_inverse(n)
    sol = [_dot3_impl(t[h], rhs[h], NN) for h in range(len(n))]
    return sol, (t, sol)


def _nilpotent_solve_bwd(res, ct):
    t, sol = res
    hs = range(len(t))
    d_rhs = [_dot3_impl(t[h], ct[h], TN) for h in hs]
    d_n = [_dot3_impl(d_rhs[h], sol[h], NT) for h in hs]
    return d_n, d_rhs


_nilpotent_solve.defvjp(_nilpotent_solve_fwd, _nilpotent_solve_bwd)


def _gdn_intra4(cq, ck, cv, ba, alog_v, dtb_v):
    c = GDN_CHUNK
    hs = range(NH)
    lane = lax.broadcasted_iota(jnp.int32, (1, HEAD), 1)
    mb = [(lane == h).astype(f32) for h in hs]
    ma = [(lane == h + NH).astype(f32) for h in hs]
    beta = [jax.nn.sigmoid(jnp.sum(ba * mb[h], axis=1, keepdims=True)) for h in hs]
    alpha = [jnp.sum(ba * ma[h], axis=1, keepdims=True) for h in hs]
    alog = [jnp.sum(alog_v * ma[h], axis=1, keepdims=True) for h in hs]
    dtb = [jnp.sum(dtb_v * ma[h], axis=1, keepdims=True) for h in hs]
    g = [-jnp.exp(alog[h]) * jax.nn.softplus(alpha[h] + dtb[h]) for h in hs]
    q = [_silu(cq[h]) for h in hs]
    q = [q[h] * lax.rsqrt(jnp.sum(q[h] * q[h], axis=-1, keepdims=True) + EPS) * (HEAD ** -0.5) for h in hs]
    k = [_silu(ck[h]) for h in hs]
    k = [k[h] * lax.rsqrt(jnp.sum(k[h] * k[h], axis=-1, keepdims=True) + EPS) for h in hs]
    v = [_silu(cv[h]) for h in hs]
    row = lax.broadcasted_iota(jnp.int32, (c, c), 0)
    col = lax.broadcasted_iota(jnp.int32, (c, c), 1)
    causal = row >= col
    tril = causal.astype(f32)
    gc = [_cumsum_rows(tril, jnp.broadcast_to(g[h], (c, HEAD))) for h in hs]
    gcc = [_cumsum_rows(tril, jnp.broadcast_to(g[h], (c, c))) for h in hs]
    decay = [jnp.where(causal, jnp.exp(jnp.where(causal, gcc[h] - gcc[h].T, 0.0)), 0.0) for h in hs]
    kb = [k[h] * beta[h] for h in hs]
    kk = [_bdot(kb[h], k[h], NT) for h in hs]
    p = [-jnp.where(row > col, kk[h] * decay[h], 0.0) for h in hs]
    egc = [jnp.exp(gc[h]) for h in hs]
    sol = _nilpotent_solve(p, [jnp.concatenate([v[h] * beta[h], kb[h] * egc[h]], axis=1) for h in hs])
    qk = [_bdot(q[h], k[h], NT) for h in hs]
    attn = [qk[h] * decay[h] for h in hs]
    rowv = lax.broadcasted_iota(jnp.int32, (c, 1), 0)
    gc_last = [jnp.sum(jnp.where(rowv == c - 1, gc[h], 0.0), axis=0, keepdims=True) for h in hs]
    return ([sol[h][:, :HEAD] for h in hs], [sol[h][:, HEAD:] for h in hs], [q[h] * egc[h] for h in hs],
            [k[h] * jnp.exp(gc_last[h] - gc[h]) for h in hs], attn, [jnp.exp(gc_last[h]) for h in hs])


def _gdn_seq4(u, w, q_dec, k_dec, attn, dl, z, s, gain):
    hs = range(NH)
    ws = [_bdot(w[h], s[h], NN) for h in hs]
    qs = [_bdot(q_dec[h], s[h], NN) for h in hs]
    v_new = [u[h] - ws[h] for h in hs]
    av = [_bdot(attn[h], v_new[h], NN) for h in hs]
    kv = [_bdot(k_dec[h], v_new[h], TN) for h in hs]
    o = [_rms(qs[h] + av[h], gain) * _silu(z[h]) for h in hs]
    return o, [s[h] * dl[h] + kv[h] for h in hs]


def _hsl(h):
    return slice(h * HEAD, (h + 1) * HEAD)


def _gdn2_fwd(conv, p, alog_v, dtb_v, gain, name):
    t = conv.shape[0]
    c = GDN_CHUNK
    nch = t // c
    w512 = NH * HEAD
    wide = lambda off: pl.BlockSpec((c, w512), lambda n: (n, off))
    vec = pl.BlockSpec((1, HEAD), lambda n: (0, 0))
    attn_spec = pl.BlockSpec((1, NH, c, c), lambda n: (n, 0, 0, 0))
    dl_spec = pl.BlockSpec((1, NH, HEAD), lambda n: (n, 0, 0))

    cps = GDN_INTRA_CHUNKS
    iwide = lambda off: pl.BlockSpec((cps * c, w512), lambda n: (n, off))

    def intra(cq, ck, cv, ba, al, dt, u_ref, w_ref, qd_ref, kd_ref, at_ref, dl_ref):
        for ci in range(cps):
            rows = slice(ci * c, (ci + 1) * c)
            u, w, qd, kd, at, dl = _gdn_intra4([cq[rows, _hsl(h)] for h in range(NH)], [ck[rows, _hsl(h)] for h in range(NH)],
                                               [cv[rows, _hsl(h)] for h in range(NH)], ba[rows, :], al[...], dt[...])
            for h in range(NH):
                u_ref[rows, _hsl(h)] = u[h]
                w_ref[rows, _hsl(h)] = w[h]
                qd_ref[rows, _hsl(h)] = qd[h]
                kd_ref[rows, _hsl(h)] = kd[h]
                at_ref[ci, h] = at[h]
                dl_ref[ci, h:h + 1, :] = dl[h]

    u, w, qd, kd, at, dl = pl.pallas_call(
        intra, grid=(nch // cps,),
        in_specs=[iwide(0), iwide(1), iwide(2), pl.BlockSpec((cps * c, HEAD), lambda n: (n, C_BA // HEAD)), vec, vec],
        out_specs=[iwide(0)] * 4 + [pl.BlockSpec((cps, NH, c, c), lambda n: (n, 0, 0, 0)),
                                    pl.BlockSpec((cps, NH, HEAD), lambda n: (n, 0, 0))],
        out_shape=[_sds((t, w512), f32)] * 4 + [_sds((nch, NH, c, c), f32), _sds((nch, NH, HEAD), f32)],
        compiler_params=_cp(), name=name + "_intra")(conv, conv, conv, p, alog_v, dtb_v)

    def seq(u_ref, w_ref, qd_ref, kd_ref, at_ref, dl_ref, z_ref, gn, o_ref, ss_ref, s_scr):
        @pl.when(pl.program_id(0) == 0)
        def _():
            s_scr[...] = jnp.zeros_like(s_scr)

        hs = range(NH)
        s = [s_scr[h] for h in hs]
        for h in hs:
            ss_ref[0, h] = s[h]
        o, s_new = _gdn_seq4([u_ref[:, _hsl(h)] for h in hs], [w_ref[:, _hsl(h)] for h in hs], [qd_ref[:, _hsl(h)] for h in hs],
                             [kd_ref[:, _hsl(h)] for h in hs], [at_ref[0, h] for h in hs], [dl_ref[0, h:h + 1, :] for h in hs],
                             [z_ref[:, _hsl(h)] for h in hs], s, gn[...])
        for h in hs:
            o_ref[:, _hsl(h)] = o[h].astype(bf16)
            s_scr[h] = s_new[h]

    o, states = pl.pallas_call(
        seq, grid=(nch,),
        in_specs=[wide(0)] * 4 + [attn_spec, dl_spec, wide(C_Z // w512), vec],
        out_specs=[wide(0), pl.BlockSpec((1, NH, HEAD, HEAD), lambda n: (n, 0, 0, 0))],
        out_shape=[_sds((t, w512), bf16), _sds((nch, NH, HEAD, HEAD), f32)],
        scratch_shapes=[pltpu.VMEM((NH, HEAD, HEAD), f32)],
        compiler_params=_cp(), name=name + "_seq")(u, w, qd, kd, at, dl, p, gain)
    return o, dict(u=u, w=w, qd=qd, kd=kd, at=at, dl=dl, states=states)


def _gdn2_bwd(conv, p, alog_v, dtb_v, gain, saved, do, name):
    t = conv.shape[0]
    c = GDN_CHUNK
    nch = t // c
    w512 = NH * HEAD
    rwide = lambda off: pl.BlockSpec((c, w512), lambda n: (nch - 1 - n, off))
    rvec = pl.BlockSpec((1, HEAD), lambda n: (0, 0))
    rattn = pl.BlockSpec((1, NH, c, c), lambda n: (nch - 1 - n, 0, 0, 0))
    rdl = pl.BlockSpec((1, NH, HEAD), lambda n: (nch - 1 - n, 0, 0))

    def seq_bwd(u_ref, w_ref, qd_ref, kd_ref, at_ref, dl_ref, z_ref, gn, ss_ref, do_ref,
                du_ref, dw_ref, dqd_ref, dkd_ref, dat_ref, ddl_ref, dz_ref, dgn_ref, ds_scr):
        @pl.when(pl.program_id(0) == 0)
        def _():
            ds_scr[...] = jnp.zeros_like(ds_scr)
            dgn_ref[...] = jnp.zeros_like(dgn_ref)

        hs = range(NH)
        _, vjp = jax.vjp(_gdn_seq4, [u_ref[:, _hsl(h)] for h in hs], [w_ref[:, _hsl(h)] for h in hs],
                         [qd_ref[:, _hsl(h)] for h in hs], [kd_ref[:, _hsl(h)] for h in hs], [at_ref[0, h] for h in hs],
                         [dl_ref[0, h:h + 1, :] for h in hs], [z_ref[:, _hsl(h)] for h in hs], [ss_ref[0, h] for h in hs], gn[...])
        du, dw, dqd, dkd, dat, ddl, dz, ds, dg = vjp(([do_ref[:, _hsl(h)] for h in hs], [ds_scr[h] for h in hs]))
        for h in hs:
            du_ref[:, _hsl(h)] = du[h]
            dw_ref[:, _hsl(h)] = dw[h]
            dqd_ref[:, _hsl(h)] = dqd[h]
            dkd_ref[:, _hsl(h)] = dkd[h]
            dat_ref[0, h] = dat[h]
            ddl_ref[0, h:h + 1, :] = ddl[h]
            dz_ref[:, _hsl(h)] = dz[h]
            ds_scr[h] = ds[h]
        dgn_ref[...] += dg

    du, dw, dqd, dkd, dat, ddl, dz, dgn = pl.pallas_call(
        seq_bwd, grid=(nch,),
        in_specs=[rwide(0)] * 4 + [rattn, rdl, rwide(C_Z // w512), rvec,
                                   pl.BlockSpec((1, NH, HEAD, HEAD), lambda n: (nch - 1 - n, 0, 0, 0)), rwide(0)],
        out_specs=[rwide(0)] * 4 + [rattn, rdl, rwide(0), rvec],
        out_shape=[_sds((t, w512), f32)] * 4 + [_sds((nch, NH, c, c), f32), _sds((nch, NH, HEAD), f32),
                                                _sds((t, w512), f32), _sds((1, HEAD), f32)],
        scratch_shapes=[pltpu.VMEM((NH, HEAD, HEAD), f32)],
        compiler_params=_cp(), name=name + "_seq")(
            saved["u"], saved["w"], saved["qd"], saved["kd"], saved["at"], saved["dl"], p, gain, saved["states"], do)

    cps = GDN_INTRA_CHUNKS
    wide = lambda off: pl.BlockSpec((cps * c, w512), lambda n: (n, off))
    vec = pl.BlockSpec((1, HEAD), lambda n: (0, 0))
    attn_spec = pl.BlockSpec((cps, NH, c, c), lambda n: (n, 0, 0, 0))
    dl_spec = pl.BlockSpec((cps, NH, HEAD), lambda n: (n, 0, 0))

    def intra_bwd(cq, ck, cv, ba, al, dt, du_ref, dw_ref, dqd_ref, dkd_ref, dat_ref, ddl_ref,
                  dc_ref, dba_ref, dal_ref, ddt_ref):
        @pl.when(pl.program_id(0) == 0)
        def _():
            dal_ref[...] = jnp.zeros_like(dal_ref)
            ddt_ref[...] = jnp.zeros_like(ddt_ref)

        dal = jnp.zeros((1, HEAD), f32)
        ddt = jnp.zeros((1, HEAD), f32)
        for ci in range(cps):
            rows = slice(ci * c, (ci + 1) * c)
            hs = range(NH)
            _, vjp = jax.vjp(_gdn_intra4, [cq[rows, _hsl(h)] for h in hs], [ck[rows, _hsl(h)] for h in hs],
                             [cv[rows, _hsl(h)] for h in hs], ba[rows, :], al[...], dt[...])
            g_q, g_k, g_v, g_ba, g_al, g_dt = vjp((
                [du_ref[rows, _hsl(h)] for h in hs], [dw_ref[rows, _hsl(h)] for h in hs], [dqd_ref[rows, _hsl(h)] for h in hs],
                [dkd_ref[rows, _hsl(h)] for h in hs], [dat_ref[ci, h] for h in hs], [ddl_ref[ci, h:h + 1, :] for h in hs]))
            for h in hs:
                dc_ref[rows, _hsl(h)] = g_q[h]
                dc_ref[rows, _hsl(NH + h)] = g_k[h]
                dc_ref[rows, _hsl(2 * NH + h)] = g_v[h]
            dal = dal + g_al
            ddt = ddt + g_dt
            dba_ref[rows, :] = g_ba
        dal_ref[...] += dal
        ddt_ref[...] += ddt

    dconv, dba, dal, ddt = pl.pallas_call(
        intra_bwd, grid=(nch // cps,),
        in_specs=[wide(0), wide(1), wide(2), pl.BlockSpec((cps * c, HEAD), lambda n: (n, C_BA // HEAD)), vec, vec]
        + [wide(0)] * 4 + [attn_spec, dl_spec],
        out_specs=[pl.BlockSpec((cps * c, 3 * w512), lambda n: (n, 0)), pl.BlockSpec((cps * c, HEAD), lambda n: (n, 0)),
                   vec, vec],
        out_shape=[_sds((t, 3 * w512), f32), _sds((t, HEAD), f32), _sds((1, HEAD), f32), _sds((1, HEAD), f32)],
        compiler_params=_cp(), name=name + "_intra")(conv, conv, conv, p, alog_v, dtb_v, du, dw, dqd, dkd, dat, ddl)
    return dconv, dz, dba, dal, ddt, dgn


def _hgrn_intra(qb, fb, ib, lb):
    c = HGRN_CHUNK
    ns = range(len(qb))
    f = [lb + (1.0 - lb) * jax.nn.sigmoid(fb[i]) for i in ns]
    logf = [jnp.log(jnp.maximum(f[i], F_FLOOR)) for i in ns]
    k = [1.0 - f[i] for i in ns]
    q = [_silu(qb[i]) for i in ns]
    row = lax.broadcasted_iota(jnp.int32, (c, c), 0)
    col = lax.broadcasted_iota(jnp.int32, (c, c), 1)
    tril = (row >= col).astype(f32)
    b = [_cumsum_rows(tril, logf[i]) for i in ns]
    ri = lax.broadcasted_iota(jnp.int32, (c, 1), 0)
    o = [jnp.zeros((c, HEAD), f32) for _ in ns]
    for j in range(c):
        mj = ri == j
        ok = ri >= j
        bj = [jnp.sum(jnp.where(mj, b[i], 0.0), axis=0, keepdims=True) for i in ns]
        kj = [jnp.sum(jnp.where(mj, k[i], 0.0), axis=0, keepdims=True) for i in ns]
        vj = [jnp.sum(jnp.where(mj, ib[i], 0.0), axis=0, keepdims=True) for i in ns]
        e = [jnp.where(ok, jnp.exp(jnp.where(ok, b[i] - bj[i], 0.0)), 0.0) for i in ns]
        s = [jnp.sum(q[i] * kj[i] * e[i], axis=1, keepdims=True) for i in ns]
        o = [o[i] + s[i] * vj[i] for i in ns]
    b_last = [jnp.sum(jnp.where(ri == c - 1, b[i], 0.0), axis=0, keepdims=True) for i in ns]
    return (o, [q[i] * jnp.exp(b[i]) for i in ns], [k[i] * jnp.exp(b_last[i] - b[i]) for i in ns],
            [jnp.exp(b_last[i]) for i in ns])


def _hgrn_seq_heads(o_intra, q_dec, k_dec, dl, v, gb, st, gain):
    hs = range(len(st))
    qs = [_bdot(q_dec[h], st[h], NT) for h in hs]
    vk = [_bdot(v[h], k_dec[h], TN) for h in hs]
    o = [_rms(o_intra[h] + qs[h], gain) * _silu(gb[h]) for h in hs]
    return o, [st[h] * dl[h] + vk[h] for h in hs]


def _hgrn_tiles(ref, rows):
    return [[ref[rw, _hsl(h)] for rw in rows] for h in range(NH)]


def _hgrn_fwd(p, lb, gain, name):
    t = p.shape[0]
    r = HGRN_STEP
    ns = t // r
    nsub = r // HGRN_CHUNK
    w512 = NH * HEAD
    blk = lambda off: pl.BlockSpec((r, w512), lambda n: (n, off // w512))

    def body(qb, fb, ib, gb, lb_ref, gn, o_ref, ss_ref, s_scr):
        @pl.when(pl.program_id(0) == 0)
        def _():
            s_scr[...] = jnp.zeros_like(s_scr)

        hs = range(NH)
        rows = [pl.ds(ch * HGRN_CHUNK, HGRN_CHUNK) for ch in range(nsub)]
        q_t, f_t, v_t, g_t = (_hgrn_tiles(ref, rows) for ref in (qb, fb, ib, gb))
        intra = [_hgrn_intra(q_t[h], f_t[h], v_t[h], lb_ref[:, _hsl(h)]) for h in hs]
        st = [s_scr[h] for h in hs]
        for h in hs:
            ss_ref[0, h] = st[h]
        for ch in range(nsub):
            o, st = _hgrn_seq_heads([intra[h][0][ch] for h in hs], [intra[h][1][ch] for h in hs], [intra[h][2][ch] for h in hs],
                                    [intra[h][3][ch] for h in hs], [v_t[h][ch] for h in hs], [g_t[h][ch] for h in hs], st, gn[...])
            for h in hs:
                o_ref[rows[ch], _hsl(h)] = o[h].astype(bf16)
        for h in hs:
            s_scr[h] = st[h]

    return pl.pallas_call(
        body, grid=(ns,),
        in_specs=[blk(C_QB), blk(C_FB), blk(C_IB), blk(C_GB),
                  pl.BlockSpec((1, w512), lambda n: (0, 0)), pl.BlockSpec((1, HEAD), lambda n: (0, 0))],
        out_specs=[pl.BlockSpec((r, w512), lambda n: (n, 0)),
                   pl.BlockSpec((1, NH, HEAD, HEAD), lambda n: (n, 0, 0, 0))],
        out_shape=[_sds((t, w512), bf16), _sds((ns, NH, HEAD, HEAD), f32)],
        scratch_shapes=[pltpu.VMEM((NH, HEAD, HEAD), f32)],
        compiler_params=_cp(), name=name)(p, p, p, p, lb, gain)


def _hgrn_bwd(p, lb, gain, states, do, name):
    t = p.shape[0]
    r = HGRN_STEP
    ns = t // r
    nsub = r // HGRN_CHUNK
    w512 = NH * HEAD
    blk = lambda off: pl.BlockSpec((r, w512), lambda n: (ns - 1 - n, off // w512))

    def body(qb, fb, ib, gb, lb_ref, gn, ss_ref, do_ref, dp_ref, dlb_ref, dgn_ref, ds_scr, st_scr):
        @pl.when(pl.program_id(0) == 0)
        def _():
            ds_scr[...] = jnp.zeros_like(ds_scr)
            dgn_ref[...] = jnp.zeros_like(dgn_ref)

        hs = range(NH)
        gnv = gn[...]
        rows = [pl.ds(ch * HGRN_CHUNK, HGRN_CHUNK) for ch in range(nsub)]
        q_t, f_t, v_t, g_t = (_hgrn_tiles(ref, rows) for ref in (qb, fb, ib, gb))
        intra, vjp_intra = [], []
        for h in hs:
            out, fn = jax.vjp(_hgrn_intra, q_t[h], f_t[h], v_t[h], lb_ref[:, _hsl(h)])
            intra.append(out)
            vjp_intra.append(fn)
        at = lambda part, ch: [intra[h][part][ch] for h in hs]
        st = [ss_ref[0, h] for h in hs]
        for ch in range(nsub):
            for h in hs:
                st_scr[ch * NH + h] = st[h]
            if ch < nsub - 1:
                vk = [_bdot(v_t[h][ch], intra[h][2][ch], TN) for h in hs]
                st = [st[h] * intra[h][3][ch] + vk[h] for h in hs]
        ds = [ds_scr[h] for h in hs]
        dgn = jnp.zeros((1, HEAD), f32)
        cots = [[[None] * nsub for _ in range(4)] for _ in hs]
        d_v = [[None] * nsub for _ in hs]
        for ch in reversed(range(nsub)):
            _, vjp = jax.vjp(_hgrn_seq_heads, at(0, ch), at(1, ch), at(2, ch), at(3, ch), [v_t[h][ch] for h in hs],
                             [g_t[h][ch] for h in hs], [st_scr[ch * NH + h] for h in hs], gnv)
            d_oi, d_qd, d_kd, d_dl, d_vv, g_g, ds, g_gn = vjp(([do_ref[rows[ch], _hsl(h)] for h in hs], ds))
            for h in hs:
                cots[h][0][ch], cots[h][1][ch], cots[h][2][ch], cots[h][3][ch] = d_oi[h], d_qd[h], d_kd[h], d_dl[h]
                d_v[h][ch] = d_vv[h]
                dp_ref[rows[ch], pl.ds(3 * w512 + h * HEAD, HEAD)] = g_g[h].astype(bf16)
            dgn = dgn + g_gn
        for h in hs:
            g_q, g_f, g_i, g_lb = vjp_intra[h](tuple(cots[h]))
            for ch in range(nsub):
                dp_ref[rows[ch], pl.ds(h * HEAD, HEAD)] = g_q[ch].astype(bf16)
                dp_ref[rows[ch], pl.ds(w512 + h * HEAD, HEAD)] = g_f[ch].astype(bf16)
                dp_ref[rows[ch], pl.ds(2 * w512 + h * HEAD, HEAD)] = (g_i[ch] + d_v[h][ch]).astype(bf16)
            dlb_ref[0, :, _hsl(h)] = g_lb
            ds_scr[h] = ds[h]
        dgn_ref[...] += dgn

    return pl.pallas_call(
        body, grid=(ns,),
        in_specs=[blk(C_QB), blk(C_FB), blk(C_IB), blk(C_GB),
                  pl.BlockSpec((1, w512), lambda n: (0, 0)), pl.BlockSpec((1, HEAD), lambda n: (0, 0)),
                  pl.BlockSpec((1, NH, HEAD, HEAD), lambda n: (ns - 1 - n, 0, 0, 0)),
                  pl.BlockSpec((r, w512), lambda n: (ns - 1 - n, 1))],
        out_specs=[pl.BlockSpec((r, 4 * w512), lambda n: (ns - 1 - n, 0)),
                   pl.BlockSpec((1, 1, w512), lambda n: (n, 0, 0)),
                   pl.BlockSpec((1, HEAD), lambda n: (0, 0))],
        out_shape=[_sds((t, 4 * w512), bf16), _sds((ns, 1, w512), f32), _sds((1, HEAD), f32)],
        scratch_shapes=[pltpu.VMEM((NH, HEAD, HEAD), f32), pltpu.VMEM((nsub * NH, HEAD, HEAD), f32)],
        compiler_params=_cp(), name=name)(p, p, p, p, lb, gain, states, do)


def _lru_gates(xb, wa, wx, ba, bx, lam):
    xh = xb.astype(bf16)
    r = jax.nn.sigmoid(_dot(xh, wa.astype(bf16), NN) + ba)
    i = jax.nn.sigmoid(_dot(xh, wx.astype(bf16), NN) + bx)
    log_a = -RG_C * r * jax.nn.softplus(-lam)
    a = jnp.exp(log_a)
    t2 = 2.0 * log_a
    series = -t2 * (1.0 + t2 * (0.5 + t2 * (1.0 / 6.0 + t2 * (1.0 / 24.0))))
    om = jnp.where(t2 > -1e-2, series, 1.0 - jnp.exp(t2))
    u = jnp.sqrt(jnp.maximum(om, 0.0)) * (i * xb)
    return a, u


def _lru_gates_fwd(xc, wa, wx, ba, bx, lam, name):
    t = xc.shape[0]
    blk = pl.BlockSpec((TT, LRU_BLOCK), lambda h, i: (i, h))
    wsp = pl.BlockSpec((1, LRU_BLOCK, LRU_BLOCK), lambda h, i: (h, 0, 0))
    vsp = pl.BlockSpec((1, LRU_BLOCK), lambda h, i: (0, h))

    def body(x_ref, wa_ref, wx_ref, ba_ref, bx_ref, lam_ref, a_ref, u_ref):
        a, u = _lru_gates(x_ref[...], wa_ref[0], wx_ref[0], ba_ref[...], bx_ref[...], lam_ref[...])
        a_ref[...] = a
        u_ref[...] = u

    return pl.pallas_call(
        body, grid=(NH, t // TT), in_specs=[blk, wsp, wsp, vsp, vsp, vsp], out_specs=[blk, blk],
        out_shape=[_sds((t, D), f32)] * 2, compiler_params=_cp(), name=name)(xc, wa, wx, ba, bx, lam)


def _lru_gates_bwd(xc, wa, wx, ba, bx, lam, da, du, name):
    t = xc.shape[0]
    blk = pl.BlockSpec((TT, LRU_BLOCK), lambda h, i: (i, h))
    wsp = pl.BlockSpec((1, LRU_BLOCK, LRU_BLOCK), lambda h, i: (h, 0, 0))
    vsp = pl.BlockSpec((1, LRU_BLOCK), lambda h, i: (0, h))

    def body(x_ref, wa_ref, wx_ref, ba_ref, bx_ref, lam_ref, da_ref, du_ref,
             dx_ref, dwa_ref, dwx_ref, dba_ref, dbx_ref, dlam_ref):
        @pl.when(pl.program_id(1) == 0)
        def _():
            for r in (dwa_ref, dwx_ref, dba_ref, dbx_ref, dlam_ref):
                r[...] = jnp.zeros_like(r)

        _, vjp = jax.vjp(_lru_gates, x_ref[...], wa_ref[0], wx_ref[0], ba_ref[...], bx_ref[...], lam_ref[...])
        dx, dwa, dwx, dba, dbx, dlam = vjp((da_ref[...], du_ref[...]))
        dx_ref[...] = dx
        dwa_ref[0] += dwa
        dwx_ref[0] += dwx
        dba_ref[...] += dba
        dbx_ref[...] += dbx
        dlam_ref[...] += dlam

    return pl.pallas_call(
        body, grid=(NH, t // TT), in_specs=[blk, wsp, wsp, vsp, vsp, vsp, blk, blk],
        out_specs=[blk, wsp, wsp, vsp, vsp, vsp],
        out_shape=[_sds((t, D), f32), _sds((NH, LRU_BLOCK, LRU_BLOCK), f32), _sds((NH, LRU_BLOCK, LRU_BLOCK), f32),
                   _sds((1, D), f32), _sds((1, D), f32), _sds((1, D), f32)],
        compiler_params=_cp(), name=name)(xc, wa, wx, ba, bx, lam, da, du)


_SCAN_SHIFTS = (1, 2, 4, 8, 16, 32, 64, 128)
_SCAN_PAD = 128


def _gelu(y):
    return jax.nn.gelu(y, approximate=True)


def _lru_scan_fwd(a, u, p2, name):
    t = a.shape[0]
    tc = 128
    blk = pl.BlockSpec((TT, tc), lambda j, i: (i, j))

    def body(a_ref, u_ref, y_ref, h_ref, hg_ref, a_s, b_s, carry):
        i = pl.program_id(1)

        @pl.when(i == 0)
        def _():
            carry[...] = jnp.zeros_like(carry)
            a_s[0:_SCAN_PAD, :] = jnp.ones((_SCAN_PAD, tc), f32)
            b_s[0:_SCAN_PAD, :] = jnp.zeros((_SCAN_PAD, tc), f32)

        av, bv = a_ref[...], u_ref[...]
        for s in _SCAN_SHIFTS:
            if s < 8:
                a_s[_SCAN_PAD:, :] = av
                b_s[_SCAN_PAD:, :] = bv
                ash = a_s[pl.ds(_SCAN_PAD - s, TT), :]
                bsh = b_s[pl.ds(_SCAN_PAD - s, TT), :]
            else:
                ash = jnp.concatenate([jnp.ones((s, tc), f32), av[:TT - s]], axis=0)
                bsh = jnp.concatenate([jnp.zeros((s, tc), f32), bv[:TT - s]], axis=0)
            bv = bv + av * bsh
            av = av * ash
        h = bv + av * carry[7:8, :]
        h_ref[...] = h
        hg_ref[...] = (h * _gelu(y_ref[...])).astype(bf16)
        carry[...] = h[TT - 8:, :]

    return pl.pallas_call(
        body, grid=(D // tc, t // TT), in_specs=[blk, blk, blk], out_specs=[blk, blk],
        out_shape=[_sds((t, D), f32), _sds((t, D), bf16)],
        scratch_shapes=[pltpu.VMEM((_SCAN_PAD + TT, tc), f32), pltpu.VMEM((_SCAN_PAD + TT, tc), f32),
                        pltpu.VMEM((8, tc), f32)],
        compiler_params=_cp(), name=name)(a, u, p2)


def _lru_scan_bwd(a, h, p2, dhg, name):
    t = a.shape[0]
    tc = 128
    nt = t // TT
    hb = TT // 8
    rblk = pl.BlockSpec((TT, tc), lambda j, i: (nt - 1 - i, j))

    def body(a_ref, an_ref, h_ref, hp_ref, y_ref, dhg_ref, du_ref, da_ref, dy_ref, a_s, b_s, ap, hp, carry):
        i = pl.program_id(1)

        @pl.when(i == 0)
        def _():
            carry[...] = jnp.zeros_like(carry)
            a_s[TT:, :] = jnp.ones((_SCAN_PAD, tc), f32)
            b_s[TT:, :] = jnp.zeros((_SCAN_PAD, tc), f32)

        ap[0:TT, :] = a_ref[...]
        ap[TT:, :] = jnp.where(i == 0, 0.0, an_ref[...])
        hp[0:8, :] = jnp.where(i == nt - 1, 0.0, hp_ref[...])
        hp[8:, :] = h_ref[...]
        y = y_ref[...]
        gate, gvjp = jax.vjp(_gelu, y)
        dhg_v = dhg_ref[...]
        dy_ref[...] = gvjp(dhg_v * h_ref[...])[0]
        av = ap[pl.ds(1, TT), :]
        bv = dhg_v * gate
        for s in _SCAN_SHIFTS:
            if s < 8:
                a_s[0:TT, :] = av
                b_s[0:TT, :] = bv
                ash = a_s[pl.ds(s, TT), :]
                bsh = b_s[pl.ds(s, TT), :]
            else:
                ash = jnp.concatenate([av[s:], jnp.ones((s, tc), f32)], axis=0)
                bsh = jnp.concatenate([bv[s:], jnp.zeros((s, tc), f32)], axis=0)
            bv = bv + av * bsh
            av = av * ash
        g = bv + av * carry[0:1, :]
        du_ref[...] = g
        da_ref[...] = g * hp[pl.ds(7, TT), :]
        carry[...] = g[0:8, :]

    in_specs = [
        rblk,
        pl.BlockSpec((8, tc), lambda j, i: (jnp.minimum((nt - i) * hb, t // 8 - 1), j)),
        rblk,
        pl.BlockSpec((8, tc), lambda j, i: (jnp.maximum((nt - 1 - i) * hb - 1, 0), j)),
        rblk, rblk,
    ]
    return pl.pallas_call(
        body, grid=(D // tc, nt), in_specs=in_specs, out_specs=[rblk, rblk, rblk],
        out_shape=[_sds((t, D), f32)] * 3,
        scratch_shapes=[pltpu.VMEM((TT + _SCAN_PAD, tc), f32), pltpu.VMEM((TT + _SCAN_PAD, tc), f32),
                        pltpu.VMEM((TT + 8, tc), f32), pltpu.VMEM((TT + 8, tc), f32), pltpu.VMEM((8, tc), f32)],
        compiler_params=_cp(), name=name)(a, a, h, h, p2, dhg)


def _ffn_act_bwd(gc, up, dact, name, tc=256):
    t = gc.shape[0]
    blk = pl.BlockSpec((TT, tc), lambda i, j: (i, j))
    vblk = pl.BlockSpec((TT, tc), lambda i, j: (i, j + D_FF // tc))

    def body(gc_ref, v_ref, da_ref, dgc_ref, dv_ref):
        _, vjp = jax.vjp(lambda g, v: _silu(g) * v, gc_ref[...].astype(f32), v_ref[...])
        dg, dv = vjp(da_ref[...].astype(f32))
        dgc_ref[...] = dg.astype(bf16)
        dv_ref[...] = dv.astype(bf16)

    return pl.pallas_call(
        body, grid=(t // TT, D_FF // tc), in_specs=[blk, vblk, blk], out_specs=[blk, vblk],
        out_shape=[_sds((t, D_FF), bf16), _sds((t, 2 * D_FF), bf16)], compiler_params=_cp(), name=name)(gc, up, dact)


def _lower_bounds_fwd(w):
    def body(w_ref, o0_ref, o1_ref):
        wv = w_ref[...]
        o0, o1 = _lb_rows(wv[0:1, :], wv[1:2, :])
        o0_ref[...] = o0
        o1_ref[...] = o1

    return pl.pallas_call(body, out_shape=[_sds((1, 512), f32)] * 2, name="lower_bounds_fwd")(w)


def _lb_rows(w0, w1):
    m = jnp.maximum(w0, w1)
    e0, e1 = jnp.exp(w0 - m), jnp.exp(w1 - m)
    s = e0 + e1
    p0, p1 = e0 / s, e1 / s
    return p0 - p0, (p0 + p1) - p0


def _lower_bounds_bwd(w, d0, d1):
    def body(w_ref, d0_ref, d1_ref, g0_ref, g1_ref):
        wv = w_ref[...]
        _, vjp = jax.vjp(_lb_rows, wv[0:1, :], wv[1:2, :])
        g0, g1 = vjp((d0_ref[...], d1_ref[...]))
        g0_ref[...] = g0
        g1_ref[...] = g1

    return pl.pallas_call(body, out_shape=[_sds((1, 512), f32)] * 2, name="lower_bounds_bwd")(w, d0, d1)


def _local_step(x, target, wt, pre_layer=None, post_grads=None):
    depth = 4
    res = []
    lb0, lb1 = _lower_bounds_fwd(wt["hgrn_lower_bounds"])
    lbs = [lb0, lb1]
    for layer in range(depth):
        j = layer // 2
        sv = {"x_in": x}
        deps = pre_layer(layer, "mix", x) if pre_layer else ()
        if layer == 0:
            h1 = _rmsnorm_fwd(x, wt["norm_mix"][layer], "rms_fwd", deps)
        sv["h1"] = h1
        if layer % 2 == 0:
            p = _mm_auto(h1, wt["ab_w_in"][j], "nn", f32, "mm_ab_in")
            conv = _conv_fwd(p, 0, 1536, wt["gdn_conv_w"][j], jnp.zeros((1, 1536), f32), "gdn_conv_fwd", tc=768)
            o_a, s_a = _gdn2_fwd(conv, p, wt["alog_v"][j], wt["dtb_v"][j], wt["gdn_norm"][j], "gdn_fwd")
            o_b, s_b = _hgrn_fwd(p, lbs[j], wt["hgrn_norm"][j], "hgrn_fwd")
            o = jnp.concatenate([o_a, o_b], axis=1)
            x, h2 = _mm_res_norm(o, wt["ab_w_out"][j], x, wt["norm_ffn"][layer], "mm_ab_out")
            sv.update(p=p, conv=conv, s_a=s_a, s_b=s_b, o=o)
        else:
            p2 = _mm_nn_slots(h1, wt["c_w_in"][j], f32, "mm_c_in", deps)
            xc = _conv_fwd(p2, 1, D, wt["c_conv_w"][j], wt["c_conv_b"][j], "lru_conv_fwd", tc=D)
            a, u = _lru_gates_fwd(xc, wt["c_gate_a_w"][j], wt["c_gate_x_w"][j], wt["c_gate_a_b"][j],
                                  wt["c_gate_x_b"][j], wt["c_lambda"][j], "lru_gates_fwd")
            h, hg = _lru_scan_fwd(a, u, p2, "lru_scan_fwd")
            x, h2 = _mm_res_norm(hg, wt["c_w_out"][j], x, wt["norm_ffn"][layer], "mm_c_out")
            sv.update(p2=p2, xc=xc, a=a, h=h, hg=hg)
        sv["x_mid"] = x
        deps = pre_layer(layer, "ffn", x) if pre_layer else ()
        up = _mm_nn_slots(h2, wt["ffn_w_up"][layer], bf16, "mm_up", deps)
        gc, act = _conv_fwd(up, 0, D_FF, wt["ffn_conv_w"][layer], wt["ffn_conv_b"][layer], "ffn_conv_fwd",
                            tc=D_FF // 2, val=up, val_col0=2, out_dtype=bf16)
        if layer + 1 < depth:
            x, h1 = _mm_res_norm(act, wt["ffn_w_down"][layer], x, wt["norm_mix"][layer + 1], "mm_down")
        else:
            x = _mm_res_norm(act, wt["ffn_w_down"][layer], x, None, "mm_down_last")
        sv.update(h2=h2, up=up, gc=gc, act=act)
        res.append(sv)

    loss, dx, dxb, d_norm_final = _final_loss(x, wt["norm_final"], target, "final_loss")

    g = {k: [None] * len(v) for k, v in wt.items() if isinstance(v, list)}
    g["norm_final"] = d_norm_final
    d_lbs = [None, None]
    for layer in reversed(range(depth)):
        j = layer // 2
        sv = res[layer]
        dact = _mm_auto(dxb, wt["ffn_w_down"][layer], "nt", bf16, "mm_down_dx")
        g["ffn_w_down"][layer] = _mm_auto(sv["act"], dxb, "tn", bf16, "mm_down_dw")
        dgc, dup = _ffn_act_bwd(sv["gc"], sv["up"], dact, "ffn_act_bwd", tc=D_FF // 2)
        dup, dcw, dcb = _conv_bwd(dgc, sv["up"], 0, wt["ffn_conv_w"][layer], "ffn_conv_bwd", tc=D_FF // 2, into=dup)
        g["ffn_conv_w"][layer] = dcw[:3]
        g["ffn_conv_b"][layer] = dcb
        g["ffn_w_up"][layer] = _mm_tn_slots(sv["h2"], dup, "mm_up_dw")
        deps = post_grads(layer, "ffn", g) if post_grads else ()
        dx, dxb, g["norm_ffn"][layer] = _mm_nt_rmsbwd(dup, wt["ffn_w_up"][layer], sv["x_mid"], wt["norm_ffn"][layer], dx,
                                                      "mm_up_dx", deps)
        if layer % 2 == 0:
            do = _mm_auto(dxb, wt["ab_w_out"][j], "nt", f32, "mm_ab_out_dx")
            g["ab_w_out"][j] = _mm_auto(sv["o"], dxb, "tn", bf16, "mm_ab_out_dw")
            dconv, dz, dba, dal, ddt, dgn = _gdn2_bwd(
                sv["conv"], sv["p"], wt["alog_v"][j], wt["dtb_v"][j], wt["gdn_norm"][j], sv["s_a"], do, "gdn_bwd")
            g["alog_v"][j], g["dtb_v"][j], g["gdn_norm"][j] = dal, ddt, dgn
            dqkv, dcw, _ = _conv_bwd(dconv, sv["p"], 0, wt["gdn_conv_w"][j], "gdn_conv_bwd", tc=768)
            g["gdn_conv_w"][j] = dcw[:4]
            dqfig, dlb, dhn = _hgrn_bwd(sv["p"], lbs[j], wt["hgrn_norm"][j], sv["s_b"], do, "hgrn_bwd")
            g["hgrn_norm"][j] = dhn
            d_lbs[j] = jnp.sum(dlb, axis=0)
            dp = jnp.concatenate([dqkv, dz.astype(bf16), dqfig, dba.astype(bf16)], axis=1)
            g["ab_w_in"][j] = _mm_auto(sv["h1"], dp, "tn", bf16, "mm_ab_in_dw")
            dpre, wpre = dp, wt["ab_w_in"][j]
        else:
            dhg = _mm_auto(dxb, wt["c_w_out"][j], "nt", f32, "mm_c_out_dx")
            g["c_w_out"][j] = _mm_auto(sv["hg"], dxb, "tn", bf16, "mm_c_out_dw")
            du, da, dy = _lru_scan_bwd(sv["a"], sv["h"], sv["p2"], dhg, "lru_scan_bwd")
            dxc, dwa, dwx, dba_, dbx_, dlam = _lru_gates_bwd(
                sv["xc"], wt["c_gate_a_w"][j], wt["c_gate_x_w"][j], wt["c_gate_a_b"][j], wt["c_gate_x_b"][j],
                wt["c_lambda"][j], da, du, "lru_gates_bwd")
            g["c_gate_a_w"][j], g["c_gate_x_w"][j] = dwa, dwx
            g["c_gate_a_b"][j], g["c_gate_x_b"][j], g["c_lambda"][j] = dba_, dbx_, dlam
            dxbr, dcw, dcb = _conv_bwd(dxc, sv["p2"], 1, wt["c_conv_w"][j], "lru_conv_bwd", tc=D)
            g["c_conv_w"][j] = dcw[:4]
            g["c_conv_b"][j] = dcb
            dp2 = jnp.concatenate([dy.astype(bf16), dxbr], axis=1)
            g["c_w_in"][j] = _mm_tn_slots(sv["h1"], dp2, "mm_c_in_dw")
            dpre, wpre = dp2, wt["c_w_in"][j]
        deps = post_grads(layer, "mix", g) if post_grads else ()
        dx, dxb, g["norm_mix"][layer] = _mm_nt_rmsbwd(dpre, wpre, sv["x_in"], wt["norm_mix"][layer], dx, "mm_mix_in_dx", deps)
    g0, g1 = _lower_bounds_bwd(wt["hgrn_lower_bounds"], d_lbs[0], d_lbs[1])
    g["hgrn_lower_bounds"] = jnp.concatenate([g0, g1], axis=0)
    return loss, dx, g


def _ab_in_to_compute(w):
    return jnp.concatenate([w[:, :2048], w[:, 2056:4104], w[:, 2048:2056], jnp.zeros((D, 120), w.dtype)], axis=1)


def _ab_in_from_compute(g):
    return jnp.concatenate([g[:, :2048], g[:, 4096:4104], g[:, 2048:4096]], axis=1)


def _lane_vec(v4):
    return jnp.zeros((1, HEAD), f32).at[0, NH:2 * NH].set(v4)


def _layout_weights(fw):
    wt = {}
    wt["norm_mix"] = [fw["norm_mix"][l][None] for l in range(4)]
    wt["norm_ffn"] = [fw["norm_ffn"][l][None] for l in range(4)]
    wt["norm_final"] = fw["norm_final"][None]
    wt["gdn_conv_w"] = [fw["gdn_conv_w"][j] for j in range(2)]
    wt["alog_v"] = [_lane_vec(fw["gdn_a_log"][j]) for j in range(2)]
    wt["dtb_v"] = [_lane_vec(fw["gdn_dt_bias"][j]) for j in range(2)]
    wt["gdn_norm"] = [fw["gdn_norm"][j][None] for j in range(2)]
    wt["hgrn_lower_bounds"] = fw["hgrn_lower_bounds"]
    wt["hgrn_norm"] = [fw["hgrn_norm"][j][None] for j in range(2)]
    wt["c_conv_w"] = [fw["c_conv_w"][j] for j in range(2)]
    for k in ("c_conv_b", "c_gate_a_b", "c_gate_x_b", "c_lambda"):
        wt[k] = [fw[k][j][None] for j in range(2)]
    wt["ffn_conv_w"] = [fw["ffn_conv_w"][l] for l in range(4)]
    wt["ffn_conv_b"] = [fw["ffn_conv_b"][l][None] for l in range(4)]
    if "ab_w_in" in fw:
        wt["ab_w_in"] = [_ab_in_to_compute(fw["ab_w_in"][j].astype(bf16)) for j in range(2)]
        for k in ("ab_w_out", "c_w_out"):
            wt[k] = [fw[k][j].astype(bf16) for j in range(2)]
        wt["c_w_in"] = [_to_slots(fw["c_w_in"][j].astype(bf16)) for j in range(2)]
        for k in ("c_gate_a_w", "c_gate_x_w"):
            wt[k] = [fw[k][j].astype(f32) for j in range(2)]
        wt["ffn_w_up"] = [_to_slots(fw["ffn_w_up"][l].astype(bf16)) for l in range(4)]
        wt["ffn_w_down"] = [fw["ffn_w_down"][l].astype(bf16) for l in range(4)]
    return wt


SLOT_MAJOR = ("c_w_in", "ffn_w_up")
NATIVE_PERM = {"ab_w_in": (2, 0, 1), "ffn_conv_w": (1, 0, 2)}


def _to_slots(wfull):
    k, n = wfull.shape
    return wfull.reshape(k, N_SLOTS, n // N_SLOTS).transpose(1, 0, 2)


def _layer_full(name, slots):
    kind = BIG[name]
    if name in SLOT_MAJOR:
        return slots
    if kind == "col":
        return _ab_in_to_compute(slots.transpose(1, 0, 2).reshape(slots.shape[1], -1))
    if kind == "row":
        return slots.reshape(-1, slots.shape[2])
    return slots.reshape(4, NH, LRU_BLOCK // 4, LRU_BLOCK).transpose(1, 0, 2, 3).reshape(NH, LRU_BLOCK, LRU_BLOCK).astype(f32)


def _layer_slots(name, g):
    kind = BIG[name]
    if name in SLOT_MAJOR:
        return g
    if kind == "col":
        g = _ab_in_from_compute(g)
        r, cdim = g.shape
        return g.reshape(r, 4, cdim // 4).transpose(1, 0, 2).astype(bf16)
    if kind == "row":
        r, cdim = g.shape
        return g.reshape(4, r // 4, cdim).astype(bf16)
    return g.reshape(NH, 4, LRU_BLOCK // 4, LRU_BLOCK).transpose(1, 0, 2, 3).reshape(4, LRU_BLOCK, LRU_BLOCK).astype(bf16)


def _unlayout_grads(g):
    out = {}
    for k in ("norm_mix", "norm_ffn", "gdn_norm", "hgrn_norm", "c_conv_b", "c_gate_a_b", "c_gate_x_b", "c_lambda",
              "ffn_conv_b"):
        out[k] = jnp.concatenate(g[k], axis=0)
    out["norm_final"] = g["norm_final"][0]
    out["ab_w_in"] = jnp.stack([_ab_in_from_compute(t) for t in g["ab_w_in"]])
    out["gdn_a_log"] = jnp.stack([t[0, NH:2 * NH] for t in g["alog_v"]])
    out["gdn_dt_bias"] = jnp.stack([t[0, NH:2 * NH] for t in g["dtb_v"]])
    out["hgrn_lower_bounds"] = g["hgrn_lower_bounds"]
    for k in ("gdn_conv_w", "ab_w_out", "c_conv_w", "c_gate_a_w", "c_gate_x_w", "c_w_out", "ffn_conv_w", "ffn_w_down"):
        out[k] = jnp.stack(g[k])
    for k in SLOT_MAJOR:
        out[k] = jnp.stack([t.transpose(1, 0, 2).reshape(t.shape[1], -1) for t in g[k]])
    return out


MESH = pl.DeviceIdType.MESH
ANY = pl.BlockSpec(memory_space=pl.ANY)
CHIP_RELATIONS = ((1, 0), (0, 1), (1, 1))
N_CHIPS = 4


def _coords():
    return lax.axis_index("x"), lax.axis_index("y"), lax.axis_index("c")


def _flip(v, f):
    return 1 - v if f else v


def _half_rows(c, a, align):
    return pl.ds(pl.multiple_of(c * (a // 2), align), a // 2)


def _all_gather_chips(shards, name):
    n = len(shards)
    shapes = [s.shape for s in shards]

    def body(*refs):
        ins, outs = refs[:n], refs[n:2 * n]
        send_sems, recv_sems = refs[2 * n:]
        x, y, c = _coords()
        me = 2 * x + y
        sibling = (x, y, 1 - c)
        started = []
        for p in range(n):
            cp = pltpu.make_async_remote_copy(
                src_ref=ins[p], dst_ref=outs[p].at[me],
                send_sem=send_sems.at[p, 6], recv_sem=recv_sems.at[p, 6],
                device_id=sibling, device_id_type=MESH)
            cp.start()
            started.append(cp)
        for p in range(n):
            mine = _half_rows(c, shapes[p][0], 16)
            for r, (fx, fy) in enumerate(CHIP_RELATIONS):
                cp = pltpu.make_async_remote_copy(
                    src_ref=ins[p].at[mine], dst_ref=outs[p].at[me, mine],
                    send_sem=send_sems.at[p, r], recv_sem=recv_sems.at[p, r],
                    device_id=(_flip(x, fx), _flip(y, fy), c), device_id_type=MESH)
                cp.start()
                started.append(cp)
        for r, (fx, fy) in enumerate(CHIP_RELATIONS):
            k = 2 * _flip(x, fx) + _flip(y, fy)
            for p in range(n):
                mine = _half_rows(c, shapes[p][0], 16)
                pltpu.make_async_remote_copy(
                    src_ref=ins[p].at[mine], dst_ref=outs[p].at[k, mine],
                    send_sem=send_sems.at[p, r], recv_sem=recv_sems.at[p, r],
                    device_id=(_flip(x, fx), _flip(y, fy), c), device_id_type=MESH).wait_recv()
                fwd = pltpu.make_async_remote_copy(
                    src_ref=outs[p].at[k, mine], dst_ref=outs[p].at[k, mine],
                    send_sem=send_sems.at[p, 3 + r], recv_sem=recv_sems.at[p, 3 + r],
                    device_id=sibling, device_id_type=MESH)
                fwd.start()
                started.append(fwd)
        for r, (fx, fy) in enumerate(CHIP_RELATIONS):
            k = 2 * _flip(x, fx) + _flip(y, fy)
            for p in range(n):
                theirs = _half_rows(1 - c, shapes[p][0], 16)
                pltpu.make_async_remote_copy(
                    src_ref=outs[p].at[k, theirs], dst_ref=outs[p].at[k, theirs],
                    send_sem=send_sems.at[p, 3 + r], recv_sem=recv_sems.at[p, 3 + r],
                    device_id=sibling, device_id_type=MESH).wait_recv()
        for p in range(n):
            pltpu.make_async_remote_copy(
                src_ref=ins[p], dst_ref=outs[p].at[me],
                send_sem=send_sems.at[p, 6], recv_sem=recv_sems.at[p, 6],
                device_id=sibling, device_id_type=MESH).wait_recv()
        for cp in started:
            cp.wait_send()

    return pl.pallas_call(
        body, in_specs=[ANY] * n, out_specs=[ANY] * n,
        out_shape=[_sds((N_CHIPS,) + s.shape, s.dtype) for s in shards],
        scratch_shapes=[pltpu.SemaphoreType.DMA((n, 7)), pltpu.SemaphoreType.DMA((n, 7))],
        name=name)(*shards)


N_DEV = 8


ROWS_EW = 128


def _adamw_math(w, g, m, v):
    m = ADAM_B1 * m + (1.0 - ADAM_B1) * g
    v = ADAM_B2 * v + (1.0 - ADAM_B2) * (g * g)
    m_hat = m / (1.0 - ADAM_B1 ** ADAM_STEP)
    v_hat = v / (1.0 - ADAM_B2 ** ADAM_STEP)
    delta = -ADAM_LR * (m_hat / (jnp.sqrt(v_hat) + ADAM_EPS) + ADAM_WD * w)
    return delta, m, v


def _adamw_big(w, g, m, v, name):
    nl, r, b = w.shape
    rt = _tile(r, (256, 352, ROWS_EW, 64))
    per = r // rt

    def body(w_ref, g_ref, m_ref, v_ref, go_ref, d_ref, mo_ref, vo_ref):
        gv = g_ref[...]
        d, mn, vn = _adamw_math(w_ref[...], gv, m_ref[...], v_ref[...])
        go_ref[...] = gv
        d_ref[...] = d
        mo_ref[...] = mn
        vo_ref[...] = vn

    blk = pl.BlockSpec((None, rt, b), lambda l, i: (l, i, 0))
    gblk = pl.BlockSpec((rt, b), lambda l, i: (l * per + i, 0))
    return pl.pallas_call(body, grid=(nl, per), in_specs=[blk, gblk, blk, blk], out_specs=[blk] * 4,
                          out_shape=[_sds((nl, r, b), f32)] * 4, compiler_params=_cp(), name=name)(w, g, m, v)


def _adamw_lead(w, g, m, v, name):
    n, r, b = w.shape
    tn = _tile(n, (64, 54, 32, 16, 8, 1))

    def body(w_ref, g_ref, m_ref, v_ref, go_ref, d_ref, mo_ref, vo_ref):
        gv = g_ref[...]
        d, mn, vn = _adamw_math(w_ref[...], gv, m_ref[...], v_ref[...])
        go_ref[...] = gv
        d_ref[...] = d
        mo_ref[...] = mn
        vo_ref[...] = vn

    blk = pl.BlockSpec((tn, r, b), lambda i: (i, 0, 0))
    return pl.pallas_call(body, grid=(n // tn,), in_specs=[blk] * 4, out_specs=[blk] * 4,
                          out_shape=[_sds((n, r, b), f32)] * 4, compiler_params=_cp(), name=name)(w, g, m, v)


def _adamw_small(ws, gs, ms, vs, name):
    n = len(ws)

    def body(*refs):
        w_r, g_r, m_r, v_r = refs[:n], refs[n:2 * n], refs[2 * n:3 * n], refs[3 * n:4 * n]
        go_r, d_r, mo_r, vo_r = refs[4 * n:5 * n], refs[5 * n:6 * n], refs[6 * n:7 * n], refs[7 * n:8 * n]
        for p in range(n):
            gv = g_r[p][...]
            d, mn, vn = _adamw_math(w_r[p][...], gv, m_r[p][...], v_r[p][...])
            go_r[p][...] = gv
            d_r[p][...] = d
            mo_r[p][...] = mn
            vo_r[p][...] = vn

    vm = pl.BlockSpec(memory_space=pltpu.VMEM)
    shp = [_sds(w.shape, f32) for w in ws]
    res = pl.pallas_call(body, in_specs=[vm] * (4 * n), out_specs=[vm] * (4 * n), out_shape=shp * 4,
                         name=name)(*ws, *gs, *ms, *vs)
    return res[:n], res[n:2 * n], res[2 * n:3 * n], res[3 * n:]


HBM = pl.BlockSpec(memory_space=pltpu.HBM)
SEM = pl.BlockSpec(memory_space=pltpu.SEMAPHORE)
EFFECT = pltpu.SideEffectType.DATAFLOW_SIDE_EFFECTING
N_REL = 8


def _rel(r):
    return (r >> 2) & 1, (r >> 1) & 1, r & 1


def _gather_copies(ins, lands, send_sems, recv_sems, shapes):
    x, y, c = _coords()
    me = 2 * x + y
    sends, recvs = [], []
    for p in range(len(ins)):
        for r in range(1, N_REL):
            fx, fy, fc = _rel(r)
            peer = (_flip(x, fx), _flip(y, fy), _flip(c, fc))
            if fx == 0 and fy == 0:
                src, dst, got = ins[p], lands[p].at[me], lands[p].at[me]
            else:
                mine = _half_rows(c, shapes[p][0], 16)
                theirs = _half_rows(_flip(c, fc), shapes[p][0], 16)
                src, dst = ins[p].at[mine], lands[p].at[me, mine]
                got = lands[p].at[2 * peer[0] + peer[1], theirs]
            sems = dict(send_sem=send_sems.at[p * N_REL + r], recv_sem=recv_sems.at[p * N_REL + r], device_id=peer,
                        device_id_type=MESH)
            sends.append(pltpu.make_async_remote_copy(src_ref=src, dst_ref=dst, **sems))
            recvs.append(pltpu.make_async_remote_copy(src_ref=src, dst_ref=got, **sems))
    return sends, recvs


def _share_copies(ins, lands, send_sems, recv_sems, shapes):
    x, y, c = _coords()
    me = 4 * x + 2 * y + c
    sends, recvs = [], []
    for p in range(len(ins)):
        for r in range(1, N_REL):
            fx, fy, fc = _rel(r)
            peer = (_flip(x, fx), _flip(y, fy), _flip(c, fc))
            sems = dict(send_sem=send_sems.at[p * N_REL + r], recv_sem=recv_sems.at[p * N_REL + r], device_id=peer,
                        device_id_type=MESH)
            sends.append(pltpu.make_async_remote_copy(src_ref=ins[p], dst_ref=lands[p].at[me], **sems))
            recvs.append(pltpu.make_async_remote_copy(src_ref=ins[p], dst_ref=lands[p].at[4 * peer[0] + 2 * peer[1] + peer[2]], **sems))
    return sends, recvs


def _sum_shared(own, land, me_arr, name):
    rows = own.shape[0]

    def body(me_ref, own_ref, land_ref, o_ref):
        me = me_ref[0]
        o_ref[...] = jnp.zeros_like(o_ref)
        for d in range(N_DEV):
            @pl.when(me == d)
            def _():
                o_ref[...] += own_ref[...]

            @pl.when(me != d)
            def _():
                o_ref[...] += land_ref[d]

    grid_spec = pltpu.PrefetchScalarGridSpec(
        num_scalar_prefetch=1, grid=(1,),
        in_specs=[pl.BlockSpec((rows, 128), lambda i, me_ref: (0, 0)), pl.BlockSpec((N_DEV, rows, 128), lambda i, me_ref: (0, 0, 0))],
        out_specs=pl.BlockSpec((rows, 128), lambda i, me_ref: (0, 0)))
    return pl.pallas_call(body, grid_spec=grid_spec, out_shape=_sds((rows, 128), f32), compiler_params=_cp(), name=name)(
        me_arr, own, land)


def _scatter_copies(ins, lands, send_sems, recv_sems, shapes):
    x, y, c = _coords()
    sends, recvs = [], []
    for p in range(len(ins)):
        for r in range(1, N_REL):
            fx, fy, fc = _rel(r)
            peer = (_flip(x, fx), _flip(y, fy), _flip(c, fc))
            theirs = _half_rows(peer[2], shapes[p][1], 16)
            sems = dict(send_sem=send_sems.at[p * N_REL + r], recv_sem=recv_sems.at[p * N_REL + r], device_id=peer,
                        device_id_type=MESH)
            cp = pltpu.make_async_remote_copy(src_ref=ins[p].at[2 * peer[0] + peer[1], theirs], dst_ref=lands[p].at[r], **sems)
            sends.append(cp)
            recvs.append(cp)
    return sends, recvs


def _split_start(copies_fn, ins, land_shapes, name, after=()):
    n = len(ins)
    shapes = [a.shape for a in ins]

    def body(*refs):
        in_refs, land_refs = refs[:n], refs[n:2 * n]
        send_sems, recv_sems = refs[2 * n + len(after)], refs[2 * n + len(after) + 1]
        token = refs[-1]
        sends, _ = copies_fn(in_refs, land_refs, send_sems, recv_sems, shapes)
        for cp in sends:
            cp.start()
        token[...] = jnp.zeros_like(token)

    lands = [lax.empty(s.shape, s.dtype) for s in land_shapes]
    na = len(after)
    res = pl.pallas_call(
        body, name=name,
        out_shape=(pltpu.SemaphoreType.DMA((n * N_REL,)), pltpu.SemaphoreType.DMA((n * N_REL,)))
        + tuple(pltpu.HBM(a.shape, a.dtype) for a in ins) + tuple(pltpu.HBM(s.shape, s.dtype) for s in land_shapes)
        + (_sds((8, 128), f32),),
        in_specs=[HBM] * (2 * n) + [pl.BlockSpec(memory_space=pl.ANY)] * na,
        out_specs=(SEM, SEM) + (HBM,) * (2 * n) + (pl.BlockSpec(memory_space=pltpu.VMEM),),
        input_output_aliases={i: 2 + i for i in range(2 * n)},
        compiler_params=pltpu.CompilerParams(has_side_effects=EFFECT),
    )(*[pltpu.with_memory_space_constraint(a, pltpu.HBM) for a in ins],
      *[pltpu.with_memory_space_constraint(a, pltpu.HBM) for a in lands], *after)
    return dict(sems=res[:2], ins=res[2:2 + n], lands=res[2 + n:2 + 2 * n], token=res[-1], shapes=shapes)


def _split_wait(copies_fn, started, after, name):
    n = len(started["ins"])
    shapes = started["shapes"]
    na = len(after)

    def body(*refs):
        in_refs, land_refs = refs[:n], refs[n:2 * n]
        send_sems, recv_sems = refs[2 * n], refs[2 * n + 1]
        sends, recvs = copies_fn(in_refs, land_refs, send_sems, recv_sems, shapes)
        for cp in sends:
            cp.wait_send()
        for cp in recvs:
            cp.wait_recv()

    arrs = list(started["ins"]) + list(started["lands"])
    res = pl.pallas_call(
        body, name=name,
        out_shape=tuple(pltpu.HBM(a.shape, a.dtype) for a in arrs),
        in_specs=[HBM] * (2 * n) + [SEM, SEM] + [pl.BlockSpec(memory_space=pl.ANY)] * na,
        out_specs=(HBM,) * (2 * n), input_output_aliases={i: i for i in range(2 * n)},
        compiler_params=pltpu.CompilerParams(has_side_effects=EFFECT),
    )(*arrs, *started["sems"], *after)
    return res[:n], res[n:]


def _sum_pieces(gb, land, ids, f_prev, blk, nblk, name):
    _, a, b = gb.shape
    rows = _tile(a // 2, (512, 352, 256, ROWS_EW, 64, 32, 16))
    nrt = (a // 2) // rows

    def body(ids_ref, g_ref, l_ref, *rest):
        o_ref = rest[-1]
        acc = g_ref[0].astype(f32)
        for r in range(1, N_REL):
            acc = acc + l_ref[r].astype(f32)
        o_ref[...] = acc

    in_specs = [pl.BlockSpec((1, rows, b), lambda i, ids_ref: (ids_ref[0], ids_ref[1] * nrt + i, 0)),
                pl.BlockSpec((N_REL, rows, b), lambda i, ids_ref: (0, i, 0))]
    args = [ids, gb, land]
    aliases = {}
    if f_prev is not None:
        in_specs.append(pl.BlockSpec(memory_space=pl.ANY))
        args.append(f_prev)
        aliases = {3: 0}
    grid_spec = pltpu.PrefetchScalarGridSpec(
        num_scalar_prefetch=1, grid=(nrt,), in_specs=in_specs,
        out_specs=pl.BlockSpec((rows, b), lambda i, ids_ref: ((2 * blk + ids_ref[1]) * nrt + i, 0)))
    return pl.pallas_call(body, grid_spec=grid_spec, out_shape=_sds((nblk * a, b), f32),
                          input_output_aliases=aliases, compiler_params=_cp(), name=name)(*args)


def _sibling_fill_blocks(fs, nblks, name):
    n = len(fs)
    shapes = [f.shape for f in fs]

    def body(*refs):
        ins, outs = refs[:n], refs[n:2 * n]
        send_sems, recv_sems = refs[2 * n:]
        x, y, c = _coords()
        cps, waits = [], []
        k = 0
        for p in range(n):
            a = shapes[p][0] // nblks[p]
            for bi in range(nblks[p]):
                mine = pl.ds(pl.multiple_of(bi * a + c * (a // 2), 8), a // 2)
                theirs = pl.ds(pl.multiple_of(bi * a + (1 - c) * (a // 2), 8), a // 2)
                sems = dict(send_sem=send_sems.at[k], recv_sem=recv_sems.at[k], device_id=(x, y, 1 - c), device_id_type=MESH)
                cp = pltpu.make_async_remote_copy(src_ref=ins[p].at[mine], dst_ref=outs[p].at[mine], **sems)
                cp.start()
                cps.append(cp)
                waits.append(pltpu.make_async_remote_copy(src_ref=ins[p].at[theirs], dst_ref=outs[p].at[theirs], **sems))
                k += 1
        for wt_ in waits:
            wt_.wait_recv()
        for cp in cps:
            cp.wait_send()

    total = sum(nblks)
    return pl.pallas_call(
        body, in_specs=[ANY] * n, out_specs=[ANY] * n,
        out_shape=[_sds(f.shape, f.dtype) for f in fs],
        input_output_aliases={p: p for p in range(n)},
        scratch_shapes=[pltpu.SemaphoreType.DMA((total,)), pltpu.SemaphoreType.DMA((total,))],
        name=name)(*fs)


WEIGHTS = ["norm_mix", "norm_ffn", "norm_final", "ab_w_in", "gdn_conv_w", "gdn_a_log", "gdn_dt_bias", "gdn_norm",
           "hgrn_lower_bounds", "hgrn_norm", "ab_w_out", "c_w_in", "c_conv_w", "c_conv_b", "c_gate_a_w", "c_gate_a_b",
           "c_gate_x_w", "c_gate_x_b", "c_lambda", "c_w_out", "ffn_w_up", "ffn_conv_w", "ffn_conv_b", "ffn_w_down"]
BIG = {"ab_w_in": "col", "ab_w_out": "row", "c_w_in": "col", "c_gate_a_w": "gate", "c_gate_x_w": "gate",
       "c_w_out": "row", "ffn_w_up": "col", "ffn_w_down": "row"}
SMALL_SHARDED = ["gdn_conv_w", "c_conv_w", "c_conv_b", "c_gate_a_b", "c_gate_x_b", "c_lambda", "ffn_conv_w"]
SMALL = [n for n in WEIGHTS if n not in BIG]
FULL_SHAPES = {
    "norm_mix": (4, 1024), "norm_ffn": (4, 1024), "norm_final": (1024,), "ab_w_in": (2, 1024, 4104),
    "gdn_conv_w": (2, 4, 1536), "gdn_a_log": (2, 4), "gdn_dt_bias": (2, 4), "gdn_norm": (2, 128),
    "hgrn_lower_bounds": (2, 512), "hgrn_norm": (2, 128), "ab_w_out": (2, 1024, 1024), "c_w_in": (2, 1024, 2048),
    "c_conv_w": (2, 4, 1024), "c_conv_b": (2, 1024), "c_gate_a_w": (2, 4, 256, 256), "c_gate_a_b": (2, 1024),
    "c_gate_x_w": (2, 4, 256, 256), "c_gate_x_b": (2, 1024), "c_lambda": (2, 1024), "c_w_out": (2, 1024, 1024),
    "ffn_w_up": (4, 1024, 5632), "ffn_conv_w": (4, 3, 2816), "ffn_conv_b": (4, 2816), "ffn_w_down": (4, 2816, 1024)}


def _pack_rows(arrs, rows):
    parts, used = [], 0
    for a in arrs:
        r = _pack_tile_rows(a.size)
        parts.append(jnp.pad(a.reshape(-1), (0, r * 128 - a.size)).reshape(r, 128))
        used += r
    assert rows >= used and (rows - used) % 8 == 0, (rows, used)
    if rows > used:
        parts.append(jnp.zeros((rows - used, 128), f32))
    return jnp.concatenate(parts, axis=0)


def _pack_tile_rows(size):
    return -(-size // 1024) * 8


def _unpack_rows(pack, shapes):
    out, row = [], 0
    for s in shapes:
        size = 1
        for d in s:
            size *= d
        r = _pack_tile_rows(size)
        out.append(pack[row:row + r].reshape(-1)[:size].reshape(s))
        row += r
    return out


def kernel(x, norm_mix, norm_ffn, norm_final, ab_w_in, gdn_conv_w, gdn_a_log, gdn_dt_bias, gdn_norm, hgrn_lower_bounds, hgrn_norm, ab_w_out, c_w_in, c_conv_w, c_conv_b, c_gate_a_w, c_gate_a_b, c_gate_x_w, c_gate_x_b, c_lambda, c_w_out, ffn_w_up, ffn_conv_w, ffn_conv_b, ffn_w_down, loss_target, m_norm_mix, m_norm_ffn, m_norm_final, m_ab_w_in, m_gdn_conv_w, m_gdn_a_log, m_gdn_dt_bias, m_gdn_norm, m_hgrn_lower_bounds, m_hgrn_norm, m_ab_w_out, m_c_w_in, m_c_conv_w, m_c_conv_b, m_c_gate_a_w, m_c_gate_a_b, m_c_gate_x_w, m_c_gate_x_b, m_c_lambda, m_c_w_out, m_ffn_w_up, m_ffn_conv_w, m_ffn_conv_b, m_ffn_w_down, v_norm_mix, v_norm_ffn, v_norm_final, v_ab_w_in, v_gdn_conv_w, v_gdn_a_log, v_gdn_dt_bias, v_gdn_norm, v_hgrn_lower_bounds, v_hgrn_norm, v_ab_w_out, v_c_w_in, v_c_conv_w, v_c_conv_b, v_c_gate_a_w, v_c_gate_a_b, v_c_gate_x_w, v_c_gate_x_b, v_c_lambda, v_c_w_out, v_ffn_w_up, v_ffn_conv_w, v_ffn_conv_b, v_ffn_w_down):
    w = dict(zip(WEIGHTS, (norm_mix, norm_ffn, norm_final, ab_w_in, gdn_conv_w, gdn_a_log, gdn_dt_bias, gdn_norm, hgrn_lower_bounds, hgrn_norm, ab_w_out, c_w_in, c_conv_w, c_conv_b, c_gate_a_w, c_gate_a_b, c_gate_x_w, c_gate_x_b, c_lambda, c_w_out, ffn_w_up, ffn_conv_w, ffn_conv_b, ffn_w_down)))
    m = dict(zip(WEIGHTS, (m_norm_mix, m_norm_ffn, m_norm_final, m_ab_w_in, m_gdn_conv_w, m_gdn_a_log, m_gdn_dt_bias, m_gdn_norm, m_hgrn_lower_bounds, m_hgrn_norm, m_ab_w_out, m_c_w_in, m_c_conv_w, m_c_conv_b, m_c_gate_a_w, m_c_gate_a_b, m_c_gate_x_w, m_c_gate_x_b, m_c_lambda, m_c_w_out, m_ffn_w_up, m_ffn_conv_w, m_ffn_conv_b, m_ffn_w_down)))
    v = dict(zip(WEIGHTS, (v_norm_mix, v_norm_ffn, v_norm_final, v_ab_w_in, v_gdn_conv_w, v_gdn_a_log, v_gdn_dt_bias, v_gdn_norm, v_hgrn_lower_bounds, v_hgrn_norm, v_ab_w_out, v_c_w_in, v_c_conv_w, v_c_conv_b, v_c_gate_a_w, v_c_gate_a_b, v_c_gate_x_w, v_c_gate_x_b, v_c_lambda, v_c_w_out, v_ffn_w_up, v_ffn_conv_w, v_ffn_conv_b, v_ffn_w_down)))
    big = list(BIG)
    chip = 2 * lax.axis_index("x") + lax.axis_index("y")
    ids = jnp.stack([chip, lax.axis_index("c")]).astype(jnp.int32)

    def layer_parts(l):
        j = l // 2
        if l % 2 == 0:
            mix = [("ab_w_in", j), ("ab_w_out", j)]
        else:
            mix = [("c_w_in", j), ("c_gate_a_w", j), ("c_gate_x_w", j), ("c_w_out", j)]
        return mix, [("ffn_w_up", l), ("ffn_w_down", l)]

    def layer_shard(n, i):
        s = w[n][i]
        return s.reshape(-1, s.shape[-1]).astype(bf16)

    small_shard_shapes = [w[n].shape for n in SMALL_SHARDED]
    small_pack = _pack_rows([w[n] for n in SMALL_SHARDED], 160)
    mix0, ffn0 = layer_parts(0)
    gathered0 = _all_gather_chips([layer_shard(n, i) for n, i in mix0] + [small_pack], "all_gather_mixer0")
    gathers = {}

    def start_gather(key, after):
        shards = [layer_shard(n, i) for n, i in layer_parts(key[0])[0 if key[1] == "mix" else 1]]
        gathers[key] = _split_start(_gather_copies, shards, [_sds((N_CHIPS,) + s.shape, bf16) for s in shards],
                                    "gather_start_%d_%s" % key, after)
        return gathers[key]["token"]

    tok = ()
    for key in ((0, "ffn"), (1, "mix"), (1, "ffn")):
        tok = (start_gather(key, tok),)
    first_tokens = tok
    start_next = {(0, "ffn"): ((2, "mix"), (2, "ffn")), (1, "mix"): ((3, "mix"),), (1, "ffn"): ((3, "ffn"),)}
    fw = {}
    per_chip = [_unpack_rows(gathered0[-1][k], small_shard_shapes) for k in range(N_CHIPS)]
    for i, n in enumerate(SMALL_SHARDED):
        fw[n] = jnp.concatenate([per_chip[k][i] for k in range(N_CHIPS)], axis=-1)
    for n in SMALL:
        if n not in fw:
            fw[n] = w[n]
    wt = _layout_weights(fw)
    for n in big:
        wt[n] = [None] * FULL_SHAPES[n][0]

    def pre_layer(l, part, x_l):
        parts = layer_parts(l)[0 if part == "mix" else 1]
        if l == 0 and part == "mix":
            lands, deps = gathered0[:len(mix0)], first_tokens
        else:
            _, lands = _split_wait(_gather_copies, gathers[(l, part)], [x_l], "gather_wait_%d_%s" % (l, part))
            deps = tuple(lands[:1])
            for key in start_next.get((l, part), ()):
                deps = (start_gather(key, deps),)
            if (l, part) not in start_next:
                deps = ()
        for (n, i), slots in zip(parts, lands):
            wt[n][i] = _layer_full(n, slots)
        return deps

    scatters = []

    def post_grads(l, part, g):
        parts = layer_parts(l)[0 if part == "mix" else 1]
        slots = [_layer_slots(n, g[n][i]) for n, i in parts]
        st = _split_start(_scatter_copies, slots, [_sds((N_REL, s.shape[1] // 2, s.shape[2]), bf16) for s in slots],
                          "scatter_start_%d_%s" % (l, part))
        scatters.append((parts, st, "scatter_wait_%d_%s" % (l, part)))
        return (st["token"],)

    loss, dx, g = _local_step(x[0], loss_target[0], wt, pre_layer, post_grads)
    gf = _unlayout_grads(g)
    loss = lax.psum(loss[0, 0], ("x", "y", "c"))

    out_g, out_d, out_m, out_v = {}, {}, {}, {}
    f = {n: None for n in big}
    last = [n for n, _ in layer_parts(0)[0]]

    def finish(group, after):
        for parts, st, wait_name in group:
            gbs, lands = _split_wait(_scatter_copies, st, after, wait_name)
            for (n, i), gb, land in zip(parts, gbs, lands):
                f[n] = _sum_pieces(gb, land, ids, f[n], i, FULL_SHAPES[n][0], "rs_sum")

    def adamw(names, tag):
        filled = _sibling_fill_blocks([f[n] for n in names], [FULL_SHAPES[n][0] for n in names], "rs_sibling_fill_" + tag)
        for n, g_n in zip(names, filled):
            shp = w[n].shape
            if n in NATIVE_PERM:
                to_native = lambda t: t.transpose(NATIVE_PERM[n])
                res = _adamw_lead(to_native(w[n]), to_native(g_n.reshape(shp)), to_native(m[n]), to_native(v[n]), "adamw_" + n)
                res = [t.transpose(1, 2, 0) for t in res]
            else:
                as3d = lambda t: t.reshape((-1,) + shp[-2:])
                res = _adamw_big(as3d(w[n]), g_n, as3d(m[n]), as3d(v[n]), "adamw_" + n)
            out_g[n], out_d[n], out_m[n], out_v[n] = (t.reshape(shp) for t in res)

    share = _split_start(_share_copies, [_pack_rows([gf[n] for n in SMALL], 688)], [_sds((N_DEV, 688, 128), f32)],
                         "share_small_start", [dx])
    finish(scatters[:-1], [dx, share["token"]])
    adamw([n for n in big if n not in last], "a")
    finish(scatters[-1:], [out_v["ffn_w_up"]])
    adamw(last, "b")

    small_full_shapes = [FULL_SHAPES[n] for n in SMALL]
    (own_pack,), (all_packs,) = _split_wait(_share_copies, share, [out_v[last[0]]], "share_small_wait")
    me_arr = (4 * lax.axis_index("x") + 2 * lax.axis_index("y") + lax.axis_index("c")).astype(jnp.int32).reshape(1)
    small_sum = _sum_shared(own_pack, all_packs, me_arr, "sum_small")
    g_small = dict(zip(SMALL, _unpack_rows(small_sum, small_full_shapes)))
    for n in SMALL_SHARDED:
        width = w[n].shape[-1]
        g_small[n] = lax.dynamic_slice_in_dim(g_small[n], chip * width, width, axis=-1)

    def small2d(n, t):
        if n in NATIVE_PERM:
            t = t.transpose(NATIVE_PERM[n])
        return t.reshape(-1, t.shape[-1])

    sg, sd, sm, sv = _adamw_small([small2d(n, w[n]) for n in SMALL], [small2d(n, g_small[n]) for n in SMALL],
                                  [small2d(n, m[n]) for n in SMALL], [small2d(n, v[n]) for n in SMALL], "adamw_small")
    for i, n in enumerate(SMALL):
        for out, t in zip((out_g, out_d, out_m, out_v), (sg, sd, sm, sv)):
            if n in NATIVE_PERM:
                perm = NATIVE_PERM[n]
                shp_t = tuple(w[n].shape[p] for p in perm)
                out[n] = t[i].reshape(shp_t).transpose(tuple(perm.index(k) for k in range(len(perm))))
            else:
                out[n] = t[i].reshape(w[n].shape)
    return (loss, dx[None], *[out_g[n] for n in WEIGHTS], *[out_d[n] for n in WEIGHTS],
            *[out_m[n] for n in WEIGHTS], *[out_v[n] for n in WEIGHTS])
```

```python
import functools

import jax
import jax.numpy as jnp
from jax import lax
from jax.experimental import pallas as pl
from jax.experimental.pallas import tpu as pltpu

f32 = jnp.float32
bf16 = jnp.bfloat16

D = 1024
EPS = 1e-6
F_FLOOR = 1e-30
GDN_CHUNK = 64
GDN_INTRA_CHUNKS = 1
HGRN_CHUNK = 16
HGRN_STEP = 128
HEAD = 128
NH = 4
LRU_BLOCK = 256
D_FF = 2816
RG_C = 8.0
C_QKV, C_Z, C_QB, C_FB, C_IB, C_GB, C_BA = 0, 1536, 2048, 2560, 3072, 3584, 4096
TT = 256
ROW_CHUNK = 16
N_SLOTS = 4
VMEM_LIMIT = 56 * 1024 * 1024

ADAM_LR, ADAM_B1, ADAM_B2, ADAM_EPS, ADAM_WD, ADAM_STEP = 0.001, 0.9, 0.999, 1e-08, 0.01, 10


def _cp(**kw):
    return pltpu.CompilerParams(vmem_limit_bytes=VMEM_LIMIT, **kw)


def _sds(shape, dtype):
    return jax.ShapeDtypeStruct(shape, dtype)


def _dot(a, b, dims, precision=None):
    return lax.dot_general(a, b, (dims, ((), ())), precision=precision, preferred_element_type=f32)


NN = ((1,), (0,))
NT = ((1,), (1,))
TN = ((0,), (0,))


def _rms(x, g):
    return x * lax.rsqrt(jnp.mean(x * x, axis=-1, keepdims=True) + EPS) * g


def _silu(x):
    return x * jax.nn.sigmoid(x)


def _mm(a, b, mode, tm, tn, out_dtype, name):
    if mode == "nn":
        (m, k), n = a.shape, b.shape[1]
        a_spec = pl.BlockSpec((tm, k), lambda i, j: (i, 0))
        b_spec = pl.BlockSpec((k, tn), lambda i, j: (0, j))
        dims = NN
    elif mode == "nt":
        (m, k), n = a.shape, b.shape[0]
        a_spec = pl.BlockSpec((tm, k), lambda i, j: (i, 0))
        b_spec = pl.BlockSpec((tn, k), lambda i, j: (j, 0))
        dims = NT
    else:
        (k, m), n = a.shape, b.shape[1]
        a_spec = pl.BlockSpec((k, tm), lambda i, j: (0, i))
        b_spec = pl.BlockSpec((k, tn), lambda i, j: (0, j))
        dims = TN
    assert m % tm == 0 and n % tn == 0, (name, m, n, tm, tn)

    def body(a_ref, b_ref, o_ref):
        o_ref[...] = _dot(a_ref[...], b_ref[...], dims).astype(out_dtype)

    return pl.pallas_call(
        body, grid=(m // tm, n // tn), in_specs=[a_spec, b_spec],
        out_specs=pl.BlockSpec((tm, tn), lambda i, j: (i, j)),
        out_shape=_sds((m, n), out_dtype), compiler_params=_cp(), name=name)(a, b)


def _mm_res_norm(a, b, res, gain, name, tm=512):
    (m, k), n = a.shape, b.shape[1]
    tm = min(tm, m)
    assert n == D and m % tm == 0, (name, m, n)

    def body(a_ref, b_ref, r_ref, *rest):
        xv = r_ref[...] + _dot(a_ref[...], b_ref[...], NN)
        if gain is None:
            rest[0][...] = xv
        else:
            g_ref, x_ref, h_ref = rest
            x_ref[...] = xv
            h_ref[...] = _rms(xv, g_ref[...]).astype(bf16)

    row = pl.BlockSpec((tm, D), lambda i: (i, 0))
    in_specs = [pl.BlockSpec((tm, k), lambda i: (i, 0)), pl.BlockSpec((k, D), lambda i: (0, 0)), row]
    args = [a, b, res]
    if gain is None:
        out_specs, out_shape = row, _sds((m, D), f32)
    else:
        in_specs.append(pl.BlockSpec((1, D), lambda i: (0, 0)))
        args.append(gain)
        out_specs, out_shape = [row, row], [_sds((m, D), f32), _sds((m, D), bf16)]
    return pl.pallas_call(body, grid=(m // tm,), in_specs=in_specs, out_specs=out_specs, out_shape=out_shape,
                          compiler_params=_cp(), name=name)(*args)


def _tile(n, cands):
    for c in cands:
        if n % c == 0:
            return c
    raise ValueError(n)


def _mm_auto(a, b, mode, out_dtype, name):
    m = a.shape[1] if mode == "tn" else a.shape[0]
    n = b.shape[0] if mode == "nt" else b.shape[1]
    return _mm(a, b, mode, _tile(m, (512, 256, 128)), _tile(n, (1024, 1408, 512, 384, 256, 128)), out_dtype, name)


def _mm_nn_slots(a, bs, out_dtype, name, after=()):
    (m, k), w = a.shape, bs.shape[2]
    tm = _tile(m, (1024, 512, 256, 128))

    def body(a_ref, b_ref, *rest):
        o_ref = rest[len(after)]
        o_ref[...] = _dot(a_ref[...], b_ref[...], NN).astype(out_dtype)

    return pl.pallas_call(
        body, grid=(m // tm, N_SLOTS),
        in_specs=[pl.BlockSpec((tm, k), lambda i, j: (i, 0)), pl.BlockSpec((None, k, w), lambda i, j: (j, 0, 0))]
        + [pl.BlockSpec(memory_space=pl.ANY)] * len(after),
        out_specs=pl.BlockSpec((tm, w), lambda i, j: (i, j)),
        out_shape=_sds((m, N_SLOTS * w), out_dtype), compiler_params=_cp(), name=name)(a, bs, *after)


def _mm_nt_rmsbwd(a, b, x, gain, dres, name, after=()):
    m, k = a.shape
    slots = b.ndim == 3
    assert b.shape[-2] == D
    tm = _tile(m, (512, 256, 128)) if k < 4096 else _tile(m, (256, 128))

    def body(a_ref, b_ref, x_ref, g_ref, r_ref, *rest):
        dx_ref, dxb_ref, dg_ref = rest[len(after):]
        if slots:
            w = b.shape[2]
            dh = _dot(a_ref[:, 0:w], b_ref[0], NT)
            for s in range(1, N_SLOTS):
                dh = dh + _dot(a_ref[:, s * w:(s + 1) * w], b_ref[s], NT)
        else:
            dh = _dot(a_ref[...], b_ref[...], NT)
        _, vjp = jax.vjp(_rms, x_ref[...], g_ref[...])
        dx, dg = vjp(dh)
        dx = dx + r_ref[...]
        dx_ref[...] = dx
        dxb_ref[...] = dx.astype(bf16)

        @pl.when(pl.program_id(0) == 0)
        def _():
            dg_ref[...] = jnp.zeros_like(dg_ref)

        dg_ref[...] += dg

    row = pl.BlockSpec((tm, D), lambda i: (i, 0))
    vec = pl.BlockSpec((1, D), lambda i: (0, 0))
    b_spec = pl.BlockSpec(b.shape, (lambda i: (0, 0, 0)) if slots else (lambda i: (0, 0)), pipeline_mode=pl.Buffered(1))
    return pl.pallas_call(
        body, grid=(m // tm,),
        in_specs=[pl.BlockSpec((tm, k), lambda i: (i, 0)), b_spec, row, vec, row] + [pl.BlockSpec(memory_space=pl.ANY)] * len(after),
        out_specs=[row, row, vec],
        out_shape=[_sds((m, D), f32), _sds((m, D), bf16), _sds((1, D), f32)],
        compiler_params=_cp(), name=name)(a, b, x, gain, dres, *after)


def _mm_tn_slots(a, b, name):
    (k, m), w = a.shape, b.shape[1] // N_SLOTS
    tm = _tile(m, (512, 256, 128))

    def body(a_ref, b_ref, o_ref):
        o_ref[...] = _dot(a_ref[...], b_ref[...], TN).astype(bf16)

    return pl.pallas_call(
        body, grid=(N_SLOTS, m // tm),
        in_specs=[pl.BlockSpec((k, tm), lambda j, i: (0, i)), pl.BlockSpec((k, w), lambda j, i: (0, j))],
        out_specs=pl.BlockSpec((None, tm, w), lambda j, i: (j, i, 0)),
        out_shape=_sds((N_SLOTS, m, w), bf16), compiler_params=_cp(), name=name)(a, b)


def _rmsnorm_fwd(x, gain, name, after=()):
    t = x.shape[0]

    def body(x_ref, g_ref, *rest):
        h_ref = rest[len(after)]
        h_ref[...] = _rms(x_ref[...], g_ref[...]).astype(bf16)

    return pl.pallas_call(
        body, grid=(t // TT,),
        in_specs=[pl.BlockSpec((TT, D), lambda i: (i, 0)), pl.BlockSpec((1, D), lambda i: (0, 0))]
        + [pl.BlockSpec(memory_space=pl.ANY)] * len(after),
        out_specs=pl.BlockSpec((TT, D), lambda i: (i, 0)),
        out_shape=_sds((t, D), bf16), compiler_params=_cp(), name=name)(x, gain, *after)


def _final_loss(x, gain, target, name):
    t = x.shape[0]

    def loss_fn(xv, gv, tv):
        e = _rms(xv, gv) - tv
        return 0.5 * jnp.sum(jnp.mean(e * e, axis=-1))

    def body(x_ref, g_ref, t_ref, loss_ref, dx_ref, dxb_ref, dg_ref):
        val, (dx, dg) = jax.value_and_grad(loss_fn, argnums=(0, 1))(x_ref[...], g_ref[...], t_ref[...])

        @pl.when(pl.program_id(0) == 0)
        def _():
            dg_ref[...] = jnp.zeros_like(dg_ref)
            loss_ref[...] = jnp.zeros_like(loss_ref)

        dx_ref[...] = dx
        dxb_ref[...] = dx.astype(bf16)
        dg_ref[...] += dg
        loss_ref[...] += jnp.full((1, 128), val, f32)

    row = pl.BlockSpec((TT, D), lambda i: (i, 0))
    vec = pl.BlockSpec((1, D), lambda i: (0, 0))
    return pl.pallas_call(
        body, grid=(t // TT,), in_specs=[row, vec, row],
        out_specs=[pl.BlockSpec((1, 128), lambda i: (0, 0)), row, row, vec],
        out_shape=[_sds((1, 128), f32), _sds((t, D), f32), _sds((t, D), bf16), _sds((1, D), f32)],
        compiler_params=_cp(), name=name)(x, gain, target)


def _halo_rows(dtype):
    return 16 if dtype == bf16 else 8


def _conv_fwd(x, col0, c, w, b, name, tc=256, val=None, val_col0=0, out_dtype=f32):
    t = x.shape[0]
    width = w.shape[0]
    nt = t // TT
    hr = _halo_rows(x.dtype)
    hb = TT // hr

    def body(*refs):
        if val is None:
            x_ref, xh_ref, w_ref, b_ref, o_ref, xp = refs
        else:
            x_ref, xh_ref, w_ref, b_ref, v_ref, o_ref, act_ref, xp = refs
        i = pl.program_id(1)
        xp[0:8, :] = jnp.where(i == 0, 0.0, xh_ref[hr - 8:hr, :].astype(f32))
        xp[8:, :] = x_ref[...].astype(f32)
        bias = b_ref[...]
        taps = [w_ref[k:k + 1, :] for k in range(width)]
        for r0 in range(0, TT, ROW_CHUNK):
            rows = pl.ds(r0, ROW_CHUNK)
            acc = bias + taps[0] * xp[pl.ds(r0 + 8 - (width - 1), ROW_CHUNK), :]
            for k in range(1, width):
                acc = acc + taps[k] * xp[pl.ds(r0 + 8 - (width - 1) + k, ROW_CHUNK), :]
            o_ref[rows, :] = acc.astype(out_dtype)
            if val is not None:
                act_ref[rows, :] = (_silu(acc) * v_ref[rows, :]).astype(bf16)

    in_specs = [
        pl.BlockSpec((TT, tc), lambda j, i: (i, j + col0)),
        pl.BlockSpec((hr, tc), lambda j, i: (jnp.maximum(i * hb - 1, 0), j + col0)),
        pl.BlockSpec((width, tc), lambda j, i: (0, j)),
        pl.BlockSpec((1, tc), lambda j, i: (0, j)),
    ]
    args = [x, x, w, b]
    out_specs = [pl.BlockSpec((TT, tc), lambda j, i: (i, j))]
    out_shape = [_sds((t, c), out_dtype)]
    if val is not None:
        in_specs.append(pl.BlockSpec((TT, tc), lambda j, i: (i, j + val_col0)))
        args.append(val)
        out_specs.append(pl.BlockSpec((TT, tc), lambda j, i: (i, j)))
        out_shape.append(_sds((t, c), bf16))
    res = pl.pallas_call(
        body, grid=(c // tc, nt), in_specs=in_specs, out_specs=out_specs, out_shape=out_shape,
        scratch_shapes=[pltpu.VMEM((TT + 8, tc), f32)], compiler_params=_cp(), name=name)(*args)
    return res[0] if val is None else res


def _conv_bwd(dc, x, col0, w, name, tc=256, dx_dtype=bf16, into=None):
    t, c = dc.shape
    width = w.shape[0]
    nt = t // TT
    hr = _halo_rows(x.dtype)
    hd = _halo_rows(dc.dtype)

    def body(dc_ref, dcn_ref, x_ref, xh_ref, w_ref, *rest):
        dx_ref, dw_ref, db_ref, dcp, xp = rest[(0 if into is None else 1):]
        i = pl.program_id(1)
        dcv = dc_ref[...].astype(f32)
        dcp[0:TT, :] = dcv
        dcp[TT:, :] = jnp.where(i == nt - 1, 0.0, dcn_ref[0:8, :].astype(f32))
        xp[0:8, :] = jnp.where(i == 0, 0.0, xh_ref[hr - 8:hr, :].astype(f32))
        xp[8:, :] = x_ref[...].astype(f32)

        @pl.when(i == 0)
        def _():
            dw_ref[...] = jnp.zeros_like(dw_ref)
            db_ref[...] = jnp.zeros_like(db_ref)

        taps = [w_ref[k:k + 1, :] for k in range(width)]
        fold = lambda p: p[0:8] + p[8:ROW_CHUNK]
        dws = [jnp.zeros((8, tc), f32) for _ in range(width)]
        dbs = jnp.zeros((8, tc), f32)
        for r0 in range(0, TT, ROW_CHUNK):
            dck = dcp[pl.ds(r0, ROW_CHUNK), :]
            acc = taps[0] * dcp[pl.ds(r0 + (width - 1), ROW_CHUNK), :]
            for k in range(1, width):
                acc = acc + taps[k] * dcp[pl.ds(r0 + (width - 1) - k, ROW_CHUNK), :]
            dx_ref[pl.ds(r0, ROW_CHUNK), :] = acc.astype(dx_dtype)
            for k in range(width):
                dws[k] = dws[k] + fold(dck * xp[pl.ds(r0 + 8 - (width - 1) + k, ROW_CHUNK), :])
            dbs = dbs + fold(dck)
        for k in range(width):
            dw_ref[k:k + 1, :] += jnp.sum(dws[k], axis=0, keepdims=True)
        db_ref[...] += jnp.sum(dbs, axis=0, keepdims=True)

    in_specs = [
        pl.BlockSpec((TT, tc), lambda j, i: (i, j)),
        pl.BlockSpec((hd, tc), lambda j, i: (jnp.minimum((i + 1) * (TT // hd), t // hd - 1), j)),
        pl.BlockSpec((TT, tc), lambda j, i: (i, j + col0)),
        pl.BlockSpec((hr, tc), lambda j, i: (jnp.maximum(i * (TT // hr) - 1, 0), j + col0)),
        pl.BlockSpec((width, tc), lambda j, i: (0, j)),
    ]
    out_specs = [
        pl.BlockSpec((TT, tc), lambda j, i: (i, j)),
        pl.BlockSpec((8, tc), lambda j, i: (0, j)),
        pl.BlockSpec((1, tc), lambda j, i: (0, j)),
    ]
    args, aliases, dx_shape = [dc, dc, x, x, w], {}, _sds((t, c), dx_dtype)
    if into is not None:
        in_specs.append(pl.BlockSpec(memory_space=pl.ANY))
        args.append(into)
        aliases, dx_shape = {5: 0}, _sds(into.shape, into.dtype)
    return pl.pallas_call(
        body, grid=(c // tc, nt), in_specs=in_specs, out_specs=out_specs,
        out_shape=[dx_shape, _sds((8, c), f32), _sds((1, c), f32)], input_output_aliases=aliases,
        scratch_shapes=[pltpu.VMEM((TT + 8, tc), f32), pltpu.VMEM((TT + 8, tc), f32)],
        compiler_params=_cp(), name=name)(*args)


def _bdot_impl(a, b, dims):
    return _dot(a.astype(bf16), b.astype(bf16), dims)


@functools.partial(jax.custom_vjp, nondiff_argnums=(2,))
def _bdot(a, b, dims):
    return _bdot_impl(a, b, dims)


def _bdot_fwd(a, b, dims):
    return _bdot_impl(a, b, dims), (a, b)


def _bdot_bwd(dims, res, ct):
    a, b = res
    if dims == NN:
        return _bdot_impl(ct, b, NT), _bdot_impl(a, ct, TN)
    if dims == NT:
        return _bdot_impl(ct, b, NN), _bdot_impl(ct, a, TN)
    return _bdot_impl(b, ct, NT), _bdot_impl(a, ct, NN)


_bdot.defvjp(_bdot_fwd, _bdot_bwd)


def _split2(a):
    hi = a.astype(bf16)
    return hi, (a - hi.astype(f32)).astype(bf16)


def _dot3_impl(a, b, dims):
    a_hi, a_lo = _split2(a)
    b_hi, b_lo = _split2(b)
    return (_dot(a_hi, b_hi, dims) + _dot(a_hi, b_lo, dims)) + _dot(a_lo, b_hi, dims)


def _tril_dot_impl(tril, x, dims):
    t = tril.astype(bf16)
    x1 = x.astype(bf16)
    r1 = x - x1.astype(f32)
    x2 = r1.astype(bf16)
    x3 = (r1 - x2.astype(f32)).astype(bf16)
    return (_dot(t, x3, dims) + _dot(t, x2, dims)) + _dot(t, x1, dims)


@jax.custom_vjp
def _cumsum_rows(tril, x):
    return _tril_dot_impl(tril, x, NN)


def _cumsum_rows_fwd(tril, x):
    return _tril_dot_impl(tril, x, NN), tril


def _cumsum_rows_bwd(tril, ct):
    return jnp.zeros_like(tril), _tril_dot_impl(tril, ct, TN)


_cumsum_rows.defvjp(_cumsum_rows_fwd, _cumsum_rows_bwd)


def _nilpotent_inverse(n):
    hs = range(len(n))
    c = n[0].shape[0]
    eye = (lax.broadcasted_iota(jnp.int32, (c, c), 0) == lax.broadcasted_iota(jnp.int32, (c, c), 1)).astype(f32)
    t = [eye + n[h] for h in hs]
    p = n
    for _ in range(5):
        p = [_dot3_impl(p[h], p[h], NN) for h in hs]
        tp = [_dot3_impl(t[h], p[h], NN) for h in hs]
        t = [t[h] + tp[h] for h in hs]
    return t


@jax.custom_vjp
def _nilpotent_solve(n, rhs):
    t = _nilpotent_inverse(n)
    return [_dot3_impl(t[h], rhs[h], NN) for h in range(len(n))]


def _nilpotent_solve_fwd(n, rhs):
    t = _nilpotent_inverse(n)
    sol = [_dot3_impl(t[h], rhs[h], NN) for h in range(len(n))]
    return sol, (t, sol)


def _nilpotent_solve_bwd(res, ct):
    t, sol = res
    hs = range(len(t))
    d_rhs = [_dot3_impl(t[h], ct[h], TN) for h in hs]
    d_n = [_dot3_impl(d_rhs[h], sol[h], NT) for h in hs]
    return d_n, d_rhs


_nilpotent_solve.defvjp(_nilpotent_solve_fwd, _nilpotent_solve_bwd)


def _gdn_intra4(cq, ck, cv, ba, alog_v, dtb_v):
    c = GDN_CHUNK
    hs = range(NH)
    lane = lax.broadcasted_iota(jnp.int32, (1, HEAD), 1)
    mb = [(lane == h).astype(f32) for h in hs]
    ma = [(lane == h + NH).astype(f32) for h in hs]
    beta = [jax.nn.sigmoid(jnp.sum(ba * mb[h], axis=1, keepdims=True)) for h in hs]
    alpha = [jnp.sum(ba * ma[h], axis=1, keepdims=True) for h in hs]
    alog = [jnp.sum(alog_v * ma[h], axis=1, keepdims=True) for h in hs]
    dtb = [jnp.sum(dtb_v * ma[h], axis=1, keepdims=True) for h in hs]
    g = [-jnp.exp(alog[h]) * jax.nn.softplus(alpha[h] + dtb[h]) for h in hs]
    q = [_silu(cq[h]) for h in hs]
    q = [q[h] * lax.rsqrt(jnp.sum(q[h] * q[h], axis=-1, keepdims=True) + EPS) * (HEAD ** -0.5) for h in hs]
    k = [_silu(ck[h]) for h in hs]
    k = [k[h] * lax.rsqrt(jnp.sum(k[h] * k[h], axis=-1, keepdims=True) + EPS) for h in hs]
    v = [_silu(cv[h]) for h in hs]
    row = lax.broadcasted_iota(jnp.int32, (c, c), 0)
    col = lax.broadcasted_iota(jnp.int32, (c, c), 1)
    causal = row >= col
    tril = causal.astype(f32)
    gc = [_cumsum_rows(tril, jnp.broadcast_to(g[h], (c, HEAD))) for h in hs]
    gcc = [_cumsum_rows(tril, jnp.broadcast_to(g[h], (c, c))) for h in hs]
    decay = [jnp.where(causal, jnp.exp(jnp.where(causal, gcc[h] - gcc[h].T, 0.0)), 0.0) for h in hs]
    kb = [k[h] * beta[h] for h in hs]
    kk = [_bdot(kb[h], k[h], NT) for h in hs]
    p = [-jnp.where(row > col, kk[h] * decay[h], 0.0) for h in hs]
    egc = [jnp.exp(gc[h]) for h in hs]
    sol = _nilpotent_solve(p, [jnp.concatenate([v[h] * beta[h], kb[h] * egc[h]], axis=1) for h in hs])
    qk = [_bdot(q[h], k[h], NT) for h in hs]
    attn = [qk[h] * decay[h] for h in hs]
    rowv = lax.broadcasted_iota(jnp.int32, (c, 1), 0)
    gc_last = [jnp.sum(jnp.where(rowv == c - 1, gc[h], 0.0), axis=0, keepdims=True) for h in hs]
    return ([sol[h][:, :HEAD] for h in hs], [sol[h][:, HEAD:] for h in hs], [q[h] * egc[h] for h in hs],
            [k[h] * jnp.exp(gc_last[h] - gc[h]) for h in hs], attn, [jnp.exp(gc_last[h]) for h in hs])


def _gdn_seq4(u, w, q_dec, k_dec, attn, dl, z, s, gain):
    hs = range(NH)
    ws = [_bdot(w[h], s[h], NN) for h in hs]
    qs = [_bdot(q_dec[h], s[h], NN) for h in hs]
    v_new = [u[h] - ws[h] for h in hs]
    av = [_bdot(attn[h], v_new[h], NN) for h in hs]
    kv = [_bdot(k_dec[h], v_new[h], TN) for h in hs]
    o = [_rms(qs[h] + av[h], gain) * _silu(z[h]) for h in hs]
    return o, [s[h] * dl[h] + kv[h] for h in hs]


def _hsl(h):
    return slice(h * HEAD, (h + 1) * HEAD)


def _gdn2_fwd(conv, p, alog_v, dtb_v, gain, name):
    t = conv.shape[0]
    c = GDN_CHUNK
    nch = t // c
    w512 = NH * HEAD
    wide = lambda off: pl.BlockSpec((c, w512), lambda n: (n, off))
    vec = pl.BlockSpec((1, HEAD), lambda n: (0, 0))
    attn_spec = pl.BlockSpec((1, NH, c, c), lambda n: (n, 0, 0, 0))
    dl_spec = pl.BlockSpec((1, NH, HEAD), lambda n: (n, 0, 0))

    cps = GDN_INTRA_CHUNKS
    iwide = lambda off: pl.BlockSpec((cps * c, w512), lambda n: (n, off))

    def intra(cq, ck, cv, ba, al, dt, u_ref, w_ref, qd_ref, kd_ref, at_ref, dl_ref):
        for ci in range(cps):
            rows = slice(ci * c, (ci + 1) * c)
            u, w, qd, kd, at, dl = _gdn_intra4([cq[rows, _hsl(h)] for h in range(NH)], [ck[rows, _hsl(h)] for h in range(NH)],
                                               [cv[rows, _hsl(h)] for h in range(NH)], ba[rows, :], al[...], dt[...])
            for h in range(NH):
                u_ref[rows, _hsl(h)] = u[h]
                w_ref[rows, _hsl(h)] = w[h]
                qd_ref[rows, _hsl(h)] = qd[h]
                kd_ref[rows, _hsl(h)] = kd[h]
                at_ref[ci, h] = at[h]
                dl_ref[ci, h:h + 1, :] = dl[h]

    u, w, qd, kd, at, dl = pl.pallas_call(
        intra, grid=(nch // cps,),
        in_specs=[iwide(0), iwide(1), iwide(2), pl.BlockSpec((cps * c, HEAD), lambda n: (n, C_BA // HEAD)), vec, vec],
        out_specs=[iwide(0)] * 4 + [pl.BlockSpec((cps, NH, c, c), lambda n: (n, 0, 0, 0)),
                                    pl.BlockSpec((cps, NH, HEAD), lambda n: (n, 0, 0))],
        out_shape=[_sds((t, w512), f32)] * 4 + [_sds((nch, NH, c, c), f32), _sds((nch, NH, HEAD), f32)],
        compiler_params=_cp(), name=name + "_intra")(conv, conv, conv, p, alog_v, dtb_v)

    def seq(u_ref, w_ref, qd_ref, kd_ref, at_ref, dl_ref, z_ref, gn, o_ref, ss_ref, s_scr):
        @pl.when(pl.program_id(0) == 0)
        def _():
            s_scr[...] = jnp.zeros_like(s_scr)

        hs = range(NH)
        s = [s_scr[h] for h in hs]
        for h in hs:
            ss_ref[0, h] = s[h]
        o, s_new = _gdn_seq4([u_ref[:, _hsl(h)] for h in hs], [w_ref[:, _hsl(h)] for h in hs], [qd_ref[:, _hsl(h)] for h in hs],
                             [kd_ref[:, _hsl(h)] for h in hs], [at_ref[0, h] for h in hs], [dl_ref[0, h:h + 1, :] for h in hs],
                             [z_ref[:, _hsl(h)] for h in hs], s, gn[...])
        for h in hs:
            o_ref[:, _hsl(h)] = o[h].astype(bf16)
            s_scr[h] = s_new[h]

    o, states = pl.pallas_call(
        seq, grid=(nch,),
        in_specs=[wide(0)] * 4 + [attn_spec, dl_spec, wide(C_Z // w512), vec],
        out_specs=[wide(0), pl.BlockSpec((1, NH, HEAD, HEAD), lambda n: (n, 0, 0, 0))],
        out_shape=[_sds((t, w512), bf16), _sds((nch, NH, HEAD, HEAD), f32)],
        scratch_shapes=[pltpu.VMEM((NH, HEAD, HEAD), f32)],
        compiler_params=_cp(), name=name + "_seq")(u, w, qd, kd, at, dl, p, gain)
    return o, dict(u=u, w=w, qd=qd, kd=kd, at=at, dl=dl, states=states)


def _gdn2_bwd(conv, p, alog_v, dtb_v, gain, saved, do, name):
    t = conv.shape[0]
    c = GDN_CHUNK
    nch = t // c
    w512 = NH * HEAD
    rwide = lambda off: pl.BlockSpec((c, w512), lambda n: (nch - 1 - n, off))
    rvec = pl.BlockSpec((1, HEAD), lambda n: (0, 0))
    rattn = pl.BlockSpec((1, NH, c, c), lambda n: (nch - 1 - n, 0, 0, 0))
    rdl = pl.BlockSpec((1, NH, HEAD), lambda n: (nch - 1 - n, 0, 0))

    def seq_bwd(u_ref, w_ref, qd_ref, kd_ref, at_ref, dl_ref, z_ref, gn, ss_ref, do_ref,
                du_ref, dw_ref, dqd_ref, dkd_ref, dat_ref, ddl_ref, dz_ref, dgn_ref, ds_scr):
        @pl.when(pl.program_id(0) == 0)
        def _():
            ds_scr[...] = jnp.zeros_like(ds_scr)
            dgn_ref[...] = jnp.zeros_like(dgn_ref)

        hs = range(NH)
        _, vjp = jax.vjp(_gdn_seq4, [u_ref[:, _hsl(h)] for h in hs], [w_ref[:, _hsl(h)] for h in hs],
                         [qd_ref[:, _hsl(h)] for h in hs], [kd_ref[:, _hsl(h)] for h in hs], [at_ref[0, h] for h in hs],
                         [dl_ref[0, h:h + 1, :] for h in hs], [z_ref[:, _hsl(h)] for h in hs], [ss_ref[0, h] for h in hs], gn[...])
        du, dw, dqd, dkd, dat, ddl, dz, ds, dg = vjp(([do_ref[:, _hsl(h)] for h in hs], [ds_scr[h] for h in hs]))
        for h in hs:
            du_ref[:, _hsl(h)] = du[h]
            dw_ref[:, _hsl(h)] = dw[h]
            dqd_ref[:, _hsl(h)] = dqd[h]
            dkd_ref[:, _hsl(h)] = dkd[h]
            dat_ref[0, h] = dat[h]
            ddl_ref[0, h:h + 1, :] = ddl[h]
            dz_ref[:, _hsl(h)] = dz[h]
            ds_scr[h] = ds[h]
        dgn_ref[...] += dg

    du, dw, dqd, dkd, dat, ddl, dz, dgn = pl.pallas_call(
        seq_bwd, grid=(nch,),
        in_specs=[rwide(0)] * 4 + [rattn, rdl, rwide(C_Z // w512), rvec,
                                   pl.BlockSpec((1, NH, HEAD, HEAD), lambda n: (nch - 1 - n, 0, 0, 0)), rwide(0)],
        out_specs=[rwide(0)] * 4 + [rattn, rdl, rwide(0), rvec],
        out_shape=[_sds((t, w512), f32)] * 4 + [_sds((nch, NH, c, c), f32), _sds((nch, NH, HEAD), f32),
                                                _sds((t, w512), f32), _sds((1, HEAD), f32)],
        scratch_shapes=[pltpu.VMEM((NH, HEAD, HEAD), f32)],
        compiler_params=_cp(), name=name + "_seq")(
            saved["u"], saved["w"], saved["qd"], saved["kd"], saved["at"], saved["dl"], p, gain, saved["states"], do)

    cps = GDN_INTRA_CHUNKS
    wide = lambda off: pl.BlockSpec((cps * c, w512), lambda n: (n, off))
    vec = pl.BlockSpec((1, HEAD), lambda n: (0, 0))
    attn_spec = pl.BlockSpec((cps, NH, c, c), lambda n: (n, 0, 0, 0))
    dl_spec = pl.BlockSpec((cps, NH, HEAD), lambda n: (n, 0, 0))

    def intra_bwd(cq, ck, cv, ba, al, dt, du_ref, dw_ref, dqd_ref, dkd_ref, dat_ref, ddl_ref,
                  dc_ref, dba_ref, dal_ref, ddt_ref):
        @pl.when(pl.program_id(0) == 0)
        def _():
            dal_ref[...] = jnp.zeros_like(dal_ref)
            ddt_ref[...] = jnp.zeros_like(ddt_ref)

        dal = jnp.zeros((1, HEAD), f32)
        ddt = jnp.zeros((1, HEAD), f32)
        for ci in range(cps):
            rows = slice(ci * c, (ci + 1) * c)
            hs = range(NH)
            _, vjp = jax.vjp(_gdn_intra4, [cq[rows, _hsl(h)] for h in hs], [ck[rows, _hsl(h)] for h in hs],
                             [cv[rows, _hsl(h)] for h in hs], ba[rows, :], al[...], dt[...])
            g_q, g_k, g_v, g_ba, g_al, g_dt = vjp((
                [du_ref[rows, _hsl(h)] for h in hs], [dw_ref[rows, _hsl(h)] for h in hs], [dqd_ref[rows, _hsl(h)] for h in hs],
                [dkd_ref[rows, _hsl(h)] for h in hs], [dat_ref[ci, h] for h in hs], [ddl_ref[ci, h:h + 1, :] for h in hs]))
            for h in hs:
                dc_ref[rows, _hsl(h)] = g_q[h]
                dc_ref[rows, _hsl(NH + h)] = g_k[h]
                dc_ref[rows, _hsl(2 * NH + h)] = g_v[h]
            dal = dal + g_al
            ddt = ddt + g_dt
            dba_ref[rows, :] = g_ba
        dal_ref[...] += dal
        ddt_ref[...] += ddt

    dconv, dba, dal, ddt = pl.pallas_call(
        intra_bwd, grid=(nch // cps,),
        in_specs=[wide(0), wide(1), wide(2), pl.BlockSpec((cps * c, HEAD), lambda n: (n, C_BA // HEAD)), vec, vec]
        + [wide(0)] * 4 + [attn_spec, dl_spec],
        out_specs=[pl.BlockSpec((cps * c, 3 * w512), lambda n: (n, 0)), pl.BlockSpec((cps * c, HEAD), lambda n: (n, 0)),
                   vec, vec],
        out_shape=[_sds((t, 3 * w512), f32), _sds((t, HEAD), f32), _sds((1, HEAD), f32), _sds((1, HEAD), f32)],
        compiler_params=_cp(), name=name + "_intra")(conv, conv, conv, p, alog_v, dtb_v, du, dw, dqd, dkd, dat, ddl)
    return dconv, dz, dba, dal, ddt, dgn


def _hgrn_intra(qb, fb, ib, lb):
    c = HGRN_CHUNK
    ns = range(len(qb))
    f = [lb + (1.0 - lb) * jax.nn.sigmoid(fb[i]) for i in ns]
    logf = [jnp.log(jnp.maximum(f[i], F_FLOOR)) for i in ns]
    k = [1.0 - f[i] for i in ns]
    q = [_silu(qb[i]) for i in ns]
    row = lax.broadcasted_iota(jnp.int32, (c, c), 0)
    col = lax.broadcasted_iota(jnp.int32, (c, c), 1)
    tril = (row >= col).astype(f32)
    b = [_cumsum_rows(tril, logf[i]) for i in ns]
    ri = lax.broadcasted_iota(jnp.int32, (c, 1), 0)
    o = [jnp.zeros((c, HEAD), f32) for _ in ns]
    for j in range(c):
        mj = ri == j
        ok = ri >= j
        bj = [jnp.sum(jnp.where(mj, b[i], 0.0), axis=0, keepdims=True) for i in ns]
        kj = [jnp.sum(jnp.where(mj, k[i], 0.0), axis=0, keepdims=True) for i in ns]
        vj = [jnp.sum(jnp.where(mj, ib[i], 0.0), axis=0, keepdims=True) for i in ns]
        e = [jnp.where(ok, jnp.exp(jnp.where(ok, b[i] - bj[i], 0.0)), 0.0) for i in ns]
        s = [jnp.sum(q[i] * kj[i] * e[i], axis=1, keepdims=True) for i in ns]
        o = [o[i] + s[i] * vj[i] for i in ns]
    b_last = [jnp.sum(jnp.where(ri == c - 1, b[i], 0.0), axis=0, keepdims=True) for i in ns]
    return (o, [q[i] * jnp.exp(b[i]) for i in ns], [k[i] * jnp.exp(b_last[i] - b[i]) for i in ns],
            [jnp.exp(b_last[i]) for i in ns])


def _hgrn_seq_heads(o_intra, q_dec, k_dec, dl, v, gb, st, gain):
    hs = range(len(st))
    qs = [_bdot(q_dec[h], st[h], NT) for h in hs]
    vk = [_bdot(v[h], k_dec[h], TN) for h in hs]
    o = [_rms(o_intra[h] + qs[h], gain) * _silu(gb[h]) for h in hs]
    return o, [st[h] * dl[h] + vk[h] for h in hs]


def _hgrn_tiles(ref, rows):
    return [[ref[rw, _hsl(h)] for rw in rows] for h in range(NH)]


def _hgrn_fwd(p, lb, gain, name):
    t = p.shape[0]
    r = HGRN_STEP
    ns = t // r
    nsub = r // HGRN_CHUNK
    w512 = NH * HEAD
    blk = lambda off: pl.BlockSpec((r, w512), lambda n: (n, off // w512))

    def body(qb, fb, ib, gb, lb_ref, gn, o_ref, ss_ref, s_scr):
        @pl.when(pl.program_id(0) == 0)
        def _():
            s_scr[...] = jnp.zeros_like(s_scr)

        hs = range(NH)
        rows = [pl.ds(ch * HGRN_CHUNK, HGRN_CHUNK) for ch in range(nsub)]
        q_t, f_t, v_t, g_t = (_hgrn_tiles(ref, rows) for ref in (qb, fb, ib, gb))
        intra = [_hgrn_intra(q_t[h], f_t[h], v_t[h], lb_ref[:, _hsl(h)]) for h in hs]
        st = [s_scr[h] for h in hs]
        for h in hs:
            ss_ref[0, h] = st[h]
        for ch in range(nsub):
            o, st = _hgrn_seq_heads([intra[h][0][ch] for h in hs], [intra[h][1][ch] for h in hs], [intra[h][2][ch] for h in hs],
                                    [intra[h][3][ch] for h in hs], [v_t[h][ch] for h in hs], [g_t[h][ch] for h in hs], st, gn[...])
            for h in hs:
                o_ref[rows[ch], _hsl(h)] = o[h].astype(bf16)
        for h in hs:
            s_scr[h] = st[h]

    return pl.pallas_call(
        body, grid=(ns,),
        in_specs=[blk(C_QB), blk(C_FB), blk(C_IB), blk(C_GB),
                  pl.BlockSpec((1, w512), lambda n: (0, 0)), pl.BlockSpec((1, HEAD), lambda n: (0, 0))],
        out_specs=[pl.BlockSpec((r, w512), lambda n: (n, 0)),
                   pl.BlockSpec((1, NH, HEAD, HEAD), lambda n: (n, 0, 0, 0))],
        out_shape=[_sds((t, w512), bf16), _sds((ns, NH, HEAD, HEAD), f32)],
        scratch_shapes=[pltpu.VMEM((NH, HEAD, HEAD), f32)],
        compiler_params=_cp(), name=name)(p, p, p, p, lb, gain)


def _hgrn_bwd(p, lb, gain, states, do, name):
    t = p.shape[0]
    r = HGRN_STEP
    ns = t // r
    nsub = r // HGRN_CHUNK
    w512 = NH * HEAD
    blk = lambda off: pl.BlockSpec((r, w512), lambda n: (ns - 1 - n, off // w512))

    def body(qb, fb, ib, gb, lb_ref, gn, ss_ref, do_ref, dp_ref, dlb_ref, dgn_ref, ds_scr, st_scr):
        @pl.when(pl.program_id(0) == 0)
        def _():
            ds_scr[...] = jnp.zeros_like(ds_scr)
            dgn_ref[...] = jnp.zeros_like(dgn_ref)

        hs = range(NH)
        gnv = gn[...]
        rows = [pl.ds(ch * HGRN_CHUNK, HGRN_CHUNK) for ch in range(nsub)]
        q_t, f_t, v_t, g_t = (_hgrn_tiles(ref, rows) for ref in (qb, fb, ib, gb))
        intra, vjp_intra = [], []
        for h in hs:
            out, fn = jax.vjp(_hgrn_intra, q_t[h], f_t[h], v_t[h], lb_ref[:, _hsl(h)])
            intra.append(out)
            vjp_intra.append(fn)
        at = lambda part, ch: [intra[h][part][ch] for h in hs]
        st = [ss_ref[0, h] for h in hs]
        for ch in range(nsub):
            for h in hs:
                st_scr[ch * NH + h] = st[h]
            if ch < nsub - 1:
                vk = [_bdot(v_t[h][ch], intra[h][2][ch], TN) for h in hs]
                st = [st[h] * intra[h][3][ch] + vk[h] for h in hs]
        ds = [ds_scr[h] for h in hs]
        dgn = jnp.zeros((1, HEAD), f32)
        cots = [[[None] * nsub for _ in range(4)] for _ in hs]
        d_v = [[None] * nsub for _ in hs]
        for ch in reversed(range(nsub)):
            _, vjp = jax.vjp(_hgrn_seq_heads, at(0, ch), at(1, ch), at(2, ch), at(3, ch), [v_t[h][ch] for h in hs],
                             [g_t[h][ch] for h in hs], [st_scr[ch * NH + h] for h in hs], gnv)
            d_oi, d_qd, d_kd, d_dl, d_vv, g_g, ds, g_gn = vjp(([do_ref[rows[ch], _hsl(h)] for h in hs], ds))
            for h in hs:
                cots[h][0][ch], cots[h][1][ch], cots[h][2][ch], cots[h][3][ch] = d_oi[h], d_qd[h], d_kd[h], d_dl[h]
                d_v[h][ch] = d_vv[h]
                dp_ref[rows[ch], pl.ds(3 * w512 + h * HEAD, HEAD)] = g_g[h].astype(bf16)
            dgn = dgn + g_gn
        for h in hs:
            g_q, g_f, g_i, g_lb = vjp_intra[h](tuple(cots[h]))
            for ch in range(nsub):
                dp_ref[rows[ch], pl.ds(h * HEAD, HEAD)] = g_q[ch].astype(bf16)
                dp_ref[rows[ch], pl.ds(w512 + h * HEAD, HEAD)] = g_f[ch].astype(bf16)
                dp_ref[rows[ch], pl.ds(2 * w512 + h * HEAD, HEAD)] = (g_i[ch] + d_v[h][ch]).astype(bf16)
            dlb_ref[0, :, _hsl(h)] = g_lb
            ds_scr[h] = ds[h]
        dgn_ref[...] += dgn

    return pl.pallas_call(
        body, grid=(ns,),
        in_specs=[blk(C_QB), blk(C_FB), blk(C_IB), blk(C_GB),
                  pl.BlockSpec((1, w512), lambda n: (0, 0)), pl.BlockSpec((1, HEAD), lambda n: (0, 0)),
                  pl.BlockSpec((1, NH, HEAD, HEAD), lambda n: (ns - 1 - n, 0, 0, 0)),
                  pl.BlockSpec((r, w512), lambda n: (ns - 1 - n, 1))],
        out_specs=[pl.BlockSpec((r, 4 * w512), lambda n: (ns - 1 - n, 0)),
                   pl.BlockSpec((1, 1, w512), lambda n: (n, 0, 0)),
                   pl.BlockSpec((1, HEAD), lambda n: (0, 0))],
        out_shape=[_sds((t, 4 * w512), bf16), _sds((ns, 1, w512), f32), _sds((1, HEAD), f32)],
        scratch_shapes=[pltpu.VMEM((NH, HEAD, HEAD), f32), pltpu.VMEM((nsub * NH, HEAD, HEAD), f32)],
        compiler_params=_cp(), name=name)(p, p, p, p, lb, gain, states, do)


def _lru_gates(xb, wa, wx, ba, bx, lam):
    xh = xb.astype(bf16)
    r = jax.nn.sigmoid(_dot(xh, wa.astype(bf16), NN) + ba)
    i = jax.nn.sigmoid(_dot(xh, wx.astype(bf16), NN) + bx)
    log_a = -RG_C * r * jax.nn.softplus(-lam)
    a = jnp.exp(log_a)
    t2 = 2.0 * log_a
    series = -t2 * (1.0 + t2 * (0.5 + t2 * (1.0 / 6.0 + t2 * (1.0 / 24.0))))
    om = jnp.where(t2 > -1e-2, series, 1.0 - jnp.exp(t2))
    u = jnp.sqrt(jnp.maximum(om, 0.0)) * (i * xb)
    return a, u


def _lru_gates_fwd(xc, wa, wx, ba, bx, lam, name):
    t = xc.shape[0]
    blk = pl.BlockSpec((TT, LRU_BLOCK), lambda h, i: (i, h))
    wsp = pl.BlockSpec((1, LRU_BLOCK, LRU_BLOCK), lambda h, i: (h, 0, 0))
    vsp = pl.BlockSpec((1, LRU_BLOCK), lambda h, i: (0, h))

    def body(x_ref, wa_ref, wx_ref, ba_ref, bx_ref, lam_ref, a_ref, u_ref):
        a, u = _lru_gates(x_ref[...], wa_ref[0], wx_ref[0], ba_ref[...], bx_ref[...], lam_ref[...])
        a_ref[...] = a
        u_ref[...] = u

    return pl.pallas_call(
        body, grid=(NH, t // TT), in_specs=[blk, wsp, wsp, vsp, vsp, vsp], out_specs=[blk, blk],
        out_shape=[_sds((t, D), f32)] * 2, compiler_params=_cp(), name=name)(xc, wa, wx, ba, bx, lam)


def _lru_gates_bwd(xc, wa, wx, ba, bx, lam, da, du, name):
    t = xc.shape[0]
    blk = pl.BlockSpec((TT, LRU_BLOCK), lambda h, i: (i, h))
    wsp = pl.BlockSpec((1, LRU_BLOCK, LRU_BLOCK), lambda h, i: (h, 0, 0))
    vsp = pl.BlockSpec((1, LRU_BLOCK), lambda h, i: (0, h))

    def body(x_ref, wa_ref, wx_ref, ba_ref, bx_ref, lam_ref, da_ref, du_ref,
             dx_ref, dwa_ref, dwx_ref, dba_ref, dbx_ref, dlam_ref):
        @pl.when(pl.program_id(1) == 0)
        def _():
            for r in (dwa_ref, dwx_ref, dba_ref, dbx_ref, dlam_ref):
                r[...] = jnp.zeros_like(r)

        _, vjp = jax.vjp(_lru_gates, x_ref[...], wa_ref[0], wx_ref[0], ba_ref[...], bx_ref[...], lam_ref[...])
        dx, dwa, dwx, dba, dbx, dlam = vjp((da_ref[...], du_ref[...]))
        dx_ref[...] = dx
        dwa_ref[0] += dwa
        dwx_ref[0] += dwx
        dba_ref[...] += dba
        dbx_ref[...] += dbx
        dlam_ref[...] += dlam

    return pl.pallas_call(
        body, grid=(NH, t // TT), in_specs=[blk, wsp, wsp, vsp, vsp, vsp, blk, blk],
        out_specs=[blk, wsp, wsp, vsp, vsp, vsp],
        out_shape=[_sds((t, D), f32), _sds((NH, LRU_BLOCK, LRU_BLOCK), f32), _sds((NH, LRU_BLOCK, LRU_BLOCK), f32),
                   _sds((1, D), f32), _sds((1, D), f32), _sds((1, D), f32)],
        compiler_params=_cp(), name=name)(xc, wa, wx, ba, bx, lam, da, du)


_SCAN_SHIFTS = (1, 2, 4, 8, 16, 32, 64, 128)
_SCAN_PAD = 128


def _gelu(y):
    return jax.nn.gelu(y, approximate=True)


def _lru_scan_fwd(a, u, p2, name):
    t = a.shape[0]
    tc = 128
    blk = pl.BlockSpec((TT, tc), lambda j, i: (i, j))

    def body(a_ref, u_ref, y_ref, h_ref, hg_ref, a_s, b_s, carry):
        i = pl.program_id(1)

        @pl.when(i == 0)
        def _():
            carry[...] = jnp.zeros_like(carry)
            a_s[0:_SCAN_PAD, :] = jnp.ones((_SCAN_PAD, tc), f32)
            b_s[0:_SCAN_PAD, :] = jnp.zeros((_SCAN_PAD, tc), f32)

        av, bv = a_ref[...], u_ref[...]
        for s in _SCAN_SHIFTS:
            if s < 8:
                a_s[_SCAN_PAD:, :] = av
                b_s[_SCAN_PAD:, :] = bv
                ash = a_s[pl.ds(_SCAN_PAD - s, TT), :]
                bsh = b_s[pl.ds(_SCAN_PAD - s, TT), :]
            else:
                ash = jnp.concatenate([jnp.ones((s, tc), f32), av[:TT - s]], axis=0)
                bsh = jnp.concatenate([jnp.zeros((s, tc), f32), bv[:TT - s]], axis=0)
            bv = bv + av * bsh
            av = av * ash
        h = bv + av * carry[7:8, :]
        h_ref[...] = h
        hg_ref[...] = (h * _gelu(y_ref[...])).astype(bf16)
        carry[...] = h[TT - 8:, :]

    return pl.pallas_call(
        body, grid=(D // tc, t // TT), in_specs=[blk, blk, blk], out_specs=[blk, blk],
        out_shape=[_sds((t, D), f32), _sds((t, D), bf16)],
        scratch_shapes=[pltpu.VMEM((_SCAN_PAD + TT, tc), f32), pltpu.VMEM((_SCAN_PAD + TT, tc), f32),
                        pltpu.VMEM((8, tc), f32)],
        compiler_params=_cp(), name=name)(a, u, p2)


def _lru_scan_bwd(a, h, p2, dhg, name):
    t = a.shape[0]
    tc = 128
    nt = t // TT
    hb = TT // 8
    rblk = pl.BlockSpec((TT, tc), lambda j, i: (nt - 1 - i, j))

    def body(a_ref, an_ref, h_ref, hp_ref, y_ref, dhg_ref, du_ref, da_ref, dy_ref, a_s, b_s, ap, hp, carry):
        i = pl.program_id(1)

        @pl.when(i == 0)
        def _():
            carry[...] = jnp.zeros_like(carry)
            a_s[TT:, :] = jnp.ones((_SCAN_PAD, tc), f32)
            b_s[TT:, :] = jnp.zeros((_SCAN_PAD, tc), f32)

        ap[0:TT, :] = a_ref[...]
        ap[TT:, :] = jnp.where(i == 0, 0.0, an_ref[...])
        hp[0:8, :] = jnp.where(i == nt - 1, 0.0, hp_ref[...])
        hp[8:, :] = h_ref[...]
        y = y_ref[...]
        gate, gvjp = jax.vjp(_gelu, y)
        dhg_v = dhg_ref[...]
        dy_ref[...] = gvjp(dhg_v * h_ref[...])[0]
        av = ap[pl.ds(1, TT), :]
        bv = dhg_v * gate
        for s in _SCAN_SHIFTS:
            if s < 8:
                a_s[0:TT, :] = av
                b_s[0:TT, :] = bv
                ash = a_s[pl.ds(s, TT), :]
                bsh = b_s[pl.ds(s, TT), :]
            else:
                ash = jnp.concatenate([av[s:], jnp.ones((s, tc), f32)], axis=0)
                bsh = jnp.concatenate([bv[s:], jnp.zeros((s, tc), f32)], axis=0)
            bv = bv + av * bsh
            av = av * ash
        g = bv + av * carry[0:1, :]
        du_ref[...] = g
        da_ref[...] = g * hp[pl.ds(7, TT), :]
        carry[...] = g[0:8, :]

    in_specs = [
        rblk,
        pl.BlockSpec((8, tc), lambda j, i: (jnp.minimum((nt - i) * hb, t // 8 - 1), j)),
        rblk,
        pl.BlockSpec((8, tc), lambda j, i: (jnp.maximum((nt - 1 - i) * hb - 1, 0), j)),
        rblk, rblk,
    ]
    return pl.pallas_call(
        body, grid=(D // tc, nt), in_specs=in_specs, out_specs=[rblk, rblk, rblk],
        out_shape=[_sds((t, D), f32)] * 3,
        scratch_shapes=[pltpu.VMEM((TT + _SCAN_PAD, tc), f32), pltpu.VMEM((TT + _SCAN_PAD, tc), f32),
                        pltpu.VMEM((TT + 8, tc), f32), pltpu.VMEM((TT + 8, tc), f32), pltpu.VMEM((8, tc), f32)],
        compiler_params=_cp(), name=name)(a, a, h, h, p2, dhg)


def _ffn_act_bwd(gc, up, dact, name, tc=256):
    t = gc.shape[0]
    blk = pl.BlockSpec((TT, tc), lambda i, j: (i, j))
    vblk = pl.BlockSpec((TT, tc), lambda i, j: (i, j + D_FF // tc))

    def body(gc_ref, v_ref, da_ref, dgc_ref, dv_ref):
        for r0 in range(0, TT, ROW_CHUNK):
            rows = pl.ds(r0, ROW_CHUNK)
            _, vjp = jax.vjp(lambda g, v: _silu(g) * v, gc_ref[rows, :].astype(f32), v_ref[rows, :])
            dg, dv = vjp(da_ref[rows, :].astype(f32))
            dgc_ref[rows, :] = dg.astype(bf16)
            dv_ref[rows, :] = dv.astype(bf16)

    return pl.pallas_call(
        body, grid=(t // TT, D_FF // tc), in_specs=[blk, vblk, blk], out_specs=[blk, vblk],
        out_shape=[_sds((t, D_FF), bf16), _sds((t, 2 * D_FF), bf16)], compiler_params=_cp(), name=name)(gc, up, dact)


def _lower_bounds_fwd(w):
    def body(w_ref, o0_ref, o1_ref):
        wv = w_ref[...]
        o0, o1 = _lb_rows(wv[0:1, :], wv[1:2, :])
        o0_ref[...] = o0
        o1_ref[...] = o1

    return pl.pallas_call(body, out_shape=[_sds((1, 512), f32)] * 2, name="lower_bounds_fwd")(w)


def _lb_rows(w0, w1):
    m = jnp.maximum(w0, w1)
    e0, e1 = jnp.exp(w0 - m), jnp.exp(w1 - m)
    s = e0 + e1
    p0, p1 = e0 / s, e1 / s
    return p0 - p0, (p0 + p1) - p0


def _lower_bounds_bwd(w, d0, d1):
    def body(w_ref, d0_ref, d1_ref, g0_ref, g1_ref):
        wv = w_ref[...]
        _, vjp = jax.vjp(_lb_rows, wv[0:1, :], wv[1:2, :])
        g0, g1 = vjp((d0_ref[...], d1_ref[...]))
        g0_ref[...] = g0
        g1_ref[...] = g1

    return pl.pallas_call(body, out_shape=[_sds((1, 512), f32)] * 2, name="lower_bounds_bwd")(w, d0, d1)


def _local_step(x, target, wt, pre_layer=None, post_grads=None):
    depth = 4
    res = []
    lb0, lb1 = _lower_bounds_fwd(wt["hgrn_lower_bounds"])
    lbs = [lb0, lb1]
    for layer in range(depth):
        j = layer // 2
        sv = {"x_in": x}
        deps = pre_layer(layer, "mix", x) if pre_layer else ()
        if layer == 0:
            h1 = _rmsnorm_fwd(x, wt["norm_mix"][layer], "rms_fwd", deps)
        sv["h1"] = h1
        if layer % 2 == 0:
            p = _mm_auto(h1, wt["ab_w_in"][j], "nn", f32, "mm_ab_in")
            conv = _conv_fwd(p, 0, 1536, wt["gdn_conv_w"][j], jnp.zeros((1, 1536), f32), "gdn_conv_fwd", tc=768)
            o_a, s_a = _gdn2_fwd(conv, p, wt["alog_v"][j], wt["dtb_v"][j], wt["gdn_norm"][j], "gdn_fwd")
            o_b, s_b = _hgrn_fwd(p, lbs[j], wt["hgrn_norm"][j], "hgrn_fwd")
            o = jnp.concatenate([o_a, o_b], axis=1)
            x, h2 = _mm_res_norm(o, wt["ab_w_out"][j], x, wt["norm_ffn"][layer], "mm_ab_out")
            sv.update(p=p, conv=conv, s_a=s_a, s_b=s_b, o=o)
        else:
            p2 = _mm_nn_slots(h1, wt["c_w_in"][j], f32, "mm_c_in", deps)
            xc = _conv_fwd(p2, 1, D, wt["c_conv_w"][j], wt["c_conv_b"][j], "lru_conv_fwd", tc=D)
            a, u = _lru_gates_fwd(xc, wt["c_gate_a_w"][j], wt["c_gate_x_w"][j], wt["c_gate_a_b"][j],
                                  wt["c_gate_x_b"][j], wt["c_lambda"][j], "lru_gates_fwd")
            h, hg = _lru_scan_fwd(a, u, p2, "lru_scan_fwd")
            x, h2 = _mm_res_norm(hg, wt["c_w_out"][j], x, wt["norm_ffn"][layer], "mm_c_out")
            sv.update(p2=p2, xc=xc, a=a, h=h, hg=hg)
        sv["x_mid"] = x
        deps = pre_layer(layer, "ffn", x) if pre_layer else ()
        up = _mm_nn_slots(h2, wt["ffn_w_up"][layer], bf16, "mm_up", deps)
        gc, act = _conv_fwd(up, 0, D_FF, wt["ffn_conv_w"][layer], wt["ffn_conv_b"][layer], "ffn_conv_fwd",
                            tc=D_FF // 2, val=up, val_col0=2, out_dtype=bf16)
        if layer + 1 < depth:
            x, h1 = _mm_res_norm(act, wt["ffn_w_down"][layer], x, wt["norm_mix"][layer + 1], "mm_down")
        else:
            x = _mm_res_norm(act, wt["ffn_w_down"][layer], x, None, "mm_down_last")
        sv.update(h2=h2, up=up, gc=gc, act=act)
        res.append(sv)

    loss, dx, dxb, d_norm_final = _final_loss(x, wt["norm_final"], target, "final_loss")

    g = {k: [None] * len(v) for k, v in wt.items() if isinstance(v, list)}
    g["norm_final"] = d_norm_final
    d_lbs = [None, None]
    for layer in reversed(range(depth)):
        j = layer // 2
        sv = res[layer]
        dact = _mm_auto(dxb, wt["ffn_w_down"][layer], "nt", bf16, "mm_down_dx")
        g["ffn_w_down"][layer] = _mm_auto(sv["act"], dxb, "tn", bf16, "mm_down_dw")
        dgc, dup = _ffn_act_bwd(sv["gc"], sv["up"], dact, "ffn_act_bwd", tc=D_FF // 2)
        dup, dcw, dcb = _conv_bwd(dgc, sv["up"], 0, wt["ffn_conv_w"][layer], "ffn_conv_bwd", tc=D_FF // 2, into=dup)
        g["ffn_conv_w"][layer] = dcw[:3]
        g["ffn_conv_b"][layer] = dcb
        g["ffn_w_up"][layer] = _mm_tn_slots(sv["h2"], dup, "mm_up_dw")
        deps = post_grads(layer, "ffn", g) if post_grads else ()
        dx, dxb, g["norm_ffn"][layer] = _mm_nt_rmsbwd(dup, wt["ffn_w_up"][layer], sv["x_mid"], wt["norm_ffn"][layer], dx,
                                                      "mm_up_dx", deps)
        if layer % 2 == 0:
            do = _mm_auto(dxb, wt["ab_w_out"][j], "nt", f32, "mm_ab_out_dx")
            g["ab_w_out"][j] = _mm_auto(sv["o"], dxb, "tn", bf16, "mm_ab_out_dw")
            dconv, dz, dba, dal, ddt, dgn = _gdn2_bwd(
                sv["conv"], sv["p"], wt["alog_v"][j], wt["dtb_v"][j], wt["gdn_norm"][j], sv["s_a"], do, "gdn_bwd")
            g["alog_v"][j], g["dtb_v"][j], g["gdn_norm"][j] = dal, ddt, dgn
            dqkv, dcw, _ = _conv_bwd(dconv, sv["p"], 0, wt["gdn_conv_w"][j], "gdn_conv_bwd", tc=768)
            g["gdn_conv_w"][j] = dcw[:4]
            dqfig, dlb, dhn = _hgrn_bwd(sv["p"], lbs[j], wt["hgrn_norm"][j], sv["s_b"], do, "hgrn_bwd")
            g["hgrn_norm"][j] = dhn
            d_lbs[j] = jnp.sum(dlb, axis=0)
            dp = jnp.concatenate([dqkv, dz.astype(bf16), dqfig, dba.astype(bf16)], axis=1)
            g["ab_w_in"][j] = _mm_auto(sv["h1"], dp, "tn", bf16, "mm_ab_in_dw")
            dpre, wpre = dp, wt["ab_w_in"][j]
        else:
            dhg = _mm_auto(dxb, wt["c_w_out"][j], "nt", f32, "mm_c_out_dx")
            g["c_w_out"][j] = _mm_auto(sv["hg"], dxb, "tn", bf16, "mm_c_out_dw")
            du, da, dy = _lru_scan_bwd(sv["a"], sv["h"], sv["p2"], dhg, "lru_scan_bwd")
            dxc, dwa, dwx, dba_, dbx_, dlam = _lru_gates_bwd(
                sv["xc"], wt["c_gate_a_w"][j], wt["c_gate_x_w"][j], wt["c_gate_a_b"][j], wt["c_gate_x_b"][j],
                wt["c_lambda"][j], da, du, "lru_gates_bwd")
            g["c_gate_a_w"][j], g["c_gate_x_w"][j] = dwa, dwx
            g["c_gate_a_b"][j], g["c_gate_x_b"][j], g["c_lambda"][j] = dba_, dbx_, dlam
            dxbr, dcw, dcb = _conv_bwd(dxc, sv["p2"], 1, wt["c_conv_w"][j], "lru_conv_bwd", tc=D)
            g["c_conv_w"][j] = dcw[:4]
            g["c_conv_b"][j] = dcb
            dp2 = jnp.concatenate([dy.astype(bf16), dxbr], axis=1)
            g["c_w_in"][j] = _mm_tn_slots(sv["h1"], dp2, "mm_c_in_dw")
            dpre, wpre = dp2, wt["c_w_in"][j]
        deps = post_grads(layer, "mix", g) if post_grads else ()
        dx, dxb, g["norm_mix"][layer] = _mm_nt_rmsbwd(dpre, wpre, sv["x_in"], wt["norm_mix"][layer], dx, "mm_mix_in_dx", deps)
    g0, g1 = _lower_bounds_bwd(wt["hgrn_lower_bounds"], d_lbs[0], d_lbs[1])
    g["hgrn_lower_bounds"] = jnp.concatenate([g0, g1], axis=0)
    return loss, dx, g


def _ab_in_to_compute(w):
    return jnp.concatenate([w[:, :2048], w[:, 2056:4104], w[:, 2048:2056], jnp.zeros((D, 120), w.dtype)], axis=1)


def _ab_in_from_compute(g):
    return jnp.concatenate([g[:, :2048], g[:, 4096:4104], g[:, 2048:4096]], axis=1)


def _lane_vec(v4):
    return jnp.zeros((1, HEAD), f32).at[0, NH:2 * NH].set(v4)


def _layout_weights(fw):
    wt = {}
    wt["norm_mix"] = [fw["norm_mix"][l][None] for l in range(4)]
    wt["norm_ffn"] = [fw["norm_ffn"][l][None] for l in range(4)]
    wt["norm_final"] = fw["norm_final"][None]
    wt["gdn_conv_w"] = [fw["gdn_conv_w"][j] for j in range(2)]
    wt["alog_v"] = [_lane_vec(fw["gdn_a_log"][j]) for j in range(2)]
    wt["dtb_v"] = [_lane_vec(fw["gdn_dt_bias"][j]) for j in range(2)]
    wt["gdn_norm"] = [fw["gdn_norm"][j][None] for j in range(2)]
    wt["hgrn_lower_bounds"] = fw["hgrn_lower_bounds"]
    wt["hgrn_norm"] = [fw["hgrn_norm"][j][None] for j in range(2)]
    wt["c_conv_w"] = [fw["c_conv_w"][j] for j in range(2)]
    for k in ("c_conv_b", "c_gate_a_b", "c_gate_x_b", "c_lambda"):
        wt[k] = [fw[k][j][None] for j in range(2)]
    wt["ffn_conv_w"] = [fw["ffn_conv_w"][l] for l in range(4)]
    wt["ffn_conv_b"] = [fw["ffn_conv_b"][l][None] for l in range(4)]
    if "ab_w_in" in fw:
        wt["ab_w_in"] = [_ab_in_to_compute(fw["ab_w_in"][j].astype(bf16)) for j in range(2)]
        for k in ("ab_w_out", "c_w_out"):
            wt[k] = [fw[k][j].astype(bf16) for j in range(2)]
        wt["c_w_in"] = [_to_slots(fw["c_w_in"][j].astype(bf16)) for j in range(2)]
        for k in ("c_gate_a_w", "c_gate_x_w"):
            wt[k] = [fw[k][j].astype(f32) for j in range(2)]
        wt["ffn_w_up"] = [_to_slots(fw["ffn_w_up"][l].astype(bf16)) for l in range(4)]
        wt["ffn_w_down"] = [fw["ffn_w_down"][l].astype(bf16) for l in range(4)]
    return wt


SLOT_MAJOR = ("c_w_in", "ffn_w_up")
NATIVE_PERM = {"ab_w_in": (2, 0, 1), "ffn_conv_w": (1, 0, 2)}


def _to_slots(wfull):
    k, n = wfull.shape
    return wfull.reshape(k, N_SLOTS, n // N_SLOTS).transpose(1, 0, 2)


def _layer_full(name, slots):
    kind = BIG[name]
    if name in SLOT_MAJOR:
        return slots
    if kind == "col":
        return _ab_in_to_compute(slots.transpose(1, 0, 2).reshape(slots.shape[1], -1))
    if kind == "row":
        return slots.reshape(-1, slots.shape[2])
    return slots.reshape(4, NH, LRU_BLOCK // 4, LRU_BLOCK).transpose(1, 0, 2, 3).reshape(NH, LRU_BLOCK, LRU_BLOCK).astype(f32)


def _layer_slots(name, g):
    kind = BIG[name]
    if name in SLOT_MAJOR:
        return g
    if kind == "col":
        g = _ab_in_from_compute(g)
        r, cdim = g.shape
        return g.reshape(r, 4, cdim // 4).transpose(1, 0, 2).astype(bf16)
    if kind == "row":
        r, cdim = g.shape
        return g.reshape(4, r // 4, cdim).astype(bf16)
    return g.reshape(NH, 4, LRU_BLOCK // 4, LRU_BLOCK).transpose(1, 0, 2, 3).reshape(4, LRU_BLOCK, LRU_BLOCK).astype(bf16)


def _unlayout_grads(g):
    out = {}
    for k in ("norm_mix", "norm_ffn", "gdn_norm", "hgrn_norm", "c_conv_b", "c_gate_a_b", "c_gate_x_b", "c_lambda",
              "ffn_conv_b"):
        out[k] = jnp.concatenate(g[k], axis=0)
    out["norm_final"] = g["norm_final"][0]
    out["ab_w_in"] = jnp.stack([_ab_in_from_compute(t) for t in g["ab_w_in"]])
    out["gdn_a_log"] = jnp.stack([t[0, NH:2 * NH] for t in g["alog_v"]])
    out["gdn_dt_bias"] = jnp.stack([t[0, NH:2 * NH] for t in g["dtb_v"]])
    out["hgrn_lower_bounds"] = g["hgrn_lower_bounds"]
    for k in ("gdn_conv_w", "ab_w_out", "c_conv_w", "c_gate_a_w", "c_gate_x_w", "c_w_out", "ffn_conv_w", "ffn_w_down"):
        out[k] = jnp.stack(g[k])
    for k in SLOT_MAJOR:
        out[k] = jnp.stack([t.transpose(1, 0, 2).reshape(t.shape[1], -1) for t in g[k]])
    return out


MESH = pl.DeviceIdType.MESH
ANY = pl.BlockSpec(memory_space=pl.ANY)
CHIP_RELATIONS = ((1, 0), (0, 1), (1, 1))
N_CHIPS = 4


def _coords():
    return lax.axis_index("x"), lax.axis_index("y"), lax.axis_index("c")


def _flip(v, f):
    return 1 - v if f else v


def _half_rows(c, a, align):
    return pl.ds(pl.multiple_of(c * (a // 2), align), a // 2)


def _all_gather_chips(shards, name):
    n = len(shards)
    shapes = [s.shape for s in shards]

    def body(*refs):
        ins, outs = refs[:n], refs[n:2 * n]
        send_sems, recv_sems = refs[2 * n:]
        x, y, c = _coords()
        me = 2 * x + y
        sibling = (x, y, 1 - c)
        started = []
        for p in range(n):
            cp = pltpu.make_async_remote_copy(
                src_ref=ins[p], dst_ref=outs[p].at[me],
                send_sem=send_sems.at[p, 6], recv_sem=recv_sems.at[p, 6],
                device_id=sibling, device_id_type=MESH)
            cp.start()
            started.append(cp)
        for p in range(n):
            mine = _half_rows(c, shapes[p][0], 16)
            for r, (fx, fy) in enumerate(CHIP_RELATIONS):
                cp = pltpu.make_async_remote_copy(
                    src_ref=ins[p].at[mine], dst_ref=outs[p].at[me, mine],
                    send_sem=send_sems.at[p, r], recv_sem=recv_sems.at[p, r],
                    device_id=(_flip(x, fx), _flip(y, fy), c), device_id_type=MESH)
                cp.start()
                started.append(cp)
        for r, (fx, fy) in enumerate(CHIP_RELATIONS):
            k = 2 * _flip(x, fx) + _flip(y, fy)
            for p in range(n):
                mine = _half_rows(c, shapes[p][0], 16)
                pltpu.make_async_remote_copy(
                    src_ref=ins[p].at[mine], dst_ref=outs[p].at[k, mine],
                    send_sem=send_sems.at[p, r], recv_sem=recv_sems.at[p, r],
                    device_id=(_flip(x, fx), _flip(y, fy), c), device_id_type=MESH).wait_recv()
                fwd = pltpu.make_async_remote_copy(
                    src_ref=outs[p].at[k, mine], dst_ref=outs[p].at[k, mine],
                    send_sem=send_sems.at[p, 3 + r], recv_sem=recv_sems.at[p, 3 + r],
                    device_id=sibling, device_id_type=MESH)
                fwd.start()
                started.append(fwd)
        for r, (fx, fy) in enumerate(CHIP_RELATIONS):
            k = 2 * _flip(x, fx) + _flip(y, fy)
            for p in range(n):
                theirs = _half_rows(1 - c, shapes[p][0], 16)
                pltpu.make_async_remote_copy(
                    src_ref=outs[p].at[k, theirs], dst_ref=outs[p].at[k, theirs],
                    send_sem=send_sems.at[p, 3 + r], recv_sem=recv_sems.at[p, 3 + r],
                    device_id=sibling, device_id_type=MESH).wait_recv()
        for p in range(n):
            pltpu.make_async_remote_copy(
                src_ref=ins[p], dst_ref=outs[p].at[me],
                send_sem=send_sems.at[p, 6], recv_sem=recv_sems.at[p, 6],
                device_id=sibling, device_id_type=MESH).wait_recv()
        for cp in started:
            cp.wait_send()

    return pl.pallas_call(
        body, in_specs=[ANY] * n, out_specs=[ANY] * n,
        out_shape=[_sds((N_CHIPS,) + s.shape, s.dtype) for s in shards],
        scratch_shapes=[pltpu.SemaphoreType.DMA((n, 7)), pltpu.SemaphoreType.DMA((n, 7))],
        name=name)(*shards)


N_DEV = 8


ROWS_EW = 128


def _adamw_math(w, g, m, v):
    m = ADAM_B1 * m + (1.0 - ADAM_B1) * g
    v = ADAM_B2 * v + (1.0 - ADAM_B2) * (g * g)
    m_hat = m / (1.0 - ADAM_B1 ** ADAM_STEP)
    v_hat = v / (1.0 - ADAM_B2 ** ADAM_STEP)
    delta = -ADAM_LR * (m_hat / (jnp.sqrt(v_hat) + ADAM_EPS) + ADAM_WD * w)
    return delta, m, v


def _adamw_big(w, g, m, v, name):
    nl, r, b = w.shape
    rt = _tile(r, (256, 352, ROWS_EW, 64))
    per = r // rt

    def body(w_ref, g_ref, m_ref, v_ref, go_ref, d_ref, mo_ref, vo_ref):
        gv = g_ref[...]
        d, mn, vn = _adamw_math(w_ref[...], gv, m_ref[...], v_ref[...])
        go_ref[...] = gv
        d_ref[...] = d
        mo_ref[...] = mn
        vo_ref[...] = vn

    blk = pl.BlockSpec((None, rt, b), lambda l, i: (l, i, 0))
    gblk = pl.BlockSpec((rt, b), lambda l, i: (l * per + i, 0))
    return pl.pallas_call(body, grid=(nl, per), in_specs=[blk, gblk, blk, blk], out_specs=[blk] * 4,
                          out_shape=[_sds((nl, r, b), f32)] * 4, compiler_params=_cp(), name=name)(w, g, m, v)


def _adamw_lead(w, g, m, v, name):
    n, r, b = w.shape
    tn = _tile(n, (64, 54, 32, 16, 8, 1))

    def body(w_ref, g_ref, m_ref, v_ref, go_ref, d_ref, mo_ref, vo_ref):
        gv = g_ref[...]
        d, mn, vn = _adamw_math(w_ref[...], gv, m_ref[...], v_ref[...])
        go_ref[...] = gv
        d_ref[...] = d
        mo_ref[...] = mn
        vo_ref[...] = vn

    blk = pl.BlockSpec((tn, r, b), lambda i: (i, 0, 0))
    return pl.pallas_call(body, grid=(n // tn,), in_specs=[blk] * 4, out_specs=[blk] * 4,
                          out_shape=[_sds((n, r, b), f32)] * 4, compiler_params=_cp(), name=name)(w, g, m, v)


def _adamw_small(ws, gs, ms, vs, name):
    n = len(ws)

    def body(*refs):
        w_r, g_r, m_r, v_r = refs[:n], refs[n:2 * n], refs[2 * n:3 * n], refs[3 * n:4 * n]
        go_r, d_r, mo_r, vo_r = refs[4 * n:5 * n], refs[5 * n:6 * n], refs[6 * n:7 * n], refs[7 * n:8 * n]
        for p in range(n):
            gv = g_r[p][...]
            d, mn, vn = _adamw_math(w_r[p][...], gv, m_r[p][...], v_r[p][...])
            go_r[p][...] = gv
            d_r[p][...] = d
            mo_r[p][...] = mn
            vo_r[p][...] = vn

    vm = pl.BlockSpec(memory_space=pltpu.VMEM)
    shp = [_sds(w.shape, f32) for w in ws]
    res = pl.pallas_call(body, in_specs=[vm] * (4 * n), out_specs=[vm] * (4 * n), out_shape=shp * 4,
                         name=name)(*ws, *gs, *ms, *vs)
    return res[:n], res[n:2 * n], res[2 * n:3 * n], res[3 * n:]


HBM = pl.BlockSpec(memory_space=pltpu.HBM)
SEM = pl.BlockSpec(memory_space=pltpu.SEMAPHORE)
EFFECT = pltpu.SideEffectType.DATAFLOW_SIDE_EFFECTING
N_REL = 8


def _rel(r):
    return (r >> 2) & 1, (r >> 1) & 1, r & 1


def _gather_copies(ins, lands, send_sems, recv_sems, shapes):
    x, y, c = _coords()
    me = 2 * x + y
    sends, recvs = [], []
    for p in range(len(ins)):
        for r in range(1, N_REL):
            fx, fy, fc = _rel(r)
            peer = (_flip(x, fx), _flip(y, fy), _flip(c, fc))
            if fx == 0 and fy == 0:
                src, dst, got = ins[p], lands[p].at[me], lands[p].at[me]
            else:
                mine = _half_rows(c, shapes[p][0], 16)
                theirs = _half_rows(_flip(c, fc), shapes[p][0], 16)
                src, dst = ins[p].at[mine], lands[p].at[me, mine]
                got = lands[p].at[2 * peer[0] + peer[1], theirs]
            sems = dict(send_sem=send_sems.at[p * N_REL + r], recv_sem=recv_sems.at[p * N_REL + r], device_id=peer,
                        device_id_type=MESH)
            sends.append(pltpu.make_async_remote_copy(src_ref=src, dst_ref=dst, **sems))
            recvs.append(pltpu.make_async_remote_copy(src_ref=src, dst_ref=got, **sems))
    return sends, recvs


def _share_copies(ins, lands, send_sems, recv_sems, shapes):
    x, y, c = _coords()
    me = 4 * x + 2 * y + c
    sends, recvs = [], []
    for p in range(len(ins)):
        for r in range(1, N_REL):
            fx, fy, fc = _rel(r)
            peer = (_flip(x, fx), _flip(y, fy), _flip(c, fc))
            sems = dict(send_sem=send_sems.at[p * N_REL + r], recv_sem=recv_sems.at[p * N_REL + r], device_id=peer,
                        device_id_type=MESH)
            sends.append(pltpu.make_async_remote_copy(src_ref=ins[p], dst_ref=lands[p].at[me], **sems))
            recvs.append(pltpu.make_async_remote_copy(src_ref=ins[p], dst_ref=lands[p].at[4 * peer[0] + 2 * peer[1] + peer[2]], **sems))
    return sends, recvs


def _sum_shared(own, land, me_arr, name):
    rows = own.shape[0]

    def body(me_ref, own_ref, land_ref, o_ref):
        me = me_ref[0]
        o_ref[...] = jnp.zeros_like(o_ref)
        for d in range(N_DEV):
            @pl.when(me == d)
            def _():
                o_ref[...] += own_ref[...]

            @pl.when(me != d)
            def _():
                o_ref[...] += land_ref[d]

    grid_spec = pltpu.PrefetchScalarGridSpec(
        num_scalar_prefetch=1, grid=(1,),
        in_specs=[pl.BlockSpec((rows, 128), lambda i, me_ref: (0, 0)), pl.BlockSpec((N_DEV, rows, 128), lambda i, me_ref: (0, 0, 0))],
        out_specs=pl.BlockSpec((rows, 128), lambda i, me_ref: (0, 0)))
    return pl.pallas_call(body, grid_spec=grid_spec, out_shape=_sds((rows, 128), f32), compiler_params=_cp(), name=name)(
        me_arr, own, land)


def _scatter_copies(ins, lands, send_sems, recv_sems, shapes):
    x, y, c = _coords()
    sends, recvs = [], []
    for p in range(len(ins)):
        for r in range(1, N_REL):
            fx, fy, fc = _rel(r)
            peer = (_flip(x, fx), _flip(y, fy), _flip(c, fc))
            theirs = _half_rows(peer[2], shapes[p][1], 16)
            sems = dict(send_sem=send_sems.at[p * N_REL + r], recv_sem=recv_sems.at[p * N_REL + r], device_id=peer,
                        device_id_type=MESH)
            cp = pltpu.make_async_remote_copy(src_ref=ins[p].at[2 * peer[0] + peer[1], theirs], dst_ref=lands[p].at[r], **sems)
            sends.append(cp)
            recvs.append(cp)
    return sends, recvs


def _split_start(copies_fn, ins, land_shapes, name, after=()):
    n = len(ins)
    shapes = [a.shape for a in ins]

    def body(*refs):
        in_refs, land_refs = refs[:n], refs[n:2 * n]
        send_sems, recv_sems = refs[2 * n + len(after)], refs[2 * n + len(after) + 1]
        token = refs[-1]
        sends, _ = copies_fn(in_refs, land_refs, send_sems, recv_sems, shapes)
        for cp in sends:
            cp.start()
        token[...] = jnp.zeros_like(token)

    lands = [lax.empty(s.shape, s.dtype) for s in land_shapes]
    na = len(after)
    res = pl.pallas_call(
        body, name=name,
        out_shape=(pltpu.SemaphoreType.DMA((n * N_REL,)), pltpu.SemaphoreType.DMA((n * N_REL,)))
        + tuple(pltpu.HBM(a.shape, a.dtype) for a in ins) + tuple(pltpu.HBM(s.shape, s.dtype) for s in land_shapes)
        + (_sds((8, 128), f32),),
        in_specs=[HBM] * (2 * n) + [pl.BlockSpec(memory_space=pl.ANY)] * na,
        out_specs=(SEM, SEM) + (HBM,) * (2 * n) + (pl.BlockSpec(memory_space=pltpu.VMEM),),
        input_output_aliases={i: 2 + i for i in range(2 * n)},
        compiler_params=pltpu.CompilerParams(has_side_effects=EFFECT),
    )(*[pltpu.with_memory_space_constraint(a, pltpu.HBM) for a in ins],
      *[pltpu.with_memory_space_constraint(a, pltpu.HBM) for a in lands], *after)
    return dict(sems=res[:2], ins=res[2:2 + n], lands=res[2 + n:2 + 2 * n], token=res[-1], shapes=shapes)


def _split_wait(copies_fn, started, after, name):
    n = len(started["ins"])
    shapes = started["shapes"]
    na = len(after)

    def body(*refs):
        in_refs, land_refs = refs[:n], refs[n:2 * n]
        send_sems, recv_sems = refs[2 * n], refs[2 * n + 1]
        sends, recvs = copies_fn(in_refs, land_refs, send_sems, recv_sems, shapes)
        for cp in sends:
            cp.wait_send()
        for cp in recvs:
            cp.wait_recv()

    arrs = list(started["ins"]) + list(started["lands"])
    res = pl.pallas_call(
        body, name=name,
        out_shape=tuple(pltpu.HBM(a.shape, a.dtype) for a in arrs),
        in_specs=[HBM] * (2 * n) + [SEM, SEM] + [pl.BlockSpec(memory_space=pl.ANY)] * na,
        out_specs=(HBM,) * (2 * n), input_output_aliases={i: i for i in range(2 * n)},
        compiler_params=pltpu.CompilerParams(has_side_effects=EFFECT),
    )(*arrs, *started["sems"], *after)
    return res[:n], res[n:]


def _sum_pieces(gb, land, ids, f_prev, blk, nblk, name):
    _, a, b = gb.shape
    rows = _tile(a // 2, (512, 352, 256, ROWS_EW, 64, 32, 16))
    nrt = (a // 2) // rows

    def body(ids_ref, g_ref, l_ref, *rest):
        o_ref = rest[-1]
        acc = g_ref[0].astype(f32)
        for r in range(1, N_REL):
            acc = acc + l_ref[r].astype(f32)
        o_ref[...] = acc

    in_specs = [pl.BlockSpec((1, rows, b), lambda i, ids_ref: (ids_ref[0], ids_ref[1] * nrt + i, 0)),
                pl.BlockSpec((N_REL, rows, b), lambda i, ids_ref: (0, i, 0))]
    args = [ids, gb, land]
    aliases = {}
    if f_prev is not None:
        in_specs.append(pl.BlockSpec(memory_space=pl.ANY))
        args.append(f_prev)
        aliases = {3: 0}
    grid_spec = pltpu.PrefetchScalarGridSpec(
        num_scalar_prefetch=1, grid=(nrt,), in_specs=in_specs,
        out_specs=pl.BlockSpec((rows, b), lambda i, ids_ref: ((2 * blk + ids_ref[1]) * nrt + i, 0)))
    return pl.pallas_call(body, grid_spec=grid_spec, out_shape=_sds((nblk * a, b), f32),
                          input_output_aliases=aliases, compiler_params=_cp(), name=name)(*args)


def _sibling_fill_blocks(fs, nblks, name):
    n = len(fs)
    shapes = [f.shape for f in fs]

    def body(*refs):
        ins, outs = refs[:n], refs[n:2 * n]
        send_sems, recv_sems = refs[2 * n:]
        x, y, c = _coords()
        cps, waits = [], []
        k = 0
        for p in range(n):
            a = shapes[p][0] // nblks[p]
            for bi in range(nblks[p]):
                mine = pl.ds(pl.multiple_of(bi * a + c * (a // 2), 8), a // 2)
                theirs = pl.ds(pl.multiple_of(bi * a + (1 - c) * (a // 2), 8), a // 2)
                sems = dict(send_sem=send_sems.at[k], recv_sem=recv_sems.at[k], device_id=(x, y, 1 - c), device_id_type=MESH)
                cp = pltpu.make_async_remote_copy(src_ref=ins[p].at[mine], dst_ref=outs[p].at[mine], **sems)
                cp.start()
                cps.append(cp)
                waits.append(pltpu.make_async_remote_copy(src_ref=ins[p].at[theirs], dst_ref=outs[p].at[theirs], **sems))
                k += 1
        for wt_ in waits:
            wt_.wait_recv()
        for cp in cps:
            cp.wait_send()

    total = sum(nblks)
    return pl.pallas_call(
        body, in_specs=[ANY] * n, out_specs=[ANY] * n,
        out_shape=[_sds(f.shape, f.dtype) for f in fs],
        input_output_aliases={p: p for p in range(n)},
        scratch_shapes=[pltpu.SemaphoreType.DMA((total,)), pltpu.SemaphoreType.DMA((total,))],
        name=name)(*fs)


WEIGHTS = ["norm_mix", "norm_ffn", "norm_final", "ab_w_in", "gdn_conv_w", "gdn_a_log", "gdn_dt_bias", "gdn_norm",
           "hgrn_lower_bounds", "hgrn_norm", "ab_w_out", "c_w_in", "c_conv_w", "c_conv_b", "c_gate_a_w", "c_gate_a_b",
           "c_gate_x_w", "c_gate_x_b", "c_lambda", "c_w_out", "ffn_w_up", "ffn_conv_w", "ffn_conv_b", "ffn_w_down"]
BIG = {"ab_w_in": "col", "ab_w_out": "row", "c_w_in": "col", "c_gate_a_w": "gate", "c_gate_x_w": "gate",
       "c_w_out": "row", "ffn_w_up": "col", "ffn_w_down": "row"}
SMALL_SHARDED = ["gdn_conv_w", "c_conv_w", "c_conv_b", "c_gate_a_b", "c_gate_x_b", "c_lambda", "ffn_conv_w"]
SMALL = [n for n in WEIGHTS if n not in BIG]
FULL_SHAPES = {
    "norm_mix": (4, 1024), "norm_ffn": (4, 1024), "norm_final": (1024,), "ab_w_in": (2, 1024, 4104),
    "gdn_conv_w": (2, 4, 1536), "gdn_a_log": (2, 4), "gdn_dt_bias": (2, 4), "gdn_norm": (2, 128),
    "hgrn_lower_bounds": (2, 512), "hgrn_norm": (2, 128), "ab_w_out": (2, 1024, 1024), "c_w_in": (2, 1024, 2048),
    "c_conv_w": (2, 4, 1024), "c_conv_b": (2, 1024), "c_gate_a_w": (2, 4, 256, 256), "c_gate_a_b": (2, 1024),
    "c_gate_x_w": (2, 4, 256, 256), "c_gate_x_b": (2, 1024), "c_lambda": (2, 1024), "c_w_out": (2, 1024, 1024),
    "ffn_w_up": (4, 1024, 5632), "ffn_conv_w": (4, 3, 2816), "ffn_conv_b": (4, 2816), "ffn_w_down": (4, 2816, 1024)}


def _pack_rows(arrs, rows):
    parts, used = [], 0
    for a in arrs:
        r = _pack_tile_rows(a.size)
        parts.append(jnp.pad(a.reshape(-1), (0, r * 128 - a.size)).reshape(r, 128))
        used += r
    assert rows >= used and (rows - used) % 8 == 0, (rows, used)
    if rows > used:
        parts.append(jnp.zeros((rows - used, 128), f32))
    return jnp.concatenate(parts, axis=0)


def _pack_tile_rows(size):
    return -(-size // 1024) * 8


def _unpack_rows(pack, shapes):
    out, row = [], 0
    for s in shapes:
        size = 1
        for d in s:
            size *= d
        r = _pack_tile_rows(size)
        out.append(pack[row:row + r].reshape(-1)[:size].reshape(s))
        row += r
    return out


def kernel(x, norm_mix, norm_ffn, norm_final, ab_w_in, gdn_conv_w, gdn_a_log, gdn_dt_bias, gdn_norm, hgrn_lower_bounds, hgrn_norm, ab_w_out, c_w_in, c_conv_w, c_conv_b, c_gate_a_w, c_gate_a_b, c_gate_x_w, c_gate_x_b, c_lambda, c_w_out, ffn_w_up, ffn_conv_w, ffn_conv_b, ffn_w_down, loss_target, m_norm_mix, m_norm_ffn, m_norm_final, m_ab_w_in, m_gdn_conv_w, m_gdn_a_log, m_gdn_dt_bias, m_gdn_norm, m_hgrn_lower_bounds, m_hgrn_norm, m_ab_w_out, m_c_w_in, m_c_conv_w, m_c_conv_b, m_c_gate_a_w, m_c_gate_a_b, m_c_gate_x_w, m_c_gate_x_b, m_c_lambda, m_c_w_out, m_ffn_w_up, m_ffn_conv_w, m_ffn_conv_b, m_ffn_w_down, v_norm_mix, v_norm_ffn, v_norm_final, v_ab_w_in, v_gdn_conv_w, v_gdn_a_log, v_gdn_dt_bias, v_gdn_norm, v_hgrn_lower_bounds, v_hgrn_norm, v_ab_w_out, v_c_w_in, v_c_conv_w, v_c_conv_b, v_c_gate_a_w, v_c_gate_a_b, v_c_gate_x_w, v_c_gate_x_b, v_c_lambda, v_c_w_out, v_ffn_w_up, v_ffn_conv_w, v_ffn_conv_b, v_ffn_w_down):
    w = dict(zip(WEIGHTS, (norm_mix, norm_ffn, norm_final, ab_w_in, gdn_conv_w, gdn_a_log, gdn_dt_bias, gdn_norm, hgrn_lower_bounds, hgrn_norm, ab_w_out, c_w_in, c_conv_w, c_conv_b, c_gate_a_w, c_gate_a_b, c_gate_x_w, c_gate_x_b, c_lambda, c_w_out, ffn_w_up, ffn_conv_w, ffn_conv_b, ffn_w_down)))
    m = dict(zip(WEIGHTS, (m_norm_mix, m_norm_ffn, m_norm_final, m_ab_w_in, m_gdn_conv_w, m_gdn_a_log, m_gdn_dt_bias, m_gdn_norm, m_hgrn_lower_bounds, m_hgrn_norm, m_ab_w_out, m_c_w_in, m_c_conv_w, m_c_conv_b, m_c_gate_a_w, m_c_gate_a_b, m_c_gate_x_w, m_c_gate_x_b, m_c_lambda, m_c_w_out, m_ffn_w_up, m_ffn_conv_w, m_ffn_conv_b, m_ffn_w_down)))
    v = dict(zip(WEIGHTS, (v_norm_mix, v_norm_ffn, v_norm_final, v_ab_w_in, v_gdn_conv_w, v_gdn_a_log, v_gdn_dt_bias, v_gdn_norm, v_hgrn_lower_bounds, v_hgrn_norm, v_ab_w_out, v_c_w_in, v_c_conv_w, v_c_conv_b, v_c_gate_a_w, v_c_gate_a_b, v_c_gate_x_w, v_c_gate_x_b, v_c_lambda, v_c_w_out, v_ffn_w_up, v_ffn_conv_w, v_ffn_conv_b, v_ffn_w_down)))
    big = list(BIG)
    chip = 2 * lax.axis_index("x") + lax.axis_index("y")
    ids = jnp.stack([chip, lax.axis_index("c")]).astype(jnp.int32)

    def layer_parts(l):
        j = l // 2
        if l % 2 == 0:
            mix = [("ab_w_in", j), ("ab_w_out", j)]
        else:
            mix = [("c_w_in", j), ("c_gate_a_w", j), ("c_gate_x_w", j), ("c_w_out", j)]
        return mix, [("ffn_w_up", l), ("ffn_w_down", l)]

    def layer_shard(n, i):
        s = w[n][i]
        return s.reshape(-1, s.shape[-1]).astype(bf16)

    small_shard_shapes = [w[n].shape for n in SMALL_SHARDED]
    small_pack = _pack_rows([w[n] for n in SMALL_SHARDED], 160)
    mix0, ffn0 = layer_parts(0)
    gathered0 = _all_gather_chips([layer_shard(n, i) for n, i in mix0] + [small_pack], "all_gather_mixer0")
    gathers = {}

    def start_gather(key, after):
        shards = [layer_shard(n, i) for n, i in layer_parts(key[0])[0 if key[1] == "mix" else 1]]
        gathers[key] = _split_start(_gather_copies, shards, [_sds((N_CHIPS,) + s.shape, bf16) for s in shards],
                                    "gather_start_%d_%s" % key, after)
        return gathers[key]["token"]

    tok = ()
    for key in ((0, "ffn"), (1, "mix"), (1, "ffn")):
        tok = (start_gather(key, tok),)
    first_tokens = tok
    start_next = {(0, "ffn"): ((2, "mix"), (2, "ffn")), (1, "mix"): ((3, "mix"),), (1, "ffn"): ((3, "ffn"),)}
    fw = {}
    per_chip = [_unpack_rows(gathered0[-1][k], small_shard_shapes) for k in range(N_CHIPS)]
    for i, n in enumerate(SMALL_SHARDED):
        fw[n] = jnp.concatenate([per_chip[k][i] for k in range(N_CHIPS)], axis=-1)
    for n in SMALL:
        if n not in fw:
            fw[n] = w[n]
    wt = _layout_weights(fw)
    for n in big:
        wt[n] = [None] * FULL_SHAPES[n][0]

    def pre_layer(l, part, x_l):
        parts = layer_parts(l)[0 if part == "mix" else 1]
        if l == 0 and part == "mix":
            lands, deps = gathered0[:len(mix0)], first_tokens
        else:
            _, lands = _split_wait(_gather_copies, gathers[(l, part)], [x_l], "gather_wait_%d_%s" % (l, part))
            deps = tuple(lands[:1])
            for key in start_next.get((l, part), ()):
                deps = (start_gather(key, deps),)
            if (l, part) not in start_next:
                deps = ()
        for (n, i), slots in zip(parts, lands):
            wt[n][i] = _layer_full(n, slots)
        return deps

    scatters = []

    def post_grads(l, part, g):
        parts = layer_parts(l)[0 if part == "mix" else 1]
        slots = [_layer_slots(n, g[n][i]) for n, i in parts]
        st = _split_start(_scatter_copies, slots, [_sds((N_REL, s.shape[1] // 2, s.shape[2]), bf16) for s in slots],
                          "scatter_start_%d_%s" % (l, part))
        scatters.append((parts, st, "scatter_wait_%d_%s" % (l, part)))
        return (st["token"],)

    loss, dx, g = _local_step(x[0], loss_target[0], wt, pre_layer, post_grads)
    gf = _unlayout_grads(g)
    loss = lax.psum(loss[0, 0], ("x", "y", "c"))

    out_g, out_d, out_m, out_v = {}, {}, {}, {}
    f = {n: None for n in big}
    last = [n for n, _ in layer_parts(0)[0]]

    def finish(group, after):
        for parts, st, wait_name in group:
            gbs, lands = _split_wait(_scatter_copies, st, after, wait_name)
            for (n, i), gb, land in zip(parts, gbs, lands):
                f[n] = _sum_pieces(gb, land, ids, f[n], i, FULL_SHAPES[n][0], "rs_sum")

    def adamw(names, tag):
        filled = _sibling_fill_blocks([f[n] for n in names], [FULL_SHAPES[n][0] for n in names], "rs_sibling_fill_" + tag)
        for n, g_n in zip(names, filled):
            shp = w[n].shape
            if n in NATIVE_PERM:
                to_native = lambda t: t.transpose(NATIVE_PERM[n])
                res = _adamw_lead(to_native(w[n]), to_native(g_n.reshape(shp)), to_native(m[n]), to_native(v[n]), "adamw_" + n)
                res = [t.transpose(1, 2, 0) for t in res]
            else:
                as3d = lambda t: t.reshape((-1,) + shp[-2:])
                res = _adamw_big(as3d(w[n]), g_n, as3d(m[n]), as3d(v[n]), "adamw_" + n)
            out_g[n], out_d[n], out_m[n], out_v[n] = (t.reshape(shp) for t in res)

    share = _split_start(_share_copies, [_pack_rows([gf[n] for n in SMALL], 688)], [_sds((N_DEV, 688, 128), f32)],
                         "share_small_start", [dx])
    finish(scatters[:-1], [dx, share["token"]])
    adamw([n for n in big if n not in last], "a")
    finish(scatters[-1:], [out_v["ffn_w_up"]])
    adamw(last, "b")

    small_full_shapes = [FULL_SHAPES[n] for n in SMALL]
    (own_pack,), (all_packs,) = _split_wait(_share_copies, share, [out_v[last[0]]], "share_small_wait")
    me_arr = (4 * lax.axis_index("x") + 2 * lax.axis_index("y") + lax.axis_index("c")).astype(jnp.int32).reshape(1)
    small_sum = _sum_shared(own_pack, all_packs, me_arr, "sum_small")
    g_small = dict(zip(SMALL, _unpack_rows(small_sum, small_full_shapes)))
    for n in SMALL_SHARDED:
        width = w[n].shape[-1]
        g_small[n] = lax.dynamic_slice_in_dim(g_small[n], chip * width, width, axis=-1)

    def small2d(n, t):
        if n in NATIVE_PERM:
            t = t.transpose(NATIVE_PERM[n])
        return t.reshape(-1, t.shape[-1])

    sg, sd, sm, sv = _adamw_small([small2d(n, w[n]) for n in SMALL], [small2d(n, g_small[n]) for n in SMALL],
                                  [small2d(n, m[n]) for n in SMALL], [small2d(n, v[n]) for n in SMALL], "adamw_small")
    for i, n in enumerate(SMALL):
        for out, t in zip((out_g, out_d, out_m, out_v), (sg, sd, sm, sv)):
            if n in NATIVE_PERM:
                perm = NATIVE_PERM[n]
                shp_t = tuple(w[n].shape[p] for p in perm)
                out[n] = t[i].reshape(shp_t).transpose(tuple(perm.index(k) for k in range(len(perm))))
            else:
                out[n] = t[i].reshape(w[n].shape)
    return (loss, dx[None], *[out_g[n] for n in WEIGHTS], *[out_d[n] for n in WEIGHTS],
            *[out_m[n] for n in WEIGHTS], *[out_v[n] for n in WEIGHTS])
```

```python
import functools

import jax
import jax.numpy as jnp
from jax import lax
from jax.experimental import pallas as pl
from jax.experimental.pallas import tpu as pltpu

f32 = jnp.float32
bf16 = jnp.bfloat16

D = 1024
EPS = 1e-6
F_FLOOR = 1e-30
GDN_CHUNK = 64
GDN_INTRA_CHUNKS = 1
HGRN_CHUNK = 16
HGRN_STEP = 128
HEAD = 128
NH = 4
LRU_BLOCK = 256
D_FF = 2816
RG_C = 8.0
C_QKV, C_Z, C_QB, C_FB, C_IB, C_GB, C_BA = 0, 1536, 2048, 2560, 3072, 3584, 4096
TT = 256
ROW_CHUNK = 16
N_SLOTS = 4
VMEM_LIMIT = 56 * 1024 * 1024

ADAM_LR, ADAM_B1, ADAM_B2, ADAM_EPS, ADAM_WD, ADAM_STEP = 0.001, 0.9, 0.999, 1e-08, 0.01, 10


def _cp(**kw):
    return pltpu.CompilerParams(vmem_limit_bytes=VMEM_LIMIT, **kw)


def _sds(shape, dtype):
    return jax.ShapeDtypeStruct(shape, dtype)


def _dot(a, b, dims, precision=None):
    return lax.dot_general(a, b, (dims, ((), ())), precision=precision, preferred_element_type=f32)


NN = ((1,), (0,))
NT = ((1,), (1,))
TN = ((0,), (0,))


def _rms(x, g):
    return x * lax.rsqrt(jnp.mean(x * x, axis=-1, keepdims=True) + EPS) * g


def _silu(x):
    return x * jax.nn.sigmoid(x)


def _mm(a, b, mode, tm, tn, out_dtype, name):
    if mode == "nn":
        (m, k), n = a.shape, b.shape[1]
        a_spec = pl.BlockSpec((tm, k), lambda i, j: (i, 0))
        b_spec = pl.BlockSpec((k, tn), lambda i, j: (0, j))
        dims = NN
    elif mode == "nt":
        (m, k), n = a.shape, b.shape[0]
        a_spec = pl.BlockSpec((tm, k), lambda i, j: (i, 0))
        b_spec = pl.BlockSpec((tn, k), lambda i, j: (j, 0))
        dims = NT
    else:
        (k, m), n = a.shape, b.shape[1]
        a_spec = pl.BlockSpec((k, tm), lambda i, j: (0, i))
        b_spec = pl.BlockSpec((k, tn), lambda i, j: (0, j))
        dims = TN
    assert m % tm == 0 and n % tn == 0, (name, m, n, tm, tn)

    def body(a_ref, b_ref, o_ref):
        o_ref[...] = _dot(a_ref[...], b_ref[...], dims).astype(out_dtype)

    return pl.pallas_call(
        body, grid=(m // tm, n // tn), in_specs=[a_spec, b_spec],
        out_specs=pl.BlockSpec((tm, tn), lambda i, j: (i, j)),
        out_shape=_sds((m, n), out_dtype), compiler_params=_cp(), name=name)(a, b)


def _mm_res_norm(a, b, res, gain, name, tm=512):
    (m, k), n = a.shape, b.shape[1]
    tm = min(tm, m)
    assert n == D and m % tm == 0, (name, m, n)

    def body(a_ref, b_ref, r_ref, *rest):
        xv = r_ref[...] + _dot(a_ref[...], b_ref[...], NN)
        if gain is None:
            rest[0][...] = xv
        else:
            g_ref, x_ref, h_ref = rest
            x_ref[...] = xv
            h_ref[...] = _rms(xv, g_ref[...]).astype(bf16)

    row = pl.BlockSpec((tm, D), lambda i: (i, 0))
    in_specs = [pl.BlockSpec((tm, k), lambda i: (i, 0)), pl.BlockSpec((k, D), lambda i: (0, 0)), row]
    args = [a, b, res]
    if gain is None:
        out_specs, out_shape = row, _sds((m, D), f32)
    else:
        in_specs.append(pl.BlockSpec((1, D), lambda i: (0, 0)))
        args.append(gain)
        out_specs, out_shape = [row, row], [_sds((m, D), f32), _sds((m, D), bf16)]
    return pl.pallas_call(body, grid=(m // tm,), in_specs=in_specs, out_specs=out_specs, out_shape=out_shape,
                          compiler_params=_cp(), name=name)(*args)


def _tile(n, cands):
    for c in cands:
        if n % c == 0:
            return c
    raise ValueError(n)


def _mm_auto(a, b, mode, out_dtype, name):
    m = a.shape[1] if mode == "tn" else a.shape[0]
    n = b.shape[0] if mode == "nt" else b.shape[1]
    return _mm(a, b, mode, _tile(m, (512, 256, 128)), _tile(n, (1024, 1408, 512, 384, 256, 128)), out_dtype, name)


def _mm_nn_slots(a, bs, out_dtype, name, after=()):
    (m, k), w = a.shape, bs.shape[2]
    tm = _tile(m, (1024, 512, 256, 128))

    def body(a_ref, b_ref, *rest):
        o_ref = rest[len(after)]
        o_ref[...] = _dot(a_ref[...], b_ref[...], NN).astype(out_dtype)

    return pl.pallas_call(
        body, grid=(m // tm, N_SLOTS),
        in_specs=[pl.BlockSpec((tm, k), lambda i, j: (i, 0)), pl.BlockSpec((None, k, w), lambda i, j: (j, 0, 0))]
        + [pl.BlockSpec(memory_space=pl.ANY)] * len(after),
        out_specs=pl.BlockSpec((tm, w), lambda i, j: (i, j)),
        out_shape=_sds((m, N_SLOTS * w), out_dtype), compiler_params=_cp(), name=name)(a, bs, *after)


def _mm_nt_rmsbwd(a, b, x, gain, dres, name, after=()):
    m, k = a.shape
    slots = b.ndim == 3
    assert b.shape[-2] == D
    tm = _tile(m, (512, 256, 128)) if k < 4096 else _tile(m, (256, 128))

    def body(a_ref, b_ref, x_ref, g_ref, r_ref, *rest):
        dx_ref, dxb_ref, dg_ref = rest[len(after):]
        if slots:
            w = b.shape[2]
            dh = _dot(a_ref[:, 0:w], b_ref[0], NT)
            for s in range(1, N_SLOTS):
                dh = dh + _dot(a_ref[:, s * w:(s + 1) * w], b_ref[s], NT)
        else:
            dh = _dot(a_ref[...], b_ref[...], NT)
        _, vjp = jax.vjp(_rms, x_ref[...], g_ref[...])
        dx, dg = vjp(dh)
        dx = dx + r_ref[...]
        dx_ref[...] = dx
        dxb_ref[...] = dx.astype(bf16)

        @pl.when(pl.program_id(0) == 0)
        def _():
            dg_ref[...] = jnp.zeros_like(dg_ref)

        dg_ref[...] += dg

    row = pl.BlockSpec((tm, D), lambda i: (i, 0))
    vec = pl.BlockSpec((1, D), lambda i: (0, 0))
    b_spec = pl.BlockSpec(b.shape, (lambda i: (0, 0, 0)) if slots else (lambda i: (0, 0)), pipeline_mode=pl.Buffered(1))
    return pl.pallas_call(
        body, grid=(m // tm,),
        in_specs=[pl.BlockSpec((tm, k), lambda i: (i, 0)), b_spec, row, vec, row] + [pl.BlockSpec(memory_space=pl.ANY)] * len(after),
        out_specs=[row, row, vec],
        out_shape=[_sds((m, D), f32), _sds((m, D), bf16), _sds((1, D), f32)],
        compiler_params=_cp(), name=name)(a, b, x, gain, dres, *after)


def _mm_tn_slots(a, b, name):
    (k, m), w = a.shape, b.shape[1] // N_SLOTS
    tm = _tile(m, (512, 256, 128))

    def body(a_ref, b_ref, o_ref):
        o_ref[...] = _dot(a_ref[...], b_ref[...], TN).astype(bf16)

    return pl.pallas_call(
        body, grid=(N_SLOTS, m // tm),
        in_specs=[pl.BlockSpec((k, tm), lambda j, i: (0, i)), pl.BlockSpec((k, w), lambda j, i: (0, j))],
        out_specs=pl.BlockSpec((None, tm, w), lambda j, i: (j, i, 0)),
        out_shape=_sds((N_SLOTS, m, w), bf16), compiler_params=_cp(), name=name)(a, b)


def _rmsnorm_fwd(x, gain, name, after=()):
    t = x.shape[0]

    def body(x_ref, g_ref, *rest):
        h_ref = rest[len(after)]
        h_ref[...] = _rms(x_ref[...], g_ref[...]).astype(bf16)

    return pl.pallas_call(
        body, grid=(t // TT,),
        in_specs=[pl.BlockSpec((TT, D), lambda i: (i, 0)), pl.BlockSpec((1, D), lambda i: (0, 0))]
        + [pl.BlockSpec(memory_space=pl.ANY)] * len(after),
        out_specs=pl.BlockSpec((TT, D), lambda i: (i, 0)),
        out_shape=_sds((t, D), bf16), compiler_params=_cp(), name=name)(x, gain, *after)


def _final_loss(x, gain, target, name):
    t = x.shape[0]

    def loss_fn(xv, gv, tv):
        e = _rms(xv, gv) - tv
        return 0.5 * jnp.sum(jnp.mean(e * e, axis=-1))

    def body(x_ref, g_ref, t_ref, loss_ref, dx_ref, dxb_ref, dg_ref):
        val, (dx, dg) = jax.value_and_grad(loss_fn, argnums=(0, 1))(x_ref[...], g_ref[...], t_ref[...])

        @pl.when(pl.program_id(0) == 0)
        def _():
            dg_ref[...] = jnp.zeros_like(dg_ref)
            loss_ref[...] = jnp.zeros_like(loss_ref)

        dx_ref[...] = dx
        dxb_ref[...] = dx.astype(bf16)
        dg_ref[...] += dg
        loss_ref[...] += jnp.full((1, 128), val, f32)

    row = pl.BlockSpec((TT, D), lambda i: (i, 0))
    vec = pl.BlockSpec((1, D), lambda i: (0, 0))
    return pl.pallas_call(
        body, grid=(t // TT,), in_specs=[row, vec, row],
        out_specs=[pl.BlockSpec((1, 128), lambda i: (0, 0)), row, row, vec],
        out_shape=[_sds((1, 128), f32), _sds((t, D), f32), _sds((t, D), bf16), _sds((1, D), f32)],
        compiler_params=_cp(), name=name)(x, gain, target)


def _halo_rows(dtype):
    return 16 if dtype == bf16 else 8


def _conv_fwd(x, col0, c, w, b, name, tc=256, val=None, val_col0=0, out_dtype=f32):
    t = x.shape[0]
    width = w.shape[0]
    nt = t // TT
    hr = _halo_rows(x.dtype)
    hb = TT // hr

    def body(*refs):
        if val is None:
            x_ref, xh_ref, w_ref, b_ref, o_ref, xp = refs
        else:
            x_ref, xh_ref, w_ref, b_ref, v_ref, o_ref, act_ref, xp = refs
        i = pl.program_id(1)
        xp[0:8, :] = jnp.where(i == 0, 0.0, xh_ref[hr - 8:hr, :].astype(f32))
        xp[8:, :] = x_ref[...].astype(f32)
        bias = b_ref[...]
        taps = [w_ref[k:k + 1, :] for k in range(width)]
        for r0 in range(0, TT, ROW_CHUNK):
            rows = pl.ds(r0, ROW_CHUNK)
            acc = bias + taps[0] * xp[pl.ds(r0 + 8 - (width - 1), ROW_CHUNK), :]
            for k in range(1, width):
                acc = acc + taps[k] * xp[pl.ds(r0 + 8 - (width - 1) + k, ROW_CHUNK), :]
            o_ref[rows, :] = acc.astype(out_dtype)
            if val is not None:
                act_ref[rows, :] = (_silu(acc) * v_ref[rows, :]).astype(bf16)

    in_specs = [
        pl.BlockSpec((TT, tc), lambda j, i: (i, j + col0)),
        pl.BlockSpec((hr, tc), lambda j, i: (jnp.maximum(i * hb - 1, 0), j + col0)),
        pl.BlockSpec((width, tc), lambda j, i: (0, j)),
        pl.BlockSpec((1, tc), lambda j, i: (0, j)),
    ]
    args = [x, x, w, b]
    out_specs = [pl.BlockSpec((TT, tc), lambda j, i: (i, j))]
    out_shape = [_sds((t, c), out_dtype)]
    if val is not None:
        in_specs.append(pl.BlockSpec((TT, tc), lambda j, i: (i, j + val_col0)))
        args.append(val)
        out_specs.append(pl.BlockSpec((TT, tc), lambda j, i: (i, j)))
        out_shape.append(_sds((t, c), bf16))
    res = pl.pallas_call(
        body, grid=(c // tc, nt), in_specs=in_specs, out_specs=out_specs, out_shape=out_shape,
        scratch_shapes=[pltpu.VMEM((TT + 8, tc), f32)], compiler_params=_cp(), name=name)(*args)
    return res[0] if val is None else res


def _conv_bwd(dc, x, col0, w, name, tc=256, dx_dtype=bf16, into=None):
    t, c = dc.shape
    width = w.shape[0]
    nt = t // TT
    hr = _halo_rows(x.dtype)
    hd = _halo_rows(dc.dtype)

    def body(dc_ref, dcn_ref, x_ref, xh_ref, w_ref, *rest):
        dx_ref, dw_ref, db_ref, dcp, xp = rest[(0 if into is None else 1):]
        i = pl.program_id(1)
        dcv = dc_ref[...].astype(f32)
        dcp[0:TT, :] = dcv
        dcp[TT:, :] = jnp.where(i == nt - 1, 0.0, dcn_ref[0:8, :].astype(f32))
        xp[0:8, :] = jnp.where(i == 0, 0.0, xh_ref[hr - 8:hr, :].astype(f32))
        xp[8:, :] = x_ref[...].astype(f32)

        @pl.when(i == 0)
        def _():
            dw_ref[...] = jnp.zeros_like(dw_ref)
            db_ref[...] = jnp.zeros_like(db_ref)

        taps = [w_ref[k:k + 1, :] for k in range(width)]
        fold = lambda p: p[0:8] + p[8:ROW_CHUNK]
        dws = [jnp.zeros((8, tc), f32) for _ in range(width)]
        dbs = jnp.zeros((8, tc), f32)
        for r0 in range(0, TT, ROW_CHUNK):
            dck = dcp[pl.ds(r0, ROW_CHUNK), :]
            acc = taps[0] * dcp[pl.ds(r0 + (width - 1), ROW_CHUNK), :]
            for k in range(1, width):
                acc = acc + taps[k] * dcp[pl.ds(r0 + (width - 1) - k, ROW_CHUNK), :]
            dx_ref[pl.ds(r0, ROW_CHUNK), :] = acc.astype(dx_dtype)
            for k in range(width):
                dws[k] = dws[k] + fold(dck * xp[pl.ds(r0 + 8 - (width - 1) + k, ROW_CHUNK), :])
            dbs = dbs + fold(dck)
        for k in range(width):
            dw_ref[k:k + 1, :] += jnp.sum(dws[k], axis=0, keepdims=True)
        db_ref[...] += jnp.sum(dbs, axis=0, keepdims=True)

    in_specs = [
        pl.BlockSpec((TT, tc), lambda j, i: (i, j)),
        pl.BlockSpec((hd, tc), lambda j, i: (jnp.minimum((i + 1) * (TT // hd), t // hd - 1), j)),
        pl.BlockSpec((TT, tc), lambda j, i: (i, j + col0)),
        pl.BlockSpec((hr, tc), lambda j, i: (jnp.maximum(i * (TT // hr) - 1, 0), j + col0)),
        pl.BlockSpec((width, tc), lambda j, i: (0, j)),
    ]
    out_specs = [
        pl.BlockSpec((TT, tc), lambda j, i: (i, j)),
        pl.BlockSpec((8, tc), lambda j, i: (0, j)),
        pl.BlockSpec((1, tc), lambda j, i: (0, j)),
    ]
    args, aliases, dx_shape = [dc, dc, x, x, w], {}, _sds((t, c), dx_dtype)
    if into is not None:
        in_specs.append(pl.BlockSpec(memory_space=pl.ANY))
        args.append(into)
        aliases, dx_shape = {5: 0}, _sds(into.shape, into.dtype)
    return pl.pallas_call(
        body, grid=(c // tc, nt), in_specs=in_specs, out_specs=out_specs,
        out_shape=[dx_shape, _sds((8, c), f32), _sds((1, c), f32)], input_output_aliases=aliases,
        scratch_shapes=[pltpu.VMEM((TT + 8, tc), f32), pltpu.VMEM((TT + 8, tc), f32)],
        compiler_params=_cp(), name=name)(*args)


def _bdot_impl(a, b, dims):
    return _dot(a.astype(bf16), b.astype(bf16), dims)


@functools.partial(jax.custom_vjp, nondiff_argnums=(2,))
def _bdot(a, b, dims):
    return _bdot_impl(a, b, dims)


def _bdot_fwd(a, b, dims):
    return _bdot_impl(a, b, dims), (a, b)


def _bdot_bwd(dims, res, ct):
    a, b = res
    if dims == NN:
        return _bdot_impl(ct, b, NT), _bdot_impl(a, ct, TN)
    if dims == NT:
        return _bdot_impl(ct, b, NN), _bdot_impl(ct, a, TN)
    return _bdot_impl(b, ct, NT), _bdot_impl(a, ct, NN)


_bdot.defvjp(_bdot_fwd, _bdot_bwd)


def _split2(a):
    hi = a.astype(bf16)
    return hi, (a - hi.astype(f32)).astype(bf16)


def _dot3_impl(a, b, dims):
    a_hi, a_lo = _split2(a)
    b_hi, b_lo = _split2(b)
    return (_dot(a_hi, b_hi, dims) + _dot(a_hi, b_lo, dims)) + _dot(a_lo, b_hi, dims)


def _tril_dot_impl(tril, x, dims):
    t = tril.astype(bf16)
    x1 = x.astype(bf16)
    r1 = x - x1.astype(f32)
    x2 = r1.astype(bf16)
    x3 = (r1 - x2.astype(f32)).astype(bf16)
    return (_dot(t, x3, dims) + _dot(t, x2, dims)) + _dot(t, x1, dims)


@jax.custom_vjp
def _cumsum_rows(tril, x):
    return _tril_dot_impl(tril, x, NN)


def _cumsum_rows_fwd(tril, x):
    return _tril_dot_impl(tril, x, NN), tril


def _cumsum_rows_bwd(tril, ct):
    return jnp.zeros_like(tril), _tril_dot_impl(tril, ct, TN)


_cumsum_rows.defvjp(_cumsum_rows_fwd, _cumsum_rows_bwd)


def _nilpotent_inverse(n):
    hs = range(len(n))
    c = n[0].shape[0]
    eye = (lax.broadcasted_iota(jnp.int32, (c, c), 0) == lax.broadcasted_iota(jnp.int32, (c, c), 1)).astype(f32)
    t = [eye + n[h] for h in hs]
    p = n
    for _ in range(5):
        p = [_dot3_impl(p[h], p[h], NN) for h in hs]
        tp = [_dot3_impl(t[h], p[h], NN) for h in hs]
        t = [t[h] + tp[h] for h in hs]
    return t


@jax.custom_vjp
def _nilpotent_solve(n, rhs):
    t = _nilpotent_inverse(n)
    return [_dot3_impl(t[h], rhs[h], NN) for h in range(len(n))]


def _nilpotent_solve_fwd(n, rhs):
    t = _nilpotent_inverse(n)
    sol = [_dot3_impl(t[h], rhs[h], NN) for h in range(len(n))]
    return sol, (t, sol)


def _nilpotent_solve_bwd(res, ct):
    t, sol = res
    hs = range(len(t))
    d_rhs = [_dot3_impl(t[h], ct[h], TN) for h in hs]
    d_n = [_dot3_impl(d_rhs[h], sol[h], NT) for h in hs]
    return d_n, d_rhs


_nilpotent_solve.defvjp(_nilpotent_solve_fwd, _nilpotent_solve_bwd)


def _gdn_intra4(cq, ck, cv, ba, alog_v, dtb_v):
    c = GDN_CHUNK
    hs = range(NH)
    lane = lax.broadcasted_iota(jnp.int32, (1, HEAD), 1)
    mb = [(lane == h).astype(f32) for h in hs]
    ma = [(lane == h + NH).astype(f32) for h in hs]
    beta = [jax.nn.sigmoid(jnp.sum(ba * mb[h], axis=1, keepdims=True)) for h in hs]
    alpha = [jnp.sum(ba * ma[h], axis=1, keepdims=True) for h in hs]
    alog = [jnp.sum(alog_v * ma[h], axis=1, keepdims=True) for h in hs]
    dtb = [jnp.sum(dtb_v * ma[h], axis=1, keepdims=True) for h in hs]
    g = [-jnp.exp(alog[h]) * jax.nn.softplus(alpha[h] + dtb[h]) for h in hs]
    q = [_silu(cq[h]) for h in hs]
    q = [q[h] * lax.rsqrt(jnp.sum(q[h] * q[h], axis=-1, keepdims=True) + EPS) * (HEAD ** -0.5) for h in hs]
    k = [_silu(ck[h]) for h in hs]
    k = [k[h] * lax.rsqrt(jnp.sum(k[h] * k[h], axis=-1, keepdims=True) + EPS) for h in hs]
    v = [_silu(cv[h]) for h in hs]
    row = lax.broadcasted_iota(jnp.int32, (c, c), 0)
    col = lax.broadcasted_iota(jnp.int32, (c, c), 1)
    causal = row >= col
    tril = causal.astype(f32)
    gc = [_cumsum_rows(tril, jnp.broadcast_to(g[h], (c, HEAD))) for h in hs]
    gcc = [_cumsum_rows(tril, jnp.broadcast_to(g[h], (c, c))) for h in hs]
    decay = [jnp.where(causal, jnp.exp(jnp.where(causal, gcc[h] - gcc[h].T, 0.0)), 0.0) for h in hs]
    kb = [k[h] * beta[h] for h in hs]
    kk = [_bdot(kb[h], k[h], NT) for h in hs]
    p = [-jnp.where(row > col, kk[h] * decay[h], 0.0) for h in hs]
    egc = [jnp.exp(gc[h]) for h in hs]
    sol = _nilpotent_solve(p, [jnp.concatenate([v[h] * beta[h], kb[h] * egc[h]], axis=1) for h in hs])
    qk = [_bdot(q[h], k[h], NT) for h in hs]
    attn = [qk[h] * decay[h] for h in hs]
    rowv = lax.broadcasted_iota(jnp.int32, (c, 1), 0)
    gc_last = [jnp.sum(jnp.where(rowv == c - 1, gc[h], 0.0), axis=0, keepdims=True) for h in hs]
    return ([sol[h][:, :HEAD] for h in hs], [sol[h][:, HEAD:] for h in hs], [q[h] * egc[h] for h in hs],
            [k[h] * jnp.exp(gc_last[h] - gc[h]) for h in hs], attn, [jnp.exp(gc_last[h]) for h in hs])


def _gdn_seq4(u, w, q_dec, k_dec, attn, dl, z, s, gain):
    hs = range(NH)
    ws = [_bdot(w[h], s[h], NN) for h in hs]
    qs = [_bdot(q_dec[h], s[h], NN) for h in hs]
    v_new = [u[h] - ws[h] for h in hs]
    av = [_bdot(attn[h], v_new[h], NN) for h in hs]
    kv = [_bdot(k_dec[h], v_new[h], TN) for h in hs]
    o = [_rms(qs[h] + av[h], gain) * _silu(z[h]) for h in hs]
    return o, [s[h] * dl[h] + kv[h] for h in hs]


def _hsl(h):
    return slice(h * HEAD, (h + 1) * HEAD)


def _gdn2_fwd(conv, p, alog_v, dtb_v, gain, name):
    t = conv.shape[0]
    c = GDN_CHUNK
    nch = t // c
    w512 = NH * HEAD
    wide = lambda off: pl.BlockSpec((c, w512), lambda n: (n, off))
    vec = pl.BlockSpec((1, HEAD), lambda n: (0, 0))
    attn_spec = pl.BlockSpec((1, NH, c, c), lambda n: (n, 0, 0, 0))
    dl_spec = pl.BlockSpec((1, NH, HEAD), lambda n: (n, 0, 0))

    cps = GDN_INTRA_CHUNKS
    iwide = lambda off: pl.BlockSpec((cps * c, w512), lambda n: (n, off))

    def intra(cq, ck, cv, ba, al, dt, u_ref, w_ref, qd_ref, kd_ref, at_ref, dl_ref):
        for ci in range(cps):
            rows = slice(ci * c, (ci + 1) * c)
            u, w, qd, kd, at, dl = _gdn_intra4([cq[rows, _hsl(h)] for h in range(NH)], [ck[rows, _hsl(h)] for h in range(NH)],
                                               [cv[rows, _hsl(h)] for h in range(NH)], ba[rows, :], al[...], dt[...])
            for h in range(NH):
                u_ref[rows, _hsl(h)] = u[h]
                w_ref[rows, _hsl(h)] = w[h]
                qd_ref[rows, _hsl(h)] = qd[h]
                kd_ref[rows, _hsl(h)] = kd[h]
                at_ref[ci, h] = at[h]
                dl_ref[ci, h:h + 1, :] = dl[h]

    u, w, qd, kd, at, dl = pl.pallas_call(
        intra, grid=(nch // cps,),
        in_specs=[iwide(0), iwide(1), iwide(2), pl.BlockSpec((cps * c, HEAD), lambda n: (n, C_BA // HEAD)), vec, vec],
        out_specs=[iwide(0)] * 4 + [pl.BlockSpec((cps, NH, c, c), lambda n: (n, 0, 0, 0)),
                                    pl.BlockSpec((cps, NH, HEAD), lambda n: (n, 0, 0))],
        out_shape=[_sds((t, w512), f32)] * 4 + [_sds((nch, NH, c, c), f32), _sds((nch, NH, HEAD), f32)],
        compiler_params=_cp(), name=name + "_intra")(conv, conv, conv, p, alog_v, dtb_v)

    def seq(u_ref, w_ref, qd_ref, kd_ref, at_ref, dl_ref, z_ref, gn, o_ref, ss_ref, s_scr):
        @pl.when(pl.program_id(0) == 0)
        def _():
            s_scr[...] = jnp.zeros_like(s_scr)

        hs = range(NH)
        s = [s_scr[h] for h in hs]
        for h in hs:
            ss_ref[0, h] = s[h]
        o, s_new = _gdn_seq4([u_ref[:, _hsl(h)] for h in hs], [w_ref[:, _hsl(h)] for h in hs], [qd_ref[:, _hsl(h)] for h in hs],
                             [kd_ref[:, _hsl(h)] for h in hs], [at_ref[0, h] for h in hs], [dl_ref[0, h:h + 1, :] for h in hs],
                             [z_ref[:, _hsl(h)] for h in hs], s, gn[...])
        for h in hs:
            o_ref[:, _hsl(h)] = o[h].astype(bf16)
            s_scr[h] = s_new[h]

    o, states = pl.pallas_call(
        seq, grid=(nch,),
        in_specs=[wide(0)] * 4 + [attn_spec, dl_spec, wide(C_Z // w512), vec],
        out_specs=[wide(0), pl.BlockSpec((1, NH, HEAD, HEAD), lambda n: (n, 0, 0, 0))],
        out_shape=[_sds((t, w512), bf16), _sds((nch, NH, HEAD, HEAD), f32)],
        scratch_shapes=[pltpu.VMEM((NH, HEAD, HEAD), f32)],
        compiler_params=_cp(), name=name + "_seq")(u, w, qd, kd, at, dl, p, gain)
    return o, dict(u=u, w=w, qd=qd, kd=kd, at=at, dl=dl, states=states)


def _gdn2_bwd(conv, p, alog_v, dtb_v, gain, saved, do, name):
    t = conv.shape[0]
    c = GDN_CHUNK
    nch = t // c
    w512 = NH * HEAD
    rwide = lambda off: pl.BlockSpec((c, w512), lambda n: (nch - 1 - n, off))
    rvec = pl.BlockSpec((1, HEAD), lambda n: (0, 0))
    rattn = pl.BlockSpec((1, NH, c, c), lambda n: (nch - 1 - n, 0, 0, 0))
    rdl = pl.BlockSpec((1, NH, HEAD), lambda n: (nch - 1 - n, 0, 0))

    def seq_bwd(u_ref, w_ref, qd_ref, kd_ref, at_ref, dl_ref, z_ref, gn, ss_ref, do_ref,
                du_ref, dw_ref, dqd_ref, dkd_ref, dat_ref, ddl_ref, dz_ref, dgn_ref, ds_scr):
        @pl.when(pl.program_id(0) == 0)
        def _():
            ds_scr[...] = jnp.zeros_like(ds_scr)
            dgn_ref[...] = jnp.zeros_like(dgn_ref)

        hs = range(NH)
        _, vjp = jax.vjp(_gdn_seq4, [u_ref[:, _hsl(h)] for h in hs], [w_ref[:, _hsl(h)] for h in hs],
                         [qd_ref[:, _hsl(h)] for h in hs], [kd_ref[:, _hsl(h)] for h in hs], [at_ref[0, h] for h in hs],
                         [dl_ref[0, h:h + 1, :] for h in hs], [z_ref[:, _hsl(h)] for h in hs], [ss_ref[0, h] for h in hs], gn[...])
        du, dw, dqd, dkd, dat, ddl, dz, ds, dg = vjp(([do_ref[:, _hsl(h)] for h in hs], [ds_scr[h] for h in hs]))
        for h in hs:
            du_ref[:, _hsl(h)] = du[h]
            dw_ref[:, _hsl(h)] = dw[h]
            dqd_ref[:, _hsl(h)] = dqd[h]
            dkd_ref[:, _hsl(h)] = dkd[h]
            dat_ref[0, h] = dat[h]
            ddl_ref[0, h:h + 1, :] = ddl[h]
            dz_ref[:, _hsl(h)] = dz[h]
            ds_scr[h] = ds[h]
        dgn_ref[...] += dg

    du, dw, dqd, dkd, dat, ddl, dz, dgn = pl.pallas_call(
        seq_bwd, grid=(nch,),
        in_specs=[rwide(0)] * 4 + [rattn, rdl, rwide(C_Z // w512), rvec,
                                   pl.BlockSpec((1, NH, HEAD, HEAD), lambda n: (nch - 1 - n, 0, 0, 0)), rwide(0)],
        out_specs=[rwide(0)] * 4 + [rattn, rdl, rwide(0), rvec],
        out_shape=[_sds((t, w512), f32)] * 4 + [_sds((nch, NH, c, c), f32), _sds((nch, NH, HEAD), f32),
                                                _sds((t, w512), f32), _sds((1, HEAD), f32)],
        scratch_shapes=[pltpu.VMEM((NH, HEAD, HEAD), f32)],
        compiler_params=_cp(), name=name + "_seq")(
            saved["u"], saved["w"], saved["qd"], saved["kd"], saved["at"], saved["dl"], p, gain, saved["states"], do)

    cps = GDN_INTRA_CHUNKS
    wide = lambda off: pl.BlockSpec((cps * c, w512), lambda n: (n, off))
    vec = pl.BlockSpec((1, HEAD), lambda n: (0, 0))
    attn_spec = pl.BlockSpec((cps, NH, c, c), lambda n: (n, 0, 0, 0))
    dl_spec = pl.BlockSpec((cps, NH, HEAD), lambda n: (n, 0, 0))

    def intra_bwd(cq, ck, cv, ba, al, dt, du_ref, dw_ref, dqd_ref, dkd_ref, dat_ref, ddl_ref,
                  dc_ref, dba_ref, dal_ref, ddt_ref):
        @pl.when(pl.program_id(0) == 0)
        def _():
            dal_ref[...] = jnp.zeros_like(dal_ref)
            ddt_ref[...] = jnp.zeros_like(ddt_ref)

        dal = jnp.zeros((1, HEAD), f32)
        ddt = jnp.zeros((1, HEAD), f32)
        for ci in range(cps):
            rows = slice(ci * c, (ci + 1) * c)
            hs = range(NH)
            _, vjp = jax.vjp(_gdn_intra4, [cq[rows, _hsl(h)] for h in hs], [ck[rows, _hsl(h)] for h in hs],
                             [cv[rows, _hsl(h)] for h in hs], ba[rows, :], al[...], dt[...])
            g_q, g_k, g_v, g_ba, g_al, g_dt = vjp((
                [du_ref[rows, _hsl(h)] for h in hs], [dw_ref[rows, _hsl(h)] for h in hs], [dqd_ref[rows, _hsl(h)] for h in hs],
                [dkd_ref[rows, _hsl(h)] for h in hs], [dat_ref[ci, h] for h in hs], [ddl_ref[ci, h:h + 1, :] for h in hs]))
            for h in hs:
                dc_ref[rows, _hsl(h)] = g_q[h]
                dc_ref[rows, _hsl(NH + h)] = g_k[h]
                dc_ref[rows, _hsl(2 * NH + h)] = g_v[h]
            dal = dal + g_al
            ddt = ddt + g_dt
            dba_ref[rows, :] = g_ba
        dal_ref[...] += dal
        ddt_ref[...] += ddt

    dconv, dba, dal, ddt = pl.pallas_call(
        intra_bwd, grid=(nch // cps,),
        in_specs=[wide(0), wide(1), wide(2), pl.BlockSpec((cps * c, HEAD), lambda n: (n, C_BA // HEAD)), vec, vec]
        + [wide(0)] * 4 + [attn_spec, dl_spec],
        out_specs=[pl.BlockSpec((cps * c, 3 * w512), lambda n: (n, 0)), pl.BlockSpec((cps * c, HEAD), lambda n: (n, 0)),
                   vec, vec],
        out_shape=[_sds((t, 3 * w512), f32), _sds((t, HEAD), f32), _sds((1, HEAD), f32), _sds((1, HEAD), f32)],
        compiler_params=_cp(), name=name + "_intra")(conv, conv, conv, p, alog_v, dtb_v, du, dw, dqd, dkd, dat, ddl)
    return dconv, dz, dba, dal, ddt, dgn


def _hgrn_intra(qb, fb, ib, lb):
    c = HGRN_CHUNK
    ns = range(len(qb))
    f = [lb + (1.0 - lb) * jax.nn.sigmoid(fb[i]) for i in ns]
    logf = [jnp.log(jnp.maximum(f[i], F_FLOOR)) for i in ns]
    k = [1.0 - f[i] for i in ns]
    q = [_silu(qb[i]) for i in ns]
    row = lax.broadcasted_iota(jnp.int32, (c, c), 0)
    col = lax.broadcasted_iota(jnp.int32, (c, c), 1)
    tril = (row >= col).astype(f32)
    b = [_cumsum_rows(tril, logf[i]) for i in ns]
    ri = lax.broadcasted_iota(jnp.int32, (c, 1), 0)
    o = [jnp.zeros((c, HEAD), f32) for _ in ns]
    for j in range(c):
        ok = ri >= j
        bj = [b[i][j:j + 1, :] for i in ns]
        kj = [k[i][j:j + 1, :] for i in ns]
        vj = [ib[i][j:j + 1, :] for i in ns]
        e = [jnp.where(ok, jnp.exp(jnp.where(ok, b[i] - bj[i], 0.0)), 0.0) for i in ns]
        s = [jnp.sum(q[i] * kj[i] * e[i], axis=1, keepdims=True) for i in ns]
        o = [o[i] + s[i] * vj[i] for i in ns]
    b_last = [jnp.sum(jnp.where(ri == c - 1, b[i], 0.0), axis=0, keepdims=True) for i in ns]
    return (o, [q[i] * jnp.exp(b[i]) for i in ns], [k[i] * jnp.exp(b_last[i] - b[i]) for i in ns],
            [jnp.exp(b_last[i]) for i in ns])


def _hgrn_seq_heads(o_intra, q_dec, k_dec, dl, v, gb, st, gain):
    hs = range(len(st))
    qs = [_bdot(q_dec[h], st[h], NT) for h in hs]
    vk = [_bdot(v[h], k_dec[h], TN) for h in hs]
    o = [_rms(o_intra[h] + qs[h], gain) * _silu(gb[h]) for h in hs]
    return o, [st[h] * dl[h] + vk[h] for h in hs]


def _hgrn_tiles(ref, rows):
    return [[ref[rw, _hsl(h)] for rw in rows] for h in range(NH)]


def _hgrn_fwd(p, lb, gain, name):
    t = p.shape[0]
    r = HGRN_STEP
    ns = t // r
    nsub = r // HGRN_CHUNK
    w512 = NH * HEAD
    blk = lambda off: pl.BlockSpec((r, w512), lambda n: (n, off // w512))

    def body(qb, fb, ib, gb, lb_ref, gn, o_ref, ss_ref, s_scr):
        @pl.when(pl.program_id(0) == 0)
        def _():
            s_scr[...] = jnp.zeros_like(s_scr)

        hs = range(NH)
        rows = [pl.ds(ch * HGRN_CHUNK, HGRN_CHUNK) for ch in range(nsub)]
        q_t, f_t, v_t, g_t = (_hgrn_tiles(ref, rows) for ref in (qb, fb, ib, gb))
        intra = [_hgrn_intra(q_t[h], f_t[h], v_t[h], lb_ref[:, _hsl(h)]) for h in hs]
        st = [s_scr[h] for h in hs]
        for h in hs:
            ss_ref[0, h] = st[h]
        for ch in range(nsub):
            o, st = _hgrn_seq_heads([intra[h][0][ch] for h in hs], [intra[h][1][ch] for h in hs], [intra[h][2][ch] for h in hs],
                                    [intra[h][3][ch] for h in hs], [v_t[h][ch] for h in hs], [g_t[h][ch] for h in hs], st, gn[...])
            for h in hs:
                o_ref[rows[ch], _hsl(h)] = o[h].astype(bf16)
        for h in hs:
            s_scr[h] = st[h]

    return pl.pallas_call(
        body, grid=(ns,),
        in_specs=[blk(C_QB), blk(C_FB), blk(C_IB), blk(C_GB),
                  pl.BlockSpec((1, w512), lambda n: (0, 0)), pl.BlockSpec((1, HEAD), lambda n: (0, 0))],
        out_specs=[pl.BlockSpec((r, w512), lambda n: (n, 0)),
                   pl.BlockSpec((1, NH, HEAD, HEAD), lambda n: (n, 0, 0, 0))],
        out_shape=[_sds((t, w512), bf16), _sds((ns, NH, HEAD, HEAD), f32)],
        scratch_shapes=[pltpu.VMEM((NH, HEAD, HEAD), f32)],
        compiler_params=_cp(), name=name)(p, p, p, p, lb, gain)


def _hgrn_bwd(p, lb, gain, states, do, name):
    t = p.shape[0]
    r = HGRN_STEP
    ns = t // r
    nsub = r // HGRN_CHUNK
    w512 = NH * HEAD
    blk = lambda off: pl.BlockSpec((r, w512), lambda n: (ns - 1 - n, off // w512))

    def body(qb, fb, ib, gb, lb_ref, gn, ss_ref, do_ref, dp_ref, dlb_ref, dgn_ref, ds_scr, st_scr):
        @pl.when(pl.program_id(0) == 0)
        def _():
            ds_scr[...] = jnp.zeros_like(ds_scr)
            dgn_ref[...] = jnp.zeros_like(dgn_ref)

        hs = range(NH)
        gnv = gn[...]
        rows = [pl.ds(ch * HGRN_CHUNK, HGRN_CHUNK) for ch in range(nsub)]
        q_t, f_t, v_t, g_t = (_hgrn_tiles(ref, rows) for ref in (qb, fb, ib, gb))
        intra, vjp_intra = [], []
        for h in hs:
            out, fn = jax.vjp(_hgrn_intra, q_t[h], f_t[h], v_t[h], lb_ref[:, _hsl(h)])
            intra.append(out)
            vjp_intra.append(fn)
        at = lambda part, ch: [intra[h][part][ch] for h in hs]
        st = [ss_ref[0, h] for h in hs]
        for ch in range(nsub):
            for h in hs:
                st_scr[ch * NH + h] = st[h]
            if ch < nsub - 1:
                vk = [_bdot(v_t[h][ch], intra[h][2][ch], TN) for h in hs]
                st = [st[h] * intra[h][3][ch] + vk[h] for h in hs]
        ds = [ds_scr[h] for h in hs]
        dgn = jnp.zeros((1, HEAD), f32)
        cots = [[[None] * nsub for _ in range(4)] for _ in hs]
        d_v = [[None] * nsub for _ in hs]
        for ch in reversed(range(nsub)):
            _, vjp = jax.vjp(_hgrn_seq_heads, at(0, ch), at(1, ch), at(2, ch), at(3, ch), [v_t[h][ch] for h in hs],
                             [g_t[h][ch] for h in hs], [st_scr[ch * NH + h] for h in hs], gnv)
            d_oi, d_qd, d_kd, d_dl, d_vv, g_g, ds, g_gn = vjp(([do_ref[rows[ch], _hsl(h)] for h in hs], ds))
            for h in hs:
                cots[h][0][ch], cots[h][1][ch], cots[h][2][ch], cots[h][3][ch] = d_oi[h], d_qd[h], d_kd[h], d_dl[h]
                d_v[h][ch] = d_vv[h]
                dp_ref[rows[ch], pl.ds(3 * w512 + h * HEAD, HEAD)] = g_g[h].astype(bf16)
            dgn = dgn + g_gn
        for h in hs:
            g_q, g_f, g_i, g_lb = vjp_intra[h](tuple(cots[h]))
            for ch in range(nsub):
                dp_ref[rows[ch], pl.ds(h * HEAD, HEAD)] = g_q[ch].astype(bf16)
                dp_ref[rows[ch], pl.ds(w512 + h * HEAD, HEAD)] = g_f[ch].astype(bf16)
                dp_ref[rows[ch], pl.ds(2 * w512 + h * HEAD, HEAD)] = (g_i[ch] + d_v[h][ch]).astype(bf16)
            dlb_ref[0, :, _hsl(h)] = g_lb
            ds_scr[h] = ds[h]
        dgn_ref[...] += dgn

    return pl.pallas_call(
        body, grid=(ns,),
        in_specs=[blk(C_QB), blk(C_FB), blk(C_IB), blk(C_GB),
                  pl.BlockSpec((1, w512), lambda n: (0, 0)), pl.BlockSpec((1, HEAD), lambda n: (0, 0)),
                  pl.BlockSpec((1, NH, HEAD, HEAD), lambda n: (ns - 1 - n, 0, 0, 0)),
                  pl.BlockSpec((r, w512), lambda n: (ns - 1 - n, 1))],
        out_specs=[pl.BlockSpec((r, 4 * w512), lambda n: (ns - 1 - n, 0)),
                   pl.BlockSpec((1, 1, w512), lambda n: (n, 0, 0)),
                   pl.BlockSpec((1, HEAD), lambda n: (0, 0))],
        out_shape=[_sds((t, 4 * w512), bf16), _sds((ns, 1, w512), f32), _sds((1, HEAD), f32)],
        scratch_shapes=[pltpu.VMEM((NH, HEAD, HEAD), f32), pltpu.VMEM((nsub * NH, HEAD, HEAD), f32)],
        compiler_params=_cp(), name=name)(p, p, p, p, lb, gain, states, do)


def _lru_gates(xb, wa, wx, ba, bx, lam):
    xh = xb.astype(bf16)
    r = jax.nn.sigmoid(_dot(xh, wa.astype(bf16), NN) + ba)
    i = jax.nn.sigmoid(_dot(xh, wx.astype(bf16), NN) + bx)
    log_a = -RG_C * r * jax.nn.softplus(-lam)
    a = jnp.exp(log_a)
    t2 = 2.0 * log_a
    series = -t2 * (1.0 + t2 * (0.5 + t2 * (1.0 / 6.0 + t2 * (1.0 / 24.0))))
    om = jnp.where(t2 > -1e-2, series, 1.0 - jnp.exp(t2))
    u = jnp.sqrt(jnp.maximum(om, 0.0)) * (i * xb)
    return a, u


def _lru_gates_fwd(xc, wa, wx, ba, bx, lam, name):
    t = xc.shape[0]
    blk = pl.BlockSpec((TT, LRU_BLOCK), lambda h, i: (i, h))
    wsp = pl.BlockSpec((1, LRU_BLOCK, LRU_BLOCK), lambda h, i: (h, 0, 0))
    vsp = pl.BlockSpec((1, LRU_BLOCK), lambda h, i: (0, h))

    def body(x_ref, wa_ref, wx_ref, ba_ref, bx_ref, lam_ref, a_ref, u_ref):
        a, u = _lru_gates(x_ref[...], wa_ref[0], wx_ref[0], ba_ref[...], bx_ref[...], lam_ref[...])
        a_ref[...] = a
        u_ref[...] = u

    return pl.pallas_call(
        body, grid=(NH, t // TT), in_specs=[blk, wsp, wsp, vsp, vsp, vsp], out_specs=[blk, blk],
        out_shape=[_sds((t, D), f32)] * 2, compiler_params=_cp(), name=name)(xc, wa, wx, ba, bx, lam)


def _lru_gates_bwd(xc, wa, wx, ba, bx, lam, da, du, name):
    t = xc.shape[0]
    blk = pl.BlockSpec((TT, LRU_BLOCK), lambda h, i: (i, h))
    wsp = pl.BlockSpec((1, LRU_BLOCK, LRU_BLOCK), lambda h, i: (h, 0, 0))
    vsp = pl.BlockSpec((1, LRU_BLOCK), lambda h, i: (0, h))

    def body(x_ref, wa_ref, wx_ref, ba_ref, bx_ref, lam_ref, da_ref, du_ref,
             dx_ref, dwa_ref, dwx_ref, dba_ref, dbx_ref, dlam_ref):
        @pl.when(pl.program_id(1) == 0)
        def _():
            for r in (dwa_ref, dwx_ref, dba_ref, dbx_ref, dlam_ref):
                r[...] = jnp.zeros_like(r)

        _, vjp = jax.vjp(_lru_gates, x_ref[...], wa_ref[0], wx_ref[0], ba_ref[...], bx_ref[...], lam_ref[...])
        dx, dwa, dwx, dba, dbx, dlam = vjp((da_ref[...], du_ref[...]))
        dx_ref[...] = dx
        dwa_ref[0] += dwa
        dwx_ref[0] += dwx
        dba_ref[...] += dba
        dbx_ref[...] += dbx
        dlam_ref[...] += dlam

    return pl.pallas_call(
        body, grid=(NH, t // TT), in_specs=[blk, wsp, wsp, vsp, vsp, vsp, blk, blk],
        out_specs=[blk, wsp, wsp, vsp, vsp, vsp],
        out_shape=[_sds((t, D), f32), _sds((NH, LRU_BLOCK, LRU_BLOCK), f32), _sds((NH, LRU_BLOCK, LRU_BLOCK), f32),
                   _sds((1, D), f32), _sds((1, D), f32), _sds((1, D), f32)],
        compiler_params=_cp(), name=name)(xc, wa, wx, ba, bx, lam, da, du)


_SCAN_SHIFTS = (1, 2, 4, 8, 16, 32, 64, 128)
_SCAN_PAD = 128


def _gelu(y):
    return jax.nn.gelu(y, approximate=True)


def _lru_scan_fwd(a, u, p2, name):
    t = a.shape[0]
    tc = 128
    blk = pl.BlockSpec((TT, tc), lambda j, i: (i, j))

    def body(a_ref, u_ref, y_ref, h_ref, hg_ref, a_s, b_s, carry):
        i = pl.program_id(1)

        @pl.when(i == 0)
        def _():
            carry[...] = jnp.zeros_like(carry)
            a_s[0:_SCAN_PAD, :] = jnp.ones((_SCAN_PAD, tc), f32)
            b_s[0:_SCAN_PAD, :] = jnp.zeros((_SCAN_PAD, tc), f32)

        av, bv = a_ref[...], u_ref[...]
        for s in _SCAN_SHIFTS:
            if s < 8:
                a_s[_SCAN_PAD:, :] = av
                b_s[_SCAN_PAD:, :] = bv
                ash = a_s[pl.ds(_SCAN_PAD - s, TT), :]
                bsh = b_s[pl.ds(_SCAN_PAD - s, TT), :]
            else:
                ash = jnp.concatenate([jnp.ones((s, tc), f32), av[:TT - s]], axis=0)
                bsh = jnp.concatenate([jnp.zeros((s, tc), f32), bv[:TT - s]], axis=0)
            bv = bv + av * bsh
            av = av * ash
        h = bv + av * carry[7:8, :]
        h_ref[...] = h
        hg_ref[...] = (h * _gelu(y_ref[...])).astype(bf16)
        carry[...] = h[TT - 8:, :]

    return pl.pallas_call(
        body, grid=(D // tc, t // TT), in_specs=[blk, blk, blk], out_specs=[blk, blk],
        out_shape=[_sds((t, D), f32), _sds((t, D), bf16)],
        scratch_shapes=[pltpu.VMEM((_SCAN_PAD + TT, tc), f32), pltpu.VMEM((_SCAN_PAD + TT, tc), f32),
                        pltpu.VMEM((8, tc), f32)],
        compiler_params=_cp(), name=name)(a, u, p2)


def _lru_scan_bwd(a, h, p2, dhg, name):
    t = a.shape[0]
    tc = 128
    nt = t // TT
    hb = TT // 8
    rblk = pl.BlockSpec((TT, tc), lambda j, i: (nt - 1 - i, j))

    def body(a_ref, an_ref, h_ref, hp_ref, y_ref, dhg_ref, du_ref, da_ref, dy_ref, a_s, b_s, ap, hp, carry):
        i = pl.program_id(1)

        @pl.when(i == 0)
        def _():
            carry[...] = jnp.zeros_like(carry)
            a_s[TT:, :] = jnp.ones((_SCAN_PAD, tc), f32)
            b_s[TT:, :] = jnp.zeros((_SCAN_PAD, tc), f32)

        ap[0:TT, :] = a_ref[...]
        ap[TT:, :] = jnp.where(i == 0, 0.0, an_ref[...])
        hp[0:8, :] = jnp.where(i == nt - 1, 0.0, hp_ref[...])
        hp[8:, :] = h_ref[...]
        y = y_ref[...]
        gate, gvjp = jax.vjp(_gelu, y)
        dhg_v = dhg_ref[...]
        dy_ref[...] = gvjp(dhg_v * h_ref[...])[0]
        av = ap[pl.ds(1, TT), :]
        bv = dhg_v * gate
        for s in _SCAN_SHIFTS:
            if s < 8:
                a_s[0:TT, :] = av
                b_s[0:TT, :] = bv
                ash = a_s[pl.ds(s, TT), :]
                bsh = b_s[pl.ds(s, TT), :]
            else:
                ash = jnp.concatenate([av[s:], jnp.ones((s, tc), f32)], axis=0)
                bsh = jnp.concatenate([bv[s:], jnp.zeros((s, tc), f32)], axis=0)
            bv = bv + av * bsh
            av = av * ash
        g = bv + av * carry[0:1, :]
        du_ref[...] = g
        da_ref[...] = g * hp[pl.ds(7, TT), :]
        carry[...] = g[0:8, :]

    in_specs = [
        rblk,
        pl.BlockSpec((8, tc), lambda j, i: (jnp.minimum((nt - i) * hb, t // 8 - 1), j)),
        rblk,
        pl.BlockSpec((8, tc), lambda j, i: (jnp.maximum((nt - 1 - i) * hb - 1, 0), j)),
        rblk, rblk,
    ]
    return pl.pallas_call(
        body, grid=(D // tc, nt), in_specs=in_specs, out_specs=[rblk, rblk, rblk],
        out_shape=[_sds((t, D), f32)] * 3,
        scratch_shapes=[pltpu.VMEM((TT + _SCAN_PAD, tc), f32), pltpu.VMEM((TT + _SCAN_PAD, tc), f32),
                        pltpu.VMEM((TT + 8, tc), f32), pltpu.VMEM((TT + 8, tc), f32), pltpu.VMEM((8, tc), f32)],
        compiler_params=_cp(), name=name)(a, a, h, h, p2, dhg)


def _ffn_act_bwd(gc, up, dact, name, tc=256):
    t = gc.shape[0]
    blk = pl.BlockSpec((TT, tc), lambda i, j: (i, j))
    vblk = pl.BlockSpec((TT, tc), lambda i, j: (i, j + D_FF // tc))

    def body(gc_ref, v_ref, da_ref, dgc_ref, dv_ref):
        for r0 in range(0, TT, ROW_CHUNK):
            rows = pl.ds(r0, ROW_CHUNK)
            _, vjp = jax.vjp(lambda g, v: _silu(g) * v, gc_ref[rows, :].astype(f32), v_ref[rows, :])
            dg, dv = vjp(da_ref[rows, :].astype(f32))
            dgc_ref[rows, :] = dg.astype(bf16)
            dv_ref[rows, :] = dv.astype(bf16)

    return pl.pallas_call(
        body, grid=(t // TT, D_FF // tc), in_specs=[blk, vblk, blk], out_specs=[blk, vblk],
        out_shape=[_sds((t, D_FF), bf16), _sds((t, 2 * D_FF), bf16)], compiler_params=_cp(), name=name)(gc, up, dact)


def _lower_bounds_fwd(w):
    def body(w_ref, o0_ref, o1_ref):
        wv = w_ref[...]
        o0, o1 = _lb_rows(wv[0:1, :], wv[1:2, :])
        o0_ref[...] = o0
        o1_ref[...] = o1

    return pl.pallas_call(body, out_shape=[_sds((1, 512), f32)] * 2, name="lower_bounds_fwd")(w)


def _lb_rows(w0, w1):
    m = jnp.maximum(w0, w1)
    e0, e1 = jnp.exp(w0 - m), jnp.exp(w1 - m)
    s = e0 + e1
    p0, p1 = e0 / s, e1 / s
    return p0 - p0, (p0 + p1) - p0


def _lower_bounds_bwd(w, d0, d1):
    def body(w_ref, d0_ref, d1_ref, g0_ref, g1_ref):
        wv = w_ref[...]
        _, vjp = jax.vjp(_lb_rows, wv[0:1, :], wv[1:2, :])
        g0, g1 = vjp((d0_ref[...], d1_ref[...]))
        g0_ref[...] = g0
        g1_ref[...] = g1

    return pl.pallas_call(body, out_shape=[_sds((1, 512), f32)] * 2, name="lower_bounds_bwd")(w, d0, d1)


def _local_step(x, target, wt, pre_layer=None, post_grads=None):
    depth = 4
    res = []
    lb0, lb1 = _lower_bounds_fwd(wt["hgrn_lower_bounds"])
    lbs = [lb0, lb1]
    for layer in range(depth):
        j = layer // 2
        sv = {"x_in": x}
        deps = pre_layer(layer, "mix", x) if pre_layer else ()
        if layer == 0:
            h1 = _rmsnorm_fwd(x, wt["norm_mix"][layer], "rms_fwd", deps)
        sv["h1"] = h1
        if layer % 2 == 0:
            p = _mm_auto(h1, wt["ab_w_in"][j], "nn", f32, "mm_ab_in")
            conv = _conv_fwd(p, 0, 1536, wt["gdn_conv_w"][j], jnp.zeros((1, 1536), f32), "gdn_conv_fwd", tc=768)
            o_a, s_a = _gdn2_fwd(conv, p, wt["alog_v"][j], wt["dtb_v"][j], wt["gdn_norm"][j], "gdn_fwd")
            o_b, s_b = _hgrn_fwd(p, lbs[j], wt["hgrn_norm"][j], "hgrn_fwd")
            o = jnp.concatenate([o_a, o_b], axis=1)
            x, h2 = _mm_res_norm(o, wt["ab_w_out"][j], x, wt["norm_ffn"][layer], "mm_ab_out")
            sv.update(p=p, conv=conv, s_a=s_a, s_b=s_b, o=o)
        else:
            p2 = _mm_nn_slots(h1, wt["c_w_in"][j], f32, "mm_c_in", deps)
            xc = _conv_fwd(p2, 1, D, wt["c_conv_w"][j], wt["c_conv_b"][j], "lru_conv_fwd", tc=D)
            a, u = _lru_gates_fwd(xc, wt["c_gate_a_w"][j], wt["c_gate_x_w"][j], wt["c_gate_a_b"][j],
                                  wt["c_gate_x_b"][j], wt["c_lambda"][j], "lru_gates_fwd")
            h, hg = _lru_scan_fwd(a, u, p2, "lru_scan_fwd")
            x, h2 = _mm_res_norm(hg, wt["c_w_out"][j], x, wt["norm_ffn"][layer], "mm_c_out")
            sv.update(p2=p2, xc=xc, a=a, h=h, hg=hg)
        sv["x_mid"] = x
        deps = pre_layer(layer, "ffn", x) if pre_layer else ()
        up = _mm_nn_slots(h2, wt["ffn_w_up"][layer], bf16, "mm_up", deps)
        gc, act = _conv_fwd(up, 0, D_FF, wt["ffn_conv_w"][layer], wt["ffn_conv_b"][layer], "ffn_conv_fwd",
                            tc=D_FF // 2, val=up, val_col0=2, out_dtype=bf16)
        if layer + 1 < depth:
            x, h1 = _mm_res_norm(act, wt["ffn_w_down"][layer], x, wt["norm_mix"][layer + 1], "mm_down")
        else:
            x = _mm_res_norm(act, wt["ffn_w_down"][layer], x, None, "mm_down_last")
        sv.update(h2=h2, up=up, gc=gc, act=act)
        res.append(sv)

    loss, dx, dxb, d_norm_final = _final_loss(x, wt["norm_final"], target, "final_loss")

    g = {k: [None] * len(v) for k, v in wt.items() if isinstance(v, list)}
    g["norm_final"] = d_norm_final
    d_lbs = [None, None]
    for layer in reversed(range(depth)):
        j = layer // 2
        sv = res[layer]
        dact = _mm_auto(dxb, wt["ffn_w_down"][layer], "nt", bf16, "mm_down_dx")
        g["ffn_w_down"][layer] = _mm_auto(sv["act"], dxb, "tn", bf16, "mm_down_dw")
        dgc, dup = _ffn_act_bwd(sv["gc"], sv["up"], dact, "ffn_act_bwd", tc=D_FF // 2)
        dup, dcw, dcb = _conv_bwd(dgc, sv["up"], 0, wt["ffn_conv_w"][layer], "ffn_conv_bwd", tc=D_FF // 2, into=dup)
        g["ffn_conv_w"][layer] = dcw[:3]
        g["ffn_conv_b"][layer] = dcb
        g["ffn_w_up"][layer] = _mm_tn_slots(sv["h2"], dup, "mm_up_dw")
        deps = post_grads(layer, "ffn", g) if post_grads else ()
        dx, dxb, g["norm_ffn"][layer] = _mm_nt_rmsbwd(dup, wt["ffn_w_up"][layer], sv["x_mid"], wt["norm_ffn"][layer], dx,
                                                      "mm_up_dx", deps)
        if layer % 2 == 0:
            do = _mm_auto(dxb, wt["ab_w_out"][j], "nt", f32, "mm_ab_out_dx")
            g["ab_w_out"][j] = _mm_auto(sv["o"], dxb, "tn", bf16, "mm_ab_out_dw")
            dconv, dz, dba, dal, ddt, dgn = _gdn2_bwd(
                sv["conv"], sv["p"], wt["alog_v"][j], wt["dtb_v"][j], wt["gdn_norm"][j], sv["s_a"], do, "gdn_bwd")
            g["alog_v"][j], g["dtb_v"][j], g["gdn_norm"][j] = dal, ddt, dgn
            dqkv, dcw, _ = _conv_bwd(dconv, sv["p"], 0, wt["gdn_conv_w"][j], "gdn_conv_bwd", tc=768)
            g["gdn_conv_w"][j] = dcw[:4]
            dqfig, dlb, dhn = _hgrn_bwd(sv["p"], lbs[j], wt["hgrn_norm"][j], sv["s_b"], do, "hgrn_bwd")
            g["hgrn_norm"][j] = dhn
            d_lbs[j] = jnp.sum(dlb, axis=0)
            dp = jnp.concatenate([dqkv, dz.astype(bf16), dqfig, dba.astype(bf16)], axis=1)
            g["ab_w_in"][j] = _mm_auto(sv["h1"], dp, "tn", bf16, "mm_ab_in_dw")
            dpre, wpre = dp, wt["ab_w_in"][j]
        else:
            dhg = _mm_auto(dxb, wt["c_w_out"][j], "nt", f32, "mm_c_out_dx")
            g["c_w_out"][j] = _mm_auto(sv["hg"], dxb, "tn", bf16, "mm_c_out_dw")
            du, da, dy = _lru_scan_bwd(sv["a"], sv["h"], sv["p2"], dhg, "lru_scan_bwd")
            dxc, dwa, dwx, dba_, dbx_, dlam = _lru_gates_bwd(
                sv["xc"], wt["c_gate_a_w"][j], wt["c_gate_x_w"][j], wt["c_gate_a_b"][j], wt["c_gate_x_b"][j],
                wt["c_lambda"][j], da, du, "lru_gates_bwd")
            g["c_gate_a_w"][j], g["c_gate_x_w"][j] = dwa, dwx
            g["c_gate_a_b"][j], g["c_gate_x_b"][j], g["c_lambda"][j] = dba_, dbx_, dlam
            dxbr, dcw, dcb = _conv_bwd(dxc, sv["p2"], 1, wt["c_conv_w"][j], "lru_conv_bwd", tc=D)
            g["c_conv_w"][j] = dcw[:4]
            g["c_conv_b"][j] = dcb
            dp2 = jnp.concatenate([dy.astype(bf16), dxbr], axis=1)
            g["c_w_in"][j] = _mm_tn_slots(sv["h1"], dp2, "mm_c_in_dw")
            dpre, wpre = dp2, wt["c_w_in"][j]
        deps = post_grads(layer, "mix", g) if post_grads else ()
        dx, dxb, g["norm_mix"][layer] = _mm_nt_rmsbwd(dpre, wpre, sv["x_in"], wt["norm_mix"][layer], dx, "mm_mix_in_dx", deps)
    g0, g1 = _lower_bounds_bwd(wt["hgrn_lower_bounds"], d_lbs[0], d_lbs[1])
    g["hgrn_lower_bounds"] = jnp.concatenate([g0, g1], axis=0)
    return loss, dx, g


def _ab_in_to_compute(w):
    return jnp.concatenate([w[:, :2048], w[:, 2056:4104], w[:, 2048:2056], jnp.zeros((D, 120), w.dtype)], axis=1)


def _ab_in_from_compute(g):
    return jnp.concatenate([g[:, :2048], g[:, 4096:4104], g[:, 2048:4096]], axis=1)


def _lane_vec(v4):
    return jnp.zeros((1, HEAD), f32).at[0, NH:2 * NH].set(v4)


def _layout_weights(fw):
    wt = {}
    wt["norm_mix"] = [fw["norm_mix"][l][None] for l in range(4)]
    wt["norm_ffn"] = [fw["norm_ffn"][l][None] for l in range(4)]
    wt["norm_final"] = fw["norm_final"][None]
    wt["gdn_conv_w"] = [fw["gdn_conv_w"][j] for j in range(2)]
    wt["alog_v"] = [_lane_vec(fw["gdn_a_log"][j]) for j in range(2)]
    wt["dtb_v"] = [_lane_vec(fw["gdn_dt_bias"][j]) for j in range(2)]
    wt["gdn_norm"] = [fw["gdn_norm"][j][None] for j in range(2)]
    wt["hgrn_lower_bounds"] = fw["hgrn_lower_bounds"]
    wt["hgrn_norm"] = [fw["hgrn_norm"][j][None] for j in range(2)]
    wt["c_conv_w"] = [fw["c_conv_w"][j] for j in range(2)]
    for k in ("c_conv_b", "c_gate_a_b", "c_gate_x_b", "c_lambda"):
        wt[k] = [fw[k][j][None] for j in range(2)]
    wt["ffn_conv_w"] = [fw["ffn_conv_w"][l] for l in range(4)]
    wt["ffn_conv_b"] = [fw["ffn_conv_b"][l][None] for l in range(4)]
    if "ab_w_in" in fw:
        wt["ab_w_in"] = [_ab_in_to_compute(fw["ab_w_in"][j].astype(bf16)) for j in range(2)]
        for k in ("ab_w_out", "c_w_out"):
            wt[k] = [fw[k][j].astype(bf16) for j in range(2)]
        wt["c_w_in"] = [_to_slots(fw["c_w_in"][j].astype(bf16)) for j in range(2)]
        for k in ("c_gate_a_w", "c_gate_x_w"):
            wt[k] = [fw[k][j].astype(f32) for j in range(2)]
        wt["ffn_w_up"] = [_to_slots(fw["ffn_w_up"][l].astype(bf16)) for l in range(4)]
        wt["ffn_w_down"] = [fw["ffn_w_down"][l].astype(bf16) for l in range(4)]
    return wt


SLOT_MAJOR = ("c_w_in", "ffn_w_up")
NATIVE_PERM = {"ab_w_in": (2, 0, 1), "ffn_conv_w": (1, 0, 2)}


def _to_slots(wfull):
    k, n = wfull.shape
    return wfull.reshape(k, N_SLOTS, n // N_SLOTS).transpose(1, 0, 2)


def _layer_full(name, slots):
    kind = BIG[name]
    if name in SLOT_MAJOR:
        return slots
    if kind == "col":
        return _ab_in_to_compute(slots.transpose(1, 0, 2).reshape(slots.shape[1], -1))
    if kind == "row":
        return slots.reshape(-1, slots.shape[2])
    return slots.reshape(4, NH, LRU_BLOCK // 4, LRU_BLOCK).transpose(1, 0, 2, 3).reshape(NH, LRU_BLOCK, LRU_BLOCK).astype(f32)


def _layer_slots(name, g):
    kind = BIG[name]
    if name in SLOT_MAJOR:
        return g
    if kind == "col":
        g = _ab_in_from_compute(g)
        r, cdim = g.shape
        return g.reshape(r, 4, cdim // 4).transpose(1, 0, 2).astype(bf16)
    if kind == "row":
        r, cdim = g.shape
        return g.reshape(4, r // 4, cdim).astype(bf16)
    return g.reshape(NH, 4, LRU_BLOCK // 4, LRU_BLOCK).transpose(1, 0, 2, 3).reshape(4, LRU_BLOCK, LRU_BLOCK).astype(bf16)


def _unlayout_grads(g):
    out = {}
    for k in ("norm_mix", "norm_ffn", "gdn_norm", "hgrn_norm", "c_conv_b", "c_gate_a_b", "c_gate_x_b", "c_lambda",
              "ffn_conv_b"):
        out[k] = jnp.concatenate(g[k], axis=0)
    out["norm_final"] = g["norm_final"][0]
    out["ab_w_in"] = jnp.stack([_ab_in_from_compute(t) for t in g["ab_w_in"]])
    out["gdn_a_log"] = jnp.stack([t[0, NH:2 * NH] for t in g["alog_v"]])
    out["gdn_dt_bias"] = jnp.stack([t[0, NH:2 * NH] for t in g["dtb_v"]])
    out["hgrn_lower_bounds"] = g["hgrn_lower_bounds"]
    for k in ("gdn_conv_w", "ab_w_out", "c_conv_w", "c_gate_a_w", "c_gate_x_w", "c_w_out", "ffn_conv_w", "ffn_w_down"):
        out[k] = jnp.stack(g[k])
    for k in SLOT_MAJOR:
        out[k] = jnp.stack([t.transpose(1, 0, 2).reshape(t.shape[1], -1) for t in g[k]])
    return out


MESH = pl.DeviceIdType.MESH
ANY = pl.BlockSpec(memory_space=pl.ANY)
CHIP_RELATIONS = ((1, 0), (0, 1), (1, 1))
N_CHIPS = 4


def _coords():
    return lax.axis_index("x"), lax.axis_index("y"), lax.axis_index("c")


def _flip(v, f):
    return 1 - v if f else v


def _half_rows(c, a, align):
    return pl.ds(pl.multiple_of(c * (a // 2), align), a // 2)


def _all_gather_chips(shards, name):
    n = len(shards)
    shapes = [s.shape for s in shards]

    def body(*refs):
        ins, outs = refs[:n], refs[n:2 * n]
        send_sems, recv_sems = refs[2 * n:]
        x, y, c = _coords()
        me = 2 * x + y
        sibling = (x, y, 1 - c)
        started = []
        for p in range(n):
            cp = pltpu.make_async_remote_copy(
                src_ref=ins[p], dst_ref=outs[p].at[me],
                send_sem=send_sems.at[p, 6], recv_sem=recv_sems.at[p, 6],
                device_id=sibling, device_id_type=MESH)
            cp.start()
            started.append(cp)
        for p in range(n):
            mine = _half_rows(c, shapes[p][0], 16)
            for r, (fx, fy) in enumerate(CHIP_RELATIONS):
                cp = pltpu.make_async_remote_copy(
                    src_ref=ins[p].at[mine], dst_ref=outs[p].at[me, mine],
                    send_sem=send_sems.at[p, r], recv_sem=recv_sems.at[p, r],
                    device_id=(_flip(x, fx), _flip(y, fy), c), device_id_type=MESH)
                cp.start()
                started.append(cp)
        for r, (fx, fy) in enumerate(CHIP_RELATIONS):
            k = 2 * _flip(x, fx) + _flip(y, fy)
            for p in range(n):
                mine = _half_rows(c, shapes[p][0], 16)
                pltpu.make_async_remote_copy(
                    src_ref=ins[p].at[mine], dst_ref=outs[p].at[k, mine],
                    send_sem=send_sems.at[p, r], recv_sem=recv_sems.at[p, r],
                    device_id=(_flip(x, fx), _flip(y, fy), c), device_id_type=MESH).wait_recv()
                fwd = pltpu.make_async_remote_copy(
                    src_ref=outs[p].at[k, mine], dst_ref=outs[p].at[k, mine],
                    send_sem=send_sems.at[p, 3 + r], recv_sem=recv_sems.at[p, 3 + r],
                    device_id=sibling, device_id_type=MESH)
                fwd.start()
                started.append(fwd)
        for r, (fx, fy) in enumerate(CHIP_RELATIONS):
            k = 2 * _flip(x, fx) + _flip(y, fy)
            for p in range(n):
                theirs = _half_rows(1 - c, shapes[p][0], 16)
                pltpu.make_async_remote_copy(
                    src_ref=outs[p].at[k, theirs], dst_ref=outs[p].at[k, theirs],
                    send_sem=send_sems.at[p, 3 + r], recv_sem=recv_sems.at[p, 3 + r],
                    device_id=sibling, device_id_type=MESH).wait_recv()
        for p in range(n):
            pltpu.make_async_remote_copy(
                src_ref=ins[p], dst_ref=outs[p].at[me],
                send_sem=send_sems.at[p, 6], recv_sem=recv_sems.at[p, 6],
                device_id=sibling, device_id_type=MESH).wait_recv()
        for cp in started:
            cp.wait_send()

    return pl.pallas_call(
        body, in_specs=[ANY] * n, out_specs=[ANY] * n,
        out_shape=[_sds((N_CHIPS,) + s.shape, s.dtype) for s in shards],
        scratch_shapes=[pltpu.SemaphoreType.DMA((n, 7)), pltpu.SemaphoreType.DMA((n, 7))],
        name=name)(*shards)


N_DEV = 8


ROWS_EW = 128


def _adamw_math(w, g, m, v):
    m = ADAM_B1 * m + (1.0 - ADAM_B1) * g
    v = ADAM_B2 * v + (1.0 - ADAM_B2) * (g * g)
    m_hat = m / (1.0 - ADAM_B1 ** ADAM_STEP)
    v_hat = v / (1.0 - ADAM_B2 ** ADAM_STEP)
    delta = -ADAM_LR * (m_hat / (jnp.sqrt(v_hat) + ADAM_EPS) + ADAM_WD * w)
    return delta, m, v


def _adamw_big(w, g, m, v, name):
    nl, r, b = w.shape
    rt = _tile(r, (256, 352, ROWS_EW, 64))
    per = r // rt

    def body(w_ref, g_ref, m_ref, v_ref, go_ref, d_ref, mo_ref, vo_ref):
        gv = g_ref[...]
        d, mn, vn = _adamw_math(w_ref[...], gv, m_ref[...], v_ref[...])
        go_ref[...] = gv
        d_ref[...] = d
        mo_ref[...] = mn
        vo_ref[...] = vn

    blk = pl.BlockSpec((None, rt, b), lambda l, i: (l, i, 0))
    gblk = pl.BlockSpec((rt, b), lambda l, i: (l * per + i, 0))
    return pl.pallas_call(body, grid=(nl, per), in_specs=[blk, gblk, blk, blk], out_specs=[blk] * 4,
                          out_shape=[_sds((nl, r, b), f32)] * 4, compiler_params=_cp(), name=name)(w, g, m, v)


def _adamw_lead(w, g, m, v, name):
    n, r, b = w.shape
    tn = _tile(n, (64, 54, 32, 16, 8, 1))

    def body(w_ref, g_ref, m_ref, v_ref, go_ref, d_ref, mo_ref, vo_ref):
        gv = g_ref[...]
        d, mn, vn = _adamw_math(w_ref[...], gv, m_ref[...], v_ref[...])
        go_ref[...] = gv
        d_ref[...] = d
        mo_ref[...] = mn
        vo_ref[...] = vn

    blk = pl.BlockSpec((tn, r, b), lambda i: (i, 0, 0))
    return pl.pallas_call(body, grid=(n // tn,), in_specs=[blk] * 4, out_specs=[blk] * 4,
                          out_shape=[_sds((n, r, b), f32)] * 4, compiler_params=_cp(), name=name)(w, g, m, v)


def _adamw_small(ws, gs, ms, vs, name):
    n = len(ws)

    def body(*refs):
        w_r, g_r, m_r, v_r = refs[:n], refs[n:2 * n], refs[2 * n:3 * n], refs[3 * n:4 * n]
        go_r, d_r, mo_r, vo_r = refs[4 * n:5 * n], refs[5 * n:6 * n], refs[6 * n:7 * n], refs[7 * n:8 * n]
        for p in range(n):
            gv = g_r[p][...]
            d, mn, vn = _adamw_math(w_r[p][...], gv, m_r[p][...], v_r[p][...])
            go_r[p][...] = gv
            d_r[p][...] = d
            mo_r[p][...] = mn
            vo_r[p][...] = vn

    vm = pl.BlockSpec(memory_space=pltpu.VMEM)
    shp = [_sds(w.shape, f32) for w in ws]
    res = pl.pallas_call(body, in_specs=[vm] * (4 * n), out_specs=[vm] * (4 * n), out_shape=shp * 4,
                         name=name)(*ws, *gs, *ms, *vs)
    return res[:n], res[n:2 * n], res[2 * n:3 * n], res[3 * n:]


HBM = pl.BlockSpec(memory_space=pltpu.HBM)
SEM = pl.BlockSpec(memory_space=pltpu.SEMAPHORE)
EFFECT = pltpu.SideEffectType.DATAFLOW_SIDE_EFFECTING
N_REL = 8


def _rel(r):
    return (r >> 2) & 1, (r >> 1) & 1, r & 1


def _gather_copies(ins, lands, send_sems, recv_sems, shapes):
    x, y, c = _coords()
    me = 2 * x + y
    sends, recvs = [], []
    for p in range(len(ins)):
        for r in range(1, N_REL):
            fx, fy, fc = _rel(r)
            peer = (_flip(x, fx), _flip(y, fy), _flip(c, fc))
            if fx == 0 and fy == 0:
                src, dst, got = ins[p], lands[p].at[me], lands[p].at[me]
            else:
                mine = _half_rows(c, shapes[p][0], 16)
                theirs = _half_rows(_flip(c, fc), shapes[p][0], 16)
                src, dst = ins[p].at[mine], lands[p].at[me, mine]
                got = lands[p].at[2 * peer[0] + peer[1], theirs]
            sems = dict(send_sem=send_sems.at[p * N_REL + r], recv_sem=recv_sems.at[p * N_REL + r], device_id=peer,
                        device_id_type=MESH)
            sends.append(pltpu.make_async_remote_copy(src_ref=src, dst_ref=dst, **sems))
            recvs.append(pltpu.make_async_remote_copy(src_ref=src, dst_ref=got, **sems))
    return sends, recvs


def _share_copies(ins, lands, send_sems, recv_sems, shapes):
    x, y, c = _coords()
    me = 4 * x + 2 * y + c
    sends, recvs = [], []
    for p in range(len(ins)):
        for r in range(1, N_REL):
            fx, fy, fc = _rel(r)
            peer = (_flip(x, fx), _flip(y, fy), _flip(c, fc))
            sems = dict(send_sem=send_sems.at[p * N_REL + r], recv_sem=recv_sems.at[p * N_REL + r], device_id=peer,
                        device_id_type=MESH)
            sends.append(pltpu.make_async_remote_copy(src_ref=ins[p], dst_ref=lands[p].at[me], **sems))
            recvs.append(pltpu.make_async_remote_copy(src_ref=ins[p], dst_ref=lands[p].at[4 * peer[0] + 2 * peer[1] + peer[2]], **sems))
    return sends, recvs


def _sum_shared(own, land, me_arr, name):
    rows = own.shape[0]

    def body(me_ref, own_ref, land_ref, o_ref):
        me = me_ref[0]
        o_ref[...] = jnp.zeros_like(o_ref)
        for d in range(N_DEV):
            @pl.when(me == d)
            def _():
                o_ref[...] += own_ref[...]

            @pl.when(me != d)
            def _():
                o_ref[...] += land_ref[d]

    grid_spec = pltpu.PrefetchScalarGridSpec(
        num_scalar_prefetch=1, grid=(1,),
        in_specs=[pl.BlockSpec((rows, 128), lambda i, me_ref: (0, 0)), pl.BlockSpec((N_DEV, rows, 128), lambda i, me_ref: (0, 0, 0))],
        out_specs=pl.BlockSpec((rows, 128), lambda i, me_ref: (0, 0)))
    return pl.pallas_call(body, grid_spec=grid_spec, out_shape=_sds((rows, 128), f32), compiler_params=_cp(), name=name)(
        me_arr, own, land)


def _scatter_copies(ins, lands, send_sems, recv_sems, shapes):
    x, y, c = _coords()
    sends, recvs = [], []
    for p in range(len(ins)):
        for r in range(1, N_REL):
            fx, fy, fc = _rel(r)
            peer = (_flip(x, fx), _flip(y, fy), _flip(c, fc))
            theirs = _half_rows(peer[2], shapes[p][1], 16)
            sems = dict(send_sem=send_sems.at[p * N_REL + r], recv_sem=recv_sems.at[p * N_REL + r], device_id=peer,
                        device_id_type=MESH)
            cp = pltpu.make_async_remote_copy(src_ref=ins[p].at[2 * peer[0] + peer[1], theirs], dst_ref=lands[p].at[r], **sems)
            sends.append(cp)
            recvs.append(cp)
    return sends, recvs


def _split_start(copies_fn, ins, land_shapes, name, after=()):
    n = len(ins)
    shapes = [a.shape for a in ins]

    def body(*refs):
        in_refs, land_refs = refs[:n], refs[n:2 * n]
        send_sems, recv_sems = refs[2 * n + len(after)], refs[2 * n + len(after) + 1]
        token = refs[-1]
        sends, _ = copies_fn(in_refs, land_refs, send_sems, recv_sems, shapes)
        for cp in sends:
            cp.start()
        token[...] = jnp.zeros_like(token)

    lands = [lax.empty(s.shape, s.dtype) for s in land_shapes]
    na = len(after)
    res = pl.pallas_call(
        body, name=name,
        out_shape=(pltpu.SemaphoreType.DMA((n * N_REL,)), pltpu.SemaphoreType.DMA((n * N_REL,)))
        + tuple(pltpu.HBM(a.shape, a.dtype) for a in ins) + tuple(pltpu.HBM(s.shape, s.dtype) for s in land_shapes)
        + (_sds((8, 128), f32),),
        in_specs=[HBM] * (2 * n) + [pl.BlockSpec(memory_space=pl.ANY)] * na,
        out_specs=(SEM, SEM) + (HBM,) * (2 * n) + (pl.BlockSpec(memory_space=pltpu.VMEM),),
        input_output_aliases={i: 2 + i for i in range(2 * n)},
        compiler_params=pltpu.CompilerParams(has_side_effects=EFFECT),
    )(*[pltpu.with_memory_space_constraint(a, pltpu.HBM) for a in ins],
      *[pltpu.with_memory_space_constraint(a, pltpu.HBM) for a in lands], *after)
    return dict(sems=res[:2], ins=res[2:2 + n], lands=res[2 + n:2 + 2 * n], token=res[-1], shapes=shapes)


def _split_wait(copies_fn, started, after, name):
    n = len(started["ins"])
    shapes = started["shapes"]
    na = len(after)

    def body(*refs):
        in_refs, land_refs = refs[:n], refs[n:2 * n]
        send_sems, recv_sems = refs[2 * n], refs[2 * n + 1]
        sends, recvs = copies_fn(in_refs, land_refs, send_sems, recv_sems, shapes)
        for cp in sends:
            cp.wait_send()
        for cp in recvs:
            cp.wait_recv()

    arrs = list(started["ins"]) + list(started["lands"])
    res = pl.pallas_call(
        body, name=name,
        out_shape=tuple(pltpu.HBM(a.shape, a.dtype) for a in arrs),
        in_specs=[HBM] * (2 * n) + [SEM, SEM] + [pl.BlockSpec(memory_space=pl.ANY)] * na,
        out_specs=(HBM,) * (2 * n), input_output_aliases={i: i for i in range(2 * n)},
        compiler_params=pltpu.CompilerParams(has_side_effects=EFFECT),
    )(*arrs, *started["sems"], *after)
    return res[:n], res[n:]


def _sum_pieces(gb, land, ids, f_prev, blk, nblk, name):
    _, a, b = gb.shape
    rows = _tile(a // 2, (512, 352, 256, ROWS_EW, 64, 32, 16))
    nrt = (a // 2) // rows

    def body(ids_ref, g_ref, l_ref, *rest):
        o_ref = rest[-1]
        acc = g_ref[0].astype(f32)
        for r in range(1, N_REL):
            acc = acc + l_ref[r].astype(f32)
        o_ref[...] = acc

    in_specs = [pl.BlockSpec((1, rows, b), lambda i, ids_ref: (ids_ref[0], ids_ref[1] * nrt + i, 0)),
                pl.BlockSpec((N_REL, rows, b), lambda i, ids_ref: (0, i, 0))]
    args = [ids, gb, land]
    aliases = {}
    if f_prev is not None:
        in_specs.append(pl.BlockSpec(memory_space=pl.ANY))
        args.append(f_prev)
        aliases = {3: 0}
    grid_spec = pltpu.PrefetchScalarGridSpec(
        num_scalar_prefetch=1, grid=(nrt,), in_specs=in_specs,
        out_specs=pl.BlockSpec((rows, b), lambda i, ids_ref: ((2 * blk + ids_ref[1]) * nrt + i, 0)))
    return pl.pallas_call(body, grid_spec=grid_spec, out_shape=_sds((nblk * a, b), f32),
                          input_output_aliases=aliases, compiler_params=_cp(), name=name)(*args)


def _sibling_fill_blocks(fs, nblks, name):
    n = len(fs)
    shapes = [f.shape for f in fs]

    def body(*refs):
        ins, outs = refs[:n], refs[n:2 * n]
        send_sems, recv_sems = refs[2 * n:]
        x, y, c = _coords()
        cps, waits = [], []
        k = 0
        for p in range(n):
            a = shapes[p][0] // nblks[p]
            for bi in range(nblks[p]):
                mine = pl.ds(pl.multiple_of(bi * a + c * (a // 2), 8), a // 2)
                theirs = pl.ds(pl.multiple_of(bi * a + (1 - c) * (a // 2), 8), a // 2)
                sems = dict(send_sem=send_sems.at[k], recv_sem=recv_sems.at[k], device_id=(x, y, 1 - c), device_id_type=MESH)
                cp = pltpu.make_async_remote_copy(src_ref=ins[p].at[mine], dst_ref=outs[p].at[mine], **sems)
                cp.start()
                cps.append(cp)
                waits.append(pltpu.make_async_remote_copy(src_ref=ins[p].at[theirs], dst_ref=outs[p].at[theirs], **sems))
                k += 1
        for wt_ in waits:
            wt_.wait_recv()
        for cp in cps:
            cp.wait_send()

    total = sum(nblks)
    return pl.pallas_call(
        body, in_specs=[ANY] * n, out_specs=[ANY] * n,
        out_shape=[_sds(f.shape, f.dtype) for f in fs],
        input_output_aliases={p: p for p in range(n)},
        scratch_shapes=[pltpu.SemaphoreType.DMA((total,)), pltpu.SemaphoreType.DMA((total,))],
        name=name)(*fs)


WEIGHTS = ["norm_mix", "norm_ffn", "norm_final", "ab_w_in", "gdn_conv_w", "gdn_a_log", "gdn_dt_bias", "gdn_norm",
           "hgrn_lower_bounds", "hgrn_norm", "ab_w_out", "c_w_in", "c_conv_w", "c_conv_b", "c_gate_a_w", "c_gate_a_b",
           "c_gate_x_w", "c_gate_x_b", "c_lambda", "c_w_out", "ffn_w_up", "ffn_conv_w", "ffn_conv_b", "ffn_w_down"]
BIG = {"ab_w_in": "col", "ab_w_out": "row", "c_w_in": "col", "c_gate_a_w": "gate", "c_gate_x_w": "gate",
       "c_w_out": "row", "ffn_w_up": "col", "ffn_w_down": "row"}
SMALL_SHARDED = ["gdn_conv_w", "c_conv_w", "c_conv_b", "c_gate_a_b", "c_gate_x_b", "c_lambda", "ffn_conv_w"]
SMALL = [n for n in WEIGHTS if n not in BIG]
FULL_SHAPES = {
    "norm_mix": (4, 1024), "norm_ffn": (4, 1024), "norm_final": (1024,), "ab_w_in": (2, 1024, 4104),
    "gdn_conv_w": (2, 4, 1536), "gdn_a_log": (2, 4), "gdn_dt_bias": (2, 4), "gdn_norm": (2, 128),
    "hgrn_lower_bounds": (2, 512), "hgrn_norm": (2, 128), "ab_w_out": (2, 1024, 1024), "c_w_in": (2, 1024, 2048),
    "c_conv_w": (2, 4, 1024), "c_conv_b": (2, 1024), "c_gate_a_w": (2, 4, 256, 256), "c_gate_a_b": (2, 1024),
    "c_gate_x_w": (2, 4, 256, 256), "c_gate_x_b": (2, 1024), "c_lambda": (2, 1024), "c_w_out": (2, 1024, 1024),
    "ffn_w_up": (4, 1024, 5632), "ffn_conv_w": (4, 3, 2816), "ffn_conv_b": (4, 2816), "ffn_w_down": (4, 2816, 1024)}


def _pack_rows(arrs, rows):
    parts, used = [], 0
    for a in arrs:
        r = _pack_tile_rows(a.size)
        parts.append(jnp.pad(a.reshape(-1), (0, r * 128 - a.size)).reshape(r, 128))
        used += r
    assert rows >= used and (rows - used) % 8 == 0, (rows, used)
    if rows > used:
        parts.append(jnp.zeros((rows - used, 128), f32))
    return jnp.concatenate(parts, axis=0)


def _pack_tile_rows(size):
    return -(-size // 1024) * 8


def _unpack_rows(pack, shapes):
    out, row = [], 0
    for s in shapes:
        size = 1
        for d in s:
            size *= d
        r = _pack_tile_rows(size)
        out.append(pack[row:row + r].reshape(-1)[:size].reshape(s))
        row += r
    return out


def kernel(x, norm_mix, norm_ffn, norm_final, ab_w_in, gdn_conv_w, gdn_a_log, gdn_dt_bias, gdn_norm, hgrn_lower_bounds, hgrn_norm, ab_w_out, c_w_in, c_conv_w, c_conv_b, c_gate_a_w, c_gate_a_b, c_gate_x_w, c_gate_x_b, c_lambda, c_w_out, ffn_w_up, ffn_conv_w, ffn_conv_b, ffn_w_down, loss_target, m_norm_mix, m_norm_ffn, m_norm_final, m_ab_w_in, m_gdn_conv_w, m_gdn_a_log, m_gdn_dt_bias, m_gdn_norm, m_hgrn_lower_bounds, m_hgrn_norm, m_ab_w_out, m_c_w_in, m_c_conv_w, m_c_conv_b, m_c_gate_a_w, m_c_gate_a_b, m_c_gate_x_w, m_c_gate_x_b, m_c_lambda, m_c_w_out, m_ffn_w_up, m_ffn_conv_w, m_ffn_conv_b, m_ffn_w_down, v_norm_mix, v_norm_ffn, v_norm_final, v_ab_w_in, v_gdn_conv_w, v_gdn_a_log, v_gdn_dt_bias, v_gdn_norm, v_hgrn_lower_bounds, v_hgrn_norm, v_ab_w_out, v_c_w_in, v_c_conv_w, v_c_conv_b, v_c_gate_a_w, v_c_gate_a_b, v_c_gate_x_w, v_c_gate_x_b, v_c_lambda, v_c_w_out, v_ffn_w_up, v_ffn_conv_w, v_ffn_conv_b, v_ffn_w_down):
    w = dict(zip(WEIGHTS, (norm_mix, norm_ffn, norm_final, ab_w_in, gdn_conv_w, gdn_a_log, gdn_dt_bias, gdn_norm, hgrn_lower_bounds, hgrn_norm, ab_w_out, c_w_in, c_conv_w, c_conv_b, c_gate_a_w, c_gate_a_b, c_gate_x_w, c_gate_x_b, c_lambda, c_w_out, ffn_w_up, ffn_conv_w, ffn_conv_b, ffn_w_down)))
    m = dict(zip(WEIGHTS, (m_norm_mix, m_norm_ffn, m_norm_final, m_ab_w_in, m_gdn_conv_w, m_gdn_a_log, m_gdn_dt_bias, m_gdn_norm, m_hgrn_lower_bounds, m_hgrn_norm, m_ab_w_out, m_c_w_in, m_c_conv_w, m_c_conv_b, m_c_gate_a_w, m_c_gate_a_b, m_c_gate_x_w, m_c_gate_x_b, m_c_lambda, m_c_w_out, m_ffn_w_up, m_ffn_conv_w, m_ffn_conv_b, m_ffn_w_down)))
    v = dict(zip(WEIGHTS, (v_norm_mix, v_norm_ffn, v_norm_final, v_ab_w_in, v_gdn_conv_w, v_gdn_a_log, v_gdn_dt_bias, v_gdn_norm, v_hgrn_lower_bounds, v_hgrn_norm, v_ab_w_out, v_c_w_in, v_c_conv_w, v_c_conv_b, v_c_gate_a_w, v_c_gate_a_b, v_c_gate_x_w, v_c_gate_x_b, v_c_lambda, v_c_w_out, v_ffn_w_up, v_ffn_conv_w, v_ffn_conv_b, v_ffn_w_down)))
    big = list(BIG)
    chip = 2 * lax.axis_index("x") + lax.axis_index("y")
    ids = jnp.stack([chip, lax.axis_index("c")]).astype(jnp.int32)

    def layer_parts(l):
        j = l // 2
        if l % 2 == 0:
            mix = [("ab_w_in", j), ("ab_w_out", j)]
        else:
            mix = [("c_w_in", j), ("c_gate_a_w", j), ("c_gate_x_w", j), ("c_w_out", j)]
        return mix, [("ffn_w_up", l), ("ffn_w_down", l)]

    def layer_shard(n, i):
        s = w[n][i]
        return s.reshape(-1, s.shape[-1]).astype(bf16)

    small_shard_shapes = [w[n].shape for n in SMALL_SHARDED]
    small_pack = _pack_rows([w[n] for n in SMALL_SHARDED], 160)
    mix0, ffn0 = layer_parts(0)
    gathered0 = _all_gather_chips([layer_shard(n, i) for n, i in mix0] + [small_pack], "all_gather_mixer0")
    gathers = {}

    def start_gather(key, after):
        shards = [layer_shard(n, i) for n, i in layer_parts(key[0])[0 if key[1] == "mix" else 1]]
        gathers[key] = _split_start(_gather_copies, shards, [_sds((N_CHIPS,) + s.shape, bf16) for s in shards],
                                    "gather_start_%d_%s" % key, after)
        return gathers[key]["token"]

    tok = ()
    for key in ((0, "ffn"), (1, "mix"), (1, "ffn")):
        tok = (start_gather(key, tok),)
    first_tokens = tok
    start_next = {(0, "ffn"): ((2, "mix"), (2, "ffn")), (1, "mix"): ((3, "mix"),), (1, "ffn"): ((3, "ffn"),)}
    fw = {}
    per_chip = [_unpack_rows(gathered0[-1][k], small_shard_shapes) for k in range(N_CHIPS)]
    for i, n in enumerate(SMALL_SHARDED):
        fw[n] = jnp.concatenate([per_chip[k][i] for k in range(N_CHIPS)], axis=-1)
    for n in SMALL:
        if n not in fw:
            fw[n] = w[n]
    wt = _layout_weights(fw)
    for n in big:
        wt[n] = [None] * FULL_SHAPES[n][0]

    def pre_layer(l, part, x_l):
        parts = layer_parts(l)[0 if part == "mix" else 1]
        if l == 0 and part == "mix":
            lands, deps = gathered0[:len(mix0)], first_tokens
        else:
            _, lands = _split_wait(_gather_copies, gathers[(l, part)], [x_l], "gather_wait_%d_%s" % (l, part))
            deps = tuple(lands[:1])
            for key in start_next.get((l, part), ()):
                deps = (start_gather(key, deps),)
            if (l, part) not in start_next:
                deps = ()
        for (n, i), slots in zip(parts, lands):
            wt[n][i] = _layer_full(n, slots)
        return deps

    scatters = []

    def post_grads(l, part, g):
        parts = layer_parts(l)[0 if part == "mix" else 1]
        slots = [_layer_slots(n, g[n][i]) for n, i in parts]
        st = _split_start(_scatter_copies, slots, [_sds((N_REL, s.shape[1] // 2, s.shape[2]), bf16) for s in slots],
                          "scatter_start_%d_%s" % (l, part))
        scatters.append((parts, st, "scatter_wait_%d_%s" % (l, part)))
        return (st["token"],)

    loss, dx, g = _local_step(x[0], loss_target[0], wt, pre_layer, post_grads)
    gf = _unlayout_grads(g)
    loss = lax.psum(loss[0, 0], ("x", "y", "c"))

    out_g, out_d, out_m, out_v = {}, {}, {}, {}
    f = {n: None for n in big}
    last = [n for n, _ in layer_parts(0)[0]]

    def finish(group, after):
        for parts, st, wait_name in group:
            gbs, lands = _split_wait(_scatter_copies, st, after, wait_name)
            for (n, i), gb, land in zip(parts, gbs, lands):
                f[n] = _sum_pieces(gb, land, ids, f[n], i, FULL_SHAPES[n][0], "rs_sum")

    def adamw(names, tag):
        filled = _sibling_fill_blocks([f[n] for n in names], [FULL_SHAPES[n][0] for n in names], "rs_sibling_fill_" + tag)
        for n, g_n in zip(names, filled):
            shp = w[n].shape
            if n in NATIVE_PERM:
                to_native = lambda t: t.transpose(NATIVE_PERM[n])
                res = _adamw_lead(to_native(w[n]), to_native(g_n.reshape(shp)), to_native(m[n]), to_native(v[n]), "adamw_" + n)
                res = [t.transpose(1, 2, 0) for t in res]
            else:
                as3d = lambda t: t.reshape((-1,) + shp[-2:])
                res = _adamw_big(as3d(w[n]), g_n, as3d(m[n]), as3d(v[n]), "adamw_" + n)
            out_g[n], out_d[n], out_m[n], out_v[n] = (t.reshape(shp) for t in res)

    share = _split_start(_share_copies, [_pack_rows([gf[n] for n in SMALL], 688)], [_sds((N_DEV, 688, 128), f32)],
                         "share_small_start", [dx])
    finish(scatters[:-1], [dx, share["token"]])
    adamw([n for n in big if n not in last], "a")
    finish(scatters[-1:], [out_v["ffn_w_up"]])
    adamw(last, "b")

    small_full_shapes = [FULL_SHAPES[n] for n in SMALL]
    (own_pack,), (all_packs,) = _split_wait(_share_copies, share, [out_v[last[0]]], "share_small_wait")
    me_arr = (4 * lax.axis_index("x") + 2 * lax.axis_index("y") + lax.axis_index("c")).astype(jnp.int32).reshape(1)
    small_sum = _sum_shared(own_pack, all_packs, me_arr, "sum_small")
    g_small = dict(zip(SMALL, _unpack_rows(small_sum, small_full_shapes)))
    for n in SMALL_SHARDED:
        width = w[n].shape[-1]
        g_small[n] = lax.dynamic_slice_in_dim(g_small[n], chip * width, width, axis=-1)

    def small2d(n, t):
        if n in NATIVE_PERM:
            t = t.transpose(NATIVE_PERM[n])
        return t.reshape(-1, t.shape[-1])

    sg, sd, sm, sv = _adamw_small([small2d(n, w[n]) for n in SMALL], [small2d(n, g_small[n]) for n in SMALL],
                                  [small2d(n, m[n]) for n in SMALL], [small2d(n, v[n]) for n in SMALL], "adamw_small")
    for i, n in enumerate(SMALL):
        for out, t in zip((out_g, out_d, out_m, out_v), (sg, sd, sm, sv)):
            if n in NATIVE_PERM:
                perm = NATIVE_PERM[n]
                shp_t = tuple(w[n].shape[p] for p in perm)
                out[n] = t[i].reshape(shp_t).transpose(tuple(perm.index(k) for k in range(len(perm))))
            else:
                out[n] = t[i].reshape(w[n].shape)
    return (loss, dx[None], *[out_g[n] for n in WEIGHTS], *[out_d[n] for n in WEIGHTS],
            *[out_m[n] for n in WEIGHTS], *[out_v[n] for n in WEIGHTS])
```
